```python
import math
import jax, jax.numpy as jnp
from jax import lax
import numpy as np

D_MODEL = 1024
BATCH = 8
SEQ = 8192
DEPTH = 1

N_META = 16
MIX_WIDTH = D_MODEL
ATTN_WIDTH = MIX_WIDTH // 2
CONV_WIDTH = MIX_WIDTH - ATTN_WIDTH
HEAD_DIM = 64
N_HEADS = ATTN_WIDTH // HEAD_DIM
N_KV_HEADS = 2
GROUP = N_HEADS // N_KV_HEADS
KV_WIDTH = N_KV_HEADS * HEAD_DIM
WINDOW = 128
BLOCK = 128
CONV_K = 31
N_CONV_GROUPS = CONV_WIDTH // HEAD_DIM
D_FF = int(math.ceil(8 * D_MODEL / 3 / 256) * 256)
IN_COLS = ATTN_WIDTH + 2 * KV_WIDTH + 2 * CONV_WIDTH
SPLITS = (ATTN_WIDTH,
          ATTN_WIDTH + KV_WIDTH,
          ATTN_WIDTH + 2 * KV_WIDTH,
          ATTN_WIDTH + 2 * KV_WIDTH + CONV_WIDTH)
NORM_EPS = 1e-5

kernel_name = "hymba_conformer_swa_sink_alibi_layer"


def rmsnorm(x, g):
    xf = x.astype(jnp.float32)
    y = xf * lax.rsqrt(jnp.mean(xf * xf, axis=-1, keepdims=True) + NORM_EPS)
    return (y * g.astype(jnp.float32)).astype(x.dtype)


def alibi_slopes():
    return jnp.power(2.0, -(8.0 / N_HEADS) * jnp.arange(1, N_HEADS + 1, dtype=jnp.float32))


def swa_sink_alibi_attention(q, k, v, sinks):
    B, L = q.shape[0], q.shape[1]
    lead = BLOCK - N_META
    P = L + lead
    nb = P // BLOCK
    pad = ((0, 0), (lead, 0), (0, 0), (0, 0))
    qb = jnp.pad(q, pad).reshape(B, nb, BLOCK, N_KV_HEADS, GROUP, HEAD_DIM)
    kb = jnp.pad(k, pad).reshape(B, nb, BLOCK, N_KV_HEADS, HEAD_DIM)
    vb = jnp.pad(v, pad).reshape(B, nb, BLOCK, N_KV_HEADS, HEAD_DIM)
    shift = ((0, 0), (1, 0), (0, 0), (0, 0), (0, 0))
    k_band = jnp.concatenate([jnp.pad(kb, shift)[:, :-1], kb], axis=2)
    v_band = jnp.concatenate([jnp.pad(vb, shift)[:, :-1], vb], axis=2)
    k_meta = jnp.broadcast_to(k[:, None, :N_META], (B, nb, N_META, N_KV_HEADS, HEAD_DIM))
    v_meta = jnp.broadcast_to(v[:, None, :N_META], (B, nb, N_META, N_KV_HEADS, HEAD_DIM))
    k_all = jnp.concatenate([k_meta, k_band], axis=2)
    v_all = jnp.concatenate([v_meta, v_band], axis=2)

    scale = 1.0 / math.sqrt(HEAD_DIM)
    s = jnp.einsum('bnqkgd,bnskd->bnkgqs', qb, k_all).astype(jnp.float32) * scale

    qi = jnp.arange(nb)[:, None] * BLOCK + jnp.arange(BLOCK)[None, :]
    kj = (jnp.arange(nb)[:, None] - 1) * BLOCK + jnp.arange(2 * BLOCK)[None, :]
    dist = qi[:, :, None] - kj[:, None, :]
    band_ok = (dist >= 0) & (dist < WINDOW) & (kj[:, None, :] >= BLOCK)
    meta_j = lead + jnp.arange(N_META)
    meta_ok = meta_j[None, None, :] <= qi[:, :, None]
    ok = jnp.concatenate([meta_ok, band_ok], axis=-1)
    dist_all = jnp.concatenate(
        [jnp.zeros((nb, BLOCK, N_META), jnp.float32), dist.astype(jnp.float32)], axis=-1)
    slopes = alibi_slopes().reshape(N_KV_HEADS, GROUP)
    bias = -slopes[None, :, :, None, None] * dist_all[:, None, None, :, :]
    s = jnp.where(ok[:, None, None], s + bias[None], -jnp.inf)

    sink = sinks.astype(jnp.float32).reshape(N_KV_HEADS, GROUP)[None, None, :, :, None, None]
    sink = jnp.broadcast_to(sink, s.shape[:-1] + (1,))
    p = jax.nn.softmax(jnp.concatenate([s, sink], axis=-1), axis=-1)[..., :-1]
    o = jnp.einsum('bnkgqs,bnskd->bnqkgd', p.astype(v.dtype), v_all)
    return o.reshape(B, P, N_HEADS * HEAD_DIM)[:, lead:]


def conformer_conv(a, gate, conv_w, conv_b, ln_g, ln_b):
    u = a * jax.nn.sigmoid(gate)
    y = lax.conv_general_dilated(
        u, conv_w[:, None, :].astype(u.dtype), window_strides=(1,),
        padding=((CONV_K - 1, 0),), dimension_numbers=('NWC', 'WIO', 'NWC'),
        feature_group_count=CONV_WIDTH) + conv_b
    yf = y.astype(jnp.float32)
    mu = jnp.mean(yf, axis=-1, keepdims=True)
    var = jnp.mean(jnp.square(yf - mu), axis=-1, keepdims=True)
    yn = (yf - mu) * lax.rsqrt(var + NORM_EPS) * ln_g.astype(jnp.float32) + ln_b.astype(jnp.float32)
    return jax.nn.silu(yn).astype(a.dtype)


def _fwd_setup_inputs(seed: int = 0) -> dict:
    key = jax.random.key(seed)
    ks = jax.random.split(key, 20)
    f32 = jnp.float32
    nrm = lambda k, shape, s: jax.random.normal(k, shape, f32) * s
    return {
        "x": nrm(ks[0], (BATCH, SEQ, D_MODEL), 1.0),
        "meta_tokens": nrm(ks[1], (N_META, D_MODEL), 1.0),
        "attn_norm_g": 1.0 + nrm(ks[2], (DEPTH, D_MODEL), 0.02),
        "w_in": nrm(ks[3], (DEPTH, D_MODEL, IN_COLS), D_MODEL ** -0.5),
        "attn_sinks": nrm(ks[4], (DEPTH, N_HEADS), 0.5),
        "conv_w": nrm(ks[5], (DEPTH, CONV_K, CONV_WIDTH), CONV_K ** -0.5),
        "conv_b": nrm(ks[6], (DEPTH, CONV_WIDTH), 0.02),
        "conv_ln_g": 1.0 + nrm(ks[7], (DEPTH, CONV_WIDTH), 0.02),
        "conv_ln_b": nrm(ks[8], (DEPTH, CONV_WIDTH), 0.02),
        "attn_out_g": 1.0 + nrm(ks[9], (DEPTH, ATTN_WIDTH), 0.02),
        "conv_out_g": 1.0 + nrm(ks[10], (DEPTH, CONV_WIDTH), 0.02),
        "w_out": nrm(ks[11], (DEPTH, MIX_WIDTH, D_MODEL), MIX_WIDTH ** -0.5),
        "ffn_norm_g": 1.0 + nrm(ks[12], (DEPTH, D_MODEL), 0.02),
        "w_gate": nrm(ks[13], (DEPTH, D_MODEL, D_FF), D_MODEL ** -0.5),
        "w_up": nrm(ks[14], (DEPTH, D_MODEL, D_FF), D_MODEL ** -0.5),
        "w_down": nrm(ks[15], (DEPTH, D_FF, D_MODEL), D_FF ** -0.5),
        "final_norm_g": 1.0 + nrm(ks[16], (D_MODEL,), 0.02),
    }


def _fwd_reference(x, meta_tokens, attn_norm_g, w_in, attn_sinks, conv_w, conv_b, conv_ln_g,
              conv_ln_b, attn_out_g, conv_out_g, w_out, ffn_norm_g, w_gate, w_up, w_down,
              final_norm_g):
    B = x.shape[0]
    meta = jnp.broadcast_to(meta_tokens[None].astype(x.dtype), (B, N_META, D_MODEL))
    h = jnp.concatenate([meta, x], axis=1)
    L = h.shape[1]
    for l in range(DEPTH):
        hn = rmsnorm(h, attn_norm_g[l])
        proj = hn @ w_in[l]
        q, k, v, ca, cg = jnp.split(proj, SPLITS, axis=-1)
        q = q.reshape(B, L, N_HEADS, HEAD_DIM)
        k = k.reshape(B, L, N_KV_HEADS, HEAD_DIM)
        v = v.reshape(B, L, N_KV_HEADS, HEAD_DIM)
        o_attn = swa_sink_alibi_attention(q, k, v, attn_sinks[l])
        o_conv = conformer_conv(ca, cg, conv_w[l], conv_b[l], conv_ln_g[l], conv_ln_b[l])
        mixed = jnp.concatenate([rmsnorm(o_attn, attn_out_g[l]),
                                 rmsnorm(o_conv, conv_out_g[l])], axis=-1)
        h = h + mixed @ w_out[l]
        hn = rmsnorm(h, ffn_norm_g[l])
        h = h + (jax.nn.silu(hn @ w_gate[l]) * (hn @ w_up[l])) @ w_down[l]
    return rmsnorm(h, final_norm_g)[:, N_META:]


import jax as _jax
import jax.numpy as _jnp

TWIN_FORMAT = 'train_step'
FWD_PARAMS = ['x', 'meta_tokens', 'attn_norm_g', 'w_in', 'attn_sinks', 'conv_w', 'conv_b', 'conv_ln_g', 'conv_ln_b', 'attn_out_g', 'conv_out_g', 'w_out', 'ffn_norm_g', 'w_gate', 'w_up', 'w_down', 'final_norm_g']
TWIN_WEIGHTS = ['meta_tokens', 'attn_norm_g', 'w_in', 'attn_sinks', 'conv_w', 'conv_b', 'conv_ln_g', 'conv_ln_b', 'attn_out_g', 'conv_out_g', 'w_out', 'ffn_norm_g', 'w_gate', 'w_up', 'w_down', 'final_norm_g']
TWIN_DIFF_INPUT = 'x'
TWIN_INPUTS = ['x', 'meta_tokens', 'attn_norm_g', 'w_in', 'attn_sinks', 'conv_w', 'conv_b', 'conv_ln_g', 'conv_ln_b', 'attn_out_g', 'conv_out_g', 'w_out', 'ffn_norm_g', 'w_gate', 'w_up', 'w_down', 'final_norm_g', 'loss_target', 'm_meta_tokens', 'm_attn_norm_g', 'm_w_in', 'm_attn_sinks', 'm_conv_w', 'm_conv_b', 'm_conv_ln_g', 'm_conv_ln_b', 'm_attn_out_g', 'm_conv_out_g', 'm_w_out', 'm_ffn_norm_g', 'm_w_gate', 'm_w_up', 'm_w_down', 'm_final_norm_g', 'v_meta_tokens', 'v_attn_norm_g', 'v_w_in', 'v_attn_sinks', 'v_conv_w', 'v_conv_b', 'v_conv_ln_g', 'v_conv_ln_b', 'v_attn_out_g', 'v_conv_out_g', 'v_w_out', 'v_ffn_norm_g', 'v_w_gate', 'v_w_up', 'v_w_down', 'v_final_norm_g']
TWIN_OUTPUTS = ['loss', 'grad_x', 'grad_meta_tokens', 'grad_attn_norm_g', 'grad_w_in', 'grad_attn_sinks', 'grad_conv_w', 'grad_conv_b', 'grad_conv_ln_g', 'grad_conv_ln_b', 'grad_attn_out_g', 'grad_conv_out_g', 'grad_w_out', 'grad_ffn_norm_g', 'grad_w_gate', 'grad_w_up', 'grad_w_down', 'grad_final_norm_g', 'delta_meta_tokens', 'delta_attn_norm_g', 'delta_w_in', 'delta_attn_sinks', 'delta_conv_w', 'delta_conv_b', 'delta_conv_ln_g', 'delta_conv_ln_b', 'delta_attn_out_g', 'delta_conv_out_g', 'delta_w_out', 'delta_ffn_norm_g', 'delta_w_gate', 'delta_w_up', 'delta_w_down', 'delta_final_norm_g', 'new_m_meta_tokens', 'new_m_attn_norm_g', 'new_m_w_in', 'new_m_attn_sinks', 'new_m_conv_w', 'new_m_conv_b', 'new_m_conv_ln_g', 'new_m_conv_ln_b', 'new_m_attn_out_g', 'new_m_conv_out_g', 'new_m_w_out', 'new_m_ffn_norm_g', 'new_m_w_gate', 'new_m_w_up', 'new_m_w_down', 'new_m_final_norm_g', 'new_v_meta_tokens', 'new_v_attn_norm_g', 'new_v_w_in', 'new_v_attn_sinks', 'new_v_conv_w', 'new_v_conv_b', 'new_v_conv_ln_g', 'new_v_conv_ln_b', 'new_v_attn_out_g', 'new_v_conv_out_g', 'new_v_w_out', 'new_v_ffn_norm_g', 'new_v_w_gate', 'new_v_w_up', 'new_v_w_down', 'new_v_final_norm_g']
TWIN_LEAF_KINDS = {'loss': 'loss', 'grad_x': 'grad_x', 'grad_meta_tokens': 'grad_w', 'grad_attn_norm_g': 'grad_w', 'grad_w_in': 'grad_w', 'grad_attn_sinks': 'grad_w', 'grad_conv_w': 'grad_w', 'grad_conv_b': 'grad_w', 'grad_conv_ln_g': 'grad_w', 'grad_conv_ln_b': 'grad_w', 'grad_attn_out_g': 'grad_w', 'grad_conv_out_g': 'grad_w', 'grad_w_out': 'grad_w', 'grad_ffn_norm_g': 'grad_w', 'grad_w_gate': 'grad_w', 'grad_w_up': 'grad_w', 'grad_w_down': 'grad_w', 'grad_final_norm_g': 'grad_w', 'delta_meta_tokens': 'delta_w', 'delta_attn_norm_g': 'delta_w', 'delta_w_in': 'delta_w', 'delta_attn_sinks': 'delta_w', 'delta_conv_w': 'delta_w', 'delta_conv_b': 'delta_w', 'delta_conv_ln_g': 'delta_w', 'delta_conv_ln_b': 'delta_w', 'delta_attn_out_g': 'delta_w', 'delta_conv_out_g': 'delta_w', 'delta_w_out': 'delta_w', 'delta_ffn_norm_g': 'delta_w', 'delta_w_gate': 'delta_w', 'delta_w_up': 'delta_w', 'delta_w_down': 'delta_w', 'delta_final_norm_g': 'delta_w', 'new_m_meta_tokens': 'new_m', 'new_m_attn_norm_g': 'new_m', 'new_m_w_in': 'new_m', 'new_m_attn_sinks': 'new_m', 'new_m_conv_w': 'new_m', 'new_m_conv_b': 'new_m', 'new_m_conv_ln_g': 'new_m', 'new_m_conv_ln_b': 'new_m', 'new_m_attn_out_g': 'new_m', 'new_m_conv_out_g': 'new_m', 'new_m_w_out': 'new_m', 'new_m_ffn_norm_g': 'new_m', 'new_m_w_gate': 'new_m', 'new_m_w_up': 'new_m', 'new_m_w_down': 'new_m', 'new_m_final_norm_g': 'new_m', 'new_v_meta_tokens': 'new_v', 'new_v_attn_norm_g': 'new_v', 'new_v_w_in': 'new_v', 'new_v_attn_sinks': 'new_v', 'new_v_conv_w': 'new_v', 'new_v_conv_b': 'new_v', 'new_v_conv_ln_g': 'new_v', 'new_v_conv_ln_b': 'new_v', 'new_v_attn_out_g': 'new_v', 'new_v_conv_out_g': 'new_v', 'new_v_w_out': 'new_v', 'new_v_ffn_norm_g': 'new_v', 'new_v_w_gate': 'new_v', 'new_v_w_up': 'new_v', 'new_v_w_down': 'new_v', 'new_v_final_norm_g': 'new_v'}


def _forward(args):
    return _fwd_reference(*[args[k] for k in FWD_PARAMS])


def _output_shape():
    def fwd():
        inp = _fwd_setup_inputs(0)
        return _fwd_reference(*[inp[k] for k in FWD_PARAMS])
    out = _jax.eval_shape(fwd)
    return out.shape, out.dtype

N_MICROBATCH = 1
ADAM_LR = 0.001
ADAM_B1 = 0.9
ADAM_B2 = 0.999
ADAM_EPS = 1e-08
ADAM_WD = 0.01
ADAM_STEP = 10
PER_EXAMPLE_BATCH_AXIS = {'x': 0, 'loss_target': 0}
SHARED_INPUTS = []
_WEIGHT_DTYPES = {'meta_tokens': _jnp.float32, 'attn_norm_g': _jnp.float32, 'w_in': _jnp.float32, 'attn_sinks': _jnp.float32, 'conv_w': _jnp.float32, 'conv_b': _jnp.float32, 'conv_ln_g': _jnp.float32, 'conv_ln_b': _jnp.float32, 'attn_out_g': _jnp.float32, 'conv_out_g': _jnp.float32, 'w_out': _jnp.float32, 'ffn_norm_g': _jnp.float32, 'w_gate': _jnp.float32, 'w_up': _jnp.float32, 'w_down': _jnp.float32, 'final_norm_g': _jnp.float32}
MOMENT_SCALE = {'meta_tokens': 5.106235e-02, 'attn_norm_g': 2.961152e-01, 'w_in': 2.247528e-01, 'attn_sinks': 3.564019e-02, 'conv_w': 1.979777e-01, 'conv_b': 5.780026e-01, 'conv_ln_g': 2.536014e-01, 'conv_ln_b': 3.136900e-01, 'attn_out_g': 2.510504e-01, 'conv_out_g': 1.925224e-01, 'w_out': 2.341336e-01, 'ffn_norm_g': 1.450775e-01, 'w_gate': 6.172155e-02, 'w_up': 6.036336e-02, 'w_down': 1.005793e-01, 'final_norm_g': 6.462338e+01}


def _to_microbatches(a, axis):
    t = _jnp.moveaxis(a, axis, 0)
    t = t.reshape((N_MICROBATCH, t.shape[0] // N_MICROBATCH) + t.shape[1:])
    return _jnp.moveaxis(t, 1, axis + 1)


def setup_inputs(seed: int = 0) -> dict:
    inp = _fwd_setup_inputs(seed)
    key = _jax.random.fold_in(_jax.random.key(seed), 7919)
    shape, _ = _output_shape()
    out = dict(inp)
    out["loss_target"] = _jax.random.normal(_jax.random.fold_in(key, 0), shape, _jnp.float32)
    for i, name in enumerate(TWIN_WEIGHTS):
        w = inp[name].astype(_jnp.float32)
        if MOMENT_SCALE is None:
            s = _jnp.sqrt(_jnp.mean(_jnp.square(w)) + 1e-30)
        else:
            s = MOMENT_SCALE[name]
        km, kv = _jax.random.split(_jax.random.fold_in(key, i + 1))
        out[name] = w
        out["m_" + name] = s * _jax.random.normal(km, w.shape, _jnp.float32)
        out["v_" + name] = (s * s) * _jax.random.uniform(kv, w.shape, _jnp.float32, 0.5, 1.5)
    if N_MICROBATCH > 1:
        for name, axis in PER_EXAMPLE_BATCH_AXIS.items():
            out[name] = _to_microbatches(out[name], axis)
    return {'x': out['x'], 'meta_tokens': out['meta_tokens'], 'attn_norm_g': out['attn_norm_g'], 'w_in': out['w_in'], 'attn_sinks': out['attn_sinks'], 'conv_w': out['conv_w'], 'conv_b': out['conv_b'], 'conv_ln_g': out['conv_ln_g'], 'conv_ln_b': out['conv_ln_b'], 'attn_out_g': out['attn_out_g'], 'conv_out_g': out['conv_out_g'], 'w_out': out['w_out'], 'ffn_norm_g': out['ffn_norm_g'], 'w_gate': out['w_gate'], 'w_up': out['w_up'], 'w_down': out['w_down'], 'final_norm_g': out['final_norm_g'], 'loss_target': out['loss_target'], 'm_meta_tokens': out['m_meta_tokens'], 'm_attn_norm_g': out['m_attn_norm_g'], 'm_w_in': out['m_w_in'], 'm_attn_sinks': out['m_attn_sinks'], 'm_conv_w': out['m_conv_w'], 'm_conv_b': out['m_conv_b'], 'm_conv_ln_g': out['m_conv_ln_g'], 'm_conv_ln_b': out['m_conv_ln_b'], 'm_attn_out_g': out['m_attn_out_g'], 'm_conv_out_g': out['m_conv_out_g'], 'm_w_out': out['m_w_out'], 'm_ffn_norm_g': out['m_ffn_norm_g'], 'm_w_gate': out['m_w_gate'], 'm_w_up': out['m_w_up'], 'm_w_down': out['m_w_down'], 'm_final_norm_g': out['m_final_norm_g'], 'v_meta_tokens': out['v_meta_tokens'], 'v_attn_norm_g': out['v_attn_norm_g'], 'v_w_in': out['v_w_in'], 'v_attn_sinks': out['v_attn_sinks'], 'v_conv_w': out['v_conv_w'], 'v_conv_b': out['v_conv_b'], 'v_conv_ln_g': out['v_conv_ln_g'], 'v_conv_ln_b': out['v_conv_ln_b'], 'v_attn_out_g': out['v_attn_out_g'], 'v_conv_out_g': out['v_conv_out_g'], 'v_w_out': out['v_w_out'], 'v_ffn_norm_g': out['v_ffn_norm_g'], 'v_w_gate': out['v_w_gate'], 'v_w_up': out['v_w_up'], 'v_w_down': out['v_w_down'], 'v_final_norm_g': out['v_final_norm_g']}


def _loss(weights, diff, rest, loss_target):
    with _jax.named_scope("forward"):
        args = {**rest, TWIN_DIFF_INPUT: diff, **{k: w.astype(_WEIGHT_DTYPES[k]) for k, w in weights.items()}}
        y = _forward(args)
    with _jax.named_scope("loss_head"):
        err = _jnp.square(y.astype(_jnp.float32) - loss_target)
        return 0.5 * _jnp.sum(_jnp.mean(err, axis=-1)) if err.ndim else 0.5 * err


def _adamw(w, g, m, v):
    m = ADAM_B1 * m + (1.0 - ADAM_B1) * g
    v = ADAM_B2 * v + (1.0 - ADAM_B2) * _jnp.square(g)
    m_hat = m / (1.0 - ADAM_B1 ** ADAM_STEP)
    v_hat = v / (1.0 - ADAM_B2 ** ADAM_STEP)
    delta = -ADAM_LR * (m_hat / (_jnp.sqrt(v_hat) + ADAM_EPS) + ADAM_WD * w)
    return delta, m, v


def reference(x, meta_tokens, attn_norm_g, w_in, attn_sinks, conv_w, conv_b, conv_ln_g, conv_ln_b, attn_out_g, conv_out_g, w_out, ffn_norm_g, w_gate, w_up, w_down, final_norm_g, loss_target, m_meta_tokens, m_attn_norm_g, m_w_in, m_attn_sinks, m_conv_w, m_conv_b, m_conv_ln_g, m_conv_ln_b, m_attn_out_g, m_conv_out_g, m_w_out, m_ffn_norm_g, m_w_gate, m_w_up, m_w_down, m_final_norm_g, v_meta_tokens, v_attn_norm_g, v_w_in, v_attn_sinks, v_conv_w, v_conv_b, v_conv_ln_g, v_conv_ln_b, v_attn_out_g, v_conv_out_g, v_w_out, v_ffn_norm_g, v_w_gate, v_w_up, v_w_down, v_final_norm_g):
    given = dict(x=x, meta_tokens=meta_tokens, attn_norm_g=attn_norm_g, w_in=w_in, attn_sinks=attn_sinks, conv_w=conv_w, conv_b=conv_b, conv_ln_g=conv_ln_g, conv_ln_b=conv_ln_b, attn_out_g=attn_out_g, conv_out_g=conv_out_g, w_out=w_out, ffn_norm_g=ffn_norm_g, w_gate=w_gate, w_up=w_up, w_down=w_down, final_norm_g=final_norm_g, loss_target=loss_target, m_meta_tokens=m_meta_tokens, m_attn_norm_g=m_attn_norm_g, m_w_in=m_w_in, m_attn_sinks=m_attn_sinks, m_conv_w=m_conv_w, m_conv_b=m_conv_b, m_conv_ln_g=m_conv_ln_g, m_conv_ln_b=m_conv_ln_b, m_attn_out_g=m_attn_out_g, m_conv_out_g=m_conv_out_g, m_w_out=m_w_out, m_ffn_norm_g=m_ffn_norm_g, m_w_gate=m_w_gate, m_w_up=m_w_up, m_w_down=m_w_down, m_final_norm_g=m_final_norm_g, v_meta_tokens=v_meta_tokens, v_attn_norm_g=v_attn_norm_g, v_w_in=v_w_in, v_attn_sinks=v_attn_sinks, v_conv_w=v_conv_w, v_conv_b=v_conv_b, v_conv_ln_g=v_conv_ln_g, v_conv_ln_b=v_conv_ln_b, v_attn_out_g=v_attn_out_g, v_conv_out_g=v_conv_out_g, v_w_out=v_w_out, v_ffn_norm_g=v_ffn_norm_g, v_w_gate=v_w_gate, v_w_up=v_w_up, v_w_down=v_w_down, v_final_norm_g=v_final_norm_g)
    weights = {n: given[n] for n in TWIN_WEIGHTS}
    shared = {n: given[n] for n in SHARED_INPUTS}
    per_example = {n: given[n] for n in ['x']}
    grad_fn = _jax.value_and_grad(_loss, argnums=(0, 1))

    def one_microbatch(ex, loss_target):
        ex = dict(ex)
        diff = ex.pop(TWIN_DIFF_INPUT)
        return grad_fn(weights, diff, {**shared, **ex}, loss_target)

    if N_MICROBATCH == 1:
        loss, (grad_w, grad_x) = one_microbatch(per_example, given["loss_target"])
    else:
        def body(carry, xs):
            loss_sum, grad_sum = carry
            l_k, (gw_k, gx_k) = one_microbatch(xs[0], xs[1])
            with _jax.named_scope("update"):
                return (loss_sum + l_k, _jax.tree.map(_jnp.add, grad_sum, gw_k)), gx_k

        init = (_jnp.zeros((), _jnp.float32), _jax.tree.map(_jnp.zeros_like, weights))
        (loss, grad_w), grad_x = _jax.lax.scan(body, init, (per_example, given["loss_target"]))
    with _jax.named_scope("update"):
        delta_w, new_m, new_v = {}, {}, {}
        for n in TWIN_WEIGHTS:
            delta_w[n], new_m[n], new_v[n] = _adamw(weights[n], grad_w[n], given["m_" + n], given["v_" + n])
    return (loss, grad_x, *[grad_w[n] for n in TWIN_WEIGHTS], *[delta_w[n] for n in TWIN_WEIGHTS],
            *[new_m[n] for n in TWIN_WEIGHTS], *[new_v[n] for n in TWIN_WEIGHTS])
```

```python
import functools
import math

import jax
import jax.numpy as jnp
from jax import lax
from jax.experimental import pallas as pl
from jax.experimental.pallas import tpu as pltpu

F32, BF16 = jnp.float32, jnp.bfloat16
MESH = pl.DeviceIdType.MESH

D_MODEL = 1024
N_META = 16
BLOCK = 128
LEAD = BLOCK - N_META
HEAD_DIM = 64
N_HEADS = 8
GROUP = 4
ATTN_W = 512
KV_W = 128
CONV_W = 512
CONV_K = 31
HALO = 32
D_FF = 2816
FF_CHUNK = D_FF // 2
FF_SUB = [slice(s, min(s + 256, FF_CHUNK)) for s in range(0, FF_CHUNK, 256)]
N_DEV = 8
EPS = 1e-5
NEG = -1e30
TM = 640
CONV_ROWS = 64
VMEM_LIMIT = 56 * 1024 * 1024

ADAM_LR, ADAM_B1, ADAM_B2, ADAM_EPS, ADAM_WD, ADAM_STEP = 0.001, 0.9, 0.999, 1e-08, 0.01, 10

NT = (((1,), (1,)), ((), ()))
NN = (((1,), (0,)), ((), ()))
TN = (((0,), (0,)), ((), ()))


def _dot(a, b, dims):
    return lax.dot_general(a, b, dims, preferred_element_type=F32)


def _sigmoid(x):
    return 1.0 / (1.0 + jnp.exp(-x))


def _pcall(body, *, name, out_shape, grid=None, in_specs=None, out_specs=None, scratch_shapes=(),
           semantics=None, **kw):
    params = dict(vmem_limit_bytes=VMEM_LIMIT)
    if semantics is not None:
        params["dimension_semantics"] = semantics
    extra = {}
    if grid is not None:
        extra["grid"] = grid
    if in_specs is not None:
        extra["in_specs"] = in_specs
    if out_specs is not None:
        extra["out_specs"] = out_specs
    return pl.pallas_call(body, name=name, out_shape=out_shape, scratch_shapes=list(scratch_shapes),
                          compiler_params=pltpu.CompilerParams(**params), **extra, **kw)


def _rows(tm, cols, off=0):
    return pl.BlockSpec((tm, cols), lambda i, *_: (i + off, 0))


def _full(shape):
    nd = len(shape)
    return pl.BlockSpec(shape, lambda *_: (0,) * nd)


def _rms_stats(x):
    return lax.rsqrt(jnp.mean(x * x, axis=-1, keepdims=True) + EPS)


def _rms_bwd(dy, x, r, g):
    t = dy * g
    dx = r * (t - x * (r * r) * jnp.mean(t * x, axis=-1, keepdims=True))
    dg = jnp.sum(dy * x * r, axis=0, keepdims=True)
    return dx, dg


def _inproj_fwd(h0, g1, w_in_t):
    R = h0.shape[0]

    def body(h_ref, g_ref, w_ref, q_ref, kv_ref, ca_ref, cg_ref):
        h = h_ref[...]
        hn = (h * _rms_stats(h) * g_ref[...]).astype(BF16)
        q_ref[...] = _dot(hn, w_ref[0:512, :], NT).astype(BF16)
        kv_ref[...] = _dot(hn, w_ref[512:768, :], NT).astype(BF16)
        ca_ref[...] = _dot(hn, w_ref[768:1280, :], NT)
        cg_ref[...] = _dot(hn, w_ref[1280:1792, :], NT)

    return _pcall(
        body, name="inproj_fwd", grid=(R // TM,),
        in_specs=[_rows(TM, D_MODEL), _full((1, D_MODEL)), _full((1792, D_MODEL))],
        out_specs=[_rows(TM, 512), _rows(TM, 256), _rows(TM, 512), _rows(TM, 512)],
        out_shape=[jax.ShapeDtypeStruct((R, 512), BF16), jax.ShapeDtypeStruct((R, 256), BF16),
                   jax.ShapeDtypeStruct((R, 512), F32), jax.ShapeDtypeStruct((R, 512), F32)],
        semantics=("parallel",),
    )(h0, g1, w_in_t)


def _attn_masks(b):
    ii = lax.broadcasted_iota(jnp.int32, (BLOCK, BLOCK), 0)
    jj = lax.broadcasted_iota(jnp.int32, (BLOCK, BLOCK), 1)
    sel = jj <= ii
    dist = jnp.where(sel, ii - jj, ii - jj + BLOCK).astype(F32)
    valid = b >= jnp.where(sel, 1, 2)
    mi = lax.broadcasted_iota(jnp.int32, (BLOCK, N_META), 0)
    mj = lax.broadcasted_iota(jnp.int32, (BLOCK, N_META), 1)
    valid_m = (mj + LEAD) <= (mi + b * BLOCK)
    return sel, dist, valid, valid_m


def _attn_scores(q, kc, kp, km, sel, dist, valid, valid_m, slope):
    scale = 1.0 / math.sqrt(HEAD_DIM)
    s_b = jnp.where(sel, _dot(q, kc, NT), _dot(q, kp, NT)) * scale
    s_b = jnp.where(valid, s_b - slope * dist, NEG)
    s_m = jnp.where(valid_m, _dot(q, km, NT) * scale, NEG)
    return s_b, s_m


def _attn_fwd(q, kv, sinks):
    R = q.shape[0]
    nb = R // BLOCK

    def body(sink_ref, q_ref, kvc_ref, kvp_ref, kvm_ref, o_ref, lse_ref):
        b = pl.program_id(0)
        sel, dist, valid, valid_m = _attn_masks(b)
        outs = []
        for h in range(N_HEADS):
            g = h // GROUP
            ks, vs = slice(HEAD_DIM * g, HEAD_DIM * (g + 1)), slice(KV_W + HEAD_DIM * g, KV_W + HEAD_DIM * (g + 1))
            qh = q_ref[:, HEAD_DIM * h:HEAD_DIM * (h + 1)]
            s_b, s_m = _attn_scores(qh, kvc_ref[:, ks], kvp_ref[:, ks], kvm_ref[LEAD:BLOCK, ks],
                                    sel, dist, valid, valid_m, 2.0 ** -(h + 1))
            sink = sink_ref[0, h]
            m = jnp.maximum(jnp.maximum(jnp.max(s_b, axis=-1, keepdims=True),
                                        jnp.max(s_m, axis=-1, keepdims=True)), sink)
            p_b = jnp.exp(s_b - m)
            p_m = jnp.exp(s_m - m)
            l = jnp.sum(p_b, axis=-1, keepdims=True) + jnp.sum(p_m, axis=-1, keepdims=True) + jnp.exp(sink - m)
            p_c = jnp.where(sel, p_b, 0.0).astype(BF16)
            p_p = jnp.where(sel, 0.0, p_b).astype(BF16)
            o = (_dot(p_c, kvc_ref[:, vs], NN) + _dot(p_p, kvp_ref[:, vs], NN)
                 + _dot(p_m.astype(BF16), kvm_ref[LEAD:BLOCK, vs], NN))
            outs.append(o / l)
            lse_ref[:, h:h + 1] = m + jnp.log(l)
        o_ref[...] = jnp.concatenate(outs, axis=1)

    return _pcall(
        body, name="attn_fwd", grid=(nb,),
        in_specs=[pl.BlockSpec(memory_space=pltpu.SMEM),
                  _rows(BLOCK, 512), _rows(BLOCK, 256),
                  pl.BlockSpec((BLOCK, 256), lambda b: (jnp.maximum(b - 1, 0), 0)),
                  _full((BLOCK, 256))],
        out_specs=[_rows(BLOCK, 512), _rows(BLOCK, N_HEADS)],
        out_shape=[jax.ShapeDtypeStruct((R, 512), F32), jax.ShapeDtypeStruct((R, N_HEADS), F32)],
        semantics=("parallel",),
    )(sinks, q, kv, kv, kv)


def _ln_silu(y, lg, lb):
    mu = jnp.mean(y, axis=-1, keepdims=True)
    xc = y - mu
    rstd = lax.rsqrt(jnp.mean(xc * xc, axis=-1, keepdims=True) + EPS)
    xhat = xc * rstd
    yn = xhat * lg + lb
    return yn, xhat, rstd


def _conv_fwd(ca, cg, conv_w, conv_b, ln_g, ln_b):
    R = ca.shape[0]
    nt = R // TM
    hpt = TM // HALO

    def body(ca_ref, cg_ref, cah_ref, cgh_ref, w_ref, b_ref, lg_ref, lb_ref, oc_ref, y_ref, u_s):
        i = pl.program_id(0)
        u_s[HALO:HALO + TM, :] = ca_ref[...] * _sigmoid(cg_ref[...])
        u_s[0:HALO, :] = jnp.where(i > 0, cah_ref[...] * _sigmoid(cgh_ref[...]), 0.0)
        for rc in range(TM // CONV_ROWS):
            base = rc * CONV_ROWS + HALO - (CONV_K - 1)
            acc = jnp.zeros((CONV_ROWS, CONV_W), F32) + b_ref[...]
            for k in range(CONV_K):
                acc = acc + u_s[pl.ds(base + k, CONV_ROWS), :] * w_ref[k:k + 1, :]
            rows = slice(rc * CONV_ROWS, (rc + 1) * CONV_ROWS)
            y_ref[rows, :] = acc
            yn, _, _ = _ln_silu(acc, lg_ref[...], lb_ref[...])
            oc_ref[rows, :] = yn * _sigmoid(yn)

    prev_halo = pl.BlockSpec((HALO, CONV_W), lambda i: (jnp.maximum(i * hpt - 1, 0), 0))
    return _pcall(
        body, name="conv_fwd", grid=(nt,),
        in_specs=[_rows(TM, CONV_W), _rows(TM, CONV_W), prev_halo, prev_halo,
                  _full((CONV_K, CONV_W)), _full((1, CONV_W)), _full((1, CONV_W)), _full((1, CONV_W))],
        out_specs=[_rows(TM, CONV_W), _rows(TM, CONV_W)],
        out_shape=[jax.ShapeDtypeStruct((R, CONV_W), F32), jax.ShapeDtypeStruct((R, CONV_W), F32)],
        scratch_shapes=[pltpu.VMEM((HALO + TM, CONV_W), F32)],
        semantics=("parallel",),
    )(ca, cg, ca, cg, conv_w, conv_b, ln_g, ln_b)


def _outproj_fwd(h0, o_attn, o_conv, ga, gc, w_out):
    R = h0.shape[0]

    def body(h_ref, oa_ref, oc_ref, ga_ref, gc_ref, w_ref, h1_ref):
        oa, oc = oa_ref[...], oc_ref[...]
        ma = (oa * _rms_stats(oa) * ga_ref[...]).astype(BF16)
        mc = (oc * _rms_stats(oc) * gc_ref[...]).astype(BF16)
        h1_ref[...] = h_ref[...] + _dot(ma, w_ref[0:512, :], NN) + _dot(mc, w_ref[512:1024, :], NN)

    return _pcall(
        body, name="outproj_fwd", grid=(R // TM,),
        in_specs=[_rows(TM, D_MODEL), _rows(TM, 512), _rows(TM, 512), _full((1, 512)), _full((1, 512)),
                  _full((D_MODEL, D_MODEL))],
        out_specs=_rows(TM, D_MODEL),
        out_shape=jax.ShapeDtypeStruct((R, D_MODEL), F32),
        semantics=("parallel",),
    )(h0, o_attn, o_conv, ga, gc, w_out)


def _target_copy(tgt_hbm, tgt_s, sem, i, first):
    if first:
        return pltpu.make_async_copy(tgt_hbm.at[pl.ds(0, TM - BLOCK)], tgt_s.at[pl.ds(BLOCK, TM - BLOCK)], sem)
    return pltpu.make_async_copy(tgt_hbm.at[pl.ds(i * TM - BLOCK, TM)], tgt_s, sem)


def _ffn_fwd(h1, g2, wg_t, wu_t, wd, gf, target):
    R = h1.shape[0]
    nt, nj = R // TM, D_FF // FF_CHUNK

    def body(h1_ref, g2_ref, wg_ref, wu_ref, wd_ref, gf_ref, tgt_hbm,
             gate_ref, up_ref, dh2_ref, loss_ref, dgf_ref, hn_s, acc_s, tgt_s, sem):
        i, j = pl.program_id(0), pl.program_id(1)

        @pl.when((i == 0) & (j == 0))
        def _():
            loss_ref[...] = jnp.zeros_like(loss_ref)
            dgf_ref[...] = jnp.zeros_like(dgf_ref)

        @pl.when(j == 0)
        def _():
            h1 = h1_ref[...]
            hn_s[...] = (h1 * _rms_stats(h1) * g2_ref[...]).astype(BF16)

            @pl.when(i == 0)
            def _():
                tgt_s[0:BLOCK, :] = jnp.zeros((BLOCK, D_MODEL), F32)
                _target_copy(tgt_hbm, tgt_s, sem, i, True).start()

            @pl.when(i > 0)
            def _():
                _target_copy(tgt_hbm, tgt_s, sem, i, False).start()

        hn = hn_s[...]
        part = None
        for cs in FF_SUB:
            gate = _dot(hn, wg_ref[cs, :], NT)
            up = _dot(hn, wu_ref[cs, :], NT)
            gate_ref[:, cs] = gate.astype(BF16)
            up_ref[:, cs] = up.astype(BF16)
            act = (gate * _sigmoid(gate) * up).astype(BF16)
            d = _dot(act, wd_ref[cs, :], NN)
            part = d if part is None else part + d

        @pl.when(j == 0)
        def _():
            acc_s[...] = part

        @pl.when(j == nj - 1)
        def _():
            @pl.when(i == 0)
            def _():
                _target_copy(tgt_hbm, tgt_s, sem, i, True).wait()

            @pl.when(i > 0)
            def _():
                _target_copy(tgt_hbm, tgt_s, sem, i, False).wait()

            h2 = h1_ref[...] + acc_s[...] + part
            rf = _rms_stats(h2)
            gf = gf_ref[...]
            row = lax.broadcasted_iota(jnp.int32, (TM, 1), 0) + i * TM
            err = jnp.where(row >= BLOCK, h2 * rf * gf - tgt_s[...], 0.0)
            dy = err * (1.0 / D_MODEL)
            dh2, dgf = _rms_bwd(dy, h2, rf, gf)
            dh2_ref[...] = dh2
            loss_ref[...] += (0.5 / D_MODEL) * jnp.sum(err * err)
            dgf_ref[...] += dgf

    wspec = pl.BlockSpec((FF_CHUNK, D_MODEL), lambda i, j: (j, 0))
    aspec = pl.BlockSpec((TM, FF_CHUNK), lambda i, j: (i, j))
    return _pcall(
        body, name="ffn_fwd", grid=(nt, nj),
        in_specs=[_rows(TM, D_MODEL), _full((1, D_MODEL)), wspec, wspec, wspec, _full((1, D_MODEL)),
                  pl.BlockSpec(memory_space=pl.ANY)],
        out_specs=[aspec, aspec, _rows(TM, D_MODEL),
                   _full((8, 128)), _full((1, D_MODEL))],
        out_shape=[jax.ShapeDtypeStruct((R, D_FF), BF16), jax.ShapeDtypeStruct((R, D_FF), BF16),
                   jax.ShapeDtypeStruct((R, D_MODEL), F32),
                   jax.ShapeDtypeStruct((8, 128), F32), jax.ShapeDtypeStruct((1, D_MODEL), F32)],
        scratch_shapes=[pltpu.VMEM((TM, D_MODEL), BF16), pltpu.VMEM((TM, D_MODEL), F32),
                        pltpu.VMEM((TM, D_MODEL), F32), pltpu.SemaphoreType.DMA],
        semantics=("arbitrary", "arbitrary"),
    )(h1, g2, wg_t, wu_t, wd, gf, target)


def _ffn_bwd(dh2, h1, g2, gate, up, wg_t, wu_t, wd):
    R = h1.shape[0]
    nt, nj = R // TM, D_FF // FF_CHUNK

    wspec = pl.BlockSpec((FF_CHUNK, D_MODEL), lambda i, j: (j, 0))
    aspec = pl.BlockSpec((TM, FF_CHUNK), lambda i, j: (i, j))
    act_shape = jax.ShapeDtypeStruct((R, D_FF), BF16)

    def act_body(dh2_ref, gate_ref, up_ref, wd_ref, dgate_ref, dup_ref, act_ref, dhb_s):
        @pl.when(pl.program_id(1) == 0)
        def _():
            dhb_s[...] = dh2_ref[...].astype(BF16)

        dhb = dhb_s[...]
        for cs in FF_SUB:
            dact = _dot(dhb, wd_ref[cs, :], NT)
            gate = gate_ref[:, cs].astype(F32)
            up = up_ref[:, cs].astype(F32)
            sig = _sigmoid(gate)
            silu = gate * sig
            dgate_ref[:, cs] = (dact * up * (sig * (1.0 + gate * (1.0 - sig)))).astype(BF16)
            dup_ref[:, cs] = (dact * silu).astype(BF16)
            act_ref[:, cs] = (silu * up).astype(BF16)

    dgate, dup, act = _pcall(
        act_body, name="ffn_bwd_act", grid=(nt, nj),
        in_specs=[_rows(TM, D_MODEL), aspec, aspec, wspec],
        out_specs=[aspec, aspec, aspec], out_shape=[act_shape, act_shape, act_shape],
        scratch_shapes=[pltpu.VMEM((TM, D_MODEL), BF16)],
        semantics=("parallel", "arbitrary"),
    )(dh2, gate, up, wd)

    def in_body(dh2_ref, h1_ref, g2_ref, dgate_ref, dup_ref, wg_ref, wu_ref, hn_ref, dh1_ref, dg2_ref, acc_s):
        i, j = pl.program_id(0), pl.program_id(1)

        @pl.when((i == 0) & (j == 0))
        def _():
            dg2_ref[...] = jnp.zeros_like(dg2_ref)

        part = None
        for cs in FF_SUB:
            d = _dot(dgate_ref[:, cs], wg_ref[cs, :], NN) + _dot(dup_ref[:, cs], wu_ref[cs, :], NN)
            part = d if part is None else part + d

        @pl.when(j == 0)
        def _():
            acc_s[...] = part

        @pl.when(j == nj - 1)
        def _():
            h1 = h1_ref[...]
            r = _rms_stats(h1)
            g2 = g2_ref[...]
            hn_ref[...] = (h1 * r * g2).astype(BF16)
            dx, dg = _rms_bwd(acc_s[...] + part, h1, r, g2)
            dh1_ref[...] = dh2_ref[...] + dx
            dg2_ref[...] += dg

    hn2, dh1, dg2 = _pcall(
        in_body, name="ffn_bwd_in", grid=(nt, nj),
        in_specs=[_rows(TM, D_MODEL), _rows(TM, D_MODEL), _full((1, D_MODEL)), aspec, aspec, wspec, wspec],
        out_specs=[_rows(TM, D_MODEL), _rows(TM, D_MODEL), _full((1, D_MODEL))],
        out_shape=[jax.ShapeDtypeStruct((R, D_MODEL), BF16), jax.ShapeDtypeStruct((R, D_MODEL), F32),
                   jax.ShapeDtypeStruct((1, D_MODEL), F32)],
        scratch_shapes=[pltpu.VMEM((TM, D_MODEL), F32)],
        semantics=("arbitrary", "arbitrary"),
    )(dh2, h1, g2, dgate, dup, wg_t, wu_t)
    return dgate, dup, act, hn2, dh1, dg2


def _wgrad(a, b, tm, name):
    K, M = a.shape
    N = b.shape[1]

    def body(a_ref, b_ref, o_ref):
        @pl.when(pl.program_id(1) == 0)
        def _():
            o_ref[...] = jnp.zeros_like(o_ref)

        o_ref[...] += _dot(a_ref[...], b_ref[...].astype(BF16), TN)

    return _pcall(
        body, name=name, grid=(M // tm, K // TM),
        in_specs=[pl.BlockSpec((TM, tm), lambda m, k: (k, m)), pl.BlockSpec((TM, N), lambda m, k: (k, 0))],
        out_specs=pl.BlockSpec((tm, N), lambda m, k: (m, 0)),
        out_shape=jax.ShapeDtypeStruct((M, N), F32),
        semantics=("parallel", "arbitrary"),
    )(a, b)


def _outproj_bwd(dh1, o_attn, o_conv, ga, gc, w_out):
    R = dh1.shape[0]

    def body(dh1_ref, oa_ref, oc_ref, ga_ref, gc_ref, w_ref, doa_ref, doc_ref, mixed_ref, dga_ref, dgc_ref):
        @pl.when(pl.program_id(0) == 0)
        def _():
            dga_ref[...] = jnp.zeros_like(dga_ref)
            dgc_ref[...] = jnp.zeros_like(dgc_ref)

        dm = _dot(dh1_ref[...].astype(BF16), w_ref[...], NT)
        oa, oc = oa_ref[...], oc_ref[...]
        ra, rc = _rms_stats(oa), _rms_stats(oc)
        mixed_ref[:, 0:512] = (oa * ra * ga_ref[...]).astype(BF16)
        mixed_ref[:, 512:1024] = (oc * rc * gc_ref[...]).astype(BF16)
        doa, dga = _rms_bwd(dm[:, 0:512], oa, ra, ga_ref[...])
        doc, dgc = _rms_bwd(dm[:, 512:1024], oc, rc, gc_ref[...])
        doa_ref[...] = doa
        doc_ref[...] = doc
        dga_ref[...] += dga
        dgc_ref[...] += dgc

    return _pcall(
        body, name="outproj_bwd", grid=(R // TM,),
        in_specs=[_rows(TM, D_MODEL), _rows(TM, 512), _rows(TM, 512), _full((1, 512)), _full((1, 512)),
                  _full((D_MODEL, D_MODEL))],
        out_specs=[_rows(TM, 512), _rows(TM, 512), _rows(TM, D_MODEL), _full((1, 512)), _full((1, 512))],
        out_shape=[jax.ShapeDtypeStruct((R, 512), F32), jax.ShapeDtypeStruct((R, 512), F32),
                   jax.ShapeDtypeStruct((R, D_MODEL), BF16),
                   jax.ShapeDtypeStruct((1, 512), F32), jax.ShapeDtypeStruct((1, 512), F32)],
        semantics=("arbitrary",),
    )(dh1, o_attn, o_conv, ga, gc, w_out)


def _conv_bwd(do_conv, y, ca, cg, conv_w, ln_g, ln_b):
    R = ca.shape[0]
    nt = R // TM
    hpt = TM // HALO

    def body(do_ref, doh_ref, y_ref, yh_ref, ca_ref, cg_ref, cah_ref, cgh_ref, w_ref, lg_ref, lb_ref,
             dca_ref, dcg_ref, dw_ref, db_ref, dlg_ref, dlb_ref, u_s, dy_s):
        i = pl.program_id(0)

        @pl.when(i == 0)
        def _():
            dw_ref[...] = jnp.zeros_like(dw_ref)
            db_ref[...] = jnp.zeros_like(db_ref)
            dlg_ref[...] = jnp.zeros_like(dlg_ref)
            dlb_ref[...] = jnp.zeros_like(dlb_ref)

        lg, lb = lg_ref[...], lb_ref[...]

        def ln_bwd(yv, dov):
            yn, xhat, rstd = _ln_silu(yv, lg, lb)
            sig = _sigmoid(yn)
            dyn = dov * (sig * (1.0 + yn * (1.0 - sig)))
            dxh = dyn * lg
            dyv = rstd * (dxh - jnp.mean(dxh, axis=-1, keepdims=True)
                          - xhat * jnp.mean(dxh * xhat, axis=-1, keepdims=True))
            return dyv, dyn, xhat

        dyv, dyn, xhat = ln_bwd(y_ref[...], do_ref[...])
        dy_s[0:TM, :] = dyv
        dlg_ref[...] += jnp.sum(dyn * xhat, axis=0, keepdims=True)
        dlb_ref[...] += jnp.sum(dyn, axis=0, keepdims=True)
        db_ref[...] += jnp.sum(dyv, axis=0, keepdims=True)
        dyh, _, _ = ln_bwd(yh_ref[...], doh_ref[...])
        dy_s[TM:TM + HALO, :] = jnp.where(i < nt - 1, dyh, 0.0)
        u_s[HALO:HALO + TM, :] = ca_ref[...] * _sigmoid(cg_ref[...])
        u_s[0:HALO, :] = jnp.where(i > 0, cah_ref[...] * _sigmoid(cgh_ref[...]), 0.0)

        for rc in range(TM // CONV_ROWS):
            acc = jnp.zeros((CONV_ROWS, CONV_W), F32)
            for k in range(CONV_K):
                acc = acc + dy_s[pl.ds(rc * CONV_ROWS + CONV_K - 1 - k, CONV_ROWS), :] * w_ref[k:k + 1, :]
            rows = slice(rc * CONV_ROWS, (rc + 1) * CONV_ROWS)
            sg = _sigmoid(cg_ref[rows, :])
            dca_ref[rows, :] = (acc * sg).astype(BF16)
            dcg_ref[rows, :] = (acc * ca_ref[rows, :] * sg * (1.0 - sg)).astype(BF16)

        for k in range(CONV_K):
            prod = u_s[pl.ds(HALO - (CONV_K - 1) + k, TM), :] * dy_s[0:TM, :]
            dw_ref[k:k + 1, :] += jnp.sum(prod, axis=0, keepdims=True)

    prev_halo = pl.BlockSpec((HALO, CONV_W), lambda i: (jnp.maximum(i * hpt - 1, 0), 0))
    next_halo = pl.BlockSpec((HALO, CONV_W), lambda i: (jnp.minimum((i + 1) * hpt, nt * hpt - 1), 0))
    vec = jax.ShapeDtypeStruct((1, CONV_W), F32)
    return _pcall(
        body, name="conv_bwd", grid=(nt,),
        in_specs=[_rows(TM, CONV_W), next_halo, _rows(TM, CONV_W), next_halo,
                  _rows(TM, CONV_W), _rows(TM, CONV_W), prev_halo, prev_halo,
                  _full((CONV_K, CONV_W)), _full((1, CONV_W)), _full((1, CONV_W))],
        out_specs=[_rows(TM, CONV_W), _rows(TM, CONV_W), _full((32, CONV_W)),
                   _full((1, CONV_W)), _full((1, CONV_W)), _full((1, CONV_W))],
        out_shape=[jax.ShapeDtypeStruct((R, CONV_W), BF16), jax.ShapeDtypeStruct((R, CONV_W), BF16),
                   jax.ShapeDtypeStruct((32, CONV_W), F32), vec, vec, vec],
        scratch_shapes=[pltpu.VMEM((HALO + TM, CONV_W), F32), pltpu.VMEM((TM + HALO, CONV_W), F32)],
        semantics=("arbitrary",),
    )(do_conv, do_conv, y, y, ca, cg, ca, cg, conv_w, ln_g, ln_b)


def _attn_bwd(q, kv, sinks, o, lse, do):
    R = q.shape[0]
    nb = R // BLOCK
    scale = 1.0 / math.sqrt(HEAD_DIM)

    def body(sink_ref, q_ref, kvc_ref, kvp_ref, kvm_ref, o_ref, lse_ref, do_ref,
             dq_ref, dkv_ref, dkvm_ref, dsink_ref, carry_s, cur_s, prev_s):
        b = pl.program_id(0)

        @pl.when(b == 0)
        def _():
            dkvm_ref[...] = jnp.zeros_like(dkvm_ref)
            carry_s[...] = jnp.zeros_like(carry_s)
            for h in range(N_HEADS):
                dsink_ref[0, h] = 0.0

        @pl.when(b < nb)
        def _():
            sel, dist, valid, valid_m = _attn_masks(b)
            cur_s[...] = jnp.zeros_like(cur_s)
            prev_s[...] = jnp.zeros_like(prev_s)
            dqs = []
            for h in range(N_HEADS):
                g = h // GROUP
                hs = slice(HEAD_DIM * h, HEAD_DIM * (h + 1))
                ks, vs = slice(HEAD_DIM * g, HEAD_DIM * (g + 1)), slice(KV_W + HEAD_DIM * g, KV_W + HEAD_DIM * (g + 1))
                qh = q_ref[:, hs]
                kc, kp, km = kvc_ref[:, ks], kvp_ref[:, ks], kvm_ref[LEAD:BLOCK, ks]
                vc, vp, vm = kvc_ref[:, vs], kvp_ref[:, vs], kvm_ref[LEAD:BLOCK, vs]
                s_b, s_m = _attn_scores(qh, kc, kp, km, sel, dist, valid, valid_m, 2.0 ** -(h + 1))
                lse_h = lse_ref[:, h:h + 1]
                p_b = jnp.exp(s_b - lse_h)
                p_m = jnp.exp(s_m - lse_h)
                doh = do_ref[:, hs]
                delta = jnp.sum(doh * o_ref[:, hs], axis=-1, keepdims=True)
                dob = doh.astype(BF16)
                dp_b = jnp.where(sel, _dot(dob, vc, NT), _dot(dob, vp, NT))
                ds_b = p_b * (dp_b - delta)
                ds_m = (p_m * (_dot(dob, vm, NT) - delta)).astype(BF16)
                dsink_ref[0, h] += -jnp.sum(jnp.exp(sink_ref[0, h] - lse_h) * delta)
                ds_c = jnp.where(sel, ds_b, 0.0).astype(BF16)
                ds_p = jnp.where(sel, 0.0, ds_b).astype(BF16)
                p_c = jnp.where(sel, p_b, 0.0).astype(BF16)
                p_p = jnp.where(sel, 0.0, p_b).astype(BF16)
                dqs.append(scale * (_dot(ds_c, kc, NN) + _dot(ds_p, kp, NN) + _dot(ds_m, km, NN)))
                cur_s[:, ks] += scale * _dot(ds_c, qh, TN)
                cur_s[:, vs] += _dot(p_c, dob, TN)
                prev_s[:, ks] += scale * _dot(ds_p, qh, TN)
                prev_s[:, vs] += _dot(p_p, dob, TN)
                dkvm_ref[:, ks] += scale * _dot(ds_m, qh, TN)
                dkvm_ref[:, vs] += _dot(p_m.astype(BF16), dob, TN)
            dq_ref[...] = jnp.concatenate(dqs, axis=1).astype(BF16)
            dkv_ref[...] = (carry_s[...] + prev_s[...]).astype(BF16)
            carry_s[...] = cur_s[...]

        @pl.when(b == nb)
        def _():
            dkv_ref[...] = carry_s[...].astype(BF16)

    def at(off):
        return lambda b: (jnp.clip(b + off, 0, nb - 1), 0)

    blk = lambda cols, off=0: pl.BlockSpec((BLOCK, cols), at(off))
    return _pcall(
        body, name="attn_bwd", grid=(nb + 1,),
        in_specs=[pl.BlockSpec(memory_space=pltpu.SMEM), blk(512), blk(256), blk(256, -1), _full((BLOCK, 256)),
                  blk(512), blk(N_HEADS), blk(512)],
        out_specs=[blk(512), blk(256, -1), _full((N_META, 256)), pl.BlockSpec(memory_space=pltpu.SMEM)],
        out_shape=[jax.ShapeDtypeStruct((R, 512), BF16), jax.ShapeDtypeStruct((R, 256), BF16),
                   jax.ShapeDtypeStruct((N_META, 256), F32), jax.ShapeDtypeStruct((1, N_HEADS), F32)],
        scratch_shapes=[pltpu.VMEM((BLOCK, 256), F32)] * 3,
        semantics=("arbitrary",),
    )(sinks, q, kv, kv, kv, o, lse, do)


def _inproj_bwd(dh1, h0, g1, dq, dkv, dkvm, dca, dcg, w_in_t):
    R = h0.shape[0]

    def body(dh1_ref, h0_ref, g_ref, dq_ref, dkv_ref, dkvm_ref, dca_ref, dcg_ref, w_ref,
             dh0_ref, dproj_ref, hn_ref, dg_ref):
        i = pl.program_id(0)

        @pl.when(i == 0)
        def _():
            dg_ref[...] = jnp.zeros_like(dg_ref)

        dproj_ref[:, 0:512] = dq_ref[...]
        dproj_ref[:, 512:768] = dkv_ref[...]
        dproj_ref[:, 768:1280] = dca_ref[...]
        dproj_ref[:, 1280:1792] = dcg_ref[...]

        @pl.when(i == 0)
        def _():
            dproj_ref[LEAD:BLOCK, 512:768] = dkvm_ref[...].astype(BF16)

        dhn = _dot(dproj_ref[...], w_ref[...], NN)
        h = h0_ref[...]
        r = _rms_stats(h)
        g = g_ref[...]
        hn_ref[...] = (h * r * g).astype(BF16)
        dx, dg = _rms_bwd(dhn, h, r, g)
        dh0_ref[...] = dh1_ref[...] + dx
        dg_ref[...] += dg

    return _pcall(
        body, name="inproj_bwd", grid=(R // TM,),
        in_specs=[_rows(TM, D_MODEL), _rows(TM, D_MODEL), _full((1, D_MODEL)), _rows(TM, 512), _rows(TM, 256),
                  _full((N_META, 256)), _rows(TM, 512), _rows(TM, 512), _full((1792, D_MODEL))],
        out_specs=[_rows(TM, D_MODEL), _rows(TM, 1792), _rows(TM, D_MODEL), _full((1, D_MODEL))],
        out_shape=[jax.ShapeDtypeStruct((R, D_MODEL), F32), jax.ShapeDtypeStruct((R, 1792), BF16),
                   jax.ShapeDtypeStruct((R, D_MODEL), BF16), jax.ShapeDtypeStruct((1, D_MODEL), F32)],
        semantics=("arbitrary",),
    )(dh1, h0, g1, dq, dkv, dkvm, dca, dcg, w_in_t)


ANY = pl.BlockSpec(memory_space=pl.ANY)


def _position():
    return lax.axis_index("x"), lax.axis_index("y"), lax.axis_index("c")


def _device_number(p):
    return 4 * p[0] + 2 * p[1] + p[2]


def _two_level_allgather(ins, outs, block, send_sems, recv_sems, local_sems, sem_base=0):
    n = len(ins)
    x, y, c = _position()
    me, sibling = (x, y, c), (x, y, 1 - c)
    chips = [(1 - x, y), (x, 1 - y), (1 - x, 1 - y)]

    def copy(w, k, origin, to, src=None):
        return pltpu.make_async_remote_copy(
            src_ref=block(w, origin) if src is None else src, dst_ref=block(w, origin),
            send_sem=send_sems.at[sem_base + 7 * w + k], recv_sem=recv_sems.at[sem_base + 7 * w + k],
            device_id=to, device_id_type=MESH)

    mine = [pltpu.make_async_copy(ins[w], block(w, me), local_sems.at[w]) for w in range(n)]
    for cp in mine:
        cp.start()
    sent = []
    for w in range(n):
        sent.append(copy(w, 0, me, sibling, src=ins[w]))
        sent += [copy(w, 1 + j, me, (*chip, c), src=ins[w]) for j, chip in enumerate(chips)]
    for cp in sent:
        cp.start()
    for w in range(n):
        for j, chip in enumerate(chips):
            copy(w, 1 + j, (*chip, c), me).wait_recv()
            passed = copy(w, 4 + j, (*chip, c), sibling)
            passed.start()
            sent.append(passed)
    for w in range(n):
        copy(w, 0, sibling, me).wait_recv()
        for j, chip in enumerate(chips):
            copy(w, 4 + j, (*chip, 1 - c), me).wait_recv()
    for cp in sent:
        cp.wait_send()
    for cp in mine:
        cp.wait()


def _allgather_params(shards, small):
    arrays = list(shards) + list(small)
    n, ns = len(arrays), len(shards)

    def body(*refs):
        ins, outs = refs[:n], refs[n:2 * n]
        send_sems, recv_sems, local_sems = refs[2 * n:]

        def block(w, p):
            d = _device_number(p)
            if w < ns:
                r = arrays[w].shape[0]
                return outs[w].at[pl.ds(pl.multiple_of(d * r, 16), r)]
            return outs[w].at[d]

        _two_level_allgather(ins, outs, block, send_sems, recv_sems, local_sems)

    out_shape = [jax.ShapeDtypeStruct((N_DEV * a.shape[0], a.shape[1]), a.dtype) for a in shards]
    out_shape += [jax.ShapeDtypeStruct((N_DEV,) + a.shape, a.dtype) for a in small]
    return _pcall(
        body, name="allgather_params", in_specs=[ANY] * n, out_specs=[ANY] * n, out_shape=out_shape,
        scratch_shapes=[pltpu.SemaphoreType.DMA((7 * n,)), pltpu.SemaphoreType.DMA((7 * n,)),
                        pltpu.SemaphoreType.DMA((n,))],
    )(*arrays)


def _reduce_siblings(grads):
    n = len(grads)

    def body(*refs):
        ins, outs = refs[:n], refs[n:2 * n]
        send_sems, recv_sems = refs[2 * n:]
        x, y, c = _position()
        copies = [pltpu.make_async_remote_copy(
            src_ref=ins[w].at[:, 1 - c], dst_ref=outs[w], send_sem=send_sems.at[w], recv_sem=recv_sems.at[w],
            device_id=(x, y, 1 - c), device_id_type=MESH) for w in range(n)]
        for cp in copies:
            cp.start()
        for cp in copies:
            cp.wait()

    return _pcall(
        body, name="reduce_siblings", in_specs=[ANY] * n, out_specs=[ANY] * n,
        out_shape=[jax.ShapeDtypeStruct((4,) + g.shape[2:], F32) for g in grads],
        scratch_shapes=[pltpu.SemaphoreType.DMA((n,)), pltpu.SemaphoreType.DMA((n,))],
    )(*grads)


def _add_sibling(grad, received, core, name):
    _, _, r, cols = grad.shape

    def body(core_ref, g_ref, r_ref, o_ref):
        o_ref[...] = g_ref[...] + r_ref[...]

    return pl.pallas_call(
        body, name=name,
        grid_spec=pltpu.PrefetchScalarGridSpec(
            num_scalar_prefetch=1, grid=(4,),
            in_specs=[pl.BlockSpec((None, None, r, cols), lambda s, core_ref: (s, core_ref[0], 0, 0)),
                      pl.BlockSpec((None, r, cols), lambda s, core_ref: (s, 0, 0))],
            out_specs=pl.BlockSpec((None, r, cols), lambda s, core_ref: (s, 0, 0))),
        out_shape=jax.ShapeDtypeStruct((4, r, cols), F32),
        compiler_params=pltpu.CompilerParams(vmem_limit_bytes=VMEM_LIMIT),
    )(core, grad, received)


def _reduce_chips(partials, small):
    n, ns = len(partials), len(small)

    def body(*refs):
        p_ins, s_ins = refs[:n], refs[n:n + ns]
        p_outs, s_outs = refs[n + ns:2 * n + ns], refs[2 * n + ns:2 * (n + ns)]
        send_sems, recv_sems, local_sems = refs[2 * (n + ns):]
        x, y, c = _position()
        chips = [(1 - x, y), (x, 1 - y), (1 - x, 1 - y)]
        copies = [pltpu.make_async_remote_copy(
            src_ref=p_ins[w].at[2 * chip[0] + chip[1]], dst_ref=p_outs[w].at[k],
            send_sem=send_sems.at[3 * w + k], recv_sem=recv_sems.at[3 * w + k],
            device_id=(*chip, c), device_id_type=MESH) for w in range(n) for k, chip in enumerate(chips)]
        for cp in copies:
            cp.start()
        _two_level_allgather(s_ins, s_outs, lambda w, p: s_outs[w].at[_device_number(p)],
                             send_sems, recv_sems, local_sems, sem_base=3 * n)
        for cp in copies:
            cp.wait()

    out_shape = [jax.ShapeDtypeStruct((3,) + p.shape[1:], F32) for p in partials]
    out_shape += [jax.ShapeDtypeStruct((N_DEV,) + a.shape, a.dtype) for a in small]
    nsem = 3 * n + 7 * ns
    return _pcall(
        body, name="reduce_chips", in_specs=[ANY] * (n + ns), out_specs=[ANY] * (n + ns), out_shape=out_shape,
        scratch_shapes=[pltpu.SemaphoreType.DMA((nsem,)), pltpu.SemaphoreType.DMA((nsem,)),
                        pltpu.SemaphoreType.DMA((ns,))],
    )(*partials, *small)


def _sum_chips(partial, received, slot, name):
    _, r, cols = partial.shape

    def body(slot_ref, p_ref, r_ref, o_ref):
        o_ref[...] = p_ref[...] + r_ref[0] + r_ref[1] + r_ref[2]

    return pl.pallas_call(
        body, name=name,
        grid_spec=pltpu.PrefetchScalarGridSpec(
            num_scalar_prefetch=1, grid=(1,),
            in_specs=[pl.BlockSpec((None, r, cols), lambda i, slot_ref: (slot_ref[0], 0, 0)),
                      pl.BlockSpec((3, r, cols), lambda i, slot_ref: (0, 0, 0))],
            out_specs=pl.BlockSpec((r, cols), lambda i, slot_ref: (0, 0))),
        out_shape=jax.ShapeDtypeStruct((r, cols), F32),
        compiler_params=pltpu.CompilerParams(vmem_limit_bytes=VMEM_LIMIT),
    )(slot, partial, received)


def _adam(g, w, m, v):
    m = ADAM_B1 * m + (1.0 - ADAM_B1) * g
    v = ADAM_B2 * v + (1.0 - ADAM_B2) * (g * g)
    m_hat = m / (1.0 - ADAM_B1 ** ADAM_STEP)
    v_hat = v / (1.0 - ADAM_B2 ** ADAM_STEP)
    delta = -ADAM_LR * (m_hat / (jnp.sqrt(v_hat) + ADAM_EPS) + ADAM_WD * w)
    return delta, m, v


def _adamw(g, w, m, v, name):
    def body(g_ref, w_ref, m_ref, v_ref, d_ref, nm_ref, nv_ref):
        d_ref[...], nm_ref[...], nv_ref[...] = _adam(g_ref[...], w_ref[...], m_ref[...], v_ref[...])

    out = jax.ShapeDtypeStruct(w.shape, F32)
    return _pcall(body, name=name, out_shape=[out, out, out])(g, w, m, v)


def _adamw_small(dev, ga, gb, gc, params):
    names = ["meta", "attn_norm", "sinks", "conv_w", "conv_b", "ln_g", "ln_b", "attn_out", "conv_out",
             "ffn_norm", "final_norm"]
    flat = [a for p in params for a in p]
    n_in = len(flat)

    def body(dev_ref, ga_ref, gb_ref, gc_ref, *refs):
        ins, outs = refs[:n_in], refs[n_in:n_in + 4 * len(names)]
        sb, sc = refs[n_in + 4 * len(names):]
        a = ga_ref[0]
        sb[...] = gb_ref[0]
        sc[...] = gc_ref[0]
        for d in range(1, N_DEV):
            a = a + ga_ref[d]
            sb[...] += gb_ref[d]
            sc[...] += gc_ref[d]
        dev = dev_ref[0]
        grads = {
            "attn_norm": a[0:1, :], "ffn_norm": a[1:2, :], "final_norm": a[2:3, :],
            "conv_b": a[3:4, 0:512], "ln_g": a[3:4, 512:1024], "ln_b": a[4:5, 0:512],
            "attn_out": a[4:5, 512:1024], "conv_out": a[5:6, 0:512], "sinks": a[5:6, 512:512 + N_HEADS],
            "meta": sb[pl.ds(pl.multiple_of(dev * N_META, N_META), N_META), :],
            "conv_w": sc[pl.ds(pl.multiple_of(dev * 32, 32), 32), :][0:CONV_K, :],
        }
        for idx, nm in enumerate(names):
            w_ref, m_ref, v_ref = ins[3 * idx:3 * idx + 3]
            g = grads[nm]
            delta, m, v = _adam(g, w_ref[...], m_ref[...], v_ref[...])
            o = outs[4 * idx:4 * idx + 4]
            o[0][...], o[1][...], o[2][...], o[3][...] = g, delta, m, v

    vm = pl.BlockSpec(memory_space=pltpu.VMEM)
    out_shape = [jax.ShapeDtypeStruct(p[0].shape, F32) for p in params for _ in range(4)]
    res = pl.pallas_call(
        body, name="adamw_small",
        grid_spec=pltpu.PrefetchScalarGridSpec(
            num_scalar_prefetch=1, grid=(1,),
            in_specs=[pl.BlockSpec(ga.shape, lambda i, d: (0, 0, 0)), pl.BlockSpec(gb.shape, lambda i, d: (0, 0, 0)),
                      pl.BlockSpec(gc.shape, lambda i, d: (0, 0, 0))]
            + [pl.BlockSpec(a.shape, lambda i, d: (0, 0)) for a in flat],
            out_specs=[pl.BlockSpec(s.shape, lambda i, d: (0, 0)) for s in out_shape],
            scratch_shapes=[pltpu.VMEM(gb.shape[1:], F32), pltpu.VMEM(gc.shape[1:], F32)]),
        out_shape=out_shape,
        compiler_params=pltpu.CompilerParams(vmem_limit_bytes=VMEM_LIMIT),
    )(dev, ga, gb, gc, *flat)
    return [res[4 * i:4 * i + 4] for i in range(len(names))]


def kernel(x, meta_tokens, attn_norm_g, w_in, attn_sinks, conv_w, conv_b, conv_ln_g, conv_ln_b, attn_out_g, conv_out_g, w_out, ffn_norm_g, w_gate, w_up, w_down, final_norm_g, loss_target, m_meta_tokens, m_attn_norm_g, m_w_in, m_attn_sinks, m_conv_w, m_conv_b, m_conv_ln_g, m_conv_ln_b, m_attn_out_g, m_conv_out_g, m_w_out, m_ffn_norm_g, m_w_gate, m_w_up, m_w_down, m_final_norm_g, v_meta_tokens, v_attn_norm_g, v_w_in, v_attn_sinks, v_conv_w, v_conv_b, v_conv_ln_g, v_conv_ln_b, v_attn_out_g, v_conv_out_g, v_w_out, v_ffn_norm_g, v_w_gate, v_w_up, v_w_down, v_final_norm_g):
    xi, yi, ci = _position()
    dev = jnp.reshape(_device_number((xi, yi, ci)), (1,)).astype(jnp.int32)
    core = jnp.reshape(ci, (1,)).astype(jnp.int32)
    slot = jnp.reshape(2 * xi + yi, (1,)).astype(jnp.int32)

    shards = [w_in[0].T.astype(BF16), w_out[0].astype(BF16), w_gate[0].T.astype(BF16), w_up[0].T.astype(BF16),
              w_down[0].astype(BF16)]
    w_in_t, w_out_b, wg_t, wu_t, wd_b, meta_st, convw_st = _allgather_params(shards, [meta_tokens, conv_w[0]])
    meta_full = jnp.transpose(meta_st, (1, 0, 2)).reshape(N_META, D_MODEL)
    convw_full = jnp.transpose(convw_st, (1, 0, 2)).reshape(CONV_K, CONV_W)

    h0 = jnp.concatenate([jnp.zeros((LEAD, D_MODEL), F32), meta_full, x[0]], axis=0)
    final_g = final_norm_g.reshape(1, D_MODEL)

    q, kv, ca, cg = _inproj_fwd(h0, attn_norm_g, w_in_t)
    o_attn, lse = _attn_fwd(q, kv, attn_sinks)
    o_conv, y_conv = _conv_fwd(ca, cg, convw_full, conv_b, conv_ln_g, conv_ln_b)
    h1 = _outproj_fwd(h0, o_attn, o_conv, attn_out_g, conv_out_g, w_out_b)
    gate, up, dh2, loss_sum, dg_final = _ffn_fwd(h1, ffn_norm_g, wg_t, wu_t, wd_b, final_g, loss_target[0])

    dgate, dup, act, hn2, dh1, dg_ffn = _ffn_bwd(dh2, h1, ffn_norm_g, gate, up, wg_t, wu_t, wd_b)
    dwg_t = _wgrad(dgate, hn2, FF_CHUNK, "wgrad_gate")
    dwu_t = _wgrad(dup, hn2, FF_CHUNK, "wgrad_up")
    dwd = _wgrad(act, dh2, FF_CHUNK, "wgrad_down")
    do_attn, do_conv, mixed, dg_ao, dg_co = _outproj_bwd(dh1, o_attn, o_conv, attn_out_g, conv_out_g, w_out_b)
    dwo = _wgrad(mixed, dh1, D_MODEL, "wgrad_out")
    dca, dcg, dconvw, dconvb, dln_g, dln_b = _conv_bwd(do_conv, y_conv, ca, cg, convw_full, conv_ln_g, conv_ln_b)
    dq, dkv, dkvm, dsinks = _attn_bwd(q, kv, attn_sinks, o_attn, lse, do_attn)
    dh0, dproj, hn1, dg_attn = _inproj_bwd(dh1, h0, attn_norm_g, dq, dkv, dkvm, dca, dcg, w_in_t)
    dwi_t = _wgrad(dproj, hn1, 1792, "wgrad_in")

    grads = [g.reshape(4, 2, g.shape[0] // N_DEV, D_MODEL) for g in (dwi_t, dwo, dwg_t, dwu_t, dwd)]
    from_sibling = _reduce_siblings(grads)
    tags = ("in", "out", "gate", "up", "down")
    chip_sums = [_add_sibling(g, r, core, "add_sibling_" + t) for g, r, t in zip(grads, from_sibling, tags)]
    small_a = jnp.concatenate([
        dg_attn, dg_ffn, dg_final, jnp.concatenate([dconvb, dln_g], axis=1), jnp.concatenate([dln_b, dg_ao], axis=1),
        jnp.concatenate([dg_co, dsinks, jnp.zeros((1, 512 - N_HEADS), F32)], axis=1),
        jnp.zeros((2, D_MODEL), F32)], axis=0)
    small_b = jnp.transpose(dh0[LEAD:BLOCK].reshape(N_META, N_DEV, 128), (1, 0, 2)).reshape(N_DEV * N_META, 128)
    small_c = jnp.transpose(dconvw.reshape(32, N_DEV, 64), (1, 0, 2)).reshape(N_DEV * 32, 64)
    *from_chips, ga, gb, gc = _reduce_chips(chip_sums, [small_a, small_b, small_c])
    g_own = [_sum_chips(p, r, slot, "sum_chips_" + t) for p, r, t in zip(chip_sums, from_chips, tags)]
    g_big = [g_own[0].T, g_own[1], g_own[2].T, g_own[3].T, g_own[4]]

    big = [(w_in, m_w_in, v_w_in), (w_out, m_w_out, v_w_out), (w_gate, m_w_gate, v_w_gate), (w_up, m_w_up, v_w_up),
           (w_down, m_w_down, v_w_down)]
    big_out = {}
    for t, g, (w, m, v) in zip(tags, g_big, big):
        delta, nm, nv = _adamw(g, w[0], m[0], v[0], "adamw_" + t)
        big_out[t] = (g[None], delta[None], nm[None], nv[None])

    small_params = [
        (meta_tokens, m_meta_tokens, v_meta_tokens), (attn_norm_g, m_attn_norm_g, v_attn_norm_g),
        (attn_sinks, m_attn_sinks, v_attn_sinks), (conv_w[0], m_conv_w[0], v_conv_w[0]),
        (conv_b, m_conv_b, v_conv_b), (conv_ln_g, m_conv_ln_g, v_conv_ln_g), (conv_ln_b, m_conv_ln_b, v_conv_ln_b),
        (attn_out_g, m_attn_out_g, v_attn_out_g), (conv_out_g, m_conv_out_g, v_conv_out_g),
        (ffn_norm_g, m_ffn_norm_g, v_ffn_norm_g),
        (final_g, m_final_norm_g.reshape(1, D_MODEL), v_final_norm_g.reshape(1, D_MODEL))]
    sm = _adamw_small(dev, ga, gb, gc, small_params)
    sm[3] = [a[None] for a in sm[3]]
    sm[10] = [a.reshape(D_MODEL) for a in sm[10]]

    per_param = [sm[0], sm[1], big_out["in"], sm[2], sm[3], sm[4], sm[5], sm[6], sm[7], sm[8], big_out["out"],
                 sm[9], big_out["gate"], big_out["up"], big_out["down"], sm[10]]
    loss = lax.psum(loss_sum[0, 0], ("x", "y", "c"))
    grad_x = dh0[BLOCK:][None]
    outs = [loss, grad_x]
    for kind in range(4):
        outs += [p[kind] for p in per_param]
    return tuple(outs)
```

```python
import functools
import math

import jax
import jax.numpy as jnp
from jax import lax
from jax.experimental import pallas as pl
from jax.experimental.pallas import tpu as pltpu

F32, BF16 = jnp.float32, jnp.bfloat16
MESH = pl.DeviceIdType.MESH

D_MODEL = 1024
N_META = 16
BLOCK = 128
LEAD = BLOCK - N_META
HEAD_DIM = 64
N_HEADS = 8
GROUP = 4
ATTN_W = 512
KV_W = 128
CONV_W = 512
CONV_K = 31
HALO = 32
D_FF = 2816
FF_CHUNK = D_FF // 2
FF_SUB = [slice(s, min(s + 256, FF_CHUNK)) for s in range(0, FF_CHUNK, 256)]
N_DEV = 8
EPS = 1e-5
NEG = -1e30
TM = 640
CONV_ROWS = 64
VMEM_LIMIT = 56 * 1024 * 1024

ADAM_LR, ADAM_B1, ADAM_B2, ADAM_EPS, ADAM_WD, ADAM_STEP = 0.001, 0.9, 0.999, 1e-08, 0.01, 10

NT = (((1,), (1,)), ((), ()))
NN = (((1,), (0,)), ((), ()))
TN = (((0,), (0,)), ((), ()))


def _dot(a, b, dims):
    return lax.dot_general(a, b, dims, preferred_element_type=F32)


def _sigmoid(x):
    return 1.0 / (1.0 + jnp.exp(-x))


def _pcall(body, *, name, out_shape, grid=None, in_specs=None, out_specs=None, scratch_shapes=(),
           semantics=None, **kw):
    params = dict(vmem_limit_bytes=VMEM_LIMIT)
    if semantics is not None:
        params["dimension_semantics"] = semantics
    extra = {}
    if grid is not None:
        extra["grid"] = grid
    if in_specs is not None:
        extra["in_specs"] = in_specs
    if out_specs is not None:
        extra["out_specs"] = out_specs
    return pl.pallas_call(body, name=name, out_shape=out_shape, scratch_shapes=list(scratch_shapes),
                          compiler_params=pltpu.CompilerParams(**params), **extra, **kw)


def _rows(tm, cols, off=0):
    return pl.BlockSpec((tm, cols), lambda i, *_: (i + off, 0))


def _full(shape):
    nd = len(shape)
    return pl.BlockSpec(shape, lambda *_: (0,) * nd)


def _rms_stats(x):
    return lax.rsqrt(jnp.mean(x * x, axis=-1, keepdims=True) + EPS)


def _rms_bwd(dy, x, r, g):
    t = dy * g
    dx = r * (t - x * (r * r) * jnp.mean(t * x, axis=-1, keepdims=True))
    dg = jnp.sum(dy * x * r, axis=0, keepdims=True)
    return dx, dg


def _inproj_fwd(h0, g1, w_in_t):
    R = h0.shape[0]

    def body(h_ref, g_ref, w_ref, q_ref, kv_ref, ca_ref, cg_ref):
        h = h_ref[...]
        hn = (h * _rms_stats(h) * g_ref[...]).astype(BF16)
        q_ref[...] = _dot(hn, w_ref[0:512, :], NT).astype(BF16)
        kv_ref[...] = _dot(hn, w_ref[512:768, :], NT).astype(BF16)
        ca_ref[...] = _dot(hn, w_ref[768:1280, :], NT)
        cg_ref[...] = _dot(hn, w_ref[1280:1792, :], NT)

    return _pcall(
        body, name="inproj_fwd", grid=(R // TM,),
        in_specs=[_rows(TM, D_MODEL), _full((1, D_MODEL)), _full((1792, D_MODEL))],
        out_specs=[_rows(TM, 512), _rows(TM, 256), _rows(TM, 512), _rows(TM, 512)],
        out_shape=[jax.ShapeDtypeStruct((R, 512), BF16), jax.ShapeDtypeStruct((R, 256), BF16),
                   jax.ShapeDtypeStruct((R, 512), F32), jax.ShapeDtypeStruct((R, 512), F32)],
        semantics=("parallel",),
    )(h0, g1, w_in_t)


def _attn_masks(b):
    ii = lax.broadcasted_iota(jnp.int32, (BLOCK, BLOCK), 0)
    jj = lax.broadcasted_iota(jnp.int32, (BLOCK, BLOCK), 1)
    sel = jj <= ii
    dist = jnp.where(sel, ii - jj, ii - jj + BLOCK).astype(F32)
    valid = b >= jnp.where(sel, 1, 2)
    mi = lax.broadcasted_iota(jnp.int32, (BLOCK, N_META), 0)
    mj = lax.broadcasted_iota(jnp.int32, (BLOCK, N_META), 1)
    valid_m = (mj + LEAD) <= (mi + b * BLOCK)
    return sel, dist, valid, valid_m


def _attn_scores(q, kc, kp, km, sel, dist, valid, valid_m, slope):
    scale = 1.0 / math.sqrt(HEAD_DIM)
    s_b = jnp.where(sel, _dot(q, kc, NT), _dot(q, kp, NT)) * scale
    s_b = jnp.where(valid, s_b - slope * dist, NEG)
    s_m = jnp.where(valid_m, _dot(q, km, NT) * scale, NEG)
    return s_b, s_m


def _attn_fwd(q, kv, sinks, shards):
    R = q.shape[0]
    nb = R // BLOCK
    ns = len(shards)
    forward_steps = [(w + 1) * (nb - 8) // (ns + 1) for w in range(ns)]

    def body(sink_ref, q_ref, kvc_ref, kvp_ref, kvm_ref, *refs):
        ag_ins, (o_ref, lse_ref), ag_outs = refs[:ns], refs[ns:ns + 2], refs[ns + 2:2 * ns + 2]
        send_sems, recv_sems, local_sems = refs[2 * ns + 2:]
        b = pl.program_id(0)
        ag_start, ag_forward, ag_finish = _two_level_allgather(
            ag_ins, ag_outs, _row_block(ag_outs, [s.shape[0] for s in shards]), send_sems, recv_sems, local_sems)
        pl.when(b == 0)(ag_start)
        for w, step in enumerate(forward_steps):
            pl.when(b == step)(functools.partial(ag_forward, w))
        sel, dist, valid, valid_m = _attn_masks(b)
        outs = []
        for h in range(N_HEADS):
            g = h // GROUP
            ks, vs = slice(HEAD_DIM * g, HEAD_DIM * (g + 1)), slice(KV_W + HEAD_DIM * g, KV_W + HEAD_DIM * (g + 1))
            qh = q_ref[:, HEAD_DIM * h:HEAD_DIM * (h + 1)]
            s_b, s_m = _attn_scores(qh, kvc_ref[:, ks], kvp_ref[:, ks], kvm_ref[LEAD:BLOCK, ks],
                                    sel, dist, valid, valid_m, 2.0 ** -(h + 1))
            sink = sink_ref[0, h]
            m = jnp.maximum(jnp.maximum(jnp.max(s_b, axis=-1, keepdims=True),
                                        jnp.max(s_m, axis=-1, keepdims=True)), sink)
            p_b = jnp.exp(s_b - m)
            p_m = jnp.exp(s_m - m)
            l = jnp.sum(p_b, axis=-1, keepdims=True) + jnp.sum(p_m, axis=-1, keepdims=True) + jnp.exp(sink - m)
            p_c = jnp.where(sel, p_b, 0.0).astype(BF16)
            p_p = jnp.where(sel, 0.0, p_b).astype(BF16)
            o = (_dot(p_c, kvc_ref[:, vs], NN) + _dot(p_p, kvp_ref[:, vs], NN)
                 + _dot(p_m.astype(BF16), kvm_ref[LEAD:BLOCK, vs], NN))
            outs.append(o / l)
            lse_ref[:, h:h + 1] = m + jnp.log(l)
        o_ref[...] = jnp.concatenate(outs, axis=1)
        pl.when(b == nb - 1)(ag_finish)

    res = _pcall(
        body, name="attn_fwd", grid=(nb,),
        in_specs=[pl.BlockSpec(memory_space=pltpu.SMEM),
                  _rows(BLOCK, 512), _rows(BLOCK, 256),
                  pl.BlockSpec((BLOCK, 256), lambda b: (jnp.maximum(b - 1, 0), 0)),
                  _full((BLOCK, 256))] + [ANY] * ns,
        out_specs=[_rows(BLOCK, 512), _rows(BLOCK, N_HEADS)] + [ANY] * ns,
        out_shape=[jax.ShapeDtypeStruct((R, 512), F32), jax.ShapeDtypeStruct((R, N_HEADS), F32)]
        + [jax.ShapeDtypeStruct((N_DEV * s.shape[0], s.shape[1]), s.dtype) for s in shards],
        scratch_shapes=_sem_pair(7 * ns) + [pltpu.SemaphoreType.DMA((ns,))],
        semantics=("arbitrary",),
    )(sinks, q, kv, kv, kv, *shards)
    return res[0], res[1], res[2:]


def _ln_silu(y, lg, lb):
    mu = jnp.mean(y, axis=-1, keepdims=True)
    xc = y - mu
    rstd = lax.rsqrt(jnp.mean(xc * xc, axis=-1, keepdims=True) + EPS)
    xhat = xc * rstd
    yn = xhat * lg + lb
    return yn, xhat, rstd


def _conv_fwd(ca, cg, conv_w, conv_b, ln_g, ln_b):
    R = ca.shape[0]
    nt = R // TM
    hpt = TM // HALO

    def body(ca_ref, cg_ref, cah_ref, cgh_ref, w_ref, b_ref, lg_ref, lb_ref, oc_ref, y_ref, u_s):
        i = pl.program_id(0)
        u_s[HALO:HALO + TM, :] = ca_ref[...] * _sigmoid(cg_ref[...])
        u_s[0:HALO, :] = jnp.where(i > 0, cah_ref[...] * _sigmoid(cgh_ref[...]), 0.0)
        for rc in range(TM // CONV_ROWS):
            base = rc * CONV_ROWS + HALO - (CONV_K - 1)
            acc = jnp.zeros((CONV_ROWS, CONV_W), F32) + b_ref[...]
            for k in range(CONV_K):
                acc = acc + u_s[pl.ds(base + k, CONV_ROWS), :] * w_ref[k:k + 1, :]
            rows = slice(rc * CONV_ROWS, (rc + 1) * CONV_ROWS)
            y_ref[rows, :] = acc
            yn, _, _ = _ln_silu(acc, lg_ref[...], lb_ref[...])
            oc_ref[rows, :] = yn * _sigmoid(yn)

    prev_halo = pl.BlockSpec((HALO, CONV_W), lambda i: (jnp.maximum(i * hpt - 1, 0), 0))
    return _pcall(
        body, name="conv_fwd", grid=(nt,),
        in_specs=[_rows(TM, CONV_W), _rows(TM, CONV_W), prev_halo, prev_halo,
                  _full((CONV_K, CONV_W)), _full((1, CONV_W)), _full((1, CONV_W)), _full((1, CONV_W))],
        out_specs=[_rows(TM, CONV_W), _rows(TM, CONV_W)],
        out_shape=[jax.ShapeDtypeStruct((R, CONV_W), F32), jax.ShapeDtypeStruct((R, CONV_W), F32)],
        scratch_shapes=[pltpu.VMEM((HALO + TM, CONV_W), F32)],
        semantics=("parallel",),
    )(ca, cg, ca, cg, conv_w, conv_b, ln_g, ln_b)


def _outproj_fwd(h0, o_attn, o_conv, ga, gc, w_out):
    R = h0.shape[0]

    def body(h_ref, oa_ref, oc_ref, ga_ref, gc_ref, w_ref, h1_ref):
        oa, oc = oa_ref[...], oc_ref[...]
        ma = (oa * _rms_stats(oa) * ga_ref[...]).astype(BF16)
        mc = (oc * _rms_stats(oc) * gc_ref[...]).astype(BF16)
        h1_ref[...] = h_ref[...] + _dot(ma, w_ref[0:512, :], NN) + _dot(mc, w_ref[512:1024, :], NN)

    return _pcall(
        body, name="outproj_fwd", grid=(R // TM,),
        in_specs=[_rows(TM, D_MODEL), _rows(TM, 512), _rows(TM, 512), _full((1, 512)), _full((1, 512)),
                  _full((D_MODEL, D_MODEL))],
        out_specs=_rows(TM, D_MODEL),
        out_shape=jax.ShapeDtypeStruct((R, D_MODEL), F32),
        semantics=("parallel",),
    )(h0, o_attn, o_conv, ga, gc, w_out)


def _target_copy(tgt_hbm, tgt_s, sem, i, first):
    if first:
        return pltpu.make_async_copy(tgt_hbm.at[pl.ds(0, TM - BLOCK)], tgt_s.at[pl.ds(BLOCK, TM - BLOCK)], sem)
    return pltpu.make_async_copy(tgt_hbm.at[pl.ds(i * TM - BLOCK, TM)], tgt_s, sem)


def _ffn_fwd(h1, g2, wg_t, wu_t, wd, gf, target):
    R = h1.shape[0]
    nt, nj = R // TM, D_FF // FF_CHUNK

    def body(h1_ref, g2_ref, wg_ref, wu_ref, wd_ref, gf_ref, tgt_hbm,
             gate_ref, up_ref, dh2_ref, loss_ref, dgf_ref, hn_s, acc_s, tgt_s, sem):
        i, j = pl.program_id(0), pl.program_id(1)

        @pl.when((i == 0) & (j == 0))
        def _():
            loss_ref[...] = jnp.zeros_like(loss_ref)
            dgf_ref[...] = jnp.zeros_like(dgf_ref)

        @pl.when(j == 0)
        def _():
            h1 = h1_ref[...]
            hn_s[...] = (h1 * _rms_stats(h1) * g2_ref[...]).astype(BF16)

            @pl.when(i == 0)
            def _():
                tgt_s[0:BLOCK, :] = jnp.zeros((BLOCK, D_MODEL), F32)
                _target_copy(tgt_hbm, tgt_s, sem, i, True).start()

            @pl.when(i > 0)
            def _():
                _target_copy(tgt_hbm, tgt_s, sem, i, False).start()

        hn = hn_s[...]
        part = None
        for cs in FF_SUB:
            gate = _dot(hn, wg_ref[cs, :], NT)
            up = _dot(hn, wu_ref[cs, :], NT)
            gate_ref[:, cs] = gate.astype(BF16)
            up_ref[:, cs] = up.astype(BF16)
            act = (gate * _sigmoid(gate) * up).astype(BF16)
            d = _dot(act, wd_ref[cs, :], NN)
            part = d if part is None else part + d

        @pl.when(j == 0)
        def _():
            acc_s[...] = part

        @pl.when(j == nj - 1)
        def _():
            @pl.when(i == 0)
            def _():
                _target_copy(tgt_hbm, tgt_s, sem, i, True).wait()

            @pl.when(i > 0)
            def _():
                _target_copy(tgt_hbm, tgt_s, sem, i, False).wait()

            h2 = h1_ref[...] + acc_s[...] + part
            rf = _rms_stats(h2)
            gf = gf_ref[...]
            row = lax.broadcasted_iota(jnp.int32, (TM, 1), 0) + i * TM
            err = jnp.where(row >= BLOCK, h2 * rf * gf - tgt_s[...], 0.0)
            dy = err * (1.0 / D_MODEL)
            dh2, dgf = _rms_bwd(dy, h2, rf, gf)
            dh2_ref[...] = dh2
            loss_ref[...] += (0.5 / D_MODEL) * jnp.sum(err * err)
            dgf_ref[...] += dgf

    wspec = pl.BlockSpec((FF_CHUNK, D_MODEL), lambda i, j: (j, 0))
    aspec = pl.BlockSpec((TM, FF_CHUNK), lambda i, j: (i, j))
    return _pcall(
        body, name="ffn_fwd", grid=(nt, nj),
        in_specs=[_rows(TM, D_MODEL), _full((1, D_MODEL)), wspec, wspec, wspec, _full((1, D_MODEL)),
                  pl.BlockSpec(memory_space=pl.ANY)],
        out_specs=[aspec, aspec, _rows(TM, D_MODEL),
                   _full((8, 128)), _full((1, D_MODEL))],
        out_shape=[jax.ShapeDtypeStruct((R, D_FF), BF16), jax.ShapeDtypeStruct((R, D_FF), BF16),
                   jax.ShapeDtypeStruct((R, D_MODEL), F32),
                   jax.ShapeDtypeStruct((8, 128), F32), jax.ShapeDtypeStruct((1, D_MODEL), F32)],
        scratch_shapes=[pltpu.VMEM((TM, D_MODEL), BF16), pltpu.VMEM((TM, D_MODEL), F32),
                        pltpu.VMEM((TM, D_MODEL), F32), pltpu.SemaphoreType.DMA],
        semantics=("arbitrary", "arbitrary"),
    )(h1, g2, wg_t, wu_t, wd, gf, target)


def _ffn_bwd(dh2, h1, g2, gate, up, wg_t, wu_t, wd):
    R = h1.shape[0]
    nt, nj = R // TM, D_FF // FF_CHUNK

    wspec = pl.BlockSpec((FF_CHUNK, D_MODEL), lambda i, j: (j, 0))
    aspec = pl.BlockSpec((TM, FF_CHUNK), lambda i, j: (i, j))
    act_shape = jax.ShapeDtypeStruct((R, D_FF), BF16)

    def act_body(dh2_ref, gate_ref, up_ref, wd_ref, dgate_ref, dup_ref, act_ref, dhb_s):
        @pl.when(pl.program_id(1) == 0)
        def _():
            dhb_s[...] = dh2_ref[...].astype(BF16)

        dhb = dhb_s[...]
        for cs in FF_SUB:
            dact = _dot(dhb, wd_ref[cs, :], NT)
            gate = gate_ref[:, cs].astype(F32)
            up = up_ref[:, cs].astype(F32)
            sig = _sigmoid(gate)
            silu = gate * sig
            dgate_ref[:, cs] = (dact * up * (sig * (1.0 + gate * (1.0 - sig)))).astype(BF16)
            dup_ref[:, cs] = (dact * silu).astype(BF16)
            act_ref[:, cs] = (silu * up).astype(BF16)

    dgate, dup, act = _pcall(
        act_body, name="ffn_bwd_act", grid=(nt, nj),
        in_specs=[_rows(TM, D_MODEL), aspec, aspec, wspec],
        out_specs=[aspec, aspec, aspec], out_shape=[act_shape, act_shape, act_shape],
        scratch_shapes=[pltpu.VMEM((TM, D_MODEL), BF16)],
        semantics=("parallel", "arbitrary"),
    )(dh2, gate, up, wd)

    def in_body(dh2_ref, h1_ref, g2_ref, dgate_ref, dup_ref, wg_ref, wu_ref, hn_ref, dh1_ref, dg2_ref, acc_s):
        i, j = pl.program_id(0), pl.program_id(1)

        @pl.when((i == 0) & (j == 0))
        def _():
            dg2_ref[...] = jnp.zeros_like(dg2_ref)

        part = None
        for cs in FF_SUB:
            d = _dot(dgate_ref[:, cs], wg_ref[cs, :], NN) + _dot(dup_ref[:, cs], wu_ref[cs, :], NN)
            part = d if part is None else part + d

        @pl.when(j == 0)
        def _():
            acc_s[...] = part

        @pl.when(j == nj - 1)
        def _():
            h1 = h1_ref[...]
            r = _rms_stats(h1)
            g2 = g2_ref[...]
            hn_ref[...] = (h1 * r * g2).astype(BF16)
            dx, dg = _rms_bwd(acc_s[...] + part, h1, r, g2)
            dh1_ref[...] = dh2_ref[...] + dx
            dg2_ref[...] += dg

    hn2, dh1, dg2 = _pcall(
        in_body, name="ffn_bwd_in", grid=(nt, nj),
        in_specs=[_rows(TM, D_MODEL), _rows(TM, D_MODEL), _full((1, D_MODEL)), aspec, aspec, wspec, wspec],
        out_specs=[_rows(TM, D_MODEL), _rows(TM, D_MODEL), _full((1, D_MODEL))],
        out_shape=[jax.ShapeDtypeStruct((R, D_MODEL), BF16), jax.ShapeDtypeStruct((R, D_MODEL), F32),
                   jax.ShapeDtypeStruct((1, D_MODEL), F32)],
        scratch_shapes=[pltpu.VMEM((TM, D_MODEL), F32)],
        semantics=("arbitrary", "arbitrary"),
    )(dh2, h1, g2, dgate, dup, wg_t, wu_t)
    return dgate, dup, act, hn2, dh1, dg2


def _wgrad(a, b, tm, name):
    K, M = a.shape
    N = b.shape[1]

    def body(a_ref, b_ref, o_ref):
        @pl.when(pl.program_id(1) == 0)
        def _():
            o_ref[...] = jnp.zeros_like(o_ref)

        o_ref[...] += _dot(a_ref[...], b_ref[...].astype(BF16), TN)

    return _pcall(
        body, name=name, grid=(M // tm, K // TM),
        in_specs=[pl.BlockSpec((TM, tm), lambda m, k: (k, m)), pl.BlockSpec((TM, N), lambda m, k: (k, 0))],
        out_specs=pl.BlockSpec((tm, N), lambda m, k: (m, 0)),
        out_shape=jax.ShapeDtypeStruct((M, N), F32),
        semantics=("parallel", "arbitrary"),
    )(a, b)


def _outproj_bwd(dh1, o_attn, o_conv, ga, gc, w_out, grads):
    R = dh1.shape[0]
    nt, ng = R // TM, len(grads)

    def body(dh1_ref, oa_ref, oc_ref, ga_ref, gc_ref, w_ref, *refs):
        g_ins, (doa_ref, doc_ref, mixed_ref, dga_ref, dgc_ref) = refs[:ng], refs[ng:ng + 5]
        g_outs, (send_sems, recv_sems) = refs[ng + 5:2 * ng + 5], refs[2 * ng + 5:]
        exchange = functools.partial(_sibling_copies, g_ins, g_outs, send_sems, recv_sems)
        _hosted(pl.program_id(0), nt, exchange)

        @pl.when(pl.program_id(0) == 0)
        def _():
            dga_ref[...] = jnp.zeros_like(dga_ref)
            dgc_ref[...] = jnp.zeros_like(dgc_ref)

        dm = _dot(dh1_ref[...].astype(BF16), w_ref[...], NT)
        oa, oc = oa_ref[...], oc_ref[...]
        ra, rc = _rms_stats(oa), _rms_stats(oc)
        mixed_ref[:, 0:512] = (oa * ra * ga_ref[...]).astype(BF16)
        mixed_ref[:, 512:1024] = (oc * rc * gc_ref[...]).astype(BF16)
        doa, dga = _rms_bwd(dm[:, 0:512], oa, ra, ga_ref[...])
        doc, dgc = _rms_bwd(dm[:, 512:1024], oc, rc, gc_ref[...])
        doa_ref[...] = doa
        doc_ref[...] = doc
        dga_ref[...] += dga
        dgc_ref[...] += dgc
        _hosted_wait(pl.program_id(0), nt, exchange)

    res = _pcall(
        body, name="outproj_bwd", grid=(nt,),
        in_specs=[_rows(TM, D_MODEL), _rows(TM, 512), _rows(TM, 512), _full((1, 512)), _full((1, 512)),
                  _full((D_MODEL, D_MODEL))] + [ANY] * ng,
        out_specs=[_rows(TM, 512), _rows(TM, 512), _rows(TM, D_MODEL), _full((1, 512)), _full((1, 512))]
        + [ANY] * ng,
        out_shape=[jax.ShapeDtypeStruct((R, 512), F32), jax.ShapeDtypeStruct((R, 512), F32),
                   jax.ShapeDtypeStruct((R, D_MODEL), BF16),
                   jax.ShapeDtypeStruct((1, 512), F32), jax.ShapeDtypeStruct((1, 512), F32)]
        + _sibling_shapes(grads),
        scratch_shapes=_sem_pair(ng),
        semantics=("arbitrary",),
    )(dh1, o_attn, o_conv, ga, gc, w_out, *grads)
    return res[:5], res[5:]


def _conv_bwd(do_conv, y, ca, cg, conv_w, ln_g, ln_b, partials):
    R = ca.shape[0]
    nt = R // TM
    hpt = TM // HALO
    npart = len(partials)

    def body(do_ref, doh_ref, y_ref, yh_ref, ca_ref, cg_ref, cah_ref, cgh_ref, w_ref, lg_ref, lb_ref, *refs):
        p_ins, (dca_ref, dcg_ref, dw_ref, db_ref, dlg_ref, dlb_ref) = refs[:npart], refs[npart:npart + 6]
        p_outs, (send_sems, recv_sems, u_s, dy_s) = refs[npart + 6:2 * npart + 6], refs[2 * npart + 6:]
        i = pl.program_id(0)
        exchange = functools.partial(_chip_copies, p_ins, p_outs, send_sems, recv_sems)
        _hosted(i, nt, exchange)

        @pl.when(i == 0)
        def _():
            dw_ref[...] = jnp.zeros_like(dw_ref)
            db_ref[...] = jnp.zeros_like(db_ref)
            dlg_ref[...] = jnp.zeros_like(dlg_ref)
            dlb_ref[...] = jnp.zeros_like(dlb_ref)

        lg, lb = lg_ref[...], lb_ref[...]

        def ln_bwd(yv, dov):
            yn, xhat, rstd = _ln_silu(yv, lg, lb)
            sig = _sigmoid(yn)
            dyn = dov * (sig * (1.0 + yn * (1.0 - sig)))
            dxh = dyn * lg
            dyv = rstd * (dxh - jnp.mean(dxh, axis=-1, keepdims=True)
                          - xhat * jnp.mean(dxh * xhat, axis=-1, keepdims=True))
            return dyv, dyn, xhat

        dyv, dyn, xhat = ln_bwd(y_ref[...], do_ref[...])
        dy_s[0:TM, :] = dyv
        dlg_ref[...] += jnp.sum(dyn * xhat, axis=0, keepdims=True)
        dlb_ref[...] += jnp.sum(dyn, axis=0, keepdims=True)
        db_ref[...] += jnp.sum(dyv, axis=0, keepdims=True)
        dyh, _, _ = ln_bwd(yh_ref[...], doh_ref[...])
        dy_s[TM:TM + HALO, :] = jnp.where(i < nt - 1, dyh, 0.0)
        u_s[HALO:HALO + TM, :] = ca_ref[...] * _sigmoid(cg_ref[...])
        u_s[0:HALO, :] = jnp.where(i > 0, cah_ref[...] * _sigmoid(cgh_ref[...]), 0.0)

        for rc in range(TM // CONV_ROWS):
            acc = jnp.zeros((CONV_ROWS, CONV_W), F32)
            for k in range(CONV_K):
                acc = acc + dy_s[pl.ds(rc * CONV_ROWS + CONV_K - 1 - k, CONV_ROWS), :] * w_ref[k:k + 1, :]
            rows = slice(rc * CONV_ROWS, (rc + 1) * CONV_ROWS)
            sg = _sigmoid(cg_ref[rows, :])
            dca_ref[rows, :] = (acc * sg).astype(BF16)
            dcg_ref[rows, :] = (acc * ca_ref[rows, :] * sg * (1.0 - sg)).astype(BF16)

        for k in range(CONV_K):
            prod = u_s[pl.ds(HALO - (CONV_K - 1) + k, TM), :] * dy_s[0:TM, :]
            dw_ref[k:k + 1, :] += jnp.sum(prod, axis=0, keepdims=True)
        _hosted_wait(i, nt, exchange)

    prev_halo = pl.BlockSpec((HALO, CONV_W), lambda i: (jnp.maximum(i * hpt - 1, 0), 0))
    next_halo = pl.BlockSpec((HALO, CONV_W), lambda i: (jnp.minimum((i + 1) * hpt, nt * hpt - 1), 0))
    vec = jax.ShapeDtypeStruct((1, CONV_W), F32)
    res = _pcall(
        body, name="conv_bwd", grid=(nt,),
        in_specs=[_rows(TM, CONV_W), next_halo, _rows(TM, CONV_W), next_halo,
                  _rows(TM, CONV_W), _rows(TM, CONV_W), prev_halo, prev_halo,
                  _full((CONV_K, CONV_W)), _full((1, CONV_W)), _full((1, CONV_W))] + [ANY] * npart,
        out_specs=[_rows(TM, CONV_W), _rows(TM, CONV_W), _full((32, CONV_W)),
                   _full((1, CONV_W)), _full((1, CONV_W)), _full((1, CONV_W))] + [ANY] * npart,
        out_shape=[jax.ShapeDtypeStruct((R, CONV_W), BF16), jax.ShapeDtypeStruct((R, CONV_W), BF16),
                   jax.ShapeDtypeStruct((32, CONV_W), F32), vec, vec, vec] + _chip_shapes(partials),
        scratch_shapes=_sem_pair(3 * npart)
        + [pltpu.VMEM((HALO + TM, CONV_W), F32), pltpu.VMEM((TM + HALO, CONV_W), F32)],
        semantics=("arbitrary",),
    )(do_conv, do_conv, y, y, ca, cg, ca, cg, conv_w, ln_g, ln_b, *partials)
    return res[:6], res[6:]


def _attn_bwd(q, kv, sinks, o, lse, do, grads):
    R = q.shape[0]
    nb = R // BLOCK
    ng = len(grads)
    scale = 1.0 / math.sqrt(HEAD_DIM)

    def body(sink_ref, q_ref, kvc_ref, kvp_ref, kvm_ref, o_ref, lse_ref, do_ref, *refs):
        g_ins, (dq_ref, dkv_ref, dkvm_ref, dsink_ref) = refs[:ng], refs[ng:ng + 4]
        g_outs, (send_sems, recv_sems, carry_s, cur_s, prev_s) = refs[ng + 4:2 * ng + 4], refs[2 * ng + 4:]
        b = pl.program_id(0)
        exchange = functools.partial(_sibling_copies, g_ins, g_outs, send_sems, recv_sems)
        _hosted(b, nb + 1, exchange)

        @pl.when(b == 0)
        def _():
            dkvm_ref[...] = jnp.zeros_like(dkvm_ref)
            carry_s[...] = jnp.zeros_like(carry_s)
            for h in range(N_HEADS):
                dsink_ref[0, h] = 0.0

        @pl.when(b < nb)
        def _():
            sel, dist, valid, valid_m = _attn_masks(b)
            cur_s[...] = jnp.zeros_like(cur_s)
            prev_s[...] = jnp.zeros_like(prev_s)
            dqs = []
            for h in range(N_HEADS):
                g = h // GROUP
                hs = slice(HEAD_DIM * h, HEAD_DIM * (h + 1))
                ks, vs = slice(HEAD_DIM * g, HEAD_DIM * (g + 1)), slice(KV_W + HEAD_DIM * g, KV_W + HEAD_DIM * (g + 1))
                qh = q_ref[:, hs]
                kc, kp, km = kvc_ref[:, ks], kvp_ref[:, ks], kvm_ref[LEAD:BLOCK, ks]
                vc, vp, vm = kvc_ref[:, vs], kvp_ref[:, vs], kvm_ref[LEAD:BLOCK, vs]
                s_b, s_m = _attn_scores(qh, kc, kp, km, sel, dist, valid, valid_m, 2.0 ** -(h + 1))
                lse_h = lse_ref[:, h:h + 1]
                p_b = jnp.exp(s_b - lse_h)
                p_m = jnp.exp(s_m - lse_h)
                doh = do_ref[:, hs]
                delta = jnp.sum(doh * o_ref[:, hs], axis=-1, keepdims=True)
                dob = doh.astype(BF16)
                dp_b = jnp.where(sel, _dot(dob, vc, NT), _dot(dob, vp, NT))
                ds_b = p_b * (dp_b - delta)
                ds_m = (p_m * (_dot(dob, vm, NT) - delta)).astype(BF16)
                dsink_ref[0, h] += -jnp.sum(jnp.exp(sink_ref[0, h] - lse_h) * delta)
                ds_c = jnp.where(sel, ds_b, 0.0).astype(BF16)
                ds_p = jnp.where(sel, 0.0, ds_b).astype(BF16)
                p_c = jnp.where(sel, p_b, 0.0).astype(BF16)
                p_p = jnp.where(sel, 0.0, p_b).astype(BF16)
                dqs.append(scale * (_dot(ds_c, kc, NN) + _dot(ds_p, kp, NN) + _dot(ds_m, km, NN)))
                cur_s[:, ks] += scale * _dot(ds_c, qh, TN)
                cur_s[:, vs] += _dot(p_c, dob, TN)
                prev_s[:, ks] += scale * _dot(ds_p, qh, TN)
                prev_s[:, vs] += _dot(p_p, dob, TN)
                dkvm_ref[:, ks] += scale * _dot(ds_m, qh, TN)
                dkvm_ref[:, vs] += _dot(p_m.astype(BF16), dob, TN)
            dq_ref[...] = jnp.concatenate(dqs, axis=1).astype(BF16)
            dkv_ref[...] = (carry_s[...] + prev_s[...]).astype(BF16)
            carry_s[...] = cur_s[...]

        @pl.when(b == nb)
        def _():
            dkv_ref[...] = carry_s[...].astype(BF16)

        _hosted_wait(b, nb + 1, exchange)

    def at(off):
        return lambda b: (jnp.clip(b + off, 0, nb - 1), 0)

    blk = lambda cols, off=0: pl.BlockSpec((BLOCK, cols), at(off))
    res = _pcall(
        body, name="attn_bwd", grid=(nb + 1,),
        in_specs=[pl.BlockSpec(memory_space=pltpu.SMEM), blk(512), blk(256), blk(256, -1), _full((BLOCK, 256)),
                  blk(512), blk(N_HEADS), blk(512)] + [ANY] * ng,
        out_specs=[blk(512), blk(256, -1), _full((N_META, 256)), pl.BlockSpec(memory_space=pltpu.SMEM)]
        + [ANY] * ng,
        out_shape=[jax.ShapeDtypeStruct((R, 512), BF16), jax.ShapeDtypeStruct((R, 256), BF16),
                   jax.ShapeDtypeStruct((N_META, 256), F32), jax.ShapeDtypeStruct((1, N_HEADS), F32)]
        + _sibling_shapes(grads),
        scratch_shapes=_sem_pair(ng) + [pltpu.VMEM((BLOCK, 256), F32)] * 3,
        semantics=("arbitrary",),
    )(sinks, q, kv, kv, kv, o, lse, do, *grads)
    return res[:4], res[4:]


def _inproj_bwd(dh1, h0, g1, dq, dkv, dkvm, dca, dcg, w_in_t, partials):
    R = h0.shape[0]
    nt, npart = R // TM, len(partials)

    def body(dh1_ref, h0_ref, g_ref, dq_ref, dkv_ref, dkvm_ref, dca_ref, dcg_ref, w_ref, *refs):
        p_ins, (dh0_ref, dproj_ref, hn_ref, dg_ref) = refs[:npart], refs[npart:npart + 4]
        p_outs, (send_sems, recv_sems) = refs[npart + 4:2 * npart + 4], refs[2 * npart + 4:]
        i = pl.program_id(0)
        exchange = functools.partial(_chip_copies, p_ins, p_outs, send_sems, recv_sems)
        _hosted(i, nt, exchange)

        @pl.when(i == 0)
        def _():
            dg_ref[...] = jnp.zeros_like(dg_ref)

        dproj_ref[:, 0:512] = dq_ref[...]
        dproj_ref[:, 512:768] = dkv_ref[...]
        dproj_ref[:, 768:1280] = dca_ref[...]
        dproj_ref[:, 1280:1792] = dcg_ref[...]

        @pl.when(i == 0)
        def _():
            dproj_ref[LEAD:BLOCK, 512:768] = dkvm_ref[...].astype(BF16)

        dhn = _dot(dproj_ref[...], w_ref[...], NN)
        h = h0_ref[...]
        r = _rms_stats(h)
        g = g_ref[...]
        hn_ref[...] = (h * r * g).astype(BF16)
        dx, dg = _rms_bwd(dhn, h, r, g)
        dh0_ref[...] = dh1_ref[...] + dx
        dg_ref[...] += dg
        _hosted_wait(i, nt, exchange)

    res = _pcall(
        body, name="inproj_bwd", grid=(nt,),
        in_specs=[_rows(TM, D_MODEL), _rows(TM, D_MODEL), _full((1, D_MODEL)), _rows(TM, 512), _rows(TM, 256),
                  _full((N_META, 256)), _rows(TM, 512), _rows(TM, 512), _full((1792, D_MODEL))] + [ANY] * npart,
        out_specs=[_rows(TM, D_MODEL), _rows(TM, 1792), _rows(TM, D_MODEL), _full((1, D_MODEL))] + [ANY] * npart,
        out_shape=[jax.ShapeDtypeStruct((R, D_MODEL), F32), jax.ShapeDtypeStruct((R, 1792), BF16),
                   jax.ShapeDtypeStruct((R, D_MODEL), BF16), jax.ShapeDtypeStruct((1, D_MODEL), F32)]
        + _chip_shapes(partials),
        scratch_shapes=_sem_pair(3 * npart),
        semantics=("arbitrary",),
    )(dh1, h0, g1, dq, dkv, dkvm, dca, dcg, w_in_t, *partials)
    return res[:4], res[4:]


ANY = pl.BlockSpec(memory_space=pl.ANY)


def _position():
    return lax.axis_index("x"), lax.axis_index("y"), lax.axis_index("c")


def _device_number(p):
    return 4 * p[0] + 2 * p[1] + p[2]


def _two_level_allgather(ins, outs, block, send_sems, recv_sems, local_sems, sem_base=0):
    n = len(ins)
    x, y, c = _position()
    me, sibling = (x, y, c), (x, y, 1 - c)
    chips = [(1 - x, y), (x, 1 - y), (1 - x, 1 - y)]

    def copy(w, k, origin, to, src=None):
        return pltpu.make_async_remote_copy(
            src_ref=block(w, origin) if src is None else src, dst_ref=block(w, origin),
            send_sem=send_sems.at[sem_base + 7 * w + k], recv_sem=recv_sems.at[sem_base + 7 * w + k],
            device_id=to, device_id_type=MESH)

    def mine(w):
        return pltpu.make_async_copy(ins[w], block(w, me), local_sems.at[w])

    def own(w):
        return [copy(w, 0, me, sibling, src=ins[w])] + [
            copy(w, 1 + j, me, (*chip, c), src=ins[w]) for j, chip in enumerate(chips)]

    def passed(w):
        return [copy(w, 4 + j, (*chip, c), sibling) for j, chip in enumerate(chips)]

    def start():
        for w in range(n):
            mine(w).start()
        for w in range(n):
            for cp in own(w):
                cp.start()

    def forward(w):
        fw = passed(w)
        for j, chip in enumerate(chips):
            copy(w, 1 + j, (*chip, c), me).wait_recv()
            fw[j].start()

    def finish():
        for w in range(n):
            copy(w, 0, sibling, me).wait_recv()
            for j, chip in enumerate(chips):
                copy(w, 4 + j, (*chip, 1 - c), me).wait_recv()
        for w in range(n):
            for cp in own(w) + passed(w):
                cp.wait_send()
            mine(w).wait()

    return start, forward, finish


def _blocking_allgather(ins, outs, block, send_sems, recv_sems, local_sems, sem_base=0):
    start, forward, finish = _two_level_allgather(ins, outs, block, send_sems, recv_sems, local_sems, sem_base)
    start()
    for w in range(len(ins)):
        forward(w)
    finish()


def _row_block(outs, rows):
    def block(w, p):
        return outs[w].at[pl.ds(pl.multiple_of(_device_number(p) * rows[w], 16), rows[w])]
    return block


def _sibling_copies(ins, outs, send_sems, recv_sems):
    x, y, c = _position()
    return [pltpu.make_async_remote_copy(
        src_ref=ins[w].at[:, 1 - c], dst_ref=outs[w], send_sem=send_sems.at[w], recv_sem=recv_sems.at[w],
        device_id=(x, y, 1 - c), device_id_type=MESH) for w in range(len(ins))]


def _chip_copies(ins, outs, send_sems, recv_sems):
    x, y, c = _position()
    chips = [(1 - x, y), (x, 1 - y), (1 - x, 1 - y)]
    return [pltpu.make_async_remote_copy(
        src_ref=ins[w].at[2 * chip[0] + chip[1]], dst_ref=outs[w].at[k],
        send_sem=send_sems.at[3 * w + k], recv_sem=recv_sems.at[3 * w + k],
        device_id=(*chip, c), device_id_type=MESH) for w in range(len(ins)) for k, chip in enumerate(chips)]


def _hosted(step, n_steps, make_copies):
    @pl.when(step == 0)
    def _():
        for cp in make_copies():
            cp.start()


def _hosted_wait(step, n_steps, make_copies):
    @pl.when(step == n_steps - 1)
    def _():
        for cp in make_copies():
            cp.wait()


def _sem_pair(n):
    return [pltpu.SemaphoreType.DMA((n,)), pltpu.SemaphoreType.DMA((n,))]


def _allgather_params(shards, small):
    arrays = list(shards) + list(small)
    n, ns = len(arrays), len(shards)

    def body(*refs):
        ins, outs = refs[:n], refs[n:2 * n]
        send_sems, recv_sems, local_sems = refs[2 * n:]

        rows = _row_block(outs, [a.shape[0] for a in arrays])

        def block(w, p):
            return rows(w, p) if w < ns else outs[w].at[_device_number(p)]

        _blocking_allgather(ins, outs, block, send_sems, recv_sems, local_sems)

    out_shape = [jax.ShapeDtypeStruct((N_DEV * a.shape[0], a.shape[1]), a.dtype) for a in shards]
    out_shape += [jax.ShapeDtypeStruct((N_DEV,) + a.shape, a.dtype) for a in small]
    return _pcall(
        body, name="allgather_params", in_specs=[ANY] * n, out_specs=[ANY] * n, out_shape=out_shape,
        scratch_shapes=[pltpu.SemaphoreType.DMA((7 * n,)), pltpu.SemaphoreType.DMA((7 * n,)),
                        pltpu.SemaphoreType.DMA((n,))],
    )(*arrays)


def _reduce_siblings(grads):
    n = len(grads)

    def body(*refs):
        ins, outs = refs[:n], refs[n:2 * n]
        send_sems, recv_sems = refs[2 * n:]
        copies = _sibling_copies(ins, outs, send_sems, recv_sems)
        for cp in copies:
            cp.start()
        for cp in copies:
            cp.wait()

    return _pcall(
        body, name="reduce_siblings", in_specs=[ANY] * n, out_specs=[ANY] * n,
        out_shape=_sibling_shapes(grads), scratch_shapes=_sem_pair(n),
    )(*grads)


def _sibling_shapes(grads):
    return [jax.ShapeDtypeStruct((4,) + g.shape[2:], F32) for g in grads]


def _chip_shapes(partials):
    return [jax.ShapeDtypeStruct((3,) + p.shape[1:], F32) for p in partials]


def _add_sibling(grad, received, core, name):
    _, _, r, cols = grad.shape

    def body(core_ref, g_ref, r_ref, o_ref):
        o_ref[...] = g_ref[...] + r_ref[...]

    return pl.pallas_call(
        body, name=name,
        grid_spec=pltpu.PrefetchScalarGridSpec(
            num_scalar_prefetch=1, grid=(4,),
            in_specs=[pl.BlockSpec((None, None, r, cols), lambda s, core_ref: (s, core_ref[0], 0, 0)),
                      pl.BlockSpec((None, r, cols), lambda s, core_ref: (s, 0, 0))],
            out_specs=pl.BlockSpec((None, r, cols), lambda s, core_ref: (s, 0, 0))),
        out_shape=jax.ShapeDtypeStruct((4, r, cols), F32),
        compiler_params=pltpu.CompilerParams(vmem_limit_bytes=VMEM_LIMIT),
    )(core, grad, received)


def _reduce_chips(partials, small):
    n, ns = len(partials), len(small)

    def body(*refs):
        p_ins, s_ins = refs[:n], refs[n:n + ns]
        p_outs, s_outs = refs[n + ns:2 * n + ns], refs[2 * n + ns:2 * (n + ns)]
        send_sems, recv_sems, local_sems = refs[2 * (n + ns):]
        copies = _chip_copies(p_ins, p_outs, send_sems, recv_sems)
        for cp in copies:
            cp.start()
        _blocking_allgather(s_ins, s_outs, lambda w, p: s_outs[w].at[_device_number(p)],
                            send_sems, recv_sems, local_sems, sem_base=3 * n)
        for cp in copies:
            cp.wait()

    out_shape = _chip_shapes(partials)
    out_shape += [jax.ShapeDtypeStruct((N_DEV,) + a.shape, a.dtype) for a in small]
    nsem = 3 * n + 7 * ns
    return _pcall(
        body, name="reduce_chips", in_specs=[ANY] * (n + ns), out_specs=[ANY] * (n + ns), out_shape=out_shape,
        scratch_shapes=[pltpu.SemaphoreType.DMA((nsem,)), pltpu.SemaphoreType.DMA((nsem,)),
                        pltpu.SemaphoreType.DMA((ns,))],
    )(*partials, *small)


def _sum_chips(partial, received, slot, name):
    _, r, cols = partial.shape

    def body(slot_ref, p_ref, r_ref, o_ref):
        o_ref[...] = p_ref[...] + r_ref[0] + r_ref[1] + r_ref[2]

    return pl.pallas_call(
        body, name=name,
        grid_spec=pltpu.PrefetchScalarGridSpec(
            num_scalar_prefetch=1, grid=(1,),
            in_specs=[pl.BlockSpec((None, r, cols), lambda i, slot_ref: (slot_ref[0], 0, 0)),
                      pl.BlockSpec((3, r, cols), lambda i, slot_ref: (0, 0, 0))],
            out_specs=pl.BlockSpec((r, cols), lambda i, slot_ref: (0, 0))),
        out_shape=jax.ShapeDtypeStruct((r, cols), F32),
        compiler_params=pltpu.CompilerParams(vmem_limit_bytes=VMEM_LIMIT),
    )(slot, partial, received)


def _adam(g, w, m, v):
    m = ADAM_B1 * m + (1.0 - ADAM_B1) * g
    v = ADAM_B2 * v + (1.0 - ADAM_B2) * (g * g)
    m_hat = m / (1.0 - ADAM_B1 ** ADAM_STEP)
    v_hat = v / (1.0 - ADAM_B2 ** ADAM_STEP)
    delta = -ADAM_LR * (m_hat / (jnp.sqrt(v_hat) + ADAM_EPS) + ADAM_WD * w)
    return delta, m, v


def _adamw(g, w, m, v, name):
    def body(g_ref, w_ref, m_ref, v_ref, d_ref, nm_ref, nv_ref):
        d_ref[...], nm_ref[...], nv_ref[...] = _adam(g_ref[...], w_ref[...], m_ref[...], v_ref[...])

    out = jax.ShapeDtypeStruct(w.shape, F32)
    return _pcall(body, name=name, out_shape=[out, out, out])(g, w, m, v)


def _adamw_small(dev, ga, gb, gc, params):
    names = ["meta", "attn_norm", "sinks", "conv_w", "conv_b", "ln_g", "ln_b", "attn_out", "conv_out",
             "ffn_norm", "final_norm"]
    flat = [a for p in params for a in p]
    n_in = len(flat)

    def body(dev_ref, ga_ref, gb_ref, gc_ref, *refs):
        ins, outs = refs[:n_in], refs[n_in:n_in + 4 * len(names)]
        sb, sc = refs[n_in + 4 * len(names):]
        a = ga_ref[0]
        sb[...] = gb_ref[0]
        sc[...] = gc_ref[0]
        for d in range(1, N_DEV):
            a = a + ga_ref[d]
            sb[...] += gb_ref[d]
            sc[...] += gc_ref[d]
        dev = dev_ref[0]
        grads = {
            "attn_norm": a[0:1, :], "ffn_norm": a[1:2, :], "final_norm": a[2:3, :],
            "conv_b": a[3:4, 0:512], "ln_g": a[3:4, 512:1024], "ln_b": a[4:5, 0:512],
            "attn_out": a[4:5, 512:1024], "conv_out": a[5:6, 0:512], "sinks": a[5:6, 512:512 + N_HEADS],
            "meta": sb[pl.ds(pl.multiple_of(dev * N_META, N_META), N_META), :],
            "conv_w": sc[pl.ds(pl.multiple_of(dev * 32, 32), 32), :][0:CONV_K, :],
        }
        for idx, nm in enumerate(names):
            w_ref, m_ref, v_ref = ins[3 * idx:3 * idx + 3]
            g = grads[nm]
            delta, m, v = _adam(g, w_ref[...], m_ref[...], v_ref[...])
            o = outs[4 * idx:4 * idx + 4]
            o[0][...], o[1][...], o[2][...], o[3][...] = g, delta, m, v

    vm = pl.BlockSpec(memory_space=pltpu.VMEM)
    out_shape = [jax.ShapeDtypeStruct(p[0].shape, F32) for p in params for _ in range(4)]
    res = pl.pallas_call(
        body, name="adamw_small",
        grid_spec=pltpu.PrefetchScalarGridSpec(
            num_scalar_prefetch=1, grid=(1,),
            in_specs=[pl.BlockSpec(ga.shape, lambda i, d: (0, 0, 0)), pl.BlockSpec(gb.shape, lambda i, d: (0, 0, 0)),
                      pl.BlockSpec(gc.shape, lambda i, d: (0, 0, 0))]
            + [pl.BlockSpec(a.shape, lambda i, d: (0, 0)) for a in flat],
            out_specs=[pl.BlockSpec(s.shape, lambda i, d: (0, 0)) for s in out_shape],
            scratch_shapes=[pltpu.VMEM(gb.shape[1:], F32), pltpu.VMEM(gc.shape[1:], F32)]),
        out_shape=out_shape,
        compiler_params=pltpu.CompilerParams(vmem_limit_bytes=VMEM_LIMIT),
    )(dev, ga, gb, gc, *flat)
    return [res[4 * i:4 * i + 4] for i in range(len(names))]


def kernel(x, meta_tokens, attn_norm_g, w_in, attn_sinks, conv_w, conv_b, conv_ln_g, conv_ln_b, attn_out_g, conv_out_g, w_out, ffn_norm_g, w_gate, w_up, w_down, final_norm_g, loss_target, m_meta_tokens, m_attn_norm_g, m_w_in, m_attn_sinks, m_conv_w, m_conv_b, m_conv_ln_g, m_conv_ln_b, m_attn_out_g, m_conv_out_g, m_w_out, m_ffn_norm_g, m_w_gate, m_w_up, m_w_down, m_final_norm_g, v_meta_tokens, v_attn_norm_g, v_w_in, v_attn_sinks, v_conv_w, v_conv_b, v_conv_ln_g, v_conv_ln_b, v_attn_out_g, v_conv_out_g, v_w_out, v_ffn_norm_g, v_w_gate, v_w_up, v_w_down, v_final_norm_g):
    xi, yi, ci = _position()
    dev = jnp.reshape(_device_number((xi, yi, ci)), (1,)).astype(jnp.int32)
    core = jnp.reshape(ci, (1,)).astype(jnp.int32)
    slot = jnp.reshape(2 * xi + yi, (1,)).astype(jnp.int32)

    w_in_t, meta_st, convw_st = _allgather_params([w_in[0].T.astype(BF16)], [meta_tokens, conv_w[0]])
    later = [w_out[0].astype(BF16), w_gate[0].T.astype(BF16), w_up[0].T.astype(BF16), w_down[0].astype(BF16)]
    meta_full = jnp.transpose(meta_st, (1, 0, 2)).reshape(N_META, D_MODEL)
    convw_full = jnp.transpose(convw_st, (1, 0, 2)).reshape(CONV_K, CONV_W)

    h0 = jnp.concatenate([jnp.zeros((LEAD, D_MODEL), F32), meta_full, x[0]], axis=0)
    final_g = final_norm_g.reshape(1, D_MODEL)

    q, kv, ca, cg = _inproj_fwd(h0, attn_norm_g, w_in_t)
    o_attn, lse, (w_out_b, wg_t, wu_t, wd_b) = _attn_fwd(q, kv, attn_sinks, later)
    o_conv, y_conv = _conv_fwd(ca, cg, convw_full, conv_b, conv_ln_g, conv_ln_b)
    h1 = _outproj_fwd(h0, o_attn, o_conv, attn_out_g, conv_out_g, w_out_b)
    gate, up, dh2, loss_sum, dg_final = _ffn_fwd(h1, ffn_norm_g, wg_t, wu_t, wd_b, final_g, loss_target[0])

    def blocks(g):
        return g.reshape(4, 2, g.shape[0] // N_DEV, D_MODEL)

    def add_siblings(grads, received, tags):
        return [_add_sibling(g, r, core, "add_sibling_" + t) for g, r, t in zip(grads, received, tags)]

    dgate, dup, act, hn2, dh1, dg_ffn = _ffn_bwd(dh2, h1, ffn_norm_g, gate, up, wg_t, wu_t, wd_b)
    ffn_grads = [blocks(_wgrad(dgate, hn2, FF_CHUNK, "wgrad_gate")), blocks(_wgrad(dup, hn2, FF_CHUNK, "wgrad_up")),
                 blocks(_wgrad(act, dh2, FF_CHUNK, "wgrad_down"))]
    (do_attn, do_conv, mixed, dg_ao, dg_co), ffn_sib = _outproj_bwd(
        dh1, o_attn, o_conv, attn_out_g, conv_out_g, w_out_b, ffn_grads)
    ffn_sums = add_siblings(ffn_grads, ffn_sib, ("gate", "up", "down"))
    out_grads = [blocks(_wgrad(mixed, dh1, D_MODEL, "wgrad_out"))]
    (dca, dcg, dconvw, dconvb, dln_g, dln_b), ffn_chips = _conv_bwd(
        do_conv, y_conv, ca, cg, convw_full, conv_ln_g, conv_ln_b, ffn_sums)
    (dq, dkv, dkvm, dsinks), out_sib = _attn_bwd(q, kv, attn_sinks, o_attn, lse, do_attn, out_grads)
    out_sums = add_siblings(out_grads, out_sib, ("out",))
    (dh0, dproj, hn1, dg_attn), out_chips = _inproj_bwd(
        dh1, h0, attn_norm_g, dq, dkv, dkvm, dca, dcg, w_in_t, out_sums)
    in_grads = [blocks(_wgrad(dproj, hn1, 1792, "wgrad_in"))]
    in_sums = add_siblings(in_grads, _reduce_siblings(in_grads), ("in",))
    small_a = jnp.concatenate([
        dg_attn, dg_ffn, dg_final, jnp.concatenate([dconvb, dln_g], axis=1), jnp.concatenate([dln_b, dg_ao], axis=1),
        jnp.concatenate([dg_co, dsinks, jnp.zeros((1, 512 - N_HEADS), F32)], axis=1),
        jnp.zeros((2, D_MODEL), F32)], axis=0)
    small_b = jnp.transpose(dh0[LEAD:BLOCK].reshape(N_META, N_DEV, 128), (1, 0, 2)).reshape(N_DEV * N_META, 128)
    small_c = jnp.transpose(dconvw.reshape(32, N_DEV, 64), (1, 0, 2)).reshape(N_DEV * 32, 64)
    in_chips, ga, gb, gc = _reduce_chips(in_sums, [small_a, small_b, small_c])
    tags = ("in", "out", "gate", "up", "down")
    chip_sums = in_sums + out_sums + ffn_sums
    from_chips = [in_chips] + list(out_chips) + list(ffn_chips)
    g_own = [_sum_chips(p, r, slot, "sum_chips_" + t) for p, r, t in zip(chip_sums, from_chips, tags)]
    g_big = [g_own[0].T, g_own[1], g_own[2].T, g_own[3].T, g_own[4]]

    big = [(w_in, m_w_in, v_w_in), (w_out, m_w_out, v_w_out), (w_gate, m_w_gate, v_w_gate), (w_up, m_w_up, v_w_up),
           (w_down, m_w_down, v_w_down)]
    big_out = {}
    for t, g, (w, m, v) in zip(tags, g_big, big):
        delta, nm, nv = _adamw(g, w[0], m[0], v[0], "adamw_" + t)
        big_out[t] = (g[None], delta[None], nm[None], nv[None])

    small_params = [
        (meta_tokens, m_meta_tokens, v_meta_tokens), (attn_norm_g, m_attn_norm_g, v_attn_norm_g),
        (attn_sinks, m_attn_sinks, v_attn_sinks), (conv_w[0], m_conv_w[0], v_conv_w[0]),
        (conv_b, m_conv_b, v_conv_b), (conv_ln_g, m_conv_ln_g, v_conv_ln_g), (conv_ln_b, m_conv_ln_b, v_conv_ln_b),
        (attn_out_g, m_attn_out_g, v_attn_out_g), (conv_out_g, m_conv_out_g, v_conv_out_g),
        (ffn_norm_g, m_ffn_norm_g, v_ffn_norm_g),
        (final_g, m_final_norm_g.reshape(1, D_MODEL), v_final_norm_g.reshape(1, D_MODEL))]
    sm = _adamw_small(dev, ga, gb, gc, small_params)
    sm[3] = [a[None] for a in sm[3]]
    sm[10] = [a.reshape(D_MODEL) for a in sm[10]]

    per_param = [sm[0], sm[1], big_out["in"], sm[2], sm[3], sm[4], sm[5], sm[6], sm[7], sm[8], big_out["out"],
                 sm[9], big_out["gate"], big_out["up"], big_out["down"], sm[10]]
    loss = lax.psum(loss_sum[0, 0], ("x", "y", "c"))
    grad_x = dh0[BLOCK:][None]
    outs = [loss, grad_x]
    for kind in range(4):
        outs += [p[kind] for p in per_param]
    return tuple(outs)
```

```python
import functools
import math

import jax
import jax.numpy as jnp
from jax import lax
from jax.experimental import pallas as pl
from jax.experimental.pallas import tpu as pltpu

F32, BF16 = jnp.float32, jnp.bfloat16
MESH = pl.DeviceIdType.MESH

D_MODEL = 1024
N_META = 16
BLOCK = 128
LEAD = BLOCK - N_META
HEAD_DIM = 64
N_HEADS = 8
GROUP = 4
ATTN_W = 512
KV_W = 128
CONV_W = 512
CONV_K = 31
HALO = 32
D_FF = 2816
FF_CHUNK = D_FF // 2
FF_SUB = [slice(s, min(s + 256, FF_CHUNK)) for s in range(0, FF_CHUNK, 256)]
N_DEV = 8
EPS = 1e-5
NEG = -1e30
TM = 640
CONV_ROWS = 64
VMEM_LIMIT = 56 * 1024 * 1024

ADAM_LR, ADAM_B1, ADAM_B2, ADAM_EPS, ADAM_WD, ADAM_STEP = 0.001, 0.9, 0.999, 1e-08, 0.01, 10

NT = (((1,), (1,)), ((), ()))
NN = (((1,), (0,)), ((), ()))
TN = (((0,), (0,)), ((), ()))


def _dot(a, b, dims):
    return lax.dot_general(a, b, dims, preferred_element_type=F32)


def _sigmoid(x):
    return 1.0 / (1.0 + jnp.exp(-x))


def _pcall(body, *, name, out_shape, grid=None, in_specs=None, out_specs=None, scratch_shapes=(),
           semantics=None, **kw):
    params = dict(vmem_limit_bytes=VMEM_LIMIT)
    if semantics is not None:
        params["dimension_semantics"] = semantics
    extra = {}
    if grid is not None:
        extra["grid"] = grid
    if in_specs is not None:
        extra["in_specs"] = in_specs
    if out_specs is not None:
        extra["out_specs"] = out_specs
    return pl.pallas_call(body, name=name, out_shape=out_shape, scratch_shapes=list(scratch_shapes),
                          compiler_params=pltpu.CompilerParams(**params), **extra, **kw)


def _rows(tm, cols, off=0):
    return pl.BlockSpec((tm, cols), lambda i, *_: (i + off, 0))


def _full(shape):
    nd = len(shape)
    return pl.BlockSpec(shape, lambda *_: (0,) * nd)


def _rms_stats(x):
    return lax.rsqrt(jnp.mean(x * x, axis=-1, keepdims=True) + EPS)


def _rms_bwd(dy, x, r, g):
    t = dy * g
    dx = r * (t - x * (r * r) * jnp.mean(t * x, axis=-1, keepdims=True))
    dg = jnp.sum(dy * x * r, axis=0, keepdims=True)
    return dx, dg


def _inproj_fwd(h0, g1, w_in_t):
    R = h0.shape[0]

    def body(h_ref, g_ref, w_ref, q_ref, kv_ref, ca_ref, cg_ref):
        h = h_ref[...]
        hn = (h * _rms_stats(h) * g_ref[...]).astype(BF16)
        q_ref[...] = _dot(hn, w_ref[0:512, :], NT).astype(BF16)
        kv_ref[...] = _dot(hn, w_ref[512:768, :], NT).astype(BF16)
        ca_ref[...] = _dot(hn, w_ref[768:1280, :], NT)
        cg_ref[...] = _dot(hn, w_ref[1280:1792, :], NT)

    return _pcall(
        body, name="inproj_fwd", grid=(R // TM,),
        in_specs=[_rows(TM, D_MODEL), _full((1, D_MODEL)), _full((1792, D_MODEL))],
        out_specs=[_rows(TM, 512), _rows(TM, 256), _rows(TM, 512), _rows(TM, 512)],
        out_shape=[jax.ShapeDtypeStruct((R, 512), BF16), jax.ShapeDtypeStruct((R, 256), BF16),
                   jax.ShapeDtypeStruct((R, 512), F32), jax.ShapeDtypeStruct((R, 512), F32)],
        semantics=("parallel",),
    )(h0, g1, w_in_t)


GB = GROUP * BLOCK
ATTN_SCALE = 1.0 / math.sqrt(HEAD_DIM)


def _stack_heads(ref, g):
    return jnp.concatenate(
        [ref[:, HEAD_DIM * (GROUP * g + j):HEAD_DIM * (GROUP * g + j + 1)] for j in range(GROUP)], axis=0)


def _unstack_heads(x):
    return jnp.concatenate([x[BLOCK * j:BLOCK * (j + 1), :] for j in range(GROUP)], axis=1)


def _attn_tables(sink_ref, bias_s, sink_s):
    ii = lax.broadcasted_iota(jnp.int32, (BLOCK, BLOCK), 0)
    jj = lax.broadcasted_iota(jnp.int32, (BLOCK, BLOCK), 1)
    dist = jnp.where(jj <= ii, ii - jj, ii - jj + BLOCK).astype(F32)
    for h in range(N_HEADS):
        bias_s[BLOCK * h:BLOCK * (h + 1), :] = dist * -(2.0 ** -(h + 1))
        sink_s[BLOCK * h:BLOCK * (h + 1), :] = jnp.zeros((BLOCK, 1), F32) + sink_ref[0, h]


def _attn_masks(b):
    ii = lax.broadcasted_iota(jnp.int32, (GB, BLOCK), 0) & (BLOCK - 1)
    jj = lax.broadcasted_iota(jnp.int32, (GB, BLOCK), 1)
    sel = jj <= ii
    pen = jnp.where(sel, jnp.where(b >= 1, 0.0, NEG), jnp.where(b >= 2, 0.0, NEG))
    mi = lax.broadcasted_iota(jnp.int32, (GB, N_META), 0) & (BLOCK - 1)
    mj = lax.broadcasted_iota(jnp.int32, (GB, N_META), 1)
    pen_m = jnp.where((mj + LEAD) <= (mi + b * BLOCK), 0.0, NEG)
    return sel, pen, pen_m


def _attn_scores(qg, kc, kp, km, sel, pen, pen_m, bias):
    s_b = jnp.where(sel, _dot(qg, kc, NT), _dot(qg, kp, NT)) + bias + pen
    s_m = _dot(qg, km, NT) + pen_m
    return s_b, s_m


def _attn_fwd(q, kv, sinks, shards):
    R = q.shape[0]
    nb = R // BLOCK
    ns = len(shards)
    forward_steps = [(w + 1) * (nb - 8) // (ns + 1) for w in range(ns)]

    def body(sink_ref, q_ref, kvc_ref, kvp_ref, kvm_ref, *refs):
        ag_ins, (o_ref, lse_ref), ag_outs = refs[:ns], refs[ns:ns + 2], refs[ns + 2:2 * ns + 2]
        send_sems, recv_sems, local_sems, bias_s, sink_s = refs[2 * ns + 2:]
        b = pl.program_id(0)
        ag_start, ag_forward, ag_finish = _two_level_allgather(
            ag_ins, ag_outs, _row_block(ag_outs, [s.shape[0] for s in shards]), send_sems, recv_sems, local_sems)
        pl.when(b == 0)(ag_start)
        for w, step in enumerate(forward_steps):
            pl.when(b == step)(functools.partial(ag_forward, w))
        pl.when(b == 0)(functools.partial(_attn_tables, sink_ref, bias_s, sink_s))
        sel, pen, pen_m = _attn_masks(b)
        for g in range(N_HEADS // GROUP):
            ks, vs = slice(HEAD_DIM * g, HEAD_DIM * (g + 1)), slice(KV_W + HEAD_DIM * g, KV_W + HEAD_DIM * (g + 1))
            rows = slice(GB * g, GB * (g + 1))
            qg = _stack_heads(q_ref, g) * ATTN_SCALE
            s_b, s_m = _attn_scores(qg, kvc_ref[:, ks], kvp_ref[:, ks], kvm_ref[LEAD:BLOCK, ks],
                                    sel, pen, pen_m, bias_s[rows, :])
            sink = sink_s[rows, :]
            m = jnp.maximum(jnp.maximum(jnp.max(s_b, axis=-1, keepdims=True),
                                        jnp.max(s_m, axis=-1, keepdims=True)), sink)
            p_b = jnp.exp(s_b - m)
            p_m = jnp.exp(s_m - m)
            l = jnp.sum(p_b, axis=-1, keepdims=True) + jnp.sum(p_m, axis=-1, keepdims=True) + jnp.exp(sink - m)
            p_c = jnp.where(sel, p_b, 0.0).astype(BF16)
            p_p = jnp.where(sel, 0.0, p_b).astype(BF16)
            o = (_dot(p_c, kvc_ref[:, vs], NN) + _dot(p_p, kvp_ref[:, vs], NN)
                 + _dot(p_m.astype(BF16), kvm_ref[LEAD:BLOCK, vs], NN))
            o_ref[:, GROUP * HEAD_DIM * g:GROUP * HEAD_DIM * (g + 1)] = _unstack_heads(o / l)
            lse = m + jnp.log(l)
            for j in range(GROUP):
                lse_ref[:, GROUP * g + j:GROUP * g + j + 1] = lse[BLOCK * j:BLOCK * (j + 1), :]
        pl.when(b == nb - 1)(ag_finish)

    res = _pcall(
        body, name="attn_fwd", grid=(nb,),
        in_specs=[pl.BlockSpec(memory_space=pltpu.SMEM),
                  _rows(BLOCK, 512), _rows(BLOCK, 256),
                  pl.BlockSpec((BLOCK, 256), lambda b: (jnp.maximum(b - 1, 0), 0)),
                  _full((BLOCK, 256))] + [ANY] * ns,
        out_specs=[_rows(BLOCK, 512), _rows(BLOCK, N_HEADS)] + [ANY] * ns,
        out_shape=[jax.ShapeDtypeStruct((R, 512), F32), jax.ShapeDtypeStruct((R, N_HEADS), F32)]
        + [jax.ShapeDtypeStruct((N_DEV * s.shape[0], s.shape[1]), s.dtype) for s in shards],
        scratch_shapes=_sem_pair(7 * ns) + [pltpu.SemaphoreType.DMA((ns,)),
                                            pltpu.VMEM((N_HEADS * BLOCK, BLOCK), F32),
                                            pltpu.VMEM((N_HEADS * BLOCK, 1), F32)],
        semantics=("arbitrary",),
    )(sinks, q, kv, kv, kv, *shards)
    return res[0], res[1], res[2:]


def _ln_silu(y, lg, lb):
    mu = jnp.mean(y, axis=-1, keepdims=True)
    xc = y - mu
    rstd = lax.rsqrt(jnp.mean(xc * xc, axis=-1, keepdims=True) + EPS)
    xhat = xc * rstd
    yn = xhat * lg + lb
    return yn, xhat, rstd


def _conv_fwd(ca, cg, conv_w, conv_b, ln_g, ln_b):
    R = ca.shape[0]
    nt = R // TM
    hpt = TM // HALO

    def body(ca_ref, cg_ref, cah_ref, cgh_ref, w_ref, b_ref, lg_ref, lb_ref, oc_ref, y_ref, u_s):
        i = pl.program_id(0)
        u_s[HALO:HALO + TM, :] = ca_ref[...] * _sigmoid(cg_ref[...])
        u_s[0:HALO, :] = jnp.where(i > 0, cah_ref[...] * _sigmoid(cgh_ref[...]), 0.0)
        for rc in range(TM // CONV_ROWS):
            base = rc * CONV_ROWS + HALO - (CONV_K - 1)
            acc = jnp.zeros((CONV_ROWS, CONV_W), F32) + b_ref[...]
            for k in range(CONV_K):
                acc = acc + u_s[pl.ds(base + k, CONV_ROWS), :] * w_ref[k:k + 1, :]
            rows = slice(rc * CONV_ROWS, (rc + 1) * CONV_ROWS)
            y_ref[rows, :] = acc
            yn, _, _ = _ln_silu(acc, lg_ref[...], lb_ref[...])
            oc_ref[rows, :] = yn * _sigmoid(yn)

    prev_halo = pl.BlockSpec((HALO, CONV_W), lambda i: (jnp.maximum(i * hpt - 1, 0), 0))
    return _pcall(
        body, name="conv_fwd", grid=(nt,),
        in_specs=[_rows(TM, CONV_W), _rows(TM, CONV_W), prev_halo, prev_halo,
                  _full((CONV_K, CONV_W)), _full((1, CONV_W)), _full((1, CONV_W)), _full((1, CONV_W))],
        out_specs=[_rows(TM, CONV_W), _rows(TM, CONV_W)],
        out_shape=[jax.ShapeDtypeStruct((R, CONV_W), F32), jax.ShapeDtypeStruct((R, CONV_W), F32)],
        scratch_shapes=[pltpu.VMEM((HALO + TM, CONV_W), F32)],
        semantics=("parallel",),
    )(ca, cg, ca, cg, conv_w, conv_b, ln_g, ln_b)


def _outproj_fwd(h0, o_attn, o_conv, ga, gc, w_out):
    R = h0.shape[0]

    def body(h_ref, oa_ref, oc_ref, ga_ref, gc_ref, w_ref, h1_ref):
        oa, oc = oa_ref[...], oc_ref[...]
        ma = (oa * _rms_stats(oa) * ga_ref[...]).astype(BF16)
        mc = (oc * _rms_stats(oc) * gc_ref[...]).astype(BF16)
        h1_ref[...] = h_ref[...] + _dot(ma, w_ref[0:512, :], NN) + _dot(mc, w_ref[512:1024, :], NN)

    return _pcall(
        body, name="outproj_fwd", grid=(R // TM,),
        in_specs=[_rows(TM, D_MODEL), _rows(TM, 512), _rows(TM, 512), _full((1, 512)), _full((1, 512)),
                  _full((D_MODEL, D_MODEL))],
        out_specs=_rows(TM, D_MODEL),
        out_shape=jax.ShapeDtypeStruct((R, D_MODEL), F32),
        semantics=("parallel",),
    )(h0, o_attn, o_conv, ga, gc, w_out)


def _target_copy(tgt_hbm, tgt_s, sem, i, first):
    if first:
        return pltpu.make_async_copy(tgt_hbm.at[pl.ds(0, TM - BLOCK)], tgt_s.at[pl.ds(BLOCK, TM - BLOCK)], sem)
    return pltpu.make_async_copy(tgt_hbm.at[pl.ds(i * TM - BLOCK, TM)], tgt_s, sem)


def _ffn_fwd(h1, g2, wg_t, wu_t, wd, gf, target):
    R = h1.shape[0]
    nt, nj = R // TM, D_FF // FF_CHUNK

    def body(h1_ref, g2_ref, wg_ref, wu_ref, wd_ref, gf_ref, tgt_hbm,
             gate_ref, up_ref, dh2_ref, loss_ref, dgf_ref, hn_s, acc_s, tgt_s, sem):
        i, j = pl.program_id(0), pl.program_id(1)

        @pl.when((i == 0) & (j == 0))
        def _():
            loss_ref[...] = jnp.zeros_like(loss_ref)
            dgf_ref[...] = jnp.zeros_like(dgf_ref)

        @pl.when(j == 0)
        def _():
            h1 = h1_ref[...]
            hn_s[...] = (h1 * _rms_stats(h1) * g2_ref[...]).astype(BF16)

            @pl.when(i == 0)
            def _():
                tgt_s[0:BLOCK, :] = jnp.zeros((BLOCK, D_MODEL), F32)
                _target_copy(tgt_hbm, tgt_s, sem, i, True).start()

            @pl.when(i > 0)
            def _():
                _target_copy(tgt_hbm, tgt_s, sem, i, False).start()

        hn = hn_s[...]
        part = None
        for cs in FF_SUB:
            gate = _dot(hn, wg_ref[cs, :], NT)
            up = _dot(hn, wu_ref[cs, :], NT)
            gate_ref[:, cs] = gate.astype(BF16)
            up_ref[:, cs] = up.astype(BF16)
            act = (gate * _sigmoid(gate) * up).astype(BF16)
            d = _dot(act, wd_ref[cs, :], NN)
            part = d if part is None else part + d

        @pl.when(j == 0)
        def _():
            acc_s[...] = part

        @pl.when(j == nj - 1)
        def _():
            @pl.when(i == 0)
            def _():
                _target_copy(tgt_hbm, tgt_s, sem, i, True).wait()

            @pl.when(i > 0)
            def _():
                _target_copy(tgt_hbm, tgt_s, sem, i, False).wait()

            h2 = h1_ref[...] + acc_s[...] + part
            rf = _rms_stats(h2)
            gf = gf_ref[...]
            row = lax.broadcasted_iota(jnp.int32, (TM, 1), 0) + i * TM
            err = jnp.where(row >= BLOCK, h2 * rf * gf - tgt_s[...], 0.0)
            dy = err * (1.0 / D_MODEL)
            dh2, dgf = _rms_bwd(dy, h2, rf, gf)
            dh2_ref[...] = dh2
            loss_ref[...] += (0.5 / D_MODEL) * jnp.sum(err * err)
            dgf_ref[...] += dgf

    wspec = pl.BlockSpec((FF_CHUNK, D_MODEL), lambda i, j: (j, 0))
    aspec = pl.BlockSpec((TM, FF_CHUNK), lambda i, j: (i, j))
    return _pcall(
        body, name="ffn_fwd", grid=(nt, nj),
        in_specs=[_rows(TM, D_MODEL), _full((1, D_MODEL)), wspec, wspec, wspec, _full((1, D_MODEL)),
                  pl.BlockSpec(memory_space=pl.ANY)],
        out_specs=[aspec, aspec, _rows(TM, D_MODEL),
                   _full((8, 128)), _full((1, D_MODEL))],
        out_shape=[jax.ShapeDtypeStruct((R, D_FF), BF16), jax.ShapeDtypeStruct((R, D_FF), BF16),
                   jax.ShapeDtypeStruct((R, D_MODEL), F32),
                   jax.ShapeDtypeStruct((8, 128), F32), jax.ShapeDtypeStruct((1, D_MODEL), F32)],
        scratch_shapes=[pltpu.VMEM((TM, D_MODEL), BF16), pltpu.VMEM((TM, D_MODEL), F32),
                        pltpu.VMEM((TM, D_MODEL), F32), pltpu.SemaphoreType.DMA],
        semantics=("arbitrary", "arbitrary"),
    )(h1, g2, wg_t, wu_t, wd, gf, target)


def _ffn_bwd(dh2, h1, g2, gate, up, wg_t, wu_t, wd):
    R = h1.shape[0]
    nt, nj = R // TM, D_FF // FF_CHUNK

    wspec = pl.BlockSpec((FF_CHUNK, D_MODEL), lambda i, j: (j, 0))
    aspec = pl.BlockSpec((TM, FF_CHUNK), lambda i, j: (i, j))
    act_shape = jax.ShapeDtypeStruct((R, D_FF), BF16)

    def act_body(dh2_ref, gate_ref, up_ref, wd_ref, dgate_ref, dup_ref, act_ref, dhb_s):
        @pl.when(pl.program_id(1) == 0)
        def _():
            dhb_s[...] = dh2_ref[...].astype(BF16)

        dhb = dhb_s[...]
        for cs in FF_SUB:
            dact = _dot(dhb, wd_ref[cs, :], NT)
            gate = gate_ref[:, cs].astype(F32)
            up = up_ref[:, cs].astype(F32)
            sig = _sigmoid(gate)
            silu = gate * sig
            dgate_ref[:, cs] = (dact * up * (sig * (1.0 + gate * (1.0 - sig)))).astype(BF16)
            dup_ref[:, cs] = (dact * silu).astype(BF16)
            act_ref[:, cs] = (silu * up).astype(BF16)

    dgate, dup, act = _pcall(
        act_body, name="ffn_bwd_act", grid=(nt, nj),
        in_specs=[_rows(TM, D_MODEL), aspec, aspec, wspec],
        out_specs=[aspec, aspec, aspec], out_shape=[act_shape, act_shape, act_shape],
        scratch_shapes=[pltpu.VMEM((TM, D_MODEL), BF16)],
        semantics=("parallel", "arbitrary"),
    )(dh2, gate, up, wd)

    def in_body(dh2_ref, h1_ref, g2_ref, dgate_ref, dup_ref, wg_ref, wu_ref, hn_ref, dh1_ref, dg2_ref, acc_s):
        i, j = pl.program_id(0), pl.program_id(1)

        @pl.when((i == 0) & (j == 0))
        def _():
            dg2_ref[...] = jnp.zeros_like(dg2_ref)

        part = None
        for cs in FF_SUB:
            d = _dot(dgate_ref[:, cs], wg_ref[cs, :], NN) + _dot(dup_ref[:, cs], wu_ref[cs, :], NN)
            part = d if part is None else part + d

        @pl.when(j == 0)
        def _():
            acc_s[...] = part

        @pl.when(j == nj - 1)
        def _():
            h1 = h1_ref[...]
            r = _rms_stats(h1)
            g2 = g2_ref[...]
            hn_ref[...] = (h1 * r * g2).astype(BF16)
            dx, dg = _rms_bwd(acc_s[...] + part, h1, r, g2)
            dh1_ref[...] = dh2_ref[...] + dx
            dg2_ref[...] += dg

    hn2, dh1, dg2 = _pcall(
        in_body, name="ffn_bwd_in", grid=(nt, nj),
        in_specs=[_rows(TM, D_MODEL), _rows(TM, D_MODEL), _full((1, D_MODEL)), aspec, aspec, wspec, wspec],
        out_specs=[_rows(TM, D_MODEL), _rows(TM, D_MODEL), _full((1, D_MODEL))],
        out_shape=[jax.ShapeDtypeStruct((R, D_MODEL), BF16), jax.ShapeDtypeStruct((R, D_MODEL), F32),
                   jax.ShapeDtypeStruct((1, D_MODEL), F32)],
        scratch_shapes=[pltpu.VMEM((TM, D_MODEL), F32)],
        semantics=("arbitrary", "arbitrary"),
    )(dh2, h1, g2, dgate, dup, wg_t, wu_t)
    return dgate, dup, act, hn2, dh1, dg2


def _wgrad(a, b, tm, name):
    K, M = a.shape
    N = b.shape[1]

    def body(a_ref, b_ref, o_ref):
        @pl.when(pl.program_id(1) == 0)
        def _():
            o_ref[...] = jnp.zeros_like(o_ref)

        o_ref[...] += _dot(a_ref[...], b_ref[...].astype(BF16), TN)

    return _pcall(
        body, name=name, grid=(M // tm, K // TM),
        in_specs=[pl.BlockSpec((TM, tm), lambda m, k: (k, m)), pl.BlockSpec((TM, N), lambda m, k: (k, 0))],
        out_specs=pl.BlockSpec((tm, N), lambda m, k: (m, 0)),
        out_shape=jax.ShapeDtypeStruct((M, N), F32),
        semantics=("parallel", "arbitrary"),
    )(a, b)


def _outproj_bwd(dh1, o_attn, o_conv, ga, gc, w_out, grads):
    R = dh1.shape[0]
    nt, ng = R // TM, len(grads)

    def body(dh1_ref, oa_ref, oc_ref, ga_ref, gc_ref, w_ref, *refs):
        g_ins, (doa_ref, doc_ref, mixed_ref, dga_ref, dgc_ref) = refs[:ng], refs[ng:ng + 5]
        g_outs, (send_sems, recv_sems) = refs[ng + 5:2 * ng + 5], refs[2 * ng + 5:]
        exchange = functools.partial(_sibling_copies, g_ins, g_outs, send_sems, recv_sems)
        _hosted(pl.program_id(0), nt, exchange)

        @pl.when(pl.program_id(0) == 0)
        def _():
            dga_ref[...] = jnp.zeros_like(dga_ref)
            dgc_ref[...] = jnp.zeros_like(dgc_ref)

        dm = _dot(dh1_ref[...].astype(BF16), w_ref[...], NT)
        oa, oc = oa_ref[...], oc_ref[...]
        ra, rc = _rms_stats(oa), _rms_stats(oc)
        mixed_ref[:, 0:512] = (oa * ra * ga_ref[...]).astype(BF16)
        mixed_ref[:, 512:1024] = (oc * rc * gc_ref[...]).astype(BF16)
        doa, dga = _rms_bwd(dm[:, 0:512], oa, ra, ga_ref[...])
        doc, dgc = _rms_bwd(dm[:, 512:1024], oc, rc, gc_ref[...])
        doa_ref[...] = doa
        doc_ref[...] = doc
        dga_ref[...] += dga
        dgc_ref[...] += dgc
        _hosted_wait(pl.program_id(0), nt, exchange)

    res = _pcall(
        body, name="outproj_bwd", grid=(nt,),
        in_specs=[_rows(TM, D_MODEL), _rows(TM, 512), _rows(TM, 512), _full((1, 512)), _full((1, 512)),
                  _full((D_MODEL, D_MODEL))] + [ANY] * ng,
        out_specs=[_rows(TM, 512), _rows(TM, 512), _rows(TM, D_MODEL), _full((1, 512)), _full((1, 512))]
        + [ANY] * ng,
        out_shape=[jax.ShapeDtypeStruct((R, 512), F32), jax.ShapeDtypeStruct((R, 512), F32),
                   jax.ShapeDtypeStruct((R, D_MODEL), BF16),
                   jax.ShapeDtypeStruct((1, 512), F32), jax.ShapeDtypeStruct((1, 512), F32)]
        + _sibling_shapes(grads),
        scratch_shapes=_sem_pair(ng),
        semantics=("arbitrary",),
    )(dh1, o_attn, o_conv, ga, gc, w_out, *grads)
    return res[:5], res[5:]


def _conv_bwd(do_conv, y, ca, cg, conv_w, ln_g, ln_b, partials):
    R = ca.shape[0]
    nt = R // TM
    hpt = TM // HALO
    npart = len(partials)

    def body(do_ref, doh_ref, y_ref, yh_ref, ca_ref, cg_ref, cah_ref, cgh_ref, w_ref, lg_ref, lb_ref, *refs):
        p_ins, (dca_ref, dcg_ref, dw_ref, db_ref, dlg_ref, dlb_ref) = refs[:npart], refs[npart:npart + 6]
        p_outs, (send_sems, recv_sems, u_s, dy_s) = refs[npart + 6:2 * npart + 6], refs[2 * npart + 6:]
        i = pl.program_id(0)
        exchange = functools.partial(_chip_copies, p_ins, p_outs, send_sems, recv_sems)
        _hosted(i, nt, exchange)

        @pl.when(i == 0)
        def _():
            dw_ref[...] = jnp.zeros_like(dw_ref)
            db_ref[...] = jnp.zeros_like(db_ref)
            dlg_ref[...] = jnp.zeros_like(dlg_ref)
            dlb_ref[...] = jnp.zeros_like(dlb_ref)

        lg, lb = lg_ref[...], lb_ref[...]

        def ln_bwd(yv, dov):
            yn, xhat, rstd = _ln_silu(yv, lg, lb)
            sig = _sigmoid(yn)
            dyn = dov * (sig * (1.0 + yn * (1.0 - sig)))
            dxh = dyn * lg
            dyv = rstd * (dxh - jnp.mean(dxh, axis=-1, keepdims=True)
                          - xhat * jnp.mean(dxh * xhat, axis=-1, keepdims=True))
            return dyv, dyn, xhat

        dyv, dyn, xhat = ln_bwd(y_ref[...], do_ref[...])
        dy_s[0:TM, :] = dyv
        dlg_ref[...] += jnp.sum(dyn * xhat, axis=0, keepdims=True)
        dlb_ref[...] += jnp.sum(dyn, axis=0, keepdims=True)
        db_ref[...] += jnp.sum(dyv, axis=0, keepdims=True)
        dyh, _, _ = ln_bwd(yh_ref[...], doh_ref[...])
        dy_s[TM:TM + HALO, :] = jnp.where(i < nt - 1, dyh, 0.0)
        u_s[HALO:HALO + TM, :] = ca_ref[...] * _sigmoid(cg_ref[...])
        u_s[0:HALO, :] = jnp.where(i > 0, cah_ref[...] * _sigmoid(cgh_ref[...]), 0.0)

        for rc in range(TM // CONV_ROWS):
            acc = jnp.zeros((CONV_ROWS, CONV_W), F32)
            for k in range(CONV_K):
                acc = acc + dy_s[pl.ds(rc * CONV_ROWS + CONV_K - 1 - k, CONV_ROWS), :] * w_ref[k:k + 1, :]
            rows = slice(rc * CONV_ROWS, (rc + 1) * CONV_ROWS)
            sg = _sigmoid(cg_ref[rows, :])
            dca_ref[rows, :] = (acc * sg).astype(BF16)
            dcg_ref[rows, :] = (acc * ca_ref[rows, :] * sg * (1.0 - sg)).astype(BF16)

        for k in range(CONV_K):
            prod = u_s[pl.ds(HALO - (CONV_K - 1) + k, TM), :] * dy_s[0:TM, :]
            dw_ref[k:k + 1, :] += jnp.sum(prod, axis=0, keepdims=True)
        _hosted_wait(i, nt, exchange)

    prev_halo = pl.BlockSpec((HALO, CONV_W), lambda i: (jnp.maximum(i * hpt - 1, 0), 0))
    next_halo = pl.BlockSpec((HALO, CONV_W), lambda i: (jnp.minimum((i + 1) * hpt, nt * hpt - 1), 0))
    vec = jax.ShapeDtypeStruct((1, CONV_W), F32)
    res = _pcall(
        body, name="conv_bwd", grid=(nt,),
        in_specs=[_rows(TM, CONV_W), next_halo, _rows(TM, CONV_W), next_halo,
                  _rows(TM, CONV_W), _rows(TM, CONV_W), prev_halo, prev_halo,
                  _full((CONV_K, CONV_W)), _full((1, CONV_W)), _full((1, CONV_W))] + [ANY] * npart,
        out_specs=[_rows(TM, CONV_W), _rows(TM, CONV_W), _full((32, CONV_W)),
                   _full((1, CONV_W)), _full((1, CONV_W)), _full((1, CONV_W))] + [ANY] * npart,
        out_shape=[jax.ShapeDtypeStruct((R, CONV_W), BF16), jax.ShapeDtypeStruct((R, CONV_W), BF16),
                   jax.ShapeDtypeStruct((32, CONV_W), F32), vec, vec, vec] + _chip_shapes(partials),
        scratch_shapes=_sem_pair(3 * npart)
        + [pltpu.VMEM((HALO + TM, CONV_W), F32), pltpu.VMEM((TM + HALO, CONV_W), F32)],
        semantics=("arbitrary",),
    )(do_conv, do_conv, y, y, ca, cg, ca, cg, conv_w, ln_g, ln_b, *partials)
    return res[:6], res[6:]


def _attn_bwd(q, kv, sinks, o, lse, do, grads):
    R = q.shape[0]
    nb = R // BLOCK
    ng = len(grads)

    def body(sink_ref, q_ref, kvc_ref, kvp_ref, kvm_ref, o_ref, lse_ref, do_ref, *refs):
        g_ins, (dq_ref, dkv_ref, dkvm_ref, dsink_ref) = refs[:ng], refs[ng:ng + 4]
        g_outs = refs[ng + 4:2 * ng + 4]
        send_sems, recv_sems, carry_s, cur_s, prev_s, bias_s, sink_s = refs[2 * ng + 4:]
        b = pl.program_id(0)
        exchange = functools.partial(_sibling_copies, g_ins, g_outs, send_sems, recv_sems)
        _hosted(b, nb + 1, exchange)

        @pl.when(b == 0)
        def _():
            dkvm_ref[...] = jnp.zeros_like(dkvm_ref)
            carry_s[...] = jnp.zeros_like(carry_s)
            for h in range(N_HEADS):
                dsink_ref[0, h] = 0.0
            _attn_tables(sink_ref, bias_s, sink_s)

        @pl.when(b < nb)
        def _():
            sel, pen, pen_m = _attn_masks(b)
            for g in range(N_HEADS // GROUP):
                ks, vs = slice(HEAD_DIM * g, HEAD_DIM * (g + 1)), slice(KV_W + HEAD_DIM * g, KV_W + HEAD_DIM * (g + 1))
                rows = slice(GB * g, GB * (g + 1))
                qg = _stack_heads(q_ref, g) * ATTN_SCALE
                kc, kp, km = kvc_ref[:, ks], kvp_ref[:, ks], kvm_ref[LEAD:BLOCK, ks]
                vc, vp, vm = kvc_ref[:, vs], kvp_ref[:, vs], kvm_ref[LEAD:BLOCK, vs]
                s_b, s_m = _attn_scores(qg, kc, kp, km, sel, pen, pen_m, bias_s[rows, :])
                lse = jnp.concatenate(
                    [lse_ref[:, GROUP * g + j:GROUP * g + j + 1] for j in range(GROUP)], axis=0)
                p_b = jnp.exp(s_b - lse)
                p_m = jnp.exp(s_m - lse)
                dog = _stack_heads(do_ref, g)
                delta = jnp.sum(dog * _stack_heads(o_ref, g), axis=-1, keepdims=True)
                dob = dog.astype(BF16)
                dp_b = jnp.where(sel, _dot(dob, vc, NT), _dot(dob, vp, NT))
                ds_b = p_b * (dp_b - delta)
                ds_m = (p_m * (_dot(dob, vm, NT) - delta)).astype(BF16)
                dsk = jnp.exp(sink_s[rows, :] - lse) * delta
                for j in range(GROUP):
                    dsink_ref[0, GROUP * g + j] += -jnp.sum(dsk[BLOCK * j:BLOCK * (j + 1), :])
                ds_c = jnp.where(sel, ds_b, 0.0).astype(BF16)
                ds_p = jnp.where(sel, 0.0, ds_b).astype(BF16)
                p_c = jnp.where(sel, p_b, 0.0).astype(BF16)
                p_p = jnp.where(sel, 0.0, p_b).astype(BF16)
                dq = (_dot(ds_c, kc, NN) + _dot(ds_p, kp, NN) + _dot(ds_m, km, NN)) * ATTN_SCALE
                dq_ref[:, GROUP * HEAD_DIM * g:GROUP * HEAD_DIM * (g + 1)] = _unstack_heads(dq).astype(BF16)
                cur_s[:, ks] = _dot(ds_c, qg, TN)
                cur_s[:, vs] = _dot(p_c, dob, TN)
                prev_s[:, ks] = _dot(ds_p, qg, TN)
                prev_s[:, vs] = _dot(p_p, dob, TN)
                dkvm_ref[:, ks] += _dot(ds_m, qg, TN)
                dkvm_ref[:, vs] += _dot(p_m.astype(BF16), dob, TN)
            dkv_ref[...] = (carry_s[...] + prev_s[...]).astype(BF16)
            carry_s[...] = cur_s[...]

        @pl.when(b == nb)
        def _():
            dkv_ref[...] = carry_s[...].astype(BF16)

        _hosted_wait(b, nb + 1, exchange)

    def at(off):
        return lambda b: (jnp.clip(b + off, 0, nb - 1), 0)

    blk = lambda cols, off=0: pl.BlockSpec((BLOCK, cols), at(off))
    res = _pcall(
        body, name="attn_bwd", grid=(nb + 1,),
        in_specs=[pl.BlockSpec(memory_space=pltpu.SMEM), blk(512), blk(256), blk(256, -1), _full((BLOCK, 256)),
                  blk(512), blk(N_HEADS), blk(512)] + [ANY] * ng,
        out_specs=[blk(512), blk(256, -1), _full((N_META, 256)), pl.BlockSpec(memory_space=pltpu.SMEM)]
        + [ANY] * ng,
        out_shape=[jax.ShapeDtypeStruct((R, 512), BF16), jax.ShapeDtypeStruct((R, 256), BF16),
                   jax.ShapeDtypeStruct((N_META, 256), F32), jax.ShapeDtypeStruct((1, N_HEADS), F32)]
        + _sibling_shapes(grads),
        scratch_shapes=_sem_pair(ng) + [pltpu.VMEM((BLOCK, 256), F32)] * 3
        + [pltpu.VMEM((N_HEADS * BLOCK, BLOCK), F32), pltpu.VMEM((N_HEADS * BLOCK, 1), F32)],
        semantics=("arbitrary",),
    )(sinks, q, kv, kv, kv, o, lse, do, *grads)
    return res[:4], res[4:]


def _inproj_bwd(dh1, h0, g1, dq, dkv, dkvm, dca, dcg, w_in_t, partials):
    R = h0.shape[0]
    nt, npart = R // TM, len(partials)

    def body(dh1_ref, h0_ref, g_ref, dq_ref, dkv_ref, dkvm_ref, dca_ref, dcg_ref, w_ref, *refs):
        p_ins, (dh0_ref, dproj_ref, hn_ref, dg_ref) = refs[:npart], refs[npart:npart + 4]
        p_outs, (send_sems, recv_sems) = refs[npart + 4:2 * npart + 4], refs[2 * npart + 4:]
        i = pl.program_id(0)
        exchange = functools.partial(_chip_copies, p_ins, p_outs, send_sems, recv_sems)
        _hosted(i, nt, exchange)

        @pl.when(i == 0)
        def _():
            dg_ref[...] = jnp.zeros_like(dg_ref)

        dproj_ref[:, 0:512] = dq_ref[...]
        dproj_ref[:, 512:768] = dkv_ref[...]
        dproj_ref[:, 768:1280] = dca_ref[...]
        dproj_ref[:, 1280:1792] = dcg_ref[...]

        @pl.when(i == 0)
        def _():
            dproj_ref[LEAD:BLOCK, 512:768] = dkvm_ref[...].astype(BF16)

        dhn = _dot(dproj_ref[...], w_ref[...], NN)
        h = h0_ref[...]
        r = _rms_stats(h)
        g = g_ref[...]
        hn_ref[...] = (h * r * g).astype(BF16)
        dx, dg = _rms_bwd(dhn, h, r, g)
        dh0_ref[...] = dh1_ref[...] + dx
        dg_ref[...] += dg
        _hosted_wait(i, nt, exchange)

    res = _pcall(
        body, name="inproj_bwd", grid=(nt,),
        in_specs=[_rows(TM, D_MODEL), _rows(TM, D_MODEL), _full((1, D_MODEL)), _rows(TM, 512), _rows(TM, 256),
                  _full((N_META, 256)), _rows(TM, 512), _rows(TM, 512), _full((1792, D_MODEL))] + [ANY] * npart,
        out_specs=[_rows(TM, D_MODEL), _rows(TM, 1792), _rows(TM, D_MODEL), _full((1, D_MODEL))] + [ANY] * npart,
        out_shape=[jax.ShapeDtypeStruct((R, D_MODEL), F32), jax.ShapeDtypeStruct((R, 1792), BF16),
                   jax.ShapeDtypeStruct((R, D_MODEL), BF16), jax.ShapeDtypeStruct((1, D_MODEL), F32)]
        + _chip_shapes(partials),
        scratch_shapes=_sem_pair(3 * npart),
        semantics=("arbitrary",),
    )(dh1, h0, g1, dq, dkv, dkvm, dca, dcg, w_in_t, *partials)
    return res[:4], res[4:]


ANY = pl.BlockSpec(memory_space=pl.ANY)


def _position():
    return lax.axis_index("x"), lax.axis_index("y"), lax.axis_index("c")


def _device_number(p):
    return 4 * p[0] + 2 * p[1] + p[2]


def _two_level_allgather(ins, outs, block, send_sems, recv_sems, local_sems, sem_base=0):
    n = len(ins)
    x, y, c = _position()
    me, sibling = (x, y, c), (x, y, 1 - c)
    chips = [(1 - x, y), (x, 1 - y), (1 - x, 1 - y)]

    def copy(w, k, origin, to, src=None):
        return pltpu.make_async_remote_copy(
            src_ref=block(w, origin) if src is None else src, dst_ref=block(w, origin),
            send_sem=send_sems.at[sem_base + 7 * w + k], recv_sem=recv_sems.at[sem_base + 7 * w + k],
            device_id=to, device_id_type=MESH)

    def mine(w):
        return pltpu.make_async_copy(ins[w], block(w, me), local_sems.at[w])

    def own(w):
        return [copy(w, 0, me, sibling, src=ins[w])] + [
            copy(w, 1 + j, me, (*chip, c), src=ins[w]) for j, chip in enumerate(chips)]

    def passed(w):
        return [copy(w, 4 + j, (*chip, c), sibling) for j, chip in enumerate(chips)]

    def start():
        for w in range(n):
            mine(w).start()
        for w in range(n):
            for cp in own(w):
                cp.start()

    def forward(w):
        fw = passed(w)
        for j, chip in enumerate(chips):
            copy(w, 1 + j, (*chip, c), me).wait_recv()
            fw[j].start()

    def finish():
        for w in range(n):
            copy(w, 0, sibling, me).wait_recv()
            for j, chip in enumerate(chips):
                copy(w, 4 + j, (*chip, 1 - c), me).wait_recv()
        for w in range(n):
            for cp in own(w) + passed(w):
                cp.wait_send()
            mine(w).wait()

    return start, forward, finish


def _blocking_allgather(ins, outs, block, send_sems, recv_sems, local_sems, sem_base=0):
    start, forward, finish = _two_level_allgather(ins, outs, block, send_sems, recv_sems, local_sems, sem_base)
    start()
    for w in range(len(ins)):
        forward(w)
    finish()


def _row_block(outs, rows):
    def block(w, p):
        return outs[w].at[pl.ds(pl.multiple_of(_device_number(p) * rows[w], 16), rows[w])]
    return block


def _sibling_copies(ins, outs, send_sems, recv_sems):
    x, y, c = _position()
    return [pltpu.make_async_remote_copy(
        src_ref=ins[w].at[:, 1 - c], dst_ref=outs[w], send_sem=send_sems.at[w], recv_sem=recv_sems.at[w],
        device_id=(x, y, 1 - c), device_id_type=MESH) for w in range(len(ins))]


def _chip_copies(ins, outs, send_sems, recv_sems):
    x, y, c = _position()
    chips = [(1 - x, y), (x, 1 - y), (1 - x, 1 - y)]
    return [pltpu.make_async_remote_copy(
        src_ref=ins[w].at[2 * chip[0] + chip[1]], dst_ref=outs[w].at[k],
        send_sem=send_sems.at[3 * w + k], recv_sem=recv_sems.at[3 * w + k],
        device_id=(*chip, c), device_id_type=MESH) for w in range(len(ins)) for k, chip in enumerate(chips)]


def _hosted(step, n_steps, make_copies):
    @pl.when(step == 0)
    def _():
        for cp in make_copies():
            cp.start()


def _hosted_wait(step, n_steps, make_copies):
    @pl.when(step == n_steps - 1)
    def _():
        for cp in make_copies():
            cp.wait()


def _sem_pair(n):
    return [pltpu.SemaphoreType.DMA((n,)), pltpu.SemaphoreType.DMA((n,))]


def _allgather_params(shards, small):
    arrays = list(shards) + list(small)
    n, ns = len(arrays), len(shards)

    def body(*refs):
        ins, outs = refs[:n], refs[n:2 * n]
        send_sems, recv_sems, local_sems = refs[2 * n:]

        rows = _row_block(outs, [a.shape[0] for a in arrays])

        def block(w, p):
            return rows(w, p) if w < ns else outs[w].at[_device_number(p)]

        _blocking_allgather(ins, outs, block, send_sems, recv_sems, local_sems)

    out_shape = [jax.ShapeDtypeStruct((N_DEV * a.shape[0], a.shape[1]), a.dtype) for a in shards]
    out_shape += [jax.ShapeDtypeStruct((N_DEV,) + a.shape, a.dtype) for a in small]
    return _pcall(
        body, name="allgather_params", in_specs=[ANY] * n, out_specs=[ANY] * n, out_shape=out_shape,
        scratch_shapes=[pltpu.SemaphoreType.DMA((7 * n,)), pltpu.SemaphoreType.DMA((7 * n,)),
                        pltpu.SemaphoreType.DMA((n,))],
    )(*arrays)


def _reduce_siblings(grads):
    n = len(grads)

    def body(*refs):
        ins, outs = refs[:n], refs[n:2 * n]
        send_sems, recv_sems = refs[2 * n:]
        copies = _sibling_copies(ins, outs, send_sems, recv_sems)
        for cp in copies:
            cp.start()
        for cp in copies:
            cp.wait()

    return _pcall(
        body, name="reduce_siblings", in_specs=[ANY] * n, out_specs=[ANY] * n,
        out_shape=_sibling_shapes(grads), scratch_shapes=_sem_pair(n),
    )(*grads)


def _sibling_shapes(grads):
    return [jax.ShapeDtypeStruct((4,) + g.shape[2:], F32) for g in grads]


def _chip_shapes(partials):
    return [jax.ShapeDtypeStruct((3,) + p.shape[1:], F32) for p in partials]


def _add_sibling(grad, received, core, name):
    _, _, r, cols = grad.shape

    def body(core_ref, g_ref, r_ref, o_ref):
        o_ref[...] = g_ref[...] + r_ref[...]

    return pl.pallas_call(
        body, name=name,
        grid_spec=pltpu.PrefetchScalarGridSpec(
            num_scalar_prefetch=1, grid=(4,),
            in_specs=[pl.BlockSpec((None, None, r, cols), lambda s, core_ref: (s, core_ref[0], 0, 0)),
                      pl.BlockSpec((None, r, cols), lambda s, core_ref: (s, 0, 0))],
            out_specs=pl.BlockSpec((None, r, cols), lambda s, core_ref: (s, 0, 0))),
        out_shape=jax.ShapeDtypeStruct((4, r, cols), F32),
        compiler_params=pltpu.CompilerParams(vmem_limit_bytes=VMEM_LIMIT),
    )(core, grad, received)


def _reduce_chips(partials, small):
    n, ns = len(partials), len(small)

    def body(*refs):
        p_ins, s_ins = refs[:n], refs[n:n + ns]
        p_outs, s_outs = refs[n + ns:2 * n + ns], refs[2 * n + ns:2 * (n + ns)]
        send_sems, recv_sems, local_sems = refs[2 * (n + ns):]
        copies = _chip_copies(p_ins, p_outs, send_sems, recv_sems)
        for cp in copies:
            cp.start()
        _blocking_allgather(s_ins, s_outs, lambda w, p: s_outs[w].at[_device_number(p)],
                            send_sems, recv_sems, local_sems, sem_base=3 * n)
        for cp in copies:
            cp.wait()

    out_shape = _chip_shapes(partials)
    out_shape += [jax.ShapeDtypeStruct((N_DEV,) + a.shape, a.dtype) for a in small]
    nsem = 3 * n + 7 * ns
    return _pcall(
        body, name="reduce_chips", in_specs=[ANY] * (n + ns), out_specs=[ANY] * (n + ns), out_shape=out_shape,
        scratch_shapes=[pltpu.SemaphoreType.DMA((nsem,)), pltpu.SemaphoreType.DMA((nsem,)),
                        pltpu.SemaphoreType.DMA((ns,))],
    )(*partials, *small)


def _sum_chips(partial, received, slot, name):
    _, r, cols = partial.shape

    def body(slot_ref, p_ref, r_ref, o_ref):
        o_ref[...] = p_ref[...] + r_ref[0] + r_ref[1] + r_ref[2]

    return pl.pallas_call(
        body, name=name,
        grid_spec=pltpu.PrefetchScalarGridSpec(
            num_scalar_prefetch=1, grid=(1,),
            in_specs=[pl.BlockSpec((None, r, cols), lambda i, slot_ref: (slot_ref[0], 0, 0)),
                      pl.BlockSpec((3, r, cols), lambda i, slot_ref: (0, 0, 0))],
            out_specs=pl.BlockSpec((r, cols), lambda i, slot_ref: (0, 0))),
        out_shape=jax.ShapeDtypeStruct((r, cols), F32),
        compiler_params=pltpu.CompilerParams(vmem_limit_bytes=VMEM_LIMIT),
    )(slot, partial, received)


def _adam(g, w, m, v):
    m = ADAM_B1 * m + (1.0 - ADAM_B1) * g
    v = ADAM_B2 * v + (1.0 - ADAM_B2) * (g * g)
    m_hat = m / (1.0 - ADAM_B1 ** ADAM_STEP)
    v_hat = v / (1.0 - ADAM_B2 ** ADAM_STEP)
    delta = -ADAM_LR * (m_hat / (jnp.sqrt(v_hat) + ADAM_EPS) + ADAM_WD * w)
    return delta, m, v


def _adamw(g, w, m, v, name):
    def body(g_ref, w_ref, m_ref, v_ref, d_ref, nm_ref, nv_ref):
        d_ref[...], nm_ref[...], nv_ref[...] = _adam(g_ref[...], w_ref[...], m_ref[...], v_ref[...])

    out = jax.ShapeDtypeStruct(w.shape, F32)
    return _pcall(body, name=name, out_shape=[out, out, out])(g, w, m, v)


def _adamw_small(dev, ga, gb, gc, params):
    names = ["meta", "attn_norm", "sinks", "conv_w", "conv_b", "ln_g", "ln_b", "attn_out", "conv_out",
             "ffn_norm", "final_norm"]
    flat = [a for p in params for a in p]
    n_in = len(flat)

    def body(dev_ref, ga_ref, gb_ref, gc_ref, *refs):
        ins, outs = refs[:n_in], refs[n_in:n_in + 4 * len(names)]
        sb, sc = refs[n_in + 4 * len(names):]
        a = ga_ref[0]
        sb[...] = gb_ref[0]
        sc[...] = gc_ref[0]
        for d in range(1, N_DEV):
            a = a + ga_ref[d]
            sb[...] += gb_ref[d]
            sc[...] += gc_ref[d]
        dev = dev_ref[0]
        grads = {
            "attn_norm": a[0:1, :], "ffn_norm": a[1:2, :], "final_norm": a[2:3, :],
            "conv_b": a[3:4, 0:512], "ln_g": a[3:4, 512:1024], "ln_b": a[4:5, 0:512],
            "attn_out": a[4:5, 512:1024], "conv_out": a[5:6, 0:512], "sinks": a[5:6, 512:512 + N_HEADS],
            "meta": sb[pl.ds(pl.multiple_of(dev * N_META, N_META), N_META), :],
            "conv_w": sc[pl.ds(pl.multiple_of(dev * 32, 32), 32), :][0:CONV_K, :],
        }
        for idx, nm in enumerate(names):
            w_ref, m_ref, v_ref = ins[3 * idx:3 * idx + 3]
            g = grads[nm]
            delta, m, v = _adam(g, w_ref[...], m_ref[...], v_ref[...])
            o = outs[4 * idx:4 * idx + 4]
            o[0][...], o[1][...], o[2][...], o[3][...] = g, delta, m, v

    vm = pl.BlockSpec(memory_space=pltpu.VMEM)
    out_shape = [jax.ShapeDtypeStruct(p[0].shape, F32) for p in params for _ in range(4)]
    res = pl.pallas_call(
        body, name="adamw_small",
        grid_spec=pltpu.PrefetchScalarGridSpec(
            num_scalar_prefetch=1, grid=(1,),
            in_specs=[pl.BlockSpec(ga.shape, lambda i, d: (0, 0, 0)), pl.BlockSpec(gb.shape, lambda i, d: (0, 0, 0)),
                      pl.BlockSpec(gc.shape, lambda i, d: (0, 0, 0))]
            + [pl.BlockSpec(a.shape, lambda i, d: (0, 0)) for a in flat],
            out_specs=[pl.BlockSpec(s.shape, lambda i, d: (0, 0)) for s in out_shape],
            scratch_shapes=[pltpu.VMEM(gb.shape[1:], F32), pltpu.VMEM(gc.shape[1:], F32)]),
        out_shape=out_shape,
        compiler_params=pltpu.CompilerParams(vmem_limit_bytes=VMEM_LIMIT),
    )(dev, ga, gb, gc, *flat)
    return [res[4 * i:4 * i + 4] for i in range(len(names))]


def kernel(x, meta_tokens, attn_norm_g, w_in, attn_sinks, conv_w, conv_b, conv_ln_g, conv_ln_b, attn_out_g, conv_out_g, w_out, ffn_norm_g, w_gate, w_up, w_down, final_norm_g, loss_target, m_meta_tokens, m_attn_norm_g, m_w_in, m_attn_sinks, m_conv_w, m_conv_b, m_conv_ln_g, m_conv_ln_b, m_attn_out_g, m_conv_out_g, m_w_out, m_ffn_norm_g, m_w_gate, m_w_up, m_w_down, m_final_norm_g, v_meta_tokens, v_attn_norm_g, v_w_in, v_attn_sinks, v_conv_w, v_conv_b, v_conv_ln_g, v_conv_ln_b, v_attn_out_g, v_conv_out_g, v_w_out, v_ffn_norm_g, v_w_gate, v_w_up, v_w_down, v_final_norm_g):
    xi, yi, ci = _position()
    dev = jnp.reshape(_device_number((xi, yi, ci)), (1,)).astype(jnp.int32)
    core = jnp.reshape(ci, (1,)).astype(jnp.int32)
    slot = jnp.reshape(2 * xi + yi, (1,)).astype(jnp.int32)

    w_in_t, meta_st, convw_st = _allgather_params([w_in[0].T.astype(BF16)], [meta_tokens, conv_w[0]])
    later = [w_out[0].astype(BF16), w_gate[0].T.astype(BF16), w_up[0].T.astype(BF16), w_down[0].astype(BF16)]
    meta_full = jnp.transpose(meta_st, (1, 0, 2)).reshape(N_META, D_MODEL)
    convw_full = jnp.transpose(convw_st, (1, 0, 2)).reshape(CONV_K, CONV_W)

    h0 = jnp.concatenate([jnp.zeros((LEAD, D_MODEL), F32), meta_full, x[0]], axis=0)
    final_g = final_norm_g.reshape(1, D_MODEL)

    q, kv, ca, cg = _inproj_fwd(h0, attn_norm_g, w_in_t)
    o_attn, lse, (w_out_b, wg_t, wu_t, wd_b) = _attn_fwd(q, kv, attn_sinks, later)
    o_conv, y_conv = _conv_fwd(ca, cg, convw_full, conv_b, conv_ln_g, conv_ln_b)
    h1 = _outproj_fwd(h0, o_attn, o_conv, attn_out_g, conv_out_g, w_out_b)
    gate, up, dh2, loss_sum, dg_final = _ffn_fwd(h1, ffn_norm_g, wg_t, wu_t, wd_b, final_g, loss_target[0])

    def blocks(g):
        return g.reshape(4, 2, g.shape[0] // N_DEV, D_MODEL)

    def add_siblings(grads, received, tags):
        return [_add_sibling(g, r, core, "add_sibling_" + t) for g, r, t in zip(grads, received, tags)]

    dgate, dup, act, hn2, dh1, dg_ffn = _ffn_bwd(dh2, h1, ffn_norm_g, gate, up, wg_t, wu_t, wd_b)
    ffn_grads = [blocks(_wgrad(dgate, hn2, FF_CHUNK, "wgrad_gate")), blocks(_wgrad(dup, hn2, FF_CHUNK, "wgrad_up")),
                 blocks(_wgrad(act, dh2, FF_CHUNK, "wgrad_down"))]
    (do_attn, do_conv, mixed, dg_ao, dg_co), ffn_sib = _outproj_bwd(
        dh1, o_attn, o_conv, attn_out_g, conv_out_g, w_out_b, ffn_grads)
    ffn_sums = add_siblings(ffn_grads, ffn_sib, ("gate", "up", "down"))
    out_grads = [blocks(_wgrad(mixed, dh1, D_MODEL, "wgrad_out"))]
    (dca, dcg, dconvw, dconvb, dln_g, dln_b), ffn_chips = _conv_bwd(
        do_conv, y_conv, ca, cg, convw_full, conv_ln_g, conv_ln_b, ffn_sums)
    (dq, dkv, dkvm, dsinks), out_sib = _attn_bwd(q, kv, attn_sinks, o_attn, lse, do_attn, out_grads)
    out_sums = add_siblings(out_grads, out_sib, ("out",))
    (dh0, dproj, hn1, dg_attn), out_chips = _inproj_bwd(
        dh1, h0, attn_norm_g, dq, dkv, dkvm, dca, dcg, w_in_t, out_sums)
    in_grads = [blocks(_wgrad(dproj, hn1, 1792, "wgrad_in"))]
    in_sums = add_siblings(in_grads, _reduce_siblings(in_grads), ("in",))
    small_a = jnp.concatenate([
        dg_attn, dg_ffn, dg_final, jnp.concatenate([dconvb, dln_g], axis=1), jnp.concatenate([dln_b, dg_ao], axis=1),
        jnp.concatenate([dg_co, dsinks, jnp.zeros((1, 512 - N_HEADS), F32)], axis=1),
        jnp.zeros((2, D_MODEL), F32)], axis=0)
    small_b = jnp.transpose(dh0[LEAD:BLOCK].reshape(N_META, N_DEV, 128), (1, 0, 2)).reshape(N_DEV * N_META, 128)
    small_c = jnp.transpose(dconvw.reshape(32, N_DEV, 64), (1, 0, 2)).reshape(N_DEV * 32, 64)
    in_chips, ga, gb, gc = _reduce_chips(in_sums, [small_a, small_b, small_c])
    tags = ("in", "out", "gate", "up", "down")
    chip_sums = in_sums + out_sums + ffn_sums
    from_chips = [in_chips] + list(out_chips) + list(ffn_chips)
    g_own = [_sum_chips(p, r, slot, "sum_chips_" + t) for p, r, t in zip(chip_sums, from_chips, tags)]
    g_big = [g_own[0].T, g_own[1], g_own[2].T, g_own[3].T, g_own[4]]

    big = [(w_in, m_w_in, v_w_in), (w_out, m_w_out, v_w_out), (w_gate, m_w_gate, v_w_gate), (w_up, m_w_up, v_w_up),
           (w_down, m_w_down, v_w_down)]
    big_out = {}
    for t, g, (w, m, v) in zip(tags, g_big, big):
        delta, nm, nv = _adamw(g, w[0], m[0], v[0], "adamw_" + t)
        big_out[t] = (g[None], delta[None], nm[None], nv[None])

    small_params = [
        (meta_tokens, m_meta_tokens, v_meta_tokens), (attn_norm_g, m_attn_norm_g, v_attn_norm_g),
        (attn_sinks, m_attn_sinks, v_attn_sinks), (conv_w[0], m_conv_w[0], v_conv_w[0]),
        (conv_b, m_conv_b, v_conv_b), (conv_ln_g, m_conv_ln_g, v_conv_ln_g), (conv_ln_b, m_conv_ln_b, v_conv_ln_b),
        (attn_out_g, m_attn_out_g, v_attn_out_g), (conv_out_g, m_conv_out_g, v_conv_out_g),
        (ffn_norm_g, m_ffn_norm_g, v_ffn_norm_g),
        (final_g, m_final_norm_g.reshape(1, D_MODEL), v_final_norm_g.reshape(1, D_MODEL))]
    sm = _adamw_small(dev, ga, gb, gc, small_params)
    sm[3] = [a[None] for a in sm[3]]
    sm[10] = [a.reshape(D_MODEL) for a in sm[10]]

    per_param = [sm[0], sm[1], big_out["in"], sm[2], sm[3], sm[4], sm[5], sm[6], sm[7], sm[8], big_out["out"],
                 sm[9], big_out["gate"], big_out["up"], big_out["down"], sm[10]]
    loss = lax.psum(loss_sum[0, 0], ("x", "y", "c"))
    grad_x = dh0[BLOCK:][None]
    outs = [loss, grad_x]
    for kind in range(4):
        outs += [p[kind] for p in per_param]
    return tuple(outs)
```

```python
import functools
import math

import jax
import jax.numpy as jnp
from jax import lax
from jax.experimental import pallas as pl
from jax.experimental.pallas import tpu as pltpu

F32, BF16 = jnp.float32, jnp.bfloat16
MESH = pl.DeviceIdType.MESH

D_MODEL = 1024
N_META = 16
BLOCK = 128
LEAD = BLOCK - N_META
HEAD_DIM = 64
N_HEADS = 8
GROUP = 4
ATTN_W = 512
KV_W = 128
CONV_W = 512
CONV_K = 31
HALO = 32
D_FF = 2816
FF_CHUNK = D_FF // 2
FF_SUB = [slice(s, min(s + 256, FF_CHUNK)) for s in range(0, FF_CHUNK, 256)]
N_DEV = 8
EPS = 1e-5
NEG = -1e30
TM = 640
CONV_ROWS = 64
VMEM_LIMIT = 56 * 1024 * 1024

ADAM_LR, ADAM_B1, ADAM_B2, ADAM_EPS, ADAM_WD, ADAM_STEP = 0.001, 0.9, 0.999, 1e-08, 0.01, 10

NT = (((1,), (1,)), ((), ()))
NN = (((1,), (0,)), ((), ()))
TN = (((0,), (0,)), ((), ()))


def _dot(a, b, dims):
    return lax.dot_general(a, b, dims, preferred_element_type=F32)


def _sigmoid(x):
    return 1.0 / (1.0 + jnp.exp(-x))


def _pcall(body, *, name, out_shape, grid=None, in_specs=None, out_specs=None, scratch_shapes=(),
           semantics=None, **kw):
    params = dict(vmem_limit_bytes=VMEM_LIMIT)
    if semantics is not None:
        params["dimension_semantics"] = semantics
    extra = {}
    if grid is not None:
        extra["grid"] = grid
    if in_specs is not None:
        extra["in_specs"] = in_specs
    if out_specs is not None:
        extra["out_specs"] = out_specs
    return pl.pallas_call(body, name=name, out_shape=out_shape, scratch_shapes=list(scratch_shapes),
                          compiler_params=pltpu.CompilerParams(**params), **extra, **kw)


def _rows(tm, cols, off=0):
    return pl.BlockSpec((tm, cols), lambda i, *_: (i + off, 0))


def _full(shape):
    nd = len(shape)
    return pl.BlockSpec(shape, lambda *_: (0,) * nd)


def _rms_stats(x):
    return lax.rsqrt(jnp.mean(x * x, axis=-1, keepdims=True) + EPS)


def _rms_bwd(dy, x, r, g):
    t = dy * g
    dx = r * (t - x * (r * r) * jnp.mean(t * x, axis=-1, keepdims=True))
    dg = jnp.sum(dy * x * r, axis=0, keepdims=True)
    return dx, dg


def _inproj_fwd(h0, g1, w_in_t):
    R = h0.shape[0]

    def body(h_ref, g_ref, w_ref, q_ref, kv_ref, ca_ref, cg_ref):
        h = h_ref[...]
        hn = (h * _rms_stats(h) * g_ref[...]).astype(BF16)
        q_ref[...] = _dot(hn, w_ref[0:512, :], NT).astype(BF16)
        kv_ref[...] = _dot(hn, w_ref[512:768, :], NT).astype(BF16)
        ca_ref[...] = _dot(hn, w_ref[768:1280, :], NT)
        cg_ref[...] = _dot(hn, w_ref[1280:1792, :], NT)

    return _pcall(
        body, name="inproj_fwd", grid=(R // TM,),
        in_specs=[_rows(TM, D_MODEL), _full((1, D_MODEL)), _full((1792, D_MODEL))],
        out_specs=[_rows(TM, 512), _rows(TM, 256), _rows(TM, 512), _rows(TM, 512)],
        out_shape=[jax.ShapeDtypeStruct((R, 512), BF16), jax.ShapeDtypeStruct((R, 256), BF16),
                   jax.ShapeDtypeStruct((R, 512), F32), jax.ShapeDtypeStruct((R, 512), F32)],
        semantics=("parallel",),
    )(h0, g1, w_in_t)


GB = GROUP * BLOCK
ATTN_SCALE = 1.0 / math.sqrt(HEAD_DIM)


def _stack_heads(ref, g):
    return jnp.concatenate(
        [ref[:, HEAD_DIM * (GROUP * g + j):HEAD_DIM * (GROUP * g + j + 1)] for j in range(GROUP)], axis=0)


def _unstack_heads(x):
    return jnp.concatenate([x[BLOCK * j:BLOCK * (j + 1), :] for j in range(GROUP)], axis=1)


def _with_ones(v):
    return jnp.concatenate([v, jnp.ones(v.shape, v.dtype)], axis=1)


def _attn_tables(sink_ref, bias_s, sink_s):
    ii = lax.broadcasted_iota(jnp.int32, (BLOCK, BLOCK), 0)
    jj = lax.broadcasted_iota(jnp.int32, (BLOCK, BLOCK), 1)
    dist = jnp.where(jj <= ii, ii - jj, ii - jj + BLOCK).astype(F32)
    for h in range(N_HEADS):
        bias_s[BLOCK * h:BLOCK * (h + 1), :] = dist * -(2.0 ** -(h + 1))
        sink_s[BLOCK * h:BLOCK * (h + 1), :] = jnp.zeros((BLOCK, 1), F32) + sink_ref[0, h]


def _attn_masks(b):
    ii = lax.broadcasted_iota(jnp.int32, (GB, BLOCK), 0) & (BLOCK - 1)
    jj = lax.broadcasted_iota(jnp.int32, (GB, BLOCK), 1)
    sel = jj <= ii
    pen = jnp.where(sel, jnp.where(b >= 1, 0.0, NEG), jnp.where(b >= 2, 0.0, NEG))
    mi = lax.broadcasted_iota(jnp.int32, (GB, N_META), 0) & (BLOCK - 1)
    mj = lax.broadcasted_iota(jnp.int32, (GB, N_META), 1)
    pen_m = jnp.where((mj + LEAD) <= (mi + b * BLOCK), 0.0, NEG)
    return sel, pen, pen_m


def _attn_scores(qg, kc, kp, km, sel, pen, pen_m, bias):
    s_b = jnp.where(sel, _dot(qg, kc, NT), _dot(qg, kp, NT)) + bias + pen
    s_m = _dot(qg, km, NT) + pen_m
    return s_b, s_m


def _attn_fwd(q, kv, sinks, shards):
    R = q.shape[0]
    nb = R // BLOCK
    ns = len(shards)
    forward_steps = [(w + 1) * (nb - 8) // (ns + 1) for w in range(ns)]

    def body(sink_ref, q_ref, kvc_ref, kvp_ref, kvm_ref, *refs):
        ag_ins, (o_ref, lse_ref), ag_outs = refs[:ns], refs[ns:ns + 2], refs[ns + 2:2 * ns + 2]
        send_sems, recv_sems, local_sems, bias_s, sink_s = refs[2 * ns + 2:]
        b = pl.program_id(0)
        ag_start, ag_forward, ag_finish = _two_level_allgather(
            ag_ins, ag_outs, _row_block(ag_outs, [s.shape[0] for s in shards]), send_sems, recv_sems, local_sems)
        pl.when(b == 0)(ag_start)
        for w, step in enumerate(forward_steps):
            pl.when(b == step)(functools.partial(ag_forward, w))
        pl.when(b == 0)(functools.partial(_attn_tables, sink_ref, bias_s, sink_s))
        sel, pen, pen_m = _attn_masks(b)
        for g in range(N_HEADS // GROUP):
            ks, vs = slice(HEAD_DIM * g, HEAD_DIM * (g + 1)), slice(KV_W + HEAD_DIM * g, KV_W + HEAD_DIM * (g + 1))
            rows = slice(GB * g, GB * (g + 1))
            qg = _stack_heads(q_ref, g) * ATTN_SCALE
            s_b, s_m = _attn_scores(qg, kvc_ref[:, ks], kvp_ref[:, ks], kvm_ref[LEAD:BLOCK, ks],
                                    sel, pen, pen_m, bias_s[rows, :])
            sink = sink_s[rows, :]
            m = jnp.maximum(jnp.maximum(jnp.max(s_b, axis=-1, keepdims=True),
                                        jnp.max(s_m, axis=-1, keepdims=True)), sink)
            p_b = jnp.exp(s_b - m)
            p_m = jnp.exp(s_m - m)
            p_c = jnp.where(sel, p_b, 0.0).astype(BF16)
            p_p = jnp.where(sel, 0.0, p_b).astype(BF16)
            ol = (_dot(p_c, _with_ones(kvc_ref[:, vs]), NN) + _dot(p_p, _with_ones(kvp_ref[:, vs]), NN)
                  + _dot(p_m.astype(BF16), _with_ones(kvm_ref[LEAD:BLOCK, vs]), NN))
            l = ol[:, HEAD_DIM:HEAD_DIM + 1] + jnp.exp(sink - m)
            o_ref[:, GROUP * HEAD_DIM * g:GROUP * HEAD_DIM * (g + 1)] = _unstack_heads(ol[:, 0:HEAD_DIM] / l)
            lse = m + jnp.log(l)
            for j in range(GROUP):
                lse_ref[:, GROUP * g + j:GROUP * g + j + 1] = lse[BLOCK * j:BLOCK * (j + 1), :]
        pl.when(b == nb - 1)(ag_finish)

    res = _pcall(
        body, name="attn_fwd", grid=(nb,),
        in_specs=[pl.BlockSpec(memory_space=pltpu.SMEM),
                  _rows(BLOCK, 512), _rows(BLOCK, 256),
                  pl.BlockSpec((BLOCK, 256), lambda b: (jnp.maximum(b - 1, 0), 0)),
                  _full((BLOCK, 256))] + [ANY] * ns,
        out_specs=[_rows(BLOCK, 512), _rows(BLOCK, N_HEADS)] + [ANY] * ns,
        out_shape=[jax.ShapeDtypeStruct((R, 512), F32), jax.ShapeDtypeStruct((R, N_HEADS), F32)]
        + [jax.ShapeDtypeStruct((N_DEV * s.shape[0], s.shape[1]), s.dtype) for s in shards],
        scratch_shapes=_sem_pair(7 * ns) + [pltpu.SemaphoreType.DMA((ns,)),
                                            pltpu.VMEM((N_HEADS * BLOCK, BLOCK), F32),
                                            pltpu.VMEM((N_HEADS * BLOCK, 1), F32)],
        semantics=("arbitrary",),
    )(sinks, q, kv, kv, kv, *shards)
    return res[0], res[1], res[2:]


def _ln_silu(y, lg, lb):
    mu = jnp.mean(y, axis=-1, keepdims=True)
    xc = y - mu
    rstd = lax.rsqrt(jnp.mean(xc * xc, axis=-1, keepdims=True) + EPS)
    xhat = xc * rstd
    yn = xhat * lg + lb
    return yn, xhat, rstd


def _conv_fwd(ca, cg, conv_w, conv_b, ln_g, ln_b):
    R = ca.shape[0]
    nt = R // TM
    hpt = TM // HALO

    def body(ca_ref, cg_ref, cah_ref, cgh_ref, w_ref, b_ref, lg_ref, lb_ref, oc_ref, y_ref, u_s):
        i = pl.program_id(0)
        u_s[HALO:HALO + TM, :] = ca_ref[...] * _sigmoid(cg_ref[...])
        u_s[0:HALO, :] = jnp.where(i > 0, cah_ref[...] * _sigmoid(cgh_ref[...]), 0.0)
        for rc in range(TM // CONV_ROWS):
            base = rc * CONV_ROWS + HALO - (CONV_K - 1)
            acc = jnp.zeros((CONV_ROWS, CONV_W), F32) + b_ref[...]
            for k in range(CONV_K):
                acc = acc + u_s[pl.ds(base + k, CONV_ROWS), :] * w_ref[k:k + 1, :]
            rows = slice(rc * CONV_ROWS, (rc + 1) * CONV_ROWS)
            y_ref[rows, :] = acc
            yn, _, _ = _ln_silu(acc, lg_ref[...], lb_ref[...])
            oc_ref[rows, :] = yn * _sigmoid(yn)

    prev_halo = pl.BlockSpec((HALO, CONV_W), lambda i: (jnp.maximum(i * hpt - 1, 0), 0))
    return _pcall(
        body, name="conv_fwd", grid=(nt,),
        in_specs=[_rows(TM, CONV_W), _rows(TM, CONV_W), prev_halo, prev_halo,
                  _full((CONV_K, CONV_W)), _full((1, CONV_W)), _full((1, CONV_W)), _full((1, CONV_W))],
        out_specs=[_rows(TM, CONV_W), _rows(TM, CONV_W)],
        out_shape=[jax.ShapeDtypeStruct((R, CONV_W), F32), jax.ShapeDtypeStruct((R, CONV_W), F32)],
        scratch_shapes=[pltpu.VMEM((HALO + TM, CONV_W), F32)],
        semantics=("parallel",),
    )(ca, cg, ca, cg, conv_w, conv_b, ln_g, ln_b)


def _outproj_fwd(h0, o_attn, o_conv, ga, gc, w_out):
    R = h0.shape[0]

    def body(h_ref, oa_ref, oc_ref, ga_ref, gc_ref, w_ref, h1_ref):
        oa, oc = oa_ref[...], oc_ref[...]
        ma = (oa * _rms_stats(oa) * ga_ref[...]).astype(BF16)
        mc = (oc * _rms_stats(oc) * gc_ref[...]).astype(BF16)
        h1_ref[...] = h_ref[...] + _dot(ma, w_ref[0:512, :], NN) + _dot(mc, w_ref[512:1024, :], NN)

    return _pcall(
        body, name="outproj_fwd", grid=(R // TM,),
        in_specs=[_rows(TM, D_MODEL), _rows(TM, 512), _rows(TM, 512), _full((1, 512)), _full((1, 512)),
                  _full((D_MODEL, D_MODEL))],
        out_specs=_rows(TM, D_MODEL),
        out_shape=jax.ShapeDtypeStruct((R, D_MODEL), F32),
        semantics=("parallel",),
    )(h0, o_attn, o_conv, ga, gc, w_out)


def _target_copy(tgt_hbm, tgt_s, sem, i, first):
    if first:
        return pltpu.make_async_copy(tgt_hbm.at[pl.ds(0, TM - BLOCK)], tgt_s.at[pl.ds(BLOCK, TM - BLOCK)], sem)
    return pltpu.make_async_copy(tgt_hbm.at[pl.ds(i * TM - BLOCK, TM)], tgt_s, sem)


def _ffn_fwd(h1, g2, wg_t, wu_t, wd, gf, target):
    R = h1.shape[0]
    nt, nj = R // TM, D_FF // FF_CHUNK

    def body(h1_ref, g2_ref, wg_ref, wu_ref, wd_ref, gf_ref, tgt_hbm,
             gate_ref, up_ref, dh2_ref, loss_ref, dgf_ref, hn_s, acc_s, tgt_s, sem, act_s):
        i, j = pl.program_id(0), pl.program_id(1)

        @pl.when((i == 0) & (j == 0))
        def _():
            loss_ref[...] = jnp.zeros_like(loss_ref)
            dgf_ref[...] = jnp.zeros_like(dgf_ref)

        @pl.when(j == 0)
        def _():
            h1 = h1_ref[...]
            hn_s[...] = (h1 * _rms_stats(h1) * g2_ref[...]).astype(BF16)

            @pl.when(i == 0)
            def _():
                tgt_s[0:BLOCK, :] = jnp.zeros((BLOCK, D_MODEL), F32)
                _target_copy(tgt_hbm, tgt_s, sem, i, True).start()

            @pl.when(i > 0)
            def _():
                _target_copy(tgt_hbm, tgt_s, sem, i, False).start()

        hn = hn_s[...]
        for cs in FF_SUB:
            gate = _dot(hn, wg_ref[cs, :], NT)
            up = _dot(hn, wu_ref[cs, :], NT)
            gate_ref[:, cs] = gate.astype(BF16)
            up_ref[:, cs] = up.astype(BF16)
            act_s[:, cs] = (gate * _sigmoid(gate) * up).astype(BF16)
        part = _dot(act_s[...], wd_ref[...], NN)

        @pl.when(j == 0)
        def _():
            acc_s[...] = part

        @pl.when(j == nj - 1)
        def _():
            @pl.when(i == 0)
            def _():
                _target_copy(tgt_hbm, tgt_s, sem, i, True).wait()

            @pl.when(i > 0)
            def _():
                _target_copy(tgt_hbm, tgt_s, sem, i, False).wait()

            h2 = h1_ref[...] + acc_s[...] + part
            rf = _rms_stats(h2)
            gf = gf_ref[...]
            row = lax.broadcasted_iota(jnp.int32, (TM, 1), 0) + i * TM
            err = jnp.where(row >= BLOCK, h2 * rf * gf - tgt_s[...], 0.0)
            dy = err * (1.0 / D_MODEL)
            dh2, dgf = _rms_bwd(dy, h2, rf, gf)
            dh2_ref[...] = dh2
            loss_ref[...] += (0.5 / D_MODEL) * jnp.sum(err * err)
            dgf_ref[...] += dgf

    wspec = pl.BlockSpec((FF_CHUNK, D_MODEL), lambda i, j: (j, 0))
    aspec = pl.BlockSpec((TM, FF_CHUNK), lambda i, j: (i, j))
    return _pcall(
        body, name="ffn_fwd", grid=(nt, nj),
        in_specs=[_rows(TM, D_MODEL), _full((1, D_MODEL)), wspec, wspec, wspec, _full((1, D_MODEL)),
                  pl.BlockSpec(memory_space=pl.ANY)],
        out_specs=[aspec, aspec, _rows(TM, D_MODEL),
                   _full((8, 128)), _full((1, D_MODEL))],
        out_shape=[jax.ShapeDtypeStruct((R, D_FF), BF16), jax.ShapeDtypeStruct((R, D_FF), BF16),
                   jax.ShapeDtypeStruct((R, D_MODEL), F32),
                   jax.ShapeDtypeStruct((8, 128), F32), jax.ShapeDtypeStruct((1, D_MODEL), F32)],
        scratch_shapes=[pltpu.VMEM((TM, D_MODEL), BF16), pltpu.VMEM((TM, D_MODEL), F32),
                        pltpu.VMEM((TM, D_MODEL), F32), pltpu.SemaphoreType.DMA, pltpu.VMEM((TM, FF_CHUNK), BF16)],
        semantics=("arbitrary", "arbitrary"),
    )(h1, g2, wg_t, wu_t, wd, gf, target)


def _ffn_bwd(dh2, h1, g2, gate, up, wg_t, wu_t, wd):
    R = h1.shape[0]
    nt, nj = R // TM, D_FF // FF_CHUNK

    wspec = pl.BlockSpec((FF_CHUNK, D_MODEL), lambda i, j: (j, 0))
    aspec = pl.BlockSpec((TM, FF_CHUNK), lambda i, j: (i, j))
    act_shape = jax.ShapeDtypeStruct((R, D_FF), BF16)

    def act_body(dh2_ref, gate_ref, up_ref, wd_ref, dgate_ref, dup_ref, act_ref, dhb_s):
        @pl.when(pl.program_id(1) == 0)
        def _():
            dhb_s[...] = dh2_ref[...].astype(BF16)

        dhb = dhb_s[...]
        for cs in FF_SUB:
            dact = _dot(dhb, wd_ref[cs, :], NT)
            gate = gate_ref[:, cs].astype(F32)
            up = up_ref[:, cs].astype(F32)
            sig = _sigmoid(gate)
            silu = gate * sig
            dgate_ref[:, cs] = (dact * up * (sig * (1.0 + gate * (1.0 - sig)))).astype(BF16)
            dup_ref[:, cs] = (dact * silu).astype(BF16)
            act_ref[:, cs] = (silu * up).astype(BF16)

    dgate, dup, act = _pcall(
        act_body, name="ffn_bwd_act", grid=(nt, nj),
        in_specs=[_rows(TM, D_MODEL), aspec, aspec, wspec],
        out_specs=[aspec, aspec, aspec], out_shape=[act_shape, act_shape, act_shape],
        scratch_shapes=[pltpu.VMEM((TM, D_MODEL), BF16)],
        semantics=("parallel", "arbitrary"),
    )(dh2, gate, up, wd)

    def in_body(dh2_ref, h1_ref, g2_ref, dgate_ref, dup_ref, wg_ref, wu_ref, hn_ref, dh1_ref, dg2_ref, acc_s):
        i, j = pl.program_id(0), pl.program_id(1)

        @pl.when((i == 0) & (j == 0))
        def _():
            dg2_ref[...] = jnp.zeros_like(dg2_ref)

        part = _dot(dgate_ref[...], wg_ref[...], NN) + _dot(dup_ref[...], wu_ref[...], NN)

        @pl.when(j == 0)
        def _():
            acc_s[...] = part

        @pl.when(j == nj - 1)
        def _():
            h1 = h1_ref[...]
            r = _rms_stats(h1)
            g2 = g2_ref[...]
            hn_ref[...] = (h1 * r * g2).astype(BF16)
            dx, dg = _rms_bwd(acc_s[...] + part, h1, r, g2)
            dh1_ref[...] = dh2_ref[...] + dx
            dg2_ref[...] += dg

    hn2, dh1, dg2 = _pcall(
        in_body, name="ffn_bwd_in", grid=(nt, nj),
        in_specs=[_rows(TM, D_MODEL), _rows(TM, D_MODEL), _full((1, D_MODEL)), aspec, aspec, wspec, wspec],
        out_specs=[_rows(TM, D_MODEL), _rows(TM, D_MODEL), _full((1, D_MODEL))],
        out_shape=[jax.ShapeDtypeStruct((R, D_MODEL), BF16), jax.ShapeDtypeStruct((R, D_MODEL), F32),
                   jax.ShapeDtypeStruct((1, D_MODEL), F32)],
        scratch_shapes=[pltpu.VMEM((TM, D_MODEL), F32)],
        semantics=("arbitrary", "arbitrary"),
    )(dh2, h1, g2, dgate, dup, wg_t, wu_t)
    return dgate, dup, act, hn2, dh1, dg2


def _wgrad(a, b, tm, name, partials=()):
    K, M = a.shape
    N = b.shape[1]
    nm, nk, npart = M // tm, K // TM, len(partials)

    def body(a_ref, b_ref, *refs):
        p_ins, o_ref, p_outs, sems = refs[:npart], refs[npart], refs[npart + 1:2 * npart + 1], refs[2 * npart + 1:]
        step = pl.program_id(0) * nk + pl.program_id(1)
        exchange = functools.partial(_chip_copies, p_ins, p_outs, *sems)
        if npart:
            _hosted(step, nm * nk, exchange)

        @pl.when(pl.program_id(1) == 0)
        def _():
            o_ref[...] = jnp.zeros_like(o_ref)

        o_ref[...] += _dot(a_ref[...], b_ref[...].astype(BF16), TN)
        if npart:
            _hosted_wait(step, nm * nk, exchange)

    res = _pcall(
        body, name=name, grid=(nm, nk),
        in_specs=[pl.BlockSpec((TM, tm), lambda m, k: (k, m)), pl.BlockSpec((TM, N), lambda m, k: (k, 0))]
        + [ANY] * npart,
        out_specs=[pl.BlockSpec((tm, N), lambda m, k: (m, 0))] + [ANY] * npart,
        out_shape=[jax.ShapeDtypeStruct((M, N), F32)] + _chip_shapes(partials),
        scratch_shapes=_sem_pair(3 * npart) if npart else [],
        semantics=("arbitrary", "arbitrary"),
    )(a, b, *partials)
    return (res[0], res[1:]) if npart else res[0]


def _outproj_bwd(dh1, o_attn, o_conv, ga, gc, w_out, grads):
    R = dh1.shape[0]
    nt, ng = R // TM, len(grads)

    def body(dh1_ref, oa_ref, oc_ref, ga_ref, gc_ref, w_ref, *refs):
        g_ins, (doa_ref, doc_ref, mixed_ref, dga_ref, dgc_ref) = refs[:ng], refs[ng:ng + 5]
        g_outs, (send_sems, recv_sems) = refs[ng + 5:2 * ng + 5], refs[2 * ng + 5:]
        exchange = functools.partial(_sibling_copies, g_ins, g_outs, send_sems, recv_sems)
        _hosted(pl.program_id(0), nt, exchange)

        @pl.when(pl.program_id(0) == 0)
        def _():
            dga_ref[...] = jnp.zeros_like(dga_ref)
            dgc_ref[...] = jnp.zeros_like(dgc_ref)

        dm = _dot(dh1_ref[...].astype(BF16), w_ref[...], NT)
        oa, oc = oa_ref[...], oc_ref[...]
        ra, rc = _rms_stats(oa), _rms_stats(oc)
        mixed_ref[:, 0:512] = (oa * ra * ga_ref[...]).astype(BF16)
        mixed_ref[:, 512:1024] = (oc * rc * gc_ref[...]).astype(BF16)
        doa, dga = _rms_bwd(dm[:, 0:512], oa, ra, ga_ref[...])
        doc, dgc = _rms_bwd(dm[:, 512:1024], oc, rc, gc_ref[...])
        doa_ref[...] = doa
        doc_ref[...] = doc
        dga_ref[...] += dga
        dgc_ref[...] += dgc
        _hosted_wait(pl.program_id(0), nt, exchange)

    res = _pcall(
        body, name="outproj_bwd", grid=(nt,),
        in_specs=[_rows(TM, D_MODEL), _rows(TM, 512), _rows(TM, 512), _full((1, 512)), _full((1, 512)),
                  _full((D_MODEL, D_MODEL))] + [ANY] * ng,
        out_specs=[_rows(TM, 512), _rows(TM, 512), _rows(TM, D_MODEL), _full((1, 512)), _full((1, 512))]
        + [ANY] * ng,
        out_shape=[jax.ShapeDtypeStruct((R, 512), F32), jax.ShapeDtypeStruct((R, 512), F32),
                   jax.ShapeDtypeStruct((R, D_MODEL), BF16),
                   jax.ShapeDtypeStruct((1, 512), F32), jax.ShapeDtypeStruct((1, 512), F32)]
        + _sibling_shapes(grads),
        scratch_shapes=_sem_pair(ng),
        semantics=("arbitrary",),
    )(dh1, o_attn, o_conv, ga, gc, w_out, *grads)
    return res[:5], res[5:]


def _conv_bwd(do_conv, y, ca, cg, conv_w, ln_g, ln_b, partials):
    R = ca.shape[0]
    nt = R // TM
    hpt = TM // HALO
    npart = len(partials)

    def body(do_ref, doh_ref, y_ref, yh_ref, ca_ref, cg_ref, cah_ref, cgh_ref, w_ref, lg_ref, lb_ref, *refs):
        p_ins, (dca_ref, dcg_ref, dw_ref, db_ref, dlg_ref, dlb_ref) = refs[:npart], refs[npart:npart + 6]
        p_outs, (send_sems, recv_sems, u_s, dy_s) = refs[npart + 6:2 * npart + 6], refs[2 * npart + 6:]
        i = pl.program_id(0)
        exchange = functools.partial(_chip_copies, p_ins, p_outs, send_sems, recv_sems)
        _hosted(i, nt, exchange)

        @pl.when(i == 0)
        def _():
            dw_ref[...] = jnp.zeros_like(dw_ref)
            db_ref[...] = jnp.zeros_like(db_ref)
            dlg_ref[...] = jnp.zeros_like(dlg_ref)
            dlb_ref[...] = jnp.zeros_like(dlb_ref)

        lg, lb = lg_ref[...], lb_ref[...]

        def ln_bwd(yv, dov):
            yn, xhat, rstd = _ln_silu(yv, lg, lb)
            sig = _sigmoid(yn)
            dyn = dov * (sig * (1.0 + yn * (1.0 - sig)))
            dxh = dyn * lg
            dyv = rstd * (dxh - jnp.mean(dxh, axis=-1, keepdims=True)
                          - xhat * jnp.mean(dxh * xhat, axis=-1, keepdims=True))
            return dyv, dyn, xhat

        dyv, dyn, xhat = ln_bwd(y_ref[...], do_ref[...])
        dy_s[0:TM, :] = dyv
        dlg_ref[...] += jnp.sum(dyn * xhat, axis=0, keepdims=True)
        dlb_ref[...] += jnp.sum(dyn, axis=0, keepdims=True)
        db_ref[...] += jnp.sum(dyv, axis=0, keepdims=True)
        dyh, _, _ = ln_bwd(yh_ref[...], doh_ref[...])
        dy_s[TM:TM + HALO, :] = jnp.where(i < nt - 1, dyh, 0.0)
        u_s[HALO:HALO + TM, :] = ca_ref[...] * _sigmoid(cg_ref[...])
        u_s[0:HALO, :] = jnp.where(i > 0, cah_ref[...] * _sigmoid(cgh_ref[...]), 0.0)

        for rc in range(TM // CONV_ROWS):
            acc = jnp.zeros((CONV_ROWS, CONV_W), F32)
            for k in range(CONV_K):
                acc = acc + dy_s[pl.ds(rc * CONV_ROWS + CONV_K - 1 - k, CONV_ROWS), :] * w_ref[k:k + 1, :]
            rows = slice(rc * CONV_ROWS, (rc + 1) * CONV_ROWS)
            sg = _sigmoid(cg_ref[rows, :])
            dca_ref[rows, :] = (acc * sg).astype(BF16)
            dcg_ref[rows, :] = (acc * ca_ref[rows, :] * sg * (1.0 - sg)).astype(BF16)

        for k in range(CONV_K):
            prod = u_s[pl.ds(HALO - (CONV_K - 1) + k, TM), :] * dy_s[0:TM, :]
            dw_ref[k:k + 1, :] += jnp.sum(prod, axis=0, keepdims=True)
        _hosted_wait(i, nt, exchange)

    prev_halo = pl.BlockSpec((HALO, CONV_W), lambda i: (jnp.maximum(i * hpt - 1, 0), 0))
    next_halo = pl.BlockSpec((HALO, CONV_W), lambda i: (jnp.minimum((i + 1) * hpt, nt * hpt - 1), 0))
    vec = jax.ShapeDtypeStruct((1, CONV_W), F32)
    res = _pcall(
        body, name="conv_bwd", grid=(nt,),
        in_specs=[_rows(TM, CONV_W), next_halo, _rows(TM, CONV_W), next_halo,
                  _rows(TM, CONV_W), _rows(TM, CONV_W), prev_halo, prev_halo,
                  _full((CONV_K, CONV_W)), _full((1, CONV_W)), _full((1, CONV_W))] + [ANY] * npart,
        out_specs=[_rows(TM, CONV_W), _rows(TM, CONV_W), _full((32, CONV_W)),
                   _full((1, CONV_W)), _full((1, CONV_W)), _full((1, CONV_W))] + [ANY] * npart,
        out_shape=[jax.ShapeDtypeStruct((R, CONV_W), BF16), jax.ShapeDtypeStruct((R, CONV_W), BF16),
                   jax.ShapeDtypeStruct((32, CONV_W), F32), vec, vec, vec] + _chip_shapes(partials),
        scratch_shapes=_sem_pair(3 * npart)
        + [pltpu.VMEM((HALO + TM, CONV_W), F32), pltpu.VMEM((TM + HALO, CONV_W), F32)],
        semantics=("arbitrary",),
    )(do_conv, do_conv, y, y, ca, cg, ca, cg, conv_w, ln_g, ln_b, *partials)
    return res[:6], res[6:]


def _attn_bwd(q, kv, sinks, o, lse, do, grads):
    R = q.shape[0]
    nb = R // BLOCK
    ng = len(grads)

    def body(sink_ref, q_ref, kvc_ref, kvp_ref, kvm_ref, o_ref, lse_ref, do_ref, *refs):
        g_ins, (dq_ref, dkv_ref, dkvm_ref, dsink_ref) = refs[:ng], refs[ng:ng + 4]
        g_outs = refs[ng + 4:2 * ng + 4]
        send_sems, recv_sems, carry_s, cur_s, prev_s, bias_s, sink_s = refs[2 * ng + 4:]
        b = pl.program_id(0)
        exchange = functools.partial(_sibling_copies, g_ins, g_outs, send_sems, recv_sems)
        _hosted(b, nb + 1, exchange)

        @pl.when(b == 0)
        def _():
            dkvm_ref[...] = jnp.zeros_like(dkvm_ref)
            carry_s[...] = jnp.zeros_like(carry_s)
            for h in range(N_HEADS):
                dsink_ref[0, h] = 0.0
            _attn_tables(sink_ref, bias_s, sink_s)

        @pl.when(b < nb)
        def _():
            sel, pen, pen_m = _attn_masks(b)
            for g in range(N_HEADS // GROUP):
                ks, vs = slice(HEAD_DIM * g, HEAD_DIM * (g + 1)), slice(KV_W + HEAD_DIM * g, KV_W + HEAD_DIM * (g + 1))
                rows = slice(GB * g, GB * (g + 1))
                qg = _stack_heads(q_ref, g) * ATTN_SCALE
                kc, kp, km = kvc_ref[:, ks], kvp_ref[:, ks], kvm_ref[LEAD:BLOCK, ks]
                vc, vp, vm = kvc_ref[:, vs], kvp_ref[:, vs], kvm_ref[LEAD:BLOCK, vs]
                s_b, s_m = _attn_scores(qg, kc, kp, km, sel, pen, pen_m, bias_s[rows, :])
                lse = jnp.concatenate(
                    [lse_ref[:, GROUP * g + j:GROUP * g + j + 1] for j in range(GROUP)], axis=0)
                p_b = jnp.exp(s_b - lse)
                p_m = jnp.exp(s_m - lse)
                dog = _stack_heads(do_ref, g)
                delta = jnp.sum(dog * _stack_heads(o_ref, g), axis=-1, keepdims=True)
                dob = dog.astype(BF16)
                dp_b = jnp.where(sel, _dot(dob, vc, NT), _dot(dob, vp, NT))
                ds_b = p_b * (dp_b - delta)
                ds_m = (p_m * (_dot(dob, vm, NT) - delta)).astype(BF16)
                dsk = jnp.exp(sink_s[rows, :] - lse) * delta
                for j in range(GROUP):
                    dsink_ref[0, GROUP * g + j] += -jnp.sum(dsk[BLOCK * j:BLOCK * (j + 1), :])
                ds_c = jnp.where(sel, ds_b, 0.0).astype(BF16)
                ds_p = jnp.where(sel, 0.0, ds_b).astype(BF16)
                p_c = jnp.where(sel, p_b, 0.0).astype(BF16)
                p_p = jnp.where(sel, 0.0, p_b).astype(BF16)
                dq = (_dot(ds_c, kc, NN) + _dot(ds_p, kp, NN) + _dot(ds_m, km, NN)) * ATTN_SCALE
                dq_ref[:, GROUP * HEAD_DIM * g:GROUP * HEAD_DIM * (g + 1)] = _unstack_heads(dq).astype(BF16)
                cur_s[:, ks] = _dot(ds_c, qg, TN)
                cur_s[:, vs] = _dot(p_c, dob, TN)
                prev_s[:, ks] = _dot(ds_p, qg, TN)
                prev_s[:, vs] = _dot(p_p, dob, TN)
                dkvm_ref[:, ks] += _dot(ds_m, qg, TN)
                dkvm_ref[:, vs] += _dot(p_m.astype(BF16), dob, TN)
            dkv_ref[...] = (carry_s[...] + prev_s[...]).astype(BF16)
            carry_s[...] = cur_s[...]

        @pl.when(b == nb)
        def _():
            dkv_ref[...] = carry_s[...].astype(BF16)

        _hosted_wait(b, nb + 1, exchange)

    def at(off):
        return lambda b: (jnp.clip(b + off, 0, nb - 1), 0)

    blk = lambda cols, off=0: pl.BlockSpec((BLOCK, cols), at(off))
    res = _pcall(
        body, name="attn_bwd", grid=(nb + 1,),
        in_specs=[pl.BlockSpec(memory_space=pltpu.SMEM), blk(512), blk(256), blk(256, -1), _full((BLOCK, 256)),
                  blk(512), blk(N_HEADS), blk(512)] + [ANY] * ng,
        out_specs=[blk(512), blk(256, -1), _full((N_META, 256)), pl.BlockSpec(memory_space=pltpu.SMEM)]
        + [ANY] * ng,
        out_shape=[jax.ShapeDtypeStruct((R, 512), BF16), jax.ShapeDtypeStruct((R, 256), BF16),
                   jax.ShapeDtypeStruct((N_META, 256), F32), jax.ShapeDtypeStruct((1, N_HEADS), F32)]
        + _sibling_shapes(grads),
        scratch_shapes=_sem_pair(ng) + [pltpu.VMEM((BLOCK, 256), F32)] * 3
        + [pltpu.VMEM((N_HEADS * BLOCK, BLOCK), F32), pltpu.VMEM((N_HEADS * BLOCK, 1), F32)],
        semantics=("arbitrary",),
    )(sinks, q, kv, kv, kv, o, lse, do, *grads)
    return res[:4], res[4:]


def _inproj_bwd(dh1, h0, g1, dq, dkv, dkvm, dca, dcg, w_in_t):
    R = h0.shape[0]
    nt = R // TM
    assert nt >= 2

    def body(dh1_ref, h0_ref, g_ref, dq_ref, dkv_ref, dkvm_ref, dca_ref, dcg_ref, w_ref,
             gx_hbm, dmeta_ref, dproj_ref, hn_ref, dg_ref, dx_s, gx_sems):
        i = pl.program_id(0)

        def gx_copy(step, slot, first):
            if first:
                return pltpu.make_async_copy(dx_s.at[slot, pl.ds(BLOCK, TM - BLOCK)],
                                             gx_hbm.at[pl.ds(0, TM - BLOCK)], gx_sems.at[slot])
            return pltpu.make_async_copy(
                dx_s.at[slot], gx_hbm.at[pl.ds(pl.multiple_of(step * TM - BLOCK, BLOCK), TM)], gx_sems.at[slot])

        @pl.when(i == 0)
        def _():
            dg_ref[...] = jnp.zeros_like(dg_ref)

        dproj_ref[:, 0:512] = dq_ref[...]
        dproj_ref[:, 512:768] = dkv_ref[...]
        dproj_ref[:, 768:1280] = dca_ref[...]
        dproj_ref[:, 1280:1792] = dcg_ref[...]

        @pl.when(i == 0)
        def _():
            dproj_ref[LEAD:BLOCK, 512:768] = dkvm_ref[...].astype(BF16)

        dhn = _dot(dproj_ref[...], w_ref[...], NN)
        h = h0_ref[...]
        r = _rms_stats(h)
        g = g_ref[...]
        hn_ref[...] = (h * r * g).astype(BF16)
        dx, dg = _rms_bwd(dhn, h, r, g)
        dg_ref[...] += dg
        slot = i % 2
        pl.when(i == 2)(lambda: gx_copy(0, 0, True).wait())
        pl.when(i > 2)(lambda: gx_copy(i - 2, slot, False).wait())
        dx_s[slot] = dh1_ref[...] + dx

        @pl.when(i == 0)
        def _():
            dmeta_ref[...] = dx_s[0, LEAD:BLOCK, :]
            gx_copy(0, 0, True).start()

        pl.when(i > 0)(lambda: gx_copy(i, slot, False).start())

        @pl.when(i == nt - 1)
        def _():
            gx_copy(nt - 2, (nt - 2) % 2, nt == 2).wait()
            gx_copy(nt - 1, (nt - 1) % 2, False).wait()

    return _pcall(
        body, name="inproj_bwd", grid=(nt,),
        in_specs=[_rows(TM, D_MODEL), _rows(TM, D_MODEL), _full((1, D_MODEL)), _rows(TM, 512), _rows(TM, 256),
                  _full((N_META, 256)), _rows(TM, 512), _rows(TM, 512), _full((1792, D_MODEL))],
        out_specs=[ANY, _full((N_META, D_MODEL)), _rows(TM, 1792), _rows(TM, D_MODEL), _full((1, D_MODEL))],
        out_shape=[jax.ShapeDtypeStruct((R - BLOCK, D_MODEL), F32), jax.ShapeDtypeStruct((N_META, D_MODEL), F32),
                   jax.ShapeDtypeStruct((R, 1792), BF16),
                   jax.ShapeDtypeStruct((R, D_MODEL), BF16), jax.ShapeDtypeStruct((1, D_MODEL), F32)],
        scratch_shapes=[pltpu.VMEM((2, TM, D_MODEL), F32), pltpu.SemaphoreType.DMA((2,))],
        semantics=("arbitrary",),
    )(dh1, h0, g1, dq, dkv, dkvm, dca, dcg, w_in_t)


ANY = pl.BlockSpec(memory_space=pl.ANY)


def _position():
    return lax.axis_index("x"), lax.axis_index("y"), lax.axis_index("c")


def _device_number(p):
    return 4 * p[0] + 2 * p[1] + p[2]


def _two_level_allgather(ins, outs, block, send_sems, recv_sems, local_sems, sem_base=0):
    n = len(ins)
    x, y, c = _position()
    me, sibling = (x, y, c), (x, y, 1 - c)
    chips = [(1 - x, y), (x, 1 - y), (1 - x, 1 - y)]

    def copy(w, k, origin, to, src=None):
        return pltpu.make_async_remote_copy(
            src_ref=block(w, origin) if src is None else src, dst_ref=block(w, origin),
            send_sem=send_sems.at[sem_base + 7 * w + k], recv_sem=recv_sems.at[sem_base + 7 * w + k],
            device_id=to, device_id_type=MESH)

    def mine(w):
        return pltpu.make_async_copy(ins[w], block(w, me), local_sems.at[w])

    def own(w):
        return [copy(w, 0, me, sibling, src=ins[w])] + [
            copy(w, 1 + j, me, (*chip, c), src=ins[w]) for j, chip in enumerate(chips)]

    def passed(w):
        return [copy(w, 4 + j, (*chip, c), sibling) for j, chip in enumerate(chips)]

    def start():
        for w in range(n):
            mine(w).start()
        for w in range(n):
            for cp in own(w):
                cp.start()

    def forward(w):
        fw = passed(w)
        for j, chip in enumerate(chips):
            copy(w, 1 + j, (*chip, c), me).wait_recv()
            fw[j].start()

    def finish():
        for w in range(n):
            copy(w, 0, sibling, me).wait_recv()
            for j, chip in enumerate(chips):
                copy(w, 4 + j, (*chip, 1 - c), me).wait_recv()
        for w in range(n):
            for cp in own(w) + passed(w):
                cp.wait_send()
            mine(w).wait()

    return start, forward, finish


def _blocking_allgather(ins, outs, block, send_sems, recv_sems, local_sems, sem_base=0):
    start, forward, finish = _two_level_allgather(ins, outs, block, send_sems, recv_sems, local_sems, sem_base)
    start()
    for w in range(len(ins)):
        forward(w)
    finish()


def _row_block(outs, rows):
    def block(w, p):
        return outs[w].at[pl.ds(pl.multiple_of(_device_number(p) * rows[w], 16), rows[w])]
    return block


def _sibling_copies(ins, outs, send_sems, recv_sems):
    x, y, c = _position()
    return [pltpu.make_async_remote_copy(
        src_ref=ins[w].at[:, 1 - c], dst_ref=outs[w], send_sem=send_sems.at[w], recv_sem=recv_sems.at[w],
        device_id=(x, y, 1 - c), device_id_type=MESH) for w in range(len(ins))]


def _chip_copies(ins, outs, send_sems, recv_sems):
    x, y, c = _position()
    chips = [(1 - x, y), (x, 1 - y), (1 - x, 1 - y)]
    return [pltpu.make_async_remote_copy(
        src_ref=ins[w].at[2 * chip[0] + chip[1]], dst_ref=outs[w].at[k],
        send_sem=send_sems.at[3 * w + k], recv_sem=recv_sems.at[3 * w + k],
        device_id=(*chip, c), device_id_type=MESH) for w in range(len(ins)) for k, chip in enumerate(chips)]


def _hosted(step, n_steps, make_copies):
    @pl.when(step == 0)
    def _():
        for cp in make_copies():
            cp.start()


def _hosted_wait(step, n_steps, make_copies):
    @pl.when(step == n_steps - 1)
    def _():
        for cp in make_copies():
            cp.wait()


def _sem_pair(n):
    return [pltpu.SemaphoreType.DMA((n,)), pltpu.SemaphoreType.DMA((n,))]


def _allgather_params(shards, small):
    arrays = list(shards) + list(small)
    n, ns = len(arrays), len(shards)

    def body(*refs):
        ins, outs = refs[:n], refs[n:2 * n]
        send_sems, recv_sems, local_sems = refs[2 * n:]

        rows = _row_block(outs, [a.shape[0] for a in arrays])

        def block(w, p):
            return rows(w, p) if w < ns else outs[w].at[_device_number(p)]

        _blocking_allgather(ins, outs, block, send_sems, recv_sems, local_sems)

    out_shape = [jax.ShapeDtypeStruct((N_DEV * a.shape[0], a.shape[1]), a.dtype) for a in shards]
    out_shape += [jax.ShapeDtypeStruct((N_DEV,) + a.shape, a.dtype) for a in small]
    return _pcall(
        body, name="allgather_params", in_specs=[ANY] * n, out_specs=[ANY] * n, out_shape=out_shape,
        scratch_shapes=[pltpu.SemaphoreType.DMA((7 * n,)), pltpu.SemaphoreType.DMA((7 * n,)),
                        pltpu.SemaphoreType.DMA((n,))],
    )(*arrays)


def _reduce_siblings(grads):
    n = len(grads)

    def body(*refs):
        ins, outs = refs[:n], refs[n:2 * n]
        send_sems, recv_sems = refs[2 * n:]
        copies = _sibling_copies(ins, outs, send_sems, recv_sems)
        for cp in copies:
            cp.start()
        for cp in copies:
            cp.wait()

    return _pcall(
        body, name="reduce_siblings", in_specs=[ANY] * n, out_specs=[ANY] * n,
        out_shape=_sibling_shapes(grads), scratch_shapes=_sem_pair(n),
    )(*grads)


def _sibling_shapes(grads):
    return [jax.ShapeDtypeStruct((4,) + g.shape[2:], F32) for g in grads]


def _chip_shapes(partials):
    return [jax.ShapeDtypeStruct((3,) + p.shape[1:], F32) for p in partials]


def _add_sibling(grad, received, core, name):
    _, _, r, cols = grad.shape

    def body(core_ref, g_ref, r_ref, o_ref):
        o_ref[...] = g_ref[...] + r_ref[...]

    return pl.pallas_call(
        body, name=name,
        grid_spec=pltpu.PrefetchScalarGridSpec(
            num_scalar_prefetch=1, grid=(4,),
            in_specs=[pl.BlockSpec((None, None, r, cols), lambda s, core_ref: (s, core_ref[0], 0, 0)),
                      pl.BlockSpec((None, r, cols), lambda s, core_ref: (s, 0, 0))],
            out_specs=pl.BlockSpec((None, r, cols), lambda s, core_ref: (s, 0, 0))),
        out_shape=jax.ShapeDtypeStruct((4, r, cols), F32),
        compiler_params=pltpu.CompilerParams(vmem_limit_bytes=VMEM_LIMIT),
    )(core, grad, received)


def _reduce_chips(partials, small):
    n, ns = len(partials), len(small)

    def body(*refs):
        p_ins, s_ins = refs[:n], refs[n:n + ns]
        p_outs, s_outs = refs[n + ns:2 * n + ns], refs[2 * n + ns:2 * (n + ns)]
        send_sems, recv_sems, local_sems = refs[2 * (n + ns):]
        copies = _chip_copies(p_ins, p_outs, send_sems, recv_sems)
        for cp in copies:
            cp.start()
        _blocking_allgather(s_ins, s_outs, lambda w, p: s_outs[w].at[_device_number(p)],
                            send_sems, recv_sems, local_sems, sem_base=3 * n)
        for cp in copies:
            cp.wait()

    out_shape = _chip_shapes(partials)
    out_shape += [jax.ShapeDtypeStruct((N_DEV,) + a.shape, a.dtype) for a in small]
    nsem = 3 * n + 7 * ns
    return _pcall(
        body, name="reduce_chips", in_specs=[ANY] * (n + ns), out_specs=[ANY] * (n + ns), out_shape=out_shape,
        scratch_shapes=[pltpu.SemaphoreType.DMA((nsem,)), pltpu.SemaphoreType.DMA((nsem,)),
                        pltpu.SemaphoreType.DMA((ns,))],
    )(*partials, *small)


def _sum_chips(partial, received, slot, name):
    _, r, cols = partial.shape

    def body(slot_ref, p_ref, r_ref, o_ref):
        o_ref[...] = p_ref[...] + r_ref[0] + r_ref[1] + r_ref[2]

    return pl.pallas_call(
        body, name=name,
        grid_spec=pltpu.PrefetchScalarGridSpec(
            num_scalar_prefetch=1, grid=(1,),
            in_specs=[pl.BlockSpec((None, r, cols), lambda i, slot_ref: (slot_ref[0], 0, 0)),
                      pl.BlockSpec((3, r, cols), lambda i, slot_ref: (0, 0, 0))],
            out_specs=pl.BlockSpec((r, cols), lambda i, slot_ref: (0, 0))),
        out_shape=jax.ShapeDtypeStruct((r, cols), F32),
        compiler_params=pltpu.CompilerParams(vmem_limit_bytes=VMEM_LIMIT),
    )(slot, partial, received)


def _adam(g, w, m, v):
    m = ADAM_B1 * m + (1.0 - ADAM_B1) * g
    v = ADAM_B2 * v + (1.0 - ADAM_B2) * (g * g)
    m_hat = m / (1.0 - ADAM_B1 ** ADAM_STEP)
    v_hat = v / (1.0 - ADAM_B2 ** ADAM_STEP)
    delta = -ADAM_LR * (m_hat / (jnp.sqrt(v_hat) + ADAM_EPS) + ADAM_WD * w)
    return delta, m, v


def _adamw(g, w, m, v, name):
    def body(g_ref, w_ref, m_ref, v_ref, d_ref, nm_ref, nv_ref):
        d_ref[...], nm_ref[...], nv_ref[...] = _adam(g_ref[...], w_ref[...], m_ref[...], v_ref[...])

    out = jax.ShapeDtypeStruct(w.shape, F32)
    return _pcall(body, name=name, out_shape=[out, out, out])(g, w, m, v)


def _adamw_small(dev, ga, gb, gc, params):
    names = ["meta", "attn_norm", "sinks", "conv_w", "conv_b", "ln_g", "ln_b", "attn_out", "conv_out",
             "ffn_norm", "final_norm"]
    flat = [a for p in params for a in p]
    n_in = len(flat)

    def body(dev_ref, ga_ref, gb_ref, gc_ref, *refs):
        ins, outs = refs[:n_in], refs[n_in:n_in + 4 * len(names)]
        loss_ref, sb, sc = refs[n_in + 4 * len(names):]
        a = ga_ref[0]
        sb[...] = gb_ref[0]
        sc[...] = gc_ref[0]
        for d in range(1, N_DEV):
            a = a + ga_ref[d]
            sb[...] += gb_ref[d]
            sc[...] += gc_ref[d]
        dev = dev_ref[0]
        grads = {
            "attn_norm": a[0:1, :], "ffn_norm": a[1:2, :], "final_norm": a[2:3, :],
            "conv_b": a[3:4, 0:512], "ln_g": a[3:4, 512:1024], "ln_b": a[4:5, 0:512],
            "attn_out": a[4:5, 512:1024], "conv_out": a[5:6, 0:512], "sinks": a[5:6, 512:512 + N_HEADS],
            "meta": sb[pl.ds(pl.multiple_of(dev * N_META, N_META), N_META), :],
            "conv_w": sc[pl.ds(pl.multiple_of(dev * 32, 32), 32), :][0:CONV_K, :],
        }
        for idx, nm in enumerate(names):
            w_ref, m_ref, v_ref = ins[3 * idx:3 * idx + 3]
            g = grads[nm]
            delta, m, v = _adam(g, w_ref[...], m_ref[...], v_ref[...])
            o = outs[4 * idx:4 * idx + 4]
            o[0][...], o[1][...], o[2][...], o[3][...] = g, delta, m, v
        loss_ref[...] = a[6:7, 0:1]

    vm = pl.BlockSpec(memory_space=pltpu.VMEM)
    out_shape = [jax.ShapeDtypeStruct(p[0].shape, F32) for p in params for _ in range(4)]
    out_shape.append(jax.ShapeDtypeStruct((1, 1), F32))
    res = pl.pallas_call(
        body, name="adamw_small",
        grid_spec=pltpu.PrefetchScalarGridSpec(
            num_scalar_prefetch=1, grid=(1,),
            in_specs=[pl.BlockSpec(ga.shape, lambda i, d: (0, 0, 0)), pl.BlockSpec(gb.shape, lambda i, d: (0, 0, 0)),
                      pl.BlockSpec(gc.shape, lambda i, d: (0, 0, 0))]
            + [pl.BlockSpec(a.shape, lambda i, d: (0, 0)) for a in flat],
            out_specs=[pl.BlockSpec(s.shape, lambda i, d: (0, 0)) for s in out_shape],
            scratch_shapes=[pltpu.VMEM(gb.shape[1:], F32), pltpu.VMEM(gc.shape[1:], F32)]),
        out_shape=out_shape,
        compiler_params=pltpu.CompilerParams(vmem_limit_bytes=VMEM_LIMIT),
    )(dev, ga, gb, gc, *flat)
    return [res[4 * i:4 * i + 4] for i in range(len(names))], res[-1]


def kernel(x, meta_tokens, attn_norm_g, w_in, attn_sinks, conv_w, conv_b, conv_ln_g, conv_ln_b, attn_out_g, conv_out_g, w_out, ffn_norm_g, w_gate, w_up, w_down, final_norm_g, loss_target, m_meta_tokens, m_attn_norm_g, m_w_in, m_attn_sinks, m_conv_w, m_conv_b, m_conv_ln_g, m_conv_ln_b, m_attn_out_g, m_conv_out_g, m_w_out, m_ffn_norm_g, m_w_gate, m_w_up, m_w_down, m_final_norm_g, v_meta_tokens, v_attn_norm_g, v_w_in, v_attn_sinks, v_conv_w, v_conv_b, v_conv_ln_g, v_conv_ln_b, v_attn_out_g, v_conv_out_g, v_w_out, v_ffn_norm_g, v_w_gate, v_w_up, v_w_down, v_final_norm_g):
    xi, yi, ci = _position()
    dev = jnp.reshape(_device_number((xi, yi, ci)), (1,)).astype(jnp.int32)
    core = jnp.reshape(ci, (1,)).astype(jnp.int32)
    slot = jnp.reshape(2 * xi + yi, (1,)).astype(jnp.int32)

    w_in_t, meta_st, convw_st = _allgather_params([w_in[0].T.astype(BF16)], [meta_tokens, conv_w[0]])
    later = [w_out[0].astype(BF16), w_gate[0].T.astype(BF16), w_up[0].T.astype(BF16), w_down[0].astype(BF16)]
    meta_full = jnp.transpose(meta_st, (1, 0, 2)).reshape(N_META, D_MODEL)
    convw_full = jnp.transpose(convw_st, (1, 0, 2)).reshape(CONV_K, CONV_W)

    h0 = jnp.concatenate([jnp.zeros((LEAD, D_MODEL), F32), meta_full, x[0]], axis=0)
    final_g = final_norm_g.reshape(1, D_MODEL)

    q, kv, ca, cg = _inproj_fwd(h0, attn_norm_g, w_in_t)
    o_attn, lse, (w_out_b, wg_t, wu_t, wd_b) = _attn_fwd(q, kv, attn_sinks, later)
    o_conv, y_conv = _conv_fwd(ca, cg, convw_full, conv_b, conv_ln_g, conv_ln_b)
    h1 = _outproj_fwd(h0, o_attn, o_conv, attn_out_g, conv_out_g, w_out_b)
    gate, up, dh2, loss_sum, dg_final = _ffn_fwd(h1, ffn_norm_g, wg_t, wu_t, wd_b, final_g, loss_target[0])

    def blocks(g):
        return g.reshape(4, 2, g.shape[0] // N_DEV, D_MODEL)

    def add_siblings(grads, received, tags):
        return [_add_sibling(g, r, core, "add_sibling_" + t) for g, r, t in zip(grads, received, tags)]

    dgate, dup, act, hn2, dh1, dg_ffn = _ffn_bwd(dh2, h1, ffn_norm_g, gate, up, wg_t, wu_t, wd_b)
    ffn_grads = [blocks(_wgrad(dgate, hn2, FF_CHUNK, "wgrad_gate")), blocks(_wgrad(dup, hn2, FF_CHUNK, "wgrad_up")),
                 blocks(_wgrad(act, dh2, FF_CHUNK, "wgrad_down"))]
    (do_attn, do_conv, mixed, dg_ao, dg_co), ffn_sib = _outproj_bwd(
        dh1, o_attn, o_conv, attn_out_g, conv_out_g, w_out_b, ffn_grads)
    ffn_sums = add_siblings(ffn_grads, ffn_sib, ("gate", "up", "down"))
    out_grads = [blocks(_wgrad(mixed, dh1, D_MODEL, "wgrad_out"))]
    (dca, dcg, dconvw, dconvb, dln_g, dln_b), ffn_chips = _conv_bwd(
        do_conv, y_conv, ca, cg, convw_full, conv_ln_g, conv_ln_b, ffn_sums)
    (dq, dkv, dkvm, dsinks), out_sib = _attn_bwd(q, kv, attn_sinks, o_attn, lse, do_attn, out_grads)
    out_sums = add_siblings(out_grads, out_sib, ("out",))
    grad_x, dmeta, dproj, hn1, dg_attn = _inproj_bwd(dh1, h0, attn_norm_g, dq, dkv, dkvm, dca, dcg, w_in_t)
    dwi_t, out_chips = _wgrad(dproj, hn1, 1792, "wgrad_in", out_sums)
    in_grads = [blocks(dwi_t)]
    in_sums = add_siblings(in_grads, _reduce_siblings(in_grads), ("in",))
    small_a = jnp.concatenate([
        dg_attn, dg_ffn, dg_final, jnp.concatenate([dconvb, dln_g], axis=1), jnp.concatenate([dln_b, dg_ao], axis=1),
        jnp.concatenate([dg_co, dsinks, jnp.zeros((1, 512 - N_HEADS), F32)], axis=1),
        jnp.concatenate([loss_sum[0:1, :], jnp.zeros((1, D_MODEL - 128), F32)], axis=1),
        jnp.zeros((1, D_MODEL), F32)], axis=0)
    small_b = jnp.transpose(dmeta.reshape(N_META, N_DEV, 128), (1, 0, 2)).reshape(N_DEV * N_META, 128)
    small_c = jnp.transpose(dconvw.reshape(32, N_DEV, 64), (1, 0, 2)).reshape(N_DEV * 32, 64)
    in_chips, ga, gb, gc = _reduce_chips(in_sums, [small_a, small_b, small_c])
    tags = ("in", "out", "gate", "up", "down")
    chip_sums = in_sums + out_sums + ffn_sums
    from_chips = [in_chips] + list(out_chips) + list(ffn_chips)
    g_own = [_sum_chips(p, r, slot, "sum_chips_" + t) for p, r, t in zip(chip_sums, from_chips, tags)]
    g_big = [g_own[0].T, g_own[1], g_own[2].T, g_own[3].T, g_own[4]]

    big = [(w_in, m_w_in, v_w_in), (w_out, m_w_out, v_w_out), (w_gate, m_w_gate, v_w_gate), (w_up, m_w_up, v_w_up),
           (w_down, m_w_down, v_w_down)]
    big_out = {}
    for t, g, (w, m, v) in zip(tags, g_big, big):
        delta, nm, nv = _adamw(g, w[0], m[0], v[0], "adamw_" + t)
        big_out[t] = (g[None], delta[None], nm[None], nv[None])

    small_params = [
        (meta_tokens, m_meta_tokens, v_meta_tokens), (attn_norm_g, m_attn_norm_g, v_attn_norm_g),
        (attn_sinks, m_attn_sinks, v_attn_sinks), (conv_w[0], m_conv_w[0], v_conv_w[0]),
        (conv_b, m_conv_b, v_conv_b), (conv_ln_g, m_conv_ln_g, v_conv_ln_g), (conv_ln_b, m_conv_ln_b, v_conv_ln_b),
        (attn_out_g, m_attn_out_g, v_attn_out_g), (conv_out_g, m_conv_out_g, v_conv_out_g),
        (ffn_norm_g, m_ffn_norm_g, v_ffn_norm_g),
        (final_g, m_final_norm_g.reshape(1, D_MODEL), v_final_norm_g.reshape(1, D_MODEL))]
    sm, loss = _adamw_small(dev, ga, gb, gc, small_params)
    sm[3] = [a[None] for a in sm[3]]
    sm[10] = [a.reshape(D_MODEL) for a in sm[10]]

    per_param = [sm[0], sm[1], big_out["in"], sm[2], sm[3], sm[4], sm[5], sm[6], sm[7], sm[8], big_out["out"],
                 sm[9], big_out["gate"], big_out["up"], big_out["down"], sm[10]]
    loss = loss.reshape(())
    outs = [loss, grad_x[None]]
    for kind in range(4):
        outs += [p[kind] for p in per_param]
    return tuple(outs)
```

```python
import functools
import math

import jax
import jax.numpy as jnp
from jax import lax
from jax.experimental import pallas as pl
from jax.experimental.pallas import tpu as pltpu

F32, BF16 = jnp.float32, jnp.bfloat16
MESH = pl.DeviceIdType.MESH

D_MODEL = 1024
N_META = 16
BLOCK = 128
LEAD = BLOCK - N_META
HEAD_DIM = 64
N_HEADS = 8
GROUP = 4
ATTN_W = 512
KV_W = 128
CONV_W = 512
CONV_K = 31
HALO = 32
D_FF = 2816
FF_CHUNK = D_FF // 2
FF_SUB = [slice(s, min(s + 256, FF_CHUNK)) for s in range(0, FF_CHUNK, 256)]
N_DEV = 8
EPS = 1e-5
NEG = -1e30
TM = 640
WGRAD_K_TILES = 5
CONV_ROWS = 64
VMEM_LIMIT = 56 * 1024 * 1024

ADAM_LR, ADAM_B1, ADAM_B2, ADAM_EPS, ADAM_WD, ADAM_STEP = 0.001, 0.9, 0.999, 1e-08, 0.01, 10

NT = (((1,), (1,)), ((), ()))
NN = (((1,), (0,)), ((), ()))
TN = (((0,), (0,)), ((), ()))


def _dot(a, b, dims):
    return lax.dot_general(a, b, dims, preferred_element_type=F32)


def _sigmoid(x):
    return 1.0 / (1.0 + jnp.exp(-x))


def _pcall(body, *, name, out_shape, grid=None, in_specs=None, out_specs=None, scratch_shapes=(),
           semantics=None, **kw):
    params = dict(vmem_limit_bytes=VMEM_LIMIT)
    if semantics is not None:
        params["dimension_semantics"] = semantics
    extra = {}
    if grid is not None:
        extra["grid"] = grid
    if in_specs is not None:
        extra["in_specs"] = in_specs
    if out_specs is not None:
        extra["out_specs"] = out_specs
    return pl.pallas_call(body, name=name, out_shape=out_shape, scratch_shapes=list(scratch_shapes),
                          compiler_params=pltpu.CompilerParams(**params), **extra, **kw)


def _rows(tm, cols, off=0):
    return pl.BlockSpec((tm, cols), lambda i, *_: (i + off, 0))


def _full(shape):
    nd = len(shape)
    return pl.BlockSpec(shape, lambda *_: (0,) * nd)


def _rms_stats(x):
    return lax.rsqrt(jnp.mean(x * x, axis=-1, keepdims=True) + EPS)


def _rms_bwd(dy, x, r, g):
    t = dy * g
    dx = r * (t - x * (r * r) * jnp.mean(t * x, axis=-1, keepdims=True))
    dg = jnp.sum(dy * x * r, axis=0, keepdims=True)
    return dx, dg


def _inproj_fwd(h0, g1, w_in_t):
    R = h0.shape[0]

    def body(h_ref, g_ref, w_ref, q_ref, kv_ref, ca_ref, cg_ref):
        h = h_ref[...]
        hn = (h * _rms_stats(h) * g_ref[...]).astype(BF16)
        q_ref[...] = _dot(hn, w_ref[0:512, :], NT).astype(BF16)
        kv_ref[...] = _dot(hn, w_ref[512:768, :], NT).astype(BF16)
        ca_ref[...] = _dot(hn, w_ref[768:1280, :], NT)
        cg_ref[...] = _dot(hn, w_ref[1280:1792, :], NT)

    return _pcall(
        body, name="inproj_fwd", grid=(R // TM,),
        in_specs=[_rows(TM, D_MODEL), _full((1, D_MODEL)), _full((1792, D_MODEL))],
        out_specs=[_rows(TM, 512), _rows(TM, 256), _rows(TM, 512), _rows(TM, 512)],
        out_shape=[jax.ShapeDtypeStruct((R, 512), BF16), jax.ShapeDtypeStruct((R, 256), BF16),
                   jax.ShapeDtypeStruct((R, 512), F32), jax.ShapeDtypeStruct((R, 512), F32)],
        semantics=("parallel",),
    )(h0, g1, w_in_t)


GB = GROUP * BLOCK
ATTN_SCALE = 1.0 / math.sqrt(HEAD_DIM)


def _stack_heads(ref, g):
    return jnp.concatenate(
        [ref[:, HEAD_DIM * (GROUP * g + j):HEAD_DIM * (GROUP * g + j + 1)] for j in range(GROUP)], axis=0)


def _unstack_heads(x):
    return jnp.concatenate([x[BLOCK * j:BLOCK * (j + 1), :] for j in range(GROUP)], axis=1)


def _with_ones(v):
    return jnp.concatenate([v, jnp.ones(v.shape, v.dtype)], axis=1)


def _attn_tables(sink_ref, bias_s, sink_s):
    ii = lax.broadcasted_iota(jnp.int32, (BLOCK, BLOCK), 0)
    jj = lax.broadcasted_iota(jnp.int32, (BLOCK, BLOCK), 1)
    dist = jnp.where(jj <= ii, ii - jj, ii - jj + BLOCK).astype(F32)
    for h in range(N_HEADS):
        bias_s[BLOCK * h:BLOCK * (h + 1), :] = dist * -(2.0 ** -(h + 1))
        sink_s[BLOCK * h:BLOCK * (h + 1), :] = jnp.zeros((BLOCK, 1), F32) + sink_ref[0, h]


def _attn_masks(b):
    ii = lax.broadcasted_iota(jnp.int32, (GB, BLOCK), 0) & (BLOCK - 1)
    jj = lax.broadcasted_iota(jnp.int32, (GB, BLOCK), 1)
    sel = jj <= ii
    pen = jnp.where(sel, jnp.where(b >= 1, 0.0, NEG), jnp.where(b >= 2, 0.0, NEG))
    mi = lax.broadcasted_iota(jnp.int32, (GB, N_META), 0) & (BLOCK - 1)
    mj = lax.broadcasted_iota(jnp.int32, (GB, N_META), 1)
    pen_m = jnp.where((mj + LEAD) <= (mi + b * BLOCK), 0.0, NEG)
    return sel, pen, pen_m


def _attn_scores(qg, kc, kp, km, sel, pen, pen_m, bias):
    s_b = jnp.where(sel, _dot(qg, kc, NT), _dot(qg, kp, NT)) + bias + pen
    s_m = _dot(qg, km, NT) + pen_m
    return s_b, s_m


def _attn_fwd(q, kv, sinks, shards):
    R = q.shape[0]
    nb = R // BLOCK
    ns = len(shards)
    forward_steps = [(w + 1) * (nb - 8) // (ns + 1) for w in range(ns)]

    def body(sink_ref, q_ref, kvc_ref, kvp_ref, kvm_ref, *refs):
        ag_ins, (o_ref, lse_ref), ag_outs = refs[:ns], refs[ns:ns + 2], refs[ns + 2:2 * ns + 2]
        send_sems, recv_sems, local_sems, bias_s, sink_s = refs[2 * ns + 2:]
        b = pl.program_id(0)
        ag_start, ag_forward, ag_finish = _two_level_allgather(
            ag_ins, ag_outs, _row_block(ag_outs, [s.shape[0] for s in shards]), send_sems, recv_sems, local_sems)
        pl.when(b == 0)(ag_start)
        for w, step in enumerate(forward_steps):
            pl.when(b == step)(functools.partial(ag_forward, w))
        pl.when(b == 0)(functools.partial(_attn_tables, sink_ref, bias_s, sink_s))
        sel, pen, pen_m = _attn_masks(b)
        for g in range(N_HEADS // GROUP):
            ks, vs = slice(HEAD_DIM * g, HEAD_DIM * (g + 1)), slice(KV_W + HEAD_DIM * g, KV_W + HEAD_DIM * (g + 1))
            rows = slice(GB * g, GB * (g + 1))
            qg = _stack_heads(q_ref, g) * ATTN_SCALE
            s_b, s_m = _attn_scores(qg, kvc_ref[:, ks], kvp_ref[:, ks], kvm_ref[LEAD:BLOCK, ks],
                                    sel, pen, pen_m, bias_s[rows, :])
            sink = sink_s[rows, :]
            m = jnp.maximum(jnp.maximum(jnp.max(s_b, axis=-1, keepdims=True),
                                        jnp.max(s_m, axis=-1, keepdims=True)), sink)
            p_b = jnp.exp(s_b - m)
            p_m = jnp.exp(s_m - m)
            p_c = jnp.where(sel, p_b, 0.0).astype(BF16)
            p_p = jnp.where(sel, 0.0, p_b).astype(BF16)
            ol = (_dot(p_c, _with_ones(kvc_ref[:, vs]), NN) + _dot(p_p, _with_ones(kvp_ref[:, vs]), NN)
                  + _dot(p_m.astype(BF16), _with_ones(kvm_ref[LEAD:BLOCK, vs]), NN))
            l = ol[:, HEAD_DIM:HEAD_DIM + 1] + jnp.exp(sink - m)
            o_ref[:, GROUP * HEAD_DIM * g:GROUP * HEAD_DIM * (g + 1)] = _unstack_heads(ol[:, 0:HEAD_DIM] / l)
            lse = m + jnp.log(l)
            for j in range(GROUP):
                lse_ref[:, GROUP * g + j:GROUP * g + j + 1] = lse[BLOCK * j:BLOCK * (j + 1), :]
        pl.when(b == nb - 1)(ag_finish)

    res = _pcall(
        body, name="attn_fwd", grid=(nb,),
        in_specs=[pl.BlockSpec(memory_space=pltpu.SMEM),
                  _rows(BLOCK, 512), _rows(BLOCK, 256),
                  pl.BlockSpec((BLOCK, 256), lambda b: (jnp.maximum(b - 1, 0), 0)),
                  _full((BLOCK, 256))] + [ANY] * ns,
        out_specs=[_rows(BLOCK, 512), _rows(BLOCK, N_HEADS)] + [ANY] * ns,
        out_shape=[jax.ShapeDtypeStruct((R, 512), F32), jax.ShapeDtypeStruct((R, N_HEADS), F32)]
        + [jax.ShapeDtypeStruct((N_DEV * s.shape[0], s.shape[1]), s.dtype) for s in shards],
        scratch_shapes=_sem_pair(7 * ns) + [pltpu.SemaphoreType.DMA((ns,)),
                                            pltpu.VMEM((N_HEADS * BLOCK, BLOCK), F32),
                                            pltpu.VMEM((N_HEADS * BLOCK, 1), F32)],
        semantics=("arbitrary",),
    )(sinks, q, kv, kv, kv, *shards)
    return res[0], res[1], res[2:]


def _ln_silu(y, lg, lb):
    mu = jnp.mean(y, axis=-1, keepdims=True)
    xc = y - mu
    rstd = lax.rsqrt(jnp.mean(xc * xc, axis=-1, keepdims=True) + EPS)
    xhat = xc * rstd
    yn = xhat * lg + lb
    return yn, xhat, rstd


PHASE_ROWS = HALO + TM - 8


def _phase_scratch():
    return pltpu.VMEM((7, PHASE_ROWS, CONV_W), F32)


def _phase_copies(src_s, ph_s):
    for b in range(1, 8):
        ph_s[b - 1] = src_s[pl.ds(b, PHASE_ROWS), :]


def _shifted(src_s, ph_s, start, rows):
    a8, b = (start // 8) * 8, start % 8
    if b == 0:
        return src_s[pl.ds(a8, rows), :]
    return ph_s[b - 1, pl.ds(a8, rows), :]


def _conv_fwd(ca, cg, conv_w, conv_b, ln_g, ln_b):
    R = ca.shape[0]
    nt = R // TM
    hpt = TM // HALO

    def body(ca_ref, cg_ref, cah_ref, cgh_ref, w_ref, b_ref, lg_ref, lb_ref, oc_ref, y_ref, u_s, uph_s):
        i = pl.program_id(0)
        u_s[HALO:HALO + TM, :] = ca_ref[...] * _sigmoid(cg_ref[...])
        u_s[0:HALO, :] = jnp.where(i > 0, cah_ref[...] * _sigmoid(cgh_ref[...]), 0.0)
        _phase_copies(u_s, uph_s)
        for rc in range(TM // CONV_ROWS):
            base = rc * CONV_ROWS + HALO - (CONV_K - 1)
            acc = jnp.zeros((CONV_ROWS, CONV_W), F32) + b_ref[...]
            for k in range(CONV_K):
                acc = acc + _shifted(u_s, uph_s, base + k, CONV_ROWS) * w_ref[k:k + 1, :]
            rows = slice(rc * CONV_ROWS, (rc + 1) * CONV_ROWS)
            y_ref[rows, :] = acc
            yn, _, _ = _ln_silu(acc, lg_ref[...], lb_ref[...])
            oc_ref[rows, :] = yn * _sigmoid(yn)

    prev_halo = pl.BlockSpec((HALO, CONV_W), lambda i: (jnp.maximum(i * hpt - 1, 0), 0))
    return _pcall(
        body, name="conv_fwd", grid=(nt,),
        in_specs=[_rows(TM, CONV_W), _rows(TM, CONV_W), prev_halo, prev_halo,
                  _full((CONV_K, CONV_W)), _full((1, CONV_W)), _full((1, CONV_W)), _full((1, CONV_W))],
        out_specs=[_rows(TM, CONV_W), _rows(TM, CONV_W)],
        out_shape=[jax.ShapeDtypeStruct((R, CONV_W), F32), jax.ShapeDtypeStruct((R, CONV_W), F32)],
        scratch_shapes=[pltpu.VMEM((HALO + TM, CONV_W), F32), _phase_scratch()],
        semantics=("parallel",),
    )(ca, cg, ca, cg, conv_w, conv_b, ln_g, ln_b)


def _outproj_fwd(h0, o_attn, o_conv, ga, gc, w_out):
    R = h0.shape[0]

    def body(h_ref, oa_ref, oc_ref, ga_ref, gc_ref, w_ref, h1_ref):
        oa, oc = oa_ref[...], oc_ref[...]
        ma = (oa * _rms_stats(oa) * ga_ref[...]).astype(BF16)
        mc = (oc * _rms_stats(oc) * gc_ref[...]).astype(BF16)
        h1_ref[...] = h_ref[...] + _dot(ma, w_ref[0:512, :], NN) + _dot(mc, w_ref[512:1024, :], NN)

    return _pcall(
        body, name="outproj_fwd", grid=(R // TM,),
        in_specs=[_rows(TM, D_MODEL), _rows(TM, 512), _rows(TM, 512), _full((1, 512)), _full((1, 512)),
                  _full((D_MODEL, D_MODEL))],
        out_specs=_rows(TM, D_MODEL),
        out_shape=jax.ShapeDtypeStruct((R, D_MODEL), F32),
        semantics=("parallel",),
    )(h0, o_attn, o_conv, ga, gc, w_out)


def _target_copy(tgt_hbm, tgt_s, sem, i, first):
    if first:
        return pltpu.make_async_copy(tgt_hbm.at[pl.ds(0, TM - BLOCK)], tgt_s.at[pl.ds(BLOCK, TM - BLOCK)], sem)
    return pltpu.make_async_copy(tgt_hbm.at[pl.ds(i * TM - BLOCK, TM)], tgt_s, sem)


def _ffn_fwd(h1, g2, wg_t, wu_t, wd, gf, target):
    R = h1.shape[0]
    nt, nj = R // TM, D_FF // FF_CHUNK

    def body(h1_ref, g2_ref, wg_ref, wu_ref, wd_ref, gf_ref, tgt_hbm,
             gate_ref, up_ref, dh2_ref, loss_ref, dgf_ref, hn_s, acc_s, tgt_s, sem, act_s):
        i, j = pl.program_id(0), pl.program_id(1)

        @pl.when((i == 0) & (j == 0))
        def _():
            loss_ref[...] = jnp.zeros_like(loss_ref)
            dgf_ref[...] = jnp.zeros_like(dgf_ref)

        @pl.when(j == 0)
        def _():
            h1 = h1_ref[...]
            hn_s[...] = (h1 * _rms_stats(h1) * g2_ref[...]).astype(BF16)

            @pl.when(i == 0)
            def _():
                tgt_s[0:BLOCK, :] = jnp.zeros((BLOCK, D_MODEL), F32)
                _target_copy(tgt_hbm, tgt_s, sem, i, True).start()

            @pl.when(i > 0)
            def _():
                _target_copy(tgt_hbm, tgt_s, sem, i, False).start()

        hn = hn_s[...]
        for cs in FF_SUB:
            gate = _dot(hn, wg_ref[cs, :], NT)
            up = _dot(hn, wu_ref[cs, :], NT)
            gate_ref[:, cs] = gate.astype(BF16)
            up_ref[:, cs] = up.astype(BF16)
            act_s[:, cs] = (gate * _sigmoid(gate) * up).astype(BF16)
        part = _dot(act_s[...], wd_ref[...], NN)

        @pl.when(j == 0)
        def _():
            acc_s[...] = part

        @pl.when(j == nj - 1)
        def _():
            @pl.when(i == 0)
            def _():
                _target_copy(tgt_hbm, tgt_s, sem, i, True).wait()

            @pl.when(i > 0)
            def _():
                _target_copy(tgt_hbm, tgt_s, sem, i, False).wait()

            h2 = h1_ref[...] + acc_s[...] + part
            rf = _rms_stats(h2)
            gf = gf_ref[...]
            row = lax.broadcasted_iota(jnp.int32, (TM, 1), 0) + i * TM
            err = jnp.where(row >= BLOCK, h2 * rf * gf - tgt_s[...], 0.0)
            dy = err * (1.0 / D_MODEL)
            dh2, dgf = _rms_bwd(dy, h2, rf, gf)
            dh2_ref[...] = dh2
            loss_ref[...] += (0.5 / D_MODEL) * jnp.sum(err * err)
            dgf_ref[...] += dgf

    wspec = pl.BlockSpec((FF_CHUNK, D_MODEL), lambda i, j: (j, 0))
    aspec = pl.BlockSpec((TM, FF_CHUNK), lambda i, j: (i, j))
    return _pcall(
        body, name="ffn_fwd", grid=(nt, nj),
        in_specs=[_rows(TM, D_MODEL), _full((1, D_MODEL)), wspec, wspec, wspec, _full((1, D_MODEL)),
                  pl.BlockSpec(memory_space=pl.ANY)],
        out_specs=[aspec, aspec, _rows(TM, D_MODEL),
                   _full((8, 128)), _full((1, D_MODEL))],
        out_shape=[jax.ShapeDtypeStruct((R, D_FF), BF16), jax.ShapeDtypeStruct((R, D_FF), BF16),
                   jax.ShapeDtypeStruct((R, D_MODEL), F32),
                   jax.ShapeDtypeStruct((8, 128), F32), jax.ShapeDtypeStruct((1, D_MODEL), F32)],
        scratch_shapes=[pltpu.VMEM((TM, D_MODEL), BF16), pltpu.VMEM((TM, D_MODEL), F32),
                        pltpu.VMEM((TM, D_MODEL), F32), pltpu.SemaphoreType.DMA, pltpu.VMEM((TM, FF_CHUNK), BF16)],
        semantics=("arbitrary", "arbitrary"),
    )(h1, g2, wg_t, wu_t, wd, gf, target)


def _ffn_bwd(dh2, h1, g2, gate, up, wg_t, wu_t, wd):
    R = h1.shape[0]
    nt, nj = R // TM, D_FF // FF_CHUNK

    wspec = pl.BlockSpec((FF_CHUNK, D_MODEL), lambda i, j: (j, 0))
    aspec = pl.BlockSpec((TM, FF_CHUNK), lambda i, j: (i, j))
    act_shape = jax.ShapeDtypeStruct((R, D_FF), BF16)

    def act_body(dh2_ref, gate_ref, up_ref, wd_ref, dgate_ref, dup_ref, act_ref, dhb_s):
        @pl.when(pl.program_id(1) == 0)
        def _():
            dhb_s[...] = dh2_ref[...].astype(BF16)

        dhb = dhb_s[...]
        for cs in FF_SUB:
            dact = _dot(dhb, wd_ref[cs, :], NT)
            gate = gate_ref[:, cs].astype(F32)
            up = up_ref[:, cs].astype(F32)
            sig = _sigmoid(gate)
            silu = gate * sig
            dgate_ref[:, cs] = (dact * up * (sig * (1.0 + gate * (1.0 - sig)))).astype(BF16)
            dup_ref[:, cs] = (dact * silu).astype(BF16)
            act_ref[:, cs] = (silu * up).astype(BF16)

    dgate, dup, act = _pcall(
        act_body, name="ffn_bwd_act", grid=(nt, nj),
        in_specs=[_rows(TM, D_MODEL), aspec, aspec, wspec],
        out_specs=[aspec, aspec, aspec], out_shape=[act_shape, act_shape, act_shape],
        scratch_shapes=[pltpu.VMEM((TM, D_MODEL), BF16)],
        semantics=("parallel", "arbitrary"),
    )(dh2, gate, up, wd)

    def in_body(dh2_ref, h1_ref, g2_ref, dgate_ref, dup_ref, wg_ref, wu_ref, hn_ref, dh1_ref, dg2_ref, acc_s):
        i, j = pl.program_id(0), pl.program_id(1)

        @pl.when((i == 0) & (j == 0))
        def _():
            dg2_ref[...] = jnp.zeros_like(dg2_ref)

        part = _dot(dgate_ref[...], wg_ref[...], NN) + _dot(dup_ref[...], wu_ref[...], NN)

        @pl.when(j == 0)
        def _():
            acc_s[...] = part

        @pl.when(j == nj - 1)
        def _():
            h1 = h1_ref[...]
            r = _rms_stats(h1)
            g2 = g2_ref[...]
            hn_ref[...] = (h1 * r * g2).astype(BF16)
            dx, dg = _rms_bwd(acc_s[...] + part, h1, r, g2)
            dh1_ref[...] = dh2_ref[...] + dx
            dg2_ref[...] += dg

    hn2, dh1, dg2 = _pcall(
        in_body, name="ffn_bwd_in", grid=(nt, nj),
        in_specs=[_rows(TM, D_MODEL), _rows(TM, D_MODEL), _full((1, D_MODEL)), aspec, aspec, wspec, wspec],
        out_specs=[_rows(TM, D_MODEL), _rows(TM, D_MODEL), _full((1, D_MODEL))],
        out_shape=[jax.ShapeDtypeStruct((R, D_MODEL), BF16), jax.ShapeDtypeStruct((R, D_MODEL), F32),
                   jax.ShapeDtypeStruct((1, D_MODEL), F32)],
        scratch_shapes=[pltpu.VMEM((TM, D_MODEL), F32)],
        semantics=("arbitrary", "arbitrary"),
    )(dh2, h1, g2, dgate, dup, wg_t, wu_t)
    return dgate, dup, act, hn2, dh1, dg2


def _wgrad(a, b, tm, name, partials=()):
    K, M = a.shape
    N = b.shape[1]
    tk = K // WGRAD_K_TILES if K % (WGRAD_K_TILES * BLOCK) == 0 else TM
    nm, nk, npart = M // tm, K // tk, len(partials)

    def body(a_ref, b_ref, *refs):
        p_ins, o_ref, p_outs, sems = refs[:npart], refs[npart], refs[npart + 1:2 * npart + 1], refs[2 * npart + 1:]
        step = pl.program_id(0) * nk + pl.program_id(1)
        exchange = functools.partial(_chip_copies, p_ins, p_outs, *sems)
        if npart:
            _hosted(step, nm * nk, exchange)

        @pl.when(pl.program_id(1) == 0)
        def _():
            o_ref[...] = jnp.zeros_like(o_ref)

        o_ref[...] += _dot(a_ref[...], b_ref[...].astype(BF16), TN)
        if npart:
            _hosted_wait(step, nm * nk, exchange)

    res = _pcall(
        body, name=name, grid=(nm, nk),
        in_specs=[pl.BlockSpec((tk, tm), lambda m, k: (k, m)), pl.BlockSpec((tk, N), lambda m, k: (k, 0))]
        + [ANY] * npart,
        out_specs=[pl.BlockSpec((tm, N), lambda m, k: (m, 0))] + [ANY] * npart,
        out_shape=[jax.ShapeDtypeStruct((M, N), F32)] + _chip_shapes(partials),
        scratch_shapes=_sem_pair(3 * npart) if npart else [],
        semantics=("arbitrary", "arbitrary"),
    )(a, b, *partials)
    return (res[0], res[1:]) if npart else res[0]


def _outproj_bwd(dh1, o_attn, o_conv, ga, gc, w_out, grads):
    R = dh1.shape[0]
    nt, ng = R // TM, len(grads)

    def body(dh1_ref, oa_ref, oc_ref, ga_ref, gc_ref, w_ref, *refs):
        g_ins, (doa_ref, doc_ref, mixed_ref, dga_ref, dgc_ref) = refs[:ng], refs[ng:ng + 5]
        g_outs, (send_sems, recv_sems) = refs[ng + 5:2 * ng + 5], refs[2 * ng + 5:]
        exchange = functools.partial(_sibling_copies, g_ins, g_outs, send_sems, recv_sems)
        _hosted(pl.program_id(0), nt, exchange)

        @pl.when(pl.program_id(0) == 0)
        def _():
            dga_ref[...] = jnp.zeros_like(dga_ref)
            dgc_ref[...] = jnp.zeros_like(dgc_ref)

        dm = _dot(dh1_ref[...].astype(BF16), w_ref[...], NT)
        oa, oc = oa_ref[...], oc_ref[...]
        ra, rc = _rms_stats(oa), _rms_stats(oc)
        mixed_ref[:, 0:512] = (oa * ra * ga_ref[...]).astype(BF16)
        mixed_ref[:, 512:1024] = (oc * rc * gc_ref[...]).astype(BF16)
        doa, dga = _rms_bwd(dm[:, 0:512], oa, ra, ga_ref[...])
        doc, dgc = _rms_bwd(dm[:, 512:1024], oc, rc, gc_ref[...])
        doa_ref[...] = doa
        doc_ref[...] = doc
        dga_ref[...] += dga
        dgc_ref[...] += dgc
        _hosted_wait(pl.program_id(0), nt, exchange)

    res = _pcall(
        body, name="outproj_bwd", grid=(nt,),
        in_specs=[_rows(TM, D_MODEL), _rows(TM, 512), _rows(TM, 512), _full((1, 512)), _full((1, 512)),
                  _full((D_MODEL, D_MODEL))] + [ANY] * ng,
        out_specs=[_rows(TM, 512), _rows(TM, 512), _rows(TM, D_MODEL), _full((1, 512)), _full((1, 512))]
        + [ANY] * ng,
        out_shape=[jax.ShapeDtypeStruct((R, 512), F32), jax.ShapeDtypeStruct((R, 512), F32),
                   jax.ShapeDtypeStruct((R, D_MODEL), BF16),
                   jax.ShapeDtypeStruct((1, 512), F32), jax.ShapeDtypeStruct((1, 512), F32)]
        + _sibling_shapes(grads),
        scratch_shapes=_sem_pair(ng),
        semantics=("arbitrary",),
    )(dh1, o_attn, o_conv, ga, gc, w_out, *grads)
    return res[:5], res[5:]


def _conv_bwd(do_conv, y, ca, cg, conv_w, ln_g, ln_b, partials):
    R = ca.shape[0]
    nt = R // TM
    hpt = TM // HALO
    npart = len(partials)

    def body(do_ref, doh_ref, y_ref, yh_ref, ca_ref, cg_ref, cah_ref, cgh_ref, w_ref, lg_ref, lb_ref, *refs):
        p_ins, (dca_ref, dcg_ref, dw_ref, db_ref, dlg_ref, dlb_ref) = refs[:npart], refs[npart:npart + 6]
        p_outs, (send_sems, recv_sems, u_s, dy_s, uph_s, dyph_s) = refs[npart + 6:2 * npart + 6], refs[2 * npart + 6:]
        i = pl.program_id(0)
        exchange = functools.partial(_chip_copies, p_ins, p_outs, send_sems, recv_sems)
        _hosted(i, nt, exchange)

        @pl.when(i == 0)
        def _():
            dw_ref[...] = jnp.zeros_like(dw_ref)
            db_ref[...] = jnp.zeros_like(db_ref)
            dlg_ref[...] = jnp.zeros_like(dlg_ref)
            dlb_ref[...] = jnp.zeros_like(dlb_ref)

        lg, lb = lg_ref[...], lb_ref[...]

        def ln_bwd(yv, dov):
            yn, xhat, rstd = _ln_silu(yv, lg, lb)
            sig = _sigmoid(yn)
            dyn = dov * (sig * (1.0 + yn * (1.0 - sig)))
            dxh = dyn * lg
            dyv = rstd * (dxh - jnp.mean(dxh, axis=-1, keepdims=True)
                          - xhat * jnp.mean(dxh * xhat, axis=-1, keepdims=True))
            return dyv, dyn, xhat

        dyv, dyn, xhat = ln_bwd(y_ref[...], do_ref[...])
        dy_s[0:TM, :] = dyv
        dlg_ref[...] += jnp.sum(dyn * xhat, axis=0, keepdims=True)
        dlb_ref[...] += jnp.sum(dyn, axis=0, keepdims=True)
        db_ref[...] += jnp.sum(dyv, axis=0, keepdims=True)
        dyh, _, _ = ln_bwd(yh_ref[...], doh_ref[...])
        dy_s[TM:TM + HALO, :] = jnp.where(i < nt - 1, dyh, 0.0)
        u_s[HALO:HALO + TM, :] = ca_ref[...] * _sigmoid(cg_ref[...])
        u_s[0:HALO, :] = jnp.where(i > 0, cah_ref[...] * _sigmoid(cgh_ref[...]), 0.0)
        _phase_copies(dy_s, dyph_s)
        _phase_copies(u_s, uph_s)

        for rc in range(TM // CONV_ROWS):
            acc = jnp.zeros((CONV_ROWS, CONV_W), F32)
            for k in range(CONV_K):
                acc = acc + _shifted(dy_s, dyph_s, rc * CONV_ROWS + CONV_K - 1 - k, CONV_ROWS) * w_ref[k:k + 1, :]
            rows = slice(rc * CONV_ROWS, (rc + 1) * CONV_ROWS)
            sg = _sigmoid(cg_ref[rows, :])
            dca_ref[rows, :] = (acc * sg).astype(BF16)
            dcg_ref[rows, :] = (acc * ca_ref[rows, :] * sg * (1.0 - sg)).astype(BF16)

        for k in range(CONV_K):
            prod = _shifted(u_s, uph_s, HALO - (CONV_K - 1) + k, TM) * dy_s[0:TM, :]
            dw_ref[k:k + 1, :] += jnp.sum(prod, axis=0, keepdims=True)
        _hosted_wait(i, nt, exchange)

    prev_halo = pl.BlockSpec((HALO, CONV_W), lambda i: (jnp.maximum(i * hpt - 1, 0), 0))
    next_halo = pl.BlockSpec((HALO, CONV_W), lambda i: (jnp.minimum((i + 1) * hpt, nt * hpt - 1), 0))
    vec = jax.ShapeDtypeStruct((1, CONV_W), F32)
    res = _pcall(
        body, name="conv_bwd", grid=(nt,),
        in_specs=[_rows(TM, CONV_W), next_halo, _rows(TM, CONV_W), next_halo,
                  _rows(TM, CONV_W), _rows(TM, CONV_W), prev_halo, prev_halo,
                  _full((CONV_K, CONV_W)), _full((1, CONV_W)), _full((1, CONV_W))] + [ANY] * npart,
        out_specs=[_rows(TM, CONV_W), _rows(TM, CONV_W), _full((32, CONV_W)),
                   _full((1, CONV_W)), _full((1, CONV_W)), _full((1, CONV_W))] + [ANY] * npart,
        out_shape=[jax.ShapeDtypeStruct((R, CONV_W), BF16), jax.ShapeDtypeStruct((R, CONV_W), BF16),
                   jax.ShapeDtypeStruct((32, CONV_W), F32), vec, vec, vec] + _chip_shapes(partials),
        scratch_shapes=_sem_pair(3 * npart)
        + [pltpu.VMEM((HALO + TM, CONV_W), F32), pltpu.VMEM((TM + HALO, CONV_W), F32),
           _phase_scratch(), _phase_scratch()],
        semantics=("arbitrary",),
    )(do_conv, do_conv, y, y, ca, cg, ca, cg, conv_w, ln_g, ln_b, *partials)
    return res[:6], res[6:]


def _attn_bwd(q, kv, sinks, o, lse, do, grads):
    R = q.shape[0]
    nb = R // BLOCK
    ng = len(grads)

    def body(sink_ref, q_ref, kvc_ref, kvp_ref, kvm_ref, o_ref, lse_ref, do_ref, *refs):
        g_ins, (dq_ref, dkv_ref, dkvm_ref, dsink_ref) = refs[:ng], refs[ng:ng + 4]
        g_outs = refs[ng + 4:2 * ng + 4]
        send_sems, recv_sems, carry_s, cur_s, prev_s, bias_s, sink_s = refs[2 * ng + 4:]
        b = pl.program_id(0)
        exchange = functools.partial(_sibling_copies, g_ins, g_outs, send_sems, recv_sems)
        _hosted(b, nb + 1, exchange)

        @pl.when(b == 0)
        def _():
            dkvm_ref[...] = jnp.zeros_like(dkvm_ref)
            carry_s[...] = jnp.zeros_like(carry_s)
            for h in range(N_HEADS):
                dsink_ref[0, h] = 0.0
            _attn_tables(sink_ref, bias_s, sink_s)

        @pl.when(b < nb)
        def _():
            sel, pen, pen_m = _attn_masks(b)
            for g in range(N_HEADS // GROUP):
                ks, vs = slice(HEAD_DIM * g, HEAD_DIM * (g + 1)), slice(KV_W + HEAD_DIM * g, KV_W + HEAD_DIM * (g + 1))
                rows = slice(GB * g, GB * (g + 1))
                qg = _stack_heads(q_ref, g) * ATTN_SCALE
                kc, kp, km = kvc_ref[:, ks], kvp_ref[:, ks], kvm_ref[LEAD:BLOCK, ks]
                vc, vp, vm = kvc_ref[:, vs], kvp_ref[:, vs], kvm_ref[LEAD:BLOCK, vs]
                s_b, s_m = _attn_scores(qg, kc, kp, km, sel, pen, pen_m, bias_s[rows, :])
                lse = jnp.concatenate(
                    [lse_ref[:, GROUP * g + j:GROUP * g + j + 1] for j in range(GROUP)], axis=0)
                p_b = jnp.exp(s_b - lse)
                p_m = jnp.exp(s_m - lse)
                dog = _stack_heads(do_ref, g)
                delta = jnp.sum(dog * _stack_heads(o_ref, g), axis=-1, keepdims=True)
                dob = dog.astype(BF16)
                dp_b = jnp.where(sel, _dot(dob, vc, NT), _dot(dob, vp, NT))
                ds_b = p_b * (dp_b - delta)
                ds_m = (p_m * (_dot(dob, vm, NT) - delta)).astype(BF16)
                dsk = jnp.exp(sink_s[rows, :] - lse) * delta
                for j in range(GROUP):
                    dsink_ref[0, GROUP * g + j] += -jnp.sum(dsk[BLOCK * j:BLOCK * (j + 1), :])
                ds_c = jnp.where(sel, ds_b, 0.0).astype(BF16)
                ds_p = jnp.where(sel, 0.0, ds_b).astype(BF16)
                p_c = jnp.where(sel, p_b, 0.0).astype(BF16)
                p_p = jnp.where(sel, 0.0, p_b).astype(BF16)
                dq = (_dot(ds_c, kc, NN) + _dot(ds_p, kp, NN) + _dot(ds_m, km, NN)) * ATTN_SCALE
                dq_ref[:, GROUP * HEAD_DIM * g:GROUP * HEAD_DIM * (g + 1)] = _unstack_heads(dq).astype(BF16)
                cur_s[:, ks] = _dot(ds_c, qg, TN)
                cur_s[:, vs] = _dot(p_c, dob, TN)
                prev_s[:, ks] = _dot(ds_p, qg, TN)
                prev_s[:, vs] = _dot(p_p, dob, TN)
                dkvm_ref[:, ks] += _dot(ds_m, qg, TN)
                dkvm_ref[:, vs] += _dot(p_m.astype(BF16), dob, TN)
            dkv_ref[...] = (carry_s[...] + prev_s[...]).astype(BF16)
            carry_s[...] = cur_s[...]

        @pl.when(b == nb)
        def _():
            dkv_ref[...] = carry_s[...].astype(BF16)

        _hosted_wait(b, nb + 1, exchange)

    def at(off):
        return lambda b: (jnp.clip(b + off, 0, nb - 1), 0)

    blk = lambda cols, off=0: pl.BlockSpec((BLOCK, cols), at(off))
    res = _pcall(
        body, name="attn_bwd", grid=(nb + 1,),
        in_specs=[pl.BlockSpec(memory_space=pltpu.SMEM), blk(512), blk(256), blk(256, -1), _full((BLOCK, 256)),
                  blk(512), blk(N_HEADS), blk(512)] + [ANY] * ng,
        out_specs=[blk(512), blk(256, -1), _full((N_META, 256)), pl.BlockSpec(memory_space=pltpu.SMEM)]
        + [ANY] * ng,
        out_shape=[jax.ShapeDtypeStruct((R, 512), BF16), jax.ShapeDtypeStruct((R, 256), BF16),
                   jax.ShapeDtypeStruct((N_META, 256), F32), jax.ShapeDtypeStruct((1, N_HEADS), F32)]
        + _sibling_shapes(grads),
        scratch_shapes=_sem_pair(ng) + [pltpu.VMEM((BLOCK, 256), F32)] * 3
        + [pltpu.VMEM((N_HEADS * BLOCK, BLOCK), F32), pltpu.VMEM((N_HEADS * BLOCK, 1), F32)],
        semantics=("arbitrary",),
    )(sinks, q, kv, kv, kv, o, lse, do, *grads)
    return res[:4], res[4:]


def _inproj_bwd(dh1, h0, g1, dq, dkv, dkvm, dca, dcg, w_in_t):
    R = h0.shape[0]
    nt = R // TM
    assert nt >= 2

    def body(dh1_ref, h0_ref, g_ref, dq_ref, dkv_ref, dkvm_ref, dca_ref, dcg_ref, w_ref,
             gx_hbm, dmeta_ref, dproj_ref, hn_ref, dg_ref, dx_s, gx_sems):
        i = pl.program_id(0)

        def gx_copy(step, slot, first):
            if first:
                return pltpu.make_async_copy(dx_s.at[slot, pl.ds(BLOCK, TM - BLOCK)],
                                             gx_hbm.at[pl.ds(0, TM - BLOCK)], gx_sems.at[slot])
            return pltpu.make_async_copy(
                dx_s.at[slot], gx_hbm.at[pl.ds(pl.multiple_of(step * TM - BLOCK, BLOCK), TM)], gx_sems.at[slot])

        @pl.when(i == 0)
        def _():
            dg_ref[...] = jnp.zeros_like(dg_ref)

        dproj_ref[:, 0:512] = dq_ref[...]
        dproj_ref[:, 512:768] = dkv_ref[...]
        dproj_ref[:, 768:1280] = dca_ref[...]
        dproj_ref[:, 1280:1792] = dcg_ref[...]

        @pl.when(i == 0)
        def _():
            dproj_ref[LEAD:BLOCK, 512:768] = dkvm_ref[...].astype(BF16)

        dhn = _dot(dproj_ref[...], w_ref[...], NN)
        h = h0_ref[...]
        r = _rms_stats(h)
        g = g_ref[...]
        hn_ref[...] = (h * r * g).astype(BF16)
        dx, dg = _rms_bwd(dhn, h, r, g)
        dg_ref[...] += dg
        slot = i % 2
        pl.when(i == 2)(lambda: gx_copy(0, 0, True).wait())
        pl.when(i > 2)(lambda: gx_copy(i - 2, slot, False).wait())
        dx_s[slot] = dh1_ref[...] + dx

        @pl.when(i == 0)
        def _():
            dmeta_ref[...] = dx_s[0, LEAD:BLOCK, :]
            gx_copy(0, 0, True).start()

        pl.when(i > 0)(lambda: gx_copy(i, slot, False).start())

        @pl.when(i == nt - 1)
        def _():
            gx_copy(nt - 2, (nt - 2) % 2, nt == 2).wait()
            gx_copy(nt - 1, (nt - 1) % 2, False).wait()

    return _pcall(
        body, name="inproj_bwd", grid=(nt,),
        in_specs=[_rows(TM, D_MODEL), _rows(TM, D_MODEL), _full((1, D_MODEL)), _rows(TM, 512), _rows(TM, 256),
                  _full((N_META, 256)), _rows(TM, 512), _rows(TM, 512), _full((1792, D_MODEL))],
        out_specs=[ANY, _full((N_META, D_MODEL)), _rows(TM, 1792), _rows(TM, D_MODEL), _full((1, D_MODEL))],
        out_shape=[jax.ShapeDtypeStruct((R - BLOCK, D_MODEL), F32), jax.ShapeDtypeStruct((N_META, D_MODEL), F32),
                   jax.ShapeDtypeStruct((R, 1792), BF16),
                   jax.ShapeDtypeStruct((R, D_MODEL), BF16), jax.ShapeDtypeStruct((1, D_MODEL), F32)],
        scratch_shapes=[pltpu.VMEM((2, TM, D_MODEL), F32), pltpu.SemaphoreType.DMA((2,))],
        semantics=("arbitrary",),
    )(dh1, h0, g1, dq, dkv, dkvm, dca, dcg, w_in_t)


ANY = pl.BlockSpec(memory_space=pl.ANY)


def _position():
    return lax.axis_index("x"), lax.axis_index("y"), lax.axis_index("c")


def _device_number(p):
    return 4 * p[0] + 2 * p[1] + p[2]


def _two_level_allgather(ins, outs, block, send_sems, recv_sems, local_sems, sem_base=0):
    n = len(ins)
    x, y, c = _position()
    me, sibling = (x, y, c), (x, y, 1 - c)
    chips = [(1 - x, y), (x, 1 - y), (1 - x, 1 - y)]

    def copy(w, k, origin, to, src=None):
        return pltpu.make_async_remote_copy(
            src_ref=block(w, origin) if src is None else src, dst_ref=block(w, origin),
            send_sem=send_sems.at[sem_base + 7 * w + k], recv_sem=recv_sems.at[sem_base + 7 * w + k],
            device_id=to, device_id_type=MESH)

    def mine(w):
        return pltpu.make_async_copy(ins[w], block(w, me), local_sems.at[w])

    def own(w):
        return [copy(w, 0, me, sibling, src=ins[w])] + [
            copy(w, 1 + j, me, (*chip, c), src=ins[w]) for j, chip in enumerate(chips)]

    def passed(w):
        return [copy(w, 4 + j, (*chip, c), sibling) for j, chip in enumerate(chips)]

    def start():
        for w in range(n):
            mine(w).start()
        for w in range(n):
            for cp in own(w):
                cp.start()

    def forward(w):
        fw = passed(w)
        for j, chip in enumerate(chips):
            copy(w, 1 + j, (*chip, c), me).wait_recv()
            fw[j].start()

    def finish():
        for w in range(n):
            copy(w, 0, sibling, me).wait_recv()
            for j, chip in enumerate(chips):
                copy(w, 4 + j, (*chip, 1 - c), me).wait_recv()
        for w in range(n):
            for cp in own(w) + passed(w):
                cp.wait_send()
            mine(w).wait()

    return start, forward, finish


def _blocking_allgather(ins, outs, block, send_sems, recv_sems, local_sems, sem_base=0):
    start, forward, finish = _two_level_allgather(ins, outs, block, send_sems, recv_sems, local_sems, sem_base)
    start()
    for w in range(len(ins)):
        forward(w)
    finish()


def _row_block(outs, rows):
    def block(w, p):
        return outs[w].at[pl.ds(pl.multiple_of(_device_number(p) * rows[w], 16), rows[w])]
    return block


def _sibling_copies(ins, outs, send_sems, recv_sems):
    x, y, c = _position()
    return [pltpu.make_async_remote_copy(
        src_ref=ins[w].at[:, 1 - c], dst_ref=outs[w], send_sem=send_sems.at[w], recv_sem=recv_sems.at[w],
        device_id=(x, y, 1 - c), device_id_type=MESH) for w in range(len(ins))]


def _chip_copies(ins, outs, send_sems, recv_sems):
    x, y, c = _position()
    chips = [(1 - x, y), (x, 1 - y), (1 - x, 1 - y)]
    return [pltpu.make_async_remote_copy(
        src_ref=ins[w].at[2 * chip[0] + chip[1]], dst_ref=outs[w].at[k],
        send_sem=send_sems.at[3 * w + k], recv_sem=recv_sems.at[3 * w + k],
        device_id=(*chip, c), device_id_type=MESH) for w in range(len(ins)) for k, chip in enumerate(chips)]


def _hosted(step, n_steps, make_copies):
    @pl.when(step == 0)
    def _():
        for cp in make_copies():
            cp.start()


def _hosted_wait(step, n_steps, make_copies):
    @pl.when(step == n_steps - 1)
    def _():
        for cp in make_copies():
            cp.wait()


def _sem_pair(n):
    return [pltpu.SemaphoreType.DMA((n,)), pltpu.SemaphoreType.DMA((n,))]


def _allgather_params(shards, small):
    arrays = list(shards) + list(small)
    n, ns = len(arrays), len(shards)

    def body(*refs):
        ins, outs = refs[:n], refs[n:2 * n]
        send_sems, recv_sems, local_sems = refs[2 * n:]

        rows = _row_block(outs, [a.shape[0] for a in arrays])

        def block(w, p):
            return rows(w, p) if w < ns else outs[w].at[_device_number(p)]

        _blocking_allgather(ins, outs, block, send_sems, recv_sems, local_sems)

    out_shape = [jax.ShapeDtypeStruct((N_DEV * a.shape[0], a.shape[1]), a.dtype) for a in shards]
    out_shape += [jax.ShapeDtypeStruct((N_DEV,) + a.shape, a.dtype) for a in small]
    return _pcall(
        body, name="allgather_params", in_specs=[ANY] * n, out_specs=[ANY] * n, out_shape=out_shape,
        scratch_shapes=[pltpu.SemaphoreType.DMA((7 * n,)), pltpu.SemaphoreType.DMA((7 * n,)),
                        pltpu.SemaphoreType.DMA((n,))],
    )(*arrays)


def _reduce_siblings(grads):
    n = len(grads)

    def body(*refs):
        ins, outs = refs[:n], refs[n:2 * n]
        send_sems, recv_sems = refs[2 * n:]
        copies = _sibling_copies(ins, outs, send_sems, recv_sems)
        for cp in copies:
            cp.start()
        for cp in copies:
            cp.wait()

    return _pcall(
        body, name="reduce_siblings", in_specs=[ANY] * n, out_specs=[ANY] * n,
        out_shape=_sibling_shapes(grads), scratch_shapes=_sem_pair(n),
    )(*grads)


def _sibling_shapes(grads):
    return [jax.ShapeDtypeStruct((4,) + g.shape[2:], F32) for g in grads]


def _chip_shapes(partials):
    return [jax.ShapeDtypeStruct((3,) + p.shape[1:], F32) for p in partials]


def _add_sibling(grad, received, core, name):
    _, _, r, cols = grad.shape

    def body(core_ref, g_ref, r_ref, o_ref):
        o_ref[...] = g_ref[...] + r_ref[...]

    return pl.pallas_call(
        body, name=name,
        grid_spec=pltpu.PrefetchScalarGridSpec(
            num_scalar_prefetch=1, grid=(4,),
            in_specs=[pl.BlockSpec((None, None, r, cols), lambda s, core_ref: (s, core_ref[0], 0, 0)),
                      pl.BlockSpec((None, r, cols), lambda s, core_ref: (s, 0, 0))],
            out_specs=pl.BlockSpec((None, r, cols), lambda s, core_ref: (s, 0, 0))),
        out_shape=jax.ShapeDtypeStruct((4, r, cols), F32),
        compiler_params=pltpu.CompilerParams(vmem_limit_bytes=VMEM_LIMIT),
    )(core, grad, received)


def _reduce_chips(partials, small):
    n, ns = len(partials), len(small)

    def body(*refs):
        p_ins, s_ins = refs[:n], refs[n:n + ns]
        p_outs, s_outs = refs[n + ns:2 * n + ns], refs[2 * n + ns:2 * (n + ns)]
        send_sems, recv_sems, local_sems = refs[2 * (n + ns):]
        copies = _chip_copies(p_ins, p_outs, send_sems, recv_sems)
        for cp in copies:
            cp.start()
        _blocking_allgather(s_ins, s_outs, lambda w, p: s_outs[w].at[_device_number(p)],
                            send_sems, recv_sems, local_sems, sem_base=3 * n)
        for cp in copies:
            cp.wait()

    out_shape = _chip_shapes(partials)
    out_shape += [jax.ShapeDtypeStruct((N_DEV,) + a.shape, a.dtype) for a in small]
    nsem = 3 * n + 7 * ns
    return _pcall(
        body, name="reduce_chips", in_specs=[ANY] * (n + ns), out_specs=[ANY] * (n + ns), out_shape=out_shape,
        scratch_shapes=[pltpu.SemaphoreType.DMA((nsem,)), pltpu.SemaphoreType.DMA((nsem,)),
                        pltpu.SemaphoreType.DMA((ns,))],
    )(*partials, *small)


def _sum_chips(partial, received, slot, name):
    _, r, cols = partial.shape

    def body(slot_ref, p_ref, r_ref, o_ref):
        o_ref[...] = p_ref[...] + r_ref[0] + r_ref[1] + r_ref[2]

    return pl.pallas_call(
        body, name=name,
        grid_spec=pltpu.PrefetchScalarGridSpec(
            num_scalar_prefetch=1, grid=(1,),
            in_specs=[pl.BlockSpec((None, r, cols), lambda i, slot_ref: (slot_ref[0], 0, 0)),
                      pl.BlockSpec((3, r, cols), lambda i, slot_ref: (0, 0, 0))],
            out_specs=pl.BlockSpec((r, cols), lambda i, slot_ref: (0, 0))),
        out_shape=jax.ShapeDtypeStruct((r, cols), F32),
        compiler_params=pltpu.CompilerParams(vmem_limit_bytes=VMEM_LIMIT),
    )(slot, partial, received)


def _adam(g, w, m, v):
    m = ADAM_B1 * m + (1.0 - ADAM_B1) * g
    v = ADAM_B2 * v + (1.0 - ADAM_B2) * (g * g)
    m_hat = m / (1.0 - ADAM_B1 ** ADAM_STEP)
    v_hat = v / (1.0 - ADAM_B2 ** ADAM_STEP)
    delta = -ADAM_LR * (m_hat / (jnp.sqrt(v_hat) + ADAM_EPS) + ADAM_WD * w)
    return delta, m, v


def _adamw(g, w, m, v, name):
    def body(g_ref, w_ref, m_ref, v_ref, d_ref, nm_ref, nv_ref):
        d_ref[...], nm_ref[...], nv_ref[...] = _adam(g_ref[...], w_ref[...], m_ref[...], v_ref[...])

    out = jax.ShapeDtypeStruct(w.shape, F32)
    return _pcall(body, name=name, out_shape=[out, out, out])(g, w, m, v)


def _adamw_small(dev, ga, gb, gc, params):
    names = ["meta", "attn_norm", "sinks", "conv_w", "conv_b", "ln_g", "ln_b", "attn_out", "conv_out",
             "ffn_norm", "final_norm"]
    flat = [a for p in params for a in p]
    n_in = len(flat)

    def body(dev_ref, ga_ref, gb_ref, gc_ref, *refs):
        ins, outs = refs[:n_in], refs[n_in:n_in + 4 * len(names)]
        loss_ref, sb, sc = refs[n_in + 4 * len(names):]
        a = ga_ref[0]
        sb[...] = gb_ref[0]
        sc[...] = gc_ref[0]
        for d in range(1, N_DEV):
            a = a + ga_ref[d]
            sb[...] += gb_ref[d]
            sc[...] += gc_ref[d]
        dev = dev_ref[0]
        grads = {
            "attn_norm": a[0:1, :], "ffn_norm": a[1:2, :], "final_norm": a[2:3, :],
            "conv_b": a[3:4, 0:512], "ln_g": a[3:4, 512:1024], "ln_b": a[4:5, 0:512],
            "attn_out": a[4:5, 512:1024], "conv_out": a[5:6, 0:512], "sinks": a[5:6, 512:512 + N_HEADS],
            "meta": sb[pl.ds(pl.multiple_of(dev * N_META, N_META), N_META), :],
            "conv_w": sc[pl.ds(pl.multiple_of(dev * 32, 32), 32), :][0:CONV_K, :],
        }
        for idx, nm in enumerate(names):
            w_ref, m_ref, v_ref = ins[3 * idx:3 * idx + 3]
            g = grads[nm]
            delta, m, v = _adam(g, w_ref[...], m_ref[...], v_ref[...])
            o = outs[4 * idx:4 * idx + 4]
            o[0][...], o[1][...], o[2][...], o[3][...] = g, delta, m, v
        loss_ref[...] = a[6:7, 0:1]

    vm = pl.BlockSpec(memory_space=pltpu.VMEM)
    out_shape = [jax.ShapeDtypeStruct(p[0].shape, F32) for p in params for _ in range(4)]
    out_shape.append(jax.ShapeDtypeStruct((1, 1), F32))
    res = pl.pallas_call(
        body, name="adamw_small",
        grid_spec=pltpu.PrefetchScalarGridSpec(
            num_scalar_prefetch=1, grid=(1,),
            in_specs=[pl.BlockSpec(ga.shape, lambda i, d: (0, 0, 0)), pl.BlockSpec(gb.shape, lambda i, d: (0, 0, 0)),
                      pl.BlockSpec(gc.shape, lambda i, d: (0, 0, 0))]
            + [pl.BlockSpec(a.shape, lambda i, d: (0, 0)) for a in flat],
            out_specs=[pl.BlockSpec(s.shape, lambda i, d: (0, 0)) for s in out_shape],
            scratch_shapes=[pltpu.VMEM(gb.shape[1:], F32), pltpu.VMEM(gc.shape[1:], F32)]),
        out_shape=out_shape,
        compiler_params=pltpu.CompilerParams(vmem_limit_bytes=VMEM_LIMIT),
    )(dev, ga, gb, gc, *flat)
    return [res[4 * i:4 * i + 4] for i in range(len(names))], res[-1]


def kernel(x, meta_tokens, attn_norm_g, w_in, attn_sinks, conv_w, conv_b, conv_ln_g, conv_ln_b, attn_out_g, conv_out_g, w_out, ffn_norm_g, w_gate, w_up, w_down, final_norm_g, loss_target, m_meta_tokens, m_attn_norm_g, m_w_in, m_attn_sinks, m_conv_w, m_conv_b, m_conv_ln_g, m_conv_ln_b, m_attn_out_g, m_conv_out_g, m_w_out, m_ffn_norm_g, m_w_gate, m_w_up, m_w_down, m_final_norm_g, v_meta_tokens, v_attn_norm_g, v_w_in, v_attn_sinks, v_conv_w, v_conv_b, v_conv_ln_g, v_conv_ln_b, v_attn_out_g, v_conv_out_g, v_w_out, v_ffn_norm_g, v_w_gate, v_w_up, v_w_down, v_final_norm_g):
    xi, yi, ci = _position()
    dev = jnp.reshape(_device_number((xi, yi, ci)), (1,)).astype(jnp.int32)
    core = jnp.reshape(ci, (1,)).astype(jnp.int32)
    slot = jnp.reshape(2 * xi + yi, (1,)).astype(jnp.int32)

    w_in_t, meta_st, convw_st = _allgather_params([w_in[0].T.astype(BF16)], [meta_tokens, conv_w[0]])
    later = [w_out[0].astype(BF16), w_gate[0].T.astype(BF16), w_up[0].T.astype(BF16), w_down[0].astype(BF16)]
    meta_full = jnp.transpose(meta_st, (1, 0, 2)).reshape(N_META, D_MODEL)
    convw_full = jnp.transpose(convw_st, (1, 0, 2)).reshape(CONV_K, CONV_W)

    h0 = jnp.concatenate([jnp.zeros((LEAD, D_MODEL), F32), meta_full, x[0]], axis=0)
    final_g = final_norm_g.reshape(1, D_MODEL)

    q, kv, ca, cg = _inproj_fwd(h0, attn_norm_g, w_in_t)
    o_attn, lse, (w_out_b, wg_t, wu_t, wd_b) = _attn_fwd(q, kv, attn_sinks, later)
    o_conv, y_conv = _conv_fwd(ca, cg, convw_full, conv_b, conv_ln_g, conv_ln_b)
    h1 = _outproj_fwd(h0, o_attn, o_conv, attn_out_g, conv_out_g, w_out_b)
    gate, up, dh2, loss_sum, dg_final = _ffn_fwd(h1, ffn_norm_g, wg_t, wu_t, wd_b, final_g, loss_target[0])

    def blocks(g):
        return g.reshape(4, 2, g.shape[0] // N_DEV, D_MODEL)

    def add_siblings(grads, received, tags):
        return [_add_sibling(g, r, core, "add_sibling_" + t) for g, r, t in zip(grads, received, tags)]

    dgate, dup, act, hn2, dh1, dg_ffn = _ffn_bwd(dh2, h1, ffn_norm_g, gate, up, wg_t, wu_t, wd_b)
    ffn_grads = [blocks(_wgrad(dgate, hn2, FF_CHUNK, "wgrad_gate")), blocks(_wgrad(dup, hn2, FF_CHUNK, "wgrad_up")),
                 blocks(_wgrad(act, dh2, FF_CHUNK, "wgrad_down"))]
    (do_attn, do_conv, mixed, dg_ao, dg_co), ffn_sib = _outproj_bwd(
        dh1, o_attn, o_conv, attn_out_g, conv_out_g, w_out_b, ffn_grads)
    ffn_sums = add_siblings(ffn_grads, ffn_sib, ("gate", "up", "down"))
    out_grads = [blocks(_wgrad(mixed, dh1, D_MODEL, "wgrad_out"))]
    (dca, dcg, dconvw, dconvb, dln_g, dln_b), ffn_chips = _conv_bwd(
        do_conv, y_conv, ca, cg, convw_full, conv_ln_g, conv_ln_b, ffn_sums)
    (dq, dkv, dkvm, dsinks), out_sib = _attn_bwd(q, kv, attn_sinks, o_attn, lse, do_attn, out_grads)
    out_sums = add_siblings(out_grads, out_sib, ("out",))
    grad_x, dmeta, dproj, hn1, dg_attn = _inproj_bwd(dh1, h0, attn_norm_g, dq, dkv, dkvm, dca, dcg, w_in_t)
    dwi_t, out_chips = _wgrad(dproj, hn1, 1792, "wgrad_in", out_sums)
    in_grads = [blocks(dwi_t)]
    in_sums = add_siblings(in_grads, _reduce_siblings(in_grads), ("in",))
    small_a = jnp.concatenate([
        dg_attn, dg_ffn, dg_final, jnp.concatenate([dconvb, dln_g], axis=1), jnp.concatenate([dln_b, dg_ao], axis=1),
        jnp.concatenate([dg_co, dsinks, jnp.zeros((1, 512 - N_HEADS), F32)], axis=1),
        jnp.concatenate([loss_sum[0:1, :], jnp.zeros((1, D_MODEL - 128), F32)], axis=1),
        jnp.zeros((1, D_MODEL), F32)], axis=0)
    small_b = jnp.transpose(dmeta.reshape(N_META, N_DEV, 128), (1, 0, 2)).reshape(N_DEV * N_META, 128)
    small_c = jnp.transpose(dconvw.reshape(32, N_DEV, 64), (1, 0, 2)).reshape(N_DEV * 32, 64)
    in_chips, ga, gb, gc = _reduce_chips(in_sums, [small_a, small_b, small_c])
    tags = ("in", "out", "gate", "up", "down")
    chip_sums = in_sums + out_sums + ffn_sums
    from_chips = [in_chips] + list(out_chips) + list(ffn_chips)
    g_own = [_sum_chips(p, r, slot, "sum_chips_" + t) for p, r, t in zip(chip_sums, from_chips, tags)]
    g_big = [g_own[0].T, g_own[1], g_own[2].T, g_own[3].T, g_own[4]]

    big = [(w_in, m_w_in, v_w_in), (w_out, m_w_out, v_w_out), (w_gate, m_w_gate, v_w_gate), (w_up, m_w_up, v_w_up),
           (w_down, m_w_down, v_w_down)]
    big_out = {}
    for t, g, (w, m, v) in zip(tags, g_big, big):
        delta, nm, nv = _adamw(g, w[0], m[0], v[0], "adamw_" + t)
        big_out[t] = (g[None], delta[None], nm[None], nv[None])

    small_params = [
        (meta_tokens, m_meta_tokens, v_meta_tokens), (attn_norm_g, m_attn_norm_g, v_attn_norm_g),
        (attn_sinks, m_attn_sinks, v_attn_sinks), (conv_w[0], m_conv_w[0], v_conv_w[0]),
        (conv_b, m_conv_b, v_conv_b), (conv_ln_g, m_conv_ln_g, v_conv_ln_g), (conv_ln_b, m_conv_ln_b, v_conv_ln_b),
        (attn_out_g, m_attn_out_g, v_attn_out_g), (conv_out_g, m_conv_out_g, v_conv_out_g),
        (ffn_norm_g, m_ffn_norm_g, v_ffn_norm_g),
        (final_g, m_final_norm_g.reshape(1, D_MODEL), v_final_norm_g.reshape(1, D_MODEL))]
    sm, loss = _adamw_small(dev, ga, gb, gc, small_params)
    sm[3] = [a[None] for a in sm[3]]
    sm[10] = [a.reshape(D_MODEL) for a in sm[10]]

    per_param = [sm[0], sm[1], big_out["in"], sm[2], sm[3], sm[4], sm[5], sm[6], sm[7], sm[8], big_out["out"],
                 sm[9], big_out["gate"], big_out["up"], big_out["down"], sm[10]]
    loss = loss.reshape(())
    outs = [loss, grad_x[None]]
    for kind in range(4):
        outs += [p[kind] for p in per_param]
    return tuple(outs)
```

```python
import functools
import math

import jax
import jax.numpy as jnp
from jax import lax
from jax.experimental import pallas as pl
from jax.experimental.pallas import tpu as pltpu

F32, BF16 = jnp.float32, jnp.bfloat16
MESH = pl.DeviceIdType.MESH

D_MODEL = 1024
N_META = 16
BLOCK = 128
LEAD = BLOCK - N_META
HEAD_DIM = 64
N_HEADS = 8
GROUP = 4
ATTN_W = 512
KV_W = 128
CONV_W = 512
CONV_K = 31
HALO = 32
D_FF = 2816
FF_CHUNK = D_FF // 2
FF_SUB = [slice(s, min(s + 256, FF_CHUNK)) for s in range(0, FF_CHUNK, 256)]
N_DEV = 8
EPS = 1e-5
NEG = -1e30
TM = 640
WGRAD_K_TILES = 5
CONV_ROWS = 64
VMEM_LIMIT = 56 * 1024 * 1024

ADAM_LR, ADAM_B1, ADAM_B2, ADAM_EPS, ADAM_WD, ADAM_STEP = 0.001, 0.9, 0.999, 1e-08, 0.01, 10

NT = (((1,), (1,)), ((), ()))
NN = (((1,), (0,)), ((), ()))
TN = (((0,), (0,)), ((), ()))


def _dot(a, b, dims):
    return lax.dot_general(a, b, dims, preferred_element_type=F32)


def _sigmoid(x):
    return 1.0 / (1.0 + jnp.exp(-x))


def _pcall(body, *, name, out_shape, grid=None, in_specs=None, out_specs=None, scratch_shapes=(),
           semantics=None, **kw):
    params = dict(vmem_limit_bytes=VMEM_LIMIT)
    if semantics is not None:
        params["dimension_semantics"] = semantics
    extra = {}
    if grid is not None:
        extra["grid"] = grid
    if in_specs is not None:
        extra["in_specs"] = in_specs
    if out_specs is not None:
        extra["out_specs"] = out_specs
    return pl.pallas_call(body, name=name, out_shape=out_shape, scratch_shapes=list(scratch_shapes),
                          compiler_params=pltpu.CompilerParams(**params), **extra, **kw)


def _rows(tm, cols, off=0):
    return pl.BlockSpec((tm, cols), lambda i, *_: (i + off, 0))


def _full(shape):
    nd = len(shape)
    return pl.BlockSpec(shape, lambda *_: (0,) * nd)


def _rms_stats(x):
    return lax.rsqrt(jnp.mean(x * x, axis=-1, keepdims=True) + EPS)


def _rms_bwd(dy, x, r, g):
    t = dy * g
    dx = r * (t - x * (r * r) * jnp.mean(t * x, axis=-1, keepdims=True))
    dg = jnp.sum(dy * x * r, axis=0, keepdims=True)
    return dx, dg


def _inproj_fwd(x, meta, g1, w_in_t):
    R = x.shape[0] + BLOCK
    nt = R // TM
    assert nt >= 2

    def body(x_hbm, meta_ref, g_ref, w_ref, h0_ref, q_ref, kv_ref, ca_ref, cg_ref, x_s, sems):
        i = pl.program_id(0)
        slot = i % 2

        def x_copy(step, slot, first):
            if first:
                return pltpu.make_async_copy(x_hbm.at[pl.ds(0, TM - BLOCK)],
                                             x_s.at[slot, pl.ds(BLOCK, TM - BLOCK)], sems.at[slot])
            return pltpu.make_async_copy(
                x_hbm.at[pl.ds(pl.multiple_of(step * TM - BLOCK, BLOCK), TM)], x_s.at[slot], sems.at[slot])

        @pl.when(i == 0)
        def _():
            x_copy(0, 0, True).start()
            x_s[0, 0:LEAD, :] = jnp.zeros((LEAD, D_MODEL), F32)
            x_s[0, LEAD:BLOCK, :] = meta_ref[...]

        pl.when(i + 1 < nt)(lambda: x_copy(i + 1, 1 - slot, False).start())
        pl.when(i == 0)(lambda: x_copy(0, 0, True).wait())
        pl.when(i > 0)(lambda: x_copy(i, slot, False).wait())
        h = x_s[slot]
        h0_ref[...] = h
        hn = (h * _rms_stats(h) * g_ref[...]).astype(BF16)
        q_ref[...] = _dot(hn, w_ref[0:512, :], NT).astype(BF16)
        kv_ref[...] = _dot(hn, w_ref[512:768, :], NT).astype(BF16)
        ca_ref[...] = _dot(hn, w_ref[768:1280, :], NT)
        cg_ref[...] = _dot(hn, w_ref[1280:1792, :], NT)

    return _pcall(
        body, name="inproj_fwd", grid=(nt,),
        in_specs=[pl.BlockSpec(memory_space=pl.ANY), _full((N_META, D_MODEL)), _full((1, D_MODEL)),
                  _full((1792, D_MODEL))],
        out_specs=[_rows(TM, D_MODEL), _rows(TM, 512), _rows(TM, 256), _rows(TM, 512), _rows(TM, 512)],
        out_shape=[jax.ShapeDtypeStruct((R, D_MODEL), F32),
                   jax.ShapeDtypeStruct((R, 512), BF16), jax.ShapeDtypeStruct((R, 256), BF16),
                   jax.ShapeDtypeStruct((R, 512), F32), jax.ShapeDtypeStruct((R, 512), F32)],
        scratch_shapes=[pltpu.VMEM((2, TM, D_MODEL), F32), pltpu.SemaphoreType.DMA((2,))],
        semantics=("arbitrary",),
    )(x, meta, g1, w_in_t)


GB = GROUP * BLOCK
ATTN_SCALE = 1.0 / math.sqrt(HEAD_DIM)


def _stack_heads(ref, g):
    return jnp.concatenate(
        [ref[:, HEAD_DIM * (GROUP * g + j):HEAD_DIM * (GROUP * g + j + 1)] for j in range(GROUP)], axis=0)


def _unstack_heads(x):
    return jnp.concatenate([x[BLOCK * j:BLOCK * (j + 1), :] for j in range(GROUP)], axis=1)


def _with_ones(v):
    return jnp.concatenate([v, jnp.ones(v.shape, v.dtype)], axis=1)


def _attn_tables(sink_ref, bias_s, sink_s):
    ii = lax.broadcasted_iota(jnp.int32, (BLOCK, BLOCK), 0)
    jj = lax.broadcasted_iota(jnp.int32, (BLOCK, BLOCK), 1)
    dist = jnp.where(jj <= ii, ii - jj, ii - jj + BLOCK).astype(F32)
    for h in range(N_HEADS):
        bias_s[BLOCK * h:BLOCK * (h + 1), :] = dist * -(2.0 ** -(h + 1))
        sink_s[BLOCK * h:BLOCK * (h + 1), :] = jnp.zeros((BLOCK, 1), F32) + sink_ref[0, h]


def _attn_masks(b):
    ii = lax.broadcasted_iota(jnp.int32, (GB, BLOCK), 0) & (BLOCK - 1)
    jj = lax.broadcasted_iota(jnp.int32, (GB, BLOCK), 1)
    sel = jj <= ii
    pen = jnp.where(sel, jnp.where(b >= 1, 0.0, NEG), jnp.where(b >= 2, 0.0, NEG))
    mi = lax.broadcasted_iota(jnp.int32, (GB, N_META), 0) & (BLOCK - 1)
    mj = lax.broadcasted_iota(jnp.int32, (GB, N_META), 1)
    pen_m = jnp.where((mj + LEAD) <= (mi + b * BLOCK), 0.0, NEG)
    return sel, pen, pen_m


def _attn_scores(qg, kc, kp, km, sel, pen, pen_m, bias):
    s_b = jnp.where(sel, _dot(qg, kc, NT), _dot(qg, kp, NT)) + bias + pen
    s_m = _dot(qg, km, NT) + pen_m
    return s_b, s_m


def _attn_fwd(q, kv, sinks, shards):
    R = q.shape[0]
    nb = R // BLOCK
    ns = len(shards)
    forward_steps = [(w + 1) * (nb - 8) // (ns + 1) for w in range(ns)]

    def body(sink_ref, q_ref, kvc_ref, kvp_ref, kvm_ref, *refs):
        ag_ins, (o_ref, lse_ref), ag_outs = refs[:ns], refs[ns:ns + 2], refs[ns + 2:2 * ns + 2]
        send_sems, recv_sems, local_sems, bias_s, sink_s = refs[2 * ns + 2:]
        b = pl.program_id(0)
        ag_start, ag_forward, ag_finish = _two_level_allgather(
            ag_ins, ag_outs, _row_block(ag_outs, [s.shape[0] for s in shards]), send_sems, recv_sems, local_sems)
        pl.when(b == 0)(ag_start)
        for w, step in enumerate(forward_steps):
            pl.when(b == step)(functools.partial(ag_forward, w))
        pl.when(b == 0)(functools.partial(_attn_tables, sink_ref, bias_s, sink_s))
        sel, pen, pen_m = _attn_masks(b)
        for g in range(N_HEADS // GROUP):
            ks, vs = slice(HEAD_DIM * g, HEAD_DIM * (g + 1)), slice(KV_W + HEAD_DIM * g, KV_W + HEAD_DIM * (g + 1))
            rows = slice(GB * g, GB * (g + 1))
            qg = _stack_heads(q_ref, g) * ATTN_SCALE
            s_b, s_m = _attn_scores(qg, kvc_ref[:, ks], kvp_ref[:, ks], kvm_ref[LEAD:BLOCK, ks],
                                    sel, pen, pen_m, bias_s[rows, :])
            sink = sink_s[rows, :]
            m = jnp.maximum(jnp.maximum(jnp.max(s_b, axis=-1, keepdims=True),
                                        jnp.max(s_m, axis=-1, keepdims=True)), sink)
            p_b = jnp.exp(s_b - m)
            p_m = jnp.exp(s_m - m)
            p_c = jnp.where(sel, p_b, 0.0).astype(BF16)
            p_p = jnp.where(sel, 0.0, p_b).astype(BF16)
            ol = (_dot(p_c, _with_ones(kvc_ref[:, vs]), NN) + _dot(p_p, _with_ones(kvp_ref[:, vs]), NN)
                  + _dot(p_m.astype(BF16), _with_ones(kvm_ref[LEAD:BLOCK, vs]), NN))
            l = ol[:, HEAD_DIM:HEAD_DIM + 1] + jnp.exp(sink - m)
            o_ref[:, GROUP * HEAD_DIM * g:GROUP * HEAD_DIM * (g + 1)] = _unstack_heads(ol[:, 0:HEAD_DIM] / l)
            lse = m + jnp.log(l)
            for j in range(GROUP):
                lse_ref[:, GROUP * g + j:GROUP * g + j + 1] = lse[BLOCK * j:BLOCK * (j + 1), :]
        pl.when(b == nb - 1)(ag_finish)

    res = _pcall(
        body, name="attn_fwd", grid=(nb,),
        in_specs=[pl.BlockSpec(memory_space=pltpu.SMEM),
                  _rows(BLOCK, 512), _rows(BLOCK, 256),
                  pl.BlockSpec((BLOCK, 256), lambda b: (jnp.maximum(b - 1, 0), 0)),
                  _full((BLOCK, 256))] + [ANY] * ns,
        out_specs=[_rows(BLOCK, 512), _rows(BLOCK, N_HEADS)] + [ANY] * ns,
        out_shape=[jax.ShapeDtypeStruct((R, 512), F32), jax.ShapeDtypeStruct((R, N_HEADS), F32)]
        + [jax.ShapeDtypeStruct((N_DEV * s.shape[0], s.shape[1]), s.dtype) for s in shards],
        scratch_shapes=_sem_pair(7 * ns) + [pltpu.SemaphoreType.DMA((ns,)),
                                            pltpu.VMEM((N_HEADS * BLOCK, BLOCK), F32),
                                            pltpu.VMEM((N_HEADS * BLOCK, 1), F32)],
        semantics=("arbitrary",),
    )(sinks, q, kv, kv, kv, *shards)
    return res[0], res[1], res[2:]


def _ln_silu(y, lg, lb):
    mu = jnp.mean(y, axis=-1, keepdims=True)
    xc = y - mu
    rstd = lax.rsqrt(jnp.mean(xc * xc, axis=-1, keepdims=True) + EPS)
    xhat = xc * rstd
    yn = xhat * lg + lb
    return yn, xhat, rstd


PHASE_ROWS = HALO + TM - 8


def _phase_scratch():
    return pltpu.VMEM((7, PHASE_ROWS, CONV_W), F32)


def _phase_copies(src_s, ph_s):
    for b in range(1, 8):
        ph_s[b - 1] = src_s[pl.ds(b, PHASE_ROWS), :]


def _shifted(src_s, ph_s, start, rows):
    a8, b = (start // 8) * 8, start % 8
    if b == 0:
        return src_s[pl.ds(a8, rows), :]
    return ph_s[b - 1, pl.ds(a8, rows), :]


def _conv_fwd(ca, cg, conv_w, conv_b, ln_g, ln_b):
    R = ca.shape[0]
    nt = R // TM
    hpt = TM // HALO

    def body(ca_ref, cg_ref, cah_ref, cgh_ref, w_ref, b_ref, lg_ref, lb_ref, oc_ref, y_ref, u_s, uph_s):
        i = pl.program_id(0)
        u_s[HALO:HALO + TM, :] = ca_ref[...] * _sigmoid(cg_ref[...])
        u_s[0:HALO, :] = jnp.where(i > 0, cah_ref[...] * _sigmoid(cgh_ref[...]), 0.0)
        _phase_copies(u_s, uph_s)
        for rc in range(TM // CONV_ROWS):
            base = rc * CONV_ROWS + HALO - (CONV_K - 1)
            acc = jnp.zeros((CONV_ROWS, CONV_W), F32) + b_ref[...]
            for k in range(CONV_K):
                acc = acc + _shifted(u_s, uph_s, base + k, CONV_ROWS) * w_ref[k:k + 1, :]
            rows = slice(rc * CONV_ROWS, (rc + 1) * CONV_ROWS)
            y_ref[rows, :] = acc
            yn, _, _ = _ln_silu(acc, lg_ref[...], lb_ref[...])
            oc_ref[rows, :] = yn * _sigmoid(yn)

    prev_halo = pl.BlockSpec((HALO, CONV_W), lambda i: (jnp.maximum(i * hpt - 1, 0), 0))
    return _pcall(
        body, name="conv_fwd", grid=(nt,),
        in_specs=[_rows(TM, CONV_W), _rows(TM, CONV_W), prev_halo, prev_halo,
                  _full((CONV_K, CONV_W)), _full((1, CONV_W)), _full((1, CONV_W)), _full((1, CONV_W))],
        out_specs=[_rows(TM, CONV_W), _rows(TM, CONV_W)],
        out_shape=[jax.ShapeDtypeStruct((R, CONV_W), F32), jax.ShapeDtypeStruct((R, CONV_W), F32)],
        scratch_shapes=[pltpu.VMEM((HALO + TM, CONV_W), F32), _phase_scratch()],
        semantics=("parallel",),
    )(ca, cg, ca, cg, conv_w, conv_b, ln_g, ln_b)


def _outproj_fwd(h0, o_attn, o_conv, ga, gc, w_out):
    R = h0.shape[0]

    def body(h_ref, oa_ref, oc_ref, ga_ref, gc_ref, w_ref, h1_ref):
        oa, oc = oa_ref[...], oc_ref[...]
        ma = (oa * _rms_stats(oa) * ga_ref[...]).astype(BF16)
        mc = (oc * _rms_stats(oc) * gc_ref[...]).astype(BF16)
        h1_ref[...] = h_ref[...] + _dot(ma, w_ref[0:512, :], NN) + _dot(mc, w_ref[512:1024, :], NN)

    return _pcall(
        body, name="outproj_fwd", grid=(R // TM,),
        in_specs=[_rows(TM, D_MODEL), _rows(TM, 512), _rows(TM, 512), _full((1, 512)), _full((1, 512)),
                  _full((D_MODEL, D_MODEL))],
        out_specs=_rows(TM, D_MODEL),
        out_shape=jax.ShapeDtypeStruct((R, D_MODEL), F32),
        semantics=("parallel",),
    )(h0, o_attn, o_conv, ga, gc, w_out)


def _target_copy(tgt_hbm, tgt_s, sem, i, first):
    if first:
        return pltpu.make_async_copy(tgt_hbm.at[pl.ds(0, TM - BLOCK)], tgt_s.at[pl.ds(BLOCK, TM - BLOCK)], sem)
    return pltpu.make_async_copy(tgt_hbm.at[pl.ds(i * TM - BLOCK, TM)], tgt_s, sem)


def _ffn_fwd(h1, g2, wg_t, wu_t, wd, gf, target):
    R = h1.shape[0]
    nt, nj = R // TM, D_FF // FF_CHUNK

    def body(h1_ref, g2_ref, wg_ref, wu_ref, wd_ref, gf_ref, tgt_hbm,
             gate_ref, up_ref, dh2_ref, loss_ref, dgf_ref, hn_s, acc_s, tgt_s, sem, act_s):
        i, j = pl.program_id(0), pl.program_id(1)

        @pl.when((i == 0) & (j == 0))
        def _():
            loss_ref[...] = jnp.zeros_like(loss_ref)
            dgf_ref[...] = jnp.zeros_like(dgf_ref)

        @pl.when(j == 0)
        def _():
            h1 = h1_ref[...]
            hn_s[...] = (h1 * _rms_stats(h1) * g2_ref[...]).astype(BF16)

            @pl.when(i == 0)
            def _():
                tgt_s[0:BLOCK, :] = jnp.zeros((BLOCK, D_MODEL), F32)
                _target_copy(tgt_hbm, tgt_s, sem, i, True).start()

            @pl.when(i > 0)
            def _():
                _target_copy(tgt_hbm, tgt_s, sem, i, False).start()

        hn = hn_s[...]
        for cs in FF_SUB:
            gate = _dot(hn, wg_ref[cs, :], NT)
            up = _dot(hn, wu_ref[cs, :], NT)
            gate_ref[:, cs] = gate.astype(BF16)
            up_ref[:, cs] = up.astype(BF16)
            act_s[:, cs] = (gate * _sigmoid(gate) * up).astype(BF16)
        part = _dot(act_s[...], wd_ref[...], NN)

        @pl.when(j == 0)
        def _():
            acc_s[...] = part

        @pl.when(j == nj - 1)
        def _():
            @pl.when(i == 0)
            def _():
                _target_copy(tgt_hbm, tgt_s, sem, i, True).wait()

            @pl.when(i > 0)
            def _():
                _target_copy(tgt_hbm, tgt_s, sem, i, False).wait()

            h2 = h1_ref[...] + acc_s[...] + part
            rf = _rms_stats(h2)
            gf = gf_ref[...]
            row = lax.broadcasted_iota(jnp.int32, (TM, 1), 0) + i * TM
            err = jnp.where(row >= BLOCK, h2 * rf * gf - tgt_s[...], 0.0)
            dy = err * (1.0 / D_MODEL)
            dh2, dgf = _rms_bwd(dy, h2, rf, gf)
            dh2_ref[...] = dh2
            loss_ref[...] += (0.5 / D_MODEL) * jnp.sum(err * err)
            dgf_ref[...] += dgf

    wspec = pl.BlockSpec((FF_CHUNK, D_MODEL), lambda i, j: (j, 0))
    aspec = pl.BlockSpec((TM, FF_CHUNK), lambda i, j: (i, j))
    return _pcall(
        body, name="ffn_fwd", grid=(nt, nj),
        in_specs=[_rows(TM, D_MODEL), _full((1, D_MODEL)), wspec, wspec, wspec, _full((1, D_MODEL)),
                  pl.BlockSpec(memory_space=pl.ANY)],
        out_specs=[aspec, aspec, _rows(TM, D_MODEL),
                   _full((8, 128)), _full((1, D_MODEL))],
        out_shape=[jax.ShapeDtypeStruct((R, D_FF), BF16), jax.ShapeDtypeStruct((R, D_FF), BF16),
                   jax.ShapeDtypeStruct((R, D_MODEL), F32),
                   jax.ShapeDtypeStruct((8, 128), F32), jax.ShapeDtypeStruct((1, D_MODEL), F32)],
        scratch_shapes=[pltpu.VMEM((TM, D_MODEL), BF16), pltpu.VMEM((TM, D_MODEL), F32),
                        pltpu.VMEM((TM, D_MODEL), F32), pltpu.SemaphoreType.DMA, pltpu.VMEM((TM, FF_CHUNK), BF16)],
        semantics=("arbitrary", "arbitrary"),
    )(h1, g2, wg_t, wu_t, wd, gf, target)


def _ffn_bwd(dh2, h1, g2, gate, up, wg_t, wu_t, wd):
    R = h1.shape[0]
    nt, nj = R // TM, D_FF // FF_CHUNK

    wspec = pl.BlockSpec((FF_CHUNK, D_MODEL), lambda i, j: (j, 0))
    aspec = pl.BlockSpec((TM, FF_CHUNK), lambda i, j: (i, j))
    act_shape = jax.ShapeDtypeStruct((R, D_FF), BF16)

    def act_body(dh2_ref, gate_ref, up_ref, wd_ref, dgate_ref, dup_ref, act_ref, dhb_s):
        @pl.when(pl.program_id(1) == 0)
        def _():
            dhb_s[...] = dh2_ref[...].astype(BF16)

        dhb = dhb_s[...]
        for cs in FF_SUB:
            dact = _dot(dhb, wd_ref[cs, :], NT)
            gate = gate_ref[:, cs].astype(F32)
            up = up_ref[:, cs].astype(F32)
            sig = _sigmoid(gate)
            silu = gate * sig
            dgate_ref[:, cs] = (dact * up * (sig * (1.0 + gate * (1.0 - sig)))).astype(BF16)
            dup_ref[:, cs] = (dact * silu).astype(BF16)
            act_ref[:, cs] = (silu * up).astype(BF16)

    dgate, dup, act = _pcall(
        act_body, name="ffn_bwd_act", grid=(nt, nj),
        in_specs=[_rows(TM, D_MODEL), aspec, aspec, wspec],
        out_specs=[aspec, aspec, aspec], out_shape=[act_shape, act_shape, act_shape],
        scratch_shapes=[pltpu.VMEM((TM, D_MODEL), BF16)],
        semantics=("parallel", "arbitrary"),
    )(dh2, gate, up, wd)

    def in_body(dh2_ref, h1_ref, g2_ref, dgate_ref, dup_ref, wg_ref, wu_ref, hn_ref, dh1_ref, dg2_ref, acc_s):
        i, j = pl.program_id(0), pl.program_id(1)

        @pl.when((i == 0) & (j == 0))
        def _():
            dg2_ref[...] = jnp.zeros_like(dg2_ref)

        part = _dot(dgate_ref[...], wg_ref[...], NN) + _dot(dup_ref[...], wu_ref[...], NN)

        @pl.when(j == 0)
        def _():
            acc_s[...] = part

        @pl.when(j == nj - 1)
        def _():
            h1 = h1_ref[...]
            r = _rms_stats(h1)
            g2 = g2_ref[...]
            hn_ref[...] = (h1 * r * g2).astype(BF16)
            dx, dg = _rms_bwd(acc_s[...] + part, h1, r, g2)
            dh1_ref[...] = dh2_ref[...] + dx
            dg2_ref[...] += dg

    hn2, dh1, dg2 = _pcall(
        in_body, name="ffn_bwd_in", grid=(nt, nj),
        in_specs=[_rows(TM, D_MODEL), _rows(TM, D_MODEL), _full((1, D_MODEL)), aspec, aspec, wspec, wspec],
        out_specs=[_rows(TM, D_MODEL), _rows(TM, D_MODEL), _full((1, D_MODEL))],
        out_shape=[jax.ShapeDtypeStruct((R, D_MODEL), BF16), jax.ShapeDtypeStruct((R, D_MODEL), F32),
                   jax.ShapeDtypeStruct((1, D_MODEL), F32)],
        scratch_shapes=[pltpu.VMEM((TM, D_MODEL), F32)],
        semantics=("arbitrary", "arbitrary"),
    )(dh2, h1, g2, dgate, dup, wg_t, wu_t)
    return dgate, dup, act, hn2, dh1, dg2


def _wgrad(a, b, tm, name, partials=()):
    K, M = a.shape
    N = b.shape[1]
    tk = K // WGRAD_K_TILES if K % (WGRAD_K_TILES * BLOCK) == 0 else TM
    nm, nk, npart = M // tm, K // tk, len(partials)

    def body(a_ref, b_ref, *refs):
        p_ins, o_ref, p_outs, sems = refs[:npart], refs[npart], refs[npart + 1:2 * npart + 1], refs[2 * npart + 1:]
        step = pl.program_id(0) * nk + pl.program_id(1)
        exchange = functools.partial(_chip_copies, p_ins, p_outs, *sems)
        if npart:
            _hosted(step, nm * nk, exchange)

        @pl.when(pl.program_id(1) == 0)
        def _():
            o_ref[...] = jnp.zeros_like(o_ref)

        o_ref[...] += _dot(a_ref[...], b_ref[...].astype(BF16), TN)
        if npart:
            _hosted_wait(step, nm * nk, exchange)

    res = _pcall(
        body, name=name, grid=(nm, nk),
        in_specs=[pl.BlockSpec((tk, tm), lambda m, k: (k, m)), pl.BlockSpec((tk, N), lambda m, k: (k, 0))]
        + [ANY] * npart,
        out_specs=[pl.BlockSpec((tm, N), lambda m, k: (m, 0))] + [ANY] * npart,
        out_shape=[jax.ShapeDtypeStruct((M, N), F32)] + _chip_shapes(partials),
        scratch_shapes=_sem_pair(3 * npart) if npart else [],
        semantics=("arbitrary", "arbitrary"),
    )(a, b, *partials)
    return (res[0], res[1:]) if npart else res[0]


def _outproj_bwd(dh1, o_attn, o_conv, ga, gc, w_out, grads):
    R = dh1.shape[0]
    nt, ng = R // TM, len(grads)

    def body(dh1_ref, oa_ref, oc_ref, ga_ref, gc_ref, w_ref, *refs):
        g_ins, (doa_ref, doc_ref, mixed_ref, dga_ref, dgc_ref) = refs[:ng], refs[ng:ng + 5]
        g_outs, (send_sems, recv_sems) = refs[ng + 5:2 * ng + 5], refs[2 * ng + 5:]
        exchange = functools.partial(_sibling_copies, g_ins, g_outs, send_sems, recv_sems)
        _hosted(pl.program_id(0), nt, exchange)

        @pl.when(pl.program_id(0) == 0)
        def _():
            dga_ref[...] = jnp.zeros_like(dga_ref)
            dgc_ref[...] = jnp.zeros_like(dgc_ref)

        dm = _dot(dh1_ref[...].astype(BF16), w_ref[...], NT)
        oa, oc = oa_ref[...], oc_ref[...]
        ra, rc = _rms_stats(oa), _rms_stats(oc)
        mixed_ref[:, 0:512] = (oa * ra * ga_ref[...]).astype(BF16)
        mixed_ref[:, 512:1024] = (oc * rc * gc_ref[...]).astype(BF16)
        doa, dga = _rms_bwd(dm[:, 0:512], oa, ra, ga_ref[...])
        doc, dgc = _rms_bwd(dm[:, 512:1024], oc, rc, gc_ref[...])
        doa_ref[...] = doa
        doc_ref[...] = doc
        dga_ref[...] += dga
        dgc_ref[...] += dgc
        _hosted_wait(pl.program_id(0), nt, exchange)

    res = _pcall(
        body, name="outproj_bwd", grid=(nt,),
        in_specs=[_rows(TM, D_MODEL), _rows(TM, 512), _rows(TM, 512), _full((1, 512)), _full((1, 512)),
                  _full((D_MODEL, D_MODEL))] + [ANY] * ng,
        out_specs=[_rows(TM, 512), _rows(TM, 512), _rows(TM, D_MODEL), _full((1, 512)), _full((1, 512))]
        + [ANY] * ng,
        out_shape=[jax.ShapeDtypeStruct((R, 512), F32), jax.ShapeDtypeStruct((R, 512), F32),
                   jax.ShapeDtypeStruct((R, D_MODEL), BF16),
                   jax.ShapeDtypeStruct((1, 512), F32), jax.ShapeDtypeStruct((1, 512), F32)]
        + _sibling_shapes(grads),
        scratch_shapes=_sem_pair(ng),
        semantics=("arbitrary",),
    )(dh1, o_attn, o_conv, ga, gc, w_out, *grads)
    return res[:5], res[5:]


def _conv_bwd(do_conv, y, ca, cg, conv_w, ln_g, ln_b, partials):
    R = ca.shape[0]
    nt = R // TM
    hpt = TM // HALO
    npart = len(partials)

    def body(do_ref, doh_ref, y_ref, yh_ref, ca_ref, cg_ref, cah_ref, cgh_ref, w_ref, lg_ref, lb_ref, *refs):
        p_ins, (dca_ref, dcg_ref, dw_ref, db_ref, dlg_ref, dlb_ref) = refs[:npart], refs[npart:npart + 6]
        p_outs, (send_sems, recv_sems, u_s, dy_s, uph_s, dyph_s) = refs[npart + 6:2 * npart + 6], refs[2 * npart + 6:]
        i = pl.program_id(0)
        exchange = functools.partial(_chip_copies, p_ins, p_outs, send_sems, recv_sems)
        _hosted(i, nt, exchange)

        @pl.when(i == 0)
        def _():
            dw_ref[...] = jnp.zeros_like(dw_ref)
            db_ref[...] = jnp.zeros_like(db_ref)
            dlg_ref[...] = jnp.zeros_like(dlg_ref)
            dlb_ref[...] = jnp.zeros_like(dlb_ref)

        lg, lb = lg_ref[...], lb_ref[...]

        def ln_bwd(yv, dov):
            yn, xhat, rstd = _ln_silu(yv, lg, lb)
            sig = _sigmoid(yn)
            dyn = dov * (sig * (1.0 + yn * (1.0 - sig)))
            dxh = dyn * lg
            dyv = rstd * (dxh - jnp.mean(dxh, axis=-1, keepdims=True)
                          - xhat * jnp.mean(dxh * xhat, axis=-1, keepdims=True))
            return dyv, dyn, xhat

        dyv, dyn, xhat = ln_bwd(y_ref[...], do_ref[...])
        dy_s[0:TM, :] = dyv
        dlg_ref[...] += jnp.sum(dyn * xhat, axis=0, keepdims=True)
        dlb_ref[...] += jnp.sum(dyn, axis=0, keepdims=True)
        db_ref[...] += jnp.sum(dyv, axis=0, keepdims=True)
        dyh, _, _ = ln_bwd(yh_ref[...], doh_ref[...])
        dy_s[TM:TM + HALO, :] = jnp.where(i < nt - 1, dyh, 0.0)
        u_s[HALO:HALO + TM, :] = ca_ref[...] * _sigmoid(cg_ref[...])
        u_s[0:HALO, :] = jnp.where(i > 0, cah_ref[...] * _sigmoid(cgh_ref[...]), 0.0)
        _phase_copies(dy_s, dyph_s)
        _phase_copies(u_s, uph_s)

        for rc in range(TM // CONV_ROWS):
            acc = jnp.zeros((CONV_ROWS, CONV_W), F32)
            for k in range(CONV_K):
                acc = acc + _shifted(dy_s, dyph_s, rc * CONV_ROWS + CONV_K - 1 - k, CONV_ROWS) * w_ref[k:k + 1, :]
            rows = slice(rc * CONV_ROWS, (rc + 1) * CONV_ROWS)
            sg = _sigmoid(cg_ref[rows, :])
            dca_ref[rows, :] = (acc * sg).astype(BF16)
            dcg_ref[rows, :] = (acc * ca_ref[rows, :] * sg * (1.0 - sg)).astype(BF16)

        for k in range(CONV_K):
            prod = _shifted(u_s, uph_s, HALO - (CONV_K - 1) + k, TM) * dy_s[0:TM, :]
            dw_ref[k:k + 1, :] += jnp.sum(prod, axis=0, keepdims=True)
        _hosted_wait(i, nt, exchange)

    prev_halo = pl.BlockSpec((HALO, CONV_W), lambda i: (jnp.maximum(i * hpt - 1, 0), 0))
    next_halo = pl.BlockSpec((HALO, CONV_W), lambda i: (jnp.minimum((i + 1) * hpt, nt * hpt - 1), 0))
    vec = jax.ShapeDtypeStruct((1, CONV_W), F32)
    res = _pcall(
        body, name="conv_bwd", grid=(nt,),
        in_specs=[_rows(TM, CONV_W), next_halo, _rows(TM, CONV_W), next_halo,
                  _rows(TM, CONV_W), _rows(TM, CONV_W), prev_halo, prev_halo,
                  _full((CONV_K, CONV_W)), _full((1, CONV_W)), _full((1, CONV_W))] + [ANY] * npart,
        out_specs=[_rows(TM, CONV_W), _rows(TM, CONV_W), _full((32, CONV_W)),
                   _full((1, CONV_W)), _full((1, CONV_W)), _full((1, CONV_W))] + [ANY] * npart,
        out_shape=[jax.ShapeDtypeStruct((R, CONV_W), BF16), jax.ShapeDtypeStruct((R, CONV_W), BF16),
                   jax.ShapeDtypeStruct((32, CONV_W), F32), vec, vec, vec] + _chip_shapes(partials),
        scratch_shapes=_sem_pair(3 * npart)
        + [pltpu.VMEM((HALO + TM, CONV_W), F32), pltpu.VMEM((TM + HALO, CONV_W), F32),
           _phase_scratch(), _phase_scratch()],
        semantics=("arbitrary",),
    )(do_conv, do_conv, y, y, ca, cg, ca, cg, conv_w, ln_g, ln_b, *partials)
    return res[:6], res[6:]


def _attn_bwd(q, kv, sinks, o, lse, do, grads, partials):
    R = q.shape[0]
    nb = R // BLOCK
    ng, npart = len(grads), len(partials)
    nx = ng + npart

    def body(sink_ref, q_ref, kvc_ref, kvp_ref, kvm_ref, o_ref, lse_ref, do_ref, *refs):
        x_ins, (dq_ref, dkv_ref, dkvm_ref, dsink_ref) = refs[:nx], refs[nx:nx + 4]
        x_outs = refs[nx + 4:2 * nx + 4]
        g_send, g_recv, p_send, p_recv, carry_s, cur_s, prev_s, bias_s, sink_s = refs[2 * nx + 4:]
        b = pl.program_id(0)

        def exchange():
            return (_sibling_copies(x_ins[:ng], x_outs[:ng], g_send, g_recv)
                    + _chip_copies(x_ins[ng:], x_outs[ng:], p_send, p_recv))

        _hosted(b, nb + 1, exchange)

        @pl.when(b == 0)
        def _():
            dkvm_ref[...] = jnp.zeros_like(dkvm_ref)
            carry_s[...] = jnp.zeros_like(carry_s)
            for h in range(N_HEADS):
                dsink_ref[0, h] = 0.0
            _attn_tables(sink_ref, bias_s, sink_s)

        @pl.when(b < nb)
        def _():
            sel, pen, pen_m = _attn_masks(b)
            for g in range(N_HEADS // GROUP):
                ks, vs = slice(HEAD_DIM * g, HEAD_DIM * (g + 1)), slice(KV_W + HEAD_DIM * g, KV_W + HEAD_DIM * (g + 1))
                rows = slice(GB * g, GB * (g + 1))
                qg = _stack_heads(q_ref, g) * ATTN_SCALE
                kc, kp, km = kvc_ref[:, ks], kvp_ref[:, ks], kvm_ref[LEAD:BLOCK, ks]
                vc, vp, vm = kvc_ref[:, vs], kvp_ref[:, vs], kvm_ref[LEAD:BLOCK, vs]
                s_b, s_m = _attn_scores(qg, kc, kp, km, sel, pen, pen_m, bias_s[rows, :])
                lse = jnp.concatenate(
                    [lse_ref[:, GROUP * g + j:GROUP * g + j + 1] for j in range(GROUP)], axis=0)
                p_b = jnp.exp(s_b - lse)
                p_m = jnp.exp(s_m - lse)
                dog = _stack_heads(do_ref, g)
                delta = jnp.sum(dog * _stack_heads(o_ref, g), axis=-1, keepdims=True)
                dob = dog.astype(BF16)
                dp_b = jnp.where(sel, _dot(dob, vc, NT), _dot(dob, vp, NT))
                ds_b = p_b * (dp_b - delta)
                ds_m = (p_m * (_dot(dob, vm, NT) - delta)).astype(BF16)
                dsk = jnp.exp(sink_s[rows, :] - lse) * delta
                for j in range(GROUP):
                    dsink_ref[0, GROUP * g + j] += -jnp.sum(dsk[BLOCK * j:BLOCK * (j + 1), :])
                ds_c = jnp.where(sel, ds_b, 0.0).astype(BF16)
                ds_p = jnp.where(sel, 0.0, ds_b).astype(BF16)
                p_c = jnp.where(sel, p_b, 0.0).astype(BF16)
                p_p = jnp.where(sel, 0.0, p_b).astype(BF16)
                dq = (_dot(ds_c, kc, NN) + _dot(ds_p, kp, NN) + _dot(ds_m, km, NN)) * ATTN_SCALE
                dq_ref[:, GROUP * HEAD_DIM * g:GROUP * HEAD_DIM * (g + 1)] = _unstack_heads(dq).astype(BF16)
                cur_s[:, ks] = _dot(ds_c, qg, TN)
                cur_s[:, vs] = _dot(p_c, dob, TN)
                prev_s[:, ks] = _dot(ds_p, qg, TN)
                prev_s[:, vs] = _dot(p_p, dob, TN)
                dkvm_ref[:, ks] += _dot(ds_m, qg, TN)
                dkvm_ref[:, vs] += _dot(p_m.astype(BF16), dob, TN)
            dkv_ref[...] = (carry_s[...] + prev_s[...]).astype(BF16)
            carry_s[...] = cur_s[...]

        @pl.when(b == nb)
        def _():
            dkv_ref[...] = carry_s[...].astype(BF16)

        _hosted_wait(b, nb + 1, exchange)

    def at(off):
        return lambda b: (jnp.clip(b + off, 0, nb - 1), 0)

    blk = lambda cols, off=0: pl.BlockSpec((BLOCK, cols), at(off))
    res = _pcall(
        body, name="attn_bwd", grid=(nb + 1,),
        in_specs=[pl.BlockSpec(memory_space=pltpu.SMEM), blk(512), blk(256), blk(256, -1), _full((BLOCK, 256)),
                  blk(512), blk(N_HEADS), blk(512)] + [ANY] * nx,
        out_specs=[blk(512), blk(256, -1), _full((N_META, 256)), pl.BlockSpec(memory_space=pltpu.SMEM)]
        + [ANY] * nx,
        out_shape=[jax.ShapeDtypeStruct((R, 512), BF16), jax.ShapeDtypeStruct((R, 256), BF16),
                   jax.ShapeDtypeStruct((N_META, 256), F32), jax.ShapeDtypeStruct((1, N_HEADS), F32)]
        + _sibling_shapes(grads) + _chip_shapes(partials),
        scratch_shapes=_sem_pair(ng) + _sem_pair(3 * npart) + [pltpu.VMEM((BLOCK, 256), F32)] * 3
        + [pltpu.VMEM((N_HEADS * BLOCK, BLOCK), F32), pltpu.VMEM((N_HEADS * BLOCK, 1), F32)],
        semantics=("arbitrary",),
    )(sinks, q, kv, kv, kv, o, lse, do, *grads, *partials)
    return res[:4], res[4:4 + ng], res[4 + ng:]


def _inproj_bwd(dh1, h0, g1, dq, dkv, dkvm, dca, dcg, w_in_t):
    R = h0.shape[0]
    nt = R // TM
    assert nt >= 2

    def body(dh1_ref, h0_ref, g_ref, dq_ref, dkv_ref, dkvm_ref, dca_ref, dcg_ref, w_ref,
             gx_hbm, dmeta_ref, dproj_ref, hn_ref, dg_ref, dx_s, gx_sems):
        i = pl.program_id(0)

        def gx_copy(step, slot, first):
            if first:
                return pltpu.make_async_copy(dx_s.at[slot, pl.ds(BLOCK, TM - BLOCK)],
                                             gx_hbm.at[pl.ds(0, TM - BLOCK)], gx_sems.at[slot])
            return pltpu.make_async_copy(
                dx_s.at[slot], gx_hbm.at[pl.ds(pl.multiple_of(step * TM - BLOCK, BLOCK), TM)], gx_sems.at[slot])

        @pl.when(i == 0)
        def _():
            dg_ref[...] = jnp.zeros_like(dg_ref)

        dproj_ref[:, 0:512] = dq_ref[...]
        dproj_ref[:, 512:768] = dkv_ref[...]
        dproj_ref[:, 768:1280] = dca_ref[...]
        dproj_ref[:, 1280:1792] = dcg_ref[...]

        @pl.when(i == 0)
        def _():
            dproj_ref[LEAD:BLOCK, 512:768] = dkvm_ref[...].astype(BF16)

        dhn = _dot(dproj_ref[...], w_ref[...], NN)
        h = h0_ref[...]
        r = _rms_stats(h)
        g = g_ref[...]
        hn_ref[...] = (h * r * g).astype(BF16)
        dx, dg = _rms_bwd(dhn, h, r, g)
        dg_ref[...] += dg
        slot = i % 2
        pl.when(i == 2)(lambda: gx_copy(0, 0, True).wait())
        pl.when(i > 2)(lambda: gx_copy(i - 2, slot, False).wait())
        dx_s[slot] = dh1_ref[...] + dx

        @pl.when(i == 0)
        def _():
            dmeta_ref[...] = dx_s[0, LEAD:BLOCK, :]
            gx_copy(0, 0, True).start()

        pl.when(i > 0)(lambda: gx_copy(i, slot, False).start())

        @pl.when(i == nt - 1)
        def _():
            gx_copy(nt - 2, (nt - 2) % 2, nt == 2).wait()
            gx_copy(nt - 1, (nt - 1) % 2, False).wait()

    return _pcall(
        body, name="inproj_bwd", grid=(nt,),
        in_specs=[_rows(TM, D_MODEL), _rows(TM, D_MODEL), _full((1, D_MODEL)), _rows(TM, 512), _rows(TM, 256),
                  _full((N_META, 256)), _rows(TM, 512), _rows(TM, 512), _full((1792, D_MODEL))],
        out_specs=[ANY, _full((N_META, D_MODEL)), _rows(TM, 1792), _rows(TM, D_MODEL), _full((1, D_MODEL))],
        out_shape=[jax.ShapeDtypeStruct((R - BLOCK, D_MODEL), F32), jax.ShapeDtypeStruct((N_META, D_MODEL), F32),
                   jax.ShapeDtypeStruct((R, 1792), BF16),
                   jax.ShapeDtypeStruct((R, D_MODEL), BF16), jax.ShapeDtypeStruct((1, D_MODEL), F32)],
        scratch_shapes=[pltpu.VMEM((2, TM, D_MODEL), F32), pltpu.SemaphoreType.DMA((2,))],
        semantics=("arbitrary",),
    )(dh1, h0, g1, dq, dkv, dkvm, dca, dcg, w_in_t)


ANY = pl.BlockSpec(memory_space=pl.ANY)


def _position():
    return lax.axis_index("x"), lax.axis_index("y"), lax.axis_index("c")


def _device_number(p):
    return 4 * p[0] + 2 * p[1] + p[2]


def _two_level_allgather(ins, outs, block, send_sems, recv_sems, local_sems, sem_base=0):
    n = len(ins)
    x, y, c = _position()
    me, sibling = (x, y, c), (x, y, 1 - c)
    chips = [(1 - x, y), (x, 1 - y), (1 - x, 1 - y)]

    def copy(w, k, origin, to, src=None):
        return pltpu.make_async_remote_copy(
            src_ref=block(w, origin) if src is None else src, dst_ref=block(w, origin),
            send_sem=send_sems.at[sem_base + 7 * w + k], recv_sem=recv_sems.at[sem_base + 7 * w + k],
            device_id=to, device_id_type=MESH)

    def mine(w):
        return pltpu.make_async_copy(ins[w], block(w, me), local_sems.at[w])

    def own(w):
        return [copy(w, 0, me, sibling, src=ins[w])] + [
            copy(w, 1 + j, me, (*chip, c), src=ins[w]) for j, chip in enumerate(chips)]

    def passed(w):
        return [copy(w, 4 + j, (*chip, c), sibling) for j, chip in enumerate(chips)]

    def start():
        for w in range(n):
            mine(w).start()
        for w in range(n):
            for cp in own(w):
                cp.start()

    def forward(w):
        fw = passed(w)
        for j, chip in enumerate(chips):
            copy(w, 1 + j, (*chip, c), me).wait_recv()
            fw[j].start()

    def finish():
        for w in range(n):
            copy(w, 0, sibling, me).wait_recv()
            for j, chip in enumerate(chips):
                copy(w, 4 + j, (*chip, 1 - c), me).wait_recv()
        for w in range(n):
            for cp in own(w) + passed(w):
                cp.wait_send()
            mine(w).wait()

    return start, forward, finish


def _blocking_allgather(ins, outs, block, send_sems, recv_sems, local_sems, sem_base=0):
    start, forward, finish = _two_level_allgather(ins, outs, block, send_sems, recv_sems, local_sems, sem_base)
    start()
    for w in range(len(ins)):
        forward(w)
    finish()


def _row_block(outs, rows):
    def block(w, p):
        return outs[w].at[pl.ds(pl.multiple_of(_device_number(p) * rows[w], 16), rows[w])]
    return block


def _sibling_copies(ins, outs, send_sems, recv_sems):
    x, y, c = _position()
    return [pltpu.make_async_remote_copy(
        src_ref=ins[w].at[:, 1 - c], dst_ref=outs[w], send_sem=send_sems.at[w], recv_sem=recv_sems.at[w],
        device_id=(x, y, 1 - c), device_id_type=MESH) for w in range(len(ins))]


def _chip_copies(ins, outs, send_sems, recv_sems):
    x, y, c = _position()
    chips = [(1 - x, y), (x, 1 - y), (1 - x, 1 - y)]
    return [pltpu.make_async_remote_copy(
        src_ref=ins[w].at[2 * chip[0] + chip[1]], dst_ref=outs[w].at[k],
        send_sem=send_sems.at[3 * w + k], recv_sem=recv_sems.at[3 * w + k],
        device_id=(*chip, c), device_id_type=MESH) for w in range(len(ins)) for k, chip in enumerate(chips)]


def _hosted(step, n_steps, make_copies):
    @pl.when(step == 0)
    def _():
        for cp in make_copies():
            cp.start()


def _hosted_wait(step, n_steps, make_copies):
    @pl.when(step == n_steps - 1)
    def _():
        for cp in make_copies():
            cp.wait()


def _sem_pair(n):
    return [pltpu.SemaphoreType.DMA((n,)), pltpu.SemaphoreType.DMA((n,))]


def _allgather_params(shards, small):
    arrays = list(shards) + list(small)
    n, ns = len(arrays), len(shards)

    def body(*refs):
        ins, outs = refs[:n], refs[n:2 * n]
        send_sems, recv_sems, local_sems = refs[2 * n:]

        rows = _row_block(outs, [a.shape[0] for a in arrays])

        def block(w, p):
            return rows(w, p) if w < ns else outs[w].at[_device_number(p)]

        _blocking_allgather(ins, outs, block, send_sems, recv_sems, local_sems)

    out_shape = [jax.ShapeDtypeStruct((N_DEV * a.shape[0], a.shape[1]), a.dtype) for a in shards]
    out_shape += [jax.ShapeDtypeStruct((N_DEV,) + a.shape, a.dtype) for a in small]
    return _pcall(
        body, name="allgather_params", in_specs=[ANY] * n, out_specs=[ANY] * n, out_shape=out_shape,
        scratch_shapes=[pltpu.SemaphoreType.DMA((7 * n,)), pltpu.SemaphoreType.DMA((7 * n,)),
                        pltpu.SemaphoreType.DMA((n,))],
    )(*arrays)


def _reduce_siblings(grads):
    n = len(grads)

    def body(*refs):
        ins, outs = refs[:n], refs[n:2 * n]
        send_sems, recv_sems = refs[2 * n:]
        copies = _sibling_copies(ins, outs, send_sems, recv_sems)
        for cp in copies:
            cp.start()
        for cp in copies:
            cp.wait()

    return _pcall(
        body, name="reduce_siblings", in_specs=[ANY] * n, out_specs=[ANY] * n,
        out_shape=_sibling_shapes(grads), scratch_shapes=_sem_pair(n),
    )(*grads)


def _sibling_shapes(grads):
    return [jax.ShapeDtypeStruct((4,) + g.shape[2:], F32) for g in grads]


def _chip_shapes(partials):
    return [jax.ShapeDtypeStruct((3,) + p.shape[1:], F32) for p in partials]


def _add_sibling(grad, received, core, name):
    _, _, r, cols = grad.shape

    def body(core_ref, g_ref, r_ref, o_ref):
        o_ref[...] = g_ref[...] + r_ref[...]

    return pl.pallas_call(
        body, name=name,
        grid_spec=pltpu.PrefetchScalarGridSpec(
            num_scalar_prefetch=1, grid=(4,),
            in_specs=[pl.BlockSpec((None, None, r, cols), lambda s, core_ref: (s, core_ref[0], 0, 0)),
                      pl.BlockSpec((None, r, cols), lambda s, core_ref: (s, 0, 0))],
            out_specs=pl.BlockSpec((None, r, cols), lambda s, core_ref: (s, 0, 0))),
        out_shape=jax.ShapeDtypeStruct((4, r, cols), F32),
        compiler_params=pltpu.CompilerParams(vmem_limit_bytes=VMEM_LIMIT),
    )(core, grad, received)


def _reduce_chips(partials, small):
    n, ns = len(partials), len(small)

    def body(*refs):
        p_ins, s_ins = refs[:n], refs[n:n + ns]
        p_outs, s_outs = refs[n + ns:2 * n + ns], refs[2 * n + ns:2 * (n + ns)]
        send_sems, recv_sems, local_sems = refs[2 * (n + ns):]
        copies = _chip_copies(p_ins, p_outs, send_sems, recv_sems)
        for cp in copies:
            cp.start()
        _blocking_allgather(s_ins, s_outs, lambda w, p: s_outs[w].at[_device_number(p)],
                            send_sems, recv_sems, local_sems, sem_base=3 * n)
        for cp in copies:
            cp.wait()

    out_shape = _chip_shapes(partials)
    out_shape += [jax.ShapeDtypeStruct((N_DEV,) + a.shape, a.dtype) for a in small]
    nsem = 3 * n + 7 * ns
    return _pcall(
        body, name="reduce_chips", in_specs=[ANY] * (n + ns), out_specs=[ANY] * (n + ns), out_shape=out_shape,
        scratch_shapes=[pltpu.SemaphoreType.DMA((nsem,)), pltpu.SemaphoreType.DMA((nsem,)),
                        pltpu.SemaphoreType.DMA((ns,))],
    )(*partials, *small)


def _adam(g, w, m, v):
    m = ADAM_B1 * m + (1.0 - ADAM_B1) * g
    v = ADAM_B2 * v + (1.0 - ADAM_B2) * (g * g)
    m_hat = m / (1.0 - ADAM_B1 ** ADAM_STEP)
    v_hat = v / (1.0 - ADAM_B2 ** ADAM_STEP)
    delta = -ADAM_LR * (m_hat / (jnp.sqrt(v_hat) + ADAM_EPS) + ADAM_WD * w)
    return delta, m, v


def _adamw(partial, received, slot, w, m, v, name):
    _, r, cols = partial.shape

    def body(slot_ref, p_ref, r_ref, w_ref, m_ref, v_ref, g_ref, d_ref, nm_ref, nv_ref):
        g = p_ref[...] + r_ref[0] + r_ref[1] + r_ref[2]
        g_ref[...] = g
        d_ref[...], nm_ref[...], nv_ref[...] = _adam(g, w_ref[...], m_ref[...], v_ref[...])

    whole = pl.BlockSpec((r, cols), lambda i, slot_ref: (0, 0))
    out = jax.ShapeDtypeStruct((r, cols), F32)
    return pl.pallas_call(
        body, name=name,
        grid_spec=pltpu.PrefetchScalarGridSpec(
            num_scalar_prefetch=1, grid=(1,),
            in_specs=[pl.BlockSpec((None, r, cols), lambda i, slot_ref: (slot_ref[0], 0, 0)),
                      pl.BlockSpec((3, r, cols), lambda i, slot_ref: (0, 0, 0)), whole, whole, whole],
            out_specs=[whole, whole, whole, whole]),
        out_shape=[out, out, out, out],
        compiler_params=pltpu.CompilerParams(vmem_limit_bytes=VMEM_LIMIT),
    )(slot, partial, received, w, m, v)


def _adamw_small(dev, ga, gb, gc, params):
    names = ["meta", "attn_norm", "sinks", "conv_w", "conv_b", "ln_g", "ln_b", "attn_out", "conv_out",
             "ffn_norm", "final_norm"]
    flat = [a for p in params for a in p]
    n_in = len(flat)

    def body(dev_ref, ga_ref, gb_ref, gc_ref, *refs):
        ins, outs = refs[:n_in], refs[n_in:n_in + 4 * len(names)]
        loss_ref, sb, sc = refs[n_in + 4 * len(names):]
        a = ga_ref[0]
        sb[...] = gb_ref[0]
        sc[...] = gc_ref[0]
        for d in range(1, N_DEV):
            a = a + ga_ref[d]
            sb[...] += gb_ref[d]
            sc[...] += gc_ref[d]
        dev = dev_ref[0]
        grads = {
            "attn_norm": a[0:1, :], "ffn_norm": a[1:2, :], "final_norm": a[2:3, :],
            "conv_b": a[3:4, 0:512], "ln_g": a[3:4, 512:1024], "ln_b": a[4:5, 0:512],
            "attn_out": a[4:5, 512:1024], "conv_out": a[5:6, 0:512], "sinks": a[5:6, 512:512 + N_HEADS],
            "meta": sb[pl.ds(pl.multiple_of(dev * N_META, N_META), N_META), :],
            "conv_w": sc[pl.ds(pl.multiple_of(dev * 32, 32), 32), :][0:CONV_K, :],
        }
        for idx, nm in enumerate(names):
            w_ref, m_ref, v_ref = ins[3 * idx:3 * idx + 3]
            g = grads[nm]
            delta, m, v = _adam(g, w_ref[...], m_ref[...], v_ref[...])
            o = outs[4 * idx:4 * idx + 4]
            o[0][...], o[1][...], o[2][...], o[3][...] = g, delta, m, v
        loss_ref[...] = a[6:7, 0:1]

    vm = pl.BlockSpec(memory_space=pltpu.VMEM)
    out_shape = [jax.ShapeDtypeStruct(p[0].shape, F32) for p in params for _ in range(4)]
    out_shape.append(jax.ShapeDtypeStruct((1, 1), F32))
    res = pl.pallas_call(
        body, name="adamw_small",
        grid_spec=pltpu.PrefetchScalarGridSpec(
            num_scalar_prefetch=1, grid=(1,),
            in_specs=[pl.BlockSpec(ga.shape, lambda i, d: (0, 0, 0)), pl.BlockSpec(gb.shape, lambda i, d: (0, 0, 0)),
                      pl.BlockSpec(gc.shape, lambda i, d: (0, 0, 0))]
            + [pl.BlockSpec(a.shape, lambda i, d: (0, 0)) for a in flat],
            out_specs=[pl.BlockSpec(s.shape, lambda i, d: (0, 0)) for s in out_shape],
            scratch_shapes=[pltpu.VMEM(gb.shape[1:], F32), pltpu.VMEM(gc.shape[1:], F32)]),
        out_shape=out_shape,
        compiler_params=pltpu.CompilerParams(vmem_limit_bytes=VMEM_LIMIT),
    )(dev, ga, gb, gc, *flat)
    return [res[4 * i:4 * i + 4] for i in range(len(names))], res[-1]


def kernel(x, meta_tokens, attn_norm_g, w_in, attn_sinks, conv_w, conv_b, conv_ln_g, conv_ln_b, attn_out_g, conv_out_g, w_out, ffn_norm_g, w_gate, w_up, w_down, final_norm_g, loss_target, m_meta_tokens, m_attn_norm_g, m_w_in, m_attn_sinks, m_conv_w, m_conv_b, m_conv_ln_g, m_conv_ln_b, m_attn_out_g, m_conv_out_g, m_w_out, m_ffn_norm_g, m_w_gate, m_w_up, m_w_down, m_final_norm_g, v_meta_tokens, v_attn_norm_g, v_w_in, v_attn_sinks, v_conv_w, v_conv_b, v_conv_ln_g, v_conv_ln_b, v_attn_out_g, v_conv_out_g, v_w_out, v_ffn_norm_g, v_w_gate, v_w_up, v_w_down, v_final_norm_g):
    xi, yi, ci = _position()
    dev = jnp.reshape(_device_number((xi, yi, ci)), (1,)).astype(jnp.int32)
    core = jnp.reshape(ci, (1,)).astype(jnp.int32)
    slot = jnp.reshape(2 * xi + yi, (1,)).astype(jnp.int32)

    w_in_t, meta_st, convw_st = _allgather_params([w_in[0].T.astype(BF16)], [meta_tokens, conv_w[0]])
    later = [w_out[0].astype(BF16), w_gate[0].T.astype(BF16), w_up[0].T.astype(BF16), w_down[0].astype(BF16)]
    meta_full = jnp.transpose(meta_st, (1, 0, 2)).reshape(N_META, D_MODEL)
    convw_full = jnp.transpose(convw_st, (1, 0, 2)).reshape(CONV_K, CONV_W)

    final_g = final_norm_g.reshape(1, D_MODEL)

    h0, q, kv, ca, cg = _inproj_fwd(x[0], meta_full, attn_norm_g, w_in_t)
    o_attn, lse, (w_out_b, wg_t, wu_t, wd_b) = _attn_fwd(q, kv, attn_sinks, later)
    o_conv, y_conv = _conv_fwd(ca, cg, convw_full, conv_b, conv_ln_g, conv_ln_b)
    h1 = _outproj_fwd(h0, o_attn, o_conv, attn_out_g, conv_out_g, w_out_b)
    gate, up, dh2, loss_sum, dg_final = _ffn_fwd(h1, ffn_norm_g, wg_t, wu_t, wd_b, final_g, loss_target[0])

    def blocks(g):
        return g.reshape(4, 2, g.shape[0] // N_DEV, D_MODEL)

    def add_siblings(grads, received, tags):
        return [_add_sibling(g, r, core, "add_sibling_" + t) for g, r, t in zip(grads, received, tags)]

    dgate, dup, act, hn2, dh1, dg_ffn = _ffn_bwd(dh2, h1, ffn_norm_g, gate, up, wg_t, wu_t, wd_b)
    ffn_grads = [blocks(_wgrad(dgate, hn2, FF_CHUNK, "wgrad_gate")), blocks(_wgrad(dup, hn2, FF_CHUNK, "wgrad_up")),
                 blocks(_wgrad(act, dh2, FF_CHUNK, "wgrad_down"))]
    (do_attn, do_conv, mixed, dg_ao, dg_co), ffn_sib = _outproj_bwd(
        dh1, o_attn, o_conv, attn_out_g, conv_out_g, w_out_b, ffn_grads)
    ffn_sums = add_siblings(ffn_grads, ffn_sib, ("gate", "up", "down"))
    out_grads = [blocks(_wgrad(mixed, dh1, D_MODEL, "wgrad_out"))]
    (dca, dcg, dconvw, dconvb, dln_g, dln_b), gate_up_chips = _conv_bwd(
        do_conv, y_conv, ca, cg, convw_full, conv_ln_g, conv_ln_b, ffn_sums[:2])
    (dq, dkv, dkvm, dsinks), out_sib, down_chips = _attn_bwd(
        q, kv, attn_sinks, o_attn, lse, do_attn, out_grads, ffn_sums[2:])
    ffn_chips = list(gate_up_chips) + list(down_chips)
    out_sums = add_siblings(out_grads, out_sib, ("out",))
    grad_x, dmeta, dproj, hn1, dg_attn = _inproj_bwd(dh1, h0, attn_norm_g, dq, dkv, dkvm, dca, dcg, w_in_t)
    dwi_t, out_chips = _wgrad(dproj, hn1, 1792, "wgrad_in", out_sums)
    in_grads = [blocks(dwi_t)]
    in_sums = add_siblings(in_grads, _reduce_siblings(in_grads), ("in",))
    small_a = jnp.concatenate([
        dg_attn, dg_ffn, dg_final, jnp.concatenate([dconvb, dln_g], axis=1), jnp.concatenate([dln_b, dg_ao], axis=1),
        jnp.concatenate([dg_co, dsinks, jnp.zeros((1, 512 - N_HEADS), F32)], axis=1),
        jnp.concatenate([loss_sum[0:1, :], jnp.zeros((1, D_MODEL - 128), F32)], axis=1),
        jnp.zeros((1, D_MODEL), F32)], axis=0)
    small_b = jnp.transpose(dmeta.reshape(N_META, N_DEV, 128), (1, 0, 2)).reshape(N_DEV * N_META, 128)
    small_c = jnp.transpose(dconvw.reshape(32, N_DEV, 64), (1, 0, 2)).reshape(N_DEV * 32, 64)
    in_chips, ga, gb, gc = _reduce_chips(in_sums, [small_a, small_b, small_c])
    tags = ("in", "out", "gate", "up", "down")
    chip_sums = in_sums + out_sums + ffn_sums
    from_chips = [in_chips] + list(out_chips) + list(ffn_chips)

    big = [(True, w_in, m_w_in, v_w_in), (False, w_out, m_w_out, v_w_out), (True, w_gate, m_w_gate, v_w_gate),
           (True, w_up, m_w_up, v_w_up), (False, w_down, m_w_down, v_w_down)]
    big_out = {}
    for t, p, r, (transposed, w, m, v) in zip(tags, chip_sums, from_chips, big):
        rows = (lambda a: jnp.transpose(a[0])) if transposed else (lambda a: a[0])
        back = (lambda a: jnp.transpose(a)[None]) if transposed else (lambda a: a[None])
        big_out[t] = [back(a) for a in _adamw(p, r, slot, rows(w), rows(m), rows(v), "adamw_" + t)]

    small_params = [
        (meta_tokens, m_meta_tokens, v_meta_tokens), (attn_norm_g, m_attn_norm_g, v_attn_norm_g),
        (attn_sinks, m_attn_sinks, v_attn_sinks), (conv_w[0], m_conv_w[0], v_conv_w[0]),
        (conv_b, m_conv_b, v_conv_b), (conv_ln_g, m_conv_ln_g, v_conv_ln_g), (conv_ln_b, m_conv_ln_b, v_conv_ln_b),
        (attn_out_g, m_attn_out_g, v_attn_out_g), (conv_out_g, m_conv_out_g, v_conv_out_g),
        (ffn_norm_g, m_ffn_norm_g, v_ffn_norm_g),
        (final_g, m_final_norm_g.reshape(1, D_MODEL), v_final_norm_g.reshape(1, D_MODEL))]
    sm, loss = _adamw_small(dev, ga, gb, gc, small_params)
    sm[3] = [a[None] for a in sm[3]]
    sm[10] = [a.reshape(D_MODEL) for a in sm[10]]

    per_param = [sm[0], sm[1], big_out["in"], sm[2], sm[3], sm[4], sm[5], sm[6], sm[7], sm[8], big_out["out"],
                 sm[9], big_out["gate"], big_out["up"], big_out["down"], sm[10]]
    loss = loss.reshape(())
    outs = [loss, grad_x[None]]
    for kind in range(4):
        outs += [p[kind] for p in per_param]
    return tuple(outs)
```

```python
import functools
import math

import jax
import jax.numpy as jnp
from jax import lax
from jax.experimental import pallas as pl
from jax.experimental.pallas import tpu as pltpu

F32, BF16 = jnp.float32, jnp.bfloat16
MESH = pl.DeviceIdType.MESH

D_MODEL = 1024
N_META = 16
BLOCK = 128
LEAD = BLOCK - N_META
HEAD_DIM = 64
N_HEADS = 8
GROUP = 4
ATTN_W = 512
KV_W = 128
CONV_W = 512
CONV_K = 31
HALO = 32
D_FF = 2816
FF_CHUNK = D_FF // 2
FF_SUB = [slice(s, min(s + 256, FF_CHUNK)) for s in range(0, FF_CHUNK, 256)]
N_DEV = 8
EPS = 1e-5
NEG = -1e30
TM = 640
WGRAD_K_TILES = 5
CONV_ROWS = 64
VMEM_LIMIT = 56 * 1024 * 1024

ADAM_LR, ADAM_B1, ADAM_B2, ADAM_EPS, ADAM_WD, ADAM_STEP = 0.001, 0.9, 0.999, 1e-08, 0.01, 10

NT = (((1,), (1,)), ((), ()))
NN = (((1,), (0,)), ((), ()))
TN = (((0,), (0,)), ((), ()))


def _dot(a, b, dims):
    return lax.dot_general(a, b, dims, preferred_element_type=F32)


def _sigmoid(x):
    return 1.0 / (1.0 + jnp.exp(-x))


def _pcall(body, *, name, out_shape, grid=None, in_specs=None, out_specs=None, scratch_shapes=(),
           semantics=None, **kw):
    params = dict(vmem_limit_bytes=VMEM_LIMIT)
    if semantics is not None:
        params["dimension_semantics"] = semantics
    extra = {}
    if grid is not None:
        extra["grid"] = grid
    if in_specs is not None:
        extra["in_specs"] = in_specs
    if out_specs is not None:
        extra["out_specs"] = out_specs
    return pl.pallas_call(body, name=name, out_shape=out_shape, scratch_shapes=list(scratch_shapes),
                          compiler_params=pltpu.CompilerParams(**params), **extra, **kw)


def _rows(tm, cols, off=0):
    return pl.BlockSpec((tm, cols), lambda i, *_: (i + off, 0))


def _full(shape):
    nd = len(shape)
    return pl.BlockSpec(shape, lambda *_: (0,) * nd)


def _rms_stats(x):
    return lax.rsqrt(jnp.mean(x * x, axis=-1, keepdims=True) + EPS)


def _rms_bwd(dy, x, r, g):
    t = dy * g
    dx = r * (t - x * (r * r) * jnp.mean(t * x, axis=-1, keepdims=True))
    dg = jnp.sum(dy * x * r, axis=0, keepdims=True)
    return dx, dg


def _inproj_fwd(x, meta, g1, w_in_t):
    R = x.shape[0] + BLOCK
    nt = R // TM
    assert nt >= 2

    def body(x_hbm, meta_ref, g_ref, w_ref, h0_ref, q_ref, kv_ref, ca_ref, cg_ref, x_s, sems):
        i = pl.program_id(0)
        slot = i % 2

        def x_copy(step, slot, first):
            if first:
                return pltpu.make_async_copy(x_hbm.at[pl.ds(0, TM - BLOCK)],
                                             x_s.at[slot, pl.ds(BLOCK, TM - BLOCK)], sems.at[slot])
            return pltpu.make_async_copy(
                x_hbm.at[pl.ds(pl.multiple_of(step * TM - BLOCK, BLOCK), TM)], x_s.at[slot], sems.at[slot])

        @pl.when(i == 0)
        def _():
            x_copy(0, 0, True).start()
            x_s[0, 0:LEAD, :] = jnp.zeros((LEAD, D_MODEL), F32)
            x_s[0, LEAD:BLOCK, :] = meta_ref[...]

        pl.when(i + 1 < nt)(lambda: x_copy(i + 1, 1 - slot, False).start())
        pl.when(i == 0)(lambda: x_copy(0, 0, True).wait())
        pl.when(i > 0)(lambda: x_copy(i, slot, False).wait())
        h = x_s[slot]
        h0_ref[...] = h
        hn = (h * _rms_stats(h) * g_ref[...]).astype(BF16)
        q_ref[...] = _dot(hn, w_ref[0:512, :], NT).astype(BF16)
        kv_ref[...] = _dot(hn, w_ref[512:768, :], NT).astype(BF16)
        ca_ref[...] = _dot(hn, w_ref[768:1280, :], NT)
        cg_ref[...] = _dot(hn, w_ref[1280:1792, :], NT)

    return _pcall(
        body, name="inproj_fwd", grid=(nt,),
        in_specs=[pl.BlockSpec(memory_space=pl.ANY), _full((N_META, D_MODEL)), _full((1, D_MODEL)),
                  _full((1792, D_MODEL))],
        out_specs=[_rows(TM, D_MODEL), _rows(TM, 512), _rows(TM, 256), _rows(TM, 512), _rows(TM, 512)],
        out_shape=[jax.ShapeDtypeStruct((R, D_MODEL), F32),
                   jax.ShapeDtypeStruct((R, 512), BF16), jax.ShapeDtypeStruct((R, 256), BF16),
                   jax.ShapeDtypeStruct((R, 512), F32), jax.ShapeDtypeStruct((R, 512), F32)],
        scratch_shapes=[pltpu.VMEM((2, TM, D_MODEL), F32), pltpu.SemaphoreType.DMA((2,))],
        semantics=("arbitrary",),
    )(x, meta, g1, w_in_t)


GB = GROUP * BLOCK
ATTN_SCALE = 1.0 / math.sqrt(HEAD_DIM)


def _group_lanes(xt, g):
    return jnp.concatenate(
        [xt[HEAD_DIM * (GROUP * g + j):HEAD_DIM * (GROUP * g + j + 1), :] for j in range(GROUP)], axis=1)


def _head_lanes(ref, g):
    return jnp.concatenate([ref[GROUP * g + j:GROUP * g + j + 1, :] for j in range(GROUP)], axis=1)


def _head_rows(xs):
    return jnp.concatenate([x[:, BLOCK * j:BLOCK * (j + 1)] for x in xs for j in range(GROUP)], axis=0)


def _attn_tables(sink_ref, bias_s, sink_s):
    kk = lax.broadcasted_iota(jnp.int32, (BLOCK, BLOCK), 0)
    ii = lax.broadcasted_iota(jnp.int32, (BLOCK, BLOCK), 1)
    dist = jnp.where(kk <= ii, ii - kk, ii - kk + BLOCK).astype(F32)
    for h in range(N_HEADS):
        bias_s[:, BLOCK * h:BLOCK * (h + 1)] = dist * -(2.0 ** -(h + 1))
        sink_s[:, BLOCK * h:BLOCK * (h + 1)] = jnp.zeros((1, BLOCK), F32) + sink_ref[0, h]


def _attn_masks(b):
    kk = lax.broadcasted_iota(jnp.int32, (BLOCK, GB), 0)
    ii = lax.broadcasted_iota(jnp.int32, (BLOCK, GB), 1) & (BLOCK - 1)
    sel = kk <= ii
    pen = jnp.where(sel, jnp.where(b >= 1, 0.0, NEG), jnp.where(b >= 2, 0.0, NEG))
    mj = lax.broadcasted_iota(jnp.int32, (N_META, GB), 0)
    mi = lax.broadcasted_iota(jnp.int32, (N_META, GB), 1) & (BLOCK - 1)
    pen_m = jnp.where((mj + LEAD) <= (mi + b * BLOCK), 0.0, NEG)
    return sel, pen, pen_m


def _attn_scores(qt, kc, kp, km, sel, pen, pen_m, bias):
    s_b = jnp.where(sel, _dot(kc, qt, NN), _dot(kp, qt, NN)) + bias + pen
    s_m = _dot(km, qt, NN) + pen_m
    return s_b, s_m


def _attn_fwd(q, kv, sinks, shards):
    R = q.shape[0]
    nb = R // BLOCK
    ns = len(shards)
    forward_steps = [(w + 1) * (nb - 8) // (ns + 1) for w in range(ns)]

    def body(sink_ref, q_ref, kvc_ref, kvp_ref, kvm_ref, *refs):
        ag_ins, (o_ref, lse_ref), ag_outs = refs[:ns], refs[ns:ns + 2], refs[ns + 2:2 * ns + 2]
        send_sems, recv_sems, local_sems, bias_s, sink_s = refs[2 * ns + 2:]
        b = pl.program_id(0)
        ag_start, ag_forward, ag_finish = _two_level_allgather(
            ag_ins, ag_outs, _row_block(ag_outs, [s.shape[0] for s in shards]), send_sems, recv_sems, local_sems)
        pl.when(b == 0)(ag_start)
        for w, step in enumerate(forward_steps):
            pl.when(b == step)(functools.partial(ag_forward, w))
        pl.when(b == 0)(functools.partial(_attn_tables, sink_ref, bias_s, sink_s))
        sel, pen, pen_m = _attn_masks(b)
        q_t = (q_ref[...] * ATTN_SCALE).T
        kvc_t, kvp_t = kvc_ref[...].T, kvp_ref[...].T
        outs = []
        for g in range(N_HEADS // GROUP):
            ks, vs = slice(HEAD_DIM * g, HEAD_DIM * (g + 1)), slice(KV_W + HEAD_DIM * g, KV_W + HEAD_DIM * (g + 1))
            lanes = slice(GB * g, GB * (g + 1))
            s_b, s_m = _attn_scores(_group_lanes(q_t, g), kvc_ref[:, ks], kvp_ref[:, ks], kvm_ref[LEAD:BLOCK, ks],
                                    sel, pen, pen_m, bias_s[:, lanes])
            sink = sink_s[:, lanes]
            m = jnp.maximum(jnp.maximum(jnp.max(s_b, axis=0, keepdims=True),
                                        jnp.max(s_m, axis=0, keepdims=True)), sink)
            p_b = jnp.exp(s_b - m)
            p_m = jnp.exp(s_m - m)
            l = jnp.sum(p_b, axis=0, keepdims=True) + jnp.sum(p_m, axis=0, keepdims=True) + jnp.exp(sink - m)
            p_c = jnp.where(sel, p_b, 0.0).astype(BF16)
            p_p = jnp.where(sel, 0.0, p_b).astype(BF16)
            o_t = (_dot(kvc_t[vs, :], p_c, NN) + _dot(kvp_t[vs, :], p_p, NN)
                   + _dot(kvm_ref[LEAD:BLOCK, vs], p_m.astype(BF16), TN))
            outs.append(o_t / l)
            lse = m + jnp.log(l)
            for j in range(GROUP):
                lse_ref[GROUP * g + j:GROUP * g + j + 1, :] = lse[:, BLOCK * j:BLOCK * (j + 1)]
        o_ref[...] = _head_rows(outs).T
        pl.when(b == nb - 1)(ag_finish)

    res = _pcall(
        body, name="attn_fwd", grid=(nb,),
        in_specs=[pl.BlockSpec(memory_space=pltpu.SMEM),
                  _rows(BLOCK, 512), _rows(BLOCK, 256),
                  pl.BlockSpec((BLOCK, 256), lambda b: (jnp.maximum(b - 1, 0), 0)),
                  _full((BLOCK, 256))] + [ANY] * ns,
        out_specs=[_rows(BLOCK, 512), pl.BlockSpec((N_HEADS, BLOCK), lambda b: (0, b))] + [ANY] * ns,
        out_shape=[jax.ShapeDtypeStruct((R, 512), F32), jax.ShapeDtypeStruct((N_HEADS, R), F32)]
        + [jax.ShapeDtypeStruct((N_DEV * s.shape[0], s.shape[1]), s.dtype) for s in shards],
        scratch_shapes=_sem_pair(7 * ns) + [pltpu.SemaphoreType.DMA((ns,)),
                                            pltpu.VMEM((BLOCK, N_HEADS * BLOCK), F32),
                                            pltpu.VMEM((1, N_HEADS * BLOCK), F32)],
        semantics=("arbitrary",),
    )(sinks, q, kv, kv, kv, *shards)
    return res[0], res[1], res[2:]


def _ln_silu(y, lg, lb):
    mu = jnp.mean(y, axis=-1, keepdims=True)
    xc = y - mu
    rstd = lax.rsqrt(jnp.mean(xc * xc, axis=-1, keepdims=True) + EPS)
    xhat = xc * rstd
    yn = xhat * lg + lb
    return yn, xhat, rstd


PHASE_ROWS = HALO + TM - 8


def _phase_scratch():
    return pltpu.VMEM((7, PHASE_ROWS, CONV_W), F32)


def _phase_copies(src_s, ph_s):
    for b in range(1, 8):
        ph_s[b - 1] = src_s[pl.ds(b, PHASE_ROWS), :]


def _shifted(src_s, ph_s, start, rows):
    a8, b = (start // 8) * 8, start % 8
    if b == 0:
        return src_s[pl.ds(a8, rows), :]
    return ph_s[b - 1, pl.ds(a8, rows), :]


def _conv_fwd(ca, cg, conv_w, conv_b, ln_g, ln_b):
    R = ca.shape[0]
    nt = R // TM
    hpt = TM // HALO

    def body(ca_ref, cg_ref, cah_ref, cgh_ref, w_ref, b_ref, lg_ref, lb_ref, oc_ref, y_ref, u_s, uph_s):
        i = pl.program_id(0)
        u_s[HALO:HALO + TM, :] = ca_ref[...] * _sigmoid(cg_ref[...])
        u_s[0:HALO, :] = jnp.where(i > 0, cah_ref[...] * _sigmoid(cgh_ref[...]), 0.0)
        _phase_copies(u_s, uph_s)
        for rc in range(TM // CONV_ROWS):
            base = rc * CONV_ROWS + HALO - (CONV_K - 1)
            acc = jnp.zeros((CONV_ROWS, CONV_W), F32) + b_ref[...]
            for k in range(CONV_K):
                acc = acc + _shifted(u_s, uph_s, base + k, CONV_ROWS) * w_ref[k:k + 1, :]
            rows = slice(rc * CONV_ROWS, (rc + 1) * CONV_ROWS)
            y_ref[rows, :] = acc
            yn, _, _ = _ln_silu(acc, lg_ref[...], lb_ref[...])
            oc_ref[rows, :] = yn * _sigmoid(yn)

    prev_halo = pl.BlockSpec((HALO, CONV_W), lambda i: (jnp.maximum(i * hpt - 1, 0), 0))
    return _pcall(
        body, name="conv_fwd", grid=(nt,),
        in_specs=[_rows(TM, CONV_W), _rows(TM, CONV_W), prev_halo, prev_halo,
                  _full((CONV_K, CONV_W)), _full((1, CONV_W)), _full((1, CONV_W)), _full((1, CONV_W))],
        out_specs=[_rows(TM, CONV_W), _rows(TM, CONV_W)],
        out_shape=[jax.ShapeDtypeStruct((R, CONV_W), F32), jax.ShapeDtypeStruct((R, CONV_W), F32)],
        scratch_shapes=[pltpu.VMEM((HALO + TM, CONV_W), F32), _phase_scratch()],
        semantics=("parallel",),
    )(ca, cg, ca, cg, conv_w, conv_b, ln_g, ln_b)


def _outproj_fwd(h0, o_attn, o_conv, ga, gc, w_out):
    R = h0.shape[0]

    def body(h_ref, oa_ref, oc_ref, ga_ref, gc_ref, w_ref, h1_ref):
        oa, oc = oa_ref[...], oc_ref[...]
        ma = (oa * _rms_stats(oa) * ga_ref[...]).astype(BF16)
        mc = (oc * _rms_stats(oc) * gc_ref[...]).astype(BF16)
        h1_ref[...] = h_ref[...] + _dot(ma, w_ref[0:512, :], NN) + _dot(mc, w_ref[512:1024, :], NN)

    return _pcall(
        body, name="outproj_fwd", grid=(R // TM,),
        in_specs=[_rows(TM, D_MODEL), _rows(TM, 512), _rows(TM, 512), _full((1, 512)), _full((1, 512)),
                  _full((D_MODEL, D_MODEL))],
        out_specs=_rows(TM, D_MODEL),
        out_shape=jax.ShapeDtypeStruct((R, D_MODEL), F32),
        semantics=("parallel",),
    )(h0, o_attn, o_conv, ga, gc, w_out)


def _target_copy(tgt_hbm, tgt_s, sem, i, first):
    if first:
        return pltpu.make_async_copy(tgt_hbm.at[pl.ds(0, TM - BLOCK)], tgt_s.at[pl.ds(BLOCK, TM - BLOCK)], sem)
    return pltpu.make_async_copy(tgt_hbm.at[pl.ds(i * TM - BLOCK, TM)], tgt_s, sem)


def _ffn_fwd(h1, g2, wg_t, wu_t, wd, gf, target):
    R = h1.shape[0]
    nt, nj = R // TM, D_FF // FF_CHUNK

    def body(h1_ref, g2_ref, wg_ref, wu_ref, wd_ref, gf_ref, tgt_hbm,
             gate_ref, up_ref, dh2_ref, loss_ref, dgf_ref, hn_s, acc_s, tgt_s, sem, act_s):
        i, j = pl.program_id(0), pl.program_id(1)

        @pl.when((i == 0) & (j == 0))
        def _():
            loss_ref[...] = jnp.zeros_like(loss_ref)
            dgf_ref[...] = jnp.zeros_like(dgf_ref)

        @pl.when(j == 0)
        def _():
            h1 = h1_ref[...]
            hn_s[...] = (h1 * _rms_stats(h1) * g2_ref[...]).astype(BF16)

            @pl.when(i == 0)
            def _():
                tgt_s[0:BLOCK, :] = jnp.zeros((BLOCK, D_MODEL), F32)
                _target_copy(tgt_hbm, tgt_s, sem, i, True).start()

            @pl.when(i > 0)
            def _():
                _target_copy(tgt_hbm, tgt_s, sem, i, False).start()

        hn = hn_s[...]
        for cs in FF_SUB:
            gate = _dot(hn, wg_ref[cs, :], NT)
            up = _dot(hn, wu_ref[cs, :], NT)
            gate_ref[:, cs] = gate.astype(BF16)
            up_ref[:, cs] = up.astype(BF16)
            act_s[:, cs] = (gate * _sigmoid(gate) * up).astype(BF16)
        part = _dot(act_s[...], wd_ref[...], NN)

        @pl.when(j == 0)
        def _():
            acc_s[...] = part

        @pl.when(j == nj - 1)
        def _():
            @pl.when(i == 0)
            def _():
                _target_copy(tgt_hbm, tgt_s, sem, i, True).wait()

            @pl.when(i > 0)
            def _():
                _target_copy(tgt_hbm, tgt_s, sem, i, False).wait()

            h2 = h1_ref[...] + acc_s[...] + part
            rf = _rms_stats(h2)
            gf = gf_ref[...]
            row = lax.broadcasted_iota(jnp.int32, (TM, 1), 0) + i * TM
            err = jnp.where(row >= BLOCK, h2 * rf * gf - tgt_s[...], 0.0)
            dy = err * (1.0 / D_MODEL)
            dh2, dgf = _rms_bwd(dy, h2, rf, gf)
            dh2_ref[...] = dh2
            loss_ref[...] += (0.5 / D_MODEL) * jnp.sum(err * err)
            dgf_ref[...] += dgf

    wspec = pl.BlockSpec((FF_CHUNK, D_MODEL), lambda i, j: (j, 0))
    aspec = pl.BlockSpec((TM, FF_CHUNK), lambda i, j: (i, j))
    return _pcall(
        body, name="ffn_fwd", grid=(nt, nj),
        in_specs=[_rows(TM, D_MODEL), _full((1, D_MODEL)), wspec, wspec, wspec, _full((1, D_MODEL)),
                  pl.BlockSpec(memory_space=pl.ANY)],
        out_specs=[aspec, aspec, _rows(TM, D_MODEL),
                   _full((8, 128)), _full((1, D_MODEL))],
        out_shape=[jax.ShapeDtypeStruct((R, D_FF), BF16), jax.ShapeDtypeStruct((R, D_FF), BF16),
                   jax.ShapeDtypeStruct((R, D_MODEL), F32),
                   jax.ShapeDtypeStruct((8, 128), F32), jax.ShapeDtypeStruct((1, D_MODEL), F32)],
        scratch_shapes=[pltpu.VMEM((TM, D_MODEL), BF16), pltpu.VMEM((TM, D_MODEL), F32),
                        pltpu.VMEM((TM, D_MODEL), F32), pltpu.SemaphoreType.DMA, pltpu.VMEM((TM, FF_CHUNK), BF16)],
        semantics=("arbitrary", "arbitrary"),
    )(h1, g2, wg_t, wu_t, wd, gf, target)


def _ffn_bwd(dh2, h1, g2, gate, up, wg_t, wu_t, wd):
    R = h1.shape[0]
    nt, nj = R // TM, D_FF // FF_CHUNK

    wspec = pl.BlockSpec((FF_CHUNK, D_MODEL), lambda i, j: (j, 0))
    aspec = pl.BlockSpec((TM, FF_CHUNK), lambda i, j: (i, j))
    act_shape = jax.ShapeDtypeStruct((R, D_FF), BF16)

    def act_body(dh2_ref, gate_ref, up_ref, wd_ref, dgate_ref, dup_ref, act_ref, dhb_s):
        @pl.when(pl.program_id(1) == 0)
        def _():
            dhb_s[...] = dh2_ref[...].astype(BF16)

        dhb = dhb_s[...]
        for cs in FF_SUB:
            dact = _dot(dhb, wd_ref[cs, :], NT)
            gate = gate_ref[:, cs].astype(F32)
            up = up_ref[:, cs].astype(F32)
            sig = _sigmoid(gate)
            silu = gate * sig
            dgate_ref[:, cs] = (dact * up * (sig * (1.0 + gate * (1.0 - sig)))).astype(BF16)
            dup_ref[:, cs] = (dact * silu).astype(BF16)
            act_ref[:, cs] = (silu * up).astype(BF16)

    dgate, dup, act = _pcall(
        act_body, name="ffn_bwd_act", grid=(nt, nj),
        in_specs=[_rows(TM, D_MODEL), aspec, aspec, wspec],
        out_specs=[aspec, aspec, aspec], out_shape=[act_shape, act_shape, act_shape],
        scratch_shapes=[pltpu.VMEM((TM, D_MODEL), BF16)],
        semantics=("parallel", "arbitrary"),
    )(dh2, gate, up, wd)

    def in_body(dh2_ref, h1_ref, g2_ref, dgate_ref, dup_ref, wg_ref, wu_ref, hn_ref, dh1_ref, dg2_ref, acc_s):
        i, j = pl.program_id(0), pl.program_id(1)

        @pl.when((i == 0) & (j == 0))
        def _():
            dg2_ref[...] = jnp.zeros_like(dg2_ref)

        part = _dot(dgate_ref[...], wg_ref[...], NN) + _dot(dup_ref[...], wu_ref[...], NN)

        @pl.when(j == 0)
        def _():
            acc_s[...] = part

        @pl.when(j == nj - 1)
        def _():
            h1 = h1_ref[...]
            r = _rms_stats(h1)
            g2 = g2_ref[...]
            hn_ref[...] = (h1 * r * g2).astype(BF16)
            dx, dg = _rms_bwd(acc_s[...] + part, h1, r, g2)
            dh1_ref[...] = dh2_ref[...] + dx
            dg2_ref[...] += dg

    hn2, dh1, dg2 = _pcall(
        in_body, name="ffn_bwd_in", grid=(nt, nj),
        in_specs=[_rows(TM, D_MODEL), _rows(TM, D_MODEL), _full((1, D_MODEL)), aspec, aspec, wspec, wspec],
        out_specs=[_rows(TM, D_MODEL), _rows(TM, D_MODEL), _full((1, D_MODEL))],
        out_shape=[jax.ShapeDtypeStruct((R, D_MODEL), BF16), jax.ShapeDtypeStruct((R, D_MODEL), F32),
                   jax.ShapeDtypeStruct((1, D_MODEL), F32)],
        scratch_shapes=[pltpu.VMEM((TM, D_MODEL), F32)],
        semantics=("arbitrary", "arbitrary"),
    )(dh2, h1, g2, dgate, dup, wg_t, wu_t)
    return dgate, dup, act, hn2, dh1, dg2


def _wgrad(a, b, tm, name, partials=()):
    K, M = a.shape
    N = b.shape[1]
    tk = K // WGRAD_K_TILES if K % (WGRAD_K_TILES * BLOCK) == 0 else TM
    nm, nk, npart = M // tm, K // tk, len(partials)

    def body(a_ref, b_ref, *refs):
        p_ins, o_ref, p_outs, sems = refs[:npart], refs[npart], refs[npart + 1:2 * npart + 1], refs[2 * npart + 1:]
        step = pl.program_id(0) * nk + pl.program_id(1)
        exchange = functools.partial(_chip_copies, p_ins, p_outs, *sems)
        if npart:
            _hosted(step, nm * nk, exchange)

        @pl.when(pl.program_id(1) == 0)
        def _():
            o_ref[...] = jnp.zeros_like(o_ref)

        o_ref[...] += _dot(a_ref[...], b_ref[...].astype(BF16), TN)
        if npart:
            _hosted_wait(step, nm * nk, exchange)

    res = _pcall(
        body, name=name, grid=(nm, nk),
        in_specs=[pl.BlockSpec((tk, tm), lambda m, k: (k, m)), pl.BlockSpec((tk, N), lambda m, k: (k, 0))]
        + [ANY] * npart,
        out_specs=[pl.BlockSpec((tm, N), lambda m, k: (m, 0))] + [ANY] * npart,
        out_shape=[jax.ShapeDtypeStruct((M, N), F32)] + _chip_shapes(partials),
        scratch_shapes=_sem_pair(3 * npart) if npart else [],
        semantics=("arbitrary", "arbitrary"),
    )(a, b, *partials)
    return (res[0], res[1:]) if npart else res[0]


def _outproj_bwd(dh1, o_attn, o_conv, ga, gc, w_out, grads):
    R = dh1.shape[0]
    nt, ng = R // TM, len(grads)

    def body(dh1_ref, oa_ref, oc_ref, ga_ref, gc_ref, w_ref, *refs):
        g_ins, (doa_ref, doc_ref, mixed_ref, dga_ref, dgc_ref) = refs[:ng], refs[ng:ng + 5]
        g_outs, (send_sems, recv_sems) = refs[ng + 5:2 * ng + 5], refs[2 * ng + 5:]
        exchange = functools.partial(_sibling_copies, g_ins, g_outs, send_sems, recv_sems)
        _hosted(pl.program_id(0), nt, exchange)

        @pl.when(pl.program_id(0) == 0)
        def _():
            dga_ref[...] = jnp.zeros_like(dga_ref)
            dgc_ref[...] = jnp.zeros_like(dgc_ref)

        dm = _dot(dh1_ref[...].astype(BF16), w_ref[...], NT)
        oa, oc = oa_ref[...], oc_ref[...]
        ra, rc = _rms_stats(oa), _rms_stats(oc)
        mixed_ref[:, 0:512] = (oa * ra * ga_ref[...]).astype(BF16)
        mixed_ref[:, 512:1024] = (oc * rc * gc_ref[...]).astype(BF16)
        doa, dga = _rms_bwd(dm[:, 0:512], oa, ra, ga_ref[...])
        doc, dgc = _rms_bwd(dm[:, 512:1024], oc, rc, gc_ref[...])
        doa_ref[...] = doa
        doc_ref[...] = doc
        dga_ref[...] += dga
        dgc_ref[...] += dgc
        _hosted_wait(pl.program_id(0), nt, exchange)

    res = _pcall(
        body, name="outproj_bwd", grid=(nt,),
        in_specs=[_rows(TM, D_MODEL), _rows(TM, 512), _rows(TM, 512), _full((1, 512)), _full((1, 512)),
                  _full((D_MODEL, D_MODEL))] + [ANY] * ng,
        out_specs=[_rows(TM, 512), _rows(TM, 512), _rows(TM, D_MODEL), _full((1, 512)), _full((1, 512))]
        + [ANY] * ng,
        out_shape=[jax.ShapeDtypeStruct((R, 512), F32), jax.ShapeDtypeStruct((R, 512), F32),
                   jax.ShapeDtypeStruct((R, D_MODEL), BF16),
                   jax.ShapeDtypeStruct((1, 512), F32), jax.ShapeDtypeStruct((1, 512), F32)]
        + _sibling_shapes(grads),
        scratch_shapes=_sem_pair(ng),
        semantics=("arbitrary",),
    )(dh1, o_attn, o_conv, ga, gc, w_out, *grads)
    return res[:5], res[5:]


def _conv_bwd(do_conv, y, ca, cg, conv_w, ln_g, ln_b, partials):
    R = ca.shape[0]
    nt = R // TM
    hpt = TM // HALO
    npart = len(partials)

    def body(do_ref, doh_ref, y_ref, yh_ref, ca_ref, cg_ref, cah_ref, cgh_ref, w_ref, lg_ref, lb_ref, *refs):
        p_ins, (dca_ref, dcg_ref, dw_ref, db_ref, dlg_ref, dlb_ref) = refs[:npart], refs[npart:npart + 6]
        p_outs, (send_sems, recv_sems, u_s, dy_s, uph_s, dyph_s) = refs[npart + 6:2 * npart + 6], refs[2 * npart + 6:]
        i = pl.program_id(0)
        exchange = functools.partial(_chip_copies, p_ins, p_outs, send_sems, recv_sems)
        _hosted(i, nt, exchange)

        @pl.when(i == 0)
        def _():
            dw_ref[...] = jnp.zeros_like(dw_ref)
            db_ref[...] = jnp.zeros_like(db_ref)
            dlg_ref[...] = jnp.zeros_like(dlg_ref)
            dlb_ref[...] = jnp.zeros_like(dlb_ref)

        lg, lb = lg_ref[...], lb_ref[...]

        def ln_bwd(yv, dov):
            yn, xhat, rstd = _ln_silu(yv, lg, lb)
            sig = _sigmoid(yn)
            dyn = dov * (sig * (1.0 + yn * (1.0 - sig)))
            dxh = dyn * lg
            dyv = rstd * (dxh - jnp.mean(dxh, axis=-1, keepdims=True)
                          - xhat * jnp.mean(dxh * xhat, axis=-1, keepdims=True))
            return dyv, dyn, xhat

        dyv, dyn, xhat = ln_bwd(y_ref[...], do_ref[...])
        dy_s[0:TM, :] = dyv
        dlg_ref[...] += jnp.sum(dyn * xhat, axis=0, keepdims=True)
        dlb_ref[...] += jnp.sum(dyn, axis=0, keepdims=True)
        db_ref[...] += jnp.sum(dyv, axis=0, keepdims=True)
        dyh, _, _ = ln_bwd(yh_ref[...], doh_ref[...])
        dy_s[TM:TM + HALO, :] = jnp.where(i < nt - 1, dyh, 0.0)
        u_s[HALO:HALO + TM, :] = ca_ref[...] * _sigmoid(cg_ref[...])
        u_s[0:HALO, :] = jnp.where(i > 0, cah_ref[...] * _sigmoid(cgh_ref[...]), 0.0)
        _phase_copies(dy_s, dyph_s)
        _phase_copies(u_s, uph_s)

        for rc in range(TM // CONV_ROWS):
            acc = jnp.zeros((CONV_ROWS, CONV_W), F32)
            for k in range(CONV_K):
                acc = acc + _shifted(dy_s, dyph_s, rc * CONV_ROWS + CONV_K - 1 - k, CONV_ROWS) * w_ref[k:k + 1, :]
            rows = slice(rc * CONV_ROWS, (rc + 1) * CONV_ROWS)
            sg = _sigmoid(cg_ref[rows, :])
            dca_ref[rows, :] = (acc * sg).astype(BF16)
            dcg_ref[rows, :] = (acc * ca_ref[rows, :] * sg * (1.0 - sg)).astype(BF16)

        for k in range(CONV_K):
            prod = _shifted(u_s, uph_s, HALO - (CONV_K - 1) + k, TM) * dy_s[0:TM, :]
            dw_ref[k:k + 1, :] += jnp.sum(prod, axis=0, keepdims=True)
        _hosted_wait(i, nt, exchange)

    prev_halo = pl.BlockSpec((HALO, CONV_W), lambda i: (jnp.maximum(i * hpt - 1, 0), 0))
    next_halo = pl.BlockSpec((HALO, CONV_W), lambda i: (jnp.minimum((i + 1) * hpt, nt * hpt - 1), 0))
    vec = jax.ShapeDtypeStruct((1, CONV_W), F32)
    res = _pcall(
        body, name="conv_bwd", grid=(nt,),
        in_specs=[_rows(TM, CONV_W), next_halo, _rows(TM, CONV_W), next_halo,
                  _rows(TM, CONV_W), _rows(TM, CONV_W), prev_halo, prev_halo,
                  _full((CONV_K, CONV_W)), _full((1, CONV_W)), _full((1, CONV_W))] + [ANY] * npart,
        out_specs=[_rows(TM, CONV_W), _rows(TM, CONV_W), _full((32, CONV_W)),
                   _full((1, CONV_W)), _full((1, CONV_W)), _full((1, CONV_W))] + [ANY] * npart,
        out_shape=[jax.ShapeDtypeStruct((R, CONV_W), BF16), jax.ShapeDtypeStruct((R, CONV_W), BF16),
                   jax.ShapeDtypeStruct((32, CONV_W), F32), vec, vec, vec] + _chip_shapes(partials),
        scratch_shapes=_sem_pair(3 * npart)
        + [pltpu.VMEM((HALO + TM, CONV_W), F32), pltpu.VMEM((TM + HALO, CONV_W), F32),
           _phase_scratch(), _phase_scratch()],
        semantics=("arbitrary",),
    )(do_conv, do_conv, y, y, ca, cg, ca, cg, conv_w, ln_g, ln_b, *partials)
    return res[:6], res[6:]


def _attn_bwd(q, kv, sinks, o, lse, do, grads, partials):
    R = q.shape[0]
    nb = R // BLOCK
    ng, npart = len(grads), len(partials)
    nx = ng + npart

    def body(sink_ref, q_ref, kvc_ref, kvp_ref, kvm_ref, o_ref, lse_ref, do_ref, *refs):
        x_ins, (dq_ref, dkv_ref, dkvm_ref, dsink_ref) = refs[:nx], refs[nx:nx + 4]
        x_outs = refs[nx + 4:2 * nx + 4]
        g_send, g_recv, p_send, p_recv, carry_s, cur_s, prev_s, bias_s, sink_s, delta_s = refs[2 * nx + 4:]
        b = pl.program_id(0)

        def exchange():
            return (_sibling_copies(x_ins[:ng], x_outs[:ng], g_send, g_recv)
                    + _chip_copies(x_ins[ng:], x_outs[ng:], p_send, p_recv))

        _hosted(b, nb + 1, exchange)

        @pl.when(b == 0)
        def _():
            dkvm_ref[...] = jnp.zeros_like(dkvm_ref)
            carry_s[...] = jnp.zeros_like(carry_s)
            for h in range(N_HEADS):
                dsink_ref[0, h] = 0.0
            _attn_tables(sink_ref, bias_s, sink_s)

        @pl.when(b < nb)
        def _():
            sel, pen, pen_m = _attn_masks(b)
            q_t = (q_ref[...] * ATTN_SCALE).T
            do_t = do_ref[...].astype(BF16).T
            kvc_t, kvp_t = kvc_ref[...].T, kvp_ref[...].T
            prod = do_ref[...] * o_ref[...]
            hi = prod.astype(BF16)
            lo = (prod - hi.astype(F32)).astype(BF16)
            head_of = lax.broadcasted_iota(jnp.int32, (N_HEADS, ATTN_W), 1) // HEAD_DIM
            ind = (head_of == lax.broadcasted_iota(jnp.int32, (N_HEADS, ATTN_W), 0)).astype(BF16)
            delta_s[...] = _dot(ind, hi, NT) + _dot(ind, lo, NT)
            dqs = []
            for g in range(N_HEADS // GROUP):
                ks, vs = slice(HEAD_DIM * g, HEAD_DIM * (g + 1)), slice(KV_W + HEAD_DIM * g, KV_W + HEAD_DIM * (g + 1))
                lanes = slice(GB * g, GB * (g + 1))
                qg, dog = _group_lanes(q_t, g), _group_lanes(do_t, g)
                kc, kp, km = kvc_ref[:, ks], kvp_ref[:, ks], kvm_ref[LEAD:BLOCK, ks]
                vc, vp, vm = kvc_ref[:, vs], kvp_ref[:, vs], kvm_ref[LEAD:BLOCK, vs]
                s_b, s_m = _attn_scores(qg, kc, kp, km, sel, pen, pen_m, bias_s[:, lanes])
                lse, delta = _head_lanes(lse_ref, g), _head_lanes(delta_s, g)
                p_b = jnp.exp(s_b - lse)
                p_m = jnp.exp(s_m - lse)
                dp_b = jnp.where(sel, _dot(vc, dog, NN), _dot(vp, dog, NN))
                ds_b = p_b * (dp_b - delta)
                ds_m = (p_m * (_dot(vm, dog, NN) - delta)).astype(BF16)
                dsk = jnp.exp(sink_s[:, lanes] - lse) * delta
                for j in range(GROUP):
                    dsink_ref[0, GROUP * g + j] += -jnp.sum(dsk[:, BLOCK * j:BLOCK * (j + 1)])
                ds_c = jnp.where(sel, ds_b, 0.0).astype(BF16)
                ds_p = jnp.where(sel, 0.0, ds_b).astype(BF16)
                p_c = jnp.where(sel, p_b, 0.0).astype(BF16)
                p_p = jnp.where(sel, 0.0, p_b).astype(BF16)
                dqs.append((_dot(kvc_t[ks, :], ds_c, NN) + _dot(kvp_t[ks, :], ds_p, NN)
                            + _dot(km, ds_m, TN)) * ATTN_SCALE)
                cur_s[:, ks] = _dot(ds_c, qg, NT)
                cur_s[:, vs] = _dot(p_c, dog, NT)
                prev_s[:, ks] = _dot(ds_p, qg, NT)
                prev_s[:, vs] = _dot(p_p, dog, NT)
                dkvm_ref[:, ks] += _dot(ds_m, qg, NT)
                dkvm_ref[:, vs] += _dot(p_m.astype(BF16), dog, NT)
            dq_ref[...] = _head_rows(dqs).astype(BF16).T
            dkv_ref[...] = (carry_s[...] + prev_s[...]).astype(BF16)
            carry_s[...] = cur_s[...]

        @pl.when(b == nb)
        def _():
            dkv_ref[...] = carry_s[...].astype(BF16)

        _hosted_wait(b, nb + 1, exchange)

    def at(off):
        return lambda b: (jnp.clip(b + off, 0, nb - 1), 0)

    blk = lambda cols, off=0: pl.BlockSpec((BLOCK, cols), at(off))
    res = _pcall(
        body, name="attn_bwd", grid=(nb + 1,),
        in_specs=[pl.BlockSpec(memory_space=pltpu.SMEM), blk(512), blk(256), blk(256, -1), _full((BLOCK, 256)),
                  blk(512), pl.BlockSpec((N_HEADS, BLOCK), lambda b: (0, jnp.minimum(b, nb - 1))), blk(512)]
        + [ANY] * nx,
        out_specs=[blk(512), blk(256, -1), _full((N_META, 256)), pl.BlockSpec(memory_space=pltpu.SMEM)]
        + [ANY] * nx,
        out_shape=[jax.ShapeDtypeStruct((R, 512), BF16), jax.ShapeDtypeStruct((R, 256), BF16),
                   jax.ShapeDtypeStruct((N_META, 256), F32), jax.ShapeDtypeStruct((1, N_HEADS), F32)]
        + _sibling_shapes(grads) + _chip_shapes(partials),
        scratch_shapes=_sem_pair(ng) + _sem_pair(3 * npart) + [pltpu.VMEM((BLOCK, 256), F32)] * 3
        + [pltpu.VMEM((BLOCK, N_HEADS * BLOCK), F32), pltpu.VMEM((1, N_HEADS * BLOCK), F32),
           pltpu.VMEM((N_HEADS, BLOCK), F32)],
        semantics=("arbitrary",),
    )(sinks, q, kv, kv, kv, o, lse, do, *grads, *partials)
    return res[:4], res[4:4 + ng], res[4 + ng:]


def _inproj_bwd(dh1, h0, g1, dq, dkv, dkvm, dca, dcg, w_in_t):
    R = h0.shape[0]
    nt = R // TM
    assert nt >= 2

    def body(dh1_ref, h0_ref, g_ref, dq_ref, dkv_ref, dkvm_ref, dca_ref, dcg_ref, w_ref,
             gx_hbm, dmeta_ref, dproj_ref, hn_ref, dg_ref, dx_s, gx_sems):
        i = pl.program_id(0)

        def gx_copy(step, slot, first):
            if first:
                return pltpu.make_async_copy(dx_s.at[slot, pl.ds(BLOCK, TM - BLOCK)],
                                             gx_hbm.at[pl.ds(0, TM - BLOCK)], gx_sems.at[slot])
            return pltpu.make_async_copy(
                dx_s.at[slot], gx_hbm.at[pl.ds(pl.multiple_of(step * TM - BLOCK, BLOCK), TM)], gx_sems.at[slot])

        @pl.when(i == 0)
        def _():
            dg_ref[...] = jnp.zeros_like(dg_ref)

        dproj_ref[:, 0:512] = dq_ref[...]
        dproj_ref[:, 512:768] = dkv_ref[...]
        dproj_ref[:, 768:1280] = dca_ref[...]
        dproj_ref[:, 1280:1792] = dcg_ref[...]

        @pl.when(i == 0)
        def _():
            dproj_ref[LEAD:BLOCK, 512:768] = dkvm_ref[...].astype(BF16)

        dhn = _dot(dproj_ref[...], w_ref[...], NN)
        h = h0_ref[...]
        r = _rms_stats(h)
        g = g_ref[...]
        hn_ref[...] = (h * r * g).astype(BF16)
        dx, dg = _rms_bwd(dhn, h, r, g)
        dg_ref[...] += dg
        slot = i % 2
        pl.when(i == 2)(lambda: gx_copy(0, 0, True).wait())
        pl.when(i > 2)(lambda: gx_copy(i - 2, slot, False).wait())
        dx_s[slot] = dh1_ref[...] + dx

        @pl.when(i == 0)
        def _():
            dmeta_ref[...] = dx_s[0, LEAD:BLOCK, :]
            gx_copy(0, 0, True).start()

        pl.when(i > 0)(lambda: gx_copy(i, slot, False).start())

        @pl.when(i == nt - 1)
        def _():
            gx_copy(nt - 2, (nt - 2) % 2, nt == 2).wait()
            gx_copy(nt - 1, (nt - 1) % 2, False).wait()

    return _pcall(
        body, name="inproj_bwd", grid=(nt,),
        in_specs=[_rows(TM, D_MODEL), _rows(TM, D_MODEL), _full((1, D_MODEL)), _rows(TM, 512), _rows(TM, 256),
                  _full((N_META, 256)), _rows(TM, 512), _rows(TM, 512), _full((1792, D_MODEL))],
        out_specs=[ANY, _full((N_META, D_MODEL)), _rows(TM, 1792), _rows(TM, D_MODEL), _full((1, D_MODEL))],
        out_shape=[jax.ShapeDtypeStruct((R - BLOCK, D_MODEL), F32), jax.ShapeDtypeStruct((N_META, D_MODEL), F32),
                   jax.ShapeDtypeStruct((R, 1792), BF16),
                   jax.ShapeDtypeStruct((R, D_MODEL), BF16), jax.ShapeDtypeStruct((1, D_MODEL), F32)],
        scratch_shapes=[pltpu.VMEM((2, TM, D_MODEL), F32), pltpu.SemaphoreType.DMA((2,))],
        semantics=("arbitrary",),
    )(dh1, h0, g1, dq, dkv, dkvm, dca, dcg, w_in_t)


ANY = pl.BlockSpec(memory_space=pl.ANY)


def _position():
    return lax.axis_index("x"), lax.axis_index("y"), lax.axis_index("c")


def _device_number(p):
    return 4 * p[0] + 2 * p[1] + p[2]


def _two_level_allgather(ins, outs, block, send_sems, recv_sems, local_sems, sem_base=0):
    n = len(ins)
    x, y, c = _position()
    me, sibling = (x, y, c), (x, y, 1 - c)
    chips = [(1 - x, y), (x, 1 - y), (1 - x, 1 - y)]

    def copy(w, k, origin, to, src=None):
        return pltpu.make_async_remote_copy(
            src_ref=block(w, origin) if src is None else src, dst_ref=block(w, origin),
            send_sem=send_sems.at[sem_base + 7 * w + k], recv_sem=recv_sems.at[sem_base + 7 * w + k],
            device_id=to, device_id_type=MESH)

    def mine(w):
        return pltpu.make_async_copy(ins[w], block(w, me), local_sems.at[w])

    def own(w):
        return [copy(w, 0, me, sibling, src=ins[w])] + [
            copy(w, 1 + j, me, (*chip, c), src=ins[w]) for j, chip in enumerate(chips)]

    def passed(w):
        return [copy(w, 4 + j, (*chip, c), sibling) for j, chip in enumerate(chips)]

    def start():
        for w in range(n):
            mine(w).start()
        for w in range(n):
            for cp in own(w):
                cp.start()

    def forward(w):
        fw = passed(w)
        for j, chip in enumerate(chips):
            copy(w, 1 + j, (*chip, c), me).wait_recv()
            fw[j].start()

    def finish():
        for w in range(n):
            copy(w, 0, sibling, me).wait_recv()
            for j, chip in enumerate(chips):
                copy(w, 4 + j, (*chip, 1 - c), me).wait_recv()
        for w in range(n):
            for cp in own(w) + passed(w):
                cp.wait_send()
            mine(w).wait()

    return start, forward, finish


def _blocking_allgather(ins, outs, block, send_sems, recv_sems, local_sems, sem_base=0):
    start, forward, finish = _two_level_allgather(ins, outs, block, send_sems, recv_sems, local_sems, sem_base)
    start()
    for w in range(len(ins)):
        forward(w)
    finish()


def _row_block(outs, rows):
    def block(w, p):
        return outs[w].at[pl.ds(pl.multiple_of(_device_number(p) * rows[w], 16), rows[w])]
    return block


def _sibling_copies(ins, outs, send_sems, recv_sems):
    x, y, c = _position()
    return [pltpu.make_async_remote_copy(
        src_ref=ins[w].at[:, 1 - c], dst_ref=outs[w], send_sem=send_sems.at[w], recv_sem=recv_sems.at[w],
        device_id=(x, y, 1 - c), device_id_type=MESH) for w in range(len(ins))]


def _chip_copies(ins, outs, send_sems, recv_sems):
    x, y, c = _position()
    chips = [(1 - x, y), (x, 1 - y), (1 - x, 1 - y)]
    return [pltpu.make_async_remote_copy(
        src_ref=ins[w].at[2 * chip[0] + chip[1]], dst_ref=outs[w].at[k],
        send_sem=send_sems.at[3 * w + k], recv_sem=recv_sems.at[3 * w + k],
        device_id=(*chip, c), device_id_type=MESH) for w in range(len(ins)) for k, chip in enumerate(chips)]


def _hosted(step, n_steps, make_copies):
    @pl.when(step == 0)
    def _():
        for cp in make_copies():
            cp.start()


def _hosted_wait(step, n_steps, make_copies):
    @pl.when(step == n_steps - 1)
    def _():
        for cp in make_copies():
            cp.wait()


def _sem_pair(n):
    return [pltpu.SemaphoreType.DMA((n,)), pltpu.SemaphoreType.DMA((n,))]


def _allgather_params(shards, small):
    arrays = list(shards) + list(small)
    n, ns = len(arrays), len(shards)

    def body(*refs):
        ins, outs = refs[:n], refs[n:2 * n]
        send_sems, recv_sems, local_sems = refs[2 * n:]

        rows = _row_block(outs, [a.shape[0] for a in arrays])

        def block(w, p):
            return rows(w, p) if w < ns else outs[w].at[_device_number(p)]

        _blocking_allgather(ins, outs, block, send_sems, recv_sems, local_sems)

    out_shape = [jax.ShapeDtypeStruct((N_DEV * a.shape[0], a.shape[1]), a.dtype) for a in shards]
    out_shape += [jax.ShapeDtypeStruct((N_DEV,) + a.shape, a.dtype) for a in small]
    return _pcall(
        body, name="allgather_params", in_specs=[ANY] * n, out_specs=[ANY] * n, out_shape=out_shape,
        scratch_shapes=[pltpu.SemaphoreType.DMA((7 * n,)), pltpu.SemaphoreType.DMA((7 * n,)),
                        pltpu.SemaphoreType.DMA((n,))],
    )(*arrays)


def _reduce_siblings(grads):
    n = len(grads)

    def body(*refs):
        ins, outs = refs[:n], refs[n:2 * n]
        send_sems, recv_sems = refs[2 * n:]
        copies = _sibling_copies(ins, outs, send_sems, recv_sems)
        for cp in copies:
            cp.start()
        for cp in copies:
            cp.wait()

    return _pcall(
        body, name="reduce_siblings", in_specs=[ANY] * n, out_specs=[ANY] * n,
        out_shape=_sibling_shapes(grads), scratch_shapes=_sem_pair(n),
    )(*grads)


def _sibling_shapes(grads):
    return [jax.ShapeDtypeStruct((4,) + g.shape[2:], F32) for g in grads]


def _chip_shapes(partials):
    return [jax.ShapeDtypeStruct((3,) + p.shape[1:], F32) for p in partials]


def _add_sibling(grad, received, core, name):
    _, _, r, cols = grad.shape

    def body(core_ref, g_ref, r_ref, o_ref):
        o_ref[...] = g_ref[...] + r_ref[...]

    return pl.pallas_call(
        body, name=name,
        grid_spec=pltpu.PrefetchScalarGridSpec(
            num_scalar_prefetch=1, grid=(4,),
            in_specs=[pl.BlockSpec((None, None, r, cols), lambda s, core_ref: (s, core_ref[0], 0, 0)),
                      pl.BlockSpec((None, r, cols), lambda s, core_ref: (s, 0, 0))],
            out_specs=pl.BlockSpec((None, r, cols), lambda s, core_ref: (s, 0, 0))),
        out_shape=jax.ShapeDtypeStruct((4, r, cols), F32),
        compiler_params=pltpu.CompilerParams(vmem_limit_bytes=VMEM_LIMIT),
    )(core, grad, received)


def _reduce_chips(partials, small):
    n, ns = len(partials), len(small)

    def body(*refs):
        p_ins, s_ins = refs[:n], refs[n:n + ns]
        p_outs, s_outs = refs[n + ns:2 * n + ns], refs[2 * n + ns:2 * (n + ns)]
        send_sems, recv_sems, local_sems = refs[2 * (n + ns):]
        copies = _chip_copies(p_ins, p_outs, send_sems, recv_sems)
        for cp in copies:
            cp.start()
        _blocking_allgather(s_ins, s_outs, lambda w, p: s_outs[w].at[_device_number(p)],
                            send_sems, recv_sems, local_sems, sem_base=3 * n)
        for cp in copies:
            cp.wait()

    out_shape = _chip_shapes(partials)
    out_shape += [jax.ShapeDtypeStruct((N_DEV,) + a.shape, a.dtype) for a in small]
    nsem = 3 * n + 7 * ns
    return _pcall(
        body, name="reduce_chips", in_specs=[ANY] * (n + ns), out_specs=[ANY] * (n + ns), out_shape=out_shape,
        scratch_shapes=[pltpu.SemaphoreType.DMA((nsem,)), pltpu.SemaphoreType.DMA((nsem,)),
                        pltpu.SemaphoreType.DMA((ns,))],
    )(*partials, *small)


def _adam(g, w, m, v):
    m = ADAM_B1 * m + (1.0 - ADAM_B1) * g
    v = ADAM_B2 * v + (1.0 - ADAM_B2) * (g * g)
    m_hat = m / (1.0 - ADAM_B1 ** ADAM_STEP)
    v_hat = v / (1.0 - ADAM_B2 ** ADAM_STEP)
    delta = -ADAM_LR * (m_hat / (jnp.sqrt(v_hat) + ADAM_EPS) + ADAM_WD * w)
    return delta, m, v


def _adamw(partial, received, slot, w, m, v, name):
    _, r, cols = partial.shape

    def body(slot_ref, p_ref, r_ref, w_ref, m_ref, v_ref, g_ref, d_ref, nm_ref, nv_ref):
        g = p_ref[...] + r_ref[0] + r_ref[1] + r_ref[2]
        g_ref[...] = g
        d_ref[...], nm_ref[...], nv_ref[...] = _adam(g, w_ref[...], m_ref[...], v_ref[...])

    whole = pl.BlockSpec((r, cols), lambda i, slot_ref: (0, 0))
    out = jax.ShapeDtypeStruct((r, cols), F32)
    return pl.pallas_call(
        body, name=name,
        grid_spec=pltpu.PrefetchScalarGridSpec(
            num_scalar_prefetch=1, grid=(1,),
            in_specs=[pl.BlockSpec((None, r, cols), lambda i, slot_ref: (slot_ref[0], 0, 0)),
                      pl.BlockSpec((3, r, cols), lambda i, slot_ref: (0, 0, 0)), whole, whole, whole],
            out_specs=[whole, whole, whole, whole]),
        out_shape=[out, out, out, out],
        compiler_params=pltpu.CompilerParams(vmem_limit_bytes=VMEM_LIMIT),
    )(slot, partial, received, w, m, v)


def _adamw_small(dev, ga, gb, gc, params):
    names = ["meta", "attn_norm", "sinks", "conv_w", "conv_b", "ln_g", "ln_b", "attn_out", "conv_out",
             "ffn_norm", "final_norm"]
    flat = [a for p in params for a in p]
    n_in = len(flat)

    def body(dev_ref, ga_ref, gb_ref, gc_ref, *refs):
        ins, outs = refs[:n_in], refs[n_in:n_in + 4 * len(names)]
        loss_ref, sb, sc = refs[n_in + 4 * len(names):]
        a = ga_ref[0]
        sb[...] = gb_ref[0]
        sc[...] = gc_ref[0]
        for d in range(1, N_DEV):
            a = a + ga_ref[d]
            sb[...] += gb_ref[d]
            sc[...] += gc_ref[d]
        dev = dev_ref[0]
        grads = {
            "attn_norm": a[0:1, :], "ffn_norm": a[1:2, :], "final_norm": a[2:3, :],
            "conv_b": a[3:4, 0:512], "ln_g": a[3:4, 512:1024], "ln_b": a[4:5, 0:512],
            "attn_out": a[4:5, 512:1024], "conv_out": a[5:6, 0:512], "sinks": a[5:6, 512:512 + N_HEADS],
            "meta": sb[pl.ds(pl.multiple_of(dev * N_META, N_META), N_META), :],
            "conv_w": sc[pl.ds(pl.multiple_of(dev * 32, 32), 32), :][0:CONV_K, :],
        }
        for idx, nm in enumerate(names):
            w_ref, m_ref, v_ref = ins[3 * idx:3 * idx + 3]
            g = grads[nm]
            delta, m, v = _adam(g, w_ref[...], m_ref[...], v_ref[...])
            o = outs[4 * idx:4 * idx + 4]
            o[0][...], o[1][...], o[2][...], o[3][...] = g, delta, m, v
        loss_ref[...] = a[6:7, 0:1]

    vm = pl.BlockSpec(memory_space=pltpu.VMEM)
    out_shape = [jax.ShapeDtypeStruct(p[0].shape, F32) for p in params for _ in range(4)]
    out_shape.append(jax.ShapeDtypeStruct((1, 1), F32))
    res = pl.pallas_call(
        body, name="adamw_small",
        grid_spec=pltpu.PrefetchScalarGridSpec(
            num_scalar_prefetch=1, grid=(1,),
            in_specs=[pl.BlockSpec(ga.shape, lambda i, d: (0, 0, 0)), pl.BlockSpec(gb.shape, lambda i, d: (0, 0, 0)),
                      pl.BlockSpec(gc.shape, lambda i, d: (0, 0, 0))]
            + [pl.BlockSpec(a.shape, lambda i, d: (0, 0)) for a in flat],
            out_specs=[pl.BlockSpec(s.shape, lambda i, d: (0, 0)) for s in out_shape],
            scratch_shapes=[pltpu.VMEM(gb.shape[1:], F32), pltpu.VMEM(gc.shape[1:], F32)]),
        out_shape=out_shape,
        compiler_params=pltpu.CompilerParams(vmem_limit_bytes=VMEM_LIMIT),
    )(dev, ga, gb, gc, *flat)
    return [res[4 * i:4 * i + 4] for i in range(len(names))], res[-1]


def kernel(x, meta_tokens, attn_norm_g, w_in, attn_sinks, conv_w, conv_b, conv_ln_g, conv_ln_b, attn_out_g, conv_out_g, w_out, ffn_norm_g, w_gate, w_up, w_down, final_norm_g, loss_target, m_meta_tokens, m_attn_norm_g, m_w_in, m_attn_sinks, m_conv_w, m_conv_b, m_conv_ln_g, m_conv_ln_b, m_attn_out_g, m_conv_out_g, m_w_out, m_ffn_norm_g, m_w_gate, m_w_up, m_w_down, m_final_norm_g, v_meta_tokens, v_attn_norm_g, v_w_in, v_attn_sinks, v_conv_w, v_conv_b, v_conv_ln_g, v_conv_ln_b, v_attn_out_g, v_conv_out_g, v_w_out, v_ffn_norm_g, v_w_gate, v_w_up, v_w_down, v_final_norm_g):
    xi, yi, ci = _position()
    dev = jnp.reshape(_device_number((xi, yi, ci)), (1,)).astype(jnp.int32)
    core = jnp.reshape(ci, (1,)).astype(jnp.int32)
    slot = jnp.reshape(2 * xi + yi, (1,)).astype(jnp.int32)

    w_in_t, meta_st, convw_st = _allgather_params([w_in[0].T.astype(BF16)], [meta_tokens, conv_w[0]])
    later = [w_out[0].astype(BF16), w_gate[0].T.astype(BF16), w_up[0].T.astype(BF16), w_down[0].astype(BF16)]
    meta_full = jnp.transpose(meta_st, (1, 0, 2)).reshape(N_META, D_MODEL)
    convw_full = jnp.transpose(convw_st, (1, 0, 2)).reshape(CONV_K, CONV_W)

    final_g = final_norm_g.reshape(1, D_MODEL)

    h0, q, kv, ca, cg = _inproj_fwd(x[0], meta_full, attn_norm_g, w_in_t)
    o_attn, lse, (w_out_b, wg_t, wu_t, wd_b) = _attn_fwd(q, kv, attn_sinks, later)
    o_conv, y_conv = _conv_fwd(ca, cg, convw_full, conv_b, conv_ln_g, conv_ln_b)
    h1 = _outproj_fwd(h0, o_attn, o_conv, attn_out_g, conv_out_g, w_out_b)
    gate, up, dh2, loss_sum, dg_final = _ffn_fwd(h1, ffn_norm_g, wg_t, wu_t, wd_b, final_g, loss_target[0])

    def blocks(g):
        return g.reshape(4, 2, g.shape[0] // N_DEV, D_MODEL)

    def add_siblings(grads, received, tags):
        return [_add_sibling(g, r, core, "add_sibling_" + t) for g, r, t in zip(grads, received, tags)]

    dgate, dup, act, hn2, dh1, dg_ffn = _ffn_bwd(dh2, h1, ffn_norm_g, gate, up, wg_t, wu_t, wd_b)
    ffn_grads = [blocks(_wgrad(dgate, hn2, FF_CHUNK, "wgrad_gate")), blocks(_wgrad(dup, hn2, FF_CHUNK, "wgrad_up")),
                 blocks(_wgrad(act, dh2, FF_CHUNK, "wgrad_down"))]
    (do_attn, do_conv, mixed, dg_ao, dg_co), ffn_sib = _outproj_bwd(
        dh1, o_attn, o_conv, attn_out_g, conv_out_g, w_out_b, ffn_grads)
    ffn_sums = add_siblings(ffn_grads, ffn_sib, ("gate", "up", "down"))
    out_grads = [blocks(_wgrad(mixed, dh1, D_MODEL, "wgrad_out"))]
    (dca, dcg, dconvw, dconvb, dln_g, dln_b), gate_up_chips = _conv_bwd(
        do_conv, y_conv, ca, cg, convw_full, conv_ln_g, conv_ln_b, ffn_sums[:2])
    (dq, dkv, dkvm, dsinks), out_sib, down_chips = _attn_bwd(
        q, kv, attn_sinks, o_attn, lse, do_attn, out_grads, ffn_sums[2:])
    ffn_chips = list(gate_up_chips) + list(down_chips)
    out_sums = add_siblings(out_grads, out_sib, ("out",))
    grad_x, dmeta, dproj, hn1, dg_attn = _inproj_bwd(dh1, h0, attn_norm_g, dq, dkv, dkvm, dca, dcg, w_in_t)
    dwi_t, out_chips = _wgrad(dproj, hn1, 1792, "wgrad_in", out_sums)
    in_grads = [blocks(dwi_t)]
    in_sums = add_siblings(in_grads, _reduce_siblings(in_grads), ("in",))
    small_a = jnp.concatenate([
        dg_attn, dg_ffn, dg_final, jnp.concatenate([dconvb, dln_g], axis=1), jnp.concatenate([dln_b, dg_ao], axis=1),
        jnp.concatenate([dg_co, dsinks, jnp.zeros((1, 512 - N_HEADS), F32)], axis=1),
        jnp.concatenate([loss_sum[0:1, :], jnp.zeros((1, D_MODEL - 128), F32)], axis=1),
        jnp.zeros((1, D_MODEL), F32)], axis=0)
    small_b = jnp.transpose(dmeta.reshape(N_META, N_DEV, 128), (1, 0, 2)).reshape(N_DEV * N_META, 128)
    small_c = jnp.transpose(dconvw.reshape(32, N_DEV, 64), (1, 0, 2)).reshape(N_DEV * 32, 64)
    in_chips, ga, gb, gc = _reduce_chips(in_sums, [small_a, small_b, small_c])
    tags = ("in", "out", "gate", "up", "down")
    chip_sums = in_sums + out_sums + ffn_sums
    from_chips = [in_chips] + list(out_chips) + list(ffn_chips)

    big = [(True, w_in, m_w_in, v_w_in), (False, w_out, m_w_out, v_w_out), (True, w_gate, m_w_gate, v_w_gate),
           (True, w_up, m_w_up, v_w_up), (False, w_down, m_w_down, v_w_down)]
    big_out = {}
    for t, p, r, (transposed, w, m, v) in zip(tags, chip_sums, from_chips, big):
        rows = (lambda a: jnp.transpose(a[0])) if transposed else (lambda a: a[0])
        back = (lambda a: jnp.transpose(a)[None]) if transposed else (lambda a: a[None])
        big_out[t] = [back(a) for a in _adamw(p, r, slot, rows(w), rows(m), rows(v), "adamw_" + t)]

    small_params = [
        (meta_tokens, m_meta_tokens, v_meta_tokens), (attn_norm_g, m_attn_norm_g, v_attn_norm_g),
        (attn_sinks, m_attn_sinks, v_attn_sinks), (conv_w[0], m_conv_w[0], v_conv_w[0]),
        (conv_b, m_conv_b, v_conv_b), (conv_ln_g, m_conv_ln_g, v_conv_ln_g), (conv_ln_b, m_conv_ln_b, v_conv_ln_b),
        (attn_out_g, m_attn_out_g, v_attn_out_g), (conv_out_g, m_conv_out_g, v_conv_out_g),
        (ffn_norm_g, m_ffn_norm_g, v_ffn_norm_g),
        (final_g, m_final_norm_g.reshape(1, D_MODEL), v_final_norm_g.reshape(1, D_MODEL))]
    sm, loss = _adamw_small(dev, ga, gb, gc, small_params)
    sm[3] = [a[None] for a in sm[3]]
    sm[10] = [a.reshape(D_MODEL) for a in sm[10]]

    per_param = [sm[0], sm[1], big_out["in"], sm[2], sm[3], sm[4], sm[5], sm[6], sm[7], sm[8], big_out["out"],
                 sm[9], big_out["gate"], big_out["up"], big_out["down"], sm[10]]
    loss = loss.reshape(())
    outs = [loss, grad_x[None]]
    for kind in range(4):
        outs += [p[kind] for p in per_param]
    return tuple(outs)
```

```python
import functools
import math

import jax
import jax.numpy as jnp
from jax import lax
from jax.experimental import pallas as pl
from jax.experimental.pallas import tpu as pltpu

F32, BF16 = jnp.float32, jnp.bfloat16
MESH = pl.DeviceIdType.MESH

D_MODEL = 1024
N_META = 16
BLOCK = 128
LEAD = BLOCK - N_META
HEAD_DIM = 64
N_HEADS = 8
GROUP = 4
ATTN_W = 512
KV_W = 128
CONV_W = 512
CONV_K = 31
HALO = 32
D_FF = 2816
FF_CHUNK = D_FF // 2
FF_SUB = [slice(s, min(s + 256, FF_CHUNK)) for s in range(0, FF_CHUNK, 256)]
N_DEV = 8
EPS = 1e-5
NEG = -1e30
TM = 640
AG_FORWARD_AT = 85
WGRAD_K_TILES = 5
CONV_ROWS = 64
VMEM_LIMIT = 56 * 1024 * 1024

ADAM_LR, ADAM_B1, ADAM_B2, ADAM_EPS, ADAM_WD, ADAM_STEP = 0.001, 0.9, 0.999, 1e-08, 0.01, 10

NT = (((1,), (1,)), ((), ()))
NN = (((1,), (0,)), ((), ()))
TN = (((0,), (0,)), ((), ()))


def _dot(a, b, dims):
    return lax.dot_general(a, b, dims, preferred_element_type=F32)


def _sigmoid(x):
    return 1.0 / (1.0 + jnp.exp(-x))


def _pcall(body, *, name, out_shape, grid=None, in_specs=None, out_specs=None, scratch_shapes=(),
           semantics=None, **kw):
    params = dict(vmem_limit_bytes=VMEM_LIMIT)
    if semantics is not None:
        params["dimension_semantics"] = semantics
    extra = {}
    if grid is not None:
        extra["grid"] = grid
    if in_specs is not None:
        extra["in_specs"] = in_specs
    if out_specs is not None:
        extra["out_specs"] = out_specs
    return pl.pallas_call(body, name=name, out_shape=out_shape, scratch_shapes=list(scratch_shapes),
                          compiler_params=pltpu.CompilerParams(**params), **extra, **kw)


def _rows(tm, cols, off=0):
    return pl.BlockSpec((tm, cols), lambda i, *_: (i + off, 0))


def _full(shape):
    nd = len(shape)
    return pl.BlockSpec(shape, lambda *_: (0,) * nd)


def _rms_stats(x):
    return lax.rsqrt(jnp.mean(x * x, axis=-1, keepdims=True) + EPS)


def _rms_bwd(dy, x, r, g):
    t = dy * g
    dx = r * (t - x * (r * r) * jnp.mean(t * x, axis=-1, keepdims=True))
    dg = jnp.sum(dy * x * r, axis=0, keepdims=True)
    return dx, dg


def _inproj_fwd(x, meta, g1, w_in_t, shards):
    R = x.shape[0] + BLOCK
    nt = R // TM
    ns = len(shards)
    assert nt >= 2

    def body(x_hbm, meta_ref, g_ref, w_ref, *refs):
        ag_ins, (h0_ref, q_ref, kv_ref, ca_ref, cg_ref), ag_outs = refs[:ns], refs[ns:ns + 5], refs[ns + 5:2 * ns + 5]
        ag_sems, (x_s, sems) = refs[2 * ns + 5:2 * ns + 8], refs[2 * ns + 8:]
        i = pl.program_id(0)
        slot = i % 2
        ag_finish = _carried_allgather(i, nt, shards, ag_ins + ag_outs + ag_sems)

        def x_copy(step, slot, first):
            if first:
                return pltpu.make_async_copy(x_hbm.at[pl.ds(0, TM - BLOCK)],
                                             x_s.at[slot, pl.ds(BLOCK, TM - BLOCK)], sems.at[slot])
            return pltpu.make_async_copy(
                x_hbm.at[pl.ds(pl.multiple_of(step * TM - BLOCK, BLOCK), TM)], x_s.at[slot], sems.at[slot])

        @pl.when(i == 0)
        def _():
            x_copy(0, 0, True).start()
            x_s[0, 0:LEAD, :] = jnp.zeros((LEAD, D_MODEL), F32)
            x_s[0, LEAD:BLOCK, :] = meta_ref[...]

        pl.when(i + 1 < nt)(lambda: x_copy(i + 1, 1 - slot, False).start())
        pl.when(i == 0)(lambda: x_copy(0, 0, True).wait())
        pl.when(i > 0)(lambda: x_copy(i, slot, False).wait())
        h = x_s[slot]
        h0_ref[...] = h
        hn = (h * _rms_stats(h) * g_ref[...]).astype(BF16)
        q_ref[...] = _dot(hn, w_ref[0:512, :], NT).astype(BF16)
        kv_ref[...] = _dot(hn, w_ref[512:768, :], NT).astype(BF16)
        ca_ref[...] = _dot(hn, w_ref[768:1280, :], NT)
        cg_ref[...] = _dot(hn, w_ref[1280:1792, :], NT)
        ag_finish()

    anywhere = pl.BlockSpec(memory_space=pl.ANY)
    res = _pcall(
        body, name="inproj_fwd", grid=(nt,),
        in_specs=[anywhere, _full((N_META, D_MODEL)), _full((1, D_MODEL)), _full((1792, D_MODEL))] + [anywhere] * ns,
        out_specs=[_rows(TM, D_MODEL), _rows(TM, 512), _rows(TM, 256), _rows(TM, 512), _rows(TM, 512)]
        + [anywhere] * ns,
        out_shape=[jax.ShapeDtypeStruct((R, D_MODEL), F32),
                   jax.ShapeDtypeStruct((R, 512), BF16), jax.ShapeDtypeStruct((R, 256), BF16),
                   jax.ShapeDtypeStruct((R, 512), F32), jax.ShapeDtypeStruct((R, 512), F32)]
        + _gathered_shapes(shards),
        scratch_shapes=_allgather_sems(ns) + [pltpu.VMEM((2, TM, D_MODEL), F32), pltpu.SemaphoreType.DMA((2,))],
        semantics=("arbitrary",),
    )(x, meta, g1, w_in_t, *shards)
    return res[:5], res[5:]


GB = GROUP * BLOCK
ATTN_SCALE = 1.0 / math.sqrt(HEAD_DIM)


def _group_lanes(xt, g):
    return jnp.concatenate(
        [xt[HEAD_DIM * (GROUP * g + j):HEAD_DIM * (GROUP * g + j + 1), :] for j in range(GROUP)], axis=1)


def _head_lanes(ref, g):
    return jnp.concatenate([ref[GROUP * g + j:GROUP * g + j + 1, :] for j in range(GROUP)], axis=1)


def _head_rows(xs):
    return jnp.concatenate([x[:, BLOCK * j:BLOCK * (j + 1)] for x in xs for j in range(GROUP)], axis=0)


def _attn_tables(sink_ref, bias_s, sink_s):
    kk = lax.broadcasted_iota(jnp.int32, (BLOCK, BLOCK), 0)
    ii = lax.broadcasted_iota(jnp.int32, (BLOCK, BLOCK), 1)
    dist = jnp.where(kk <= ii, ii - kk, ii - kk + BLOCK).astype(F32)
    for h in range(N_HEADS):
        bias_s[:, BLOCK * h:BLOCK * (h + 1)] = dist * -(2.0 ** -(h + 1))
        sink_s[:, BLOCK * h:BLOCK * (h + 1)] = jnp.zeros((1, BLOCK), F32) + sink_ref[0, h]


def _attn_masks(b):
    kk = lax.broadcasted_iota(jnp.int32, (BLOCK, GB), 0)
    ii = lax.broadcasted_iota(jnp.int32, (BLOCK, GB), 1) & (BLOCK - 1)
    sel = kk <= ii
    pen = jnp.where(sel, jnp.where(b >= 1, 0.0, NEG), jnp.where(b >= 2, 0.0, NEG))
    mj = lax.broadcasted_iota(jnp.int32, (N_META, GB), 0)
    mi = lax.broadcasted_iota(jnp.int32, (N_META, GB), 1) & (BLOCK - 1)
    pen_m = jnp.where((mj + LEAD) <= (mi + b * BLOCK), 0.0, NEG)
    return sel, pen, pen_m


def _attn_scores(qt, kc, kp, km, sel, pen, pen_m, bias):
    s_b = jnp.where(sel, _dot(kc, qt, NN), _dot(kp, qt, NN)) + bias + pen
    s_m = _dot(km, qt, NN) + pen_m
    return s_b, s_m


def _attn_fwd(q, kv, sinks, shards):
    R = q.shape[0]
    nb = R // BLOCK
    ns = len(shards)

    def body(sink_ref, q_ref, kvc_ref, kvp_ref, kvm_ref, *refs):
        ag_ins, (o_ref, lse_ref), ag_outs = refs[:ns], refs[ns:ns + 2], refs[ns + 2:2 * ns + 2]
        ag_sems, (bias_s, sink_s) = refs[2 * ns + 2:2 * ns + 5], refs[2 * ns + 5:]
        b = pl.program_id(0)
        ag_finish = _carried_allgather(b, nb, shards, ag_ins + ag_outs + ag_sems)
        pl.when(b == 0)(functools.partial(_attn_tables, sink_ref, bias_s, sink_s))
        sel, pen, pen_m = _attn_masks(b)
        q_t = (q_ref[...] * ATTN_SCALE).T
        kvc_t, kvp_t = kvc_ref[...].T, kvp_ref[...].T
        outs = []
        for g in range(N_HEADS // GROUP):
            ks, vs = slice(HEAD_DIM * g, HEAD_DIM * (g + 1)), slice(KV_W + HEAD_DIM * g, KV_W + HEAD_DIM * (g + 1))
            lanes = slice(GB * g, GB * (g + 1))
            s_b, s_m = _attn_scores(_group_lanes(q_t, g), kvc_ref[:, ks], kvp_ref[:, ks], kvm_ref[LEAD:BLOCK, ks],
                                    sel, pen, pen_m, bias_s[:, lanes])
            sink = sink_s[:, lanes]
            m = jnp.maximum(jnp.maximum(jnp.max(s_b, axis=0, keepdims=True),
                                        jnp.max(s_m, axis=0, keepdims=True)), sink)
            p_b = jnp.exp(s_b - m)
            p_m = jnp.exp(s_m - m)
            l = jnp.sum(p_b, axis=0, keepdims=True) + jnp.sum(p_m, axis=0, keepdims=True) + jnp.exp(sink - m)
            p_c = jnp.where(sel, p_b, 0.0).astype(BF16)
            p_p = jnp.where(sel, 0.0, p_b).astype(BF16)
            o_t = (_dot(kvc_t[vs, :], p_c, NN) + _dot(kvp_t[vs, :], p_p, NN)
                   + _dot(kvm_ref[LEAD:BLOCK, vs], p_m.astype(BF16), TN))
            outs.append(o_t / l)
            lse = m + jnp.log(l)
            for j in range(GROUP):
                lse_ref[GROUP * g + j:GROUP * g + j + 1, :] = lse[:, BLOCK * j:BLOCK * (j + 1)]
        o_ref[...] = _head_rows(outs).T
        ag_finish()

    res = _pcall(
        body, name="attn_fwd", grid=(nb,),
        in_specs=[pl.BlockSpec(memory_space=pltpu.SMEM),
                  _rows(BLOCK, 512), _rows(BLOCK, 256),
                  pl.BlockSpec((BLOCK, 256), lambda b: (jnp.maximum(b - 1, 0), 0)),
                  _full((BLOCK, 256))] + [ANY] * ns,
        out_specs=[_rows(BLOCK, 512), pl.BlockSpec((N_HEADS, BLOCK), lambda b: (0, b))] + [ANY] * ns,
        out_shape=[jax.ShapeDtypeStruct((R, 512), F32), jax.ShapeDtypeStruct((N_HEADS, R), F32)]
        + _gathered_shapes(shards),
        scratch_shapes=_allgather_sems(ns) + [pltpu.VMEM((BLOCK, N_HEADS * BLOCK), F32),
                                              pltpu.VMEM((1, N_HEADS * BLOCK), F32)],
        semantics=("arbitrary",),
    )(sinks, q, kv, kv, kv, *shards)
    return res[0], res[1], res[2:]


def _ln_silu(y, lg, lb):
    mu = jnp.mean(y, axis=-1, keepdims=True)
    xc = y - mu
    rstd = lax.rsqrt(jnp.mean(xc * xc, axis=-1, keepdims=True) + EPS)
    xhat = xc * rstd
    yn = xhat * lg + lb
    return yn, xhat, rstd


PHASE_ROWS = HALO + TM - 8


def _phase_scratch():
    return pltpu.VMEM((7, PHASE_ROWS, CONV_W), F32)


def _phase_copies(src_s, ph_s):
    for b in range(1, 8):
        ph_s[b - 1] = src_s[pl.ds(b, PHASE_ROWS), :]


def _shifted(src_s, ph_s, start, rows):
    a8, b = (start // 8) * 8, start % 8
    if b == 0:
        return src_s[pl.ds(a8, rows), :]
    return ph_s[b - 1, pl.ds(a8, rows), :]


def _conv_fwd(ca, cg, conv_w, conv_b, ln_g, ln_b, shards):
    R = ca.shape[0]
    nt = R // TM
    hpt = TM // HALO
    ns = len(shards)

    def body(ca_ref, cg_ref, cah_ref, cgh_ref, w_ref, b_ref, lg_ref, lb_ref, *refs):
        ag_ins, (oc_ref, y_ref), ag_outs = refs[:ns], refs[ns:ns + 2], refs[ns + 2:2 * ns + 2]
        ag_sems, (u_s, uph_s) = refs[2 * ns + 2:2 * ns + 5], refs[2 * ns + 5:]
        i = pl.program_id(0)
        ag_finish = _carried_allgather(i, nt, shards, ag_ins + ag_outs + ag_sems)
        u_s[HALO:HALO + TM, :] = ca_ref[...] * _sigmoid(cg_ref[...])
        u_s[0:HALO, :] = jnp.where(i > 0, cah_ref[...] * _sigmoid(cgh_ref[...]), 0.0)
        _phase_copies(u_s, uph_s)
        for rc in range(TM // CONV_ROWS):
            base = rc * CONV_ROWS + HALO - (CONV_K - 1)
            acc = jnp.zeros((CONV_ROWS, CONV_W), F32) + b_ref[...]
            for k in range(CONV_K):
                acc = acc + _shifted(u_s, uph_s, base + k, CONV_ROWS) * w_ref[k:k + 1, :]
            rows = slice(rc * CONV_ROWS, (rc + 1) * CONV_ROWS)
            y_ref[rows, :] = acc
            yn, _, _ = _ln_silu(acc, lg_ref[...], lb_ref[...])
            oc_ref[rows, :] = yn * _sigmoid(yn)
        ag_finish()

    prev_halo = pl.BlockSpec((HALO, CONV_W), lambda i: (jnp.maximum(i * hpt - 1, 0), 0))
    anywhere = pl.BlockSpec(memory_space=pl.ANY)
    res = _pcall(
        body, name="conv_fwd", grid=(nt,),
        in_specs=[_rows(TM, CONV_W), _rows(TM, CONV_W), prev_halo, prev_halo,
                  _full((CONV_K, CONV_W)), _full((1, CONV_W)), _full((1, CONV_W)), _full((1, CONV_W))]
        + [anywhere] * ns,
        out_specs=[_rows(TM, CONV_W), _rows(TM, CONV_W)] + [anywhere] * ns,
        out_shape=[jax.ShapeDtypeStruct((R, CONV_W), F32), jax.ShapeDtypeStruct((R, CONV_W), F32)]
        + _gathered_shapes(shards),
        scratch_shapes=_allgather_sems(ns) + [pltpu.VMEM((HALO + TM, CONV_W), F32), _phase_scratch()],
        semantics=("arbitrary",),
    )(ca, cg, ca, cg, conv_w, conv_b, ln_g, ln_b, *shards)
    return res[:2], res[2:]


def _outproj_fwd(h0, o_attn, o_conv, ga, gc, w_out):
    R = h0.shape[0]

    def body(h_ref, oa_ref, oc_ref, ga_ref, gc_ref, w_ref, h1_ref):
        oa, oc = oa_ref[...], oc_ref[...]
        ma = (oa * _rms_stats(oa) * ga_ref[...]).astype(BF16)
        mc = (oc * _rms_stats(oc) * gc_ref[...]).astype(BF16)
        h1_ref[...] = h_ref[...] + _dot(ma, w_ref[0:512, :], NN) + _dot(mc, w_ref[512:1024, :], NN)

    return _pcall(
        body, name="outproj_fwd", grid=(R // TM,),
        in_specs=[_rows(TM, D_MODEL), _rows(TM, 512), _rows(TM, 512), _full((1, 512)), _full((1, 512)),
                  _full((D_MODEL, D_MODEL))],
        out_specs=_rows(TM, D_MODEL),
        out_shape=jax.ShapeDtypeStruct((R, D_MODEL), F32),
        semantics=("parallel",),
    )(h0, o_attn, o_conv, ga, gc, w_out)


def _target_copy(tgt_hbm, tgt_s, sem, i, first):
    if first:
        return pltpu.make_async_copy(tgt_hbm.at[pl.ds(0, TM - BLOCK)], tgt_s.at[pl.ds(BLOCK, TM - BLOCK)], sem)
    return pltpu.make_async_copy(tgt_hbm.at[pl.ds(i * TM - BLOCK, TM)], tgt_s, sem)


def _ffn_fwd(h1, g2, wg_t, wu_t, wd, gf, target):
    R = h1.shape[0]
    nt, nj = R // TM, D_FF // FF_CHUNK

    def body(h1_ref, g2_ref, wg_ref, wu_ref, wd_ref, gf_ref, tgt_hbm,
             gate_ref, up_ref, dh2_ref, loss_ref, dgf_ref, hn_s, acc_s, tgt_s, sem, act_s):
        i, j = pl.program_id(0), pl.program_id(1)

        @pl.when((i == 0) & (j == 0))
        def _():
            loss_ref[...] = jnp.zeros_like(loss_ref)
            dgf_ref[...] = jnp.zeros_like(dgf_ref)

        @pl.when(j == 0)
        def _():
            h1 = h1_ref[...]
            hn_s[...] = (h1 * _rms_stats(h1) * g2_ref[...]).astype(BF16)

            @pl.when(i == 0)
            def _():
                tgt_s[0:BLOCK, :] = jnp.zeros((BLOCK, D_MODEL), F32)
                _target_copy(tgt_hbm, tgt_s, sem, i, True).start()

            @pl.when(i > 0)
            def _():
                _target_copy(tgt_hbm, tgt_s, sem, i, False).start()

        hn = hn_s[...]
        for cs in FF_SUB:
            gate = _dot(hn, wg_ref[cs, :], NT)
            up = _dot(hn, wu_ref[cs, :], NT)
            gate_ref[:, cs] = gate.astype(BF16)
            up_ref[:, cs] = up.astype(BF16)
            act_s[:, cs] = (gate * _sigmoid(gate) * up).astype(BF16)
        part = _dot(act_s[...], wd_ref[...], NN)

        @pl.when(j == 0)
        def _():
            acc_s[...] = part

        @pl.when(j == nj - 1)
        def _():
            @pl.when(i == 0)
            def _():
                _target_copy(tgt_hbm, tgt_s, sem, i, True).wait()

            @pl.when(i > 0)
            def _():
                _target_copy(tgt_hbm, tgt_s, sem, i, False).wait()

            h2 = h1_ref[...] + acc_s[...] + part
            rf = _rms_stats(h2)
            gf = gf_ref[...]
            row = lax.broadcasted_iota(jnp.int32, (TM, 1), 0) + i * TM
            err = jnp.where(row >= BLOCK, h2 * rf * gf - tgt_s[...], 0.0)
            dy = err * (1.0 / D_MODEL)
            dh2, dgf = _rms_bwd(dy, h2, rf, gf)
            dh2_ref[...] = dh2
            loss_ref[...] += (0.5 / D_MODEL) * jnp.sum(err * err)
            dgf_ref[...] += dgf

    wspec = pl.BlockSpec((FF_CHUNK, D_MODEL), lambda i, j: (j, 0))
    aspec = pl.BlockSpec((TM, FF_CHUNK), lambda i, j: (i, j))
    return _pcall(
        body, name="ffn_fwd", grid=(nt, nj),
        in_specs=[_rows(TM, D_MODEL), _full((1, D_MODEL)), wspec, wspec, wspec, _full((1, D_MODEL)),
                  pl.BlockSpec(memory_space=pl.ANY)],
        out_specs=[aspec, aspec, _rows(TM, D_MODEL),
                   _full((8, 128)), _full((1, D_MODEL))],
        out_shape=[jax.ShapeDtypeStruct((R, D_FF), BF16), jax.ShapeDtypeStruct((R, D_FF), BF16),
                   jax.ShapeDtypeStruct((R, D_MODEL), F32),
                   jax.ShapeDtypeStruct((8, 128), F32), jax.ShapeDtypeStruct((1, D_MODEL), F32)],
        scratch_shapes=[pltpu.VMEM((TM, D_MODEL), BF16), pltpu.VMEM((TM, D_MODEL), F32),
                        pltpu.VMEM((TM, D_MODEL), F32), pltpu.SemaphoreType.DMA, pltpu.VMEM((TM, FF_CHUNK), BF16)],
        semantics=("arbitrary", "arbitrary"),
    )(h1, g2, wg_t, wu_t, wd, gf, target)


def _ffn_bwd(dh2, h1, g2, gate, up, wg_t, wu_t, wd):
    R = h1.shape[0]
    nt, nj = R // TM, D_FF // FF_CHUNK

    wspec = pl.BlockSpec((FF_CHUNK, D_MODEL), lambda i, j: (j, 0))
    aspec = pl.BlockSpec((TM, FF_CHUNK), lambda i, j: (i, j))
    act_shape = jax.ShapeDtypeStruct((R, D_FF), BF16)

    def act_body(dh2_ref, gate_ref, up_ref, wd_ref, dgate_ref, dup_ref, act_ref, dhb_s):
        @pl.when(pl.program_id(1) == 0)
        def _():
            dhb_s[...] = dh2_ref[...].astype(BF16)

        dhb = dhb_s[...]
        for cs in FF_SUB:
            dact = _dot(dhb, wd_ref[cs, :], NT)
            gate = gate_ref[:, cs].astype(F32)
            up = up_ref[:, cs].astype(F32)
            sig = _sigmoid(gate)
            silu = gate * sig
            dgate_ref[:, cs] = (dact * up * (sig * (1.0 + gate * (1.0 - sig)))).astype(BF16)
            dup_ref[:, cs] = (dact * silu).astype(BF16)
            act_ref[:, cs] = (silu * up).astype(BF16)

    dgate, dup, act = _pcall(
        act_body, name="ffn_bwd_act", grid=(nt, nj),
        in_specs=[_rows(TM, D_MODEL), aspec, aspec, wspec],
        out_specs=[aspec, aspec, aspec], out_shape=[act_shape, act_shape, act_shape],
        scratch_shapes=[pltpu.VMEM((TM, D_MODEL), BF16)],
        semantics=("parallel", "arbitrary"),
    )(dh2, gate, up, wd)

    def in_body(dh2_ref, h1_ref, g2_ref, dgate_ref, dup_ref, wg_ref, wu_ref, hn_ref, dh1_ref, dg2_ref, acc_s):
        i, j = pl.program_id(0), pl.program_id(1)

        @pl.when((i == 0) & (j == 0))
        def _():
            dg2_ref[...] = jnp.zeros_like(dg2_ref)

        part = _dot(dgate_ref[...], wg_ref[...], NN) + _dot(dup_ref[...], wu_ref[...], NN)

        @pl.when(j == 0)
        def _():
            acc_s[...] = part

        @pl.when(j == nj - 1)
        def _():
            h1 = h1_ref[...]
            r = _rms_stats(h1)
            g2 = g2_ref[...]
            hn_ref[...] = (h1 * r * g2).astype(BF16)
            dx, dg = _rms_bwd(acc_s[...] + part, h1, r, g2)
            dh1_ref[...] = dh2_ref[...] + dx
            dg2_ref[...] += dg

    hn2, dh1, dg2 = _pcall(
        in_body, name="ffn_bwd_in", grid=(nt, nj),
        in_specs=[_rows(TM, D_MODEL), _rows(TM, D_MODEL), _full((1, D_MODEL)), aspec, aspec, wspec, wspec],
        out_specs=[_rows(TM, D_MODEL), _rows(TM, D_MODEL), _full((1, D_MODEL))],
        out_shape=[jax.ShapeDtypeStruct((R, D_MODEL), BF16), jax.ShapeDtypeStruct((R, D_MODEL), F32),
                   jax.ShapeDtypeStruct((1, D_MODEL), F32)],
        scratch_shapes=[pltpu.VMEM((TM, D_MODEL), F32)],
        semantics=("arbitrary", "arbitrary"),
    )(dh2, h1, g2, dgate, dup, wg_t, wu_t)
    return dgate, dup, act, hn2, dh1, dg2


def _wgrad(a, b, tm, name, partials=()):
    K, M = a.shape
    N = b.shape[1]
    tk = K // WGRAD_K_TILES if K % (WGRAD_K_TILES * BLOCK) == 0 else TM
    nm, nk, npart = M // tm, K // tk, len(partials)

    def body(a_ref, b_ref, *refs):
        p_ins, o_ref, p_outs, sems = refs[:npart], refs[npart], refs[npart + 1:2 * npart + 1], refs[2 * npart + 1:]
        step = pl.program_id(0) * nk + pl.program_id(1)
        exchange = functools.partial(_chip_copies, p_ins, p_outs, *sems)
        if npart:
            _hosted(step, nm * nk, exchange)

        @pl.when(pl.program_id(1) == 0)
        def _():
            o_ref[...] = jnp.zeros_like(o_ref)

        o_ref[...] += _dot(a_ref[...], b_ref[...].astype(BF16), TN)
        if npart:
            _hosted_wait(step, nm * nk, exchange)

    res = _pcall(
        body, name=name, grid=(nm, nk),
        in_specs=[pl.BlockSpec((tk, tm), lambda m, k: (k, m)), pl.BlockSpec((tk, N), lambda m, k: (k, 0))]
        + [ANY] * npart,
        out_specs=[pl.BlockSpec((tm, N), lambda m, k: (m, 0))] + [ANY] * npart,
        out_shape=[jax.ShapeDtypeStruct((M, N), F32)] + _chip_shapes(partials),
        scratch_shapes=_sem_pair(3 * npart) if npart else [],
        semantics=("arbitrary", "arbitrary"),
    )(a, b, *partials)
    return (res[0], res[1:]) if npart else res[0]


def _outproj_bwd(dh1, o_attn, o_conv, ga, gc, w_out, grads):
    R = dh1.shape[0]
    nt, ng = R // TM, len(grads)

    def body(dh1_ref, oa_ref, oc_ref, ga_ref, gc_ref, w_ref, *refs):
        g_ins, (doa_ref, doc_ref, mixed_ref, dga_ref, dgc_ref) = refs[:ng], refs[ng:ng + 5]
        g_outs, (send_sems, recv_sems) = refs[ng + 5:2 * ng + 5], refs[2 * ng + 5:]
        exchange = functools.partial(_sibling_copies, g_ins, g_outs, send_sems, recv_sems)
        _hosted(pl.program_id(0), nt, exchange)

        @pl.when(pl.program_id(0) == 0)
        def _():
            dga_ref[...] = jnp.zeros_like(dga_ref)
            dgc_ref[...] = jnp.zeros_like(dgc_ref)

        dm = _dot(dh1_ref[...].astype(BF16), w_ref[...], NT)
        oa, oc = oa_ref[...], oc_ref[...]
        ra, rc = _rms_stats(oa), _rms_stats(oc)
        mixed_ref[:, 0:512] = (oa * ra * ga_ref[...]).astype(BF16)
        mixed_ref[:, 512:1024] = (oc * rc * gc_ref[...]).astype(BF16)
        doa, dga = _rms_bwd(dm[:, 0:512], oa, ra, ga_ref[...])
        doc, dgc = _rms_bwd(dm[:, 512:1024], oc, rc, gc_ref[...])
        doa_ref[...] = doa
        doc_ref[...] = doc
        dga_ref[...] += dga
        dgc_ref[...] += dgc
        _hosted_wait(pl.program_id(0), nt, exchange)

    res = _pcall(
        body, name="outproj_bwd", grid=(nt,),
        in_specs=[_rows(TM, D_MODEL), _rows(TM, 512), _rows(TM, 512), _full((1, 512)), _full((1, 512)),
                  _full((D_MODEL, D_MODEL))] + [ANY] * ng,
        out_specs=[_rows(TM, 512), _rows(TM, 512), _rows(TM, D_MODEL), _full((1, 512)), _full((1, 512))]
        + [ANY] * ng,
        out_shape=[jax.ShapeDtypeStruct((R, 512), F32), jax.ShapeDtypeStruct((R, 512), F32),
                   jax.ShapeDtypeStruct((R, D_MODEL), BF16),
                   jax.ShapeDtypeStruct((1, 512), F32), jax.ShapeDtypeStruct((1, 512), F32)]
        + _sibling_shapes(grads),
        scratch_shapes=_sem_pair(ng),
        semantics=("arbitrary",),
    )(dh1, o_attn, o_conv, ga, gc, w_out, *grads)
    return res[:5], res[5:]


def _conv_bwd(do_conv, y, ca, cg, conv_w, ln_g, ln_b, partials):
    R = ca.shape[0]
    nt = R // TM
    hpt = TM // HALO
    npart = len(partials)

    def body(do_ref, doh_ref, y_ref, yh_ref, ca_ref, cg_ref, cah_ref, cgh_ref, w_ref, lg_ref, lb_ref, *refs):
        p_ins, (dca_ref, dcg_ref, dw_ref, db_ref, dlg_ref, dlb_ref) = refs[:npart], refs[npart:npart + 6]
        p_outs, (send_sems, recv_sems, u_s, dy_s, uph_s, dyph_s) = refs[npart + 6:2 * npart + 6], refs[2 * npart + 6:]
        i = pl.program_id(0)
        exchange = functools.partial(_chip_copies, p_ins, p_outs, send_sems, recv_sems)
        _hosted(i, nt, exchange)

        @pl.when(i == 0)
        def _():
            dw_ref[...] = jnp.zeros_like(dw_ref)
            db_ref[...] = jnp.zeros_like(db_ref)
            dlg_ref[...] = jnp.zeros_like(dlg_ref)
            dlb_ref[...] = jnp.zeros_like(dlb_ref)

        lg, lb = lg_ref[...], lb_ref[...]

        def ln_bwd(yv, dov):
            yn, xhat, rstd = _ln_silu(yv, lg, lb)
            sig = _sigmoid(yn)
            dyn = dov * (sig * (1.0 + yn * (1.0 - sig)))
            dxh = dyn * lg
            dyv = rstd * (dxh - jnp.mean(dxh, axis=-1, keepdims=True)
                          - xhat * jnp.mean(dxh * xhat, axis=-1, keepdims=True))
            return dyv, dyn, xhat

        dyv, dyn, xhat = ln_bwd(y_ref[...], do_ref[...])
        dy_s[0:TM, :] = dyv
        dlg_ref[...] += jnp.sum(dyn * xhat, axis=0, keepdims=True)
        dlb_ref[...] += jnp.sum(dyn, axis=0, keepdims=True)
        db_ref[...] += jnp.sum(dyv, axis=0, keepdims=True)
        dyh, _, _ = ln_bwd(yh_ref[...], doh_ref[...])
        dy_s[TM:TM + HALO, :] = jnp.where(i < nt - 1, dyh, 0.0)
        u_s[HALO:HALO + TM, :] = ca_ref[...] * _sigmoid(cg_ref[...])
        u_s[0:HALO, :] = jnp.where(i > 0, cah_ref[...] * _sigmoid(cgh_ref[...]), 0.0)
        _phase_copies(dy_s, dyph_s)
        _phase_copies(u_s, uph_s)

        for rc in range(TM // CONV_ROWS):
            acc = jnp.zeros((CONV_ROWS, CONV_W), F32)
            for k in range(CONV_K):
                acc = acc + _shifted(dy_s, dyph_s, rc * CONV_ROWS + CONV_K - 1 - k, CONV_ROWS) * w_ref[k:k + 1, :]
            rows = slice(rc * CONV_ROWS, (rc + 1) * CONV_ROWS)
            sg = _sigmoid(cg_ref[rows, :])
            dca_ref[rows, :] = (acc * sg).astype(BF16)
            dcg_ref[rows, :] = (acc * ca_ref[rows, :] * sg * (1.0 - sg)).astype(BF16)

        for k in range(CONV_K):
            prod = _shifted(u_s, uph_s, HALO - (CONV_K - 1) + k, TM) * dy_s[0:TM, :]
            dw_ref[k:k + 1, :] += jnp.sum(prod, axis=0, keepdims=True)
        _hosted_wait(i, nt, exchange)

    prev_halo = pl.BlockSpec((HALO, CONV_W), lambda i: (jnp.maximum(i * hpt - 1, 0), 0))
    next_halo = pl.BlockSpec((HALO, CONV_W), lambda i: (jnp.minimum((i + 1) * hpt, nt * hpt - 1), 0))
    vec = jax.ShapeDtypeStruct((1, CONV_W), F32)
    res = _pcall(
        body, name="conv_bwd", grid=(nt,),
        in_specs=[_rows(TM, CONV_W), next_halo, _rows(TM, CONV_W), next_halo,
                  _rows(TM, CONV_W), _rows(TM, CONV_W), prev_halo, prev_halo,
                  _full((CONV_K, CONV_W)), _full((1, CONV_W)), _full((1, CONV_W))] + [ANY] * npart,
        out_specs=[_rows(TM, CONV_W), _rows(TM, CONV_W), _full((32, CONV_W)),
                   _full((1, CONV_W)), _full((1, CONV_W)), _full((1, CONV_W))] + [ANY] * npart,
        out_shape=[jax.ShapeDtypeStruct((R, CONV_W), BF16), jax.ShapeDtypeStruct((R, CONV_W), BF16),
                   jax.ShapeDtypeStruct((32, CONV_W), F32), vec, vec, vec] + _chip_shapes(partials),
        scratch_shapes=_sem_pair(3 * npart)
        + [pltpu.VMEM((HALO + TM, CONV_W), F32), pltpu.VMEM((TM + HALO, CONV_W), F32),
           _phase_scratch(), _phase_scratch()],
        semantics=("arbitrary",),
    )(do_conv, do_conv, y, y, ca, cg, ca, cg, conv_w, ln_g, ln_b, *partials)
    return res[:6], res[6:]


def _attn_bwd(q, kv, sinks, o, lse, do, grads, partials):
    R = q.shape[0]
    nb = R // BLOCK
    ng, npart = len(grads), len(partials)
    nx = ng + npart

    def body(sink_ref, q_ref, kvc_ref, kvp_ref, kvm_ref, o_ref, lse_ref, do_ref, *refs):
        x_ins, (dq_ref, dkv_ref, dkvm_ref, dsink_ref) = refs[:nx], refs[nx:nx + 4]
        x_outs = refs[nx + 4:2 * nx + 4]
        g_send, g_recv, p_send, p_recv, carry_s, cur_s, prev_s, bias_s, sink_s, delta_s = refs[2 * nx + 4:]
        b = pl.program_id(0)

        def exchange():
            return (_sibling_copies(x_ins[:ng], x_outs[:ng], g_send, g_recv)
                    + _chip_copies(x_ins[ng:], x_outs[ng:], p_send, p_recv))

        _hosted(b, nb + 1, exchange)

        @pl.when(b == 0)
        def _():
            dkvm_ref[...] = jnp.zeros_like(dkvm_ref)
            carry_s[...] = jnp.zeros_like(carry_s)
            for h in range(N_HEADS):
                dsink_ref[0, h] = 0.0
            _attn_tables(sink_ref, bias_s, sink_s)

        @pl.when(b < nb)
        def _():
            sel, pen, pen_m = _attn_masks(b)
            q_t = (q_ref[...] * ATTN_SCALE).T
            do_t = do_ref[...].astype(BF16).T
            kvc_t, kvp_t = kvc_ref[...].T, kvp_ref[...].T
            prod = do_ref[...] * o_ref[...]
            hi = prod.astype(BF16)
            lo = (prod - hi.astype(F32)).astype(BF16)
            head_of = lax.broadcasted_iota(jnp.int32, (N_HEADS, ATTN_W), 1) // HEAD_DIM
            ind = (head_of == lax.broadcasted_iota(jnp.int32, (N_HEADS, ATTN_W), 0)).astype(BF16)
            delta_s[...] = _dot(ind, hi, NT) + _dot(ind, lo, NT)
            dqs = []
            for g in range(N_HEADS // GROUP):
                ks, vs = slice(HEAD_DIM * g, HEAD_DIM * (g + 1)), slice(KV_W + HEAD_DIM * g, KV_W + HEAD_DIM * (g + 1))
                lanes = slice(GB * g, GB * (g + 1))
                qg, dog = _group_lanes(q_t, g), _group_lanes(do_t, g)
                kc, kp, km = kvc_ref[:, ks], kvp_ref[:, ks], kvm_ref[LEAD:BLOCK, ks]
                vc, vp, vm = kvc_ref[:, vs], kvp_ref[:, vs], kvm_ref[LEAD:BLOCK, vs]
                s_b, s_m = _attn_scores(qg, kc, kp, km, sel, pen, pen_m, bias_s[:, lanes])
                lse, delta = _head_lanes(lse_ref, g), _head_lanes(delta_s, g)
                p_b = jnp.exp(s_b - lse)
                p_m = jnp.exp(s_m - lse)
                dp_b = jnp.where(sel, _dot(vc, dog, NN), _dot(vp, dog, NN))
                ds_b = p_b * (dp_b - delta)
                ds_m = (p_m * (_dot(vm, dog, NN) - delta)).astype(BF16)
                dsk = jnp.exp(sink_s[:, lanes] - lse) * delta
                for j in range(GROUP):
                    dsink_ref[0, GROUP * g + j] += -jnp.sum(dsk[:, BLOCK * j:BLOCK * (j + 1)])
                ds_c = jnp.where(sel, ds_b, 0.0).astype(BF16)
                ds_p = jnp.where(sel, 0.0, ds_b).astype(BF16)
                p_c = jnp.where(sel, p_b, 0.0).astype(BF16)
                p_p = jnp.where(sel, 0.0, p_b).astype(BF16)
                dqs.append((_dot(kvc_t[ks, :], ds_c, NN) + _dot(kvp_t[ks, :], ds_p, NN)
                            + _dot(km, ds_m, TN)) * ATTN_SCALE)
                cur_s[:, ks] = _dot(ds_c, qg, NT)
                cur_s[:, vs] = _dot(p_c, dog, NT)
                prev_s[:, ks] = _dot(ds_p, qg, NT)
                prev_s[:, vs] = _dot(p_p, dog, NT)
                dkvm_ref[:, ks] += _dot(ds_m, qg, NT)
                dkvm_ref[:, vs] += _dot(p_m.astype(BF16), dog, NT)
            dq_ref[...] = _head_rows(dqs).astype(BF16).T
            dkv_ref[...] = (carry_s[...] + prev_s[...]).astype(BF16)
            carry_s[...] = cur_s[...]

        @pl.when(b == nb)
        def _():
            dkv_ref[...] = carry_s[...].astype(BF16)

        _hosted_wait(b, nb + 1, exchange)

    def at(off):
        return lambda b: (jnp.clip(b + off, 0, nb - 1), 0)

    blk = lambda cols, off=0: pl.BlockSpec((BLOCK, cols), at(off))
    res = _pcall(
        body, name="attn_bwd", grid=(nb + 1,),
        in_specs=[pl.BlockSpec(memory_space=pltpu.SMEM), blk(512), blk(256), blk(256, -1), _full((BLOCK, 256)),
                  blk(512), pl.BlockSpec((N_HEADS, BLOCK), lambda b: (0, jnp.minimum(b, nb - 1))), blk(512)]
        + [ANY] * nx,
        out_specs=[blk(512), blk(256, -1), _full((N_META, 256)), pl.BlockSpec(memory_space=pltpu.SMEM)]
        + [ANY] * nx,
        out_shape=[jax.ShapeDtypeStruct((R, 512), BF16), jax.ShapeDtypeStruct((R, 256), BF16),
                   jax.ShapeDtypeStruct((N_META, 256), F32), jax.ShapeDtypeStruct((1, N_HEADS), F32)]
        + _sibling_shapes(grads) + _chip_shapes(partials),
        scratch_shapes=_sem_pair(ng) + _sem_pair(3 * npart) + [pltpu.VMEM((BLOCK, 256), F32)] * 3
        + [pltpu.VMEM((BLOCK, N_HEADS * BLOCK), F32), pltpu.VMEM((1, N_HEADS * BLOCK), F32),
           pltpu.VMEM((N_HEADS, BLOCK), F32)],
        semantics=("arbitrary",),
    )(sinks, q, kv, kv, kv, o, lse, do, *grads, *partials)
    return res[:4], res[4:4 + ng], res[4 + ng:]


def _inproj_bwd(dh1, h0, g1, dq, dkv, dkvm, dca, dcg, w_in_t):
    R = h0.shape[0]
    nt = R // TM
    assert nt >= 2

    def body(dh1_ref, h0_ref, g_ref, dq_ref, dkv_ref, dkvm_ref, dca_ref, dcg_ref, w_ref,
             gx_hbm, dmeta_ref, dproj_ref, hn_ref, dg_ref, dx_s, gx_sems):
        i = pl.program_id(0)

        def gx_copy(step, slot, first):
            if first:
                return pltpu.make_async_copy(dx_s.at[slot, pl.ds(BLOCK, TM - BLOCK)],
                                             gx_hbm.at[pl.ds(0, TM - BLOCK)], gx_sems.at[slot])
            return pltpu.make_async_copy(
                dx_s.at[slot], gx_hbm.at[pl.ds(pl.multiple_of(step * TM - BLOCK, BLOCK), TM)], gx_sems.at[slot])

        @pl.when(i == 0)
        def _():
            dg_ref[...] = jnp.zeros_like(dg_ref)

        dproj_ref[:, 0:512] = dq_ref[...]
        dproj_ref[:, 512:768] = dkv_ref[...]
        dproj_ref[:, 768:1280] = dca_ref[...]
        dproj_ref[:, 1280:1792] = dcg_ref[...]

        @pl.when(i == 0)
        def _():
            dproj_ref[LEAD:BLOCK, 512:768] = dkvm_ref[...].astype(BF16)

        dhn = _dot(dproj_ref[...], w_ref[...], NN)
        h = h0_ref[...]
        r = _rms_stats(h)
        g = g_ref[...]
        hn_ref[...] = (h * r * g).astype(BF16)
        dx, dg = _rms_bwd(dhn, h, r, g)
        dg_ref[...] += dg
        slot = i % 2
        pl.when(i == 2)(lambda: gx_copy(0, 0, True).wait())
        pl.when(i > 2)(lambda: gx_copy(i - 2, slot, False).wait())
        dx_s[slot] = dh1_ref[...] + dx

        @pl.when(i == 0)
        def _():
            dmeta_ref[...] = dx_s[0, LEAD:BLOCK, :]
            gx_copy(0, 0, True).start()

        pl.when(i > 0)(lambda: gx_copy(i, slot, False).start())

        @pl.when(i == nt - 1)
        def _():
            gx_copy(nt - 2, (nt - 2) % 2, nt == 2).wait()
            gx_copy(nt - 1, (nt - 1) % 2, False).wait()

    return _pcall(
        body, name="inproj_bwd", grid=(nt,),
        in_specs=[_rows(TM, D_MODEL), _rows(TM, D_MODEL), _full((1, D_MODEL)), _rows(TM, 512), _rows(TM, 256),
                  _full((N_META, 256)), _rows(TM, 512), _rows(TM, 512), _full((1792, D_MODEL))],
        out_specs=[ANY, _full((N_META, D_MODEL)), _rows(TM, 1792), _rows(TM, D_MODEL), _full((1, D_MODEL))],
        out_shape=[jax.ShapeDtypeStruct((R - BLOCK, D_MODEL), F32), jax.ShapeDtypeStruct((N_META, D_MODEL), F32),
                   jax.ShapeDtypeStruct((R, 1792), BF16),
                   jax.ShapeDtypeStruct((R, D_MODEL), BF16), jax.ShapeDtypeStruct((1, D_MODEL), F32)],
        scratch_shapes=[pltpu.VMEM((2, TM, D_MODEL), F32), pltpu.SemaphoreType.DMA((2,))],
        semantics=("arbitrary",),
    )(dh1, h0, g1, dq, dkv, dkvm, dca, dcg, w_in_t)


ANY = pl.BlockSpec(memory_space=pl.ANY)


def _position():
    return lax.axis_index("x"), lax.axis_index("y"), lax.axis_index("c")


def _device_number(p):
    return 4 * p[0] + 2 * p[1] + p[2]


def _two_level_allgather(ins, outs, block, send_sems, recv_sems, local_sems, sem_base=0):
    n = len(ins)
    x, y, c = _position()
    me, sibling = (x, y, c), (x, y, 1 - c)
    chips = [(1 - x, y), (x, 1 - y), (1 - x, 1 - y)]

    def copy(w, k, origin, to, src=None):
        return pltpu.make_async_remote_copy(
            src_ref=block(w, origin) if src is None else src, dst_ref=block(w, origin),
            send_sem=send_sems.at[sem_base + 7 * w + k], recv_sem=recv_sems.at[sem_base + 7 * w + k],
            device_id=to, device_id_type=MESH)

    def mine(w):
        return pltpu.make_async_copy(ins[w], block(w, me), local_sems.at[w])

    def own(w):
        return [copy(w, 0, me, sibling, src=ins[w])] + [
            copy(w, 1 + j, me, (*chip, c), src=ins[w]) for j, chip in enumerate(chips)]

    def passed(w):
        return [copy(w, 4 + j, (*chip, c), sibling) for j, chip in enumerate(chips)]

    def start():
        for w in range(n):
            mine(w).start()
        for w in range(n):
            for cp in own(w):
                cp.start()

    def forward(w):
        fw = passed(w)
        for j, chip in enumerate(chips):
            copy(w, 1 + j, (*chip, c), me).wait_recv()
            fw[j].start()

    def finish():
        for w in range(n):
            copy(w, 0, sibling, me).wait_recv()
            for j, chip in enumerate(chips):
                copy(w, 4 + j, (*chip, 1 - c), me).wait_recv()
        for w in range(n):
            for cp in own(w) + passed(w):
                cp.wait_send()
            mine(w).wait()

    return start, forward, finish


def _carried_allgather(step, n_steps, shards, refs):
    ns = len(shards)
    ins, outs, (send_sems, recv_sems, local_sems) = refs[:ns], refs[ns:2 * ns], refs[2 * ns:]
    start, forward, finish = _two_level_allgather(
        ins, outs, _row_block(outs, [s.shape[0] for s in shards]), send_sems, recv_sems, local_sems)
    pl.when(step == 0)(start)
    total = sum(s.shape[0] for s in shards)
    sent = 0
    for w, s in enumerate(shards):
        sent += s.shape[0]
        pl.when(step == (AG_FORWARD_AT * sent * (n_steps - 1)) // (100 * total))(functools.partial(forward, w))
    return lambda: pl.when(step == n_steps - 1)(finish)


def _gathered_shapes(shards):
    return [jax.ShapeDtypeStruct((N_DEV * s.shape[0], s.shape[1]), s.dtype) for s in shards]


def _allgather_sems(ns):
    return _sem_pair(7 * ns) + [pltpu.SemaphoreType.DMA((ns,))]


def _blocking_allgather(ins, outs, block, send_sems, recv_sems, local_sems, sem_base=0):
    start, forward, finish = _two_level_allgather(ins, outs, block, send_sems, recv_sems, local_sems, sem_base)
    start()
    for w in range(len(ins)):
        forward(w)
    finish()


def _row_block(outs, rows):
    def block(w, p):
        return outs[w].at[pl.ds(pl.multiple_of(_device_number(p) * rows[w], 16), rows[w])]
    return block


def _sibling_copies(ins, outs, send_sems, recv_sems):
    x, y, c = _position()
    return [pltpu.make_async_remote_copy(
        src_ref=ins[w].at[:, 1 - c], dst_ref=outs[w], send_sem=send_sems.at[w], recv_sem=recv_sems.at[w],
        device_id=(x, y, 1 - c), device_id_type=MESH) for w in range(len(ins))]


def _chip_copies(ins, outs, send_sems, recv_sems):
    x, y, c = _position()
    chips = [(1 - x, y), (x, 1 - y), (1 - x, 1 - y)]
    return [pltpu.make_async_remote_copy(
        src_ref=ins[w].at[2 * chip[0] + chip[1]], dst_ref=outs[w].at[k],
        send_sem=send_sems.at[3 * w + k], recv_sem=recv_sems.at[3 * w + k],
        device_id=(*chip, c), device_id_type=MESH) for w in range(len(ins)) for k, chip in enumerate(chips)]


def _hosted(step, n_steps, make_copies):
    @pl.when(step == 0)
    def _():
        for cp in make_copies():
            cp.start()


def _hosted_wait(step, n_steps, make_copies):
    @pl.when(step == n_steps - 1)
    def _():
        for cp in make_copies():
            cp.wait()


def _sem_pair(n):
    return [pltpu.SemaphoreType.DMA((n,)), pltpu.SemaphoreType.DMA((n,))]


def _allgather_params(shards, small):
    arrays = list(shards) + list(small)
    n, ns = len(arrays), len(shards)

    def body(*refs):
        ins, outs = refs[:n], refs[n:2 * n]
        send_sems, recv_sems, local_sems = refs[2 * n:]

        rows = _row_block(outs, [a.shape[0] for a in arrays])

        def block(w, p):
            return rows(w, p) if w < ns else outs[w].at[_device_number(p)]

        _blocking_allgather(ins, outs, block, send_sems, recv_sems, local_sems)

    out_shape = [jax.ShapeDtypeStruct((N_DEV * a.shape[0], a.shape[1]), a.dtype) for a in shards]
    out_shape += [jax.ShapeDtypeStruct((N_DEV,) + a.shape, a.dtype) for a in small]
    return _pcall(
        body, name="allgather_params", in_specs=[ANY] * n, out_specs=[ANY] * n, out_shape=out_shape,
        scratch_shapes=[pltpu.SemaphoreType.DMA((7 * n,)), pltpu.SemaphoreType.DMA((7 * n,)),
                        pltpu.SemaphoreType.DMA((n,))],
    )(*arrays)


def _reduce_siblings(grads):
    n = len(grads)

    def body(*refs):
        ins, outs = refs[:n], refs[n:2 * n]
        send_sems, recv_sems = refs[2 * n:]
        copies = _sibling_copies(ins, outs, send_sems, recv_sems)
        for cp in copies:
            cp.start()
        for cp in copies:
            cp.wait()

    return _pcall(
        body, name="reduce_siblings", in_specs=[ANY] * n, out_specs=[ANY] * n,
        out_shape=_sibling_shapes(grads), scratch_shapes=_sem_pair(n),
    )(*grads)


def _sibling_shapes(grads):
    return [jax.ShapeDtypeStruct((4,) + g.shape[2:], F32) for g in grads]


def _chip_shapes(partials):
    return [jax.ShapeDtypeStruct((3,) + p.shape[1:], F32) for p in partials]


def _add_sibling(grad, received, core, name):
    _, _, r, cols = grad.shape

    def body(core_ref, g_ref, r_ref, o_ref):
        o_ref[...] = g_ref[...] + r_ref[...]

    return pl.pallas_call(
        body, name=name,
        grid_spec=pltpu.PrefetchScalarGridSpec(
            num_scalar_prefetch=1, grid=(4,),
            in_specs=[pl.BlockSpec((None, None, r, cols), lambda s, core_ref: (s, core_ref[0], 0, 0)),
                      pl.BlockSpec((None, r, cols), lambda s, core_ref: (s, 0, 0))],
            out_specs=pl.BlockSpec((None, r, cols), lambda s, core_ref: (s, 0, 0))),
        out_shape=jax.ShapeDtypeStruct((4, r, cols), F32),
        compiler_params=pltpu.CompilerParams(vmem_limit_bytes=VMEM_LIMIT),
    )(core, grad, received)


def _reduce_chips(partials, small):
    n, ns = len(partials), len(small)

    def body(*refs):
        p_ins, s_ins = refs[:n], refs[n:n + ns]
        p_outs, s_outs = refs[n + ns:2 * n + ns], refs[2 * n + ns:2 * (n + ns)]
        send_sems, recv_sems, local_sems = refs[2 * (n + ns):]
        copies = _chip_copies(p_ins, p_outs, send_sems, recv_sems)
        for cp in copies:
            cp.start()
        _blocking_allgather(s_ins, s_outs, lambda w, p: s_outs[w].at[_device_number(p)],
                            send_sems, recv_sems, local_sems, sem_base=3 * n)
        for cp in copies:
            cp.wait()

    out_shape = _chip_shapes(partials)
    out_shape += [jax.ShapeDtypeStruct((N_DEV,) + a.shape, a.dtype) for a in small]
    nsem = 3 * n + 7 * ns
    return _pcall(
        body, name="reduce_chips", in_specs=[ANY] * (n + ns), out_specs=[ANY] * (n + ns), out_shape=out_shape,
        scratch_shapes=[pltpu.SemaphoreType.DMA((nsem,)), pltpu.SemaphoreType.DMA((nsem,)),
                        pltpu.SemaphoreType.DMA((ns,))],
    )(*partials, *small)


def _adam(g, w, m, v):
    m = ADAM_B1 * m + (1.0 - ADAM_B1) * g
    v = ADAM_B2 * v + (1.0 - ADAM_B2) * (g * g)
    m_hat = m / (1.0 - ADAM_B1 ** ADAM_STEP)
    v_hat = v / (1.0 - ADAM_B2 ** ADAM_STEP)
    delta = -ADAM_LR * (m_hat / (jnp.sqrt(v_hat) + ADAM_EPS) + ADAM_WD * w)
    return delta, m, v


def _adamw(partial, received, slot, w, m, v, name):
    _, r, cols = partial.shape

    def body(slot_ref, p_ref, r_ref, w_ref, m_ref, v_ref, g_ref, d_ref, nm_ref, nv_ref):
        g = p_ref[...] + r_ref[0] + r_ref[1] + r_ref[2]
        g_ref[...] = g
        d_ref[...], nm_ref[...], nv_ref[...] = _adam(g, w_ref[...], m_ref[...], v_ref[...])

    whole = pl.BlockSpec((r, cols), lambda i, slot_ref: (0, 0))
    out = jax.ShapeDtypeStruct((r, cols), F32)
    return pl.pallas_call(
        body, name=name,
        grid_spec=pltpu.PrefetchScalarGridSpec(
            num_scalar_prefetch=1, grid=(1,),
            in_specs=[pl.BlockSpec((None, r, cols), lambda i, slot_ref: (slot_ref[0], 0, 0)),
                      pl.BlockSpec((3, r, cols), lambda i, slot_ref: (0, 0, 0)), whole, whole, whole],
            out_specs=[whole, whole, whole, whole]),
        out_shape=[out, out, out, out],
        compiler_params=pltpu.CompilerParams(vmem_limit_bytes=VMEM_LIMIT),
    )(slot, partial, received, w, m, v)


def _adamw_small(dev, ga, gb, gc, params):
    names = ["meta", "attn_norm", "sinks", "conv_w", "conv_b", "ln_g", "ln_b", "attn_out", "conv_out",
             "ffn_norm", "final_norm"]
    flat = [a for p in params for a in p]
    n_in = len(flat)

    def body(dev_ref, ga_ref, gb_ref, gc_ref, *refs):
        ins, outs = refs[:n_in], refs[n_in:n_in + 4 * len(names)]
        loss_ref, sb, sc = refs[n_in + 4 * len(names):]
        a = ga_ref[0]
        sb[...] = gb_ref[0]
        sc[...] = gc_ref[0]
        for d in range(1, N_DEV):
            a = a + ga_ref[d]
            sb[...] += gb_ref[d]
            sc[...] += gc_ref[d]
        dev = dev_ref[0]
        grads = {
            "attn_norm": a[0:1, :], "ffn_norm": a[1:2, :], "final_norm": a[2:3, :],
            "conv_b": a[3:4, 0:512], "ln_g": a[3:4, 512:1024], "ln_b": a[4:5, 0:512],
            "attn_out": a[4:5, 512:1024], "conv_out": a[5:6, 0:512], "sinks": a[5:6, 512:512 + N_HEADS],
            "meta": sb[pl.ds(pl.multiple_of(dev * N_META, N_META), N_META), :],
            "conv_w": sc[pl.ds(pl.multiple_of(dev * 32, 32), 32), :][0:CONV_K, :],
        }
        for idx, nm in enumerate(names):
            w_ref, m_ref, v_ref = ins[3 * idx:3 * idx + 3]
            g = grads[nm]
            delta, m, v = _adam(g, w_ref[...], m_ref[...], v_ref[...])
            o = outs[4 * idx:4 * idx + 4]
            o[0][...], o[1][...], o[2][...], o[3][...] = g, delta, m, v
        loss_ref[...] = a[6:7, 0:1]

    vm = pl.BlockSpec(memory_space=pltpu.VMEM)
    out_shape = [jax.ShapeDtypeStruct(p[0].shape, F32) for p in params for _ in range(4)]
    out_shape.append(jax.ShapeDtypeStruct((1, 1), F32))
    res = pl.pallas_call(
        body, name="adamw_small",
        grid_spec=pltpu.PrefetchScalarGridSpec(
            num_scalar_prefetch=1, grid=(1,),
            in_specs=[pl.BlockSpec(ga.shape, lambda i, d: (0, 0, 0)), pl.BlockSpec(gb.shape, lambda i, d: (0, 0, 0)),
                      pl.BlockSpec(gc.shape, lambda i, d: (0, 0, 0))]
            + [pl.BlockSpec(a.shape, lambda i, d: (0, 0)) for a in flat],
            out_specs=[pl.BlockSpec(s.shape, lambda i, d: (0, 0)) for s in out_shape],
            scratch_shapes=[pltpu.VMEM(gb.shape[1:], F32), pltpu.VMEM(gc.shape[1:], F32)]),
        out_shape=out_shape,
        compiler_params=pltpu.CompilerParams(vmem_limit_bytes=VMEM_LIMIT),
    )(dev, ga, gb, gc, *flat)
    return [res[4 * i:4 * i + 4] for i in range(len(names))], res[-1]


def kernel(x, meta_tokens, attn_norm_g, w_in, attn_sinks, conv_w, conv_b, conv_ln_g, conv_ln_b, attn_out_g, conv_out_g, w_out, ffn_norm_g, w_gate, w_up, w_down, final_norm_g, loss_target, m_meta_tokens, m_attn_norm_g, m_w_in, m_attn_sinks, m_conv_w, m_conv_b, m_conv_ln_g, m_conv_ln_b, m_attn_out_g, m_conv_out_g, m_w_out, m_ffn_norm_g, m_w_gate, m_w_up, m_w_down, m_final_norm_g, v_meta_tokens, v_attn_norm_g, v_w_in, v_attn_sinks, v_conv_w, v_conv_b, v_conv_ln_g, v_conv_ln_b, v_attn_out_g, v_conv_out_g, v_w_out, v_ffn_norm_g, v_w_gate, v_w_up, v_w_down, v_final_norm_g):
    xi, yi, ci = _position()
    dev = jnp.reshape(_device_number((xi, yi, ci)), (1,)).astype(jnp.int32)
    core = jnp.reshape(ci, (1,)).astype(jnp.int32)
    slot = jnp.reshape(2 * xi + yi, (1,)).astype(jnp.int32)

    w_in_t, meta_st, convw_st = _allgather_params([w_in[0].T.astype(BF16)], [meta_tokens, conv_w[0]])
    meta_full = jnp.transpose(meta_st, (1, 0, 2)).reshape(N_META, D_MODEL)
    convw_full = jnp.transpose(convw_st, (1, 0, 2)).reshape(CONV_K, CONV_W)

    final_g = final_norm_g.reshape(1, D_MODEL)

    (h0, q, kv, ca, cg), (w_out_b, wg_t) = _inproj_fwd(
        x[0], meta_full, attn_norm_g, w_in_t, [w_out[0].astype(BF16), w_gate[0].T.astype(BF16)])
    o_attn, lse, (wu_t,) = _attn_fwd(q, kv, attn_sinks, [w_up[0].T.astype(BF16)])
    (o_conv, y_conv), (wd_b,) = _conv_fwd(ca, cg, convw_full, conv_b, conv_ln_g, conv_ln_b, [w_down[0].astype(BF16)])
    h1 = _outproj_fwd(h0, o_attn, o_conv, attn_out_g, conv_out_g, w_out_b)
    gate, up, dh2, loss_sum, dg_final = _ffn_fwd(h1, ffn_norm_g, wg_t, wu_t, wd_b, final_g, loss_target[0])

    def blocks(g):
        return g.reshape(4, 2, g.shape[0] // N_DEV, D_MODEL)

    def add_siblings(grads, received, tags):
        return [_add_sibling(g, r, core, "add_sibling_" + t) for g, r, t in zip(grads, received, tags)]

    dgate, dup, act, hn2, dh1, dg_ffn = _ffn_bwd(dh2, h1, ffn_norm_g, gate, up, wg_t, wu_t, wd_b)
    ffn_grads = [blocks(_wgrad(dgate, hn2, FF_CHUNK, "wgrad_gate")), blocks(_wgrad(dup, hn2, FF_CHUNK, "wgrad_up")),
                 blocks(_wgrad(act, dh2, FF_CHUNK, "wgrad_down"))]
    (do_attn, do_conv, mixed, dg_ao, dg_co), ffn_sib = _outproj_bwd(
        dh1, o_attn, o_conv, attn_out_g, conv_out_g, w_out_b, ffn_grads)
    ffn_sums = add_siblings(ffn_grads, ffn_sib, ("gate", "up", "down"))
    out_grads = [blocks(_wgrad(mixed, dh1, D_MODEL, "wgrad_out"))]
    (dca, dcg, dconvw, dconvb, dln_g, dln_b), gate_up_chips = _conv_bwd(
        do_conv, y_conv, ca, cg, convw_full, conv_ln_g, conv_ln_b, ffn_sums[:2])
    (dq, dkv, dkvm, dsinks), out_sib, down_chips = _attn_bwd(
        q, kv, attn_sinks, o_attn, lse, do_attn, out_grads, ffn_sums[2:])
    ffn_chips = list(gate_up_chips) + list(down_chips)
    out_sums = add_siblings(out_grads, out_sib, ("out",))
    grad_x, dmeta, dproj, hn1, dg_attn = _inproj_bwd(dh1, h0, attn_norm_g, dq, dkv, dkvm, dca, dcg, w_in_t)
    dwi_t, out_chips = _wgrad(dproj, hn1, 1792, "wgrad_in", out_sums)
    in_grads = [blocks(dwi_t)]
    in_sums = add_siblings(in_grads, _reduce_siblings(in_grads), ("in",))
    small_a = jnp.concatenate([
        dg_attn, dg_ffn, dg_final, jnp.concatenate([dconvb, dln_g], axis=1), jnp.concatenate([dln_b, dg_ao], axis=1),
        jnp.concatenate([dg_co, dsinks, jnp.zeros((1, 512 - N_HEADS), F32)], axis=1),
        jnp.concatenate([loss_sum[0:1, :], jnp.zeros((1, D_MODEL - 128), F32)], axis=1),
        jnp.zeros((1, D_MODEL), F32)], axis=0)
    small_b = jnp.transpose(dmeta.reshape(N_META, N_DEV, 128), (1, 0, 2)).reshape(N_DEV * N_META, 128)
    small_c = jnp.transpose(dconvw.reshape(32, N_DEV, 64), (1, 0, 2)).reshape(N_DEV * 32, 64)
    in_chips, ga, gb, gc = _reduce_chips(in_sums, [small_a, small_b, small_c])
    tags = ("in", "out", "gate", "up", "down")
    chip_sums = in_sums + out_sums + ffn_sums
    from_chips = [in_chips] + list(out_chips) + list(ffn_chips)

    big = [(True, w_in, m_w_in, v_w_in), (False, w_out, m_w_out, v_w_out), (True, w_gate, m_w_gate, v_w_gate),
           (True, w_up, m_w_up, v_w_up), (False, w_down, m_w_down, v_w_down)]
    big_out = {}
    for t, p, r, (transposed, w, m, v) in zip(tags, chip_sums, from_chips, big):
        rows = (lambda a: jnp.transpose(a[0])) if transposed else (lambda a: a[0])
        back = (lambda a: jnp.transpose(a)[None]) if transposed else (lambda a: a[None])
        big_out[t] = [back(a) for a in _adamw(p, r, slot, rows(w), rows(m), rows(v), "adamw_" + t)]

    small_params = [
        (meta_tokens, m_meta_tokens, v_meta_tokens), (attn_norm_g, m_attn_norm_g, v_attn_norm_g),
        (attn_sinks, m_attn_sinks, v_attn_sinks), (conv_w[0], m_conv_w[0], v_conv_w[0]),
        (conv_b, m_conv_b, v_conv_b), (conv_ln_g, m_conv_ln_g, v_conv_ln_g), (conv_ln_b, m_conv_ln_b, v_conv_ln_b),
        (attn_out_g, m_attn_out_g, v_attn_out_g), (conv_out_g, m_conv_out_g, v_conv_out_g),
        (ffn_norm_g, m_ffn_norm_g, v_ffn_norm_g),
        (final_g, m_final_norm_g.reshape(1, D_MODEL), v_final_norm_g.reshape(1, D_MODEL))]
    sm, loss = _adamw_small(dev, ga, gb, gc, small_params)
    sm[3] = [a[None] for a in sm[3]]
    sm[10] = [a.reshape(D_MODEL) for a in sm[10]]

    per_param = [sm[0], sm[1], big_out["in"], sm[2], sm[3], sm[4], sm[5], sm[6], sm[7], sm[8], big_out["out"],
                 sm[9], big_out["gate"], big_out["up"], big_out["down"], sm[10]]
    loss = loss.reshape(())
    outs = [loss, grad_x[None]]
    for kind in range(4):
        outs += [p[kind] for p in per_param]
    return tuple(outs)
```

```python
import functools
import math

import jax
import jax.numpy as jnp
from jax import lax
from jax.experimental import pallas as pl
from jax.experimental.pallas import tpu as pltpu

F32, BF16 = jnp.float32, jnp.bfloat16
MESH = pl.DeviceIdType.MESH

D_MODEL = 1024
N_META = 16
BLOCK = 128
LEAD = BLOCK - N_META
HEAD_DIM = 64
N_HEADS = 8
GROUP = 4
ATTN_W = 512
KV_W = 128
CONV_W = 512
CONV_K = 31
HALO = 32
D_FF = 2816
FF_CHUNK = D_FF // 2
FF_SUB = [slice(s, s + 256) for s in range(0, D_FF, 256)]
CHUNK_SUB = [slice(s, min(s + 256, FF_CHUNK)) for s in range(0, FF_CHUNK, 256)]
N_DEV = 8
EPS = 1e-5
NEG = -1e30
TM = 640
AG_FORWARD_AT = 85
WGRAD_K_TILES = 5
CONV_ROWS = 64
VMEM_LIMIT = 56 * 1024 * 1024

ADAM_LR, ADAM_B1, ADAM_B2, ADAM_EPS, ADAM_WD, ADAM_STEP = 0.001, 0.9, 0.999, 1e-08, 0.01, 10

NT = (((1,), (1,)), ((), ()))
NN = (((1,), (0,)), ((), ()))
TN = (((0,), (0,)), ((), ()))


def _dot(a, b, dims):
    return lax.dot_general(a, b, dims, preferred_element_type=F32)


def _sigmoid(x):
    return 1.0 / (1.0 + jnp.exp(-x))


def _pcall(body, *, name, out_shape, grid=None, in_specs=None, out_specs=None, scratch_shapes=(),
           semantics=None, **kw):
    params = dict(vmem_limit_bytes=VMEM_LIMIT)
    if semantics is not None:
        params["dimension_semantics"] = semantics
    extra = {}
    if grid is not None:
        extra["grid"] = grid
    if in_specs is not None:
        extra["in_specs"] = in_specs
    if out_specs is not None:
        extra["out_specs"] = out_specs
    return pl.pallas_call(body, name=name, out_shape=out_shape, scratch_shapes=list(scratch_shapes),
                          compiler_params=pltpu.CompilerParams(**params), **extra, **kw)


def _rows(tm, cols, off=0):
    return pl.BlockSpec((tm, cols), lambda i, *_: (i + off, 0))


def _full(shape):
    nd = len(shape)
    return pl.BlockSpec(shape, lambda *_: (0,) * nd)


def _rms_stats(x):
    return lax.rsqrt(jnp.mean(x * x, axis=-1, keepdims=True) + EPS)


def _rms_bwd(dy, x, r, g):
    t = dy * g
    dx = r * (t - x * (r * r) * jnp.mean(t * x, axis=-1, keepdims=True))
    dg = jnp.sum(dy * x * r, axis=0, keepdims=True)
    return dx, dg


def _inproj_fwd(x, meta, g1, w_in_t, shards):
    R = x.shape[0] + BLOCK
    nt = R // TM
    ns = len(shards)
    assert nt >= 2

    def body(x_hbm, meta_ref, g_ref, w_ref, *refs):
        ag_ins, (h0_ref, q_ref, kv_ref, ca_ref, cg_ref), ag_outs = refs[:ns], refs[ns:ns + 5], refs[ns + 5:2 * ns + 5]
        ag_sems, (x_s, sems) = refs[2 * ns + 5:2 * ns + 8], refs[2 * ns + 8:]
        i = pl.program_id(0)
        slot = i % 2
        ag_finish = _carried_allgather(i, nt, shards, ag_ins + ag_outs + ag_sems)

        def x_copy(step, slot, first):
            if first:
                return pltpu.make_async_copy(x_hbm.at[pl.ds(0, TM - BLOCK)],
                                             x_s.at[slot, pl.ds(BLOCK, TM - BLOCK)], sems.at[slot])
            return pltpu.make_async_copy(
                x_hbm.at[pl.ds(pl.multiple_of(step * TM - BLOCK, BLOCK), TM)], x_s.at[slot], sems.at[slot])

        @pl.when(i == 0)
        def _():
            x_copy(0, 0, True).start()
            x_s[0, 0:LEAD, :] = jnp.zeros((LEAD, D_MODEL), F32)
            x_s[0, LEAD:BLOCK, :] = meta_ref[...]

        pl.when(i + 1 < nt)(lambda: x_copy(i + 1, 1 - slot, False).start())
        pl.when(i == 0)(lambda: x_copy(0, 0, True).wait())
        pl.when(i > 0)(lambda: x_copy(i, slot, False).wait())
        h = x_s[slot]
        h0_ref[...] = h
        hn = (h * _rms_stats(h) * g_ref[...]).astype(BF16)
        q_ref[...] = _dot(hn, w_ref[0:512, :], NT).astype(BF16)
        kv_ref[...] = _dot(hn, w_ref[512:768, :], NT).astype(BF16)
        ca_ref[...] = _dot(hn, w_ref[768:1280, :], NT)
        cg_ref[...] = _dot(hn, w_ref[1280:1792, :], NT)
        ag_finish()

    anywhere = pl.BlockSpec(memory_space=pl.ANY)
    res = _pcall(
        body, name="inproj_fwd", grid=(nt,),
        in_specs=[anywhere, _full((N_META, D_MODEL)), _full((1, D_MODEL)), _full((1792, D_MODEL))] + [anywhere] * ns,
        out_specs=[_rows(TM, D_MODEL), _rows(TM, 512), _rows(TM, 256), _rows(TM, 512), _rows(TM, 512)]
        + [anywhere] * ns,
        out_shape=[jax.ShapeDtypeStruct((R, D_MODEL), F32),
                   jax.ShapeDtypeStruct((R, 512), BF16), jax.ShapeDtypeStruct((R, 256), BF16),
                   jax.ShapeDtypeStruct((R, 512), F32), jax.ShapeDtypeStruct((R, 512), F32)]
        + _gathered_shapes(shards),
        scratch_shapes=_allgather_sems(ns) + [pltpu.VMEM((2, TM, D_MODEL), F32), pltpu.SemaphoreType.DMA((2,))],
        semantics=("arbitrary",),
    )(x, meta, g1, w_in_t, *shards)
    return res[:5], res[5:]


GB = GROUP * BLOCK
ATTN_SCALE = 1.0 / math.sqrt(HEAD_DIM)


def _group_lanes(xt, g):
    return jnp.concatenate(
        [xt[HEAD_DIM * (GROUP * g + j):HEAD_DIM * (GROUP * g + j + 1), :] for j in range(GROUP)], axis=1)


def _head_lanes(ref, g):
    return jnp.concatenate([ref[GROUP * g + j:GROUP * g + j + 1, :] for j in range(GROUP)], axis=1)


def _head_rows(xs):
    return jnp.concatenate([x[:, BLOCK * j:BLOCK * (j + 1)] for x in xs for j in range(GROUP)], axis=0)


def _attn_tables(sink_ref, bias_s, sink_s):
    kk = lax.broadcasted_iota(jnp.int32, (BLOCK, BLOCK), 0)
    ii = lax.broadcasted_iota(jnp.int32, (BLOCK, BLOCK), 1)
    dist = jnp.where(kk <= ii, ii - kk, ii - kk + BLOCK).astype(F32)
    for h in range(N_HEADS):
        bias_s[:, BLOCK * h:BLOCK * (h + 1)] = dist * -(2.0 ** -(h + 1))
        sink_s[:, BLOCK * h:BLOCK * (h + 1)] = jnp.zeros((1, BLOCK), F32) + sink_ref[0, h]


def _attn_masks(b):
    kk = lax.broadcasted_iota(jnp.int32, (BLOCK, GB), 0)
    ii = lax.broadcasted_iota(jnp.int32, (BLOCK, GB), 1) & (BLOCK - 1)
    sel = kk <= ii
    pen = jnp.where(sel, jnp.where(b >= 1, 0.0, NEG), jnp.where(b >= 2, 0.0, NEG))
    mj = lax.broadcasted_iota(jnp.int32, (N_META, GB), 0)
    mi = lax.broadcasted_iota(jnp.int32, (N_META, GB), 1) & (BLOCK - 1)
    pen_m = jnp.where((mj + LEAD) <= (mi + b * BLOCK), 0.0, NEG)
    return sel, pen, pen_m


def _attn_scores(qt, kc, kp, km, sel, pen, pen_m, bias):
    s_b = jnp.where(sel, _dot(kc, qt, NN), _dot(kp, qt, NN)) + bias + pen
    s_m = _dot(km, qt, NN) + pen_m
    return s_b, s_m


def _attn_fwd(q, kv, sinks, shards):
    R = q.shape[0]
    nb = R // BLOCK
    ns = len(shards)

    def body(sink_ref, q_ref, kvc_ref, kvp_ref, kvm_ref, *refs):
        ag_ins, (o_ref, lse_ref), ag_outs = refs[:ns], refs[ns:ns + 2], refs[ns + 2:2 * ns + 2]
        ag_sems, (bias_s, sink_s) = refs[2 * ns + 2:2 * ns + 5], refs[2 * ns + 5:]
        b = pl.program_id(0)
        ag_finish = _carried_allgather(b, nb, shards, ag_ins + ag_outs + ag_sems)
        pl.when(b == 0)(functools.partial(_attn_tables, sink_ref, bias_s, sink_s))
        sel, pen, pen_m = _attn_masks(b)
        q_t = (q_ref[...] * ATTN_SCALE).T
        kvc_t, kvp_t = kvc_ref[...].T, kvp_ref[...].T
        outs = []
        for g in range(N_HEADS // GROUP):
            ks, vs = slice(HEAD_DIM * g, HEAD_DIM * (g + 1)), slice(KV_W + HEAD_DIM * g, KV_W + HEAD_DIM * (g + 1))
            lanes = slice(GB * g, GB * (g + 1))
            s_b, s_m = _attn_scores(_group_lanes(q_t, g), kvc_ref[:, ks], kvp_ref[:, ks], kvm_ref[LEAD:BLOCK, ks],
                                    sel, pen, pen_m, bias_s[:, lanes])
            sink = sink_s[:, lanes]
            m = jnp.maximum(jnp.maximum(jnp.max(s_b, axis=0, keepdims=True),
                                        jnp.max(s_m, axis=0, keepdims=True)), sink)
            p_b = jnp.exp(s_b - m)
            p_m = jnp.exp(s_m - m)
            l = jnp.sum(p_b, axis=0, keepdims=True) + jnp.sum(p_m, axis=0, keepdims=True) + jnp.exp(sink - m)
            p_c = jnp.where(sel, p_b, 0.0).astype(BF16)
            p_p = jnp.where(sel, 0.0, p_b).astype(BF16)
            o_t = (_dot(kvc_t[vs, :], p_c, NN) + _dot(kvp_t[vs, :], p_p, NN)
                   + _dot(kvm_ref[LEAD:BLOCK, vs], p_m.astype(BF16), TN))
            outs.append(o_t / l)
            lse = m + jnp.log(l)
            for j in range(GROUP):
                lse_ref[GROUP * g + j:GROUP * g + j + 1, :] = lse[:, BLOCK * j:BLOCK * (j + 1)]
        o_ref[...] = _head_rows(outs).T
        ag_finish()

    res = _pcall(
        body, name="attn_fwd", grid=(nb,),
        in_specs=[pl.BlockSpec(memory_space=pltpu.SMEM),
                  _rows(BLOCK, 512), _rows(BLOCK, 256),
                  pl.BlockSpec((BLOCK, 256), lambda b: (jnp.maximum(b - 1, 0), 0)),
                  _full((BLOCK, 256))] + [ANY] * ns,
        out_specs=[_rows(BLOCK, 512), pl.BlockSpec((N_HEADS, BLOCK), lambda b: (0, b))] + [ANY] * ns,
        out_shape=[jax.ShapeDtypeStruct((R, 512), F32), jax.ShapeDtypeStruct((N_HEADS, R), F32)]
        + _gathered_shapes(shards),
        scratch_shapes=_allgather_sems(ns) + [pltpu.VMEM((BLOCK, N_HEADS * BLOCK), F32),
                                              pltpu.VMEM((1, N_HEADS * BLOCK), F32)],
        semantics=("arbitrary",),
    )(sinks, q, kv, kv, kv, *shards)
    return res[0], res[1], res[2:]


def _ln_silu(y, lg, lb):
    mu = jnp.mean(y, axis=-1, keepdims=True)
    xc = y - mu
    rstd = lax.rsqrt(jnp.mean(xc * xc, axis=-1, keepdims=True) + EPS)
    xhat = xc * rstd
    yn = xhat * lg + lb
    return yn, xhat, rstd


PHASE_ROWS = HALO + TM - 8


def _phase_scratch():
    return pltpu.VMEM((7, PHASE_ROWS, CONV_W), F32)


def _phase_copies(src_s, ph_s):
    for b in range(1, 8):
        ph_s[b - 1] = src_s[pl.ds(b, PHASE_ROWS), :]


def _shifted(src_s, ph_s, start, rows):
    a8, b = (start // 8) * 8, start % 8
    if b == 0:
        return src_s[pl.ds(a8, rows), :]
    return ph_s[b - 1, pl.ds(a8, rows), :]


def _conv_fwd(ca, cg, conv_w, conv_b, ln_g, ln_b, shards):
    R = ca.shape[0]
    nt = R // TM
    hpt = TM // HALO
    ns = len(shards)

    def body(ca_ref, cg_ref, cah_ref, cgh_ref, w_ref, b_ref, lg_ref, lb_ref, *refs):
        ag_ins, (oc_ref, y_ref), ag_outs = refs[:ns], refs[ns:ns + 2], refs[ns + 2:2 * ns + 2]
        ag_sems, (u_s, uph_s) = refs[2 * ns + 2:2 * ns + 5], refs[2 * ns + 5:]
        i = pl.program_id(0)
        ag_finish = _carried_allgather(i, nt, shards, ag_ins + ag_outs + ag_sems)
        u_s[HALO:HALO + TM, :] = ca_ref[...] * _sigmoid(cg_ref[...])
        u_s[0:HALO, :] = jnp.where(i > 0, cah_ref[...] * _sigmoid(cgh_ref[...]), 0.0)
        _phase_copies(u_s, uph_s)
        for rc in range(TM // CONV_ROWS):
            base = rc * CONV_ROWS + HALO - (CONV_K - 1)
            acc = jnp.zeros((CONV_ROWS, CONV_W), F32) + b_ref[...]
            for k in range(CONV_K):
                acc = acc + _shifted(u_s, uph_s, base + k, CONV_ROWS) * w_ref[k:k + 1, :]
            rows = slice(rc * CONV_ROWS, (rc + 1) * CONV_ROWS)
            y_ref[rows, :] = acc
            yn, _, _ = _ln_silu(acc, lg_ref[...], lb_ref[...])
            oc_ref[rows, :] = yn * _sigmoid(yn)
        ag_finish()

    prev_halo = pl.BlockSpec((HALO, CONV_W), lambda i: (jnp.maximum(i * hpt - 1, 0), 0))
    anywhere = pl.BlockSpec(memory_space=pl.ANY)
    res = _pcall(
        body, name="conv_fwd", grid=(nt,),
        in_specs=[_rows(TM, CONV_W), _rows(TM, CONV_W), prev_halo, prev_halo,
                  _full((CONV_K, CONV_W)), _full((1, CONV_W)), _full((1, CONV_W)), _full((1, CONV_W))]
        + [anywhere] * ns,
        out_specs=[_rows(TM, CONV_W), _rows(TM, CONV_W)] + [anywhere] * ns,
        out_shape=[jax.ShapeDtypeStruct((R, CONV_W), F32), jax.ShapeDtypeStruct((R, CONV_W), F32)]
        + _gathered_shapes(shards),
        scratch_shapes=_allgather_sems(ns) + [pltpu.VMEM((HALO + TM, CONV_W), F32), _phase_scratch()],
        semantics=("arbitrary",),
    )(ca, cg, ca, cg, conv_w, conv_b, ln_g, ln_b, *shards)
    return res[:2], res[2:]


def _outproj_fwd(h0, o_attn, o_conv, ga, gc, w_out):
    R = h0.shape[0]

    def body(h_ref, oa_ref, oc_ref, ga_ref, gc_ref, w_ref, h1_ref):
        oa, oc = oa_ref[...], oc_ref[...]
        ma = (oa * _rms_stats(oa) * ga_ref[...]).astype(BF16)
        mc = (oc * _rms_stats(oc) * gc_ref[...]).astype(BF16)
        h1_ref[...] = h_ref[...] + _dot(ma, w_ref[0:512, :], NN) + _dot(mc, w_ref[512:1024, :], NN)

    return _pcall(
        body, name="outproj_fwd", grid=(R // TM,),
        in_specs=[_rows(TM, D_MODEL), _rows(TM, 512), _rows(TM, 512), _full((1, 512)), _full((1, 512)),
                  _full((D_MODEL, D_MODEL))],
        out_specs=_rows(TM, D_MODEL),
        out_shape=jax.ShapeDtypeStruct((R, D_MODEL), F32),
        semantics=("parallel",),
    )(h0, o_attn, o_conv, ga, gc, w_out)


def _target_copy(tgt_hbm, tgt_s, sem, i, first):
    if first:
        return pltpu.make_async_copy(tgt_hbm.at[pl.ds(0, TM - BLOCK)], tgt_s.at[pl.ds(BLOCK, TM - BLOCK)], sem)
    return pltpu.make_async_copy(tgt_hbm.at[pl.ds(i * TM - BLOCK, TM)], tgt_s, sem)


def _resident(shape):
    nd = len(shape)
    return pl.BlockSpec(shape, lambda *_: (0,) * nd, pipeline_mode=pl.Buffered(1))


def _ffn_fwd(h1, g2, wg_t, wu_t, wd, gf, target):
    R = h1.shape[0]
    nt = R // TM

    def body(h1_ref, g2_ref, wg_ref, wu_ref, wd_ref, gf_ref, tgt_hbm,
             gate_ref, up_ref, dh2_ref, loss_ref, dgf_ref, tgt_s, sem, act_s):
        i = pl.program_id(0)

        @pl.when(i == 0)
        def _():
            loss_ref[...] = jnp.zeros_like(loss_ref)
            dgf_ref[...] = jnp.zeros_like(dgf_ref)
            tgt_s[0:BLOCK, :] = jnp.zeros((BLOCK, D_MODEL), F32)
            _target_copy(tgt_hbm, tgt_s, sem, i, True).start()

        pl.when(i > 0)(lambda: _target_copy(tgt_hbm, tgt_s, sem, i, False).start())
        h1 = h1_ref[...]
        hn = (h1 * _rms_stats(h1) * g2_ref[...]).astype(BF16)
        for cs in FF_SUB:
            gate = _dot(hn, wg_ref[cs, :], NT)
            up = _dot(hn, wu_ref[cs, :], NT)
            gate_ref[:, cs] = gate.astype(BF16)
            up_ref[:, cs] = up.astype(BF16)
            act_s[:, cs] = (gate * _sigmoid(gate) * up).astype(BF16)
        part = _dot(act_s[...], wd_ref[...], NN)
        pl.when(i == 0)(lambda: _target_copy(tgt_hbm, tgt_s, sem, i, True).wait())
        pl.when(i > 0)(lambda: _target_copy(tgt_hbm, tgt_s, sem, i, False).wait())
        h2 = h1 + part
        rf = _rms_stats(h2)
        gf = gf_ref[...]
        row = lax.broadcasted_iota(jnp.int32, (TM, 1), 0) + i * TM
        err = jnp.where(row >= BLOCK, h2 * rf * gf - tgt_s[...], 0.0)
        dy = err * (1.0 / D_MODEL)
        dh2, dgf = _rms_bwd(dy, h2, rf, gf)
        dh2_ref[...] = dh2
        loss_ref[...] += (0.5 / D_MODEL) * jnp.sum(err * err)
        dgf_ref[...] += dgf

    wspec = _resident((D_FF, D_MODEL))
    return _pcall(
        body, name="ffn_fwd", grid=(nt,),
        in_specs=[_rows(TM, D_MODEL), _full((1, D_MODEL)), wspec, wspec, wspec, _full((1, D_MODEL)),
                  pl.BlockSpec(memory_space=pl.ANY)],
        out_specs=[_rows(TM, D_FF), _rows(TM, D_FF), _rows(TM, D_MODEL), _full((8, 128)), _full((1, D_MODEL))],
        out_shape=[jax.ShapeDtypeStruct((R, D_FF), BF16), jax.ShapeDtypeStruct((R, D_FF), BF16),
                   jax.ShapeDtypeStruct((R, D_MODEL), F32),
                   jax.ShapeDtypeStruct((8, 128), F32), jax.ShapeDtypeStruct((1, D_MODEL), F32)],
        scratch_shapes=[pltpu.VMEM((TM, D_MODEL), F32), pltpu.SemaphoreType.DMA, pltpu.VMEM((TM, D_FF), BF16)],
        semantics=("arbitrary",),
    )(h1, g2, wg_t, wu_t, wd, gf, target)


def _ffn_bwd(dh2, h1, g2, gate, up, wg_t, wu_t, wd):
    R = h1.shape[0]
    nt, nj = R // TM, D_FF // FF_CHUNK

    wspec = pl.BlockSpec((FF_CHUNK, D_MODEL), lambda i, j: (j, 0))
    aspec = pl.BlockSpec((TM, FF_CHUNK), lambda i, j: (i, j))
    act_shape = jax.ShapeDtypeStruct((R, D_FF), BF16)

    def act_body(dh2_ref, gate_ref, up_ref, wd_ref, dgate_ref, dup_ref, act_ref, dhb_s):
        @pl.when(pl.program_id(1) == 0)
        def _():
            dhb_s[...] = dh2_ref[...].astype(BF16)

        dhb = dhb_s[...]
        for cs in CHUNK_SUB:
            dact = _dot(dhb, wd_ref[cs, :], NT)
            gate = gate_ref[:, cs].astype(F32)
            up = up_ref[:, cs].astype(F32)
            sig = _sigmoid(gate)
            silu = gate * sig
            dgate_ref[:, cs] = (dact * up * (sig * (1.0 + gate * (1.0 - sig)))).astype(BF16)
            dup_ref[:, cs] = (dact * silu).astype(BF16)
            act_ref[:, cs] = (silu * up).astype(BF16)

    dgate, dup, act = _pcall(
        act_body, name="ffn_bwd_act", grid=(nt, nj),
        in_specs=[_rows(TM, D_MODEL), aspec, aspec, wspec],
        out_specs=[aspec, aspec, aspec], out_shape=[act_shape, act_shape, act_shape],
        scratch_shapes=[pltpu.VMEM((TM, D_MODEL), BF16)],
        semantics=("parallel", "arbitrary"),
    )(dh2, gate, up, wd)

    def in_body(dh2_ref, h1_ref, g2_ref, dgate_ref, dup_ref, wg_ref, wu_ref, hn_ref, dh1_ref, dg2_ref):
        @pl.when(pl.program_id(0) == 0)
        def _():
            dg2_ref[...] = jnp.zeros_like(dg2_ref)

        dhn = _dot(dgate_ref[...], wg_ref[...], NN) + _dot(dup_ref[...], wu_ref[...], NN)
        h1 = h1_ref[...]
        r = _rms_stats(h1)
        g2 = g2_ref[...]
        hn_ref[...] = (h1 * r * g2).astype(BF16)
        dx, dg = _rms_bwd(dhn, h1, r, g2)
        dh1_ref[...] = dh2_ref[...] + dx
        dg2_ref[...] += dg

    hn2, dh1, dg2 = _pcall(
        in_body, name="ffn_bwd_in", grid=(nt,),
        in_specs=[_rows(TM, D_MODEL), _rows(TM, D_MODEL), _full((1, D_MODEL)), _rows(TM, D_FF), _rows(TM, D_FF),
                  _resident((D_FF, D_MODEL)), _resident((D_FF, D_MODEL))],
        out_specs=[_rows(TM, D_MODEL), _rows(TM, D_MODEL), _full((1, D_MODEL))],
        out_shape=[jax.ShapeDtypeStruct((R, D_MODEL), BF16), jax.ShapeDtypeStruct((R, D_MODEL), F32),
                   jax.ShapeDtypeStruct((1, D_MODEL), F32)],
        semantics=("arbitrary",),
    )(dh2, h1, g2, dgate, dup, wg_t, wu_t)
    return dgate, dup, act, hn2, dh1, dg2


def _wgrad(a, b, tm, name, partials=()):
    K, M = a.shape
    N = b.shape[1]
    tk = K // WGRAD_K_TILES if K % (WGRAD_K_TILES * BLOCK) == 0 else TM
    nm, nk, npart = M // tm, K // tk, len(partials)

    def body(a_ref, b_ref, *refs):
        p_ins, o_ref, p_outs, sems = refs[:npart], refs[npart], refs[npart + 1:2 * npart + 1], refs[2 * npart + 1:]
        step = pl.program_id(0) * nk + pl.program_id(1)
        exchange = functools.partial(_chip_copies, p_ins, p_outs, *sems)
        if npart:
            _hosted(step, nm * nk, exchange)

        @pl.when(pl.program_id(1) == 0)
        def _():
            o_ref[...] = jnp.zeros_like(o_ref)

        o_ref[...] += _dot(a_ref[...], b_ref[...].astype(BF16), TN)
        if npart:
            _hosted_wait(step, nm * nk, exchange)

    res = _pcall(
        body, name=name, grid=(nm, nk),
        in_specs=[pl.BlockSpec((tk, tm), lambda m, k: (k, m)), pl.BlockSpec((tk, N), lambda m, k: (k, 0))]
        + [ANY] * npart,
        out_specs=[pl.BlockSpec((tm, N), lambda m, k: (m, 0))] + [ANY] * npart,
        out_shape=[jax.ShapeDtypeStruct((M, N), F32)] + _chip_shapes(partials),
        scratch_shapes=_sem_pair(3 * npart) if npart else [],
        semantics=("arbitrary", "arbitrary"),
    )(a, b, *partials)
    return (res[0], res[1:]) if npart else res[0]


def _outproj_bwd(dh1, o_attn, o_conv, ga, gc, w_out, grads):
    R = dh1.shape[0]
    nt, ng = R // TM, len(grads)

    def body(dh1_ref, oa_ref, oc_ref, ga_ref, gc_ref, w_ref, *refs):
        g_ins, (doa_ref, doc_ref, mixed_ref, dga_ref, dgc_ref) = refs[:ng], refs[ng:ng + 5]
        g_outs, (send_sems, recv_sems) = refs[ng + 5:2 * ng + 5], refs[2 * ng + 5:]
        exchange = functools.partial(_sibling_copies, g_ins, g_outs, send_sems, recv_sems)
        _hosted(pl.program_id(0), nt, exchange)

        @pl.when(pl.program_id(0) == 0)
        def _():
            dga_ref[...] = jnp.zeros_like(dga_ref)
            dgc_ref[...] = jnp.zeros_like(dgc_ref)

        dm = _dot(dh1_ref[...].astype(BF16), w_ref[...], NT)
        oa, oc = oa_ref[...], oc_ref[...]
        ra, rc = _rms_stats(oa), _rms_stats(oc)
        mixed_ref[:, 0:512] = (oa * ra * ga_ref[...]).astype(BF16)
        mixed_ref[:, 512:1024] = (oc * rc * gc_ref[...]).astype(BF16)
        doa, dga = _rms_bwd(dm[:, 0:512], oa, ra, ga_ref[...])
        doc, dgc = _rms_bwd(dm[:, 512:1024], oc, rc, gc_ref[...])
        doa_ref[...] = doa
        doc_ref[...] = doc
        dga_ref[...] += dga
        dgc_ref[...] += dgc
        _hosted_wait(pl.program_id(0), nt, exchange)

    res = _pcall(
        body, name="outproj_bwd", grid=(nt,),
        in_specs=[_rows(TM, D_MODEL), _rows(TM, 512), _rows(TM, 512), _full((1, 512)), _full((1, 512)),
                  _full((D_MODEL, D_MODEL))] + [ANY] * ng,
        out_specs=[_rows(TM, 512), _rows(TM, 512), _rows(TM, D_MODEL), _full((1, 512)), _full((1, 512))]
        + [ANY] * ng,
        out_shape=[jax.ShapeDtypeStruct((R, 512), F32), jax.ShapeDtypeStruct((R, 512), F32),
                   jax.ShapeDtypeStruct((R, D_MODEL), BF16),
                   jax.ShapeDtypeStruct((1, 512), F32), jax.ShapeDtypeStruct((1, 512), F32)]
        + _sibling_shapes(grads),
        scratch_shapes=_sem_pair(ng),
        semantics=("arbitrary",),
    )(dh1, o_attn, o_conv, ga, gc, w_out, *grads)
    return res[:5], res[5:]


def _conv_bwd(do_conv, y, ca, cg, conv_w, ln_g, ln_b, partials):
    R = ca.shape[0]
    nt = R // TM
    hpt = TM // HALO
    npart = len(partials)

    def body(do_ref, doh_ref, y_ref, yh_ref, ca_ref, cg_ref, cah_ref, cgh_ref, w_ref, lg_ref, lb_ref, *refs):
        p_ins, (dca_ref, dcg_ref, dw_ref, db_ref, dlg_ref, dlb_ref) = refs[:npart], refs[npart:npart + 6]
        p_outs, (send_sems, recv_sems, u_s, dy_s, uph_s, dyph_s) = refs[npart + 6:2 * npart + 6], refs[2 * npart + 6:]
        i = pl.program_id(0)
        exchange = functools.partial(_chip_copies, p_ins, p_outs, send_sems, recv_sems)
        _hosted(i, nt, exchange)

        @pl.when(i == 0)
        def _():
            dw_ref[...] = jnp.zeros_like(dw_ref)
            db_ref[...] = jnp.zeros_like(db_ref)
            dlg_ref[...] = jnp.zeros_like(dlg_ref)
            dlb_ref[...] = jnp.zeros_like(dlb_ref)

        lg, lb = lg_ref[...], lb_ref[...]

        def ln_bwd(yv, dov):
            yn, xhat, rstd = _ln_silu(yv, lg, lb)
            sig = _sigmoid(yn)
            dyn = dov * (sig * (1.0 + yn * (1.0 - sig)))
            dxh = dyn * lg
            dyv = rstd * (dxh - jnp.mean(dxh, axis=-1, keepdims=True)
                          - xhat * jnp.mean(dxh * xhat, axis=-1, keepdims=True))
            return dyv, dyn, xhat

        dyv, dyn, xhat = ln_bwd(y_ref[...], do_ref[...])
        dy_s[0:TM, :] = dyv
        dlg_ref[...] += jnp.sum(dyn * xhat, axis=0, keepdims=True)
        dlb_ref[...] += jnp.sum(dyn, axis=0, keepdims=True)
        db_ref[...] += jnp.sum(dyv, axis=0, keepdims=True)
        dyh, _, _ = ln_bwd(yh_ref[...], doh_ref[...])
        dy_s[TM:TM + HALO, :] = jnp.where(i < nt - 1, dyh, 0.0)
        u_s[HALO:HALO + TM, :] = ca_ref[...] * _sigmoid(cg_ref[...])
        u_s[0:HALO, :] = jnp.where(i > 0, cah_ref[...] * _sigmoid(cgh_ref[...]), 0.0)
        _phase_copies(dy_s, dyph_s)
        _phase_copies(u_s, uph_s)

        for rc in range(TM // CONV_ROWS):
            acc = jnp.zeros((CONV_ROWS, CONV_W), F32)
            for k in range(CONV_K):
                acc = acc + _shifted(dy_s, dyph_s, rc * CONV_ROWS + CONV_K - 1 - k, CONV_ROWS) * w_ref[k:k + 1, :]
            rows = slice(rc * CONV_ROWS, (rc + 1) * CONV_ROWS)
            sg = _sigmoid(cg_ref[rows, :])
            dca_ref[rows, :] = (acc * sg).astype(BF16)
            dcg_ref[rows, :] = (acc * ca_ref[rows, :] * sg * (1.0 - sg)).astype(BF16)

        for k in range(CONV_K):
            prod = _shifted(u_s, uph_s, HALO - (CONV_K - 1) + k, TM) * dy_s[0:TM, :]
            dw_ref[k:k + 1, :] += jnp.sum(prod, axis=0, keepdims=True)
        _hosted_wait(i, nt, exchange)

    prev_halo = pl.BlockSpec((HALO, CONV_W), lambda i: (jnp.maximum(i * hpt - 1, 0), 0))
    next_halo = pl.BlockSpec((HALO, CONV_W), lambda i: (jnp.minimum((i + 1) * hpt, nt * hpt - 1), 0))
    vec = jax.ShapeDtypeStruct((1, CONV_W), F32)
    res = _pcall(
        body, name="conv_bwd", grid=(nt,),
        in_specs=[_rows(TM, CONV_W), next_halo, _rows(TM, CONV_W), next_halo,
                  _rows(TM, CONV_W), _rows(TM, CONV_W), prev_halo, prev_halo,
                  _full((CONV_K, CONV_W)), _full((1, CONV_W)), _full((1, CONV_W))] + [ANY] * npart,
        out_specs=[_rows(TM, CONV_W), _rows(TM, CONV_W), _full((32, CONV_W)),
                   _full((1, CONV_W)), _full((1, CONV_W)), _full((1, CONV_W))] + [ANY] * npart,
        out_shape=[jax.ShapeDtypeStruct((R, CONV_W), BF16), jax.ShapeDtypeStruct((R, CONV_W), BF16),
                   jax.ShapeDtypeStruct((32, CONV_W), F32), vec, vec, vec] + _chip_shapes(partials),
        scratch_shapes=_sem_pair(3 * npart)
        + [pltpu.VMEM((HALO + TM, CONV_W), F32), pltpu.VMEM((TM + HALO, CONV_W), F32),
           _phase_scratch(), _phase_scratch()],
        semantics=("arbitrary",),
    )(do_conv, do_conv, y, y, ca, cg, ca, cg, conv_w, ln_g, ln_b, *partials)
    return res[:6], res[6:]


def _attn_bwd(q, kv, sinks, o, lse, do, grads, partials):
    R = q.shape[0]
    nb = R // BLOCK
    ng, npart = len(grads), len(partials)
    nx = ng + npart

    def body(sink_ref, q_ref, kvc_ref, kvp_ref, kvm_ref, o_ref, lse_ref, do_ref, *refs):
        x_ins, (dq_ref, dkv_ref, dkvm_ref, dsink_ref) = refs[:nx], refs[nx:nx + 4]
        x_outs = refs[nx + 4:2 * nx + 4]
        g_send, g_recv, p_send, p_recv, carry_s, cur_s, prev_s, bias_s, sink_s, delta_s = refs[2 * nx + 4:]
        b = pl.program_id(0)

        def exchange():
            return (_sibling_copies(x_ins[:ng], x_outs[:ng], g_send, g_recv)
                    + _chip_copies(x_ins[ng:], x_outs[ng:], p_send, p_recv))

        _hosted(b, nb + 1, exchange)

        @pl.when(b == 0)
        def _():
            dkvm_ref[...] = jnp.zeros_like(dkvm_ref)
            carry_s[...] = jnp.zeros_like(carry_s)
            for h in range(N_HEADS):
                dsink_ref[0, h] = 0.0
            _attn_tables(sink_ref, bias_s, sink_s)

        @pl.when(b < nb)
        def _():
            sel, pen, pen_m = _attn_masks(b)
            q_t = (q_ref[...] * ATTN_SCALE).T
            do_t = do_ref[...].astype(BF16).T
            kvc_t, kvp_t = kvc_ref[...].T, kvp_ref[...].T
            prod = do_ref[...] * o_ref[...]
            hi = prod.astype(BF16)
            lo = (prod - hi.astype(F32)).astype(BF16)
            head_of = lax.broadcasted_iota(jnp.int32, (N_HEADS, ATTN_W), 1) // HEAD_DIM
            ind = (head_of == lax.broadcasted_iota(jnp.int32, (N_HEADS, ATTN_W), 0)).astype(BF16)
            delta_s[...] = _dot(ind, hi, NT) + _dot(ind, lo, NT)
            dqs = []
            for g in range(N_HEADS // GROUP):
                ks, vs = slice(HEAD_DIM * g, HEAD_DIM * (g + 1)), slice(KV_W + HEAD_DIM * g, KV_W + HEAD_DIM * (g + 1))
                lanes = slice(GB * g, GB * (g + 1))
                qg, dog = _group_lanes(q_t, g), _group_lanes(do_t, g)
                kc, kp, km = kvc_ref[:, ks], kvp_ref[:, ks], kvm_ref[LEAD:BLOCK, ks]
                vc, vp, vm = kvc_ref[:, vs], kvp_ref[:, vs], kvm_ref[LEAD:BLOCK, vs]
                s_b, s_m = _attn_scores(qg, kc, kp, km, sel, pen, pen_m, bias_s[:, lanes])
                lse, delta = _head_lanes(lse_ref, g), _head_lanes(delta_s, g)
                p_b = jnp.exp(s_b - lse)
                p_m = jnp.exp(s_m - lse)
                dp_b = jnp.where(sel, _dot(vc, dog, NN), _dot(vp, dog, NN))
                ds_b = p_b * (dp_b - delta)
                ds_m = (p_m * (_dot(vm, dog, NN) - delta)).astype(BF16)
                dsk = jnp.exp(sink_s[:, lanes] - lse) * delta
                for j in range(GROUP):
                    dsink_ref[0, GROUP * g + j] += -jnp.sum(dsk[:, BLOCK * j:BLOCK * (j + 1)])
                ds_c = jnp.where(sel, ds_b, 0.0).astype(BF16)
                ds_p = jnp.where(sel, 0.0, ds_b).astype(BF16)
                p_c = jnp.where(sel, p_b, 0.0).astype(BF16)
                p_p = jnp.where(sel, 0.0, p_b).astype(BF16)
                dqs.append((_dot(kvc_t[ks, :], ds_c, NN) + _dot(kvp_t[ks, :], ds_p, NN)
                            + _dot(km, ds_m, TN)) * ATTN_SCALE)
                cur_s[:, ks] = _dot(ds_c, qg, NT)
                cur_s[:, vs] = _dot(p_c, dog, NT)
                prev_s[:, ks] = _dot(ds_p, qg, NT)
                prev_s[:, vs] = _dot(p_p, dog, NT)
                dkvm_ref[:, ks] += _dot(ds_m, qg, NT)
                dkvm_ref[:, vs] += _dot(p_m.astype(BF16), dog, NT)
            dq_ref[...] = _head_rows(dqs).astype(BF16).T
            dkv_ref[...] = (carry_s[...] + prev_s[...]).astype(BF16)
            carry_s[...] = cur_s[...]

        @pl.when(b == nb)
        def _():
            dkv_ref[...] = carry_s[...].astype(BF16)

        _hosted_wait(b, nb + 1, exchange)

    def at(off):
        return lambda b: (jnp.clip(b + off, 0, nb - 1), 0)

    blk = lambda cols, off=0: pl.BlockSpec((BLOCK, cols), at(off))
    res = _pcall(
        body, name="attn_bwd", grid=(nb + 1,),
        in_specs=[pl.BlockSpec(memory_space=pltpu.SMEM), blk(512), blk(256), blk(256, -1), _full((BLOCK, 256)),
                  blk(512), pl.BlockSpec((N_HEADS, BLOCK), lambda b: (0, jnp.minimum(b, nb - 1))), blk(512)]
        + [ANY] * nx,
        out_specs=[blk(512), blk(256, -1), _full((N_META, 256)), pl.BlockSpec(memory_space=pltpu.SMEM)]
        + [ANY] * nx,
        out_shape=[jax.ShapeDtypeStruct((R, 512), BF16), jax.ShapeDtypeStruct((R, 256), BF16),
                   jax.ShapeDtypeStruct((N_META, 256), F32), jax.ShapeDtypeStruct((1, N_HEADS), F32)]
        + _sibling_shapes(grads) + _chip_shapes(partials),
        scratch_shapes=_sem_pair(ng) + _sem_pair(3 * npart) + [pltpu.VMEM((BLOCK, 256), F32)] * 3
        + [pltpu.VMEM((BLOCK, N_HEADS * BLOCK), F32), pltpu.VMEM((1, N_HEADS * BLOCK), F32),
           pltpu.VMEM((N_HEADS, BLOCK), F32)],
        semantics=("arbitrary",),
    )(sinks, q, kv, kv, kv, o, lse, do, *grads, *partials)
    return res[:4], res[4:4 + ng], res[4 + ng:]


def _inproj_bwd(dh1, h0, g1, dq, dkv, dkvm, dca, dcg, w_in_t):
    R = h0.shape[0]
    nt = R // TM
    assert nt >= 2

    def body(dh1_ref, h0_ref, g_ref, dq_ref, dkv_ref, dkvm_ref, dca_ref, dcg_ref, w_ref,
             gx_hbm, dmeta_ref, dproj_ref, hn_ref, dg_ref, dx_s, gx_sems):
        i = pl.program_id(0)

        def gx_copy(step, slot, first):
            if first:
                return pltpu.make_async_copy(dx_s.at[slot, pl.ds(BLOCK, TM - BLOCK)],
                                             gx_hbm.at[pl.ds(0, TM - BLOCK)], gx_sems.at[slot])
            return pltpu.make_async_copy(
                dx_s.at[slot], gx_hbm.at[pl.ds(pl.multiple_of(step * TM - BLOCK, BLOCK), TM)], gx_sems.at[slot])

        @pl.when(i == 0)
        def _():
            dg_ref[...] = jnp.zeros_like(dg_ref)

        dproj_ref[:, 0:512] = dq_ref[...]
        dproj_ref[:, 512:768] = dkv_ref[...]
        dproj_ref[:, 768:1280] = dca_ref[...]
        dproj_ref[:, 1280:1792] = dcg_ref[...]

        @pl.when(i == 0)
        def _():
            dproj_ref[LEAD:BLOCK, 512:768] = dkvm_ref[...].astype(BF16)

        dhn = _dot(dproj_ref[...], w_ref[...], NN)
        h = h0_ref[...]
        r = _rms_stats(h)
        g = g_ref[...]
        hn_ref[...] = (h * r * g).astype(BF16)
        dx, dg = _rms_bwd(dhn, h, r, g)
        dg_ref[...] += dg
        slot = i % 2
        pl.when(i == 2)(lambda: gx_copy(0, 0, True).wait())
        pl.when(i > 2)(lambda: gx_copy(i - 2, slot, False).wait())
        dx_s[slot] = dh1_ref[...] + dx

        @pl.when(i == 0)
        def _():
            dmeta_ref[...] = dx_s[0, LEAD:BLOCK, :]
            gx_copy(0, 0, True).start()

        pl.when(i > 0)(lambda: gx_copy(i, slot, False).start())

        @pl.when(i == nt - 1)
        def _():
            gx_copy(nt - 2, (nt - 2) % 2, nt == 2).wait()
            gx_copy(nt - 1, (nt - 1) % 2, False).wait()

    return _pcall(
        body, name="inproj_bwd", grid=(nt,),
        in_specs=[_rows(TM, D_MODEL), _rows(TM, D_MODEL), _full((1, D_MODEL)), _rows(TM, 512), _rows(TM, 256),
                  _full((N_META, 256)), _rows(TM, 512), _rows(TM, 512), _full((1792, D_MODEL))],
        out_specs=[ANY, _full((N_META, D_MODEL)), _rows(TM, 1792), _rows(TM, D_MODEL), _full((1, D_MODEL))],
        out_shape=[jax.ShapeDtypeStruct((R - BLOCK, D_MODEL), F32), jax.ShapeDtypeStruct((N_META, D_MODEL), F32),
                   jax.ShapeDtypeStruct((R, 1792), BF16),
                   jax.ShapeDtypeStruct((R, D_MODEL), BF16), jax.ShapeDtypeStruct((1, D_MODEL), F32)],
        scratch_shapes=[pltpu.VMEM((2, TM, D_MODEL), F32), pltpu.SemaphoreType.DMA((2,))],
        semantics=("arbitrary",),
    )(dh1, h0, g1, dq, dkv, dkvm, dca, dcg, w_in_t)


ANY = pl.BlockSpec(memory_space=pl.ANY)


def _position():
    return lax.axis_index("x"), lax.axis_index("y"), lax.axis_index("c")


def _device_number(p):
    return 4 * p[0] + 2 * p[1] + p[2]


def _two_level_allgather(ins, outs, block, send_sems, recv_sems, local_sems, sem_base=0):
    n = len(ins)
    x, y, c = _position()
    me, sibling = (x, y, c), (x, y, 1 - c)
    chips = [(1 - x, y), (x, 1 - y), (1 - x, 1 - y)]

    def copy(w, k, origin, to, src=None):
        return pltpu.make_async_remote_copy(
            src_ref=block(w, origin) if src is None else src, dst_ref=block(w, origin),
            send_sem=send_sems.at[sem_base + 7 * w + k], recv_sem=recv_sems.at[sem_base + 7 * w + k],
            device_id=to, device_id_type=MESH)

    def mine(w):
        return pltpu.make_async_copy(ins[w], block(w, me), local_sems.at[w])

    def own(w):
        return [copy(w, 0, me, sibling, src=ins[w])] + [
            copy(w, 1 + j, me, (*chip, c), src=ins[w]) for j, chip in enumerate(chips)]

    def passed(w):
        return [copy(w, 4 + j, (*chip, c), sibling) for j, chip in enumerate(chips)]

    def start():
        for w in range(n):
            mine(w).start()
        for w in range(n):
            for cp in own(w):
                cp.start()

    def forward(w):
        fw = passed(w)
        for j, chip in enumerate(chips):
            copy(w, 1 + j, (*chip, c), me).wait_recv()
            fw[j].start()

    def finish():
        for w in range(n):
            copy(w, 0, sibling, me).wait_recv()
            for j, chip in enumerate(chips):
                copy(w, 4 + j, (*chip, 1 - c), me).wait_recv()
        for w in range(n):
            for cp in own(w) + passed(w):
                cp.wait_send()
            mine(w).wait()

    return start, forward, finish


def _carried_allgather(step, n_steps, shards, refs):
    ns = len(shards)
    ins, outs, (send_sems, recv_sems, local_sems) = refs[:ns], refs[ns:2 * ns], refs[2 * ns:]
    start, forward, finish = _two_level_allgather(
        ins, outs, _row_block(outs, [s.shape[0] for s in shards]), send_sems, recv_sems, local_sems)
    pl.when(step == 0)(start)
    total = sum(s.shape[0] for s in shards)
    sent = 0
    for w, s in enumerate(shards):
        sent += s.shape[0]
        pl.when(step == (AG_FORWARD_AT * sent * (n_steps - 1)) // (100 * total))(functools.partial(forward, w))
    return lambda: pl.when(step == n_steps - 1)(finish)


def _gathered_shapes(shards):
    return [jax.ShapeDtypeStruct((N_DEV * s.shape[0], s.shape[1]), s.dtype) for s in shards]


def _allgather_sems(ns):
    return _sem_pair(7 * ns) + [pltpu.SemaphoreType.DMA((ns,))]


def _blocking_allgather(ins, outs, block, send_sems, recv_sems, local_sems, sem_base=0):
    start, forward, finish = _two_level_allgather(ins, outs, block, send_sems, recv_sems, local_sems, sem_base)
    start()
    for w in range(len(ins)):
        forward(w)
    finish()


def _row_block(outs, rows):
    def block(w, p):
        return outs[w].at[pl.ds(pl.multiple_of(_device_number(p) * rows[w], 16), rows[w])]
    return block


def _sibling_copies(ins, outs, send_sems, recv_sems):
    x, y, c = _position()
    return [pltpu.make_async_remote_copy(
        src_ref=ins[w].at[:, 1 - c], dst_ref=outs[w], send_sem=send_sems.at[w], recv_sem=recv_sems.at[w],
        device_id=(x, y, 1 - c), device_id_type=MESH) for w in range(len(ins))]


def _chip_copies(ins, outs, send_sems, recv_sems):
    x, y, c = _position()
    chips = [(1 - x, y), (x, 1 - y), (1 - x, 1 - y)]
    return [pltpu.make_async_remote_copy(
        src_ref=ins[w].at[2 * chip[0] + chip[1]], dst_ref=outs[w].at[k],
        send_sem=send_sems.at[3 * w + k], recv_sem=recv_sems.at[3 * w + k],
        device_id=(*chip, c), device_id_type=MESH) for w in range(len(ins)) for k, chip in enumerate(chips)]


def _hosted(step, n_steps, make_copies):
    @pl.when(step == 0)
    def _():
        for cp in make_copies():
            cp.start()


def _hosted_wait(step, n_steps, make_copies):
    @pl.when(step == n_steps - 1)
    def _():
        for cp in make_copies():
            cp.wait()


def _sem_pair(n):
    return [pltpu.SemaphoreType.DMA((n,)), pltpu.SemaphoreType.DMA((n,))]


def _allgather_params(shards, small):
    arrays = list(shards) + list(small)
    n, ns = len(arrays), len(shards)

    def body(*refs):
        ins, outs = refs[:n], refs[n:2 * n]
        send_sems, recv_sems, local_sems = refs[2 * n:]

        rows = _row_block(outs, [a.shape[0] for a in arrays])

        def block(w, p):
            return rows(w, p) if w < ns else outs[w].at[_device_number(p)]

        _blocking_allgather(ins, outs, block, send_sems, recv_sems, local_sems)

    out_shape = [jax.ShapeDtypeStruct((N_DEV * a.shape[0], a.shape[1]), a.dtype) for a in shards]
    out_shape += [jax.ShapeDtypeStruct((N_DEV,) + a.shape, a.dtype) for a in small]
    return _pcall(
        body, name="allgather_params", in_specs=[ANY] * n, out_specs=[ANY] * n, out_shape=out_shape,
        scratch_shapes=[pltpu.SemaphoreType.DMA((7 * n,)), pltpu.SemaphoreType.DMA((7 * n,)),
                        pltpu.SemaphoreType.DMA((n,))],
    )(*arrays)


def _reduce_siblings(grads):
    n = len(grads)

    def body(*refs):
        ins, outs = refs[:n], refs[n:2 * n]
        send_sems, recv_sems = refs[2 * n:]
        copies = _sibling_copies(ins, outs, send_sems, recv_sems)
        for cp in copies:
            cp.start()
        for cp in copies:
            cp.wait()

    return _pcall(
        body, name="reduce_siblings", in_specs=[ANY] * n, out_specs=[ANY] * n,
        out_shape=_sibling_shapes(grads), scratch_shapes=_sem_pair(n),
    )(*grads)


def _sibling_shapes(grads):
    return [jax.ShapeDtypeStruct((4,) + g.shape[2:], F32) for g in grads]


def _chip_shapes(partials):
    return [jax.ShapeDtypeStruct((3,) + p.shape[1:], F32) for p in partials]


def _add_sibling(grad, received, core, name):
    _, _, r, cols = grad.shape

    def body(core_ref, g_ref, r_ref, o_ref):
        o_ref[...] = g_ref[...] + r_ref[...]

    return pl.pallas_call(
        body, name=name,
        grid_spec=pltpu.PrefetchScalarGridSpec(
            num_scalar_prefetch=1, grid=(4,),
            in_specs=[pl.BlockSpec((None, None, r, cols), lambda s, core_ref: (s, core_ref[0], 0, 0)),
                      pl.BlockSpec((None, r, cols), lambda s, core_ref: (s, 0, 0))],
            out_specs=pl.BlockSpec((None, r, cols), lambda s, core_ref: (s, 0, 0))),
        out_shape=jax.ShapeDtypeStruct((4, r, cols), F32),
        compiler_params=pltpu.CompilerParams(vmem_limit_bytes=VMEM_LIMIT),
    )(core, grad, received)


def _reduce_chips(partials, small):
    n, ns = len(partials), len(small)

    def body(*refs):
        p_ins, s_ins = refs[:n], refs[n:n + ns]
        p_outs, s_outs = refs[n + ns:2 * n + ns], refs[2 * n + ns:2 * (n + ns)]
        send_sems, recv_sems, local_sems = refs[2 * (n + ns):]
        copies = _chip_copies(p_ins, p_outs, send_sems, recv_sems)
        for cp in copies:
            cp.start()
        _blocking_allgather(s_ins, s_outs, lambda w, p: s_outs[w].at[_device_number(p)],
                            send_sems, recv_sems, local_sems, sem_base=3 * n)
        for cp in copies:
            cp.wait()

    out_shape = _chip_shapes(partials)
    out_shape += [jax.ShapeDtypeStruct((N_DEV,) + a.shape, a.dtype) for a in small]
    nsem = 3 * n + 7 * ns
    return _pcall(
        body, name="reduce_chips", in_specs=[ANY] * (n + ns), out_specs=[ANY] * (n + ns), out_shape=out_shape,
        scratch_shapes=[pltpu.SemaphoreType.DMA((nsem,)), pltpu.SemaphoreType.DMA((nsem,)),
                        pltpu.SemaphoreType.DMA((ns,))],
    )(*partials, *small)


def _adam(g, w, m, v):
    m = ADAM_B1 * m + (1.0 - ADAM_B1) * g
    v = ADAM_B2 * v + (1.0 - ADAM_B2) * (g * g)
    m_hat = m / (1.0 - ADAM_B1 ** ADAM_STEP)
    v_hat = v / (1.0 - ADAM_B2 ** ADAM_STEP)
    delta = -ADAM_LR * (m_hat / (jnp.sqrt(v_hat) + ADAM_EPS) + ADAM_WD * w)
    return delta, m, v


def _adamw(partial, received, slot, w, m, v, name):
    _, r, cols = partial.shape

    def body(slot_ref, p_ref, r_ref, w_ref, m_ref, v_ref, g_ref, d_ref, nm_ref, nv_ref):
        g = p_ref[...] + r_ref[0] + r_ref[1] + r_ref[2]
        g_ref[...] = g
        d_ref[...], nm_ref[...], nv_ref[...] = _adam(g, w_ref[...], m_ref[...], v_ref[...])

    whole = pl.BlockSpec((r, cols), lambda i, slot_ref: (0, 0))
    out = jax.ShapeDtypeStruct((r, cols), F32)
    return pl.pallas_call(
        body, name=name,
        grid_spec=pltpu.PrefetchScalarGridSpec(
            num_scalar_prefetch=1, grid=(1,),
            in_specs=[pl.BlockSpec((None, r, cols), lambda i, slot_ref: (slot_ref[0], 0, 0)),
                      pl.BlockSpec((3, r, cols), lambda i, slot_ref: (0, 0, 0)), whole, whole, whole],
            out_specs=[whole, whole, whole, whole]),
        out_shape=[out, out, out, out],
        compiler_params=pltpu.CompilerParams(vmem_limit_bytes=VMEM_LIMIT),
    )(slot, partial, received, w, m, v)


def _adamw_small(dev, ga, gb, gc, params):
    names = ["meta", "attn_norm", "sinks", "conv_w", "conv_b", "ln_g", "ln_b", "attn_out", "conv_out",
             "ffn_norm", "final_norm"]
    flat = [a for p in params for a in p]
    n_in = len(flat)

    def body(dev_ref, ga_ref, gb_ref, gc_ref, *refs):
        ins, outs = refs[:n_in], refs[n_in:n_in + 4 * len(names)]
        loss_ref, sb, sc = refs[n_in + 4 * len(names):]
        a = ga_ref[0]
        sb[...] = gb_ref[0]
        sc[...] = gc_ref[0]
        for d in range(1, N_DEV):
            a = a + ga_ref[d]
            sb[...] += gb_ref[d]
            sc[...] += gc_ref[d]
        dev = dev_ref[0]
        grads = {
            "attn_norm": a[0:1, :], "ffn_norm": a[1:2, :], "final_norm": a[2:3, :],
            "conv_b": a[3:4, 0:512], "ln_g": a[3:4, 512:1024], "ln_b": a[4:5, 0:512],
            "attn_out": a[4:5, 512:1024], "conv_out": a[5:6, 0:512], "sinks": a[5:6, 512:512 + N_HEADS],
            "meta": sb[pl.ds(pl.multiple_of(dev * N_META, N_META), N_META), :],
            "conv_w": sc[pl.ds(pl.multiple_of(dev * 32, 32), 32), :][0:CONV_K, :],
        }
        for idx, nm in enumerate(names):
            w_ref, m_ref, v_ref = ins[3 * idx:3 * idx + 3]
            g = grads[nm]
            delta, m, v = _adam(g, w_ref[...], m_ref[...], v_ref[...])
            o = outs[4 * idx:4 * idx + 4]
            o[0][...], o[1][...], o[2][...], o[3][...] = g, delta, m, v
        loss_ref[...] = a[6:7, 0:1]

    vm = pl.BlockSpec(memory_space=pltpu.VMEM)
    out_shape = [jax.ShapeDtypeStruct(p[0].shape, F32) for p in params for _ in range(4)]
    out_shape.append(jax.ShapeDtypeStruct((1, 1), F32))
    res = pl.pallas_call(
        body, name="adamw_small",
        grid_spec=pltpu.PrefetchScalarGridSpec(
            num_scalar_prefetch=1, grid=(1,),
            in_specs=[pl.BlockSpec(ga.shape, lambda i, d: (0, 0, 0)), pl.BlockSpec(gb.shape, lambda i, d: (0, 0, 0)),
                      pl.BlockSpec(gc.shape, lambda i, d: (0, 0, 0))]
            + [pl.BlockSpec(a.shape, lambda i, d: (0, 0)) for a in flat],
            out_specs=[pl.BlockSpec(s.shape, lambda i, d: (0, 0)) for s in out_shape],
            scratch_shapes=[pltpu.VMEM(gb.shape[1:], F32), pltpu.VMEM(gc.shape[1:], F32)]),
        out_shape=out_shape,
        compiler_params=pltpu.CompilerParams(vmem_limit_bytes=VMEM_LIMIT),
    )(dev, ga, gb, gc, *flat)
    return [res[4 * i:4 * i + 4] for i in range(len(names))], res[-1]


def kernel(x, meta_tokens, attn_norm_g, w_in, attn_sinks, conv_w, conv_b, conv_ln_g, conv_ln_b, attn_out_g, conv_out_g, w_out, ffn_norm_g, w_gate, w_up, w_down, final_norm_g, loss_target, m_meta_tokens, m_attn_norm_g, m_w_in, m_attn_sinks, m_conv_w, m_conv_b, m_conv_ln_g, m_conv_ln_b, m_attn_out_g, m_conv_out_g, m_w_out, m_ffn_norm_g, m_w_gate, m_w_up, m_w_down, m_final_norm_g, v_meta_tokens, v_attn_norm_g, v_w_in, v_attn_sinks, v_conv_w, v_conv_b, v_conv_ln_g, v_conv_ln_b, v_attn_out_g, v_conv_out_g, v_w_out, v_ffn_norm_g, v_w_gate, v_w_up, v_w_down, v_final_norm_g):
    xi, yi, ci = _position()
    dev = jnp.reshape(_device_number((xi, yi, ci)), (1,)).astype(jnp.int32)
    core = jnp.reshape(ci, (1,)).astype(jnp.int32)
    slot = jnp.reshape(2 * xi + yi, (1,)).astype(jnp.int32)

    w_in_t, meta_st, convw_st = _allgather_params([w_in[0].T.astype(BF16)], [meta_tokens, conv_w[0]])
    meta_full = jnp.transpose(meta_st, (1, 0, 2)).reshape(N_META, D_MODEL)
    convw_full = jnp.transpose(convw_st, (1, 0, 2)).reshape(CONV_K, CONV_W)

    final_g = final_norm_g.reshape(1, D_MODEL)

    (h0, q, kv, ca, cg), (w_out_b,) = _inproj_fwd(x[0], meta_full, attn_norm_g, w_in_t, [w_out[0].astype(BF16)])
    o_attn, lse, (wg_t, wu_t) = _attn_fwd(
        q, kv, attn_sinks, [w_gate[0].T.astype(BF16), w_up[0].T.astype(BF16)])
    (o_conv, y_conv), (wd_b,) = _conv_fwd(ca, cg, convw_full, conv_b, conv_ln_g, conv_ln_b, [w_down[0].astype(BF16)])
    h1 = _outproj_fwd(h0, o_attn, o_conv, attn_out_g, conv_out_g, w_out_b)
    gate, up, dh2, loss_sum, dg_final = _ffn_fwd(h1, ffn_norm_g, wg_t, wu_t, wd_b, final_g, loss_target[0])

    def blocks(g):
        return g.reshape(4, 2, g.shape[0] // N_DEV, D_MODEL)

    def add_siblings(grads, received, tags):
        return [_add_sibling(g, r, core, "add_sibling_" + t) for g, r, t in zip(grads, received, tags)]

    dgate, dup, act, hn2, dh1, dg_ffn = _ffn_bwd(dh2, h1, ffn_norm_g, gate, up, wg_t, wu_t, wd_b)
    ffn_grads = [blocks(_wgrad(dgate, hn2, FF_CHUNK, "wgrad_gate")), blocks(_wgrad(dup, hn2, FF_CHUNK, "wgrad_up")),
                 blocks(_wgrad(act, dh2, FF_CHUNK, "wgrad_down"))]
    (do_attn, do_conv, mixed, dg_ao, dg_co), ffn_sib = _outproj_bwd(
        dh1, o_attn, o_conv, attn_out_g, conv_out_g, w_out_b, ffn_grads)
    ffn_sums = add_siblings(ffn_grads, ffn_sib, ("gate", "up", "down"))
    out_grads = [blocks(_wgrad(mixed, dh1, D_MODEL, "wgrad_out"))]
    (dca, dcg, dconvw, dconvb, dln_g, dln_b), gate_up_chips = _conv_bwd(
        do_conv, y_conv, ca, cg, convw_full, conv_ln_g, conv_ln_b, ffn_sums[:2])
    (dq, dkv, dkvm, dsinks), out_sib, down_chips = _attn_bwd(
        q, kv, attn_sinks, o_attn, lse, do_attn, out_grads, ffn_sums[2:])
    ffn_chips = list(gate_up_chips) + list(down_chips)
    out_sums = add_siblings(out_grads, out_sib, ("out",))
    grad_x, dmeta, dproj, hn1, dg_attn = _inproj_bwd(dh1, h0, attn_norm_g, dq, dkv, dkvm, dca, dcg, w_in_t)
    dwi_t, out_chips = _wgrad(dproj, hn1, 1792, "wgrad_in", out_sums)
    in_grads = [blocks(dwi_t)]
    in_sums = add_siblings(in_grads, _reduce_siblings(in_grads), ("in",))
    small_a = jnp.concatenate([
        dg_attn, dg_ffn, dg_final, jnp.concatenate([dconvb, dln_g], axis=1), jnp.concatenate([dln_b, dg_ao], axis=1),
        jnp.concatenate([dg_co, dsinks, jnp.zeros((1, 512 - N_HEADS), F32)], axis=1),
        jnp.concatenate([loss_sum[0:1, :], jnp.zeros((1, D_MODEL - 128), F32)], axis=1),
        jnp.zeros((1, D_MODEL), F32)], axis=0)
    small_b = jnp.transpose(dmeta.reshape(N_META, N_DEV, 128), (1, 0, 2)).reshape(N_DEV * N_META, 128)
    small_c = jnp.transpose(dconvw.reshape(32, N_DEV, 64), (1, 0, 2)).reshape(N_DEV * 32, 64)
    in_chips, ga, gb, gc = _reduce_chips(in_sums, [small_a, small_b, small_c])
    tags = ("in", "out", "gate", "up", "down")
    chip_sums = in_sums + out_sums + ffn_sums
    from_chips = [in_chips] + list(out_chips) + list(ffn_chips)

    big = [(True, w_in, m_w_in, v_w_in), (False, w_out, m_w_out, v_w_out), (True, w_gate, m_w_gate, v_w_gate),
           (True, w_up, m_w_up, v_w_up), (False, w_down, m_w_down, v_w_down)]
    big_out = {}
    for t, p, r, (transposed, w, m, v) in zip(tags, chip_sums, from_chips, big):
        rows = (lambda a: jnp.transpose(a[0])) if transposed else (lambda a: a[0])
        back = (lambda a: jnp.transpose(a)[None]) if transposed else (lambda a: a[None])
        big_out[t] = [back(a) for a in _adamw(p, r, slot, rows(w), rows(m), rows(v), "adamw_" + t)]

    small_params = [
        (meta_tokens, m_meta_tokens, v_meta_tokens), (attn_norm_g, m_attn_norm_g, v_attn_norm_g),
        (attn_sinks, m_attn_sinks, v_attn_sinks), (conv_w[0], m_conv_w[0], v_conv_w[0]),
        (conv_b, m_conv_b, v_conv_b), (conv_ln_g, m_conv_ln_g, v_conv_ln_g), (conv_ln_b, m_conv_ln_b, v_conv_ln_b),
        (attn_out_g, m_attn_out_g, v_attn_out_g), (conv_out_g, m_conv_out_g, v_conv_out_g),
        (ffn_norm_g, m_ffn_norm_g, v_ffn_norm_g),
        (final_g, m_final_norm_g.reshape(1, D_MODEL), v_final_norm_g.reshape(1, D_MODEL))]
    sm, loss = _adamw_small(dev, ga, gb, gc, small_params)
    sm[3] = [a[None] for a in sm[3]]
    sm[10] = [a.reshape(D_MODEL) for a in sm[10]]

    per_param = [sm[0], sm[1], big_out["in"], sm[2], sm[3], sm[4], sm[5], sm[6], sm[7], sm[8], big_out["out"],
                 sm[9], big_out["gate"], big_out["up"], big_out["down"], sm[10]]
    loss = loss.reshape(())
    outs = [loss, grad_x[None]]
    for kind in range(4):
        outs += [p[kind] for p in per_param]
    return tuple(outs)
```

```python
import functools
import math

import jax
import jax.numpy as jnp
from jax import lax
from jax.experimental import pallas as pl
from jax.experimental.pallas import tpu as pltpu

F32, BF16 = jnp.float32, jnp.bfloat16
MESH = pl.DeviceIdType.MESH

D_MODEL = 1024
N_META = 16
BLOCK = 128
LEAD = BLOCK - N_META
HEAD_DIM = 64
N_HEADS = 8
GROUP = 4
ATTN_W = 512
KV_W = 128
CONV_W = 512
CONV_K = 31
HALO = 32
D_FF = 2816
FF_CHUNK = D_FF // 2
FF_SUB = [slice(s, s + 256) for s in range(0, D_FF, 256)]
N_DEV = 8
EPS = 1e-5
NEG = -1e30
TM = 640
AG_FORWARD_AT = 85
WGRAD_K_TILES = 5
CONV_ROWS = 32
VMEM_LIMIT = 56 * 1024 * 1024

ADAM_LR, ADAM_B1, ADAM_B2, ADAM_EPS, ADAM_WD, ADAM_STEP = 0.001, 0.9, 0.999, 1e-08, 0.01, 10

NT = (((1,), (1,)), ((), ()))
NN = (((1,), (0,)), ((), ()))
TN = (((0,), (0,)), ((), ()))


def _dot(a, b, dims):
    return lax.dot_general(a, b, dims, preferred_element_type=F32)


def _sigmoid(x):
    return 1.0 / (1.0 + jnp.exp(-x))


def _pcall(body, *, name, out_shape, grid=None, in_specs=None, out_specs=None, scratch_shapes=(),
           semantics=None, **kw):
    params = dict(vmem_limit_bytes=VMEM_LIMIT)
    if semantics is not None:
        params["dimension_semantics"] = semantics
    extra = {}
    if grid is not None:
        extra["grid"] = grid
    if in_specs is not None:
        extra["in_specs"] = in_specs
    if out_specs is not None:
        extra["out_specs"] = out_specs
    return pl.pallas_call(body, name=name, out_shape=out_shape, scratch_shapes=list(scratch_shapes),
                          compiler_params=pltpu.CompilerParams(**params), **extra, **kw)


def _rows(tm, cols, off=0):
    return pl.BlockSpec((tm, cols), lambda i, *_: (i + off, 0))


def _full(shape):
    nd = len(shape)
    return pl.BlockSpec(shape, lambda *_: (0,) * nd)


def _rms_stats(x):
    return lax.rsqrt(jnp.mean(x * x, axis=-1, keepdims=True) + EPS)


def _rms_bwd(dy, x, r, g):
    t = dy * g
    dx = r * (t - x * (r * r) * jnp.mean(t * x, axis=-1, keepdims=True))
    dg = jnp.sum(dy * x * r, axis=0, keepdims=True)
    return dx, dg


def _inproj_fwd(x, meta, g1, w_in_t, shards):
    R = x.shape[0] + BLOCK
    nt = R // TM
    ns = len(shards)
    assert nt >= 2

    def body(x_hbm, meta_ref, g_ref, w_ref, *refs):
        ag_ins, (h0_ref, q_ref, kv_ref, ca_ref, cg_ref), ag_outs = refs[:ns], refs[ns:ns + 5], refs[ns + 5:2 * ns + 5]
        ag_sems, (x_s, sems) = refs[2 * ns + 5:2 * ns + 8], refs[2 * ns + 8:]
        i = pl.program_id(0)
        slot = i % 2
        ag_finish = _carried_allgather(i, nt, shards, ag_ins + ag_outs + ag_sems)

        def x_copy(step, slot, first):
            if first:
                return pltpu.make_async_copy(x_hbm.at[pl.ds(0, TM - BLOCK)],
                                             x_s.at[slot, pl.ds(BLOCK, TM - BLOCK)], sems.at[slot])
            return pltpu.make_async_copy(
                x_hbm.at[pl.ds(pl.multiple_of(step * TM - BLOCK, BLOCK), TM)], x_s.at[slot], sems.at[slot])

        @pl.when(i == 0)
        def _():
            x_copy(0, 0, True).start()
            x_s[0, 0:LEAD, :] = jnp.zeros((LEAD, D_MODEL), F32)
            x_s[0, LEAD:BLOCK, :] = meta_ref[...]

        pl.when(i + 1 < nt)(lambda: x_copy(i + 1, 1 - slot, False).start())
        pl.when(i == 0)(lambda: x_copy(0, 0, True).wait())
        pl.when(i > 0)(lambda: x_copy(i, slot, False).wait())
        h = x_s[slot]
        h0_ref[...] = h
        hn = (h * _rms_stats(h) * g_ref[...]).astype(BF16)
        q_ref[...] = _dot(hn, w_ref[0:512, :], NT).astype(BF16)
        kv_ref[...] = _dot(hn, w_ref[512:768, :], NT).astype(BF16)
        ca_ref[...] = _dot(hn, w_ref[768:1280, :], NT)
        cg_ref[...] = _dot(hn, w_ref[1280:1792, :], NT)
        ag_finish()

    anywhere = pl.BlockSpec(memory_space=pl.ANY)
    res = _pcall(
        body, name="inproj_fwd", grid=(nt,),
        in_specs=[anywhere, _full((N_META, D_MODEL)), _full((1, D_MODEL)), _full((1792, D_MODEL))] + [anywhere] * ns,
        out_specs=[_rows(TM, D_MODEL), _rows(TM, 512), _rows(TM, 256), _rows(TM, 512), _rows(TM, 512)]
        + [anywhere] * ns,
        out_shape=[jax.ShapeDtypeStruct((R, D_MODEL), F32),
                   jax.ShapeDtypeStruct((R, 512), BF16), jax.ShapeDtypeStruct((R, 256), BF16),
                   jax.ShapeDtypeStruct((R, 512), F32), jax.ShapeDtypeStruct((R, 512), F32)]
        + _gathered_shapes(shards),
        scratch_shapes=_allgather_sems(ns) + [pltpu.VMEM((2, TM, D_MODEL), F32), pltpu.SemaphoreType.DMA((2,))],
        semantics=("arbitrary",),
    )(x, meta, g1, w_in_t, *shards)
    return res[:5], res[5:]


GB = GROUP * BLOCK
ATTN_SCALE = 1.0 / math.sqrt(HEAD_DIM)


def _group_lanes(xt, g):
    return jnp.concatenate(
        [xt[HEAD_DIM * (GROUP * g + j):HEAD_DIM * (GROUP * g + j + 1), :] for j in range(GROUP)], axis=1)


def _head_lanes(ref, g):
    return jnp.concatenate([ref[GROUP * g + j:GROUP * g + j + 1, :] for j in range(GROUP)], axis=1)


def _head_rows(xs):
    return jnp.concatenate([x[:, BLOCK * j:BLOCK * (j + 1)] for x in xs for j in range(GROUP)], axis=0)


def _attn_tables(sink_ref, bias_s, sink_s):
    kk = lax.broadcasted_iota(jnp.int32, (BLOCK, BLOCK), 0)
    ii = lax.broadcasted_iota(jnp.int32, (BLOCK, BLOCK), 1)
    dist = jnp.where(kk <= ii, ii - kk, ii - kk + BLOCK).astype(F32)
    for h in range(N_HEADS):
        bias_s[:, BLOCK * h:BLOCK * (h + 1)] = dist * -(2.0 ** -(h + 1))
        sink_s[:, BLOCK * h:BLOCK * (h + 1)] = jnp.zeros((1, BLOCK), F32) + sink_ref[0, h]


def _attn_masks(b):
    kk = lax.broadcasted_iota(jnp.int32, (BLOCK, GB), 0)
    ii = lax.broadcasted_iota(jnp.int32, (BLOCK, GB), 1) & (BLOCK - 1)
    sel = kk <= ii
    pen = jnp.where(sel, jnp.where(b >= 1, 0.0, NEG), jnp.where(b >= 2, 0.0, NEG))
    mj = lax.broadcasted_iota(jnp.int32, (N_META, GB), 0)
    mi = lax.broadcasted_iota(jnp.int32, (N_META, GB), 1) & (BLOCK - 1)
    pen_m = jnp.where((mj + LEAD) <= (mi + b * BLOCK), 0.0, NEG)
    return sel, pen, pen_m


def _attn_scores(qt, kc, kp, km, sel, pen, pen_m, bias):
    s_b = jnp.where(sel, _dot(kc, qt, NN), _dot(kp, qt, NN)) + bias + pen
    s_m = _dot(km, qt, NN) + pen_m
    return s_b, s_m


def _attn_fwd(q, kv, sinks, shards):
    R = q.shape[0]
    nb = R // BLOCK
    ns = len(shards)

    def body(sink_ref, q_ref, kvc_ref, kvp_ref, kvm_ref, *refs):
        ag_ins, (o_ref, lse_ref), ag_outs = refs[:ns], refs[ns:ns + 2], refs[ns + 2:2 * ns + 2]
        ag_sems, (bias_s, sink_s) = refs[2 * ns + 2:2 * ns + 5], refs[2 * ns + 5:]
        b = pl.program_id(0)
        ag_finish = _carried_allgather(b, nb, shards, ag_ins + ag_outs + ag_sems)
        pl.when(b == 0)(functools.partial(_attn_tables, sink_ref, bias_s, sink_s))
        sel, pen, pen_m = _attn_masks(b)
        q_t = (q_ref[...] * ATTN_SCALE).T
        kvc_t, kvp_t = kvc_ref[...].T, kvp_ref[...].T
        outs = []
        for g in range(N_HEADS // GROUP):
            ks, vs = slice(HEAD_DIM * g, HEAD_DIM * (g + 1)), slice(KV_W + HEAD_DIM * g, KV_W + HEAD_DIM * (g + 1))
            lanes = slice(GB * g, GB * (g + 1))
            s_b, s_m = _attn_scores(_group_lanes(q_t, g), kvc_ref[:, ks], kvp_ref[:, ks], kvm_ref[LEAD:BLOCK, ks],
                                    sel, pen, pen_m, bias_s[:, lanes])
            sink = sink_s[:, lanes]
            m = jnp.maximum(jnp.maximum(jnp.max(s_b, axis=0, keepdims=True),
                                        jnp.max(s_m, axis=0, keepdims=True)), sink)
            p_b = jnp.exp(s_b - m)
            p_m = jnp.exp(s_m - m)
            l = jnp.sum(p_b, axis=0, keepdims=True) + jnp.sum(p_m, axis=0, keepdims=True) + jnp.exp(sink - m)
            p_c = jnp.where(sel, p_b, 0.0).astype(BF16)
            p_p = jnp.where(sel, 0.0, p_b).astype(BF16)
            o_t = (_dot(kvc_t[vs, :], p_c, NN) + _dot(kvp_t[vs, :], p_p, NN)
                   + _dot(kvm_ref[LEAD:BLOCK, vs], p_m.astype(BF16), TN))
            outs.append(o_t / l)
            lse = m + jnp.log(l)
            for j in range(GROUP):
                lse_ref[GROUP * g + j:GROUP * g + j + 1, :] = lse[:, BLOCK * j:BLOCK * (j + 1)]
        o_ref[...] = _head_rows(outs).T
        ag_finish()

    res = _pcall(
        body, name="attn_fwd", grid=(nb,),
        in_specs=[pl.BlockSpec(memory_space=pltpu.SMEM),
                  _rows(BLOCK, 512), _rows(BLOCK, 256),
                  pl.BlockSpec((BLOCK, 256), lambda b: (jnp.maximum(b - 1, 0), 0)),
                  _full((BLOCK, 256))] + [ANY] * ns,
        out_specs=[_rows(BLOCK, 512), pl.BlockSpec((N_HEADS, BLOCK), lambda b: (0, b))] + [ANY] * ns,
        out_shape=[jax.ShapeDtypeStruct((R, 512), F32), jax.ShapeDtypeStruct((N_HEADS, R), F32)]
        + _gathered_shapes(shards),
        scratch_shapes=_allgather_sems(ns) + [pltpu.VMEM((BLOCK, N_HEADS * BLOCK), F32),
                                              pltpu.VMEM((1, N_HEADS * BLOCK), F32)],
        semantics=("arbitrary",),
    )(sinks, q, kv, kv, kv, *shards)
    return res[0], res[1], res[2:]


def _ln_silu(y, lg, lb):
    mu = jnp.mean(y, axis=-1, keepdims=True)
    xc = y - mu
    rstd = lax.rsqrt(jnp.mean(xc * xc, axis=-1, keepdims=True) + EPS)
    xhat = xc * rstd
    yn = xhat * lg + lb
    return yn, xhat, rstd


PHASE_ROWS = HALO + TM - 8


def _phase_scratch():
    return pltpu.VMEM((7, PHASE_ROWS, CONV_W), F32)


def _phase_copies(src_s, ph_s):
    for b in range(1, 8):
        ph_s[b - 1] = src_s[pl.ds(b, PHASE_ROWS), :]


def _shifted(src_s, ph_s, start, rows):
    a8, b = (start // 8) * 8, start % 8
    if b == 0:
        return src_s[pl.ds(a8, rows), :]
    return ph_s[b - 1, pl.ds(a8, rows), :]


def _conv_fwd(ca, cg, conv_w, conv_b, ln_g, ln_b, shards):
    R = ca.shape[0]
    nt = R // TM
    hpt = TM // HALO
    ns = len(shards)

    def body(ca_ref, cg_ref, cah_ref, cgh_ref, w_ref, b_ref, lg_ref, lb_ref, *refs):
        ag_ins, (oc_ref, y_ref), ag_outs = refs[:ns], refs[ns:ns + 2], refs[ns + 2:2 * ns + 2]
        ag_sems, (u_s, uph_s) = refs[2 * ns + 2:2 * ns + 5], refs[2 * ns + 5:]
        i = pl.program_id(0)
        ag_finish = _carried_allgather(i, nt, shards, ag_ins + ag_outs + ag_sems)
        u_s[HALO:HALO + TM, :] = ca_ref[...] * _sigmoid(cg_ref[...])
        u_s[0:HALO, :] = jnp.where(i > 0, cah_ref[...] * _sigmoid(cgh_ref[...]), 0.0)
        _phase_copies(u_s, uph_s)
        for rc in range(TM // CONV_ROWS):
            base = rc * CONV_ROWS + HALO - (CONV_K - 1)
            acc = jnp.zeros((CONV_ROWS, CONV_W), F32) + b_ref[...]
            for k in range(CONV_K):
                acc = acc + _shifted(u_s, uph_s, base + k, CONV_ROWS) * w_ref[k:k + 1, :]
            rows = slice(rc * CONV_ROWS, (rc + 1) * CONV_ROWS)
            y_ref[rows, :] = acc
            yn, _, _ = _ln_silu(acc, lg_ref[...], lb_ref[...])
            oc_ref[rows, :] = yn * _sigmoid(yn)
        ag_finish()

    prev_halo = pl.BlockSpec((HALO, CONV_W), lambda i: (jnp.maximum(i * hpt - 1, 0), 0))
    anywhere = pl.BlockSpec(memory_space=pl.ANY)
    res = _pcall(
        body, name="conv_fwd", grid=(nt,),
        in_specs=[_rows(TM, CONV_W), _rows(TM, CONV_W), prev_halo, prev_halo,
                  _full((CONV_K, CONV_W)), _full((1, CONV_W)), _full((1, CONV_W)), _full((1, CONV_W))]
        + [anywhere] * ns,
        out_specs=[_rows(TM, CONV_W), _rows(TM, CONV_W)] + [anywhere] * ns,
        out_shape=[jax.ShapeDtypeStruct((R, CONV_W), F32), jax.ShapeDtypeStruct((R, CONV_W), F32)]
        + _gathered_shapes(shards),
        scratch_shapes=_allgather_sems(ns) + [pltpu.VMEM((HALO + TM, CONV_W), F32), _phase_scratch()],
        semantics=("arbitrary",),
    )(ca, cg, ca, cg, conv_w, conv_b, ln_g, ln_b, *shards)
    return res[:2], res[2:]


def _outproj_fwd(h0, o_attn, o_conv, ga, gc, w_out):
    R = h0.shape[0]

    def body(h_ref, oa_ref, oc_ref, ga_ref, gc_ref, w_ref, h1_ref):
        oa, oc = oa_ref[...], oc_ref[...]
        ma = (oa * _rms_stats(oa) * ga_ref[...]).astype(BF16)
        mc = (oc * _rms_stats(oc) * gc_ref[...]).astype(BF16)
        h1_ref[...] = h_ref[...] + _dot(ma, w_ref[0:512, :], NN) + _dot(mc, w_ref[512:1024, :], NN)

    return _pcall(
        body, name="outproj_fwd", grid=(R // TM,),
        in_specs=[_rows(TM, D_MODEL), _rows(TM, 512), _rows(TM, 512), _full((1, 512)), _full((1, 512)),
                  _full((D_MODEL, D_MODEL))],
        out_specs=_rows(TM, D_MODEL),
        out_shape=jax.ShapeDtypeStruct((R, D_MODEL), F32),
        semantics=("parallel",),
    )(h0, o_attn, o_conv, ga, gc, w_out)


def _target_copy(tgt_hbm, tgt_s, sem, i, first):
    if first:
        return pltpu.make_async_copy(tgt_hbm.at[pl.ds(0, TM - BLOCK)], tgt_s.at[pl.ds(BLOCK, TM - BLOCK)], sem)
    return pltpu.make_async_copy(tgt_hbm.at[pl.ds(i * TM - BLOCK, TM)], tgt_s, sem)


def _resident(shape):
    nd = len(shape)
    return pl.BlockSpec(shape, lambda *_: (0,) * nd, pipeline_mode=pl.Buffered(1))


def _ffn_fwd(h1, g2, wg_t, wu_t, wd, gf, target):
    R = h1.shape[0]
    nt = R // TM

    def body(h1_ref, g2_ref, wg_ref, wu_ref, wd_ref, gf_ref, tgt_hbm,
             gate_ref, up_ref, act_s, dh2_ref, loss_ref, dgf_ref, tgt_s, sem):
        i = pl.program_id(0)

        @pl.when(i == 0)
        def _():
            loss_ref[...] = jnp.zeros_like(loss_ref)
            dgf_ref[...] = jnp.zeros_like(dgf_ref)
            tgt_s[0:BLOCK, :] = jnp.zeros((BLOCK, D_MODEL), F32)
            _target_copy(tgt_hbm, tgt_s, sem, i, True).start()

        pl.when(i > 0)(lambda: _target_copy(tgt_hbm, tgt_s, sem, i, False).start())
        h1 = h1_ref[...]
        hn = (h1 * _rms_stats(h1) * g2_ref[...]).astype(BF16)
        for cs in FF_SUB:
            gate = _dot(hn, wg_ref[cs, :], NT)
            up = _dot(hn, wu_ref[cs, :], NT)
            gate_ref[:, cs] = gate.astype(BF16)
            up_ref[:, cs] = up.astype(BF16)
            act_s[:, cs] = (gate * _sigmoid(gate) * up).astype(BF16)
        part = _dot(act_s[...], wd_ref[...], NN)
        pl.when(i == 0)(lambda: _target_copy(tgt_hbm, tgt_s, sem, i, True).wait())
        pl.when(i > 0)(lambda: _target_copy(tgt_hbm, tgt_s, sem, i, False).wait())
        h2 = h1 + part
        rf = _rms_stats(h2)
        gf = gf_ref[...]
        row = lax.broadcasted_iota(jnp.int32, (TM, 1), 0) + i * TM
        err = jnp.where(row >= BLOCK, h2 * rf * gf - tgt_s[...], 0.0)
        dy = err * (1.0 / D_MODEL)
        dh2, dgf = _rms_bwd(dy, h2, rf, gf)
        dh2_ref[...] = dh2
        loss_ref[...] += (0.5 / D_MODEL) * jnp.sum(err * err)
        dgf_ref[...] += dgf

    wspec = _resident((D_FF, D_MODEL))
    return _pcall(
        body, name="ffn_fwd", grid=(nt,),
        in_specs=[_rows(TM, D_MODEL), _full((1, D_MODEL)), wspec, wspec, wspec, _full((1, D_MODEL)),
                  pl.BlockSpec(memory_space=pl.ANY)],
        out_specs=[_rows(TM, D_FF), _rows(TM, D_FF), _rows(TM, D_FF), _rows(TM, D_MODEL), _full((8, 128)),
                   _full((1, D_MODEL))],
        out_shape=[jax.ShapeDtypeStruct((R, D_FF), BF16)] * 3
        + [jax.ShapeDtypeStruct((R, D_MODEL), F32),
           jax.ShapeDtypeStruct((8, 128), F32), jax.ShapeDtypeStruct((1, D_MODEL), F32)],
        scratch_shapes=[pltpu.VMEM((TM, D_MODEL), F32), pltpu.SemaphoreType.DMA],
        semantics=("arbitrary",),
    )(h1, g2, wg_t, wu_t, wd, gf, target)


def _ffn_bwd(dh2, h1, g2, gate, up, wg_t, wu_t, wd):
    R = h1.shape[0]
    nt = R // TM
    act_shape = jax.ShapeDtypeStruct((R, D_FF), BF16)

    def act_body(dh2_ref, gate_ref, up_ref, wd_ref, dgate_ref, dup_ref):
        dhb = dh2_ref[...].astype(BF16)
        for cs in FF_SUB:
            dact = _dot(dhb, wd_ref[cs, :], NT)
            gate = gate_ref[:, cs].astype(F32)
            up = up_ref[:, cs].astype(F32)
            sig = _sigmoid(gate)
            dgate_ref[:, cs] = (dact * up * (sig * (1.0 + gate * (1.0 - sig)))).astype(BF16)
            dup_ref[:, cs] = (dact * (gate * sig)).astype(BF16)

    dgate, dup = _pcall(
        act_body, name="ffn_bwd_act", grid=(nt,),
        in_specs=[_rows(TM, D_MODEL), _rows(TM, D_FF), _rows(TM, D_FF), _resident((D_FF, D_MODEL))],
        out_specs=[_rows(TM, D_FF), _rows(TM, D_FF)], out_shape=[act_shape, act_shape],
        semantics=("parallel",),
    )(dh2, gate, up, wd)

    def in_body(dh2_ref, h1_ref, g2_ref, dgate_ref, dup_ref, wg_ref, wu_ref, hn_ref, dh1_ref, dg2_ref):
        @pl.when(pl.program_id(0) == 0)
        def _():
            dg2_ref[...] = jnp.zeros_like(dg2_ref)

        dhn = _dot(dgate_ref[...], wg_ref[...], NN) + _dot(dup_ref[...], wu_ref[...], NN)
        h1 = h1_ref[...]
        r = _rms_stats(h1)
        g2 = g2_ref[...]
        hn_ref[...] = (h1 * r * g2).astype(BF16)
        dx, dg = _rms_bwd(dhn, h1, r, g2)
        dh1_ref[...] = dh2_ref[...] + dx
        dg2_ref[...] += dg

    hn2, dh1, dg2 = _pcall(
        in_body, name="ffn_bwd_in", grid=(nt,),
        in_specs=[_rows(TM, D_MODEL), _rows(TM, D_MODEL), _full((1, D_MODEL)), _rows(TM, D_FF), _rows(TM, D_FF),
                  _resident((D_FF, D_MODEL)), _resident((D_FF, D_MODEL))],
        out_specs=[_rows(TM, D_MODEL), _rows(TM, D_MODEL), _full((1, D_MODEL))],
        out_shape=[jax.ShapeDtypeStruct((R, D_MODEL), BF16), jax.ShapeDtypeStruct((R, D_MODEL), F32),
                   jax.ShapeDtypeStruct((1, D_MODEL), F32)],
        semantics=("arbitrary",),
    )(dh2, h1, g2, dgate, dup, wg_t, wu_t)
    return dgate, dup, hn2, dh1, dg2


def _wgrad(a, b, tm, name, partials=()):
    K, M = a.shape
    N = b.shape[1]
    tk = K // WGRAD_K_TILES if K % (WGRAD_K_TILES * BLOCK) == 0 else TM
    nm, nk, npart = M // tm, K // tk, len(partials)

    def body(a_ref, b_ref, *refs):
        p_ins, o_ref, p_outs, sems = refs[:npart], refs[npart], refs[npart + 1:2 * npart + 1], refs[2 * npart + 1:]
        step = pl.program_id(0) * nk + pl.program_id(1)
        exchange = functools.partial(_chip_copies, p_ins, p_outs, *sems)
        if npart:
            _hosted(step, nm * nk, exchange)

        @pl.when(pl.program_id(1) == 0)
        def _():
            o_ref[...] = jnp.zeros_like(o_ref)

        o_ref[...] += _dot(a_ref[...], b_ref[...].astype(BF16), TN)
        if npart:
            _hosted_wait(step, nm * nk, exchange)

    res = _pcall(
        body, name=name, grid=(nm, nk),
        in_specs=[pl.BlockSpec((tk, tm), lambda m, k: (k, m)), pl.BlockSpec((tk, N), lambda m, k: (k, 0))]
        + [ANY] * npart,
        out_specs=[pl.BlockSpec((tm, N), lambda m, k: (m, 0))] + [ANY] * npart,
        out_shape=[jax.ShapeDtypeStruct((M, N), F32)] + _chip_shapes(partials),
        scratch_shapes=_sem_pair(3 * npart) if npart else [],
        semantics=("arbitrary", "arbitrary"),
    )(a, b, *partials)
    return (res[0], res[1:]) if npart else res[0]


def _outproj_bwd(dh1, o_attn, o_conv, ga, gc, w_out, grads):
    R = dh1.shape[0]
    nt, ng = R // TM, len(grads)

    def body(dh1_ref, oa_ref, oc_ref, ga_ref, gc_ref, w_ref, *refs):
        g_ins, (doa_ref, doc_ref, mixed_ref, dga_ref, dgc_ref) = refs[:ng], refs[ng:ng + 5]
        g_outs, (send_sems, recv_sems) = refs[ng + 5:2 * ng + 5], refs[2 * ng + 5:]
        exchange = functools.partial(_sibling_copies, g_ins, g_outs, send_sems, recv_sems)
        _hosted(pl.program_id(0), nt, exchange)

        @pl.when(pl.program_id(0) == 0)
        def _():
            dga_ref[...] = jnp.zeros_like(dga_ref)
            dgc_ref[...] = jnp.zeros_like(dgc_ref)

        dm = _dot(dh1_ref[...].astype(BF16), w_ref[...], NT)
        oa, oc = oa_ref[...], oc_ref[...]
        ra, rc = _rms_stats(oa), _rms_stats(oc)
        mixed_ref[:, 0:512] = (oa * ra * ga_ref[...]).astype(BF16)
        mixed_ref[:, 512:1024] = (oc * rc * gc_ref[...]).astype(BF16)
        doa, dga = _rms_bwd(dm[:, 0:512], oa, ra, ga_ref[...])
        doc, dgc = _rms_bwd(dm[:, 512:1024], oc, rc, gc_ref[...])
        doa_ref[...] = doa
        doc_ref[...] = doc
        dga_ref[...] += dga
        dgc_ref[...] += dgc
        _hosted_wait(pl.program_id(0), nt, exchange)

    res = _pcall(
        body, name="outproj_bwd", grid=(nt,),
        in_specs=[_rows(TM, D_MODEL), _rows(TM, 512), _rows(TM, 512), _full((1, 512)), _full((1, 512)),
                  _full((D_MODEL, D_MODEL))] + [ANY] * ng,
        out_specs=[_rows(TM, 512), _rows(TM, 512), _rows(TM, D_MODEL), _full((1, 512)), _full((1, 512))]
        + [ANY] * ng,
        out_shape=[jax.ShapeDtypeStruct((R, 512), F32), jax.ShapeDtypeStruct((R, 512), F32),
                   jax.ShapeDtypeStruct((R, D_MODEL), BF16),
                   jax.ShapeDtypeStruct((1, 512), F32), jax.ShapeDtypeStruct((1, 512), F32)]
        + _sibling_shapes(grads),
        scratch_shapes=_sem_pair(ng),
        semantics=("arbitrary",),
    )(dh1, o_attn, o_conv, ga, gc, w_out, *grads)
    return res[:5], res[5:]


def _conv_bwd(do_conv, y, ca, cg, conv_w, ln_g, ln_b, partials):
    R = ca.shape[0]
    nt = R // TM
    hpt = TM // HALO
    npart = len(partials)

    def body(do_ref, doh_ref, y_ref, yh_ref, ca_ref, cg_ref, cah_ref, cgh_ref, w_ref, lg_ref, lb_ref, *refs):
        p_ins, (dca_ref, dcg_ref, dw_ref, db_ref, dlg_ref, dlb_ref) = refs[:npart], refs[npart:npart + 6]
        p_outs, (send_sems, recv_sems, u_s, dy_s, uph_s, dyph_s) = refs[npart + 6:2 * npart + 6], refs[2 * npart + 6:]
        i = pl.program_id(0)
        exchange = functools.partial(_chip_copies, p_ins, p_outs, send_sems, recv_sems)
        _hosted(i, nt, exchange)

        @pl.when(i == 0)
        def _():
            dw_ref[...] = jnp.zeros_like(dw_ref)
            db_ref[...] = jnp.zeros_like(db_ref)
            dlg_ref[...] = jnp.zeros_like(dlg_ref)
            dlb_ref[...] = jnp.zeros_like(dlb_ref)

        lg, lb = lg_ref[...], lb_ref[...]

        def ln_bwd(yv, dov):
            yn, xhat, rstd = _ln_silu(yv, lg, lb)
            sig = _sigmoid(yn)
            dyn = dov * (sig * (1.0 + yn * (1.0 - sig)))
            dxh = dyn * lg
            dyv = rstd * (dxh - jnp.mean(dxh, axis=-1, keepdims=True)
                          - xhat * jnp.mean(dxh * xhat, axis=-1, keepdims=True))
            return dyv, dyn, xhat

        dyv, dyn, xhat = ln_bwd(y_ref[...], do_ref[...])
        dy_s[0:TM, :] = dyv
        dlg_ref[...] += jnp.sum(dyn * xhat, axis=0, keepdims=True)
        dlb_ref[...] += jnp.sum(dyn, axis=0, keepdims=True)
        db_ref[...] += jnp.sum(dyv, axis=0, keepdims=True)
        dyh, _, _ = ln_bwd(yh_ref[...], doh_ref[...])
        dy_s[TM:TM + HALO, :] = jnp.where(i < nt - 1, dyh, 0.0)
        u_s[HALO:HALO + TM, :] = ca_ref[...] * _sigmoid(cg_ref[...])
        u_s[0:HALO, :] = jnp.where(i > 0, cah_ref[...] * _sigmoid(cgh_ref[...]), 0.0)
        _phase_copies(dy_s, dyph_s)
        _phase_copies(u_s, uph_s)

        for rc in range(TM // CONV_ROWS):
            acc = jnp.zeros((CONV_ROWS, CONV_W), F32)
            for k in range(CONV_K):
                acc = acc + _shifted(dy_s, dyph_s, rc * CONV_ROWS + CONV_K - 1 - k, CONV_ROWS) * w_ref[k:k + 1, :]
            rows = slice(rc * CONV_ROWS, (rc + 1) * CONV_ROWS)
            sg = _sigmoid(cg_ref[rows, :])
            dca_ref[rows, :] = (acc * sg).astype(BF16)
            dcg_ref[rows, :] = (acc * ca_ref[rows, :] * sg * (1.0 - sg)).astype(BF16)

        for k in range(CONV_K):
            prod = _shifted(u_s, uph_s, HALO - (CONV_K - 1) + k, TM) * dy_s[0:TM, :]
            dw_ref[k:k + 1, :] += jnp.sum(prod, axis=0, keepdims=True)
        _hosted_wait(i, nt, exchange)

    prev_halo = pl.BlockSpec((HALO, CONV_W), lambda i: (jnp.maximum(i * hpt - 1, 0), 0))
    next_halo = pl.BlockSpec((HALO, CONV_W), lambda i: (jnp.minimum((i + 1) * hpt, nt * hpt - 1), 0))
    vec = jax.ShapeDtypeStruct((1, CONV_W), F32)
    res = _pcall(
        body, name="conv_bwd", grid=(nt,),
        in_specs=[_rows(TM, CONV_W), next_halo, _rows(TM, CONV_W), next_halo,
                  _rows(TM, CONV_W), _rows(TM, CONV_W), prev_halo, prev_halo,
                  _full((CONV_K, CONV_W)), _full((1, CONV_W)), _full((1, CONV_W))] + [ANY] * npart,
        out_specs=[_rows(TM, CONV_W), _rows(TM, CONV_W), _full((32, CONV_W)),
                   _full((1, CONV_W)), _full((1, CONV_W)), _full((1, CONV_W))] + [ANY] * npart,
        out_shape=[jax.ShapeDtypeStruct((R, CONV_W), BF16), jax.ShapeDtypeStruct((R, CONV_W), BF16),
                   jax.ShapeDtypeStruct((32, CONV_W), F32), vec, vec, vec] + _chip_shapes(partials),
        scratch_shapes=_sem_pair(3 * npart)
        + [pltpu.VMEM((HALO + TM, CONV_W), F32), pltpu.VMEM((TM + HALO, CONV_W), F32),
           _phase_scratch(), _phase_scratch()],
        semantics=("arbitrary",),
    )(do_conv, do_conv, y, y, ca, cg, ca, cg, conv_w, ln_g, ln_b, *partials)
    return res[:6], res[6:]


def _attn_bwd(q, kv, sinks, o, lse, do, grads, partials):
    R = q.shape[0]
    nb = R // BLOCK
    ng, npart = len(grads), len(partials)
    nx = ng + npart

    def body(sink_ref, q_ref, kvc_ref, kvp_ref, kvm_ref, o_ref, lse_ref, do_ref, *refs):
        x_ins, (dq_ref, dkv_ref, dkvm_ref, dsink_ref) = refs[:nx], refs[nx:nx + 4]
        x_outs = refs[nx + 4:2 * nx + 4]
        g_send, g_recv, p_send, p_recv, carry_s, cur_s, prev_s, bias_s, sink_s, delta_s = refs[2 * nx + 4:]
        b = pl.program_id(0)

        def exchange():
            return (_sibling_copies(x_ins[:ng], x_outs[:ng], g_send, g_recv)
                    + _chip_copies(x_ins[ng:], x_outs[ng:], p_send, p_recv))

        _hosted(b, nb + 1, exchange)

        @pl.when(b == 0)
        def _():
            dkvm_ref[...] = jnp.zeros_like(dkvm_ref)
            carry_s[...] = jnp.zeros_like(carry_s)
            for h in range(N_HEADS):
                dsink_ref[0, h] = 0.0
            _attn_tables(sink_ref, bias_s, sink_s)

        @pl.when(b < nb)
        def _():
            sel, pen, pen_m = _attn_masks(b)
            q_t = (q_ref[...] * ATTN_SCALE).T
            do_t = do_ref[...].astype(BF16).T
            kvc_t, kvp_t = kvc_ref[...].T, kvp_ref[...].T
            prod = do_ref[...] * o_ref[...]
            hi = prod.astype(BF16)
            lo = (prod - hi.astype(F32)).astype(BF16)
            head_of = lax.broadcasted_iota(jnp.int32, (N_HEADS, ATTN_W), 1) // HEAD_DIM
            ind = (head_of == lax.broadcasted_iota(jnp.int32, (N_HEADS, ATTN_W), 0)).astype(BF16)
            delta_s[...] = _dot(ind, hi, NT) + _dot(ind, lo, NT)
            dqs = []
            for g in range(N_HEADS // GROUP):
                ks, vs = slice(HEAD_DIM * g, HEAD_DIM * (g + 1)), slice(KV_W + HEAD_DIM * g, KV_W + HEAD_DIM * (g + 1))
                lanes = slice(GB * g, GB * (g + 1))
                qg, dog = _group_lanes(q_t, g), _group_lanes(do_t, g)
                kc, kp, km = kvc_ref[:, ks], kvp_ref[:, ks], kvm_ref[LEAD:BLOCK, ks]
                vc, vp, vm = kvc_ref[:, vs], kvp_ref[:, vs], kvm_ref[LEAD:BLOCK, vs]
                s_b, s_m = _attn_scores(qg, kc, kp, km, sel, pen, pen_m, bias_s[:, lanes])
                lse, delta = _head_lanes(lse_ref, g), _head_lanes(delta_s, g)
                p_b = jnp.exp(s_b - lse)
                p_m = jnp.exp(s_m - lse)
                dp_b = jnp.where(sel, _dot(vc, dog, NN), _dot(vp, dog, NN))
                ds_b = p_b * (dp_b - delta)
                ds_m = (p_m * (_dot(vm, dog, NN) - delta)).astype(BF16)
                dsk = jnp.exp(sink_s[:, lanes] - lse) * delta
                for j in range(GROUP):
                    dsink_ref[0, GROUP * g + j] += -jnp.sum(dsk[:, BLOCK * j:BLOCK * (j + 1)])
                ds_c = jnp.where(sel, ds_b, 0.0).astype(BF16)
                ds_p = jnp.where(sel, 0.0, ds_b).astype(BF16)
                p_c = jnp.where(sel, p_b, 0.0).astype(BF16)
                p_p = jnp.where(sel, 0.0, p_b).astype(BF16)
                dqs.append((_dot(kvc_t[ks, :], ds_c, NN) + _dot(kvp_t[ks, :], ds_p, NN)
                            + _dot(km, ds_m, TN)) * ATTN_SCALE)
                cur_s[:, ks] = _dot(ds_c, qg, NT)
                cur_s[:, vs] = _dot(p_c, dog, NT)
                prev_s[:, ks] = _dot(ds_p, qg, NT)
                prev_s[:, vs] = _dot(p_p, dog, NT)
                dkvm_ref[:, ks] += _dot(ds_m, qg, NT)
                dkvm_ref[:, vs] += _dot(p_m.astype(BF16), dog, NT)
            dq_ref[...] = _head_rows(dqs).astype(BF16).T
            dkv_ref[...] = (carry_s[...] + prev_s[...]).astype(BF16)
            carry_s[...] = cur_s[...]

        @pl.when(b == nb)
        def _():
            dkv_ref[...] = carry_s[...].astype(BF16)

        _hosted_wait(b, nb + 1, exchange)

    def at(off):
        return lambda b: (jnp.clip(b + off, 0, nb - 1), 0)

    blk = lambda cols, off=0: pl.BlockSpec((BLOCK, cols), at(off))
    res = _pcall(
        body, name="attn_bwd", grid=(nb + 1,),
        in_specs=[pl.BlockSpec(memory_space=pltpu.SMEM), blk(512), blk(256), blk(256, -1), _full((BLOCK, 256)),
                  blk(512), pl.BlockSpec((N_HEADS, BLOCK), lambda b: (0, jnp.minimum(b, nb - 1))), blk(512)]
        + [ANY] * nx,
        out_specs=[blk(512), blk(256, -1), _full((N_META, 256)), pl.BlockSpec(memory_space=pltpu.SMEM)]
        + [ANY] * nx,
        out_shape=[jax.ShapeDtypeStruct((R, 512), BF16), jax.ShapeDtypeStruct((R, 256), BF16),
                   jax.ShapeDtypeStruct((N_META, 256), F32), jax.ShapeDtypeStruct((1, N_HEADS), F32)]
        + _sibling_shapes(grads) + _chip_shapes(partials),
        scratch_shapes=_sem_pair(ng) + _sem_pair(3 * npart) + [pltpu.VMEM((BLOCK, 256), F32)] * 3
        + [pltpu.VMEM((BLOCK, N_HEADS * BLOCK), F32), pltpu.VMEM((1, N_HEADS * BLOCK), F32),
           pltpu.VMEM((N_HEADS, BLOCK), F32)],
        semantics=("arbitrary",),
    )(sinks, q, kv, kv, kv, o, lse, do, *grads, *partials)
    return res[:4], res[4:4 + ng], res[4 + ng:]


def _inproj_bwd(dh1, h0, g1, dq, dkv, dkvm, dca, dcg, w_in_t):
    R = h0.shape[0]
    nt = R // TM
    assert nt >= 2

    def body(dh1_ref, h0_ref, g_ref, dq_ref, dkv_ref, dkvm_ref, dca_ref, dcg_ref, w_ref,
             gx_hbm, dmeta_ref, dproj_ref, hn_ref, dg_ref, dx_s, gx_sems):
        i = pl.program_id(0)

        def gx_copy(step, slot, first):
            if first:
                return pltpu.make_async_copy(dx_s.at[slot, pl.ds(BLOCK, TM - BLOCK)],
                                             gx_hbm.at[pl.ds(0, TM - BLOCK)], gx_sems.at[slot])
            return pltpu.make_async_copy(
                dx_s.at[slot], gx_hbm.at[pl.ds(pl.multiple_of(step * TM - BLOCK, BLOCK), TM)], gx_sems.at[slot])

        @pl.when(i == 0)
        def _():
            dg_ref[...] = jnp.zeros_like(dg_ref)

        dproj_ref[:, 0:512] = dq_ref[...]
        dproj_ref[:, 512:768] = dkv_ref[...]
        dproj_ref[:, 768:1280] = dca_ref[...]
        dproj_ref[:, 1280:1792] = dcg_ref[...]

        @pl.when(i == 0)
        def _():
            dproj_ref[LEAD:BLOCK, 512:768] = dkvm_ref[...].astype(BF16)

        dhn = _dot(dproj_ref[...], w_ref[...], NN)
        h = h0_ref[...]
        r = _rms_stats(h)
        g = g_ref[...]
        hn_ref[...] = (h * r * g).astype(BF16)
        dx, dg = _rms_bwd(dhn, h, r, g)
        dg_ref[...] += dg
        slot = i % 2
        pl.when(i == 2)(lambda: gx_copy(0, 0, True).wait())
        pl.when(i > 2)(lambda: gx_copy(i - 2, slot, False).wait())
        dx_s[slot] = dh1_ref[...] + dx

        @pl.when(i == 0)
        def _():
            dmeta_ref[...] = dx_s[0, LEAD:BLOCK, :]
            gx_copy(0, 0, True).start()

        pl.when(i > 0)(lambda: gx_copy(i, slot, False).start())

        @pl.when(i == nt - 1)
        def _():
            gx_copy(nt - 2, (nt - 2) % 2, nt == 2).wait()
            gx_copy(nt - 1, (nt - 1) % 2, False).wait()

    return _pcall(
        body, name="inproj_bwd", grid=(nt,),
        in_specs=[_rows(TM, D_MODEL), _rows(TM, D_MODEL), _full((1, D_MODEL)), _rows(TM, 512), _rows(TM, 256),
                  _full((N_META, 256)), _rows(TM, 512), _rows(TM, 512), _full((1792, D_MODEL))],
        out_specs=[ANY, _full((N_META, D_MODEL)), _rows(TM, 1792), _rows(TM, D_MODEL), _full((1, D_MODEL))],
        out_shape=[jax.ShapeDtypeStruct((R - BLOCK, D_MODEL), F32), jax.ShapeDtypeStruct((N_META, D_MODEL), F32),
                   jax.ShapeDtypeStruct((R, 1792), BF16),
                   jax.ShapeDtypeStruct((R, D_MODEL), BF16), jax.ShapeDtypeStruct((1, D_MODEL), F32)],
        scratch_shapes=[pltpu.VMEM((2, TM, D_MODEL), F32), pltpu.SemaphoreType.DMA((2,))],
        semantics=("arbitrary",),
    )(dh1, h0, g1, dq, dkv, dkvm, dca, dcg, w_in_t)


ANY = pl.BlockSpec(memory_space=pl.ANY)


def _position():
    return lax.axis_index("x"), lax.axis_index("y"), lax.axis_index("c")


def _device_number(p):
    return 4 * p[0] + 2 * p[1] + p[2]


def _two_level_allgather(ins, outs, block, send_sems, recv_sems, local_sems, sem_base=0):
    n = len(ins)
    x, y, c = _position()
    me, sibling = (x, y, c), (x, y, 1 - c)
    chips = [(1 - x, y), (x, 1 - y), (1 - x, 1 - y)]

    def copy(w, k, origin, to, src=None):
        return pltpu.make_async_remote_copy(
            src_ref=block(w, origin) if src is None else src, dst_ref=block(w, origin),
            send_sem=send_sems.at[sem_base + 7 * w + k], recv_sem=recv_sems.at[sem_base + 7 * w + k],
            device_id=to, device_id_type=MESH)

    def mine(w):
        return pltpu.make_async_copy(ins[w], block(w, me), local_sems.at[w])

    def own(w):
        return [copy(w, 0, me, sibling, src=ins[w])] + [
            copy(w, 1 + j, me, (*chip, c), src=ins[w]) for j, chip in enumerate(chips)]

    def passed(w):
        return [copy(w, 4 + j, (*chip, c), sibling) for j, chip in enumerate(chips)]

    def start():
        for w in range(n):
            mine(w).start()
        for w in range(n):
            for cp in own(w):
                cp.start()

    def forward(w):
        fw = passed(w)
        for j, chip in enumerate(chips):
            copy(w, 1 + j, (*chip, c), me).wait_recv()
            fw[j].start()

    def finish():
        for w in range(n):
            copy(w, 0, sibling, me).wait_recv()
            for j, chip in enumerate(chips):
                copy(w, 4 + j, (*chip, 1 - c), me).wait_recv()
        for w in range(n):
            for cp in own(w) + passed(w):
                cp.wait_send()
            mine(w).wait()

    return start, forward, finish


def _carried_allgather(step, n_steps, shards, refs):
    ns = len(shards)
    ins, outs, (send_sems, recv_sems, local_sems) = refs[:ns], refs[ns:2 * ns], refs[2 * ns:]
    start, forward, finish = _two_level_allgather(
        ins, outs, _row_block(outs, [s.shape[0] for s in shards]), send_sems, recv_sems, local_sems)
    pl.when(step == 0)(start)
    total = sum(s.shape[0] for s in shards)
    sent = 0
    for w, s in enumerate(shards):
        sent += s.shape[0]
        pl.when(step == (AG_FORWARD_AT * sent * (n_steps - 1)) // (100 * total))(functools.partial(forward, w))
    return lambda: pl.when(step == n_steps - 1)(finish)


def _gathered_shapes(shards):
    return [jax.ShapeDtypeStruct((N_DEV * s.shape[0], s.shape[1]), s.dtype) for s in shards]


def _allgather_sems(ns):
    return _sem_pair(7 * ns) + [pltpu.SemaphoreType.DMA((ns,))]


def _blocking_allgather(ins, outs, block, send_sems, recv_sems, local_sems, sem_base=0):
    start, forward, finish = _two_level_allgather(ins, outs, block, send_sems, recv_sems, local_sems, sem_base)
    start()
    for w in range(len(ins)):
        forward(w)
    finish()


def _row_block(outs, rows):
    def block(w, p):
        return outs[w].at[pl.ds(pl.multiple_of(_device_number(p) * rows[w], 16), rows[w])]
    return block


def _sibling_copies(ins, outs, send_sems, recv_sems):
    x, y, c = _position()
    return [pltpu.make_async_remote_copy(
        src_ref=ins[w].at[:, 1 - c], dst_ref=outs[w], send_sem=send_sems.at[w], recv_sem=recv_sems.at[w],
        device_id=(x, y, 1 - c), device_id_type=MESH) for w in range(len(ins))]


def _chip_copies(ins, outs, send_sems, recv_sems):
    x, y, c = _position()
    chips = [(1 - x, y), (x, 1 - y), (1 - x, 1 - y)]
    return [pltpu.make_async_remote_copy(
        src_ref=ins[w].at[2 * chip[0] + chip[1]], dst_ref=outs[w].at[k],
        send_sem=send_sems.at[3 * w + k], recv_sem=recv_sems.at[3 * w + k],
        device_id=(*chip, c), device_id_type=MESH) for w in range(len(ins)) for k, chip in enumerate(chips)]


def _hosted(step, n_steps, make_copies):
    @pl.when(step == 0)
    def _():
        for cp in make_copies():
            cp.start()


def _hosted_wait(step, n_steps, make_copies):
    @pl.when(step == n_steps - 1)
    def _():
        for cp in make_copies():
            cp.wait()


def _sem_pair(n):
    return [pltpu.SemaphoreType.DMA((n,)), pltpu.SemaphoreType.DMA((n,))]


def _allgather_params(shards, small):
    arrays = list(shards) + list(small)
    n, ns = len(arrays), len(shards)

    def body(*refs):
        ins, outs = refs[:n], refs[n:2 * n]
        send_sems, recv_sems, local_sems = refs[2 * n:]

        rows = _row_block(outs, [a.shape[0] for a in arrays])

        def block(w, p):
            return rows(w, p) if w < ns else outs[w].at[_device_number(p)]

        _blocking_allgather(ins, outs, block, send_sems, recv_sems, local_sems)

    out_shape = [jax.ShapeDtypeStruct((N_DEV * a.shape[0], a.shape[1]), a.dtype) for a in shards]
    out_shape += [jax.ShapeDtypeStruct((N_DEV,) + a.shape, a.dtype) for a in small]
    return _pcall(
        body, name="allgather_params", in_specs=[ANY] * n, out_specs=[ANY] * n, out_shape=out_shape,
        scratch_shapes=[pltpu.SemaphoreType.DMA((7 * n,)), pltpu.SemaphoreType.DMA((7 * n,)),
                        pltpu.SemaphoreType.DMA((n,))],
    )(*arrays)


def _reduce_siblings(grads):
    n = len(grads)

    def body(*refs):
        ins, outs = refs[:n], refs[n:2 * n]
        send_sems, recv_sems = refs[2 * n:]
        copies = _sibling_copies(ins, outs, send_sems, recv_sems)
        for cp in copies:
            cp.start()
        for cp in copies:
            cp.wait()

    return _pcall(
        body, name="reduce_siblings", in_specs=[ANY] * n, out_specs=[ANY] * n,
        out_shape=_sibling_shapes(grads), scratch_shapes=_sem_pair(n),
    )(*grads)


def _sibling_shapes(grads):
    return [jax.ShapeDtypeStruct((4,) + g.shape[2:], F32) for g in grads]


def _chip_shapes(partials):
    return [jax.ShapeDtypeStruct((3,) + p.shape[1:], F32) for p in partials]


def _add_sibling(grad, received, core, name):
    _, _, r, cols = grad.shape

    def body(core_ref, g_ref, r_ref, o_ref):
        o_ref[...] = g_ref[...] + r_ref[...]

    return pl.pallas_call(
        body, name=name,
        grid_spec=pltpu.PrefetchScalarGridSpec(
            num_scalar_prefetch=1, grid=(4,),
            in_specs=[pl.BlockSpec((None, None, r, cols), lambda s, core_ref: (s, core_ref[0], 0, 0)),
                      pl.BlockSpec((None, r, cols), lambda s, core_ref: (s, 0, 0))],
            out_specs=pl.BlockSpec((None, r, cols), lambda s, core_ref: (s, 0, 0))),
        out_shape=jax.ShapeDtypeStruct((4, r, cols), F32),
        compiler_params=pltpu.CompilerParams(vmem_limit_bytes=VMEM_LIMIT),
    )(core, grad, received)


def _reduce_chips(partials, small):
    n, ns = len(partials), len(small)

    def body(*refs):
        p_ins, s_ins = refs[:n], refs[n:n + ns]
        p_outs, s_outs = refs[n + ns:2 * n + ns], refs[2 * n + ns:2 * (n + ns)]
        send_sems, recv_sems, local_sems = refs[2 * (n + ns):]
        copies = _chip_copies(p_ins, p_outs, send_sems, recv_sems)
        for cp in copies:
            cp.start()
        _blocking_allgather(s_ins, s_outs, lambda w, p: s_outs[w].at[_device_number(p)],
                            send_sems, recv_sems, local_sems, sem_base=3 * n)
        for cp in copies:
            cp.wait()

    out_shape = _chip_shapes(partials)
    out_shape += [jax.ShapeDtypeStruct((N_DEV,) + a.shape, a.dtype) for a in small]
    nsem = 3 * n + 7 * ns
    return _pcall(
        body, name="reduce_chips", in_specs=[ANY] * (n + ns), out_specs=[ANY] * (n + ns), out_shape=out_shape,
        scratch_shapes=[pltpu.SemaphoreType.DMA((nsem,)), pltpu.SemaphoreType.DMA((nsem,)),
                        pltpu.SemaphoreType.DMA((ns,))],
    )(*partials, *small)


def _adam(g, w, m, v):
    m = ADAM_B1 * m + (1.0 - ADAM_B1) * g
    v = ADAM_B2 * v + (1.0 - ADAM_B2) * (g * g)
    m_hat = m / (1.0 - ADAM_B1 ** ADAM_STEP)
    v_hat = v / (1.0 - ADAM_B2 ** ADAM_STEP)
    delta = -ADAM_LR * (m_hat / (jnp.sqrt(v_hat) + ADAM_EPS) + ADAM_WD * w)
    return delta, m, v


def _adamw(partial, received, slot, w, m, v, name):
    _, r, cols = partial.shape

    def body(slot_ref, p_ref, r_ref, w_ref, m_ref, v_ref, g_ref, d_ref, nm_ref, nv_ref):
        g = p_ref[...] + r_ref[0] + r_ref[1] + r_ref[2]
        g_ref[...] = g
        d_ref[...], nm_ref[...], nv_ref[...] = _adam(g, w_ref[...], m_ref[...], v_ref[...])

    whole = pl.BlockSpec((r, cols), lambda i, slot_ref: (0, 0))
    out = jax.ShapeDtypeStruct((r, cols), F32)
    return pl.pallas_call(
        body, name=name,
        grid_spec=pltpu.PrefetchScalarGridSpec(
            num_scalar_prefetch=1, grid=(1,),
            in_specs=[pl.BlockSpec((None, r, cols), lambda i, slot_ref: (slot_ref[0], 0, 0)),
                      pl.BlockSpec((3, r, cols), lambda i, slot_ref: (0, 0, 0)), whole, whole, whole],
            out_specs=[whole, whole, whole, whole]),
        out_shape=[out, out, out, out],
        compiler_params=pltpu.CompilerParams(vmem_limit_bytes=VMEM_LIMIT),
    )(slot, partial, received, w, m, v)


def _adamw_small(dev, ga, gb, gc, params):
    names = ["meta", "attn_norm", "sinks", "conv_w", "conv_b", "ln_g", "ln_b", "attn_out", "conv_out",
             "ffn_norm", "final_norm"]
    flat = [a for p in params for a in p]
    n_in = len(flat)

    def body(dev_ref, ga_ref, gb_ref, gc_ref, *refs):
        ins, outs = refs[:n_in], refs[n_in:n_in + 4 * len(names)]
        loss_ref, sb, sc = refs[n_in + 4 * len(names):]
        a = ga_ref[0]
        sb[...] = gb_ref[0]
        sc[...] = gc_ref[0]
        for d in range(1, N_DEV):
            a = a + ga_ref[d]
            sb[...] += gb_ref[d]
            sc[...] += gc_ref[d]
        dev = dev_ref[0]
        grads = {
            "attn_norm": a[0:1, :], "ffn_norm": a[1:2, :], "final_norm": a[2:3, :],
            "conv_b": a[3:4, 0:512], "ln_g": a[3:4, 512:1024], "ln_b": a[4:5, 0:512],
            "attn_out": a[4:5, 512:1024], "conv_out": a[5:6, 0:512], "sinks": a[5:6, 512:512 + N_HEADS],
            "meta": sb[pl.ds(pl.multiple_of(dev * N_META, N_META), N_META), :],
            "conv_w": sc[pl.ds(pl.multiple_of(dev * 32, 32), 32), :][0:CONV_K, :],
        }
        for idx, nm in enumerate(names):
            w_ref, m_ref, v_ref = ins[3 * idx:3 * idx + 3]
            g = grads[nm]
            delta, m, v = _adam(g, w_ref[...], m_ref[...], v_ref[...])
            o = outs[4 * idx:4 * idx + 4]
            o[0][...], o[1][...], o[2][...], o[3][...] = g, delta, m, v
        loss_ref[...] = a[6:7, 0:1]

    vm = pl.BlockSpec(memory_space=pltpu.VMEM)
    out_shape = [jax.ShapeDtypeStruct(p[0].shape, F32) for p in params for _ in range(4)]
    out_shape.append(jax.ShapeDtypeStruct((1, 1), F32))
    res = pl.pallas_call(
        body, name="adamw_small",
        grid_spec=pltpu.PrefetchScalarGridSpec(
            num_scalar_prefetch=1, grid=(1,),
            in_specs=[pl.BlockSpec(ga.shape, lambda i, d: (0, 0, 0)), pl.BlockSpec(gb.shape, lambda i, d: (0, 0, 0)),
                      pl.BlockSpec(gc.shape, lambda i, d: (0, 0, 0))]
            + [pl.BlockSpec(a.shape, lambda i, d: (0, 0)) for a in flat],
            out_specs=[pl.BlockSpec(s.shape, lambda i, d: (0, 0)) for s in out_shape],
            scratch_shapes=[pltpu.VMEM(gb.shape[1:], F32), pltpu.VMEM(gc.shape[1:], F32)]),
        out_shape=out_shape,
        compiler_params=pltpu.CompilerParams(vmem_limit_bytes=VMEM_LIMIT),
    )(dev, ga, gb, gc, *flat)
    return [res[4 * i:4 * i + 4] for i in range(len(names))], res[-1]


def kernel(x, meta_tokens, attn_norm_g, w_in, attn_sinks, conv_w, conv_b, conv_ln_g, conv_ln_b, attn_out_g, conv_out_g, w_out, ffn_norm_g, w_gate, w_up, w_down, final_norm_g, loss_target, m_meta_tokens, m_attn_norm_g, m_w_in, m_attn_sinks, m_conv_w, m_conv_b, m_conv_ln_g, m_conv_ln_b, m_attn_out_g, m_conv_out_g, m_w_out, m_ffn_norm_g, m_w_gate, m_w_up, m_w_down, m_final_norm_g, v_meta_tokens, v_attn_norm_g, v_w_in, v_attn_sinks, v_conv_w, v_conv_b, v_conv_ln_g, v_conv_ln_b, v_attn_out_g, v_conv_out_g, v_w_out, v_ffn_norm_g, v_w_gate, v_w_up, v_w_down, v_final_norm_g):
    xi, yi, ci = _position()
    dev = jnp.reshape(_device_number((xi, yi, ci)), (1,)).astype(jnp.int32)
    core = jnp.reshape(ci, (1,)).astype(jnp.int32)
    slot = jnp.reshape(2 * xi + yi, (1,)).astype(jnp.int32)

    w_in_t, meta_st, convw_st = _allgather_params([w_in[0].T.astype(BF16)], [meta_tokens, conv_w[0]])
    meta_full = jnp.transpose(meta_st, (1, 0, 2)).reshape(N_META, D_MODEL)
    convw_full = jnp.transpose(convw_st, (1, 0, 2)).reshape(CONV_K, CONV_W)

    final_g = final_norm_g.reshape(1, D_MODEL)

    (h0, q, kv, ca, cg), (w_out_b,) = _inproj_fwd(x[0], meta_full, attn_norm_g, w_in_t, [w_out[0].astype(BF16)])
    o_attn, lse, (wg_t, wu_t) = _attn_fwd(
        q, kv, attn_sinks, [w_gate[0].T.astype(BF16), w_up[0].T.astype(BF16)])
    (o_conv, y_conv), (wd_b,) = _conv_fwd(ca, cg, convw_full, conv_b, conv_ln_g, conv_ln_b, [w_down[0].astype(BF16)])
    h1 = _outproj_fwd(h0, o_attn, o_conv, attn_out_g, conv_out_g, w_out_b)
    gate, up, act, dh2, loss_sum, dg_final = _ffn_fwd(h1, ffn_norm_g, wg_t, wu_t, wd_b, final_g, loss_target[0])

    def blocks(g):
        return g.reshape(4, 2, g.shape[0] // N_DEV, D_MODEL)

    def add_siblings(grads, received, tags):
        return [_add_sibling(g, r, core, "add_sibling_" + t) for g, r, t in zip(grads, received, tags)]

    dgate, dup, hn2, dh1, dg_ffn = _ffn_bwd(dh2, h1, ffn_norm_g, gate, up, wg_t, wu_t, wd_b)
    ffn_grads = [blocks(_wgrad(dgate, hn2, FF_CHUNK, "wgrad_gate")), blocks(_wgrad(dup, hn2, FF_CHUNK, "wgrad_up")),
                 blocks(_wgrad(act, dh2, FF_CHUNK, "wgrad_down"))]
    (do_attn, do_conv, mixed, dg_ao, dg_co), ffn_sib = _outproj_bwd(
        dh1, o_attn, o_conv, attn_out_g, conv_out_g, w_out_b, ffn_grads)
    ffn_sums = add_siblings(ffn_grads, ffn_sib, ("gate", "up", "down"))
    out_grads = [blocks(_wgrad(mixed, dh1, D_MODEL, "wgrad_out"))]
    (dca, dcg, dconvw, dconvb, dln_g, dln_b), gate_up_chips = _conv_bwd(
        do_conv, y_conv, ca, cg, convw_full, conv_ln_g, conv_ln_b, ffn_sums[:2])
    (dq, dkv, dkvm, dsinks), out_sib, down_chips = _attn_bwd(
        q, kv, attn_sinks, o_attn, lse, do_attn, out_grads, ffn_sums[2:])
    ffn_chips = list(gate_up_chips) + list(down_chips)
    out_sums = add_siblings(out_grads, out_sib, ("out",))
    grad_x, dmeta, dproj, hn1, dg_attn = _inproj_bwd(dh1, h0, attn_norm_g, dq, dkv, dkvm, dca, dcg, w_in_t)
    dwi_t, out_chips = _wgrad(dproj, hn1, 1792, "wgrad_in", out_sums)
    in_grads = [blocks(dwi_t)]
    in_sums = add_siblings(in_grads, _reduce_siblings(in_grads), ("in",))
    small_a = jnp.concatenate([
        dg_attn, dg_ffn, dg_final, jnp.concatenate([dconvb, dln_g], axis=1), jnp.concatenate([dln_b, dg_ao], axis=1),
        jnp.concatenate([dg_co, dsinks, jnp.zeros((1, 512 - N_HEADS), F32)], axis=1),
        jnp.concatenate([loss_sum[0:1, :], jnp.zeros((1, D_MODEL - 128), F32)], axis=1),
        jnp.zeros((1, D_MODEL), F32)], axis=0)
    small_b = jnp.transpose(dmeta.reshape(N_META, N_DEV, 128), (1, 0, 2)).reshape(N_DEV * N_META, 128)
    small_c = jnp.transpose(dconvw.reshape(32, N_DEV, 64), (1, 0, 2)).reshape(N_DEV * 32, 64)
    in_chips, ga, gb, gc = _reduce_chips(in_sums, [small_a, small_b, small_c])
    tags = ("in", "out", "gate", "up", "down")
    chip_sums = in_sums + out_sums + ffn_sums
    from_chips = [in_chips] + list(out_chips) + list(ffn_chips)

    big = [(True, w_in, m_w_in, v_w_in), (False, w_out, m_w_out, v_w_out), (True, w_gate, m_w_gate, v_w_gate),
           (True, w_up, m_w_up, v_w_up), (False, w_down, m_w_down, v_w_down)]
    big_out = {}
    for t, p, r, (transposed, w, m, v) in zip(tags, chip_sums, from_chips, big):
        rows = (lambda a: jnp.transpose(a[0])) if transposed else (lambda a: a[0])
        back = (lambda a: jnp.transpose(a)[None]) if transposed else (lambda a: a[None])
        big_out[t] = [back(a) for a in _adamw(p, r, slot, rows(w), rows(m), rows(v), "adamw_" + t)]

    small_params = [
        (meta_tokens, m_meta_tokens, v_meta_tokens), (attn_norm_g, m_attn_norm_g, v_attn_norm_g),
        (attn_sinks, m_attn_sinks, v_attn_sinks), (conv_w[0], m_conv_w[0], v_conv_w[0]),
        (conv_b, m_conv_b, v_conv_b), (conv_ln_g, m_conv_ln_g, v_conv_ln_g), (conv_ln_b, m_conv_ln_b, v_conv_ln_b),
        (attn_out_g, m_attn_out_g, v_attn_out_g), (conv_out_g, m_conv_out_g, v_conv_out_g),
        (ffn_norm_g, m_ffn_norm_g, v_ffn_norm_g),
        (final_g, m_final_norm_g.reshape(1, D_MODEL), v_final_norm_g.reshape(1, D_MODEL))]
    sm, loss = _adamw_small(dev, ga, gb, gc, small_params)
    sm[3] = [a[None] for a in sm[3]]
    sm[10] = [a.reshape(D_MODEL) for a in sm[10]]

    per_param = [sm[0], sm[1], big_out["in"], sm[2], sm[3], sm[4], sm[5], sm[6], sm[7], sm[8], big_out["out"],
                 sm[9], big_out["gate"], big_out["up"], big_out["down"], sm[10]]
    loss = loss.reshape(())
    outs = [loss, grad_x[None]]
    for kind in range(4):
        outs += [p[kind] for p in per_param]
    return tuple(outs)
```

```python
import functools
import math

import jax
import jax.numpy as jnp
from jax import lax
from jax.experimental import pallas as pl
from jax.experimental.pallas import tpu as pltpu

F32, BF16 = jnp.float32, jnp.bfloat16
MESH = pl.DeviceIdType.MESH

D_MODEL = 1024
N_META = 16
BLOCK = 128
LEAD = BLOCK - N_META
HEAD_DIM = 64
N_HEADS = 8
GROUP = 4
ATTN_W = 512
KV_W = 128
CONV_W = 512
CONV_K = 31
HALO = 32
D_FF = 2816
FF_CHUNK = D_FF // 2
FF_SUB = [slice(s, s + 256) for s in range(0, D_FF, 256)]
N_DEV = 8
EPS = 1e-5
NEG = -1e30
TM = 640
AG_FORWARD_AT = 85
WGRAD_K_TILES = 5
CONV_ROWS = 32
VMEM_LIMIT = 60 * 1024 * 1024

ADAM_LR, ADAM_B1, ADAM_B2, ADAM_EPS, ADAM_WD, ADAM_STEP = 0.001, 0.9, 0.999, 1e-08, 0.01, 10

NT = (((1,), (1,)), ((), ()))
NN = (((1,), (0,)), ((), ()))
TN = (((0,), (0,)), ((), ()))


def _dot(a, b, dims):
    return lax.dot_general(a, b, dims, preferred_element_type=F32)


def _sigmoid(x):
    return 1.0 / (1.0 + jnp.exp(-x))


def _pcall(body, *, name, out_shape, grid=None, in_specs=None, out_specs=None, scratch_shapes=(),
           semantics=None, **kw):
    params = dict(vmem_limit_bytes=VMEM_LIMIT)
    if semantics is not None:
        params["dimension_semantics"] = semantics
    extra = {}
    if grid is not None:
        extra["grid"] = grid
    if in_specs is not None:
        extra["in_specs"] = in_specs
    if out_specs is not None:
        extra["out_specs"] = out_specs
    return pl.pallas_call(body, name=name, out_shape=out_shape, scratch_shapes=list(scratch_shapes),
                          compiler_params=pltpu.CompilerParams(**params), **extra, **kw)


def _rows(tm, cols, off=0):
    return pl.BlockSpec((tm, cols), lambda i, *_: (i + off, 0))


def _full(shape):
    nd = len(shape)
    return pl.BlockSpec(shape, lambda *_: (0,) * nd)


def _rms_stats(x):
    return lax.rsqrt(jnp.mean(x * x, axis=-1, keepdims=True) + EPS)


def _rms_bwd(dy, x, r, g):
    t = dy * g
    dx = r * (t - x * (r * r) * jnp.mean(t * x, axis=-1, keepdims=True))
    dg = jnp.sum(dy * x * r, axis=0, keepdims=True)
    return dx, dg


def _inproj_fwd(x, meta, g1, w_in_t, shards):
    R = x.shape[0] + BLOCK
    nt = R // TM
    ns = len(shards)
    assert nt >= 2

    def body(x_hbm, meta_ref, g_ref, w_ref, *refs):
        ag_ins, (h0_ref, q_ref, kv_ref, ca_ref, cg_ref), ag_outs = refs[:ns], refs[ns:ns + 5], refs[ns + 5:2 * ns + 5]
        ag_sems, (x_s, sems) = refs[2 * ns + 5:2 * ns + 8], refs[2 * ns + 8:]
        i = pl.program_id(0)
        slot = i % 2
        ag_finish = _carried_allgather(i, nt, shards, ag_ins + ag_outs + ag_sems)

        def x_copy(step, slot, first):
            if first:
                return pltpu.make_async_copy(x_hbm.at[pl.ds(0, TM - BLOCK)],
                                             x_s.at[slot, pl.ds(BLOCK, TM - BLOCK)], sems.at[slot])
            return pltpu.make_async_copy(
                x_hbm.at[pl.ds(pl.multiple_of(step * TM - BLOCK, BLOCK), TM)], x_s.at[slot], sems.at[slot])

        @pl.when(i == 0)
        def _():
            x_copy(0, 0, True).start()
            x_s[0, 0:LEAD, :] = jnp.zeros((LEAD, D_MODEL), F32)
            x_s[0, LEAD:BLOCK, :] = meta_ref[...]

        pl.when(i + 1 < nt)(lambda: x_copy(i + 1, 1 - slot, False).start())
        pl.when(i == 0)(lambda: x_copy(0, 0, True).wait())
        pl.when(i > 0)(lambda: x_copy(i, slot, False).wait())
        h = x_s[slot]
        h0_ref[...] = h
        hn = (h * _rms_stats(h) * g_ref[...]).astype(BF16)
        q_ref[...] = _dot(hn, w_ref[0:512, :], NT).astype(BF16)
        kv_ref[...] = _dot(hn, w_ref[512:768, :], NT).astype(BF16)
        ca_ref[...] = _dot(hn, w_ref[768:1280, :], NT)
        cg_ref[...] = _dot(hn, w_ref[1280:1792, :], NT)
        ag_finish()

    anywhere = pl.BlockSpec(memory_space=pl.ANY)
    res = _pcall(
        body, name="inproj_fwd", grid=(nt,),
        in_specs=[anywhere, _full((N_META, D_MODEL)), _full((1, D_MODEL)), _full((1792, D_MODEL))] + [anywhere] * ns,
        out_specs=[_rows(TM, D_MODEL), _rows(TM, 512), _rows(TM, 256), _rows(TM, 512), _rows(TM, 512)]
        + [anywhere] * ns,
        out_shape=[jax.ShapeDtypeStruct((R, D_MODEL), F32),
                   jax.ShapeDtypeStruct((R, 512), BF16), jax.ShapeDtypeStruct((R, 256), BF16),
                   jax.ShapeDtypeStruct((R, 512), F32), jax.ShapeDtypeStruct((R, 512), F32)]
        + _gathered_shapes(shards),
        scratch_shapes=_allgather_sems(ns) + [pltpu.VMEM((2, TM, D_MODEL), F32), pltpu.SemaphoreType.DMA((2,))],
        semantics=("arbitrary",),
    )(x, meta, g1, w_in_t, *shards)
    return res[:5], res[5:]


GB = GROUP * BLOCK
ATTN_SCALE = 1.0 / math.sqrt(HEAD_DIM)


def _group_lanes(xt, g):
    return jnp.concatenate(
        [xt[HEAD_DIM * (GROUP * g + j):HEAD_DIM * (GROUP * g + j + 1), :] for j in range(GROUP)], axis=1)


def _head_lanes(ref, g):
    return jnp.concatenate([ref[GROUP * g + j:GROUP * g + j + 1, :] for j in range(GROUP)], axis=1)


def _head_rows(xs):
    return jnp.concatenate([x[:, BLOCK * j:BLOCK * (j + 1)] for x in xs for j in range(GROUP)], axis=0)


def _attn_tables(sink_ref, bias_s, sink_s):
    kk = lax.broadcasted_iota(jnp.int32, (BLOCK, BLOCK), 0)
    ii = lax.broadcasted_iota(jnp.int32, (BLOCK, BLOCK), 1)
    dist = jnp.where(kk <= ii, ii - kk, ii - kk + BLOCK).astype(F32)
    for h in range(N_HEADS):
        bias_s[:, BLOCK * h:BLOCK * (h + 1)] = dist * -(2.0 ** -(h + 1))
        sink_s[:, BLOCK * h:BLOCK * (h + 1)] = jnp.zeros((1, BLOCK), F32) + sink_ref[0, h]


def _attn_masks(b):
    kk = lax.broadcasted_iota(jnp.int32, (BLOCK, GB), 0)
    ii = lax.broadcasted_iota(jnp.int32, (BLOCK, GB), 1) & (BLOCK - 1)
    sel = kk <= ii
    pen = jnp.where(sel, jnp.where(b >= 1, 0.0, NEG), jnp.where(b >= 2, 0.0, NEG))
    mj = lax.broadcasted_iota(jnp.int32, (N_META, GB), 0)
    mi = lax.broadcasted_iota(jnp.int32, (N_META, GB), 1) & (BLOCK - 1)
    pen_m = jnp.where((mj + LEAD) <= (mi + b * BLOCK), 0.0, NEG)
    return sel, pen, pen_m


def _attn_scores(qt, kc, kp, km, sel, pen, pen_m, bias):
    s_b = jnp.where(sel, _dot(kc, qt, NN), _dot(kp, qt, NN)) + bias + pen
    s_m = _dot(km, qt, NN) + pen_m
    return s_b, s_m


def _attn_fwd(q, kv, sinks, shards):
    R = q.shape[0]
    nb = R // BLOCK
    ns = len(shards)

    def body(sink_ref, q_ref, kvc_ref, kvp_ref, kvm_ref, *refs):
        ag_ins, (o_ref, lse_ref), ag_outs = refs[:ns], refs[ns:ns + 2], refs[ns + 2:2 * ns + 2]
        ag_sems, (bias_s, sink_s) = refs[2 * ns + 2:2 * ns + 5], refs[2 * ns + 5:]
        b = pl.program_id(0)
        ag_finish = _carried_allgather(b, nb, shards, ag_ins + ag_outs + ag_sems)
        pl.when(b == 0)(functools.partial(_attn_tables, sink_ref, bias_s, sink_s))
        sel, pen, pen_m = _attn_masks(b)
        q_t = (q_ref[...] * ATTN_SCALE).T
        kvc_t, kvp_t = kvc_ref[...].T, kvp_ref[...].T
        outs = []
        for g in range(N_HEADS // GROUP):
            ks, vs = slice(HEAD_DIM * g, HEAD_DIM * (g + 1)), slice(KV_W + HEAD_DIM * g, KV_W + HEAD_DIM * (g + 1))
            lanes = slice(GB * g, GB * (g + 1))
            s_b, s_m = _attn_scores(_group_lanes(q_t, g), kvc_ref[:, ks], kvp_ref[:, ks], kvm_ref[LEAD:BLOCK, ks],
                                    sel, pen, pen_m, bias_s[:, lanes])
            sink = sink_s[:, lanes]
            m = jnp.maximum(jnp.maximum(jnp.max(s_b, axis=0, keepdims=True),
                                        jnp.max(s_m, axis=0, keepdims=True)), sink)
            p_b = jnp.exp(s_b - m)
            p_m = jnp.exp(s_m - m)
            l = jnp.sum(p_b, axis=0, keepdims=True) + jnp.sum(p_m, axis=0, keepdims=True) + jnp.exp(sink - m)
            p_c = jnp.where(sel, p_b, 0.0).astype(BF16)
            p_p = jnp.where(sel, 0.0, p_b).astype(BF16)
            o_t = (_dot(kvc_t[vs, :], p_c, NN) + _dot(kvp_t[vs, :], p_p, NN)
                   + _dot(kvm_ref[LEAD:BLOCK, vs], p_m.astype(BF16), TN))
            outs.append(o_t / l)
            lse = m + jnp.log(l)
            for j in range(GROUP):
                lse_ref[GROUP * g + j:GROUP * g + j + 1, :] = lse[:, BLOCK * j:BLOCK * (j + 1)]
        o_ref[...] = _head_rows(outs).T
        ag_finish()

    res = _pcall(
        body, name="attn_fwd", grid=(nb,),
        in_specs=[pl.BlockSpec(memory_space=pltpu.SMEM),
                  _rows(BLOCK, 512), _rows(BLOCK, 256),
                  pl.BlockSpec((BLOCK, 256), lambda b: (jnp.maximum(b - 1, 0), 0)),
                  _full((BLOCK, 256))] + [ANY] * ns,
        out_specs=[_rows(BLOCK, 512), pl.BlockSpec((N_HEADS, BLOCK), lambda b: (0, b))] + [ANY] * ns,
        out_shape=[jax.ShapeDtypeStruct((R, 512), F32), jax.ShapeDtypeStruct((N_HEADS, R), F32)]
        + _gathered_shapes(shards),
        scratch_shapes=_allgather_sems(ns) + [pltpu.VMEM((BLOCK, N_HEADS * BLOCK), F32),
                                              pltpu.VMEM((1, N_HEADS * BLOCK), F32)],
        semantics=("arbitrary",),
    )(sinks, q, kv, kv, kv, *shards)
    return res[0], res[1], res[2:]


def _ln_silu(y, lg, lb):
    mu = jnp.mean(y, axis=-1, keepdims=True)
    xc = y - mu
    rstd = lax.rsqrt(jnp.mean(xc * xc, axis=-1, keepdims=True) + EPS)
    xhat = xc * rstd
    yn = xhat * lg + lb
    return yn, xhat, rstd


PHASE_ROWS = HALO + TM - 8


def _phase_scratch():
    return pltpu.VMEM((7, PHASE_ROWS, CONV_W), F32)


def _phase_copies(src_s, ph_s):
    for b in range(1, 8):
        ph_s[b - 1] = src_s[pl.ds(b, PHASE_ROWS), :]


def _shifted(src_s, ph_s, start, rows):
    a8, b = (start // 8) * 8, start % 8
    if b == 0:
        return src_s[pl.ds(a8, rows), :]
    return ph_s[b - 1, pl.ds(a8, rows), :]


def _conv_fwd(ca, cg, conv_w, conv_b, ln_g, ln_b, shards):
    R = ca.shape[0]
    nt = R // TM
    hpt = TM // HALO
    ns = len(shards)

    def body(ca_ref, cg_ref, cah_ref, cgh_ref, w_ref, b_ref, lg_ref, lb_ref, *refs):
        ag_ins, (oc_ref, y_ref), ag_outs = refs[:ns], refs[ns:ns + 2], refs[ns + 2:2 * ns + 2]
        ag_sems, (u_s, uph_s) = refs[2 * ns + 2:2 * ns + 5], refs[2 * ns + 5:]
        i = pl.program_id(0)
        ag_finish = _carried_allgather(i, nt, shards, ag_ins + ag_outs + ag_sems)
        u_s[HALO:HALO + TM, :] = ca_ref[...] * _sigmoid(cg_ref[...])
        u_s[0:HALO, :] = jnp.where(i > 0, cah_ref[...] * _sigmoid(cgh_ref[...]), 0.0)
        _phase_copies(u_s, uph_s)
        for rc in range(TM // CONV_ROWS):
            base = rc * CONV_ROWS + HALO - (CONV_K - 1)
            acc = jnp.zeros((CONV_ROWS, CONV_W), F32) + b_ref[...]
            for k in range(CONV_K):
                acc = acc + _shifted(u_s, uph_s, base + k, CONV_ROWS) * w_ref[k:k + 1, :]
            rows = slice(rc * CONV_ROWS, (rc + 1) * CONV_ROWS)
            y_ref[rows, :] = acc
            yn, _, _ = _ln_silu(acc, lg_ref[...], lb_ref[...])
            oc_ref[rows, :] = yn * _sigmoid(yn)
        ag_finish()

    prev_halo = pl.BlockSpec((HALO, CONV_W), lambda i: (jnp.maximum(i * hpt - 1, 0), 0))
    anywhere = pl.BlockSpec(memory_space=pl.ANY)
    res = _pcall(
        body, name="conv_fwd", grid=(nt,),
        in_specs=[_rows(TM, CONV_W), _rows(TM, CONV_W), prev_halo, prev_halo,
                  _full((CONV_K, CONV_W)), _full((1, CONV_W)), _full((1, CONV_W)), _full((1, CONV_W))]
        + [anywhere] * ns,
        out_specs=[_rows(TM, CONV_W), _rows(TM, CONV_W)] + [anywhere] * ns,
        out_shape=[jax.ShapeDtypeStruct((R, CONV_W), F32), jax.ShapeDtypeStruct((R, CONV_W), F32)]
        + _gathered_shapes(shards),
        scratch_shapes=_allgather_sems(ns) + [pltpu.VMEM((HALO + TM, CONV_W), F32), _phase_scratch()],
        semantics=("arbitrary",),
    )(ca, cg, ca, cg, conv_w, conv_b, ln_g, ln_b, *shards)
    return res[:2], res[2:]


def _outproj_fwd(h0, o_attn, o_conv, ga, gc, w_out):
    R = h0.shape[0]

    def body(h_ref, oa_ref, oc_ref, ga_ref, gc_ref, w_ref, h1_ref):
        oa, oc = oa_ref[...], oc_ref[...]
        ma = (oa * _rms_stats(oa) * ga_ref[...]).astype(BF16)
        mc = (oc * _rms_stats(oc) * gc_ref[...]).astype(BF16)
        h1_ref[...] = h_ref[...] + _dot(ma, w_ref[0:512, :], NN) + _dot(mc, w_ref[512:1024, :], NN)

    return _pcall(
        body, name="outproj_fwd", grid=(R // TM,),
        in_specs=[_rows(TM, D_MODEL), _rows(TM, 512), _rows(TM, 512), _full((1, 512)), _full((1, 512)),
                  _full((D_MODEL, D_MODEL))],
        out_specs=_rows(TM, D_MODEL),
        out_shape=jax.ShapeDtypeStruct((R, D_MODEL), F32),
        semantics=("parallel",),
    )(h0, o_attn, o_conv, ga, gc, w_out)


def _target_copy(tgt_hbm, tgt_s, sem, i, first):
    if first:
        return pltpu.make_async_copy(tgt_hbm.at[pl.ds(0, TM - BLOCK)], tgt_s.at[pl.ds(BLOCK, TM - BLOCK)], sem)
    return pltpu.make_async_copy(tgt_hbm.at[pl.ds(i * TM - BLOCK, TM)], tgt_s, sem)


def _resident(shape):
    nd = len(shape)
    return pl.BlockSpec(shape, lambda *_: (0,) * nd, pipeline_mode=pl.Buffered(1))


def _ffn_fwd(h1, g2, wg_t, wu_t, wd, gf, target):
    R = h1.shape[0]
    nt = R // TM

    def body(h1_ref, g2_ref, wg_ref, wu_ref, wd_ref, gf_ref, tgt_hbm,
             gate_ref, up_ref, act_s, dh2_ref, dh2b_ref, loss_ref, dgf_ref, tgt_s, sem):
        i = pl.program_id(0)

        @pl.when(i == 0)
        def _():
            loss_ref[...] = jnp.zeros_like(loss_ref)
            dgf_ref[...] = jnp.zeros_like(dgf_ref)
            tgt_s[0:BLOCK, :] = jnp.zeros((BLOCK, D_MODEL), F32)
            _target_copy(tgt_hbm, tgt_s, sem, i, True).start()

        pl.when(i > 0)(lambda: _target_copy(tgt_hbm, tgt_s, sem, i, False).start())
        h1 = h1_ref[...]
        hn = (h1 * _rms_stats(h1) * g2_ref[...]).astype(BF16)
        for cs in FF_SUB:
            gate = _dot(hn, wg_ref[cs, :], NT)
            up = _dot(hn, wu_ref[cs, :], NT)
            gate_ref[:, cs] = gate.astype(BF16)
            up_ref[:, cs] = up.astype(BF16)
            act_s[:, cs] = (gate * _sigmoid(gate) * up).astype(BF16)
        part = _dot(act_s[...], wd_ref[...], NN)
        pl.when(i == 0)(lambda: _target_copy(tgt_hbm, tgt_s, sem, i, True).wait())
        pl.when(i > 0)(lambda: _target_copy(tgt_hbm, tgt_s, sem, i, False).wait())
        h2 = h1 + part
        rf = _rms_stats(h2)
        gf = gf_ref[...]
        row = lax.broadcasted_iota(jnp.int32, (TM, 1), 0) + i * TM
        err = jnp.where(row >= BLOCK, h2 * rf * gf - tgt_s[...], 0.0)
        dy = err * (1.0 / D_MODEL)
        dh2, dgf = _rms_bwd(dy, h2, rf, gf)
        dh2_ref[...] = dh2
        dh2b_ref[...] = dh2.astype(BF16)
        loss_ref[...] += (0.5 / D_MODEL) * jnp.sum(err * err)
        dgf_ref[...] += dgf

    wspec = _resident((D_FF, D_MODEL))
    return _pcall(
        body, name="ffn_fwd", grid=(nt,),
        in_specs=[_rows(TM, D_MODEL), _full((1, D_MODEL)), wspec, wspec, wspec, _full((1, D_MODEL)),
                  pl.BlockSpec(memory_space=pl.ANY)],
        out_specs=[_rows(TM, D_FF), _rows(TM, D_FF), _rows(TM, D_FF), _rows(TM, D_MODEL), _rows(TM, D_MODEL),
                   _full((8, 128)), _full((1, D_MODEL))],
        out_shape=[jax.ShapeDtypeStruct((R, D_FF), BF16)] * 3
        + [jax.ShapeDtypeStruct((R, D_MODEL), F32), jax.ShapeDtypeStruct((R, D_MODEL), BF16),
           jax.ShapeDtypeStruct((8, 128), F32), jax.ShapeDtypeStruct((1, D_MODEL), F32)],
        scratch_shapes=[pltpu.VMEM((TM, D_MODEL), F32), pltpu.SemaphoreType.DMA],
        semantics=("arbitrary",),
    )(h1, g2, wg_t, wu_t, wd, gf, target)


def _ffn_bwd(dh2, dh2b, h1, g2, gate, up, wg_t, wu_t, wd):
    R = h1.shape[0]
    nt = R // TM
    act_shape = jax.ShapeDtypeStruct((R, D_FF), BF16)

    def act_body(dh2_ref, gate_ref, up_ref, wd_ref, dgate_ref, dup_ref):
        dhb = dh2_ref[...]
        for cs in FF_SUB:
            dact = _dot(dhb, wd_ref[cs, :], NT)
            gate = gate_ref[:, cs].astype(F32)
            up = up_ref[:, cs].astype(F32)
            sig = _sigmoid(gate)
            dgate_ref[:, cs] = (dact * up * (sig * (1.0 + gate * (1.0 - sig)))).astype(BF16)
            dup_ref[:, cs] = (dact * (gate * sig)).astype(BF16)

    dgate, dup = _pcall(
        act_body, name="ffn_bwd_act", grid=(nt,),
        in_specs=[_rows(TM, D_MODEL), _rows(TM, D_FF), _rows(TM, D_FF), _resident((D_FF, D_MODEL))],
        out_specs=[_rows(TM, D_FF), _rows(TM, D_FF)], out_shape=[act_shape, act_shape],
        semantics=("parallel",),
    )(dh2b, gate, up, wd)

    def in_body(dh2_ref, h1_ref, g2_ref, dgate_ref, dup_ref, wg_ref, wu_ref, hn_ref, dh1_ref, dg2_ref):
        @pl.when(pl.program_id(0) == 0)
        def _():
            dg2_ref[...] = jnp.zeros_like(dg2_ref)

        dhn = _dot(dgate_ref[...], wg_ref[...], NN) + _dot(dup_ref[...], wu_ref[...], NN)
        h1 = h1_ref[...]
        r = _rms_stats(h1)
        g2 = g2_ref[...]
        hn_ref[...] = (h1 * r * g2).astype(BF16)
        dx, dg = _rms_bwd(dhn, h1, r, g2)
        dh1_ref[...] = dh2_ref[...] + dx
        dg2_ref[...] += dg

    hn2, dh1, dg2 = _pcall(
        in_body, name="ffn_bwd_in", grid=(nt,),
        in_specs=[_rows(TM, D_MODEL), _rows(TM, D_MODEL), _full((1, D_MODEL)), _rows(TM, D_FF), _rows(TM, D_FF),
                  _resident((D_FF, D_MODEL)), _resident((D_FF, D_MODEL))],
        out_specs=[_rows(TM, D_MODEL), _rows(TM, D_MODEL), _full((1, D_MODEL))],
        out_shape=[jax.ShapeDtypeStruct((R, D_MODEL), BF16), jax.ShapeDtypeStruct((R, D_MODEL), F32),
                   jax.ShapeDtypeStruct((1, D_MODEL), F32)],
        semantics=("arbitrary",),
    )(dh2, h1, g2, dgate, dup, wg_t, wu_t)
    return dgate, dup, hn2, dh1, dg2


def _wgrad(a, b, tm, name, partials=()):
    K, M = a.shape
    N = b.shape[1]
    tk = K // WGRAD_K_TILES if K % (WGRAD_K_TILES * BLOCK) == 0 else TM
    nm, nk, npart = M // tm, K // tk, len(partials)

    def body(a_ref, b_ref, *refs):
        p_ins, o_ref, p_outs, sems = refs[:npart], refs[npart], refs[npart + 1:2 * npart + 1], refs[2 * npart + 1:]
        step = pl.program_id(0) * nk + pl.program_id(1)
        exchange = functools.partial(_chip_copies, p_ins, p_outs, *sems)
        if npart:
            _hosted(step, nm * nk, exchange)

        @pl.when(pl.program_id(1) == 0)
        def _():
            o_ref[...] = jnp.zeros_like(o_ref)

        o_ref[...] += _dot(a_ref[...], b_ref[...].astype(BF16), TN)
        if npart:
            _hosted_wait(step, nm * nk, exchange)

    res = _pcall(
        body, name=name, grid=(nm, nk),
        in_specs=[pl.BlockSpec((tk, tm), lambda m, k: (k, m)), pl.BlockSpec((tk, N), lambda m, k: (k, 0))]
        + [ANY] * npart,
        out_specs=[pl.BlockSpec((tm, N), lambda m, k: (m, 0))] + [ANY] * npart,
        out_shape=[jax.ShapeDtypeStruct((M, N), F32)] + _chip_shapes(partials),
        scratch_shapes=_sem_pair(3 * npart) if npart else [],
        semantics=("arbitrary", "arbitrary"),
    )(a, b, *partials)
    return (res[0], res[1:]) if npart else res[0]


def _outproj_bwd(dh1, o_attn, o_conv, ga, gc, w_out, grads):
    R = dh1.shape[0]
    nt, ng = R // TM, len(grads)

    def body(dh1_ref, oa_ref, oc_ref, ga_ref, gc_ref, w_ref, *refs):
        g_ins, (doa_ref, doc_ref, mixed_ref, dga_ref, dgc_ref) = refs[:ng], refs[ng:ng + 5]
        g_outs, (send_sems, recv_sems) = refs[ng + 5:2 * ng + 5], refs[2 * ng + 5:]
        exchange = functools.partial(_sibling_copies, g_ins, g_outs, send_sems, recv_sems)
        _hosted(pl.program_id(0), nt, exchange)

        @pl.when(pl.program_id(0) == 0)
        def _():
            dga_ref[...] = jnp.zeros_like(dga_ref)
            dgc_ref[...] = jnp.zeros_like(dgc_ref)

        dm = _dot(dh1_ref[...].astype(BF16), w_ref[...], NT)
        oa, oc = oa_ref[...], oc_ref[...]
        ra, rc = _rms_stats(oa), _rms_stats(oc)
        mixed_ref[:, 0:512] = (oa * ra * ga_ref[...]).astype(BF16)
        mixed_ref[:, 512:1024] = (oc * rc * gc_ref[...]).astype(BF16)
        doa, dga = _rms_bwd(dm[:, 0:512], oa, ra, ga_ref[...])
        doc, dgc = _rms_bwd(dm[:, 512:1024], oc, rc, gc_ref[...])
        doa_ref[...] = doa
        doc_ref[...] = doc
        dga_ref[...] += dga
        dgc_ref[...] += dgc
        _hosted_wait(pl.program_id(0), nt, exchange)

    res = _pcall(
        body, name="outproj_bwd", grid=(nt,),
        in_specs=[_rows(TM, D_MODEL), _rows(TM, 512), _rows(TM, 512), _full((1, 512)), _full((1, 512)),
                  _full((D_MODEL, D_MODEL))] + [ANY] * ng,
        out_specs=[_rows(TM, 512), _rows(TM, 512), _rows(TM, D_MODEL), _full((1, 512)), _full((1, 512))]
        + [ANY] * ng,
        out_shape=[jax.ShapeDtypeStruct((R, 512), F32), jax.ShapeDtypeStruct((R, 512), F32),
                   jax.ShapeDtypeStruct((R, D_MODEL), BF16),
                   jax.ShapeDtypeStruct((1, 512), F32), jax.ShapeDtypeStruct((1, 512), F32)]
        + _sibling_shapes(grads),
        scratch_shapes=_sem_pair(ng),
        semantics=("arbitrary",),
    )(dh1, o_attn, o_conv, ga, gc, w_out, *grads)
    return res[:5], res[5:]


def _conv_bwd(do_conv, y, ca, cg, conv_w, ln_g, ln_b, partials):
    R = ca.shape[0]
    nt = R // TM
    hpt = TM // HALO
    npart = len(partials)

    def body(do_ref, doh_ref, y_ref, yh_ref, ca_ref, cg_ref, cah_ref, cgh_ref, w_ref, lg_ref, lb_ref, *refs):
        p_ins, (dca_ref, dcg_ref, dw_ref, db_ref, dlg_ref, dlb_ref) = refs[:npart], refs[npart:npart + 6]
        p_outs, (send_sems, recv_sems, u_s, dy_s, uph_s, dyph_s) = refs[npart + 6:2 * npart + 6], refs[2 * npart + 6:]
        i = pl.program_id(0)
        exchange = functools.partial(_chip_copies, p_ins, p_outs, send_sems, recv_sems)
        _hosted(i, nt, exchange)

        @pl.when(i == 0)
        def _():
            dw_ref[...] = jnp.zeros_like(dw_ref)
            db_ref[...] = jnp.zeros_like(db_ref)
            dlg_ref[...] = jnp.zeros_like(dlg_ref)
            dlb_ref[...] = jnp.zeros_like(dlb_ref)

        lg, lb = lg_ref[...], lb_ref[...]

        def ln_bwd(yv, dov):
            yn, xhat, rstd = _ln_silu(yv, lg, lb)
            sig = _sigmoid(yn)
            dyn = dov * (sig * (1.0 + yn * (1.0 - sig)))
            dxh = dyn * lg
            dyv = rstd * (dxh - jnp.mean(dxh, axis=-1, keepdims=True)
                          - xhat * jnp.mean(dxh * xhat, axis=-1, keepdims=True))
            return dyv, dyn, xhat

        dyv, dyn, xhat = ln_bwd(y_ref[...], do_ref[...])
        dy_s[0:TM, :] = dyv
        dlg_ref[...] += jnp.sum(dyn * xhat, axis=0, keepdims=True)
        dlb_ref[...] += jnp.sum(dyn, axis=0, keepdims=True)
        db_ref[...] += jnp.sum(dyv, axis=0, keepdims=True)
        dyh, _, _ = ln_bwd(yh_ref[...], doh_ref[...])
        dy_s[TM:TM + HALO, :] = jnp.where(i < nt - 1, dyh, 0.0)
        u_s[HALO:HALO + TM, :] = ca_ref[...] * _sigmoid(cg_ref[...])
        u_s[0:HALO, :] = jnp.where(i > 0, cah_ref[...] * _sigmoid(cgh_ref[...]), 0.0)
        _phase_copies(dy_s, dyph_s)
        _phase_copies(u_s, uph_s)

        for rc in range(TM // CONV_ROWS):
            acc = jnp.zeros((CONV_ROWS, CONV_W), F32)
            for k in range(CONV_K):
                acc = acc + _shifted(dy_s, dyph_s, rc * CONV_ROWS + CONV_K - 1 - k, CONV_ROWS) * w_ref[k:k + 1, :]
            rows = slice(rc * CONV_ROWS, (rc + 1) * CONV_ROWS)
            sg = _sigmoid(cg_ref[rows, :])
            dca_ref[rows, :] = (acc * sg).astype(BF16)
            dcg_ref[rows, :] = (acc * ca_ref[rows, :] * sg * (1.0 - sg)).astype(BF16)

        for k in range(CONV_K):
            prod = _shifted(u_s, uph_s, HALO - (CONV_K - 1) + k, TM) * dy_s[0:TM, :]
            dw_ref[k:k + 1, :] += jnp.sum(prod, axis=0, keepdims=True)
        _hosted_wait(i, nt, exchange)

    prev_halo = pl.BlockSpec((HALO, CONV_W), lambda i: (jnp.maximum(i * hpt - 1, 0), 0))
    next_halo = pl.BlockSpec((HALO, CONV_W), lambda i: (jnp.minimum((i + 1) * hpt, nt * hpt - 1), 0))
    vec = jax.ShapeDtypeStruct((1, CONV_W), F32)
    res = _pcall(
        body, name="conv_bwd", grid=(nt,),
        in_specs=[_rows(TM, CONV_W), next_halo, _rows(TM, CONV_W), next_halo,
                  _rows(TM, CONV_W), _rows(TM, CONV_W), prev_halo, prev_halo,
                  _full((CONV_K, CONV_W)), _full((1, CONV_W)), _full((1, CONV_W))] + [ANY] * npart,
        out_specs=[_rows(TM, CONV_W), _rows(TM, CONV_W), _full((32, CONV_W)),
                   _full((1, CONV_W)), _full((1, CONV_W)), _full((1, CONV_W))] + [ANY] * npart,
        out_shape=[jax.ShapeDtypeStruct((R, CONV_W), BF16), jax.ShapeDtypeStruct((R, CONV_W), BF16),
                   jax.ShapeDtypeStruct((32, CONV_W), F32), vec, vec, vec] + _chip_shapes(partials),
        scratch_shapes=_sem_pair(3 * npart)
        + [pltpu.VMEM((HALO + TM, CONV_W), F32), pltpu.VMEM((TM + HALO, CONV_W), F32),
           _phase_scratch(), _phase_scratch()],
        semantics=("arbitrary",),
    )(do_conv, do_conv, y, y, ca, cg, ca, cg, conv_w, ln_g, ln_b, *partials)
    return res[:6], res[6:]


def _attn_bwd(q, kv, sinks, o, lse, do, grads, partials):
    R = q.shape[0]
    nb = R // BLOCK
    ng, npart = len(grads), len(partials)
    nx = ng + npart

    def body(sink_ref, q_ref, kvc_ref, kvp_ref, kvm_ref, o_ref, lse_ref, do_ref, *refs):
        x_ins, (dq_ref, dkv_ref, dkvm_ref, dsink_ref) = refs[:nx], refs[nx:nx + 4]
        x_outs = refs[nx + 4:2 * nx + 4]
        g_send, g_recv, p_send, p_recv, carry_s, cur_s, prev_s, bias_s, sink_s, delta_s = refs[2 * nx + 4:]
        b = pl.program_id(0)

        def exchange():
            return (_sibling_copies(x_ins[:ng], x_outs[:ng], g_send, g_recv)
                    + _chip_copies(x_ins[ng:], x_outs[ng:], p_send, p_recv))

        _hosted(b, nb + 1, exchange)

        @pl.when(b == 0)
        def _():
            dkvm_ref[...] = jnp.zeros_like(dkvm_ref)
            carry_s[...] = jnp.zeros_like(carry_s)
            for h in range(N_HEADS):
                dsink_ref[0, h] = 0.0
            _attn_tables(sink_ref, bias_s, sink_s)

        @pl.when(b < nb)
        def _():
            sel, pen, pen_m = _attn_masks(b)
            q_t = (q_ref[...] * ATTN_SCALE).T
            do_t = do_ref[...].astype(BF16).T
            kvc_t, kvp_t = kvc_ref[...].T, kvp_ref[...].T
            prod = do_ref[...] * o_ref[...]
            hi = prod.astype(BF16)
            lo = (prod - hi.astype(F32)).astype(BF16)
            head_of = lax.broadcasted_iota(jnp.int32, (N_HEADS, ATTN_W), 1) // HEAD_DIM
            ind = (head_of == lax.broadcasted_iota(jnp.int32, (N_HEADS, ATTN_W), 0)).astype(BF16)
            delta_s[...] = _dot(ind, hi, NT) + _dot(ind, lo, NT)
            dqs = []
            for g in range(N_HEADS // GROUP):
                ks, vs = slice(HEAD_DIM * g, HEAD_DIM * (g + 1)), slice(KV_W + HEAD_DIM * g, KV_W + HEAD_DIM * (g + 1))
                lanes = slice(GB * g, GB * (g + 1))
                qg, dog = _group_lanes(q_t, g), _group_lanes(do_t, g)
                kc, kp, km = kvc_ref[:, ks], kvp_ref[:, ks], kvm_ref[LEAD:BLOCK, ks]
                vc, vp, vm = kvc_ref[:, vs], kvp_ref[:, vs], kvm_ref[LEAD:BLOCK, vs]
                s_b, s_m = _attn_scores(qg, kc, kp, km, sel, pen, pen_m, bias_s[:, lanes])
                lse, delta = _head_lanes(lse_ref, g), _head_lanes(delta_s, g)
                p_b = jnp.exp(s_b - lse)
                p_m = jnp.exp(s_m - lse)
                dp_b = jnp.where(sel, _dot(vc, dog, NN), _dot(vp, dog, NN))
                ds_b = p_b * (dp_b - delta)
                ds_m = (p_m * (_dot(vm, dog, NN) - delta)).astype(BF16)
                dsk = jnp.exp(sink_s[:, lanes] - lse) * delta
                for j in range(GROUP):
                    dsink_ref[0, GROUP * g + j] += -jnp.sum(dsk[:, BLOCK * j:BLOCK * (j + 1)])
                ds_c = jnp.where(sel, ds_b, 0.0).astype(BF16)
                ds_p = jnp.where(sel, 0.0, ds_b).astype(BF16)
                p_c = jnp.where(sel, p_b, 0.0).astype(BF16)
                p_p = jnp.where(sel, 0.0, p_b).astype(BF16)
                dqs.append((_dot(kvc_t[ks, :], ds_c, NN) + _dot(kvp_t[ks, :], ds_p, NN)
                            + _dot(km, ds_m, TN)) * ATTN_SCALE)
                cur_s[:, ks] = _dot(ds_c, qg, NT)
                cur_s[:, vs] = _dot(p_c, dog, NT)
                prev_s[:, ks] = _dot(ds_p, qg, NT)
                prev_s[:, vs] = _dot(p_p, dog, NT)
                dkvm_ref[:, ks] += _dot(ds_m, qg, NT)
                dkvm_ref[:, vs] += _dot(p_m.astype(BF16), dog, NT)
            dq_ref[...] = _head_rows(dqs).astype(BF16).T
            dkv_ref[...] = (carry_s[...] + prev_s[...]).astype(BF16)
            carry_s[...] = cur_s[...]

        @pl.when(b == nb)
        def _():
            dkv_ref[...] = carry_s[...].astype(BF16)

        _hosted_wait(b, nb + 1, exchange)

    def at(off):
        return lambda b: (jnp.clip(b + off, 0, nb - 1), 0)

    blk = lambda cols, off=0: pl.BlockSpec((BLOCK, cols), at(off))
    res = _pcall(
        body, name="attn_bwd", grid=(nb + 1,),
        in_specs=[pl.BlockSpec(memory_space=pltpu.SMEM), blk(512), blk(256), blk(256, -1), _full((BLOCK, 256)),
                  blk(512), pl.BlockSpec((N_HEADS, BLOCK), lambda b: (0, jnp.minimum(b, nb - 1))), blk(512)]
        + [ANY] * nx,
        out_specs=[blk(512), blk(256, -1), _full((N_META, 256)), pl.BlockSpec(memory_space=pltpu.SMEM)]
        + [ANY] * nx,
        out_shape=[jax.ShapeDtypeStruct((R, 512), BF16), jax.ShapeDtypeStruct((R, 256), BF16),
                   jax.ShapeDtypeStruct((N_META, 256), F32), jax.ShapeDtypeStruct((1, N_HEADS), F32)]
        + _sibling_shapes(grads) + _chip_shapes(partials),
        scratch_shapes=_sem_pair(ng) + _sem_pair(3 * npart) + [pltpu.VMEM((BLOCK, 256), F32)] * 3
        + [pltpu.VMEM((BLOCK, N_HEADS * BLOCK), F32), pltpu.VMEM((1, N_HEADS * BLOCK), F32),
           pltpu.VMEM((N_HEADS, BLOCK), F32)],
        semantics=("arbitrary",),
    )(sinks, q, kv, kv, kv, o, lse, do, *grads, *partials)
    return res[:4], res[4:4 + ng], res[4 + ng:]


def _inproj_bwd(dh1, h0, g1, dq, dkv, dkvm, dca, dcg, w_in_t):
    R = h0.shape[0]
    nt = R // TM
    assert nt >= 2

    def body(dh1_ref, h0_ref, g_ref, dq_ref, dkv_ref, dkvm_ref, dca_ref, dcg_ref, w_ref,
             gx_hbm, dmeta_ref, dproj_ref, hn_ref, dg_ref, dx_s, gx_sems):
        i = pl.program_id(0)

        def gx_copy(step, slot, first):
            if first:
                return pltpu.make_async_copy(dx_s.at[slot, pl.ds(BLOCK, TM - BLOCK)],
                                             gx_hbm.at[pl.ds(0, TM - BLOCK)], gx_sems.at[slot])
            return pltpu.make_async_copy(
                dx_s.at[slot], gx_hbm.at[pl.ds(pl.multiple_of(step * TM - BLOCK, BLOCK), TM)], gx_sems.at[slot])

        @pl.when(i == 0)
        def _():
            dg_ref[...] = jnp.zeros_like(dg_ref)

        dproj_ref[:, 0:512] = dq_ref[...]
        dproj_ref[:, 512:768] = dkv_ref[...]
        dproj_ref[:, 768:1280] = dca_ref[...]
        dproj_ref[:, 1280:1792] = dcg_ref[...]

        @pl.when(i == 0)
        def _():
            dproj_ref[LEAD:BLOCK, 512:768] = dkvm_ref[...].astype(BF16)

        dhn = _dot(dproj_ref[...], w_ref[...], NN)
        h = h0_ref[...]
        r = _rms_stats(h)
        g = g_ref[...]
        hn_ref[...] = (h * r * g).astype(BF16)
        dx, dg = _rms_bwd(dhn, h, r, g)
        dg_ref[...] += dg
        slot = i % 2
        pl.when(i == 2)(lambda: gx_copy(0, 0, True).wait())
        pl.when(i > 2)(lambda: gx_copy(i - 2, slot, False).wait())
        dx_s[slot] = dh1_ref[...] + dx

        @pl.when(i == 0)
        def _():
            dmeta_ref[...] = dx_s[0, LEAD:BLOCK, :]
            gx_copy(0, 0, True).start()

        pl.when(i > 0)(lambda: gx_copy(i, slot, False).start())

        @pl.when(i == nt - 1)
        def _():
            gx_copy(nt - 2, (nt - 2) % 2, nt == 2).wait()
            gx_copy(nt - 1, (nt - 1) % 2, False).wait()

    return _pcall(
        body, name="inproj_bwd", grid=(nt,),
        in_specs=[_rows(TM, D_MODEL), _rows(TM, D_MODEL), _full((1, D_MODEL)), _rows(TM, 512), _rows(TM, 256),
                  _full((N_META, 256)), _rows(TM, 512), _rows(TM, 512), _full((1792, D_MODEL))],
        out_specs=[ANY, _full((N_META, D_MODEL)), _rows(TM, 1792), _rows(TM, D_MODEL), _full((1, D_MODEL))],
        out_shape=[jax.ShapeDtypeStruct((R - BLOCK, D_MODEL), F32), jax.ShapeDtypeStruct((N_META, D_MODEL), F32),
                   jax.ShapeDtypeStruct((R, 1792), BF16),
                   jax.ShapeDtypeStruct((R, D_MODEL), BF16), jax.ShapeDtypeStruct((1, D_MODEL), F32)],
        scratch_shapes=[pltpu.VMEM((2, TM, D_MODEL), F32), pltpu.SemaphoreType.DMA((2,))],
        semantics=("arbitrary",),
    )(dh1, h0, g1, dq, dkv, dkvm, dca, dcg, w_in_t)


ANY = pl.BlockSpec(memory_space=pl.ANY)


def _position():
    return lax.axis_index("x"), lax.axis_index("y"), lax.axis_index("c")


def _device_number(p):
    return 4 * p[0] + 2 * p[1] + p[2]


def _two_level_allgather(ins, outs, block, send_sems, recv_sems, local_sems, sem_base=0):
    n = len(ins)
    x, y, c = _position()
    me, sibling = (x, y, c), (x, y, 1 - c)
    chips = [(1 - x, y), (x, 1 - y), (1 - x, 1 - y)]

    def copy(w, k, origin, to, src=None):
        return pltpu.make_async_remote_copy(
            src_ref=block(w, origin) if src is None else src, dst_ref=block(w, origin),
            send_sem=send_sems.at[sem_base + 7 * w + k], recv_sem=recv_sems.at[sem_base + 7 * w + k],
            device_id=to, device_id_type=MESH)

    def mine(w):
        return pltpu.make_async_copy(ins[w], block(w, me), local_sems.at[w])

    def own(w):
        return [copy(w, 0, me, sibling, src=ins[w])] + [
            copy(w, 1 + j, me, (*chip, c), src=ins[w]) for j, chip in enumerate(chips)]

    def passed(w):
        return [copy(w, 4 + j, (*chip, c), sibling) for j, chip in enumerate(chips)]

    def start():
        for w in range(n):
            mine(w).start()
        for w in range(n):
            for cp in own(w):
                cp.start()

    def forward(w):
        fw = passed(w)
        for j, chip in enumerate(chips):
            copy(w, 1 + j, (*chip, c), me).wait_recv()
            fw[j].start()

    def finish():
        for w in range(n):
            copy(w, 0, sibling, me).wait_recv()
            for j, chip in enumerate(chips):
                copy(w, 4 + j, (*chip, 1 - c), me).wait_recv()
        for w in range(n):
            for cp in own(w) + passed(w):
                cp.wait_send()
            mine(w).wait()

    return start, forward, finish


def _carried_allgather(step, n_steps, shards, refs):
    ns = len(shards)
    ins, outs, (send_sems, recv_sems, local_sems) = refs[:ns], refs[ns:2 * ns], refs[2 * ns:]
    start, forward, finish = _two_level_allgather(
        ins, outs, _row_block(outs, [s.shape[0] for s in shards]), send_sems, recv_sems, local_sems)
    pl.when(step == 0)(start)
    total = sum(s.shape[0] for s in shards)
    sent = 0
    for w, s in enumerate(shards):
        sent += s.shape[0]
        pl.when(step == (AG_FORWARD_AT * sent * (n_steps - 1)) // (100 * total))(functools.partial(forward, w))
    return lambda: pl.when(step == n_steps - 1)(finish)


def _gathered_shapes(shards):
    return [jax.ShapeDtypeStruct((N_DEV * s.shape[0], s.shape[1]), s.dtype) for s in shards]


def _allgather_sems(ns):
    return _sem_pair(7 * ns) + [pltpu.SemaphoreType.DMA((ns,))]


def _blocking_allgather(ins, outs, block, send_sems, recv_sems, local_sems, sem_base=0):
    start, forward, finish = _two_level_allgather(ins, outs, block, send_sems, recv_sems, local_sems, sem_base)
    start()
    for w in range(len(ins)):
        forward(w)
    finish()


def _row_block(outs, rows):
    def block(w, p):
        return outs[w].at[pl.ds(pl.multiple_of(_device_number(p) * rows[w], 16), rows[w])]
    return block


def _sibling_copies(ins, outs, send_sems, recv_sems):
    x, y, c = _position()
    return [pltpu.make_async_remote_copy(
        src_ref=ins[w].at[:, 1 - c], dst_ref=outs[w], send_sem=send_sems.at[w], recv_sem=recv_sems.at[w],
        device_id=(x, y, 1 - c), device_id_type=MESH) for w in range(len(ins))]


def _chip_copies(ins, outs, send_sems, recv_sems):
    x, y, c = _position()
    chips = [(1 - x, y), (x, 1 - y), (1 - x, 1 - y)]
    return [pltpu.make_async_remote_copy(
        src_ref=ins[w].at[2 * chip[0] + chip[1]], dst_ref=outs[w].at[k],
        send_sem=send_sems.at[3 * w + k], recv_sem=recv_sems.at[3 * w + k],
        device_id=(*chip, c), device_id_type=MESH) for w in range(len(ins)) for k, chip in enumerate(chips)]


def _hosted(step, n_steps, make_copies):
    @pl.when(step == 0)
    def _():
        for cp in make_copies():
            cp.start()


def _hosted_wait(step, n_steps, make_copies):
    @pl.when(step == n_steps - 1)
    def _():
        for cp in make_copies():
            cp.wait()


def _sem_pair(n):
    return [pltpu.SemaphoreType.DMA((n,)), pltpu.SemaphoreType.DMA((n,))]


def _allgather_params(shards, small):
    arrays = list(shards) + list(small)
    n, ns = len(arrays), len(shards)

    def body(*refs):
        ins, outs = refs[:n], refs[n:2 * n]
        send_sems, recv_sems, local_sems = refs[2 * n:]

        rows = _row_block(outs, [a.shape[0] for a in arrays])

        def block(w, p):
            return rows(w, p) if w < ns else outs[w].at[_device_number(p)]

        _blocking_allgather(ins, outs, block, send_sems, recv_sems, local_sems)

    out_shape = [jax.ShapeDtypeStruct((N_DEV * a.shape[0], a.shape[1]), a.dtype) for a in shards]
    out_shape += [jax.ShapeDtypeStruct((N_DEV,) + a.shape, a.dtype) for a in small]
    return _pcall(
        body, name="allgather_params", in_specs=[ANY] * n, out_specs=[ANY] * n, out_shape=out_shape,
        scratch_shapes=[pltpu.SemaphoreType.DMA((7 * n,)), pltpu.SemaphoreType.DMA((7 * n,)),
                        pltpu.SemaphoreType.DMA((n,))],
    )(*arrays)


def _reduce_siblings(grads):
    n = len(grads)

    def body(*refs):
        ins, outs = refs[:n], refs[n:2 * n]
        send_sems, recv_sems = refs[2 * n:]
        copies = _sibling_copies(ins, outs, send_sems, recv_sems)
        for cp in copies:
            cp.start()
        for cp in copies:
            cp.wait()

    return _pcall(
        body, name="reduce_siblings", in_specs=[ANY] * n, out_specs=[ANY] * n,
        out_shape=_sibling_shapes(grads), scratch_shapes=_sem_pair(n),
    )(*grads)


def _sibling_shapes(grads):
    return [jax.ShapeDtypeStruct((4,) + g.shape[2:], F32) for g in grads]


def _chip_shapes(partials):
    return [jax.ShapeDtypeStruct((3,) + p.shape[1:], F32) for p in partials]


def _add_sibling(grad, received, core, name):
    _, _, r, cols = grad.shape

    def body(core_ref, g_ref, r_ref, o_ref):
        o_ref[...] = g_ref[...] + r_ref[...]

    return pl.pallas_call(
        body, name=name,
        grid_spec=pltpu.PrefetchScalarGridSpec(
            num_scalar_prefetch=1, grid=(4,),
            in_specs=[pl.BlockSpec((None, None, r, cols), lambda s, core_ref: (s, core_ref[0], 0, 0)),
                      pl.BlockSpec((None, r, cols), lambda s, core_ref: (s, 0, 0))],
            out_specs=pl.BlockSpec((None, r, cols), lambda s, core_ref: (s, 0, 0))),
        out_shape=jax.ShapeDtypeStruct((4, r, cols), F32),
        compiler_params=pltpu.CompilerParams(vmem_limit_bytes=VMEM_LIMIT),
    )(core, grad, received)


def _reduce_chips(partials, small):
    n, ns = len(partials), len(small)

    def body(*refs):
        p_ins, s_ins = refs[:n], refs[n:n + ns]
        p_outs, s_outs = refs[n + ns:2 * n + ns], refs[2 * n + ns:2 * (n + ns)]
        send_sems, recv_sems, local_sems = refs[2 * (n + ns):]
        copies = _chip_copies(p_ins, p_outs, send_sems, recv_sems)
        for cp in copies:
            cp.start()
        _blocking_allgather(s_ins, s_outs, lambda w, p: s_outs[w].at[_device_number(p)],
                            send_sems, recv_sems, local_sems, sem_base=3 * n)
        for cp in copies:
            cp.wait()

    out_shape = _chip_shapes(partials)
    out_shape += [jax.ShapeDtypeStruct((N_DEV,) + a.shape, a.dtype) for a in small]
    nsem = 3 * n + 7 * ns
    return _pcall(
        body, name="reduce_chips", in_specs=[ANY] * (n + ns), out_specs=[ANY] * (n + ns), out_shape=out_shape,
        scratch_shapes=[pltpu.SemaphoreType.DMA((nsem,)), pltpu.SemaphoreType.DMA((nsem,)),
                        pltpu.SemaphoreType.DMA((ns,))],
    )(*partials, *small)


def _adam(g, w, m, v):
    m = ADAM_B1 * m + (1.0 - ADAM_B1) * g
    v = ADAM_B2 * v + (1.0 - ADAM_B2) * (g * g)
    m_hat = m / (1.0 - ADAM_B1 ** ADAM_STEP)
    v_hat = v / (1.0 - ADAM_B2 ** ADAM_STEP)
    delta = -ADAM_LR * (m_hat / (jnp.sqrt(v_hat) + ADAM_EPS) + ADAM_WD * w)
    return delta, m, v


def _adamw(partial, received, slot, w, m, v, name):
    _, r, cols = partial.shape

    def body(slot_ref, p_ref, r_ref, w_ref, m_ref, v_ref, g_ref, d_ref, nm_ref, nv_ref):
        g = p_ref[...] + r_ref[0] + r_ref[1] + r_ref[2]
        g_ref[...] = g
        d_ref[...], nm_ref[...], nv_ref[...] = _adam(g, w_ref[...], m_ref[...], v_ref[...])

    whole = pl.BlockSpec((r, cols), lambda i, slot_ref: (0, 0))
    out = jax.ShapeDtypeStruct((r, cols), F32)
    return pl.pallas_call(
        body, name=name,
        grid_spec=pltpu.PrefetchScalarGridSpec(
            num_scalar_prefetch=1, grid=(1,),
            in_specs=[pl.BlockSpec((None, r, cols), lambda i, slot_ref: (slot_ref[0], 0, 0)),
                      pl.BlockSpec((3, r, cols), lambda i, slot_ref: (0, 0, 0)), whole, whole, whole],
            out_specs=[whole, whole, whole, whole]),
        out_shape=[out, out, out, out],
        compiler_params=pltpu.CompilerParams(vmem_limit_bytes=VMEM_LIMIT),
    )(slot, partial, received, w, m, v)


def _adamw_small(dev, ga, gb, gc, params):
    names = ["meta", "attn_norm", "sinks", "conv_w", "conv_b", "ln_g", "ln_b", "attn_out", "conv_out",
             "ffn_norm", "final_norm"]
    flat = [a for p in params for a in p]
    n_in = len(flat)

    def body(dev_ref, ga_ref, gb_ref, gc_ref, *refs):
        ins, outs = refs[:n_in], refs[n_in:n_in + 4 * len(names)]
        loss_ref, sb, sc = refs[n_in + 4 * len(names):]
        a = ga_ref[0]
        sb[...] = gb_ref[0]
        sc[...] = gc_ref[0]
        for d in range(1, N_DEV):
            a = a + ga_ref[d]
            sb[...] += gb_ref[d]
            sc[...] += gc_ref[d]
        dev = dev_ref[0]
        grads = {
            "attn_norm": a[0:1, :], "ffn_norm": a[1:2, :], "final_norm": a[2:3, :],
            "conv_b": a[3:4, 0:512], "ln_g": a[3:4, 512:1024], "ln_b": a[4:5, 0:512],
            "attn_out": a[4:5, 512:1024], "conv_out": a[5:6, 0:512], "sinks": a[5:6, 512:512 + N_HEADS],
            "meta": sb[pl.ds(pl.multiple_of(dev * N_META, N_META), N_META), :],
            "conv_w": sc[pl.ds(pl.multiple_of(dev * 32, 32), 32), :][0:CONV_K, :],
        }
        for idx, nm in enumerate(names):
            w_ref, m_ref, v_ref = ins[3 * idx:3 * idx + 3]
            g = grads[nm]
            delta, m, v = _adam(g, w_ref[...], m_ref[...], v_ref[...])
            o = outs[4 * idx:4 * idx + 4]
            o[0][...], o[1][...], o[2][...], o[3][...] = g, delta, m, v
        loss_ref[...] = a[6:7, 0:1]

    vm = pl.BlockSpec(memory_space=pltpu.VMEM)
    out_shape = [jax.ShapeDtypeStruct(p[0].shape, F32) for p in params for _ in range(4)]
    out_shape.append(jax.ShapeDtypeStruct((1, 1), F32))
    res = pl.pallas_call(
        body, name="adamw_small",
        grid_spec=pltpu.PrefetchScalarGridSpec(
            num_scalar_prefetch=1, grid=(1,),
            in_specs=[pl.BlockSpec(ga.shape, lambda i, d: (0, 0, 0)), pl.BlockSpec(gb.shape, lambda i, d: (0, 0, 0)),
                      pl.BlockSpec(gc.shape, lambda i, d: (0, 0, 0))]
            + [pl.BlockSpec(a.shape, lambda i, d: (0, 0)) for a in flat],
            out_specs=[pl.BlockSpec(s.shape, lambda i, d: (0, 0)) for s in out_shape],
            scratch_shapes=[pltpu.VMEM(gb.shape[1:], F32), pltpu.VMEM(gc.shape[1:], F32)]),
        out_shape=out_shape,
        compiler_params=pltpu.CompilerParams(vmem_limit_bytes=VMEM_LIMIT),
    )(dev, ga, gb, gc, *flat)
    return [res[4 * i:4 * i + 4] for i in range(len(names))], res[-1]


def kernel(x, meta_tokens, attn_norm_g, w_in, attn_sinks, conv_w, conv_b, conv_ln_g, conv_ln_b, attn_out_g, conv_out_g, w_out, ffn_norm_g, w_gate, w_up, w_down, final_norm_g, loss_target, m_meta_tokens, m_attn_norm_g, m_w_in, m_attn_sinks, m_conv_w, m_conv_b, m_conv_ln_g, m_conv_ln_b, m_attn_out_g, m_conv_out_g, m_w_out, m_ffn_norm_g, m_w_gate, m_w_up, m_w_down, m_final_norm_g, v_meta_tokens, v_attn_norm_g, v_w_in, v_attn_sinks, v_conv_w, v_conv_b, v_conv_ln_g, v_conv_ln_b, v_attn_out_g, v_conv_out_g, v_w_out, v_ffn_norm_g, v_w_gate, v_w_up, v_w_down, v_final_norm_g):
    xi, yi, ci = _position()
    dev = jnp.reshape(_device_number((xi, yi, ci)), (1,)).astype(jnp.int32)
    core = jnp.reshape(ci, (1,)).astype(jnp.int32)
    slot = jnp.reshape(2 * xi + yi, (1,)).astype(jnp.int32)

    w_in_t, meta_st, convw_st = _allgather_params([w_in[0].T.astype(BF16)], [meta_tokens, conv_w[0]])
    meta_full = jnp.transpose(meta_st, (1, 0, 2)).reshape(N_META, D_MODEL)
    convw_full = jnp.transpose(convw_st, (1, 0, 2)).reshape(CONV_K, CONV_W)

    final_g = final_norm_g.reshape(1, D_MODEL)

    (h0, q, kv, ca, cg), (w_out_b,) = _inproj_fwd(x[0], meta_full, attn_norm_g, w_in_t, [w_out[0].astype(BF16)])
    o_attn, lse, (wg_t, wu_t) = _attn_fwd(
        q, kv, attn_sinks, [w_gate[0].T.astype(BF16), w_up[0].T.astype(BF16)])
    (o_conv, y_conv), (wd_b,) = _conv_fwd(ca, cg, convw_full, conv_b, conv_ln_g, conv_ln_b, [w_down[0].astype(BF16)])
    h1 = _outproj_fwd(h0, o_attn, o_conv, attn_out_g, conv_out_g, w_out_b)
    gate, up, act, dh2, dh2b, loss_sum, dg_final = _ffn_fwd(h1, ffn_norm_g, wg_t, wu_t, wd_b, final_g, loss_target[0])

    def blocks(g):
        return g.reshape(4, 2, g.shape[0] // N_DEV, D_MODEL)

    def add_siblings(grads, received, tags):
        return [_add_sibling(g, r, core, "add_sibling_" + t) for g, r, t in zip(grads, received, tags)]

    dgate, dup, hn2, dh1, dg_ffn = _ffn_bwd(dh2, dh2b, h1, ffn_norm_g, gate, up, wg_t, wu_t, wd_b)
    ffn_grads = [blocks(_wgrad(dgate, hn2, FF_CHUNK, "wgrad_gate")), blocks(_wgrad(dup, hn2, FF_CHUNK, "wgrad_up")),
                 blocks(_wgrad(act, dh2b, FF_CHUNK, "wgrad_down"))]
    (do_attn, do_conv, mixed, dg_ao, dg_co), ffn_sib = _outproj_bwd(
        dh1, o_attn, o_conv, attn_out_g, conv_out_g, w_out_b, ffn_grads)
    ffn_sums = add_siblings(ffn_grads, ffn_sib, ("gate", "up", "down"))
    out_grads = [blocks(_wgrad(mixed, dh1, D_MODEL, "wgrad_out"))]
    (dca, dcg, dconvw, dconvb, dln_g, dln_b), gate_up_chips = _conv_bwd(
        do_conv, y_conv, ca, cg, convw_full, conv_ln_g, conv_ln_b, ffn_sums[:2])
    (dq, dkv, dkvm, dsinks), out_sib, down_chips = _attn_bwd(
        q, kv, attn_sinks, o_attn, lse, do_attn, out_grads, ffn_sums[2:])
    ffn_chips = list(gate_up_chips) + list(down_chips)
    out_sums = add_siblings(out_grads, out_sib, ("out",))
    grad_x, dmeta, dproj, hn1, dg_attn = _inproj_bwd(dh1, h0, attn_norm_g, dq, dkv, dkvm, dca, dcg, w_in_t)
    dwi_t, out_chips = _wgrad(dproj, hn1, 1792, "wgrad_in", out_sums)
    in_grads = [blocks(dwi_t)]
    in_sums = add_siblings(in_grads, _reduce_siblings(in_grads), ("in",))
    small_a = jnp.concatenate([
        dg_attn, dg_ffn, dg_final, jnp.concatenate([dconvb, dln_g], axis=1), jnp.concatenate([dln_b, dg_ao], axis=1),
        jnp.concatenate([dg_co, dsinks, jnp.zeros((1, 512 - N_HEADS), F32)], axis=1),
        jnp.concatenate([loss_sum[0:1, :], jnp.zeros((1, D_MODEL - 128), F32)], axis=1),
        jnp.zeros((1, D_MODEL), F32)], axis=0)
    small_b = jnp.transpose(dmeta.reshape(N_META, N_DEV, 128), (1, 0, 2)).reshape(N_DEV * N_META, 128)
    small_c = jnp.transpose(dconvw.reshape(32, N_DEV, 64), (1, 0, 2)).reshape(N_DEV * 32, 64)
    in_chips, ga, gb, gc = _reduce_chips(in_sums, [small_a, small_b, small_c])
    tags = ("in", "out", "gate", "up", "down")
    chip_sums = in_sums + out_sums + ffn_sums
    from_chips = [in_chips] + list(out_chips) + list(ffn_chips)

    big = [(True, w_in, m_w_in, v_w_in), (False, w_out, m_w_out, v_w_out), (True, w_gate, m_w_gate, v_w_gate),
           (True, w_up, m_w_up, v_w_up), (False, w_down, m_w_down, v_w_down)]
    big_out = {}
    for t, p, r, (transposed, w, m, v) in zip(tags, chip_sums, from_chips, big):
        rows = (lambda a: jnp.transpose(a[0])) if transposed else (lambda a: a[0])
        back = (lambda a: jnp.transpose(a)[None]) if transposed else (lambda a: a[None])
        big_out[t] = [back(a) for a in _adamw(p, r, slot, rows(w), rows(m), rows(v), "adamw_" + t)]

    small_params = [
        (meta_tokens, m_meta_tokens, v_meta_tokens), (attn_norm_g, m_attn_norm_g, v_attn_norm_g),
        (attn_sinks, m_attn_sinks, v_attn_sinks), (conv_w[0], m_conv_w[0], v_conv_w[0]),
        (conv_b, m_conv_b, v_conv_b), (conv_ln_g, m_conv_ln_g, v_conv_ln_g), (conv_ln_b, m_conv_ln_b, v_conv_ln_b),
        (attn_out_g, m_attn_out_g, v_attn_out_g), (conv_out_g, m_conv_out_g, v_conv_out_g),
        (ffn_norm_g, m_ffn_norm_g, v_ffn_norm_g),
        (final_g, m_final_norm_g.reshape(1, D_MODEL), v_final_norm_g.reshape(1, D_MODEL))]
    sm, loss = _adamw_small(dev, ga, gb, gc, small_params)
    sm[3] = [a[None] for a in sm[3]]
    sm[10] = [a.reshape(D_MODEL) for a in sm[10]]

    per_param = [sm[0], sm[1], big_out["in"], sm[2], sm[3], sm[4], sm[5], sm[6], sm[7], sm[8], big_out["out"],
                 sm[9], big_out["gate"], big_out["up"], big_out["down"], sm[10]]
    loss = loss.reshape(())
    outs = [loss, grad_x[None]]
    for kind in range(4):
        outs += [p[kind] for p in per_param]
    return tuple(outs)
```

```python
import functools
import math

import jax
import jax.numpy as jnp
from jax import lax
from jax.experimental import pallas as pl
from jax.experimental.pallas import tpu as pltpu

F32, BF16 = jnp.float32, jnp.bfloat16
MESH = pl.DeviceIdType.MESH

D_MODEL = 1024
N_META = 16
BLOCK = 128
LEAD = BLOCK - N_META
HEAD_DIM = 64
N_HEADS = 8
GROUP = 4
ATTN_W = 512
KV_W = 128
CONV_W = 512
CONV_K = 31
HALO = 32
D_FF = 2816
FF_CHUNK = D_FF // 2
FF_SUB = [slice(s, s + 256) for s in range(0, D_FF, 256)]
N_DEV = 8
EPS = 1e-5
NEG = -1e30
TM = 640
AG_FORWARD_AT = 85
WGRAD_K_TILES = 5
CONV_ROWS = 32
VMEM_LIMIT = 56 * 1024 * 1024

ADAM_LR, ADAM_B1, ADAM_B2, ADAM_EPS, ADAM_WD, ADAM_STEP = 0.001, 0.9, 0.999, 1e-08, 0.01, 10

NT = (((1,), (1,)), ((), ()))
NN = (((1,), (0,)), ((), ()))
TN = (((0,), (0,)), ((), ()))


def _dot(a, b, dims):
    return lax.dot_general(a, b, dims, preferred_element_type=F32)


def _sigmoid(x):
    return 1.0 / (1.0 + jnp.exp(-x))


def _pcall(body, *, name, out_shape, grid=None, in_specs=None, out_specs=None, scratch_shapes=(),
           semantics=None, **kw):
    params = dict(vmem_limit_bytes=VMEM_LIMIT)
    if semantics is not None:
        params["dimension_semantics"] = semantics
    extra = {}
    if grid is not None:
        extra["grid"] = grid
    if in_specs is not None:
        extra["in_specs"] = in_specs
    if out_specs is not None:
        extra["out_specs"] = out_specs
    return pl.pallas_call(body, name=name, out_shape=out_shape, scratch_shapes=list(scratch_shapes),
                          compiler_params=pltpu.CompilerParams(**params), **extra, **kw)


def _rows(tm, cols):
    return pl.BlockSpec((tm, cols), lambda i, *_: (i, 0))


def _full(shape):
    nd = len(shape)
    return pl.BlockSpec(shape, lambda *_: (0,) * nd)


def _rms_stats(x):
    return lax.rsqrt(jnp.mean(x * x, axis=-1, keepdims=True) + EPS)


def _rms_bwd(dy, x, r, g):
    t = dy * g
    dx = r * (t - x * (r * r) * jnp.mean(t * x, axis=-1, keepdims=True))
    dg = jnp.sum(dy * x * r, axis=0, keepdims=True)
    return dx, dg


def _inproj_fwd(x, meta, g1, w_in_t, shards):
    R = x.shape[0] + BLOCK
    nt = R // TM
    ns = len(shards)
    assert nt >= 2

    def body(x_hbm, meta_ref, g_ref, w_ref, *refs):
        ag_ins, (h0_ref, q_ref, kv_ref, ca_ref, cg_ref), ag_outs = refs[:ns], refs[ns:ns + 5], refs[ns + 5:2 * ns + 5]
        ag_sems, (x_s, sems) = refs[2 * ns + 5:2 * ns + 8], refs[2 * ns + 8:]
        i = pl.program_id(0)
        slot = i % 2
        ag_finish = _carried_allgather(i, nt, shards, ag_ins + ag_outs + ag_sems)

        def x_copy(step, slot, first):
            if first:
                return pltpu.make_async_copy(x_hbm.at[pl.ds(0, TM - BLOCK)],
                                             x_s.at[slot, pl.ds(BLOCK, TM - BLOCK)], sems.at[slot])
            return pltpu.make_async_copy(
                x_hbm.at[pl.ds(pl.multiple_of(step * TM - BLOCK, BLOCK), TM)], x_s.at[slot], sems.at[slot])

        @pl.when(i == 0)
        def _():
            x_copy(0, 0, True).start()
            x_s[0, 0:LEAD, :] = jnp.zeros((LEAD, D_MODEL), F32)
            x_s[0, LEAD:BLOCK, :] = meta_ref[...]

        pl.when(i + 1 < nt)(lambda: x_copy(i + 1, 1 - slot, False).start())
        pl.when(i == 0)(lambda: x_copy(0, 0, True).wait())
        pl.when(i > 0)(lambda: x_copy(i, slot, False).wait())
        h = x_s[slot]
        h0_ref[...] = h
        hn = (h * _rms_stats(h) * g_ref[...]).astype(BF16)
        q_ref[...] = _dot(hn, w_ref[0:512, :], NT).astype(BF16)
        kv_ref[...] = _dot(hn, w_ref[512:768, :], NT).astype(BF16)
        ca_ref[...] = _dot(hn, w_ref[768:1280, :], NT)
        cg_ref[...] = _dot(hn, w_ref[1280:1792, :], NT)
        ag_finish()

    anywhere = pl.BlockSpec(memory_space=pl.ANY)
    res = _pcall(
        body, name="inproj_fwd", grid=(nt,),
        in_specs=[anywhere, _full((N_META, D_MODEL)), _full((1, D_MODEL)), _full((1792, D_MODEL))] + [anywhere] * ns,
        out_specs=[_rows(TM, D_MODEL), _rows(TM, 512), _rows(TM, 256), _rows(TM, 512), _rows(TM, 512)]
        + [anywhere] * ns,
        out_shape=[jax.ShapeDtypeStruct((R, D_MODEL), F32),
                   jax.ShapeDtypeStruct((R, 512), BF16), jax.ShapeDtypeStruct((R, 256), BF16),
                   jax.ShapeDtypeStruct((R, 512), F32), jax.ShapeDtypeStruct((R, 512), F32)]
        + _gathered_shapes(shards),
        scratch_shapes=_allgather_sems(ns) + [pltpu.VMEM((2, TM, D_MODEL), F32), pltpu.SemaphoreType.DMA((2,))],
        semantics=("arbitrary",),
    )(x, meta, g1, w_in_t, *shards)
    return res[:5], res[5:]


GB = GROUP * BLOCK
ATTN_SCALE = 1.0 / math.sqrt(HEAD_DIM)


def _group_lanes(xt, g):
    return jnp.concatenate(
        [xt[HEAD_DIM * (GROUP * g + j):HEAD_DIM * (GROUP * g + j + 1), :] for j in range(GROUP)], axis=1)


def _head_lanes(ref, g):
    return jnp.concatenate([ref[GROUP * g + j:GROUP * g + j + 1, :] for j in range(GROUP)], axis=1)


def _head_rows(xs):
    return jnp.concatenate([x[:, BLOCK * j:BLOCK * (j + 1)] for x in xs for j in range(GROUP)], axis=0)


def _attn_tables(sink_ref, bias_s, sink_s):
    kk = lax.broadcasted_iota(jnp.int32, (BLOCK, BLOCK), 0)
    ii = lax.broadcasted_iota(jnp.int32, (BLOCK, BLOCK), 1)
    dist = jnp.where(kk <= ii, ii - kk, ii - kk + BLOCK).astype(F32)
    for h in range(N_HEADS):
        bias_s[:, BLOCK * h:BLOCK * (h + 1)] = dist * -(2.0 ** -(h + 1))
        sink_s[:, BLOCK * h:BLOCK * (h + 1)] = jnp.zeros((1, BLOCK), F32) + sink_ref[0, h]


def _attn_masks(b):
    kk = lax.broadcasted_iota(jnp.int32, (BLOCK, GB), 0)
    ii = lax.broadcasted_iota(jnp.int32, (BLOCK, GB), 1) & (BLOCK - 1)
    sel = kk <= ii
    pen = jnp.where(sel, jnp.where(b >= 1, 0.0, NEG), jnp.where(b >= 2, 0.0, NEG))
    mj = lax.broadcasted_iota(jnp.int32, (N_META, GB), 0)
    mi = lax.broadcasted_iota(jnp.int32, (N_META, GB), 1) & (BLOCK - 1)
    pen_m = jnp.where((mj + LEAD) <= (mi + b * BLOCK), 0.0, NEG)
    return sel, pen, pen_m


def _attn_scores(qt, kc, kp, km, sel, pen, pen_m, bias):
    s_b = jnp.where(sel, _dot(kc, qt, NN), _dot(kp, qt, NN)) + bias + pen
    s_m = _dot(km, qt, NN) + pen_m
    return s_b, s_m


def _attn_fwd(q, kv, sinks, shards):
    R = q.shape[0]
    nb = R // BLOCK
    ns = len(shards)

    def body(sink_ref, q_ref, kvc_ref, kvp_ref, kvm_ref, *refs):
        ag_ins, (o_ref, lse_ref), ag_outs = refs[:ns], refs[ns:ns + 2], refs[ns + 2:2 * ns + 2]
        ag_sems, (bias_s, sink_s) = refs[2 * ns + 2:2 * ns + 5], refs[2 * ns + 5:]
        b = pl.program_id(0)
        ag_finish = _carried_allgather(b, nb, shards, ag_ins + ag_outs + ag_sems)
        pl.when(b == 0)(functools.partial(_attn_tables, sink_ref, bias_s, sink_s))
        sel, pen, pen_m = _attn_masks(b)
        q_t = (q_ref[...] * ATTN_SCALE).T
        kvc_t, kvp_t = kvc_ref[...].T, kvp_ref[...].T
        outs = []
        for g in range(N_HEADS // GROUP):
            ks, vs = slice(HEAD_DIM * g, HEAD_DIM * (g + 1)), slice(KV_W + HEAD_DIM * g, KV_W + HEAD_DIM * (g + 1))
            lanes = slice(GB * g, GB * (g + 1))
            s_b, s_m = _attn_scores(_group_lanes(q_t, g), kvc_ref[:, ks], kvp_ref[:, ks], kvm_ref[LEAD:BLOCK, ks],
                                    sel, pen, pen_m, bias_s[:, lanes])
            sink = sink_s[:, lanes]
            m = jnp.maximum(jnp.maximum(jnp.max(s_b, axis=0, keepdims=True),
                                        jnp.max(s_m, axis=0, keepdims=True)), sink)
            p_b = jnp.exp(s_b - m)
            p_m = jnp.exp(s_m - m)
            l = jnp.sum(p_b, axis=0, keepdims=True) + jnp.sum(p_m, axis=0, keepdims=True) + jnp.exp(sink - m)
            p_c = jnp.where(sel, p_b, 0.0).astype(BF16)
            p_p = jnp.where(sel, 0.0, p_b).astype(BF16)
            o_t = (_dot(kvc_t[vs, :], p_c, NN) + _dot(kvp_t[vs, :], p_p, NN)
                   + _dot(kvm_ref[LEAD:BLOCK, vs], p_m.astype(BF16), TN))
            outs.append(o_t / l)
            lse = m + jnp.log(l)
            for j in range(GROUP):
                lse_ref[GROUP * g + j:GROUP * g + j + 1, :] = lse[:, BLOCK * j:BLOCK * (j + 1)]
        o_ref[...] = _head_rows(outs).T
        ag_finish()

    res = _pcall(
        body, name="attn_fwd", grid=(nb,),
        in_specs=[pl.BlockSpec(memory_space=pltpu.SMEM),
                  _rows(BLOCK, 512), _rows(BLOCK, 256),
                  pl.BlockSpec((BLOCK, 256), lambda b: (jnp.maximum(b - 1, 0), 0)),
                  _full((BLOCK, 256))] + [ANY] * ns,
        out_specs=[_rows(BLOCK, 512), pl.BlockSpec((N_HEADS, BLOCK), lambda b: (0, b))] + [ANY] * ns,
        out_shape=[jax.ShapeDtypeStruct((R, 512), F32), jax.ShapeDtypeStruct((N_HEADS, R), F32)]
        + _gathered_shapes(shards),
        scratch_shapes=_allgather_sems(ns) + [pltpu.VMEM((BLOCK, N_HEADS * BLOCK), F32),
                                              pltpu.VMEM((1, N_HEADS * BLOCK), F32)],
        semantics=("arbitrary",),
    )(sinks, q, kv, kv, kv, *shards)
    return res[0], res[1], res[2:]


def _ln_silu(y, lg, lb):
    mu = jnp.mean(y, axis=-1, keepdims=True)
    xc = y - mu
    rstd = lax.rsqrt(jnp.mean(xc * xc, axis=-1, keepdims=True) + EPS)
    xhat = xc * rstd
    yn = xhat * lg + lb
    return yn, xhat, rstd


PHASE_ROWS = HALO + TM - 8


def _phase_scratch():
    return pltpu.VMEM((7, PHASE_ROWS, CONV_W), F32)


def _phase_copies(src_s, ph_s):
    for b in range(1, 8):
        ph_s[b - 1] = src_s[pl.ds(b, PHASE_ROWS), :]


def _shifted(src_s, ph_s, start, rows):
    a8, b = (start // 8) * 8, start % 8
    if b == 0:
        return src_s[pl.ds(a8, rows), :]
    return ph_s[b - 1, pl.ds(a8, rows), :]


def _conv_fwd(ca, cg, conv_w, conv_b, ln_g, ln_b, shards):
    R = ca.shape[0]
    nt = R // TM
    hpt = TM // HALO
    ns = len(shards)

    def body(ca_ref, cg_ref, cah_ref, cgh_ref, w_ref, b_ref, lg_ref, lb_ref, *refs):
        ag_ins, (oc_ref, y_ref), ag_outs = refs[:ns], refs[ns:ns + 2], refs[ns + 2:2 * ns + 2]
        ag_sems, (u_s, uph_s) = refs[2 * ns + 2:2 * ns + 5], refs[2 * ns + 5:]
        i = pl.program_id(0)
        ag_finish = _carried_allgather(i, nt, shards, ag_ins + ag_outs + ag_sems)
        u_s[HALO:HALO + TM, :] = ca_ref[...] * _sigmoid(cg_ref[...])
        u_s[0:HALO, :] = jnp.where(i > 0, cah_ref[...] * _sigmoid(cgh_ref[...]), 0.0)
        _phase_copies(u_s, uph_s)
        for rc in range(TM // CONV_ROWS):
            base = rc * CONV_ROWS + HALO - (CONV_K - 1)
            acc = jnp.zeros((CONV_ROWS, CONV_W), F32) + b_ref[...]
            for k in range(CONV_K):
                acc = acc + _shifted(u_s, uph_s, base + k, CONV_ROWS) * w_ref[k:k + 1, :]
            rows = slice(rc * CONV_ROWS, (rc + 1) * CONV_ROWS)
            y_ref[rows, :] = acc
            yn, _, _ = _ln_silu(acc, lg_ref[...], lb_ref[...])
            oc_ref[rows, :] = yn * _sigmoid(yn)
        ag_finish()

    prev_halo = pl.BlockSpec((HALO, CONV_W), lambda i: (jnp.maximum(i * hpt - 1, 0), 0))
    anywhere = pl.BlockSpec(memory_space=pl.ANY)
    res = _pcall(
        body, name="conv_fwd", grid=(nt,),
        in_specs=[_rows(TM, CONV_W), _rows(TM, CONV_W), prev_halo, prev_halo,
                  _full((CONV_K, CONV_W)), _full((1, CONV_W)), _full((1, CONV_W)), _full((1, CONV_W))]
        + [anywhere] * ns,
        out_specs=[_rows(TM, CONV_W), _rows(TM, CONV_W)] + [anywhere] * ns,
        out_shape=[jax.ShapeDtypeStruct((R, CONV_W), F32), jax.ShapeDtypeStruct((R, CONV_W), F32)]
        + _gathered_shapes(shards),
        scratch_shapes=_allgather_sems(ns) + [pltpu.VMEM((HALO + TM, CONV_W), F32), _phase_scratch()],
        semantics=("arbitrary",),
    )(ca, cg, ca, cg, conv_w, conv_b, ln_g, ln_b, *shards)
    return res[:2], res[2:]


def _outproj_fwd(h0, o_attn, o_conv, ga, gc, w_out):
    R = h0.shape[0]

    def body(h_ref, oa_ref, oc_ref, ga_ref, gc_ref, w_ref, h1_ref):
        oa, oc = oa_ref[...], oc_ref[...]
        ma = (oa * _rms_stats(oa) * ga_ref[...]).astype(BF16)
        mc = (oc * _rms_stats(oc) * gc_ref[...]).astype(BF16)
        h1_ref[...] = h_ref[...] + _dot(ma, w_ref[0:512, :], NN) + _dot(mc, w_ref[512:1024, :], NN)

    return _pcall(
        body, name="outproj_fwd", grid=(R // TM,),
        in_specs=[_rows(TM, D_MODEL), _rows(TM, 512), _rows(TM, 512), _full((1, 512)), _full((1, 512)),
                  _full((D_MODEL, D_MODEL))],
        out_specs=_rows(TM, D_MODEL),
        out_shape=jax.ShapeDtypeStruct((R, D_MODEL), F32),
        semantics=("parallel",),
    )(h0, o_attn, o_conv, ga, gc, w_out)


def _target_copy(tgt_hbm, tgt_s, sem, i, first):
    if first:
        return pltpu.make_async_copy(tgt_hbm.at[pl.ds(0, TM - BLOCK)], tgt_s.at[pl.ds(BLOCK, TM - BLOCK)], sem)
    return pltpu.make_async_copy(tgt_hbm.at[pl.ds(i * TM - BLOCK, TM)], tgt_s, sem)


def _resident(shape):
    nd = len(shape)
    return pl.BlockSpec(shape, lambda *_: (0,) * nd, pipeline_mode=pl.Buffered(1))


def _ffn_fwd(h1, g2, wg_t, wu_t, wd, gf, target):
    R = h1.shape[0]
    nt = R // TM

    def body(h1_ref, g2_ref, wg_ref, wu_ref, wd_ref, gf_ref, tgt_hbm,
             gate_ref, up_ref, act_s, dh2_ref, loss_ref, dgf_ref, tgt_s, sem):
        i = pl.program_id(0)

        @pl.when(i == 0)
        def _():
            loss_ref[...] = jnp.zeros_like(loss_ref)
            dgf_ref[...] = jnp.zeros_like(dgf_ref)
            tgt_s[0:BLOCK, :] = jnp.zeros((BLOCK, D_MODEL), F32)
            _target_copy(tgt_hbm, tgt_s, sem, i, True).start()

        pl.when(i > 0)(lambda: _target_copy(tgt_hbm, tgt_s, sem, i, False).start())
        h1 = h1_ref[...]
        hn = (h1 * _rms_stats(h1) * g2_ref[...]).astype(BF16)
        for cs in FF_SUB:
            gate = _dot(hn, wg_ref[cs, :], NT)
            up = _dot(hn, wu_ref[cs, :], NT)
            gate_ref[:, cs] = gate.astype(BF16)
            up_ref[:, cs] = up.astype(BF16)
            act_s[:, cs] = (gate * _sigmoid(gate) * up).astype(BF16)
        part = _dot(act_s[...], wd_ref[...], NN)
        pl.when(i == 0)(lambda: _target_copy(tgt_hbm, tgt_s, sem, i, True).wait())
        pl.when(i > 0)(lambda: _target_copy(tgt_hbm, tgt_s, sem, i, False).wait())
        h2 = h1 + part
        rf = _rms_stats(h2)
        gf = gf_ref[...]
        row = lax.broadcasted_iota(jnp.int32, (TM, 1), 0) + i * TM
        err = jnp.where(row >= BLOCK, h2 * rf * gf - tgt_s[...], 0.0)
        dy = err * (1.0 / D_MODEL)
        dh2, dgf = _rms_bwd(dy, h2, rf, gf)
        dh2_ref[...] = dh2
        loss_ref[...] += (0.5 / D_MODEL) * jnp.sum(err * err)
        dgf_ref[...] += dgf

    wspec = _resident((D_FF, D_MODEL))
    return _pcall(
        body, name="ffn_fwd", grid=(nt,),
        in_specs=[_rows(TM, D_MODEL), _full((1, D_MODEL)), wspec, wspec, wspec, _full((1, D_MODEL)),
                  pl.BlockSpec(memory_space=pl.ANY)],
        out_specs=[_rows(TM, D_FF), _rows(TM, D_FF), _rows(TM, D_FF), _rows(TM, D_MODEL), _full((8, 128)),
                   _full((1, D_MODEL))],
        out_shape=[jax.ShapeDtypeStruct((R, D_FF), BF16)] * 3
        + [jax.ShapeDtypeStruct((R, D_MODEL), F32),
           jax.ShapeDtypeStruct((8, 128), F32), jax.ShapeDtypeStruct((1, D_MODEL), F32)],
        scratch_shapes=[pltpu.VMEM((TM, D_MODEL), F32), pltpu.SemaphoreType.DMA],
        semantics=("arbitrary",),
    )(h1, g2, wg_t, wu_t, wd, gf, target)


def _ffn_bwd(dh2, h1, g2, gate, up, wg_t, wu_t, wd):
    R = h1.shape[0]
    nt = R // TM
    act_shape = jax.ShapeDtypeStruct((R, D_FF), BF16)

    def act_body(dh2_ref, gate_ref, up_ref, wd_ref, dgate_ref, dup_ref):
        dhb = dh2_ref[...].astype(BF16)
        for cs in FF_SUB:
            dact = _dot(dhb, wd_ref[cs, :], NT)
            gate = gate_ref[:, cs].astype(F32)
            up = up_ref[:, cs].astype(F32)
            sig = _sigmoid(gate)
            dgate_ref[:, cs] = (dact * up * (sig * (1.0 + gate * (1.0 - sig)))).astype(BF16)
            dup_ref[:, cs] = (dact * (gate * sig)).astype(BF16)

    dgate, dup = _pcall(
        act_body, name="ffn_bwd_act", grid=(nt,),
        in_specs=[_rows(TM, D_MODEL), _rows(TM, D_FF), _rows(TM, D_FF), _resident((D_FF, D_MODEL))],
        out_specs=[_rows(TM, D_FF), _rows(TM, D_FF)], out_shape=[act_shape, act_shape],
        semantics=("parallel",),
    )(dh2, gate, up, wd)

    def in_body(dh2_ref, h1_ref, g2_ref, dgate_ref, dup_ref, wg_ref, wu_ref, hn_ref, dh1_ref, dg2_ref):
        @pl.when(pl.program_id(0) == 0)
        def _():
            dg2_ref[...] = jnp.zeros_like(dg2_ref)

        dhn = _dot(dgate_ref[...], wg_ref[...], NN) + _dot(dup_ref[...], wu_ref[...], NN)
        h1 = h1_ref[...]
        r = _rms_stats(h1)
        g2 = g2_ref[...]
        hn_ref[...] = (h1 * r * g2).astype(BF16)
        dx, dg = _rms_bwd(dhn, h1, r, g2)
        dh1_ref[...] = dh2_ref[...] + dx
        dg2_ref[...] += dg

    hn2, dh1, dg2 = _pcall(
        in_body, name="ffn_bwd_in", grid=(nt,),
        in_specs=[_rows(TM, D_MODEL), _rows(TM, D_MODEL), _full((1, D_MODEL)), _rows(TM, D_FF), _rows(TM, D_FF),
                  _resident((D_FF, D_MODEL)), _resident((D_FF, D_MODEL))],
        out_specs=[_rows(TM, D_MODEL), _rows(TM, D_MODEL), _full((1, D_MODEL))],
        out_shape=[jax.ShapeDtypeStruct((R, D_MODEL), BF16), jax.ShapeDtypeStruct((R, D_MODEL), F32),
                   jax.ShapeDtypeStruct((1, D_MODEL), F32)],
        semantics=("arbitrary",),
    )(dh2, h1, g2, dgate, dup, wg_t, wu_t)
    return dgate, dup, hn2, dh1, dg2


def _wgrad(a, b, tm, name, partials=()):
    K, M = a.shape
    N = b.shape[1]
    tk = K // WGRAD_K_TILES if K % (WGRAD_K_TILES * BLOCK) == 0 else TM
    nm, nk, npart = M // tm, K // tk, len(partials)

    def body(a_ref, b_ref, *refs):
        p_ins, o_ref, p_outs, sems = refs[:npart], refs[npart], refs[npart + 1:2 * npart + 1], refs[2 * npart + 1:]
        step = pl.program_id(0) * nk + pl.program_id(1)
        exchange = functools.partial(_chip_copies, p_ins, p_outs, *sems)
        if npart:
            _hosted(step, exchange)

        @pl.when(pl.program_id(1) == 0)
        def _():
            o_ref[...] = jnp.zeros_like(o_ref)

        o_ref[...] += _dot(a_ref[...], b_ref[...].astype(BF16), TN)
        if npart:
            _hosted_wait(step, nm * nk, exchange)

    res = _pcall(
        body, name=name, grid=(nm, nk),
        in_specs=[pl.BlockSpec((tk, tm), lambda m, k: (k, m)), pl.BlockSpec((tk, N), lambda m, k: (k, 0))]
        + [ANY] * npart,
        out_specs=[pl.BlockSpec((tm, N), lambda m, k: (m, 0))] + [ANY] * npart,
        out_shape=[jax.ShapeDtypeStruct((M, N), F32)] + _chip_shapes(partials),
        scratch_shapes=_sem_pair(3 * npart) if npart else [],
        semantics=("arbitrary", "arbitrary"),
    )(a, b, *partials)
    return (res[0], res[1:]) if npart else res[0]


def _outproj_bwd(dh1, o_attn, o_conv, ga, gc, w_out, grads):
    R = dh1.shape[0]
    nt, ng = R // TM, len(grads)

    def body(dh1_ref, oa_ref, oc_ref, ga_ref, gc_ref, w_ref, *refs):
        g_ins, (doa_ref, doc_ref, mixed_ref, dga_ref, dgc_ref) = refs[:ng], refs[ng:ng + 5]
        g_outs, (send_sems, recv_sems) = refs[ng + 5:2 * ng + 5], refs[2 * ng + 5:]
        exchange = functools.partial(_sibling_copies, g_ins, g_outs, send_sems, recv_sems)
        _hosted(pl.program_id(0), exchange)

        @pl.when(pl.program_id(0) == 0)
        def _():
            dga_ref[...] = jnp.zeros_like(dga_ref)
            dgc_ref[...] = jnp.zeros_like(dgc_ref)

        dm = _dot(dh1_ref[...].astype(BF16), w_ref[...], NT)
        oa, oc = oa_ref[...], oc_ref[...]
        ra, rc = _rms_stats(oa), _rms_stats(oc)
        mixed_ref[:, 0:512] = (oa * ra * ga_ref[...]).astype(BF16)
        mixed_ref[:, 512:1024] = (oc * rc * gc_ref[...]).astype(BF16)
        doa, dga = _rms_bwd(dm[:, 0:512], oa, ra, ga_ref[...])
        doc, dgc = _rms_bwd(dm[:, 512:1024], oc, rc, gc_ref[...])
        doa_ref[...] = doa
        doc_ref[...] = doc
        dga_ref[...] += dga
        dgc_ref[...] += dgc
        _hosted_wait(pl.program_id(0), nt, exchange)

    res = _pcall(
        body, name="outproj_bwd", grid=(nt,),
        in_specs=[_rows(TM, D_MODEL), _rows(TM, 512), _rows(TM, 512), _full((1, 512)), _full((1, 512)),
                  _full((D_MODEL, D_MODEL))] + [ANY] * ng,
        out_specs=[_rows(TM, 512), _rows(TM, 512), _rows(TM, D_MODEL), _full((1, 512)), _full((1, 512))]
        + [ANY] * ng,
        out_shape=[jax.ShapeDtypeStruct((R, 512), F32), jax.ShapeDtypeStruct((R, 512), F32),
                   jax.ShapeDtypeStruct((R, D_MODEL), BF16),
                   jax.ShapeDtypeStruct((1, 512), F32), jax.ShapeDtypeStruct((1, 512), F32)]
        + _sibling_shapes(grads),
        scratch_shapes=_sem_pair(ng),
        semantics=("arbitrary",),
    )(dh1, o_attn, o_conv, ga, gc, w_out, *grads)
    return res[:5], res[5:]


def _conv_bwd(do_conv, y, ca, cg, conv_w, ln_g, ln_b, partials):
    R = ca.shape[0]
    nt = R // TM
    hpt = TM // HALO
    npart = len(partials)

    def body(do_ref, doh_ref, y_ref, yh_ref, ca_ref, cg_ref, cah_ref, cgh_ref, w_ref, lg_ref, lb_ref, *refs):
        p_ins, (dca_ref, dcg_ref, dw_ref, db_ref, dlg_ref, dlb_ref) = refs[:npart], refs[npart:npart + 6]
        p_outs, (send_sems, recv_sems, u_s, dy_s, uph_s, dyph_s) = refs[npart + 6:2 * npart + 6], refs[2 * npart + 6:]
        i = pl.program_id(0)
        exchange = functools.partial(_chip_copies, p_ins, p_outs, send_sems, recv_sems)
        _hosted(i, exchange)

        @pl.when(i == 0)
        def _():
            dw_ref[...] = jnp.zeros_like(dw_ref)
            db_ref[...] = jnp.zeros_like(db_ref)
            dlg_ref[...] = jnp.zeros_like(dlg_ref)
            dlb_ref[...] = jnp.zeros_like(dlb_ref)

        lg, lb = lg_ref[...], lb_ref[...]

        def ln_bwd(yv, dov):
            yn, xhat, rstd = _ln_silu(yv, lg, lb)
            sig = _sigmoid(yn)
            dyn = dov * (sig * (1.0 + yn * (1.0 - sig)))
            dxh = dyn * lg
            dyv = rstd * (dxh - jnp.mean(dxh, axis=-1, keepdims=True)
                          - xhat * jnp.mean(dxh * xhat, axis=-1, keepdims=True))
            return dyv, dyn, xhat

        dyv, dyn, xhat = ln_bwd(y_ref[...], do_ref[...])
        dy_s[0:TM, :] = dyv
        dlg_ref[...] += jnp.sum(dyn * xhat, axis=0, keepdims=True)
        dlb_ref[...] += jnp.sum(dyn, axis=0, keepdims=True)
        db_ref[...] += jnp.sum(dyv, axis=0, keepdims=True)
        dyh, _, _ = ln_bwd(yh_ref[...], doh_ref[...])
        dy_s[TM:TM + HALO, :] = jnp.where(i < nt - 1, dyh, 0.0)
        u_s[HALO:HALO + TM, :] = ca_ref[...] * _sigmoid(cg_ref[...])
        u_s[0:HALO, :] = jnp.where(i > 0, cah_ref[...] * _sigmoid(cgh_ref[...]), 0.0)
        _phase_copies(dy_s, dyph_s)
        _phase_copies(u_s, uph_s)

        for rc in range(TM // CONV_ROWS):
            acc = jnp.zeros((CONV_ROWS, CONV_W), F32)
            for k in range(CONV_K):
                acc = acc + _shifted(dy_s, dyph_s, rc * CONV_ROWS + CONV_K - 1 - k, CONV_ROWS) * w_ref[k:k + 1, :]
            rows = slice(rc * CONV_ROWS, (rc + 1) * CONV_ROWS)
            sg = _sigmoid(cg_ref[rows, :])
            dca_ref[rows, :] = (acc * sg).astype(BF16)
            dcg_ref[rows, :] = (acc * ca_ref[rows, :] * sg * (1.0 - sg)).astype(BF16)

        for k in range(CONV_K):
            prod = _shifted(u_s, uph_s, HALO - (CONV_K - 1) + k, TM) * dy_s[0:TM, :]
            dw_ref[k:k + 1, :] += jnp.sum(prod, axis=0, keepdims=True)
        _hosted_wait(i, nt, exchange)

    prev_halo = pl.BlockSpec((HALO, CONV_W), lambda i: (jnp.maximum(i * hpt - 1, 0), 0))
    next_halo = pl.BlockSpec((HALO, CONV_W), lambda i: (jnp.minimum((i + 1) * hpt, nt * hpt - 1), 0))
    vec = jax.ShapeDtypeStruct((1, CONV_W), F32)
    res = _pcall(
        body, name="conv_bwd", grid=(nt,),
        in_specs=[_rows(TM, CONV_W), next_halo, _rows(TM, CONV_W), next_halo,
                  _rows(TM, CONV_W), _rows(TM, CONV_W), prev_halo, prev_halo,
                  _full((CONV_K, CONV_W)), _full((1, CONV_W)), _full((1, CONV_W))] + [ANY] * npart,
        out_specs=[_rows(TM, CONV_W), _rows(TM, CONV_W), _full((32, CONV_W)),
                   _full((1, CONV_W)), _full((1, CONV_W)), _full((1, CONV_W))] + [ANY] * npart,
        out_shape=[jax.ShapeDtypeStruct((R, CONV_W), BF16), jax.ShapeDtypeStruct((R, CONV_W), BF16),
                   jax.ShapeDtypeStruct((32, CONV_W), F32), vec, vec, vec] + _chip_shapes(partials),
        scratch_shapes=_sem_pair(3 * npart)
        + [pltpu.VMEM((HALO + TM, CONV_W), F32), pltpu.VMEM((TM + HALO, CONV_W), F32),
           _phase_scratch(), _phase_scratch()],
        semantics=("arbitrary",),
    )(do_conv, do_conv, y, y, ca, cg, ca, cg, conv_w, ln_g, ln_b, *partials)
    return res[:6], res[6:]


def _attn_bwd(q, kv, sinks, o, lse, do, grads, partials):
    R = q.shape[0]
    nb = R // BLOCK
    ng, npart = len(grads), len(partials)
    nx = ng + npart

    def body(sink_ref, q_ref, kvc_ref, kvp_ref, kvm_ref, o_ref, lse_ref, do_ref, *refs):
        x_ins, (dq_ref, dkv_ref, dkvm_ref, dsink_ref) = refs[:nx], refs[nx:nx + 4]
        x_outs = refs[nx + 4:2 * nx + 4]
        g_send, g_recv, p_send, p_recv, carry_s, cur_s, prev_s, bias_s, sink_s, delta_s = refs[2 * nx + 4:]
        b = pl.program_id(0)

        def exchange():
            return (_sibling_copies(x_ins[:ng], x_outs[:ng], g_send, g_recv)
                    + _chip_copies(x_ins[ng:], x_outs[ng:], p_send, p_recv))

        _hosted(b, exchange)

        @pl.when(b == 0)
        def _():
            dkvm_ref[...] = jnp.zeros_like(dkvm_ref)
            carry_s[...] = jnp.zeros_like(carry_s)
            for h in range(N_HEADS):
                dsink_ref[0, h] = 0.0
            _attn_tables(sink_ref, bias_s, sink_s)

        @pl.when(b < nb)
        def _():
            sel, pen, pen_m = _attn_masks(b)
            q_t = (q_ref[...] * ATTN_SCALE).T
            do_t = do_ref[...].astype(BF16).T
            kvc_t, kvp_t = kvc_ref[...].T, kvp_ref[...].T
            prod = do_ref[...] * o_ref[...]
            hi = prod.astype(BF16)
            lo = (prod - hi.astype(F32)).astype(BF16)
            head_of = lax.broadcasted_iota(jnp.int32, (N_HEADS, ATTN_W), 1) // HEAD_DIM
            ind = (head_of == lax.broadcasted_iota(jnp.int32, (N_HEADS, ATTN_W), 0)).astype(BF16)
            delta_s[...] = _dot(ind, hi, NT) + _dot(ind, lo, NT)
            dqs = []
            for g in range(N_HEADS // GROUP):
                ks, vs = slice(HEAD_DIM * g, HEAD_DIM * (g + 1)), slice(KV_W + HEAD_DIM * g, KV_W + HEAD_DIM * (g + 1))
                lanes = slice(GB * g, GB * (g + 1))
                qg, dog = _group_lanes(q_t, g), _group_lanes(do_t, g)
                kc, kp, km = kvc_ref[:, ks], kvp_ref[:, ks], kvm_ref[LEAD:BLOCK, ks]
                vc, vp, vm = kvc_ref[:, vs], kvp_ref[:, vs], kvm_ref[LEAD:BLOCK, vs]
                s_b, s_m = _attn_scores(qg, kc, kp, km, sel, pen, pen_m, bias_s[:, lanes])
                lse, delta = _head_lanes(lse_ref, g), _head_lanes(delta_s, g)
                p_b = jnp.exp(s_b - lse)
                p_m = jnp.exp(s_m - lse)
                dp_b = jnp.where(sel, _dot(vc, dog, NN), _dot(vp, dog, NN))
                ds_b = p_b * (dp_b - delta)
                ds_m = (p_m * (_dot(vm, dog, NN) - delta)).astype(BF16)
                dsk = jnp.exp(sink_s[:, lanes] - lse) * delta
                for j in range(GROUP):
                    dsink_ref[0, GROUP * g + j] += -jnp.sum(dsk[:, BLOCK * j:BLOCK * (j + 1)])
                ds_c = jnp.where(sel, ds_b, 0.0).astype(BF16)
                ds_p = jnp.where(sel, 0.0, ds_b).astype(BF16)
                p_c = jnp.where(sel, p_b, 0.0).astype(BF16)
                p_p = jnp.where(sel, 0.0, p_b).astype(BF16)
                dqs.append((_dot(kvc_t[ks, :], ds_c, NN) + _dot(kvp_t[ks, :], ds_p, NN)
                            + _dot(km, ds_m, TN)) * ATTN_SCALE)
                cur_s[:, ks] = _dot(ds_c, qg, NT)
                cur_s[:, vs] = _dot(p_c, dog, NT)
                prev_s[:, ks] = _dot(ds_p, qg, NT)
                prev_s[:, vs] = _dot(p_p, dog, NT)
                dkvm_ref[:, ks] += _dot(ds_m, qg, NT)
                dkvm_ref[:, vs] += _dot(p_m.astype(BF16), dog, NT)
            dq_ref[...] = _head_rows(dqs).astype(BF16).T
            dkv_ref[...] = (carry_s[...] + prev_s[...]).astype(BF16)
            carry_s[...] = cur_s[...]

        @pl.when(b == nb)
        def _():
            dkv_ref[...] = carry_s[...].astype(BF16)

        _hosted_wait(b, nb + 1, exchange)

    def at(off):
        return lambda b: (jnp.clip(b + off, 0, nb - 1), 0)

    blk = lambda cols, off=0: pl.BlockSpec((BLOCK, cols), at(off))
    res = _pcall(
        body, name="attn_bwd", grid=(nb + 1,),
        in_specs=[pl.BlockSpec(memory_space=pltpu.SMEM), blk(512), blk(256), blk(256, -1), _full((BLOCK, 256)),
                  blk(512), pl.BlockSpec((N_HEADS, BLOCK), lambda b: (0, jnp.minimum(b, nb - 1))), blk(512)]
        + [ANY] * nx,
        out_specs=[blk(512), blk(256, -1), _full((N_META, 256)), pl.BlockSpec(memory_space=pltpu.SMEM)]
        + [ANY] * nx,
        out_shape=[jax.ShapeDtypeStruct((R, 512), BF16), jax.ShapeDtypeStruct((R, 256), BF16),
                   jax.ShapeDtypeStruct((N_META, 256), F32), jax.ShapeDtypeStruct((1, N_HEADS), F32)]
        + _sibling_shapes(grads) + _chip_shapes(partials),
        scratch_shapes=_sem_pair(ng) + _sem_pair(3 * npart) + [pltpu.VMEM((BLOCK, 256), F32)] * 3
        + [pltpu.VMEM((BLOCK, N_HEADS * BLOCK), F32), pltpu.VMEM((1, N_HEADS * BLOCK), F32),
           pltpu.VMEM((N_HEADS, BLOCK), F32)],
        semantics=("arbitrary",),
    )(sinks, q, kv, kv, kv, o, lse, do, *grads, *partials)
    return res[:4], res[4:4 + ng], res[4 + ng:]


def _inproj_bwd(dh1, h0, g1, dq, dkv, dkvm, dca, dcg, w_in_t):
    R = h0.shape[0]
    nt = R // TM
    assert nt >= 2

    def body(dh1_ref, h0_ref, g_ref, dq_ref, dkv_ref, dkvm_ref, dca_ref, dcg_ref, w_ref,
             gx_hbm, dmeta_ref, dproj_ref, hn_ref, dg_ref, dx_s, gx_sems):
        i = pl.program_id(0)

        def gx_copy(step, slot, first):
            if first:
                return pltpu.make_async_copy(dx_s.at[slot, pl.ds(BLOCK, TM - BLOCK)],
                                             gx_hbm.at[pl.ds(0, TM - BLOCK)], gx_sems.at[slot])
            return pltpu.make_async_copy(
                dx_s.at[slot], gx_hbm.at[pl.ds(pl.multiple_of(step * TM - BLOCK, BLOCK), TM)], gx_sems.at[slot])

        @pl.when(i == 0)
        def _():
            dg_ref[...] = jnp.zeros_like(dg_ref)

        dproj_ref[:, 0:512] = dq_ref[...]
        dproj_ref[:, 512:768] = dkv_ref[...]
        dproj_ref[:, 768:1280] = dca_ref[...]
        dproj_ref[:, 1280:1792] = dcg_ref[...]

        @pl.when(i == 0)
        def _():
            dproj_ref[LEAD:BLOCK, 512:768] = dkvm_ref[...].astype(BF16)

        dhn = _dot(dproj_ref[...], w_ref[...], NN)
        h = h0_ref[...]
        r = _rms_stats(h)
        g = g_ref[...]
        hn_ref[...] = (h * r * g).astype(BF16)
        dx, dg = _rms_bwd(dhn, h, r, g)
        dg_ref[...] += dg
        slot = i % 2
        pl.when(i == 2)(lambda: gx_copy(0, 0, True).wait())
        pl.when(i > 2)(lambda: gx_copy(i - 2, slot, False).wait())
        dx_s[slot] = dh1_ref[...] + dx

        @pl.when(i == 0)
        def _():
            dmeta_ref[...] = dx_s[0, LEAD:BLOCK, :]
            gx_copy(0, 0, True).start()

        pl.when(i > 0)(lambda: gx_copy(i, slot, False).start())

        @pl.when(i == nt - 1)
        def _():
            gx_copy(nt - 2, (nt - 2) % 2, nt == 2).wait()
            gx_copy(nt - 1, (nt - 1) % 2, False).wait()

    return _pcall(
        body, name="inproj_bwd", grid=(nt,),
        in_specs=[_rows(TM, D_MODEL), _rows(TM, D_MODEL), _full((1, D_MODEL)), _rows(TM, 512), _rows(TM, 256),
                  _full((N_META, 256)), _rows(TM, 512), _rows(TM, 512), _full((1792, D_MODEL))],
        out_specs=[ANY, _full((N_META, D_MODEL)), _rows(TM, 1792), _rows(TM, D_MODEL), _full((1, D_MODEL))],
        out_shape=[jax.ShapeDtypeStruct((R - BLOCK, D_MODEL), F32), jax.ShapeDtypeStruct((N_META, D_MODEL), F32),
                   jax.ShapeDtypeStruct((R, 1792), BF16),
                   jax.ShapeDtypeStruct((R, D_MODEL), BF16), jax.ShapeDtypeStruct((1, D_MODEL), F32)],
        scratch_shapes=[pltpu.VMEM((2, TM, D_MODEL), F32), pltpu.SemaphoreType.DMA((2,))],
        semantics=("arbitrary",),
    )(dh1, h0, g1, dq, dkv, dkvm, dca, dcg, w_in_t)


ANY = pl.BlockSpec(memory_space=pl.ANY)


def _position():
    return lax.axis_index("x"), lax.axis_index("y"), lax.axis_index("c")


def _device_number(p):
    return 4 * p[0] + 2 * p[1] + p[2]


def _two_level_allgather(ins, outs, block, send_sems, recv_sems, local_sems, sem_base=0):
    n = len(ins)
    x, y, c = _position()
    me, sibling = (x, y, c), (x, y, 1 - c)
    chips = [(1 - x, y), (x, 1 - y), (1 - x, 1 - y)]

    def copy(w, k, origin, to, src=None):
        return pltpu.make_async_remote_copy(
            src_ref=block(w, origin) if src is None else src, dst_ref=block(w, origin),
            send_sem=send_sems.at[sem_base + 7 * w + k], recv_sem=recv_sems.at[sem_base + 7 * w + k],
            device_id=to, device_id_type=MESH)

    def mine(w):
        return pltpu.make_async_copy(ins[w], block(w, me), local_sems.at[w])

    def own(w):
        return [copy(w, 0, me, sibling, src=ins[w])] + [
            copy(w, 1 + j, me, (*chip, c), src=ins[w]) for j, chip in enumerate(chips)]

    def passed(w):
        return [copy(w, 4 + j, (*chip, c), sibling) for j, chip in enumerate(chips)]

    def start():
        for w in range(n):
            mine(w).start()
        for w in range(n):
            for cp in own(w):
                cp.start()

    def forward(w):
        fw = passed(w)
        for j, chip in enumerate(chips):
            copy(w, 1 + j, (*chip, c), me).wait_recv()
            fw[j].start()

    def finish():
        for w in range(n):
            copy(w, 0, sibling, me).wait_recv()
            for j, chip in enumerate(chips):
                copy(w, 4 + j, (*chip, 1 - c), me).wait_recv()
        for w in range(n):
            for cp in own(w) + passed(w):
                cp.wait_send()
            mine(w).wait()

    return start, forward, finish


def _carried_allgather(step, n_steps, shards, refs):
    ns = len(shards)
    ins, outs, (send_sems, recv_sems, local_sems) = refs[:ns], refs[ns:2 * ns], refs[2 * ns:]
    start, forward, finish = _two_level_allgather(
        ins, outs, _row_block(outs, [s.shape[0] for s in shards]), send_sems, recv_sems, local_sems)
    pl.when(step == 0)(start)
    total = sum(s.shape[0] for s in shards)
    sent = 0
    for w, s in enumerate(shards):
        sent += s.shape[0]
        pl.when(step == (AG_FORWARD_AT * sent * (n_steps - 1)) // (100 * total))(functools.partial(forward, w))
    return lambda: pl.when(step == n_steps - 1)(finish)


def _gathered_shapes(shards):
    return [jax.ShapeDtypeStruct((N_DEV * s.shape[0], s.shape[1]), s.dtype) for s in shards]


def _allgather_sems(ns):
    return _sem_pair(7 * ns) + [pltpu.SemaphoreType.DMA((ns,))]


def _blocking_allgather(ins, outs, block, send_sems, recv_sems, local_sems, sem_base=0):
    start, forward, finish = _two_level_allgather(ins, outs, block, send_sems, recv_sems, local_sems, sem_base)
    start()
    for w in range(len(ins)):
        forward(w)
    finish()


def _row_block(outs, rows):
    def block(w, p):
        return outs[w].at[pl.ds(pl.multiple_of(_device_number(p) * rows[w], 16), rows[w])]
    return block


def _sibling_copies(ins, outs, send_sems, recv_sems):
    x, y, c = _position()
    return [pltpu.make_async_remote_copy(
        src_ref=ins[w].at[:, 1 - c], dst_ref=outs[w], send_sem=send_sems.at[w], recv_sem=recv_sems.at[w],
        device_id=(x, y, 1 - c), device_id_type=MESH) for w in range(len(ins))]


def _chip_copies(ins, outs, send_sems, recv_sems):
    x, y, c = _position()
    chips = [(1 - x, y), (x, 1 - y), (1 - x, 1 - y)]
    return [pltpu.make_async_remote_copy(
        src_ref=ins[w].at[2 * chip[0] + chip[1]], dst_ref=outs[w].at[k],
        send_sem=send_sems.at[3 * w + k], recv_sem=recv_sems.at[3 * w + k],
        device_id=(*chip, c), device_id_type=MESH) for w in range(len(ins)) for k, chip in enumerate(chips)]


def _hosted(step, make_copies):
    @pl.when(step == 0)
    def _():
        for cp in make_copies():
            cp.start()


def _hosted_wait(step, n_steps, make_copies):
    @pl.when(step == n_steps - 1)
    def _():
        for cp in make_copies():
            cp.wait()


def _sem_pair(n):
    return [pltpu.SemaphoreType.DMA((n,)), pltpu.SemaphoreType.DMA((n,))]


def _allgather_params(shards, small):
    arrays = list(shards) + list(small)
    n, ns = len(arrays), len(shards)

    def body(*refs):
        ins, outs = refs[:n], refs[n:2 * n]
        send_sems, recv_sems, local_sems = refs[2 * n:]

        rows = _row_block(outs, [a.shape[0] for a in arrays])

        def block(w, p):
            return rows(w, p) if w < ns else outs[w].at[_device_number(p)]

        _blocking_allgather(ins, outs, block, send_sems, recv_sems, local_sems)

    out_shape = [jax.ShapeDtypeStruct((N_DEV * a.shape[0], a.shape[1]), a.dtype) for a in shards]
    out_shape += [jax.ShapeDtypeStruct((N_DEV,) + a.shape, a.dtype) for a in small]
    return _pcall(
        body, name="allgather_params", in_specs=[ANY] * n, out_specs=[ANY] * n, out_shape=out_shape,
        scratch_shapes=[pltpu.SemaphoreType.DMA((7 * n,)), pltpu.SemaphoreType.DMA((7 * n,)),
                        pltpu.SemaphoreType.DMA((n,))],
    )(*arrays)


def _reduce_siblings(grads):
    n = len(grads)

    def body(*refs):
        ins, outs = refs[:n], refs[n:2 * n]
        send_sems, recv_sems = refs[2 * n:]
        copies = _sibling_copies(ins, outs, send_sems, recv_sems)
        for cp in copies:
            cp.start()
        for cp in copies:
            cp.wait()

    return _pcall(
        body, name="reduce_siblings", in_specs=[ANY] * n, out_specs=[ANY] * n,
        out_shape=_sibling_shapes(grads), scratch_shapes=_sem_pair(n),
    )(*grads)


def _sibling_shapes(grads):
    return [jax.ShapeDtypeStruct((4,) + g.shape[2:], F32) for g in grads]


def _chip_shapes(partials):
    return [jax.ShapeDtypeStruct((3,) + p.shape[1:], F32) for p in partials]


def _add_sibling(grad, received, core, name):
    _, _, r, cols = grad.shape

    def body(core_ref, g_ref, r_ref, o_ref):
        o_ref[...] = g_ref[...] + r_ref[...]

    return pl.pallas_call(
        body, name=name,
        grid_spec=pltpu.PrefetchScalarGridSpec(
            num_scalar_prefetch=1, grid=(4,),
            in_specs=[pl.BlockSpec((None, None, r, cols), lambda s, core_ref: (s, core_ref[0], 0, 0)),
                      pl.BlockSpec((None, r, cols), lambda s, core_ref: (s, 0, 0))],
            out_specs=pl.BlockSpec((None, r, cols), lambda s, core_ref: (s, 0, 0))),
        out_shape=jax.ShapeDtypeStruct((4, r, cols), F32),
        compiler_params=pltpu.CompilerParams(vmem_limit_bytes=VMEM_LIMIT),
    )(core, grad, received)


def _reduce_chips(partials, small):
    n, ns = len(partials), len(small)

    def body(*refs):
        p_ins, s_ins = refs[:n], refs[n:n + ns]
        p_outs, s_outs = refs[n + ns:2 * n + ns], refs[2 * n + ns:2 * (n + ns)]
        send_sems, recv_sems, local_sems = refs[2 * (n + ns):]
        copies = _chip_copies(p_ins, p_outs, send_sems, recv_sems)
        for cp in copies:
            cp.start()
        _blocking_allgather(s_ins, s_outs, lambda w, p: s_outs[w].at[_device_number(p)],
                            send_sems, recv_sems, local_sems, sem_base=3 * n)
        for cp in copies:
            cp.wait()

    out_shape = _chip_shapes(partials)
    out_shape += [jax.ShapeDtypeStruct((N_DEV,) + a.shape, a.dtype) for a in small]
    nsem = 3 * n + 7 * ns
    return _pcall(
        body, name="reduce_chips", in_specs=[ANY] * (n + ns), out_specs=[ANY] * (n + ns), out_shape=out_shape,
        scratch_shapes=[pltpu.SemaphoreType.DMA((nsem,)), pltpu.SemaphoreType.DMA((nsem,)),
                        pltpu.SemaphoreType.DMA((ns,))],
    )(*partials, *small)


def _adam(g, w, m, v):
    m = ADAM_B1 * m + (1.0 - ADAM_B1) * g
    v = ADAM_B2 * v + (1.0 - ADAM_B2) * (g * g)
    m_hat = m / (1.0 - ADAM_B1 ** ADAM_STEP)
    v_hat = v / (1.0 - ADAM_B2 ** ADAM_STEP)
    delta = -ADAM_LR * (m_hat / (jnp.sqrt(v_hat) + ADAM_EPS) + ADAM_WD * w)
    return delta, m, v


def _adamw(partial, received, slot, w, m, v, name):
    _, r, cols = partial.shape

    def body(slot_ref, p_ref, r_ref, w_ref, m_ref, v_ref, g_ref, d_ref, nm_ref, nv_ref):
        g = p_ref[...] + r_ref[0] + r_ref[1] + r_ref[2]
        g_ref[...] = g
        d_ref[...], nm_ref[...], nv_ref[...] = _adam(g, w_ref[...], m_ref[...], v_ref[...])

    whole = pl.BlockSpec((r, cols), lambda i, slot_ref: (0, 0))
    out = jax.ShapeDtypeStruct((r, cols), F32)
    return pl.pallas_call(
        body, name=name,
        grid_spec=pltpu.PrefetchScalarGridSpec(
            num_scalar_prefetch=1, grid=(1,),
            in_specs=[pl.BlockSpec((None, r, cols), lambda i, slot_ref: (slot_ref[0], 0, 0)),
                      pl.BlockSpec((3, r, cols), lambda i, slot_ref: (0, 0, 0)), whole, whole, whole],
            out_specs=[whole, whole, whole, whole]),
        out_shape=[out, out, out, out],
        compiler_params=pltpu.CompilerParams(vmem_limit_bytes=VMEM_LIMIT),
    )(slot, partial, received, w, m, v)


def _adamw_small(dev, ga, gb, gc, params):
    names = ["meta", "attn_norm", "sinks", "conv_w", "conv_b", "ln_g", "ln_b", "attn_out", "conv_out",
             "ffn_norm", "final_norm"]
    flat = [a for p in params for a in p]
    n_in = len(flat)

    def body(dev_ref, ga_ref, gb_ref, gc_ref, *refs):
        ins, outs = refs[:n_in], refs[n_in:n_in + 4 * len(names)]
        loss_ref, sb, sc = refs[n_in + 4 * len(names):]
        a = ga_ref[0]
        sb[...] = gb_ref[0]
        sc[...] = gc_ref[0]
        for d in range(1, N_DEV):
            a = a + ga_ref[d]
            sb[...] += gb_ref[d]
            sc[...] += gc_ref[d]
        dev = dev_ref[0]
        grads = {
            "attn_norm": a[0:1, :], "ffn_norm": a[1:2, :], "final_norm": a[2:3, :],
            "conv_b": a[3:4, 0:512], "ln_g": a[3:4, 512:1024], "ln_b": a[4:5, 0:512],
            "attn_out": a[4:5, 512:1024], "conv_out": a[5:6, 0:512], "sinks": a[5:6, 512:512 + N_HEADS],
            "meta": sb[pl.ds(pl.multiple_of(dev * N_META, N_META), N_META), :],
            "conv_w": sc[pl.ds(pl.multiple_of(dev * 32, 32), 32), :][0:CONV_K, :],
        }
        for idx, nm in enumerate(names):
            w_ref, m_ref, v_ref = ins[3 * idx:3 * idx + 3]
            g = grads[nm]
            delta, m, v = _adam(g, w_ref[...], m_ref[...], v_ref[...])
            o = outs[4 * idx:4 * idx + 4]
            o[0][...], o[1][...], o[2][...], o[3][...] = g, delta, m, v
        loss_ref[...] = a[6:7, 0:1]

    vm = pl.BlockSpec(memory_space=pltpu.VMEM)
    out_shape = [jax.ShapeDtypeStruct(p[0].shape, F32) for p in params for _ in range(4)]
    out_shape.append(jax.ShapeDtypeStruct((1, 1), F32))
    res = pl.pallas_call(
        body, name="adamw_small",
        grid_spec=pltpu.PrefetchScalarGridSpec(
            num_scalar_prefetch=1, grid=(1,),
            in_specs=[pl.BlockSpec(ga.shape, lambda i, d: (0, 0, 0)), pl.BlockSpec(gb.shape, lambda i, d: (0, 0, 0)),
                      pl.BlockSpec(gc.shape, lambda i, d: (0, 0, 0))]
            + [pl.BlockSpec(a.shape, lambda i, d: (0, 0)) for a in flat],
            out_specs=[pl.BlockSpec(s.shape, lambda i, d: (0, 0)) for s in out_shape],
            scratch_shapes=[pltpu.VMEM(gb.shape[1:], F32), pltpu.VMEM(gc.shape[1:], F32)]),
        out_shape=out_shape,
        compiler_params=pltpu.CompilerParams(vmem_limit_bytes=VMEM_LIMIT),
    )(dev, ga, gb, gc, *flat)
    return [res[4 * i:4 * i + 4] for i in range(len(names))], res[-1]


def kernel(x, meta_tokens, attn_norm_g, w_in, attn_sinks, conv_w, conv_b, conv_ln_g, conv_ln_b, attn_out_g, conv_out_g, w_out, ffn_norm_g, w_gate, w_up, w_down, final_norm_g, loss_target, m_meta_tokens, m_attn_norm_g, m_w_in, m_attn_sinks, m_conv_w, m_conv_b, m_conv_ln_g, m_conv_ln_b, m_attn_out_g, m_conv_out_g, m_w_out, m_ffn_norm_g, m_w_gate, m_w_up, m_w_down, m_final_norm_g, v_meta_tokens, v_attn_norm_g, v_w_in, v_attn_sinks, v_conv_w, v_conv_b, v_conv_ln_g, v_conv_ln_b, v_attn_out_g, v_conv_out_g, v_w_out, v_ffn_norm_g, v_w_gate, v_w_up, v_w_down, v_final_norm_g):
    xi, yi, ci = _position()
    dev = jnp.reshape(_device_number((xi, yi, ci)), (1,)).astype(jnp.int32)
    core = jnp.reshape(ci, (1,)).astype(jnp.int32)
    slot = jnp.reshape(2 * xi + yi, (1,)).astype(jnp.int32)

    w_in_t, meta_st, convw_st = _allgather_params([w_in[0].T.astype(BF16)], [meta_tokens, conv_w[0]])
    meta_full = jnp.transpose(meta_st, (1, 0, 2)).reshape(N_META, D_MODEL)
    convw_full = jnp.transpose(convw_st, (1, 0, 2)).reshape(CONV_K, CONV_W)

    final_g = final_norm_g.reshape(1, D_MODEL)

    (h0, q, kv, ca, cg), (w_out_b,) = _inproj_fwd(x[0], meta_full, attn_norm_g, w_in_t, [w_out[0].astype(BF16)])
    o_attn, lse, (wg_t, wu_t) = _attn_fwd(
        q, kv, attn_sinks, [w_gate[0].T.astype(BF16), w_up[0].T.astype(BF16)])
    (o_conv, y_conv), (wd_b,) = _conv_fwd(ca, cg, convw_full, conv_b, conv_ln_g, conv_ln_b, [w_down[0].astype(BF16)])
    h1 = _outproj_fwd(h0, o_attn, o_conv, attn_out_g, conv_out_g, w_out_b)
    gate, up, act, dh2, loss_sum, dg_final = _ffn_fwd(h1, ffn_norm_g, wg_t, wu_t, wd_b, final_g, loss_target[0])

    def blocks(g):
        return g.reshape(4, 2, g.shape[0] // N_DEV, D_MODEL)

    def add_siblings(grads, received, tags):
        return [_add_sibling(g, r, core, "add_sibling_" + t) for g, r, t in zip(grads, received, tags)]

    dgate, dup, hn2, dh1, dg_ffn = _ffn_bwd(dh2, h1, ffn_norm_g, gate, up, wg_t, wu_t, wd_b)
    ffn_grads = [blocks(_wgrad(dgate, hn2, FF_CHUNK, "wgrad_gate")), blocks(_wgrad(dup, hn2, FF_CHUNK, "wgrad_up")),
                 blocks(_wgrad(act, dh2, FF_CHUNK, "wgrad_down"))]
    (do_attn, do_conv, mixed, dg_ao, dg_co), ffn_sib = _outproj_bwd(
        dh1, o_attn, o_conv, attn_out_g, conv_out_g, w_out_b, ffn_grads)
    ffn_sums = add_siblings(ffn_grads, ffn_sib, ("gate", "up", "down"))
    out_grads = [blocks(_wgrad(mixed, dh1, D_MODEL, "wgrad_out"))]
    (dca, dcg, dconvw, dconvb, dln_g, dln_b), gate_up_chips = _conv_bwd(
        do_conv, y_conv, ca, cg, convw_full, conv_ln_g, conv_ln_b, ffn_sums[:2])
    (dq, dkv, dkvm, dsinks), out_sib, down_chips = _attn_bwd(
        q, kv, attn_sinks, o_attn, lse, do_attn, out_grads, ffn_sums[2:])
    ffn_chips = list(gate_up_chips) + list(down_chips)
    out_sums = add_siblings(out_grads, out_sib, ("out",))
    grad_x, dmeta, dproj, hn1, dg_attn = _inproj_bwd(dh1, h0, attn_norm_g, dq, dkv, dkvm, dca, dcg, w_in_t)
    dwi_t, out_chips = _wgrad(dproj, hn1, 1792, "wgrad_in", out_sums)
    in_grads = [blocks(dwi_t)]
    in_sums = add_siblings(in_grads, _reduce_siblings(in_grads), ("in",))
    small_a = jnp.concatenate([
        dg_attn, dg_ffn, dg_final, jnp.concatenate([dconvb, dln_g], axis=1), jnp.concatenate([dln_b, dg_ao], axis=1),
        jnp.concatenate([dg_co, dsinks, jnp.zeros((1, 512 - N_HEADS), F32)], axis=1),
        jnp.concatenate([loss_sum[0:1, :], jnp.zeros((1, D_MODEL - 128), F32)], axis=1),
        jnp.zeros((1, D_MODEL), F32)], axis=0)
    small_b = jnp.transpose(dmeta.reshape(N_META, N_DEV, 128), (1, 0, 2)).reshape(N_DEV * N_META, 128)
    small_c = jnp.transpose(dconvw.reshape(32, N_DEV, 64), (1, 0, 2)).reshape(N_DEV * 32, 64)
    in_chips, ga, gb, gc = _reduce_chips(in_sums, [small_a, small_b, small_c])
    tags = ("in", "out", "gate", "up", "down")
    chip_sums = in_sums + out_sums + ffn_sums
    from_chips = [in_chips] + list(out_chips) + list(ffn_chips)

    big = [(True, w_in, m_w_in, v_w_in), (False, w_out, m_w_out, v_w_out), (True, w_gate, m_w_gate, v_w_gate),
           (True, w_up, m_w_up, v_w_up), (False, w_down, m_w_down, v_w_down)]
    big_out = {}
    for t, p, r, (transposed, w, m, v) in zip(tags, chip_sums, from_chips, big):
        rows = (lambda a: jnp.transpose(a[0])) if transposed else (lambda a: a[0])
        back = (lambda a: jnp.transpose(a)[None]) if transposed else (lambda a: a[None])
        big_out[t] = [back(a) for a in _adamw(p, r, slot, rows(w), rows(m), rows(v), "adamw_" + t)]

    small_params = [
        (meta_tokens, m_meta_tokens, v_meta_tokens), (attn_norm_g, m_attn_norm_g, v_attn_norm_g),
        (attn_sinks, m_attn_sinks, v_attn_sinks), (conv_w[0], m_conv_w[0], v_conv_w[0]),
        (conv_b, m_conv_b, v_conv_b), (conv_ln_g, m_conv_ln_g, v_conv_ln_g), (conv_ln_b, m_conv_ln_b, v_conv_ln_b),
        (attn_out_g, m_attn_out_g, v_attn_out_g), (conv_out_g, m_conv_out_g, v_conv_out_g),
        (ffn_norm_g, m_ffn_norm_g, v_ffn_norm_g),
        (final_g, m_final_norm_g.reshape(1, D_MODEL), v_final_norm_g.reshape(1, D_MODEL))]
    sm, loss = _adamw_small(dev, ga, gb, gc, small_params)
    sm[3] = [a[None] for a in sm[3]]
    sm[10] = [a.reshape(D_MODEL) for a in sm[10]]

    per_param = [sm[0], sm[1], big_out["in"], sm[2], sm[3], sm[4], sm[5], sm[6], sm[7], sm[8], big_out["out"],
                 sm[9], big_out["gate"], big_out["up"], big_out["down"], sm[10]]
    loss = loss.reshape(())
    outs = [loss, grad_x[None]]
    for kind in range(4):
        outs += [p[kind] for p in per_param]
    return tuple(outs)
```

```python
import functools
import math

import jax
import jax.numpy as jnp
from jax import lax
from jax.experimental import pallas as pl
from jax.experimental.pallas import tpu as pltpu

F32, BF16 = jnp.float32, jnp.bfloat16
MESH = pl.DeviceIdType.MESH

D_MODEL = 1024
N_META = 16
BLOCK = 128
LEAD = BLOCK - N_META
HEAD_DIM = 64
N_HEADS = 8
GROUP = 4
ATTN_W = 512
KV_W = 128
CONV_W = 512
CONV_K = 31
HALO = 32
D_FF = 2816
FF_CHUNK = D_FF // 2
FF_SUB = [slice(s, s + 256) for s in range(0, D_FF, 256)]
N_DEV = 8
EPS = 1e-5
NEG = -1e30
TM = 640
AG_FORWARD_AT = 85
WGRAD_K_TILES = 5
CONV_ROWS = 32
VMEM_LIMIT = 56 * 1024 * 1024

ADAM_LR, ADAM_B1, ADAM_B2, ADAM_EPS, ADAM_WD, ADAM_STEP = 0.001, 0.9, 0.999, 1e-08, 0.01, 10

NT = (((1,), (1,)), ((), ()))
NN = (((1,), (0,)), ((), ()))
TN = (((0,), (0,)), ((), ()))


def _dot(a, b, dims):
    return lax.dot_general(a, b, dims, preferred_element_type=F32)


def _sigmoid(x):
    return 1.0 / (1.0 + jnp.exp(-x))


def _pcall(body, *, name, out_shape, grid=None, in_specs=None, out_specs=None, scratch_shapes=(),
           semantics=None, **kw):
    params = dict(vmem_limit_bytes=VMEM_LIMIT)
    if semantics is not None:
        params["dimension_semantics"] = semantics
    extra = {}
    if grid is not None:
        extra["grid"] = grid
    if in_specs is not None:
        extra["in_specs"] = in_specs
    if out_specs is not None:
        extra["out_specs"] = out_specs
    return pl.pallas_call(body, name=name, out_shape=out_shape, scratch_shapes=list(scratch_shapes),
                          compiler_params=pltpu.CompilerParams(**params), **extra, **kw)


def _rows(tm, cols):
    return pl.BlockSpec((tm, cols), lambda i, *_: (i, 0))


def _full(shape):
    nd = len(shape)
    return pl.BlockSpec(shape, lambda *_: (0,) * nd)


def _rms_stats(x):
    return lax.rsqrt(jnp.mean(x * x, axis=-1, keepdims=True) + EPS)


def _rms_bwd(dy, x, r, g):
    t = dy * g
    dx = r * (t - x * (r * r) * jnp.mean(t * x, axis=-1, keepdims=True))
    dg = jnp.sum(dy * x * r, axis=0, keepdims=True)
    return dx, dg


def _inproj_fwd(x, meta, g1, w_in_t):
    R = x.shape[0] + BLOCK
    nt = R // TM
    assert nt >= 2

    def body(x_hbm, meta_ref, g_ref, w_ref, h0_ref, q_ref, kv_ref, ca_ref, cg_ref, x_s, sems):
        i = pl.program_id(0)
        slot = i % 2

        def x_copy(step, slot, first):
            if first:
                return pltpu.make_async_copy(x_hbm.at[pl.ds(0, TM - BLOCK)],
                                             x_s.at[slot, pl.ds(BLOCK, TM - BLOCK)], sems.at[slot])
            return pltpu.make_async_copy(
                x_hbm.at[pl.ds(pl.multiple_of(step * TM - BLOCK, BLOCK), TM)], x_s.at[slot], sems.at[slot])

        @pl.when(i == 0)
        def _():
            x_copy(0, 0, True).start()
            x_s[0, 0:LEAD, :] = jnp.zeros((LEAD, D_MODEL), F32)
            x_s[0, LEAD:BLOCK, :] = meta_ref[...]

        pl.when(i + 1 < nt)(lambda: x_copy(i + 1, 1 - slot, False).start())
        pl.when(i == 0)(lambda: x_copy(0, 0, True).wait())
        pl.when(i > 0)(lambda: x_copy(i, slot, False).wait())
        h = x_s[slot]
        h0_ref[...] = h
        hn = (h * _rms_stats(h) * g_ref[...]).astype(BF16)
        q_ref[...] = _dot(hn, w_ref[0:512, :], NT).astype(BF16)
        kv_ref[...] = _dot(hn, w_ref[512:768, :], NT).astype(BF16)
        ca_ref[...] = _dot(hn, w_ref[768:1280, :], NT)
        cg_ref[...] = _dot(hn, w_ref[1280:1792, :], NT)

    return _pcall(
        body, name="inproj_fwd", grid=(nt,),
        in_specs=[pl.BlockSpec(memory_space=pl.ANY), _full((N_META, D_MODEL)), _full((1, D_MODEL)),
                  _full((1792, D_MODEL))],
        out_specs=[_rows(TM, D_MODEL), _rows(TM, 512), _rows(TM, 256), _rows(TM, 512), _rows(TM, 512)],
        out_shape=[jax.ShapeDtypeStruct((R, D_MODEL), F32),
                   jax.ShapeDtypeStruct((R, 512), BF16), jax.ShapeDtypeStruct((R, 256), BF16),
                   jax.ShapeDtypeStruct((R, 512), F32), jax.ShapeDtypeStruct((R, 512), F32)],
        scratch_shapes=[pltpu.VMEM((2, TM, D_MODEL), F32), pltpu.SemaphoreType.DMA((2,))],
        semantics=("arbitrary",),
    )(x, meta, g1, w_in_t)


GB = GROUP * BLOCK
ATTN_SCALE = 1.0 / math.sqrt(HEAD_DIM)


def _group_lanes(xt, g):
    return jnp.concatenate(
        [xt[HEAD_DIM * (GROUP * g + j):HEAD_DIM * (GROUP * g + j + 1), :] for j in range(GROUP)], axis=1)


def _head_lanes(ref, g):
    return jnp.concatenate([ref[GROUP * g + j:GROUP * g + j + 1, :] for j in range(GROUP)], axis=1)


def _head_rows(xs):
    return jnp.concatenate([x[:, BLOCK * j:BLOCK * (j + 1)] for x in xs for j in range(GROUP)], axis=0)


def _attn_tables(sink_ref, bias_s, sink_s):
    kk = lax.broadcasted_iota(jnp.int32, (BLOCK, BLOCK), 0)
    ii = lax.broadcasted_iota(jnp.int32, (BLOCK, BLOCK), 1)
    dist = jnp.where(kk <= ii, ii - kk, ii - kk + BLOCK).astype(F32)
    for h in range(N_HEADS):
        bias_s[:, BLOCK * h:BLOCK * (h + 1)] = dist * -(2.0 ** -(h + 1))
        sink_s[:, BLOCK * h:BLOCK * (h + 1)] = jnp.zeros((1, BLOCK), F32) + sink_ref[0, h]


def _attn_masks(b):
    kk = lax.broadcasted_iota(jnp.int32, (BLOCK, GB), 0)
    ii = lax.broadcasted_iota(jnp.int32, (BLOCK, GB), 1) & (BLOCK - 1)
    sel = kk <= ii
    pen = jnp.where(sel, jnp.where(b >= 1, 0.0, NEG), jnp.where(b >= 2, 0.0, NEG))
    mj = lax.broadcasted_iota(jnp.int32, (N_META, GB), 0)
    mi = lax.broadcasted_iota(jnp.int32, (N_META, GB), 1) & (BLOCK - 1)
    pen_m = jnp.where((mj + LEAD) <= (mi + b * BLOCK), 0.0, NEG)
    return sel, pen, pen_m


def _attn_scores(qt, kc, kp, km, sel, pen, pen_m, bias):
    s_b = jnp.where(sel, _dot(kc, qt, NN), _dot(kp, qt, NN)) + bias + pen
    s_m = _dot(km, qt, NN) + pen_m
    return s_b, s_m


def _attn_fwd(q, kv, sinks, shards):
    R = q.shape[0]
    nb = R // BLOCK
    ns = len(shards)

    def body(sink_ref, q_ref, kvc_ref, kvp_ref, kvm_ref, *refs):
        ag_ins, (o_ref, lse_ref), ag_outs = refs[:ns], refs[ns:ns + 2], refs[ns + 2:2 * ns + 2]
        ag_sems, (bias_s, sink_s) = refs[2 * ns + 2:2 * ns + 5], refs[2 * ns + 5:]
        b = pl.program_id(0)
        ag_finish = _carried_allgather(b, nb, shards, ag_ins + ag_outs + ag_sems)
        pl.when(b == 0)(functools.partial(_attn_tables, sink_ref, bias_s, sink_s))
        sel, pen, pen_m = _attn_masks(b)
        q_t = (q_ref[...] * ATTN_SCALE).T
        kvc_t, kvp_t = kvc_ref[...].T, kvp_ref[...].T
        outs = []
        for g in range(N_HEADS // GROUP):
            ks, vs = slice(HEAD_DIM * g, HEAD_DIM * (g + 1)), slice(KV_W + HEAD_DIM * g, KV_W + HEAD_DIM * (g + 1))
            lanes = slice(GB * g, GB * (g + 1))
            s_b, s_m = _attn_scores(_group_lanes(q_t, g), kvc_ref[:, ks], kvp_ref[:, ks], kvm_ref[LEAD:BLOCK, ks],
                                    sel, pen, pen_m, bias_s[:, lanes])
            sink = sink_s[:, lanes]
            m = jnp.maximum(jnp.maximum(jnp.max(s_b, axis=0, keepdims=True),
                                        jnp.max(s_m, axis=0, keepdims=True)), sink)
            p_b = jnp.exp(s_b - m)
            p_m = jnp.exp(s_m - m)
            l = jnp.sum(p_b, axis=0, keepdims=True) + jnp.sum(p_m, axis=0, keepdims=True) + jnp.exp(sink - m)
            p_c = jnp.where(sel, p_b, 0.0).astype(BF16)
            p_p = jnp.where(sel, 0.0, p_b).astype(BF16)
            o_t = (_dot(kvc_t[vs, :], p_c, NN) + _dot(kvp_t[vs, :], p_p, NN)
                   + _dot(kvm_ref[LEAD:BLOCK, vs], p_m.astype(BF16), TN))
            outs.append(o_t / l)
            lse = m + jnp.log(l)
            for j in range(GROUP):
                lse_ref[GROUP * g + j:GROUP * g + j + 1, :] = lse[:, BLOCK * j:BLOCK * (j + 1)]
        o_ref[...] = _head_rows(outs).T
        ag_finish()

    res = _pcall(
        body, name="attn_fwd", grid=(nb,),
        in_specs=[pl.BlockSpec(memory_space=pltpu.SMEM),
                  _rows(BLOCK, 512), _rows(BLOCK, 256),
                  pl.BlockSpec((BLOCK, 256), lambda b: (jnp.maximum(b - 1, 0), 0)),
                  _full((BLOCK, 256))] + [ANY] * ns,
        out_specs=[_rows(BLOCK, 512), pl.BlockSpec((N_HEADS, BLOCK), lambda b: (0, b))] + [ANY] * ns,
        out_shape=[jax.ShapeDtypeStruct((R, 512), F32), jax.ShapeDtypeStruct((N_HEADS, R), F32)]
        + _gathered_shapes(shards),
        scratch_shapes=_allgather_sems(ns) + [pltpu.VMEM((BLOCK, N_HEADS * BLOCK), F32),
                                              pltpu.VMEM((1, N_HEADS * BLOCK), F32)],
        semantics=("arbitrary",),
    )(sinks, q, kv, kv, kv, *shards)
    return res[0], res[1], res[2:]


def _ln_silu(y, lg, lb):
    mu = jnp.mean(y, axis=-1, keepdims=True)
    xc = y - mu
    rstd = lax.rsqrt(jnp.mean(xc * xc, axis=-1, keepdims=True) + EPS)
    xhat = xc * rstd
    yn = xhat * lg + lb
    return yn, xhat, rstd


PHASE_ROWS = HALO + TM - 8


def _phase_scratch():
    return pltpu.VMEM((7, PHASE_ROWS, CONV_W), F32)


def _phase_copies(src_s, ph_s):
    for b in range(1, 8):
        ph_s[b - 1] = src_s[pl.ds(b, PHASE_ROWS), :]


def _shifted(src_s, ph_s, start, rows):
    a8, b = (start // 8) * 8, start % 8
    if b == 0:
        return src_s[pl.ds(a8, rows), :]
    return ph_s[b - 1, pl.ds(a8, rows), :]


def _conv_fwd(ca, cg, conv_w, conv_b, ln_g, ln_b, shards):
    R = ca.shape[0]
    nt = R // TM
    hpt = TM // HALO
    ns = len(shards)

    def body(ca_ref, cg_ref, cah_ref, cgh_ref, w_ref, b_ref, lg_ref, lb_ref, *refs):
        ag_ins, (oc_ref, y_ref), ag_outs = refs[:ns], refs[ns:ns + 2], refs[ns + 2:2 * ns + 2]
        ag_sems, (u_s, uph_s) = refs[2 * ns + 2:2 * ns + 5], refs[2 * ns + 5:]
        i = pl.program_id(0)
        ag_finish = _carried_allgather(i, nt, shards, ag_ins + ag_outs + ag_sems)
        u_s[HALO:HALO + TM, :] = ca_ref[...] * _sigmoid(cg_ref[...])
        u_s[0:HALO, :] = jnp.where(i > 0, cah_ref[...] * _sigmoid(cgh_ref[...]), 0.0)
        _phase_copies(u_s, uph_s)
        for rc in range(TM // CONV_ROWS):
            base = rc * CONV_ROWS + HALO - (CONV_K - 1)
            acc = jnp.zeros((CONV_ROWS, CONV_W), F32) + b_ref[...]
            for k in range(CONV_K):
                acc = acc + _shifted(u_s, uph_s, base + k, CONV_ROWS) * w_ref[k:k + 1, :]
            rows = slice(rc * CONV_ROWS, (rc + 1) * CONV_ROWS)
            y_ref[rows, :] = acc
            yn, _, _ = _ln_silu(acc, lg_ref[...], lb_ref[...])
            oc_ref[rows, :] = yn * _sigmoid(yn)
        ag_finish()

    prev_halo = pl.BlockSpec((HALO, CONV_W), lambda i: (jnp.maximum(i * hpt - 1, 0), 0))
    anywhere = pl.BlockSpec(memory_space=pl.ANY)
    res = _pcall(
        body, name="conv_fwd", grid=(nt,),
        in_specs=[_rows(TM, CONV_W), _rows(TM, CONV_W), prev_halo, prev_halo,
                  _full((CONV_K, CONV_W)), _full((1, CONV_W)), _full((1, CONV_W)), _full((1, CONV_W))]
        + [anywhere] * ns,
        out_specs=[_rows(TM, CONV_W), _rows(TM, CONV_W)] + [anywhere] * ns,
        out_shape=[jax.ShapeDtypeStruct((R, CONV_W), F32), jax.ShapeDtypeStruct((R, CONV_W), F32)]
        + _gathered_shapes(shards),
        scratch_shapes=_allgather_sems(ns) + [pltpu.VMEM((HALO + TM, CONV_W), F32), _phase_scratch()],
        semantics=("arbitrary",),
    )(ca, cg, ca, cg, conv_w, conv_b, ln_g, ln_b, *shards)
    return res[:2], res[2:]


def _outproj_fwd(h0, o_attn, o_conv, ga, gc, w_out):
    R = h0.shape[0]

    def body(h_ref, oa_ref, oc_ref, ga_ref, gc_ref, w_ref, h1_ref):
        oa, oc = oa_ref[...], oc_ref[...]
        ma = (oa * _rms_stats(oa) * ga_ref[...]).astype(BF16)
        mc = (oc * _rms_stats(oc) * gc_ref[...]).astype(BF16)
        h1_ref[...] = h_ref[...] + _dot(ma, w_ref[0:512, :], NN) + _dot(mc, w_ref[512:1024, :], NN)

    return _pcall(
        body, name="outproj_fwd", grid=(R // TM,),
        in_specs=[_rows(TM, D_MODEL), _rows(TM, 512), _rows(TM, 512), _full((1, 512)), _full((1, 512)),
                  _full((D_MODEL, D_MODEL))],
        out_specs=_rows(TM, D_MODEL),
        out_shape=jax.ShapeDtypeStruct((R, D_MODEL), F32),
        semantics=("parallel",),
    )(h0, o_attn, o_conv, ga, gc, w_out)


def _target_copy(tgt_hbm, tgt_s, sem, i, first):
    if first:
        return pltpu.make_async_copy(tgt_hbm.at[pl.ds(0, TM - BLOCK)], tgt_s.at[pl.ds(BLOCK, TM - BLOCK)], sem)
    return pltpu.make_async_copy(tgt_hbm.at[pl.ds(i * TM - BLOCK, TM)], tgt_s, sem)


def _resident(shape):
    nd = len(shape)
    return pl.BlockSpec(shape, lambda *_: (0,) * nd, pipeline_mode=pl.Buffered(1))


def _ffn_fwd(h1, g2, wg_t, wu_t, wd, gf, target):
    R = h1.shape[0]
    nt = R // TM

    def body(h1_ref, g2_ref, wg_ref, wu_ref, wd_ref, gf_ref, tgt_hbm,
             gate_ref, up_ref, act_s, dh2_ref, loss_ref, dgf_ref, tgt_s, sem):
        i = pl.program_id(0)

        @pl.when(i == 0)
        def _():
            loss_ref[...] = jnp.zeros_like(loss_ref)
            dgf_ref[...] = jnp.zeros_like(dgf_ref)
            tgt_s[0:BLOCK, :] = jnp.zeros((BLOCK, D_MODEL), F32)
            _target_copy(tgt_hbm, tgt_s, sem, i, True).start()

        pl.when(i > 0)(lambda: _target_copy(tgt_hbm, tgt_s, sem, i, False).start())
        h1 = h1_ref[...]
        hn = (h1 * _rms_stats(h1) * g2_ref[...]).astype(BF16)
        for cs in FF_SUB:
            gate = _dot(hn, wg_ref[cs, :], NT)
            up = _dot(hn, wu_ref[cs, :], NT)
            gate_ref[:, cs] = gate.astype(BF16)
            up_ref[:, cs] = up.astype(BF16)
            act_s[:, cs] = (gate * _sigmoid(gate) * up).astype(BF16)
        part = _dot(act_s[...], wd_ref[...], NN)
        pl.when(i == 0)(lambda: _target_copy(tgt_hbm, tgt_s, sem, i, True).wait())
        pl.when(i > 0)(lambda: _target_copy(tgt_hbm, tgt_s, sem, i, False).wait())
        h2 = h1 + part
        rf = _rms_stats(h2)
        gf = gf_ref[...]
        row = lax.broadcasted_iota(jnp.int32, (TM, 1), 0) + i * TM
        err = jnp.where(row >= BLOCK, h2 * rf * gf - tgt_s[...], 0.0)
        dy = err * (1.0 / D_MODEL)
        dh2, dgf = _rms_bwd(dy, h2, rf, gf)
        dh2_ref[...] = dh2
        loss_ref[...] += (0.5 / D_MODEL) * jnp.sum(err * err)
        dgf_ref[...] += dgf

    wspec = _resident((D_FF, D_MODEL))
    return _pcall(
        body, name="ffn_fwd", grid=(nt,),
        in_specs=[_rows(TM, D_MODEL), _full((1, D_MODEL)), wspec, wspec, wspec, _full((1, D_MODEL)),
                  pl.BlockSpec(memory_space=pl.ANY)],
        out_specs=[_rows(TM, D_FF), _rows(TM, D_FF), _rows(TM, D_FF), _rows(TM, D_MODEL), _full((8, 128)),
                   _full((1, D_MODEL))],
        out_shape=[jax.ShapeDtypeStruct((R, D_FF), BF16)] * 3
        + [jax.ShapeDtypeStruct((R, D_MODEL), F32),
           jax.ShapeDtypeStruct((8, 128), F32), jax.ShapeDtypeStruct((1, D_MODEL), F32)],
        scratch_shapes=[pltpu.VMEM((TM, D_MODEL), F32), pltpu.SemaphoreType.DMA],
        semantics=("arbitrary",),
    )(h1, g2, wg_t, wu_t, wd, gf, target)


def _ffn_bwd(dh2, h1, g2, gate, up, wg_t, wu_t, wd):
    R = h1.shape[0]
    nt = R // TM
    act_shape = jax.ShapeDtypeStruct((R, D_FF), BF16)

    def act_body(dh2_ref, gate_ref, up_ref, wd_ref, dgate_ref, dup_ref):
        dhb = dh2_ref[...].astype(BF16)
        for cs in FF_SUB:
            dact = _dot(dhb, wd_ref[cs, :], NT)
            gate = gate_ref[:, cs].astype(F32)
            up = up_ref[:, cs].astype(F32)
            sig = _sigmoid(gate)
            dgate_ref[:, cs] = (dact * up * (sig * (1.0 + gate * (1.0 - sig)))).astype(BF16)
            dup_ref[:, cs] = (dact * (gate * sig)).astype(BF16)

    dgate, dup = _pcall(
        act_body, name="ffn_bwd_act", grid=(nt,),
        in_specs=[_rows(TM, D_MODEL), _rows(TM, D_FF), _rows(TM, D_FF), _resident((D_FF, D_MODEL))],
        out_specs=[_rows(TM, D_FF), _rows(TM, D_FF)], out_shape=[act_shape, act_shape],
        semantics=("parallel",),
    )(dh2, gate, up, wd)

    def in_body(dh2_ref, h1_ref, g2_ref, dgate_ref, dup_ref, wg_ref, wu_ref, hn_ref, dh1_ref, dg2_ref):
        @pl.when(pl.program_id(0) == 0)
        def _():
            dg2_ref[...] = jnp.zeros_like(dg2_ref)

        dhn = _dot(dgate_ref[...], wg_ref[...], NN) + _dot(dup_ref[...], wu_ref[...], NN)
        h1 = h1_ref[...]
        r = _rms_stats(h1)
        g2 = g2_ref[...]
        hn_ref[...] = (h1 * r * g2).astype(BF16)
        dx, dg = _rms_bwd(dhn, h1, r, g2)
        dh1_ref[...] = dh2_ref[...] + dx
        dg2_ref[...] += dg

    hn2, dh1, dg2 = _pcall(
        in_body, name="ffn_bwd_in", grid=(nt,),
        in_specs=[_rows(TM, D_MODEL), _rows(TM, D_MODEL), _full((1, D_MODEL)), _rows(TM, D_FF), _rows(TM, D_FF),
                  _resident((D_FF, D_MODEL)), _resident((D_FF, D_MODEL))],
        out_specs=[_rows(TM, D_MODEL), _rows(TM, D_MODEL), _full((1, D_MODEL))],
        out_shape=[jax.ShapeDtypeStruct((R, D_MODEL), BF16), jax.ShapeDtypeStruct((R, D_MODEL), F32),
                   jax.ShapeDtypeStruct((1, D_MODEL), F32)],
        semantics=("arbitrary",),
    )(dh2, h1, g2, dgate, dup, wg_t, wu_t)
    return dgate, dup, hn2, dh1, dg2


def _wgrad(a, b, tm, name, partials=()):
    K, M = a.shape
    N = b.shape[1]
    tk = K // WGRAD_K_TILES if K % (WGRAD_K_TILES * BLOCK) == 0 else TM
    nm, nk, npart = M // tm, K // tk, len(partials)

    def body(a_ref, b_ref, *refs):
        p_ins, o_ref, p_outs, sems = refs[:npart], refs[npart], refs[npart + 1:2 * npart + 1], refs[2 * npart + 1:]
        step = pl.program_id(0) * nk + pl.program_id(1)
        exchange = functools.partial(_chip_copies, p_ins, p_outs, *sems)
        if npart:
            _hosted(step, exchange)

        @pl.when(pl.program_id(1) == 0)
        def _():
            o_ref[...] = jnp.zeros_like(o_ref)

        o_ref[...] += _dot(a_ref[...], b_ref[...].astype(BF16), TN)
        if npart:
            _hosted_wait(step, nm * nk, exchange)

    res = _pcall(
        body, name=name, grid=(nm, nk),
        in_specs=[pl.BlockSpec((tk, tm), lambda m, k: (k, m)), pl.BlockSpec((tk, N), lambda m, k: (k, 0))]
        + [ANY] * npart,
        out_specs=[pl.BlockSpec((tm, N), lambda m, k: (m, 0))] + [ANY] * npart,
        out_shape=[jax.ShapeDtypeStruct((M, N), F32)] + _chip_shapes(partials),
        scratch_shapes=_sem_pair(3 * npart) if npart else [],
        semantics=("arbitrary", "arbitrary"),
    )(a, b, *partials)
    return (res[0], res[1:]) if npart else res[0]


def _outproj_bwd(dh1, o_attn, o_conv, ga, gc, w_out, grads):
    R = dh1.shape[0]
    nt, ng = R // TM, len(grads)

    def body(dh1_ref, oa_ref, oc_ref, ga_ref, gc_ref, w_ref, *refs):
        g_ins, (doa_ref, doc_ref, mixed_ref, dga_ref, dgc_ref) = refs[:ng], refs[ng:ng + 5]
        g_outs, (send_sems, recv_sems) = refs[ng + 5:2 * ng + 5], refs[2 * ng + 5:]
        exchange = functools.partial(_sibling_copies, g_ins, g_outs, send_sems, recv_sems)
        _hosted(pl.program_id(0), exchange)

        @pl.when(pl.program_id(0) == 0)
        def _():
            dga_ref[...] = jnp.zeros_like(dga_ref)
            dgc_ref[...] = jnp.zeros_like(dgc_ref)

        dm = _dot(dh1_ref[...].astype(BF16), w_ref[...], NT)
        oa, oc = oa_ref[...], oc_ref[...]
        ra, rc = _rms_stats(oa), _rms_stats(oc)
        mixed_ref[:, 0:512] = (oa * ra * ga_ref[...]).astype(BF16)
        mixed_ref[:, 512:1024] = (oc * rc * gc_ref[...]).astype(BF16)
        doa, dga = _rms_bwd(dm[:, 0:512], oa, ra, ga_ref[...])
        doc, dgc = _rms_bwd(dm[:, 512:1024], oc, rc, gc_ref[...])
        doa_ref[...] = doa
        doc_ref[...] = doc
        dga_ref[...] += dga
        dgc_ref[...] += dgc
        _hosted_wait(pl.program_id(0), nt, exchange)

    res = _pcall(
        body, name="outproj_bwd", grid=(nt,),
        in_specs=[_rows(TM, D_MODEL), _rows(TM, 512), _rows(TM, 512), _full((1, 512)), _full((1, 512)),
                  _full((D_MODEL, D_MODEL))] + [ANY] * ng,
        out_specs=[_rows(TM, 512), _rows(TM, 512), _rows(TM, D_MODEL), _full((1, 512)), _full((1, 512))]
        + [ANY] * ng,
        out_shape=[jax.ShapeDtypeStruct((R, 512), F32), jax.ShapeDtypeStruct((R, 512), F32),
                   jax.ShapeDtypeStruct((R, D_MODEL), BF16),
                   jax.ShapeDtypeStruct((1, 512), F32), jax.ShapeDtypeStruct((1, 512), F32)]
        + _sibling_shapes(grads),
        scratch_shapes=_sem_pair(ng),
        semantics=("arbitrary",),
    )(dh1, o_attn, o_conv, ga, gc, w_out, *grads)
    return res[:5], res[5:]


def _conv_bwd(do_conv, y, ca, cg, conv_w, ln_g, ln_b, partials):
    R = ca.shape[0]
    nt = R // TM
    hpt = TM // HALO
    npart = len(partials)

    def body(do_ref, doh_ref, y_ref, yh_ref, ca_ref, cg_ref, cah_ref, cgh_ref, w_ref, lg_ref, lb_ref, *refs):
        p_ins, (dca_ref, dcg_ref, dw_ref, db_ref, dlg_ref, dlb_ref) = refs[:npart], refs[npart:npart + 6]
        p_outs, (send_sems, recv_sems, u_s, dy_s, uph_s, dyph_s) = refs[npart + 6:2 * npart + 6], refs[2 * npart + 6:]
        i = pl.program_id(0)
        exchange = functools.partial(_chip_copies, p_ins, p_outs, send_sems, recv_sems)
        _hosted(i, exchange)

        @pl.when(i == 0)
        def _():
            dw_ref[...] = jnp.zeros_like(dw_ref)
            db_ref[...] = jnp.zeros_like(db_ref)
            dlg_ref[...] = jnp.zeros_like(dlg_ref)
            dlb_ref[...] = jnp.zeros_like(dlb_ref)

        lg, lb = lg_ref[...], lb_ref[...]

        def ln_bwd(yv, dov):
            yn, xhat, rstd = _ln_silu(yv, lg, lb)
            sig = _sigmoid(yn)
            dyn = dov * (sig * (1.0 + yn * (1.0 - sig)))
            dxh = dyn * lg
            dyv = rstd * (dxh - jnp.mean(dxh, axis=-1, keepdims=True)
                          - xhat * jnp.mean(dxh * xhat, axis=-1, keepdims=True))
            return dyv, dyn, xhat

        dyv, dyn, xhat = ln_bwd(y_ref[...], do_ref[...])
        dy_s[0:TM, :] = dyv
        dlg_ref[...] += jnp.sum(dyn * xhat, axis=0, keepdims=True)
        dlb_ref[...] += jnp.sum(dyn, axis=0, keepdims=True)
        db_ref[...] += jnp.sum(dyv, axis=0, keepdims=True)
        dyh, _, _ = ln_bwd(yh_ref[...], doh_ref[...])
        dy_s[TM:TM + HALO, :] = jnp.where(i < nt - 1, dyh, 0.0)
        u_s[HALO:HALO + TM, :] = ca_ref[...] * _sigmoid(cg_ref[...])
        u_s[0:HALO, :] = jnp.where(i > 0, cah_ref[...] * _sigmoid(cgh_ref[...]), 0.0)
        _phase_copies(dy_s, dyph_s)
        _phase_copies(u_s, uph_s)

        for rc in range(TM // CONV_ROWS):
            acc = jnp.zeros((CONV_ROWS, CONV_W), F32)
            for k in range(CONV_K):
                acc = acc + _shifted(dy_s, dyph_s, rc * CONV_ROWS + CONV_K - 1 - k, CONV_ROWS) * w_ref[k:k + 1, :]
            rows = slice(rc * CONV_ROWS, (rc + 1) * CONV_ROWS)
            sg = _sigmoid(cg_ref[rows, :])
            dca_ref[rows, :] = (acc * sg).astype(BF16)
            dcg_ref[rows, :] = (acc * ca_ref[rows, :] * sg * (1.0 - sg)).astype(BF16)

        for k in range(CONV_K):
            prod = _shifted(u_s, uph_s, HALO - (CONV_K - 1) + k, TM) * dy_s[0:TM, :]
            dw_ref[k:k + 1, :] += jnp.sum(prod, axis=0, keepdims=True)
        _hosted_wait(i, nt, exchange)

    prev_halo = pl.BlockSpec((HALO, CONV_W), lambda i: (jnp.maximum(i * hpt - 1, 0), 0))
    next_halo = pl.BlockSpec((HALO, CONV_W), lambda i: (jnp.minimum((i + 1) * hpt, nt * hpt - 1), 0))
    vec = jax.ShapeDtypeStruct((1, CONV_W), F32)
    res = _pcall(
        body, name="conv_bwd", grid=(nt,),
        in_specs=[_rows(TM, CONV_W), next_halo, _rows(TM, CONV_W), next_halo,
                  _rows(TM, CONV_W), _rows(TM, CONV_W), prev_halo, prev_halo,
                  _full((CONV_K, CONV_W)), _full((1, CONV_W)), _full((1, CONV_W))] + [ANY] * npart,
        out_specs=[_rows(TM, CONV_W), _rows(TM, CONV_W), _full((32, CONV_W)),
                   _full((1, CONV_W)), _full((1, CONV_W)), _full((1, CONV_W))] + [ANY] * npart,
        out_shape=[jax.ShapeDtypeStruct((R, CONV_W), BF16), jax.ShapeDtypeStruct((R, CONV_W), BF16),
                   jax.ShapeDtypeStruct((32, CONV_W), F32), vec, vec, vec] + _chip_shapes(partials),
        scratch_shapes=_sem_pair(3 * npart)
        + [pltpu.VMEM((HALO + TM, CONV_W), F32), pltpu.VMEM((TM + HALO, CONV_W), F32),
           _phase_scratch(), _phase_scratch()],
        semantics=("arbitrary",),
    )(do_conv, do_conv, y, y, ca, cg, ca, cg, conv_w, ln_g, ln_b, *partials)
    return res[:6], res[6:]


def _attn_bwd(q, kv, sinks, o, lse, do, grads, partials):
    R = q.shape[0]
    nb = R // BLOCK
    ng, npart = len(grads), len(partials)
    nx = ng + npart

    def body(sink_ref, q_ref, kvc_ref, kvp_ref, kvm_ref, o_ref, lse_ref, do_ref, *refs):
        x_ins, (dq_ref, dkv_ref, dkvm_ref, dsink_ref) = refs[:nx], refs[nx:nx + 4]
        x_outs = refs[nx + 4:2 * nx + 4]
        g_send, g_recv, p_send, p_recv, carry_s, cur_s, prev_s, bias_s, sink_s, delta_s = refs[2 * nx + 4:]
        b = pl.program_id(0)

        def exchange():
            return (_sibling_copies(x_ins[:ng], x_outs[:ng], g_send, g_recv)
                    + _chip_copies(x_ins[ng:], x_outs[ng:], p_send, p_recv))

        _hosted(b, exchange)

        @pl.when(b == 0)
        def _():
            dkvm_ref[...] = jnp.zeros_like(dkvm_ref)
            carry_s[...] = jnp.zeros_like(carry_s)
            for h in range(N_HEADS):
                dsink_ref[0, h] = 0.0
            _attn_tables(sink_ref, bias_s, sink_s)

        @pl.when(b < nb)
        def _():
            sel, pen, pen_m = _attn_masks(b)
            q_t = (q_ref[...] * ATTN_SCALE).T
            do_t = do_ref[...].astype(BF16).T
            kvc_t, kvp_t = kvc_ref[...].T, kvp_ref[...].T
            prod = do_ref[...] * o_ref[...]
            hi = prod.astype(BF16)
            lo = (prod - hi.astype(F32)).astype(BF16)
            head_of = lax.broadcasted_iota(jnp.int32, (N_HEADS, ATTN_W), 1) // HEAD_DIM
            ind = (head_of == lax.broadcasted_iota(jnp.int32, (N_HEADS, ATTN_W), 0)).astype(BF16)
            delta_s[...] = _dot(ind, hi, NT) + _dot(ind, lo, NT)
            dqs = []
            for g in range(N_HEADS // GROUP):
                ks, vs = slice(HEAD_DIM * g, HEAD_DIM * (g + 1)), slice(KV_W + HEAD_DIM * g, KV_W + HEAD_DIM * (g + 1))
                lanes = slice(GB * g, GB * (g + 1))
                qg, dog = _group_lanes(q_t, g), _group_lanes(do_t, g)
                kc, kp, km = kvc_ref[:, ks], kvp_ref[:, ks], kvm_ref[LEAD:BLOCK, ks]
                vc, vp, vm = kvc_ref[:, vs], kvp_ref[:, vs], kvm_ref[LEAD:BLOCK, vs]
                s_b, s_m = _attn_scores(qg, kc, kp, km, sel, pen, pen_m, bias_s[:, lanes])
                lse, delta = _head_lanes(lse_ref, g), _head_lanes(delta_s, g)
                p_b = jnp.exp(s_b - lse)
                p_m = jnp.exp(s_m - lse)
                dp_b = jnp.where(sel, _dot(vc, dog, NN), _dot(vp, dog, NN))
                ds_b = p_b * (dp_b - delta)
                ds_m = (p_m * (_dot(vm, dog, NN) - delta)).astype(BF16)
                dsk = jnp.exp(sink_s[:, lanes] - lse) * delta
                for j in range(GROUP):
                    dsink_ref[0, GROUP * g + j] += -jnp.sum(dsk[:, BLOCK * j:BLOCK * (j + 1)])
                ds_c = jnp.where(sel, ds_b, 0.0).astype(BF16)
                ds_p = jnp.where(sel, 0.0, ds_b).astype(BF16)
                p_c = jnp.where(sel, p_b, 0.0).astype(BF16)
                p_p = jnp.where(sel, 0.0, p_b).astype(BF16)
                dqs.append((_dot(kvc_t[ks, :], ds_c, NN) + _dot(kvp_t[ks, :], ds_p, NN)
                            + _dot(km, ds_m, TN)) * ATTN_SCALE)
                cur_s[:, ks] = _dot(ds_c, qg, NT)
                cur_s[:, vs] = _dot(p_c, dog, NT)
                prev_s[:, ks] = _dot(ds_p, qg, NT)
                prev_s[:, vs] = _dot(p_p, dog, NT)
                dkvm_ref[:, ks] += _dot(ds_m, qg, NT)
                dkvm_ref[:, vs] += _dot(p_m.astype(BF16), dog, NT)
            dq_ref[...] = _head_rows(dqs).astype(BF16).T
            dkv_ref[...] = (carry_s[...] + prev_s[...]).astype(BF16)
            carry_s[...] = cur_s[...]

        @pl.when(b == nb)
        def _():
            dkv_ref[...] = carry_s[...].astype(BF16)

        _hosted_wait(b, nb + 1, exchange)

    def at(off):
        return lambda b: (jnp.clip(b + off, 0, nb - 1), 0)

    blk = lambda cols, off=0: pl.BlockSpec((BLOCK, cols), at(off))
    res = _pcall(
        body, name="attn_bwd", grid=(nb + 1,),
        in_specs=[pl.BlockSpec(memory_space=pltpu.SMEM), blk(512), blk(256), blk(256, -1), _full((BLOCK, 256)),
                  blk(512), pl.BlockSpec((N_HEADS, BLOCK), lambda b: (0, jnp.minimum(b, nb - 1))), blk(512)]
        + [ANY] * nx,
        out_specs=[blk(512), blk(256, -1), _full((N_META, 256)), pl.BlockSpec(memory_space=pltpu.SMEM)]
        + [ANY] * nx,
        out_shape=[jax.ShapeDtypeStruct((R, 512), BF16), jax.ShapeDtypeStruct((R, 256), BF16),
                   jax.ShapeDtypeStruct((N_META, 256), F32), jax.ShapeDtypeStruct((1, N_HEADS), F32)]
        + _sibling_shapes(grads) + _chip_shapes(partials),
        scratch_shapes=_sem_pair(ng) + _sem_pair(3 * npart) + [pltpu.VMEM((BLOCK, 256), F32)] * 3
        + [pltpu.VMEM((BLOCK, N_HEADS * BLOCK), F32), pltpu.VMEM((1, N_HEADS * BLOCK), F32),
           pltpu.VMEM((N_HEADS, BLOCK), F32)],
        semantics=("arbitrary",),
    )(sinks, q, kv, kv, kv, o, lse, do, *grads, *partials)
    return res[:4], res[4:4 + ng], res[4 + ng:]


def _inproj_bwd(dh1, h0, g1, dq, dkv, dkvm, dca, dcg, w_in_t):
    R = h0.shape[0]
    nt = R // TM
    assert nt >= 2

    def body(dh1_ref, h0_ref, g_ref, dq_ref, dkv_ref, dkvm_ref, dca_ref, dcg_ref, w_ref,
             gx_hbm, dmeta_ref, dproj_ref, hn_ref, dg_ref, dx_s, gx_sems):
        i = pl.program_id(0)

        def gx_copy(step, slot, first):
            if first:
                return pltpu.make_async_copy(dx_s.at[slot, pl.ds(BLOCK, TM - BLOCK)],
                                             gx_hbm.at[pl.ds(0, TM - BLOCK)], gx_sems.at[slot])
            return pltpu.make_async_copy(
                dx_s.at[slot], gx_hbm.at[pl.ds(pl.multiple_of(step * TM - BLOCK, BLOCK), TM)], gx_sems.at[slot])

        @pl.when(i == 0)
        def _():
            dg_ref[...] = jnp.zeros_like(dg_ref)

        dproj_ref[:, 0:512] = dq_ref[...]
        dproj_ref[:, 512:768] = dkv_ref[...]
        dproj_ref[:, 768:1280] = dca_ref[...]
        dproj_ref[:, 1280:1792] = dcg_ref[...]

        @pl.when(i == 0)
        def _():
            dproj_ref[LEAD:BLOCK, 512:768] = dkvm_ref[...].astype(BF16)

        dhn = _dot(dproj_ref[...], w_ref[...], NN)
        h = h0_ref[...]
        r = _rms_stats(h)
        g = g_ref[...]
        hn_ref[...] = (h * r * g).astype(BF16)
        dx, dg = _rms_bwd(dhn, h, r, g)
        dg_ref[...] += dg
        slot = i % 2
        pl.when(i == 2)(lambda: gx_copy(0, 0, True).wait())
        pl.when(i > 2)(lambda: gx_copy(i - 2, slot, False).wait())
        dx_s[slot] = dh1_ref[...] + dx

        @pl.when(i == 0)
        def _():
            dmeta_ref[...] = dx_s[0, LEAD:BLOCK, :]
            gx_copy(0, 0, True).start()

        pl.when(i > 0)(lambda: gx_copy(i, slot, False).start())

        @pl.when(i == nt - 1)
        def _():
            gx_copy(nt - 2, (nt - 2) % 2, nt == 2).wait()
            gx_copy(nt - 1, (nt - 1) % 2, False).wait()

    return _pcall(
        body, name="inproj_bwd", grid=(nt,),
        in_specs=[_rows(TM, D_MODEL), _rows(TM, D_MODEL), _full((1, D_MODEL)), _rows(TM, 512), _rows(TM, 256),
                  _full((N_META, 256)), _rows(TM, 512), _rows(TM, 512), _full((1792, D_MODEL))],
        out_specs=[ANY, _full((N_META, D_MODEL)), _rows(TM, 1792), _rows(TM, D_MODEL), _full((1, D_MODEL))],
        out_shape=[jax.ShapeDtypeStruct((R - BLOCK, D_MODEL), F32), jax.ShapeDtypeStruct((N_META, D_MODEL), F32),
                   jax.ShapeDtypeStruct((R, 1792), BF16),
                   jax.ShapeDtypeStruct((R, D_MODEL), BF16), jax.ShapeDtypeStruct((1, D_MODEL), F32)],
        scratch_shapes=[pltpu.VMEM((2, TM, D_MODEL), F32), pltpu.SemaphoreType.DMA((2,))],
        semantics=("arbitrary",),
    )(dh1, h0, g1, dq, dkv, dkvm, dca, dcg, w_in_t)


ANY = pl.BlockSpec(memory_space=pl.ANY)


def _position():
    return lax.axis_index("x"), lax.axis_index("y"), lax.axis_index("c")


def _device_number(p):
    return 4 * p[0] + 2 * p[1] + p[2]


def _two_level_allgather(ins, outs, block, send_sems, recv_sems, local_sems, sem_base=0):
    n = len(ins)
    x, y, c = _position()
    me, sibling = (x, y, c), (x, y, 1 - c)
    chips = [(1 - x, y), (x, 1 - y), (1 - x, 1 - y)]

    def copy(w, k, origin, to, src=None):
        return pltpu.make_async_remote_copy(
            src_ref=block(w, origin) if src is None else src, dst_ref=block(w, origin),
            send_sem=send_sems.at[sem_base + 7 * w + k], recv_sem=recv_sems.at[sem_base + 7 * w + k],
            device_id=to, device_id_type=MESH)

    def mine(w):
        return pltpu.make_async_copy(ins[w], block(w, me), local_sems.at[w])

    def own(w):
        return [copy(w, 0, me, sibling, src=ins[w])] + [
            copy(w, 1 + j, me, (*chip, c), src=ins[w]) for j, chip in enumerate(chips)]

    def passed(w):
        return [copy(w, 4 + j, (*chip, c), sibling) for j, chip in enumerate(chips)]

    def start():
        for w in range(n):
            mine(w).start()
        for w in range(n):
            for cp in own(w):
                cp.start()

    def forward(w):
        fw = passed(w)
        for j, chip in enumerate(chips):
            copy(w, 1 + j, (*chip, c), me).wait_recv()
            fw[j].start()

    def finish():
        for w in range(n):
            copy(w, 0, sibling, me).wait_recv()
            for j, chip in enumerate(chips):
                copy(w, 4 + j, (*chip, 1 - c), me).wait_recv()
        for w in range(n):
            for cp in own(w) + passed(w):
                cp.wait_send()
            mine(w).wait()

    return start, forward, finish


def _carried_allgather(step, n_steps, shards, refs):
    ns = len(shards)
    ins, outs, (send_sems, recv_sems, local_sems) = refs[:ns], refs[ns:2 * ns], refs[2 * ns:]
    start, forward, finish = _two_level_allgather(
        ins, outs, _row_block(outs, [s.shape[0] for s in shards]), send_sems, recv_sems, local_sems)
    pl.when(step == 0)(start)
    total = sum(s.shape[0] for s in shards)
    sent = 0
    for w, s in enumerate(shards):
        sent += s.shape[0]
        pl.when(step == (AG_FORWARD_AT * sent * (n_steps - 1)) // (100 * total))(functools.partial(forward, w))
    return lambda: pl.when(step == n_steps - 1)(finish)


def _gathered_shapes(shards):
    return [jax.ShapeDtypeStruct((N_DEV * s.shape[0], s.shape[1]), s.dtype) for s in shards]


def _allgather_sems(ns):
    return _sem_pair(7 * ns) + [pltpu.SemaphoreType.DMA((ns,))]


def _blocking_allgather(ins, outs, block, send_sems, recv_sems, local_sems, sem_base=0):
    start, forward, finish = _two_level_allgather(ins, outs, block, send_sems, recv_sems, local_sems, sem_base)
    start()
    for w in range(len(ins)):
        forward(w)
    finish()


def _row_block(outs, rows):
    def block(w, p):
        return outs[w].at[pl.ds(pl.multiple_of(_device_number(p) * rows[w], 16), rows[w])]
    return block


def _sibling_copies(ins, outs, send_sems, recv_sems):
    x, y, c = _position()
    return [pltpu.make_async_remote_copy(
        src_ref=ins[w].at[:, 1 - c], dst_ref=outs[w], send_sem=send_sems.at[w], recv_sem=recv_sems.at[w],
        device_id=(x, y, 1 - c), device_id_type=MESH) for w in range(len(ins))]


def _chip_copies(ins, outs, send_sems, recv_sems):
    x, y, c = _position()
    chips = [(1 - x, y), (x, 1 - y), (1 - x, 1 - y)]
    return [pltpu.make_async_remote_copy(
        src_ref=ins[w].at[2 * chip[0] + chip[1]], dst_ref=outs[w].at[k],
        send_sem=send_sems.at[3 * w + k], recv_sem=recv_sems.at[3 * w + k],
        device_id=(*chip, c), device_id_type=MESH) for w in range(len(ins)) for k, chip in enumerate(chips)]


def _hosted(step, make_copies):
    @pl.when(step == 0)
    def _():
        for cp in make_copies():
            cp.start()


def _hosted_wait(step, n_steps, make_copies):
    @pl.when(step == n_steps - 1)
    def _():
        for cp in make_copies():
            cp.wait()


def _sem_pair(n):
    return [pltpu.SemaphoreType.DMA((n,)), pltpu.SemaphoreType.DMA((n,))]


def _allgather_params(shards, small):
    arrays = list(shards) + list(small)
    n, ns = len(arrays), len(shards)

    def body(*refs):
        ins, outs = refs[:n], refs[n:2 * n]
        send_sems, recv_sems, local_sems = refs[2 * n:]

        rows = _row_block(outs, [a.shape[0] for a in arrays])

        def block(w, p):
            return rows(w, p) if w < ns else outs[w].at[_device_number(p)]

        _blocking_allgather(ins, outs, block, send_sems, recv_sems, local_sems)

    out_shape = [jax.ShapeDtypeStruct((N_DEV * a.shape[0], a.shape[1]), a.dtype) for a in shards]
    out_shape += [jax.ShapeDtypeStruct((N_DEV,) + a.shape, a.dtype) for a in small]
    return _pcall(
        body, name="allgather_params", in_specs=[ANY] * n, out_specs=[ANY] * n, out_shape=out_shape,
        scratch_shapes=[pltpu.SemaphoreType.DMA((7 * n,)), pltpu.SemaphoreType.DMA((7 * n,)),
                        pltpu.SemaphoreType.DMA((n,))],
    )(*arrays)


def _reduce_siblings(grads):
    n = len(grads)

    def body(*refs):
        ins, outs = refs[:n], refs[n:2 * n]
        send_sems, recv_sems = refs[2 * n:]
        copies = _sibling_copies(ins, outs, send_sems, recv_sems)
        for cp in copies:
            cp.start()
        for cp in copies:
            cp.wait()

    return _pcall(
        body, name="reduce_siblings", in_specs=[ANY] * n, out_specs=[ANY] * n,
        out_shape=_sibling_shapes(grads), scratch_shapes=_sem_pair(n),
    )(*grads)


def _sibling_shapes(grads):
    return [jax.ShapeDtypeStruct((4,) + g.shape[2:], F32) for g in grads]


def _chip_shapes(partials):
    return [jax.ShapeDtypeStruct((3,) + p.shape[1:], F32) for p in partials]


def _add_sibling(grad, received, core, name):
    _, _, r, cols = grad.shape

    def body(core_ref, g_ref, r_ref, o_ref):
        o_ref[...] = g_ref[...] + r_ref[...]

    return pl.pallas_call(
        body, name=name,
        grid_spec=pltpu.PrefetchScalarGridSpec(
            num_scalar_prefetch=1, grid=(4,),
            in_specs=[pl.BlockSpec((None, None, r, cols), lambda s, core_ref: (s, core_ref[0], 0, 0)),
                      pl.BlockSpec((None, r, cols), lambda s, core_ref: (s, 0, 0))],
            out_specs=pl.BlockSpec((None, r, cols), lambda s, core_ref: (s, 0, 0))),
        out_shape=jax.ShapeDtypeStruct((4, r, cols), F32),
        compiler_params=pltpu.CompilerParams(vmem_limit_bytes=VMEM_LIMIT),
    )(core, grad, received)


def _reduce_chips(partials, small):
    n, ns = len(partials), len(small)

    def body(*refs):
        p_ins, s_ins = refs[:n], refs[n:n + ns]
        p_outs, s_outs = refs[n + ns:2 * n + ns], refs[2 * n + ns:2 * (n + ns)]
        send_sems, recv_sems, local_sems = refs[2 * (n + ns):]
        copies = _chip_copies(p_ins, p_outs, send_sems, recv_sems)
        for cp in copies:
            cp.start()
        _blocking_allgather(s_ins, s_outs, lambda w, p: s_outs[w].at[_device_number(p)],
                            send_sems, recv_sems, local_sems, sem_base=3 * n)
        for cp in copies:
            cp.wait()

    out_shape = _chip_shapes(partials)
    out_shape += [jax.ShapeDtypeStruct((N_DEV,) + a.shape, a.dtype) for a in small]
    nsem = 3 * n + 7 * ns
    return _pcall(
        body, name="reduce_chips", in_specs=[ANY] * (n + ns), out_specs=[ANY] * (n + ns), out_shape=out_shape,
        scratch_shapes=[pltpu.SemaphoreType.DMA((nsem,)), pltpu.SemaphoreType.DMA((nsem,)),
                        pltpu.SemaphoreType.DMA((ns,))],
    )(*partials, *small)


def _adam(g, w, m, v):
    m = ADAM_B1 * m + (1.0 - ADAM_B1) * g
    v = ADAM_B2 * v + (1.0 - ADAM_B2) * (g * g)
    m_hat = m / (1.0 - ADAM_B1 ** ADAM_STEP)
    v_hat = v / (1.0 - ADAM_B2 ** ADAM_STEP)
    delta = -ADAM_LR * (m_hat / (jnp.sqrt(v_hat) + ADAM_EPS) + ADAM_WD * w)
    return delta, m, v


def _adamw(partial, received, slot, w, m, v, name):
    _, r, cols = partial.shape

    def body(slot_ref, p_ref, r_ref, w_ref, m_ref, v_ref, g_ref, d_ref, nm_ref, nv_ref):
        g = p_ref[...] + r_ref[0] + r_ref[1] + r_ref[2]
        g_ref[...] = g
        d_ref[...], nm_ref[...], nv_ref[...] = _adam(g, w_ref[...], m_ref[...], v_ref[...])

    whole = pl.BlockSpec((r, cols), lambda i, slot_ref: (0, 0))
    out = jax.ShapeDtypeStruct((r, cols), F32)
    return pl.pallas_call(
        body, name=name,
        grid_spec=pltpu.PrefetchScalarGridSpec(
            num_scalar_prefetch=1, grid=(1,),
            in_specs=[pl.BlockSpec((None, r, cols), lambda i, slot_ref: (slot_ref[0], 0, 0)),
                      pl.BlockSpec((3, r, cols), lambda i, slot_ref: (0, 0, 0)), whole, whole, whole],
            out_specs=[whole, whole, whole, whole]),
        out_shape=[out, out, out, out],
        compiler_params=pltpu.CompilerParams(vmem_limit_bytes=VMEM_LIMIT),
    )(slot, partial, received, w, m, v)


def _adamw_small(dev, ga, gb, gc, params):
    names = ["meta", "attn_norm", "sinks", "conv_w", "conv_b", "ln_g", "ln_b", "attn_out", "conv_out",
             "ffn_norm", "final_norm"]
    flat = [a for p in params for a in p]
    n_in = len(flat)

    def body(dev_ref, ga_ref, gb_ref, gc_ref, *refs):
        ins, outs = refs[:n_in], refs[n_in:n_in + 4 * len(names)]
        loss_ref, sb, sc = refs[n_in + 4 * len(names):]
        a = ga_ref[0]
        sb[...] = gb_ref[0]
        sc[...] = gc_ref[0]
        for d in range(1, N_DEV):
            a = a + ga_ref[d]
            sb[...] += gb_ref[d]
            sc[...] += gc_ref[d]
        dev = dev_ref[0]
        grads = {
            "attn_norm": a[0:1, :], "ffn_norm": a[1:2, :], "final_norm": a[2:3, :],
            "conv_b": a[3:4, 0:512], "ln_g": a[3:4, 512:1024], "ln_b": a[4:5, 0:512],
            "attn_out": a[4:5, 512:1024], "conv_out": a[5:6, 0:512], "sinks": a[5:6, 512:512 + N_HEADS],
            "meta": sb[pl.ds(pl.multiple_of(dev * N_META, N_META), N_META), :],
            "conv_w": sc[pl.ds(pl.multiple_of(dev * 32, 32), 32), :][0:CONV_K, :],
        }
        for idx, nm in enumerate(names):
            w_ref, m_ref, v_ref = ins[3 * idx:3 * idx + 3]
            g = grads[nm]
            delta, m, v = _adam(g, w_ref[...], m_ref[...], v_ref[...])
            o = outs[4 * idx:4 * idx + 4]
            o[0][...], o[1][...], o[2][...], o[3][...] = g, delta, m, v
        loss_ref[...] = a[6:7, 0:1]

    vm = pl.BlockSpec(memory_space=pltpu.VMEM)
    out_shape = [jax.ShapeDtypeStruct(p[0].shape, F32) for p in params for _ in range(4)]
    out_shape.append(jax.ShapeDtypeStruct((1, 1), F32))
    res = pl.pallas_call(
        body, name="adamw_small",
        grid_spec=pltpu.PrefetchScalarGridSpec(
            num_scalar_prefetch=1, grid=(1,),
            in_specs=[pl.BlockSpec(ga.shape, lambda i, d: (0, 0, 0)), pl.BlockSpec(gb.shape, lambda i, d: (0, 0, 0)),
                      pl.BlockSpec(gc.shape, lambda i, d: (0, 0, 0))]
            + [pl.BlockSpec(a.shape, lambda i, d: (0, 0)) for a in flat],
            out_specs=[pl.BlockSpec(s.shape, lambda i, d: (0, 0)) for s in out_shape],
            scratch_shapes=[pltpu.VMEM(gb.shape[1:], F32), pltpu.VMEM(gc.shape[1:], F32)]),
        out_shape=out_shape,
        compiler_params=pltpu.CompilerParams(vmem_limit_bytes=VMEM_LIMIT),
    )(dev, ga, gb, gc, *flat)
    return [res[4 * i:4 * i + 4] for i in range(len(names))], res[-1]


def kernel(x, meta_tokens, attn_norm_g, w_in, attn_sinks, conv_w, conv_b, conv_ln_g, conv_ln_b, attn_out_g, conv_out_g, w_out, ffn_norm_g, w_gate, w_up, w_down, final_norm_g, loss_target, m_meta_tokens, m_attn_norm_g, m_w_in, m_attn_sinks, m_conv_w, m_conv_b, m_conv_ln_g, m_conv_ln_b, m_attn_out_g, m_conv_out_g, m_w_out, m_ffn_norm_g, m_w_gate, m_w_up, m_w_down, m_final_norm_g, v_meta_tokens, v_attn_norm_g, v_w_in, v_attn_sinks, v_conv_w, v_conv_b, v_conv_ln_g, v_conv_ln_b, v_attn_out_g, v_conv_out_g, v_w_out, v_ffn_norm_g, v_w_gate, v_w_up, v_w_down, v_final_norm_g):
    xi, yi, ci = _position()
    dev = jnp.reshape(_device_number((xi, yi, ci)), (1,)).astype(jnp.int32)
    core = jnp.reshape(ci, (1,)).astype(jnp.int32)
    slot = jnp.reshape(2 * xi + yi, (1,)).astype(jnp.int32)

    w_in_t, meta_st, convw_st = _allgather_params([w_in[0].T.astype(BF16)], [meta_tokens, conv_w[0]])
    meta_full = jnp.transpose(meta_st, (1, 0, 2)).reshape(N_META, D_MODEL)
    convw_full = jnp.transpose(convw_st, (1, 0, 2)).reshape(CONV_K, CONV_W)

    final_g = final_norm_g.reshape(1, D_MODEL)

    h0, q, kv, ca, cg = _inproj_fwd(x[0], meta_full, attn_norm_g, w_in_t)
    o_attn, lse, (wg_t, wu_t) = _attn_fwd(
        q, kv, attn_sinks, [w_gate[0].T.astype(BF16), w_up[0].T.astype(BF16)])
    (o_conv, y_conv), (w_out_b, wd_b) = _conv_fwd(
        ca, cg, convw_full, conv_b, conv_ln_g, conv_ln_b, [w_out[0].astype(BF16), w_down[0].astype(BF16)])
    h1 = _outproj_fwd(h0, o_attn, o_conv, attn_out_g, conv_out_g, w_out_b)
    gate, up, act, dh2, loss_sum, dg_final = _ffn_fwd(h1, ffn_norm_g, wg_t, wu_t, wd_b, final_g, loss_target[0])

    def blocks(g):
        return g.reshape(4, 2, g.shape[0] // N_DEV, D_MODEL)

    def add_siblings(grads, received, tags):
        return [_add_sibling(g, r, core, "add_sibling_" + t) for g, r, t in zip(grads, received, tags)]

    dgate, dup, hn2, dh1, dg_ffn = _ffn_bwd(dh2, h1, ffn_norm_g, gate, up, wg_t, wu_t, wd_b)
    ffn_grads = [blocks(_wgrad(dgate, hn2, FF_CHUNK, "wgrad_gate")), blocks(_wgrad(dup, hn2, FF_CHUNK, "wgrad_up")),
                 blocks(_wgrad(act, dh2, FF_CHUNK, "wgrad_down"))]
    (do_attn, do_conv, mixed, dg_ao, dg_co), ffn_sib = _outproj_bwd(
        dh1, o_attn, o_conv, attn_out_g, conv_out_g, w_out_b, ffn_grads)
    ffn_sums = add_siblings(ffn_grads, ffn_sib, ("gate", "up", "down"))
    out_grads = [blocks(_wgrad(mixed, dh1, D_MODEL, "wgrad_out"))]
    (dca, dcg, dconvw, dconvb, dln_g, dln_b), gate_up_chips = _conv_bwd(
        do_conv, y_conv, ca, cg, convw_full, conv_ln_g, conv_ln_b, ffn_sums[:2])
    (dq, dkv, dkvm, dsinks), out_sib, down_chips = _attn_bwd(
        q, kv, attn_sinks, o_attn, lse, do_attn, out_grads, ffn_sums[2:])
    ffn_chips = list(gate_up_chips) + list(down_chips)
    out_sums = add_siblings(out_grads, out_sib, ("out",))
    grad_x, dmeta, dproj, hn1, dg_attn = _inproj_bwd(dh1, h0, attn_norm_g, dq, dkv, dkvm, dca, dcg, w_in_t)
    dwi_t, out_chips = _wgrad(dproj, hn1, 1792, "wgrad_in", out_sums)
    in_grads = [blocks(dwi_t)]
    in_sums = add_siblings(in_grads, _reduce_siblings(in_grads), ("in",))
    small_a = jnp.concatenate([
        dg_attn, dg_ffn, dg_final, jnp.concatenate([dconvb, dln_g], axis=1), jnp.concatenate([dln_b, dg_ao], axis=1),
        jnp.concatenate([dg_co, dsinks, jnp.zeros((1, 512 - N_HEADS), F32)], axis=1),
        jnp.concatenate([loss_sum[0:1, :], jnp.zeros((1, D_MODEL - 128), F32)], axis=1),
        jnp.zeros((1, D_MODEL), F32)], axis=0)
    small_b = jnp.transpose(dmeta.reshape(N_META, N_DEV, 128), (1, 0, 2)).reshape(N_DEV * N_META, 128)
    small_c = jnp.transpose(dconvw.reshape(32, N_DEV, 64), (1, 0, 2)).reshape(N_DEV * 32, 64)
    in_chips, ga, gb, gc = _reduce_chips(in_sums, [small_a, small_b, small_c])
    tags = ("in", "out", "gate", "up", "down")
    chip_sums = in_sums + out_sums + ffn_sums
    from_chips = [in_chips] + list(out_chips) + list(ffn_chips)

    big = [(True, w_in, m_w_in, v_w_in), (False, w_out, m_w_out, v_w_out), (True, w_gate, m_w_gate, v_w_gate),
           (True, w_up, m_w_up, v_w_up), (False, w_down, m_w_down, v_w_down)]
    big_out = {}
    for t, p, r, (transposed, w, m, v) in zip(tags, chip_sums, from_chips, big):
        rows = (lambda a: jnp.transpose(a[0])) if transposed else (lambda a: a[0])
        back = (lambda a: jnp.transpose(a)[None]) if transposed else (lambda a: a[None])
        big_out[t] = [back(a) for a in _adamw(p, r, slot, rows(w), rows(m), rows(v), "adamw_" + t)]

    small_params = [
        (meta_tokens, m_meta_tokens, v_meta_tokens), (attn_norm_g, m_attn_norm_g, v_attn_norm_g),
        (attn_sinks, m_attn_sinks, v_attn_sinks), (conv_w[0], m_conv_w[0], v_conv_w[0]),
        (conv_b, m_conv_b, v_conv_b), (conv_ln_g, m_conv_ln_g, v_conv_ln_g), (conv_ln_b, m_conv_ln_b, v_conv_ln_b),
        (attn_out_g, m_attn_out_g, v_attn_out_g), (conv_out_g, m_conv_out_g, v_conv_out_g),
        (ffn_norm_g, m_ffn_norm_g, v_ffn_norm_g),
        (final_g, m_final_norm_g.reshape(1, D_MODEL), v_final_norm_g.reshape(1, D_MODEL))]
    sm, loss = _adamw_small(dev, ga, gb, gc, small_params)
    sm[3] = [a[None] for a in sm[3]]
    sm[10] = [a.reshape(D_MODEL) for a in sm[10]]

    per_param = [sm[0], sm[1], big_out["in"], sm[2], sm[3], sm[4], sm[5], sm[6], sm[7], sm[8], big_out["out"],
                 sm[9], big_out["gate"], big_out["up"], big_out["down"], sm[10]]
    loss = loss.reshape(())
    outs = [loss, grad_x[None]]
    for kind in range(4):
        outs += [p[kind] for p in per_param]
    return tuple(outs)
```

```python
import functools
import math

import jax
import jax.numpy as jnp
from jax import lax
from jax.experimental import pallas as pl
from jax.experimental.pallas import tpu as pltpu

F32, BF16 = jnp.float32, jnp.bfloat16
MESH = pl.DeviceIdType.MESH

D_MODEL = 1024
N_META = 16
BLOCK = 128
LEAD = BLOCK - N_META
HEAD_DIM = 64
N_HEADS = 8
GROUP = 4
ATTN_W = 512
KV_W = 128
CONV_W = 512
CONV_K = 31
HALO = 32
D_FF = 2816
FF_CHUNK = D_FF // 2
FF_SUB = [slice(s, s + 256) for s in range(0, D_FF, 256)]
N_DEV = 8
EPS = 1e-5
NEG = -1e30
TM = 640
TILE_BLOCKS = TM // BLOCK
AG_FORWARD_AT = 85
WGRAD_K_TILES = 5
CONV_ROWS = 32
VMEM_LIMIT = 56 * 1024 * 1024

ADAM_LR, ADAM_B1, ADAM_B2, ADAM_EPS, ADAM_WD, ADAM_STEP = 0.001, 0.9, 0.999, 1e-08, 0.01, 10

NT = (((1,), (1,)), ((), ()))
NN = (((1,), (0,)), ((), ()))
TN = (((0,), (0,)), ((), ()))


def _dot(a, b, dims):
    return lax.dot_general(a, b, dims, preferred_element_type=F32)


def _sigmoid(x):
    return 1.0 / (1.0 + jnp.exp(-x))


def _pcall(body, *, name, out_shape, grid=None, in_specs=None, out_specs=None, scratch_shapes=(),
           semantics=None, **kw):
    params = dict(vmem_limit_bytes=VMEM_LIMIT)
    if semantics is not None:
        params["dimension_semantics"] = semantics
    extra = {}
    if grid is not None:
        extra["grid"] = grid
    if in_specs is not None:
        extra["in_specs"] = in_specs
    if out_specs is not None:
        extra["out_specs"] = out_specs
    return pl.pallas_call(body, name=name, out_shape=out_shape, scratch_shapes=list(scratch_shapes),
                          compiler_params=pltpu.CompilerParams(**params), **extra, **kw)


def _rows(tm, cols):
    return pl.BlockSpec((tm, cols), lambda i, *_: (i, 0))


def _full(shape):
    nd = len(shape)
    return pl.BlockSpec(shape, lambda *_: (0,) * nd)


def _rms_stats(x):
    return lax.rsqrt(jnp.mean(x * x, axis=-1, keepdims=True) + EPS)


def _rms_bwd(dy, x, r, g):
    t = dy * g
    dx = r * (t - x * (r * r) * jnp.mean(t * x, axis=-1, keepdims=True))
    dg = jnp.sum(dy * x * r, axis=0, keepdims=True)
    return dx, dg


def _inproj_fwd(x, meta, g1, w_in_t):
    R = x.shape[0] + BLOCK
    nt = R // TM
    assert nt >= 2

    def body(x_hbm, meta_ref, g_ref, w_ref, h0_ref, q_ref, kv_ref, ca_ref, cg_ref, x_s, sems):
        i = pl.program_id(0)
        slot = i % 2

        def x_copy(step, slot, first):
            if first:
                return pltpu.make_async_copy(x_hbm.at[pl.ds(0, TM - BLOCK)],
                                             x_s.at[slot, pl.ds(BLOCK, TM - BLOCK)], sems.at[slot])
            return pltpu.make_async_copy(
                x_hbm.at[pl.ds(pl.multiple_of(step * TM - BLOCK, BLOCK), TM)], x_s.at[slot], sems.at[slot])

        @pl.when(i == 0)
        def _():
            x_copy(0, 0, True).start()
            x_s[0, 0:LEAD, :] = jnp.zeros((LEAD, D_MODEL), F32)
            x_s[0, LEAD:BLOCK, :] = meta_ref[...]

        pl.when(i + 1 < nt)(lambda: x_copy(i + 1, 1 - slot, False).start())
        pl.when(i == 0)(lambda: x_copy(0, 0, True).wait())
        pl.when(i > 0)(lambda: x_copy(i, slot, False).wait())
        h = x_s[slot]
        h0_ref[...] = h
        hn = (h * _rms_stats(h) * g_ref[...]).astype(BF16)
        q_ref[...] = _dot(hn, w_ref[0:512, :], NT).astype(BF16)
        kv_ref[...] = _dot(hn, w_ref[512:768, :], NT).astype(BF16)
        ca_ref[...] = _dot(hn, w_ref[768:1280, :], NT)
        cg_ref[...] = _dot(hn, w_ref[1280:1792, :], NT)

    return _pcall(
        body, name="inproj_fwd", grid=(nt,),
        in_specs=[pl.BlockSpec(memory_space=pl.ANY), _full((N_META, D_MODEL)), _full((1, D_MODEL)),
                  _full((1792, D_MODEL))],
        out_specs=[_rows(TM, D_MODEL), _rows(TM, 512), _rows(TM, 256), _rows(TM, 512), _rows(TM, 512)],
        out_shape=[jax.ShapeDtypeStruct((R, D_MODEL), F32),
                   jax.ShapeDtypeStruct((R, 512), BF16), jax.ShapeDtypeStruct((R, 256), BF16),
                   jax.ShapeDtypeStruct((R, 512), F32), jax.ShapeDtypeStruct((R, 512), F32)],
        scratch_shapes=[pltpu.VMEM((2, TM, D_MODEL), F32), pltpu.SemaphoreType.DMA((2,))],
        semantics=("arbitrary",),
    )(x, meta, g1, w_in_t)


GB = GROUP * BLOCK
ATTN_SCALE = 1.0 / math.sqrt(HEAD_DIM)


def _group_lanes(xt, g):
    return jnp.concatenate(
        [xt[HEAD_DIM * (GROUP * g + j):HEAD_DIM * (GROUP * g + j + 1), :] for j in range(GROUP)], axis=1)


def _head_lanes(ref, g):
    return jnp.concatenate([ref[GROUP * g + j:GROUP * g + j + 1, :] for j in range(GROUP)], axis=1)


def _head_rows(xs):
    return jnp.concatenate([x[:, BLOCK * j:BLOCK * (j + 1)] for x in xs for j in range(GROUP)], axis=0)


def _attn_tables(sink_ref, bias_s, sink_s):
    kk = lax.broadcasted_iota(jnp.int32, (BLOCK, BLOCK), 0)
    ii = lax.broadcasted_iota(jnp.int32, (BLOCK, BLOCK), 1)
    dist = jnp.where(kk <= ii, ii - kk, ii - kk + BLOCK).astype(F32)
    for h in range(N_HEADS):
        bias_s[:, BLOCK * h:BLOCK * (h + 1)] = dist * -(2.0 ** -(h + 1))
        sink_s[:, BLOCK * h:BLOCK * (h + 1)] = jnp.zeros((1, BLOCK), F32) + sink_ref[0, h]


def _attn_masks(b):
    kk = lax.broadcasted_iota(jnp.int32, (BLOCK, GB), 0)
    ii = lax.broadcasted_iota(jnp.int32, (BLOCK, GB), 1) & (BLOCK - 1)
    sel = kk <= ii
    pen = jnp.where(sel, jnp.where(b >= 1, 0.0, NEG), jnp.where(b >= 2, 0.0, NEG))
    mj = lax.broadcasted_iota(jnp.int32, (N_META, GB), 0)
    mi = lax.broadcasted_iota(jnp.int32, (N_META, GB), 1) & (BLOCK - 1)
    pen_m = jnp.where((mj + LEAD) <= (mi + b * BLOCK), 0.0, NEG)
    return sel, pen, pen_m


def _attn_scores(qt, kc, kp, km, sel, pen, pen_m, bias):
    s_b = jnp.where(sel, _dot(kc, qt, NN), _dot(kp, qt, NN)) + bias + pen
    s_m = _dot(km, qt, NN) + pen_m
    return s_b, s_m


def _attn_fwd(q, kv, sinks, shards):
    R = q.shape[0]
    nt = R // TM
    ns = len(shards)

    def body(sink_ref, q_ref, kv_ref, kvh_ref, kvm_ref, *refs):
        ag_ins, (o_ref, lse_ref), ag_outs = refs[:ns], refs[ns:ns + 2], refs[ns + 2:2 * ns + 2]
        ag_sems, (bias_s, sink_s) = refs[2 * ns + 2:2 * ns + 5], refs[2 * ns + 5:]
        i = pl.program_id(0)
        ag_finish = _carried_allgather(i, nt, shards, ag_ins + ag_outs + ag_sems)
        pl.when(i == 0)(functools.partial(_attn_tables, sink_ref, bias_s, sink_s))
        for s in range(TILE_BLOCKS):
            rows = slice(BLOCK * s, BLOCK * (s + 1))
            kvp = kvh_ref[...] if s == 0 else kv_ref[BLOCK * (s - 1):BLOCK * s, :]
            sel, pen, pen_m = _attn_masks(i * TILE_BLOCKS + s)
            q_t = (q_ref[rows, :] * ATTN_SCALE).T
            kvc_t, kvp_t = kv_ref[rows, :].T, kvp.T
            outs = []
            for g in range(N_HEADS // GROUP):
                ks = slice(HEAD_DIM * g, HEAD_DIM * (g + 1))
                vs = slice(KV_W + HEAD_DIM * g, KV_W + HEAD_DIM * (g + 1))
                lanes = slice(GB * g, GB * (g + 1))
                s_b, s_m = _attn_scores(_group_lanes(q_t, g), kv_ref[rows, ks], kvp[:, ks], kvm_ref[LEAD:BLOCK, ks],
                                        sel, pen, pen_m, bias_s[:, lanes])
                sink = sink_s[:, lanes]
                m = jnp.maximum(jnp.maximum(jnp.max(s_b, axis=0, keepdims=True),
                                            jnp.max(s_m, axis=0, keepdims=True)), sink)
                p_b = jnp.exp(s_b - m)
                p_m = jnp.exp(s_m - m)
                l = jnp.sum(p_b, axis=0, keepdims=True) + jnp.sum(p_m, axis=0, keepdims=True) + jnp.exp(sink - m)
                p_c = jnp.where(sel, p_b, 0.0).astype(BF16)
                p_p = jnp.where(sel, 0.0, p_b).astype(BF16)
                o_t = (_dot(kvc_t[vs, :], p_c, NN) + _dot(kvp_t[vs, :], p_p, NN)
                       + _dot(kvm_ref[LEAD:BLOCK, vs], p_m.astype(BF16), TN))
                outs.append(o_t / l)
                lse = m + jnp.log(l)
                for j in range(GROUP):
                    lse_ref[GROUP * g + j:GROUP * g + j + 1, rows] = lse[:, BLOCK * j:BLOCK * (j + 1)]
            o_ref[rows, :] = _head_rows(outs).T
        ag_finish()

    res = _pcall(
        body, name="attn_fwd", grid=(nt,),
        in_specs=[pl.BlockSpec(memory_space=pltpu.SMEM),
                  _rows(TM, 512), _rows(TM, 256),
                  pl.BlockSpec((BLOCK, 256), lambda i: (jnp.maximum(i * TILE_BLOCKS - 1, 0), 0)),
                  _full((BLOCK, 256))] + [ANY] * ns,
        out_specs=[_rows(TM, 512), pl.BlockSpec((N_HEADS, TM), lambda i: (0, i))] + [ANY] * ns,
        out_shape=[jax.ShapeDtypeStruct((R, 512), F32), jax.ShapeDtypeStruct((N_HEADS, R), F32)]
        + _gathered_shapes(shards),
        scratch_shapes=_allgather_sems(ns) + [pltpu.VMEM((BLOCK, N_HEADS * BLOCK), F32),
                                              pltpu.VMEM((1, N_HEADS * BLOCK), F32)],
        semantics=("arbitrary",),
    )(sinks, q, kv, kv, kv, *shards)
    return res[0], res[1], res[2:]


def _ln_silu(y, lg, lb):
    mu = jnp.mean(y, axis=-1, keepdims=True)
    xc = y - mu
    rstd = lax.rsqrt(jnp.mean(xc * xc, axis=-1, keepdims=True) + EPS)
    xhat = xc * rstd
    yn = xhat * lg + lb
    return yn, xhat, rstd


PHASE_ROWS = HALO + TM - 8


def _phase_scratch():
    return pltpu.VMEM((7, PHASE_ROWS, CONV_W), F32)


def _phase_copies(src_s, ph_s):
    for b in range(1, 8):
        ph_s[b - 1] = src_s[pl.ds(b, PHASE_ROWS), :]


def _shifted(src_s, ph_s, start, rows):
    a8, b = (start // 8) * 8, start % 8
    if b == 0:
        return src_s[pl.ds(a8, rows), :]
    return ph_s[b - 1, pl.ds(a8, rows), :]


def _conv_fwd(ca, cg, conv_w, conv_b, ln_g, ln_b, shards):
    R = ca.shape[0]
    nt = R // TM
    hpt = TM // HALO
    ns = len(shards)

    def body(ca_ref, cg_ref, cah_ref, cgh_ref, w_ref, b_ref, lg_ref, lb_ref, *refs):
        ag_ins, (oc_ref, y_ref), ag_outs = refs[:ns], refs[ns:ns + 2], refs[ns + 2:2 * ns + 2]
        ag_sems, (u_s, uph_s) = refs[2 * ns + 2:2 * ns + 5], refs[2 * ns + 5:]
        i = pl.program_id(0)
        ag_finish = _carried_allgather(i, nt, shards, ag_ins + ag_outs + ag_sems)
        u_s[HALO:HALO + TM, :] = ca_ref[...] * _sigmoid(cg_ref[...])
        u_s[0:HALO, :] = jnp.where(i > 0, cah_ref[...] * _sigmoid(cgh_ref[...]), 0.0)
        _phase_copies(u_s, uph_s)
        for rc in range(TM // CONV_ROWS):
            base = rc * CONV_ROWS + HALO - (CONV_K - 1)
            acc = jnp.zeros((CONV_ROWS, CONV_W), F32) + b_ref[...]
            for k in range(CONV_K):
                acc = acc + _shifted(u_s, uph_s, base + k, CONV_ROWS) * w_ref[k:k + 1, :]
            rows = slice(rc * CONV_ROWS, (rc + 1) * CONV_ROWS)
            y_ref[rows, :] = acc
            yn, _, _ = _ln_silu(acc, lg_ref[...], lb_ref[...])
            oc_ref[rows, :] = yn * _sigmoid(yn)
        ag_finish()

    prev_halo = pl.BlockSpec((HALO, CONV_W), lambda i: (jnp.maximum(i * hpt - 1, 0), 0))
    anywhere = pl.BlockSpec(memory_space=pl.ANY)
    res = _pcall(
        body, name="conv_fwd", grid=(nt,),
        in_specs=[_rows(TM, CONV_W), _rows(TM, CONV_W), prev_halo, prev_halo,
                  _full((CONV_K, CONV_W)), _full((1, CONV_W)), _full((1, CONV_W)), _full((1, CONV_W))]
        + [anywhere] * ns,
        out_specs=[_rows(TM, CONV_W), _rows(TM, CONV_W)] + [anywhere] * ns,
        out_shape=[jax.ShapeDtypeStruct((R, CONV_W), F32), jax.ShapeDtypeStruct((R, CONV_W), F32)]
        + _gathered_shapes(shards),
        scratch_shapes=_allgather_sems(ns) + [pltpu.VMEM((HALO + TM, CONV_W), F32), _phase_scratch()],
        semantics=("arbitrary",),
    )(ca, cg, ca, cg, conv_w, conv_b, ln_g, ln_b, *shards)
    return res[:2], res[2:]


def _outproj_fwd(h0, o_attn, o_conv, ga, gc, w_out, shards):
    R = h0.shape[0]
    nt = R // TM
    ns = len(shards)

    def body(h_ref, oa_ref, oc_ref, ga_ref, gc_ref, w_ref, *refs):
        ag_ins, h1_ref, ag_outs, ag_sems = refs[:ns], refs[ns], refs[ns + 1:2 * ns + 1], refs[2 * ns + 1:]
        ag_finish = _carried_allgather(pl.program_id(0), nt, shards, ag_ins + ag_outs + ag_sems)
        oa, oc = oa_ref[...], oc_ref[...]
        ma = (oa * _rms_stats(oa) * ga_ref[...]).astype(BF16)
        mc = (oc * _rms_stats(oc) * gc_ref[...]).astype(BF16)
        h1_ref[...] = h_ref[...] + _dot(ma, w_ref[0:512, :], NN) + _dot(mc, w_ref[512:1024, :], NN)
        ag_finish()

    anywhere = pl.BlockSpec(memory_space=pl.ANY)
    res = _pcall(
        body, name="outproj_fwd", grid=(nt,),
        in_specs=[_rows(TM, D_MODEL), _rows(TM, 512), _rows(TM, 512), _full((1, 512)), _full((1, 512)),
                  _full((D_MODEL, D_MODEL))] + [anywhere] * ns,
        out_specs=[_rows(TM, D_MODEL)] + [anywhere] * ns,
        out_shape=[jax.ShapeDtypeStruct((R, D_MODEL), F32)] + _gathered_shapes(shards),
        scratch_shapes=_allgather_sems(ns),
        semantics=("arbitrary",),
    )(h0, o_attn, o_conv, ga, gc, w_out, *shards)
    return res[0], res[1:]


def _target_copy(tgt_hbm, tgt_s, sem, i, first):
    if first:
        return pltpu.make_async_copy(tgt_hbm.at[pl.ds(0, TM - BLOCK)], tgt_s.at[pl.ds(BLOCK, TM - BLOCK)], sem)
    return pltpu.make_async_copy(tgt_hbm.at[pl.ds(i * TM - BLOCK, TM)], tgt_s, sem)


def _resident(shape):
    nd = len(shape)
    return pl.BlockSpec(shape, lambda *_: (0,) * nd, pipeline_mode=pl.Buffered(1))


def _ffn_fwd(h1, g2, wg_t, wu_t, wd, gf, target):
    R = h1.shape[0]
    nt = R // TM

    def body(h1_ref, g2_ref, wg_ref, wu_ref, wd_ref, gf_ref, tgt_hbm,
             gate_ref, up_ref, act_s, dh2_ref, loss_ref, dgf_ref, tgt_s, sem):
        i = pl.program_id(0)

        @pl.when(i == 0)
        def _():
            loss_ref[...] = jnp.zeros_like(loss_ref)
            dgf_ref[...] = jnp.zeros_like(dgf_ref)
            tgt_s[0:BLOCK, :] = jnp.zeros((BLOCK, D_MODEL), F32)
            _target_copy(tgt_hbm, tgt_s, sem, i, True).start()

        pl.when(i > 0)(lambda: _target_copy(tgt_hbm, tgt_s, sem, i, False).start())
        h1 = h1_ref[...]
        hn = (h1 * _rms_stats(h1) * g2_ref[...]).astype(BF16)
        for cs in FF_SUB:
            gate = _dot(hn, wg_ref[cs, :], NT)
            up = _dot(hn, wu_ref[cs, :], NT)
            gate_ref[:, cs] = gate.astype(BF16)
            up_ref[:, cs] = up.astype(BF16)
            act_s[:, cs] = (gate * _sigmoid(gate) * up).astype(BF16)
        part = _dot(act_s[...], wd_ref[...], NN)
        pl.when(i == 0)(lambda: _target_copy(tgt_hbm, tgt_s, sem, i, True).wait())
        pl.when(i > 0)(lambda: _target_copy(tgt_hbm, tgt_s, sem, i, False).wait())
        h2 = h1 + part
        rf = _rms_stats(h2)
        gf = gf_ref[...]
        row = lax.broadcasted_iota(jnp.int32, (TM, 1), 0) + i * TM
        err = jnp.where(row >= BLOCK, h2 * rf * gf - tgt_s[...], 0.0)
        dy = err * (1.0 / D_MODEL)
        dh2, dgf = _rms_bwd(dy, h2, rf, gf)
        dh2_ref[...] = dh2
        loss_ref[...] += (0.5 / D_MODEL) * jnp.sum(err * err)
        dgf_ref[...] += dgf

    wspec = _resident((D_FF, D_MODEL))
    return _pcall(
        body, name="ffn_fwd", grid=(nt,),
        in_specs=[_rows(TM, D_MODEL), _full((1, D_MODEL)), wspec, wspec, wspec, _full((1, D_MODEL)),
                  pl.BlockSpec(memory_space=pl.ANY)],
        out_specs=[_rows(TM, D_FF), _rows(TM, D_FF), _rows(TM, D_FF), _rows(TM, D_MODEL), _full((8, 128)),
                   _full((1, D_MODEL))],
        out_shape=[jax.ShapeDtypeStruct((R, D_FF), BF16)] * 3
        + [jax.ShapeDtypeStruct((R, D_MODEL), F32),
           jax.ShapeDtypeStruct((8, 128), F32), jax.ShapeDtypeStruct((1, D_MODEL), F32)],
        scratch_shapes=[pltpu.VMEM((TM, D_MODEL), F32), pltpu.SemaphoreType.DMA],
        semantics=("arbitrary",),
    )(h1, g2, wg_t, wu_t, wd, gf, target)


def _ffn_bwd(dh2, h1, g2, gate, up, wg_t, wu_t, wd):
    R = h1.shape[0]
    nt = R // TM
    act_shape = jax.ShapeDtypeStruct((R, D_FF), BF16)

    def act_body(dh2_ref, gate_ref, up_ref, wd_ref, dgate_ref, dup_ref):
        dhb = dh2_ref[...].astype(BF16)
        for cs in FF_SUB:
            dact = _dot(dhb, wd_ref[cs, :], NT)
            gate = gate_ref[:, cs].astype(F32)
            up = up_ref[:, cs].astype(F32)
            sig = _sigmoid(gate)
            dgate_ref[:, cs] = (dact * up * (sig * (1.0 + gate * (1.0 - sig)))).astype(BF16)
            dup_ref[:, cs] = (dact * (gate * sig)).astype(BF16)

    dgate, dup = _pcall(
        act_body, name="ffn_bwd_act", grid=(nt,),
        in_specs=[_rows(TM, D_MODEL), _rows(TM, D_FF), _rows(TM, D_FF), _resident((D_FF, D_MODEL))],
        out_specs=[_rows(TM, D_FF), _rows(TM, D_FF)], out_shape=[act_shape, act_shape],
        semantics=("parallel",),
    )(dh2, gate, up, wd)

    def in_body(dh2_ref, h1_ref, g2_ref, dgate_ref, dup_ref, wg_ref, wu_ref, hn_ref, dh1_ref, dg2_ref):
        @pl.when(pl.program_id(0) == 0)
        def _():
            dg2_ref[...] = jnp.zeros_like(dg2_ref)

        dhn = _dot(dgate_ref[...], wg_ref[...], NN) + _dot(dup_ref[...], wu_ref[...], NN)
        h1 = h1_ref[...]
        r = _rms_stats(h1)
        g2 = g2_ref[...]
        hn_ref[...] = (h1 * r * g2).astype(BF16)
        dx, dg = _rms_bwd(dhn, h1, r, g2)
        dh1_ref[...] = dh2_ref[...] + dx
        dg2_ref[...] += dg

    hn2, dh1, dg2 = _pcall(
        in_body, name="ffn_bwd_in", grid=(nt,),
        in_specs=[_rows(TM, D_MODEL), _rows(TM, D_MODEL), _full((1, D_MODEL)), _rows(TM, D_FF), _rows(TM, D_FF),
                  _resident((D_FF, D_MODEL)), _resident((D_FF, D_MODEL))],
        out_specs=[_rows(TM, D_MODEL), _rows(TM, D_MODEL), _full((1, D_MODEL))],
        out_shape=[jax.ShapeDtypeStruct((R, D_MODEL), BF16), jax.ShapeDtypeStruct((R, D_MODEL), F32),
                   jax.ShapeDtypeStruct((1, D_MODEL), F32)],
        semantics=("arbitrary",),
    )(dh2, h1, g2, dgate, dup, wg_t, wu_t)
    return dgate, dup, hn2, dh1, dg2


def _wgrad(a, b, tm, name, partials=()):
    K, M = a.shape
    N = b.shape[1]
    tk = K // WGRAD_K_TILES if K % (WGRAD_K_TILES * BLOCK) == 0 else TM
    nm, nk, npart = M // tm, K // tk, len(partials)

    def body(a_ref, b_ref, *refs):
        p_ins, o_ref, p_outs, sems = refs[:npart], refs[npart], refs[npart + 1:2 * npart + 1], refs[2 * npart + 1:]
        step = pl.program_id(0) * nk + pl.program_id(1)
        exchange = functools.partial(_chip_copies, p_ins, p_outs, *sems)
        if npart:
            _hosted(step, exchange)

        @pl.when(pl.program_id(1) == 0)
        def _():
            o_ref[...] = jnp.zeros_like(o_ref)

        o_ref[...] += _dot(a_ref[...], b_ref[...].astype(BF16), TN)
        if npart:
            _hosted_wait(step, nm * nk, exchange)

    res = _pcall(
        body, name=name, grid=(nm, nk),
        in_specs=[pl.BlockSpec((tk, tm), lambda m, k: (k, m)), pl.BlockSpec((tk, N), lambda m, k: (k, 0))]
        + [ANY] * npart,
        out_specs=[pl.BlockSpec((tm, N), lambda m, k: (m, 0))] + [ANY] * npart,
        out_shape=[jax.ShapeDtypeStruct((M, N), F32)] + _chip_shapes(partials),
        scratch_shapes=_sem_pair(3 * npart) if npart else [],
        semantics=("arbitrary", "arbitrary"),
    )(a, b, *partials)
    return (res[0], res[1:]) if npart else res[0]


def _outproj_bwd(dh1, o_attn, o_conv, ga, gc, w_out, grads):
    R = dh1.shape[0]
    nt, ng = R // TM, len(grads)

    def body(dh1_ref, oa_ref, oc_ref, ga_ref, gc_ref, w_ref, *refs):
        g_ins, (doa_ref, doc_ref, mixed_ref, dga_ref, dgc_ref) = refs[:ng], refs[ng:ng + 5]
        g_outs, (send_sems, recv_sems) = refs[ng + 5:2 * ng + 5], refs[2 * ng + 5:]
        exchange = functools.partial(_sibling_copies, g_ins, g_outs, send_sems, recv_sems)
        _hosted(pl.program_id(0), exchange)

        @pl.when(pl.program_id(0) == 0)
        def _():
            dga_ref[...] = jnp.zeros_like(dga_ref)
            dgc_ref[...] = jnp.zeros_like(dgc_ref)

        dm = _dot(dh1_ref[...].astype(BF16), w_ref[...], NT)
        oa, oc = oa_ref[...], oc_ref[...]
        ra, rc = _rms_stats(oa), _rms_stats(oc)
        mixed_ref[:, 0:512] = (oa * ra * ga_ref[...]).astype(BF16)
        mixed_ref[:, 512:1024] = (oc * rc * gc_ref[...]).astype(BF16)
        doa, dga = _rms_bwd(dm[:, 0:512], oa, ra, ga_ref[...])
        doc, dgc = _rms_bwd(dm[:, 512:1024], oc, rc, gc_ref[...])
        doa_ref[...] = doa
        doc_ref[...] = doc
        dga_ref[...] += dga
        dgc_ref[...] += dgc
        _hosted_wait(pl.program_id(0), nt, exchange)

    res = _pcall(
        body, name="outproj_bwd", grid=(nt,),
        in_specs=[_rows(TM, D_MODEL), _rows(TM, 512), _rows(TM, 512), _full((1, 512)), _full((1, 512)),
                  _full((D_MODEL, D_MODEL))] + [ANY] * ng,
        out_specs=[_rows(TM, 512), _rows(TM, 512), _rows(TM, D_MODEL), _full((1, 512)), _full((1, 512))]
        + [ANY] * ng,
        out_shape=[jax.ShapeDtypeStruct((R, 512), F32), jax.ShapeDtypeStruct((R, 512), F32),
                   jax.ShapeDtypeStruct((R, D_MODEL), BF16),
                   jax.ShapeDtypeStruct((1, 512), F32), jax.ShapeDtypeStruct((1, 512), F32)]
        + _sibling_shapes(grads),
        scratch_shapes=_sem_pair(ng),
        semantics=("arbitrary",),
    )(dh1, o_attn, o_conv, ga, gc, w_out, *grads)
    return res[:5], res[5:]


def _conv_bwd(do_conv, y, ca, cg, conv_w, ln_g, ln_b, partials):
    R = ca.shape[0]
    nt = R // TM
    hpt = TM // HALO
    npart = len(partials)

    def body(do_ref, doh_ref, y_ref, yh_ref, ca_ref, cg_ref, cah_ref, cgh_ref, w_ref, lg_ref, lb_ref, *refs):
        p_ins, (dca_ref, dcg_ref, dw_ref, db_ref, dlg_ref, dlb_ref) = refs[:npart], refs[npart:npart + 6]
        p_outs, (send_sems, recv_sems, u_s, dy_s, uph_s, dyph_s) = refs[npart + 6:2 * npart + 6], refs[2 * npart + 6:]
        i = pl.program_id(0)
        exchange = functools.partial(_chip_copies, p_ins, p_outs, send_sems, recv_sems)
        _hosted(i, exchange)

        @pl.when(i == 0)
        def _():
            dw_ref[...] = jnp.zeros_like(dw_ref)
            db_ref[...] = jnp.zeros_like(db_ref)
            dlg_ref[...] = jnp.zeros_like(dlg_ref)
            dlb_ref[...] = jnp.zeros_like(dlb_ref)

        lg, lb = lg_ref[...], lb_ref[...]

        def ln_bwd(yv, dov):
            yn, xhat, rstd = _ln_silu(yv, lg, lb)
            sig = _sigmoid(yn)
            dyn = dov * (sig * (1.0 + yn * (1.0 - sig)))
            dxh = dyn * lg
            dyv = rstd * (dxh - jnp.mean(dxh, axis=-1, keepdims=True)
                          - xhat * jnp.mean(dxh * xhat, axis=-1, keepdims=True))
            return dyv, dyn, xhat

        dyv, dyn, xhat = ln_bwd(y_ref[...], do_ref[...])
        dy_s[0:TM, :] = dyv
        dlg_ref[...] += jnp.sum(dyn * xhat, axis=0, keepdims=True)
        dlb_ref[...] += jnp.sum(dyn, axis=0, keepdims=True)
        db_ref[...] += jnp.sum(dyv, axis=0, keepdims=True)
        dyh, _, _ = ln_bwd(yh_ref[...], doh_ref[...])
        dy_s[TM:TM + HALO, :] = jnp.where(i < nt - 1, dyh, 0.0)
        u_s[HALO:HALO + TM, :] = ca_ref[...] * _sigmoid(cg_ref[...])
        u_s[0:HALO, :] = jnp.where(i > 0, cah_ref[...] * _sigmoid(cgh_ref[...]), 0.0)
        _phase_copies(dy_s, dyph_s)
        _phase_copies(u_s, uph_s)

        for rc in range(TM // CONV_ROWS):
            acc = jnp.zeros((CONV_ROWS, CONV_W), F32)
            for k in range(CONV_K):
                acc = acc + _shifted(dy_s, dyph_s, rc * CONV_ROWS + CONV_K - 1 - k, CONV_ROWS) * w_ref[k:k + 1, :]
            rows = slice(rc * CONV_ROWS, (rc + 1) * CONV_ROWS)
            sg = _sigmoid(cg_ref[rows, :])
            dca_ref[rows, :] = (acc * sg).astype(BF16)
            dcg_ref[rows, :] = (acc * ca_ref[rows, :] * sg * (1.0 - sg)).astype(BF16)

        for k in range(CONV_K):
            prod = _shifted(u_s, uph_s, HALO - (CONV_K - 1) + k, TM) * dy_s[0:TM, :]
            dw_ref[k:k + 1, :] += jnp.sum(prod, axis=0, keepdims=True)
        _hosted_wait(i, nt, exchange)

    prev_halo = pl.BlockSpec((HALO, CONV_W), lambda i: (jnp.maximum(i * hpt - 1, 0), 0))
    next_halo = pl.BlockSpec((HALO, CONV_W), lambda i: (jnp.minimum((i + 1) * hpt, nt * hpt - 1), 0))
    vec = jax.ShapeDtypeStruct((1, CONV_W), F32)
    res = _pcall(
        body, name="conv_bwd", grid=(nt,),
        in_specs=[_rows(TM, CONV_W), next_halo, _rows(TM, CONV_W), next_halo,
                  _rows(TM, CONV_W), _rows(TM, CONV_W), prev_halo, prev_halo,
                  _full((CONV_K, CONV_W)), _full((1, CONV_W)), _full((1, CONV_W))] + [ANY] * npart,
        out_specs=[_rows(TM, CONV_W), _rows(TM, CONV_W), _full((32, CONV_W)),
                   _full((1, CONV_W)), _full((1, CONV_W)), _full((1, CONV_W))] + [ANY] * npart,
        out_shape=[jax.ShapeDtypeStruct((R, CONV_W), BF16), jax.ShapeDtypeStruct((R, CONV_W), BF16),
                   jax.ShapeDtypeStruct((32, CONV_W), F32), vec, vec, vec] + _chip_shapes(partials),
        scratch_shapes=_sem_pair(3 * npart)
        + [pltpu.VMEM((HALO + TM, CONV_W), F32), pltpu.VMEM((TM + HALO, CONV_W), F32),
           _phase_scratch(), _phase_scratch()],
        semantics=("arbitrary",),
    )(do_conv, do_conv, y, y, ca, cg, ca, cg, conv_w, ln_g, ln_b, *partials)
    return res[:6], res[6:]


def _attn_bwd(q, kv, sinks, o, lse, do, grads, partials):
    R = q.shape[0]
    nt = R // TM
    ng, npart = len(grads), len(partials)
    nx = ng + npart

    def body(sink_ref, q_ref, kv_ref, kvh_ref, kvm_ref, o_ref, lse_ref, do_ref, *refs):
        x_ins, (dq_ref, dkv_ref, dkvm_ref, dsink_ref) = refs[:nx], refs[nx:nx + 4]
        x_outs = refs[nx + 4:2 * nx + 4]
        g_send, g_recv, p_send, p_recv, carry_s, bias_s, sink_s = refs[2 * nx + 4:]
        i = pl.program_id(0)

        def exchange():
            return (_sibling_copies(x_ins[:ng], x_outs[:ng], g_send, g_recv)
                    + _chip_copies(x_ins[ng:], x_outs[ng:], p_send, p_recv))

        _hosted(i, exchange)

        @pl.when(i == 0)
        def _():
            dkvm_ref[...] = jnp.zeros_like(dkvm_ref)
            carry_s[...] = jnp.zeros_like(carry_s)
            for h in range(N_HEADS):
                dsink_ref[0, h] = 0.0
            _attn_tables(sink_ref, bias_s, sink_s)

        @pl.when(i < nt)
        def _():
            head_of = lax.broadcasted_iota(jnp.int32, (N_HEADS, ATTN_W), 1) // HEAD_DIM
            ind = (head_of == lax.broadcasted_iota(jnp.int32, (N_HEADS, ATTN_W), 0)).astype(BF16)
            dkm = [jnp.zeros((N_META, 2 * KV_W), F32)]
            dsink = [0.0] * N_HEADS
            prev_cur = carry_s[...]
            for s in range(TILE_BLOCKS):
                rows = slice(BLOCK * s, BLOCK * (s + 1))
                kvp = kvh_ref[...] if s == 0 else kv_ref[BLOCK * (s - 1):BLOCK * s, :]
                sel, pen, pen_m = _attn_masks(i * TILE_BLOCKS + s)
                q_t = (q_ref[rows, :] * ATTN_SCALE).T
                do_t = do_ref[rows, :].astype(BF16).T
                kvc_t, kvp_t = kv_ref[rows, :].T, kvp.T
                prod = do_ref[rows, :] * o_ref[rows, :]
                hi = prod.astype(BF16)
                lo = (prod - hi.astype(F32)).astype(BF16)
                delta8 = _dot(ind, hi, NT) + _dot(ind, lo, NT)
                lse8 = lse_ref[:, rows]
                dqs, cur, prev, meta = [], [None] * 4, [None] * 4, [None] * 4
                for g in range(N_HEADS // GROUP):
                    ks = slice(HEAD_DIM * g, HEAD_DIM * (g + 1))
                    vs = slice(KV_W + HEAD_DIM * g, KV_W + HEAD_DIM * (g + 1))
                    lanes = slice(GB * g, GB * (g + 1))
                    qg, dog = _group_lanes(q_t, g), _group_lanes(do_t, g)
                    kc, kp, km = kv_ref[rows, ks], kvp[:, ks], kvm_ref[LEAD:BLOCK, ks]
                    vc, vp, vm = kv_ref[rows, vs], kvp[:, vs], kvm_ref[LEAD:BLOCK, vs]
                    s_b, s_m = _attn_scores(qg, kc, kp, km, sel, pen, pen_m, bias_s[:, lanes])
                    lse, delta = _head_lanes(lse8, g), _head_lanes(delta8, g)
                    p_b = jnp.exp(s_b - lse)
                    p_m = jnp.exp(s_m - lse)
                    dp_b = jnp.where(sel, _dot(vc, dog, NN), _dot(vp, dog, NN))
                    ds_b = p_b * (dp_b - delta)
                    ds_m = (p_m * (_dot(vm, dog, NN) - delta)).astype(BF16)
                    dsk = jnp.exp(sink_s[:, lanes] - lse) * delta
                    for j in range(GROUP):
                        dsink[GROUP * g + j] = dsink[GROUP * g + j] - jnp.sum(dsk[:, BLOCK * j:BLOCK * (j + 1)])
                    ds_c = jnp.where(sel, ds_b, 0.0).astype(BF16)
                    ds_p = jnp.where(sel, 0.0, ds_b).astype(BF16)
                    p_c = jnp.where(sel, p_b, 0.0).astype(BF16)
                    p_p = jnp.where(sel, 0.0, p_b).astype(BF16)
                    dqs.append((_dot(kvc_t[ks, :], ds_c, NN) + _dot(kvp_t[ks, :], ds_p, NN)
                                + _dot(km, ds_m, TN)) * ATTN_SCALE)
                    cur[g], cur[2 + g] = _dot(ds_c, qg, NT), _dot(p_c, dog, NT)
                    prev[g], prev[2 + g] = _dot(ds_p, qg, NT), _dot(p_p, dog, NT)
                    meta[g], meta[2 + g] = _dot(ds_m, qg, NT), _dot(p_m.astype(BF16), dog, NT)
                dq_ref[rows, :] = _head_rows(dqs).astype(BF16).T
                dkv_ref[rows, :] = (prev_cur + jnp.concatenate(prev, axis=1)).astype(BF16)
                prev_cur = jnp.concatenate(cur, axis=1)
                dkm.append(jnp.concatenate(meta, axis=1))
            carry_s[...] = prev_cur
            dkvm_ref[...] += functools.reduce(lambda a, b: a + b, dkm)
            for h in range(N_HEADS):
                dsink_ref[0, h] += dsink[h]

        @pl.when(i == nt)
        def _():
            dkv_ref[...] = jnp.zeros_like(dkv_ref)
            dkv_ref[0:BLOCK, :] = carry_s[...].astype(BF16)

        _hosted_wait(i, nt + 1, exchange)

    tile = lambda cols: pl.BlockSpec((TM, cols), lambda i: (jnp.minimum(i, nt - 1), 0))
    res = _pcall(
        body, name="attn_bwd", grid=(nt + 1,),
        in_specs=[pl.BlockSpec(memory_space=pltpu.SMEM), tile(512), tile(256),
                  pl.BlockSpec((BLOCK, 256), lambda i: (jnp.clip(i * TILE_BLOCKS - 1, 0, R // BLOCK - 1), 0)),
                  _full((BLOCK, 256)), tile(512),
                  pl.BlockSpec((N_HEADS, TM), lambda i: (0, jnp.minimum(i, nt - 1))), tile(512)]
        + [ANY] * nx,
        out_specs=[tile(512), _rows(TM, 256), _full((N_META, 256)), pl.BlockSpec(memory_space=pltpu.SMEM)]
        + [ANY] * nx,
        out_shape=[jax.ShapeDtypeStruct((R, 512), BF16), jax.ShapeDtypeStruct((R + TM, 256), BF16),
                   jax.ShapeDtypeStruct((N_META, 256), F32), jax.ShapeDtypeStruct((1, N_HEADS), F32)]
        + _sibling_shapes(grads) + _chip_shapes(partials),
        scratch_shapes=_sem_pair(ng) + _sem_pair(3 * npart)
        + [pltpu.VMEM((BLOCK, 256), F32), pltpu.VMEM((BLOCK, N_HEADS * BLOCK), F32),
           pltpu.VMEM((1, N_HEADS * BLOCK), F32)],
        semantics=("arbitrary",),
    )(sinks, q, kv, kv, kv, o, lse, do, *grads, *partials)
    return res[:4], res[4:4 + ng], res[4 + ng:]


def _inproj_bwd(dh1, h0, g1, dq, dkv, dkvm, dca, dcg, w_in_t):
    R = h0.shape[0]
    nt = R // TM
    assert nt >= 2

    def body(dh1_ref, h0_ref, g_ref, dq_ref, dkv_ref, dkvm_ref, dca_ref, dcg_ref, w_ref,
             gx_hbm, dmeta_ref, dproj_ref, hn_ref, dg_ref, dx_s, gx_sems):
        i = pl.program_id(0)

        def gx_copy(step, slot, first):
            if first:
                return pltpu.make_async_copy(dx_s.at[slot, pl.ds(BLOCK, TM - BLOCK)],
                                             gx_hbm.at[pl.ds(0, TM - BLOCK)], gx_sems.at[slot])
            return pltpu.make_async_copy(
                dx_s.at[slot], gx_hbm.at[pl.ds(pl.multiple_of(step * TM - BLOCK, BLOCK), TM)], gx_sems.at[slot])

        @pl.when(i == 0)
        def _():
            dg_ref[...] = jnp.zeros_like(dg_ref)

        dproj_ref[:, 0:512] = dq_ref[...]
        dproj_ref[:, 512:768] = dkv_ref[...]
        dproj_ref[:, 768:1280] = dca_ref[...]
        dproj_ref[:, 1280:1792] = dcg_ref[...]

        @pl.when(i == 0)
        def _():
            dproj_ref[LEAD:BLOCK, 512:768] = dkvm_ref[...].astype(BF16)

        dhn = _dot(dproj_ref[...], w_ref[...], NN)
        h = h0_ref[...]
        r = _rms_stats(h)
        g = g_ref[...]
        hn_ref[...] = (h * r * g).astype(BF16)
        dx, dg = _rms_bwd(dhn, h, r, g)
        dg_ref[...] += dg
        slot = i % 2
        pl.when(i == 2)(lambda: gx_copy(0, 0, True).wait())
        pl.when(i > 2)(lambda: gx_copy(i - 2, slot, False).wait())
        dx_s[slot] = dh1_ref[...] + dx

        @pl.when(i == 0)
        def _():
            dmeta_ref[...] = dx_s[0, LEAD:BLOCK, :]
            gx_copy(0, 0, True).start()

        pl.when(i > 0)(lambda: gx_copy(i, slot, False).start())

        @pl.when(i == nt - 1)
        def _():
            gx_copy(nt - 2, (nt - 2) % 2, nt == 2).wait()
            gx_copy(nt - 1, (nt - 1) % 2, False).wait()

    return _pcall(
        body, name="inproj_bwd", grid=(nt,),
        in_specs=[_rows(TM, D_MODEL), _rows(TM, D_MODEL), _full((1, D_MODEL)), _rows(TM, 512), _rows(TM, 256),
                  _full((N_META, 256)), _rows(TM, 512), _rows(TM, 512), _full((1792, D_MODEL))],
        out_specs=[ANY, _full((N_META, D_MODEL)), _rows(TM, 1792), _rows(TM, D_MODEL), _full((1, D_MODEL))],
        out_shape=[jax.ShapeDtypeStruct((R - BLOCK, D_MODEL), F32), jax.ShapeDtypeStruct((N_META, D_MODEL), F32),
                   jax.ShapeDtypeStruct((R, 1792), BF16),
                   jax.ShapeDtypeStruct((R, D_MODEL), BF16), jax.ShapeDtypeStruct((1, D_MODEL), F32)],
        scratch_shapes=[pltpu.VMEM((2, TM, D_MODEL), F32), pltpu.SemaphoreType.DMA((2,))],
        semantics=("arbitrary",),
    )(dh1, h0, g1, dq, dkv, dkvm, dca, dcg, w_in_t)


ANY = pl.BlockSpec(memory_space=pl.ANY)


def _position():
    return lax.axis_index("x"), lax.axis_index("y"), lax.axis_index("c")


def _device_number(p):
    return 4 * p[0] + 2 * p[1] + p[2]


def _two_level_allgather(ins, outs, block, send_sems, recv_sems, local_sems, sem_base=0):
    n = len(ins)
    x, y, c = _position()
    me, sibling = (x, y, c), (x, y, 1 - c)
    chips = [(1 - x, y), (x, 1 - y), (1 - x, 1 - y)]

    def copy(w, k, origin, to, src=None):
        return pltpu.make_async_remote_copy(
            src_ref=block(w, origin) if src is None else src, dst_ref=block(w, origin),
            send_sem=send_sems.at[sem_base + 7 * w + k], recv_sem=recv_sems.at[sem_base + 7 * w + k],
            device_id=to, device_id_type=MESH)

    def mine(w):
        return pltpu.make_async_copy(ins[w], block(w, me), local_sems.at[w])

    def own(w):
        return [copy(w, 0, me, sibling, src=ins[w])] + [
            copy(w, 1 + j, me, (*chip, c), src=ins[w]) for j, chip in enumerate(chips)]

    def passed(w):
        return [copy(w, 4 + j, (*chip, c), sibling) for j, chip in enumerate(chips)]

    def start():
        for w in range(n):
            mine(w).start()
        for w in range(n):
            for cp in own(w):
                cp.start()

    def forward(w):
        fw = passed(w)
        for j, chip in enumerate(chips):
            copy(w, 1 + j, (*chip, c), me).wait_recv()
            fw[j].start()

    def finish():
        for w in range(n):
            copy(w, 0, sibling, me).wait_recv()
            for j, chip in enumerate(chips):
                copy(w, 4 + j, (*chip, 1 - c), me).wait_recv()
        for w in range(n):
            for cp in own(w) + passed(w):
                cp.wait_send()
            mine(w).wait()

    return start, forward, finish


def _carried_allgather(step, n_steps, shards, refs):
    ns = len(shards)
    ins, outs, (send_sems, recv_sems, local_sems) = refs[:ns], refs[ns:2 * ns], refs[2 * ns:]
    start, forward, finish = _two_level_allgather(
        ins, outs, _row_block(outs, [s.shape[0] for s in shards]), send_sems, recv_sems, local_sems)
    pl.when(step == 0)(start)
    total = sum(s.shape[0] for s in shards)
    sent = 0
    for w, s in enumerate(shards):
        sent += s.shape[0]
        pl.when(step == (AG_FORWARD_AT * sent * (n_steps - 1)) // (100 * total))(functools.partial(forward, w))
    return lambda: pl.when(step == n_steps - 1)(finish)


def _gathered_shapes(shards):
    return [jax.ShapeDtypeStruct((N_DEV * s.shape[0], s.shape[1]), s.dtype) for s in shards]


def _allgather_sems(ns):
    return _sem_pair(7 * ns) + [pltpu.SemaphoreType.DMA((ns,))]


def _blocking_allgather(ins, outs, block, send_sems, recv_sems, local_sems, sem_base=0):
    start, forward, finish = _two_level_allgather(ins, outs, block, send_sems, recv_sems, local_sems, sem_base)
    start()
    for w in range(len(ins)):
        forward(w)
    finish()


def _row_block(outs, rows):
    def block(w, p):
        return outs[w].at[pl.ds(pl.multiple_of(_device_number(p) * rows[w], 16), rows[w])]
    return block


def _sibling_copies(ins, outs, send_sems, recv_sems):
    x, y, c = _position()
    return [pltpu.make_async_remote_copy(
        src_ref=ins[w].at[:, 1 - c], dst_ref=outs[w], send_sem=send_sems.at[w], recv_sem=recv_sems.at[w],
        device_id=(x, y, 1 - c), device_id_type=MESH) for w in range(len(ins))]


def _chip_copies(ins, outs, send_sems, recv_sems):
    x, y, c = _position()
    chips = [(1 - x, y), (x, 1 - y), (1 - x, 1 - y)]
    return [pltpu.make_async_remote_copy(
        src_ref=ins[w].at[2 * chip[0] + chip[1]], dst_ref=outs[w].at[k],
        send_sem=send_sems.at[3 * w + k], recv_sem=recv_sems.at[3 * w + k],
        device_id=(*chip, c), device_id_type=MESH) for w in range(len(ins)) for k, chip in enumerate(chips)]


def _hosted(step, make_copies):
    @pl.when(step == 0)
    def _():
        for cp in make_copies():
            cp.start()


def _hosted_wait(step, n_steps, make_copies):
    @pl.when(step == n_steps - 1)
    def _():
        for cp in make_copies():
            cp.wait()


def _sem_pair(n):
    return [pltpu.SemaphoreType.DMA((n,)), pltpu.SemaphoreType.DMA((n,))]


def _allgather_params(shards, small):
    arrays = list(shards) + list(small)
    n, ns = len(arrays), len(shards)

    def body(*refs):
        ins, outs = refs[:n], refs[n:2 * n]
        send_sems, recv_sems, local_sems = refs[2 * n:]

        rows = _row_block(outs, [a.shape[0] for a in arrays])

        def block(w, p):
            return rows(w, p) if w < ns else outs[w].at[_device_number(p)]

        _blocking_allgather(ins, outs, block, send_sems, recv_sems, local_sems)

    out_shape = [jax.ShapeDtypeStruct((N_DEV * a.shape[0], a.shape[1]), a.dtype) for a in shards]
    out_shape += [jax.ShapeDtypeStruct((N_DEV,) + a.shape, a.dtype) for a in small]
    return _pcall(
        body, name="allgather_params", in_specs=[ANY] * n, out_specs=[ANY] * n, out_shape=out_shape,
        scratch_shapes=[pltpu.SemaphoreType.DMA((7 * n,)), pltpu.SemaphoreType.DMA((7 * n,)),
                        pltpu.SemaphoreType.DMA((n,))],
    )(*arrays)


def _reduce_siblings(grads):
    n = len(grads)

    def body(*refs):
        ins, outs = refs[:n], refs[n:2 * n]
        send_sems, recv_sems = refs[2 * n:]
        copies = _sibling_copies(ins, outs, send_sems, recv_sems)
        for cp in copies:
            cp.start()
        for cp in copies:
            cp.wait()

    return _pcall(
        body, name="reduce_siblings", in_specs=[ANY] * n, out_specs=[ANY] * n,
        out_shape=_sibling_shapes(grads), scratch_shapes=_sem_pair(n),
    )(*grads)


def _sibling_shapes(grads):
    return [jax.ShapeDtypeStruct((4,) + g.shape[2:], F32) for g in grads]


def _chip_shapes(partials):
    return [jax.ShapeDtypeStruct((3,) + p.shape[1:], F32) for p in partials]


def _add_sibling(grad, received, core, name):
    _, _, r, cols = grad.shape

    def body(core_ref, g_ref, r_ref, o_ref):
        o_ref[...] = g_ref[...] + r_ref[...]

    return pl.pallas_call(
        body, name=name,
        grid_spec=pltpu.PrefetchScalarGridSpec(
            num_scalar_prefetch=1, grid=(4,),
            in_specs=[pl.BlockSpec((None, None, r, cols), lambda s, core_ref: (s, core_ref[0], 0, 0)),
                      pl.BlockSpec((None, r, cols), lambda s, core_ref: (s, 0, 0))],
            out_specs=pl.BlockSpec((None, r, cols), lambda s, core_ref: (s, 0, 0))),
        out_shape=jax.ShapeDtypeStruct((4, r, cols), F32),
        compiler_params=pltpu.CompilerParams(vmem_limit_bytes=VMEM_LIMIT),
    )(core, grad, received)


def _reduce_chips(partials, small):
    n, ns = len(partials), len(small)

    def body(*refs):
        p_ins, s_ins = refs[:n], refs[n:n + ns]
        p_outs, s_outs = refs[n + ns:2 * n + ns], refs[2 * n + ns:2 * (n + ns)]
        send_sems, recv_sems, local_sems = refs[2 * (n + ns):]
        copies = _chip_copies(p_ins, p_outs, send_sems, recv_sems)
        for cp in copies:
            cp.start()
        _blocking_allgather(s_ins, s_outs, lambda w, p: s_outs[w].at[_device_number(p)],
                            send_sems, recv_sems, local_sems, sem_base=3 * n)
        for cp in copies:
            cp.wait()

    out_shape = _chip_shapes(partials)
    out_shape += [jax.ShapeDtypeStruct((N_DEV,) + a.shape, a.dtype) for a in small]
    nsem = 3 * n + 7 * ns
    return _pcall(
        body, name="reduce_chips", in_specs=[ANY] * (n + ns), out_specs=[ANY] * (n + ns), out_shape=out_shape,
        scratch_shapes=[pltpu.SemaphoreType.DMA((nsem,)), pltpu.SemaphoreType.DMA((nsem,)),
                        pltpu.SemaphoreType.DMA((ns,))],
    )(*partials, *small)


def _adam(g, w, m, v):
    m = ADAM_B1 * m + (1.0 - ADAM_B1) * g
    v = ADAM_B2 * v + (1.0 - ADAM_B2) * (g * g)
    m_hat = m / (1.0 - ADAM_B1 ** ADAM_STEP)
    v_hat = v / (1.0 - ADAM_B2 ** ADAM_STEP)
    delta = -ADAM_LR * (m_hat / (jnp.sqrt(v_hat) + ADAM_EPS) + ADAM_WD * w)
    return delta, m, v


def _adamw(partial, received, slot, w, m, v, name):
    _, r, cols = partial.shape

    def body(slot_ref, p_ref, r_ref, w_ref, m_ref, v_ref, g_ref, d_ref, nm_ref, nv_ref):
        g = p_ref[...] + r_ref[0] + r_ref[1] + r_ref[2]
        g_ref[...] = g
        d_ref[...], nm_ref[...], nv_ref[...] = _adam(g, w_ref[...], m_ref[...], v_ref[...])

    whole = pl.BlockSpec((r, cols), lambda i, slot_ref: (0, 0))
    out = jax.ShapeDtypeStruct((r, cols), F32)
    return pl.pallas_call(
        body, name=name,
        grid_spec=pltpu.PrefetchScalarGridSpec(
            num_scalar_prefetch=1, grid=(1,),
            in_specs=[pl.BlockSpec((None, r, cols), lambda i, slot_ref: (slot_ref[0], 0, 0)),
                      pl.BlockSpec((3, r, cols), lambda i, slot_ref: (0, 0, 0)), whole, whole, whole],
            out_specs=[whole, whole, whole, whole]),
        out_shape=[out, out, out, out],
        compiler_params=pltpu.CompilerParams(vmem_limit_bytes=VMEM_LIMIT),
    )(slot, partial, received, w, m, v)


def _adamw_small(dev, ga, gb, gc, params):
    names = ["meta", "attn_norm", "sinks", "conv_w", "conv_b", "ln_g", "ln_b", "attn_out", "conv_out",
             "ffn_norm", "final_norm"]
    flat = [a for p in params for a in p]
    n_in = len(flat)

    def body(dev_ref, ga_ref, gb_ref, gc_ref, *refs):
        ins, outs = refs[:n_in], refs[n_in:n_in + 4 * len(names)]
        loss_ref, sb, sc = refs[n_in + 4 * len(names):]
        a = ga_ref[0]
        sb[...] = gb_ref[0]
        sc[...] = gc_ref[0]
        for d in range(1, N_DEV):
            a = a + ga_ref[d]
            sb[...] += gb_ref[d]
            sc[...] += gc_ref[d]
        dev = dev_ref[0]
        grads = {
            "attn_norm": a[0:1, :], "ffn_norm": a[1:2, :], "final_norm": a[2:3, :],
            "conv_b": a[3:4, 0:512], "ln_g": a[3:4, 512:1024], "ln_b": a[4:5, 0:512],
            "attn_out": a[4:5, 512:1024], "conv_out": a[5:6, 0:512], "sinks": a[5:6, 512:512 + N_HEADS],
            "meta": sb[pl.ds(pl.multiple_of(dev * N_META, N_META), N_META), :],
            "conv_w": sc[pl.ds(pl.multiple_of(dev * 32, 32), 32), :][0:CONV_K, :],
        }
        for idx, nm in enumerate(names):
            w_ref, m_ref, v_ref = ins[3 * idx:3 * idx + 3]
            g = grads[nm]
            delta, m, v = _adam(g, w_ref[...], m_ref[...], v_ref[...])
            o = outs[4 * idx:4 * idx + 4]
            o[0][...], o[1][...], o[2][...], o[3][...] = g, delta, m, v
        loss_ref[...] = a[6:7, 0:1]

    vm = pl.BlockSpec(memory_space=pltpu.VMEM)
    out_shape = [jax.ShapeDtypeStruct(p[0].shape, F32) for p in params for _ in range(4)]
    out_shape.append(jax.ShapeDtypeStruct((1, 1), F32))
    res = pl.pallas_call(
        body, name="adamw_small",
        grid_spec=pltpu.PrefetchScalarGridSpec(
            num_scalar_prefetch=1, grid=(1,),
            in_specs=[pl.BlockSpec(ga.shape, lambda i, d: (0, 0, 0)), pl.BlockSpec(gb.shape, lambda i, d: (0, 0, 0)),
                      pl.BlockSpec(gc.shape, lambda i, d: (0, 0, 0))]
            + [pl.BlockSpec(a.shape, lambda i, d: (0, 0)) for a in flat],
            out_specs=[pl.BlockSpec(s.shape, lambda i, d: (0, 0)) for s in out_shape],
            scratch_shapes=[pltpu.VMEM(gb.shape[1:], F32), pltpu.VMEM(gc.shape[1:], F32)]),
        out_shape=out_shape,
        compiler_params=pltpu.CompilerParams(vmem_limit_bytes=VMEM_LIMIT),
    )(dev, ga, gb, gc, *flat)
    return [res[4 * i:4 * i + 4] for i in range(len(names))], res[-1]


def kernel(x, meta_tokens, attn_norm_g, w_in, attn_sinks, conv_w, conv_b, conv_ln_g, conv_ln_b, attn_out_g, conv_out_g, w_out, ffn_norm_g, w_gate, w_up, w_down, final_norm_g, loss_target, m_meta_tokens, m_attn_norm_g, m_w_in, m_attn_sinks, m_conv_w, m_conv_b, m_conv_ln_g, m_conv_ln_b, m_attn_out_g, m_conv_out_g, m_w_out, m_ffn_norm_g, m_w_gate, m_w_up, m_w_down, m_final_norm_g, v_meta_tokens, v_attn_norm_g, v_w_in, v_attn_sinks, v_conv_w, v_conv_b, v_conv_ln_g, v_conv_ln_b, v_attn_out_g, v_conv_out_g, v_w_out, v_ffn_norm_g, v_w_gate, v_w_up, v_w_down, v_final_norm_g):
    xi, yi, ci = _position()
    dev = jnp.reshape(_device_number((xi, yi, ci)), (1,)).astype(jnp.int32)
    core = jnp.reshape(ci, (1,)).astype(jnp.int32)
    slot = jnp.reshape(2 * xi + yi, (1,)).astype(jnp.int32)

    w_in_t, meta_st, convw_st = _allgather_params([w_in[0].T.astype(BF16)], [meta_tokens, conv_w[0]])
    meta_full = jnp.transpose(meta_st, (1, 0, 2)).reshape(N_META, D_MODEL)
    convw_full = jnp.transpose(convw_st, (1, 0, 2)).reshape(CONV_K, CONV_W)

    final_g = final_norm_g.reshape(1, D_MODEL)

    h0, q, kv, ca, cg = _inproj_fwd(x[0], meta_full, attn_norm_g, w_in_t)
    o_attn, lse, (wg_t,) = _attn_fwd(q, kv, attn_sinks, [w_gate[0].T.astype(BF16)])
    (o_conv, y_conv), (w_out_b, wd_b) = _conv_fwd(
        ca, cg, convw_full, conv_b, conv_ln_g, conv_ln_b, [w_out[0].astype(BF16), w_down[0].astype(BF16)])
    h1, (wu_t,) = _outproj_fwd(h0, o_attn, o_conv, attn_out_g, conv_out_g, w_out_b, [w_up[0].T.astype(BF16)])
    gate, up, act, dh2, loss_sum, dg_final = _ffn_fwd(h1, ffn_norm_g, wg_t, wu_t, wd_b, final_g, loss_target[0])

    def blocks(g):
        return g.reshape(4, 2, g.shape[0] // N_DEV, D_MODEL)

    def add_siblings(grads, received, tags):
        return [_add_sibling(g, r, core, "add_sibling_" + t) for g, r, t in zip(grads, received, tags)]

    dgate, dup, hn2, dh1, dg_ffn = _ffn_bwd(dh2, h1, ffn_norm_g, gate, up, wg_t, wu_t, wd_b)
    ffn_grads = [blocks(_wgrad(dgate, hn2, FF_CHUNK, "wgrad_gate")), blocks(_wgrad(dup, hn2, FF_CHUNK, "wgrad_up")),
                 blocks(_wgrad(act, dh2, FF_CHUNK, "wgrad_down"))]
    (do_attn, do_conv, mixed, dg_ao, dg_co), ffn_sib = _outproj_bwd(
        dh1, o_attn, o_conv, attn_out_g, conv_out_g, w_out_b, ffn_grads)
    ffn_sums = add_siblings(ffn_grads, ffn_sib, ("gate", "up", "down"))
    out_grads = [blocks(_wgrad(mixed, dh1, D_MODEL, "wgrad_out"))]
    (dca, dcg, dconvw, dconvb, dln_g, dln_b), gate_up_chips = _conv_bwd(
        do_conv, y_conv, ca, cg, convw_full, conv_ln_g, conv_ln_b, ffn_sums[:2])
    (dq, dkv_shifted, dkvm, dsinks), out_sib, down_chips = _attn_bwd(
        q, kv, attn_sinks, o_attn, lse, do_attn, out_grads, ffn_sums[2:])
    dkv = dkv_shifted[BLOCK:BLOCK + dq.shape[0]]
    ffn_chips = list(gate_up_chips) + list(down_chips)
    out_sums = add_siblings(out_grads, out_sib, ("out",))
    grad_x, dmeta, dproj, hn1, dg_attn = _inproj_bwd(dh1, h0, attn_norm_g, dq, dkv, dkvm, dca, dcg, w_in_t)
    dwi_t, out_chips = _wgrad(dproj, hn1, 1792, "wgrad_in", out_sums)
    in_grads = [blocks(dwi_t)]
    in_sums = add_siblings(in_grads, _reduce_siblings(in_grads), ("in",))
    small_a = jnp.concatenate([
        dg_attn, dg_ffn, dg_final, jnp.concatenate([dconvb, dln_g], axis=1), jnp.concatenate([dln_b, dg_ao], axis=1),
        jnp.concatenate([dg_co, dsinks, jnp.zeros((1, 512 - N_HEADS), F32)], axis=1),
        jnp.concatenate([loss_sum[0:1, :], jnp.zeros((1, D_MODEL - 128), F32)], axis=1),
        jnp.zeros((1, D_MODEL), F32)], axis=0)
    small_b = jnp.transpose(dmeta.reshape(N_META, N_DEV, 128), (1, 0, 2)).reshape(N_DEV * N_META, 128)
    small_c = jnp.transpose(dconvw.reshape(32, N_DEV, 64), (1, 0, 2)).reshape(N_DEV * 32, 64)
    in_chips, ga, gb, gc = _reduce_chips(in_sums, [small_a, small_b, small_c])
    tags = ("in", "out", "gate", "up", "down")
    chip_sums = in_sums + out_sums + ffn_sums
    from_chips = [in_chips] + list(out_chips) + list(ffn_chips)

    big = [(True, w_in, m_w_in, v_w_in), (False, w_out, m_w_out, v_w_out), (True, w_gate, m_w_gate, v_w_gate),
           (True, w_up, m_w_up, v_w_up), (False, w_down, m_w_down, v_w_down)]
    big_out = {}
    for t, p, r, (transposed, w, m, v) in zip(tags, chip_sums, from_chips, big):
        rows = (lambda a: jnp.transpose(a[0])) if transposed else (lambda a: a[0])
        back = (lambda a: jnp.transpose(a)[None]) if transposed else (lambda a: a[None])
        big_out[t] = [back(a) for a in _adamw(p, r, slot, rows(w), rows(m), rows(v), "adamw_" + t)]

    small_params = [
        (meta_tokens, m_meta_tokens, v_meta_tokens), (attn_norm_g, m_attn_norm_g, v_attn_norm_g),
        (attn_sinks, m_attn_sinks, v_attn_sinks), (conv_w[0], m_conv_w[0], v_conv_w[0]),
        (conv_b, m_conv_b, v_conv_b), (conv_ln_g, m_conv_ln_g, v_conv_ln_g), (conv_ln_b, m_conv_ln_b, v_conv_ln_b),
        (attn_out_g, m_attn_out_g, v_attn_out_g), (conv_out_g, m_conv_out_g, v_conv_out_g),
        (ffn_norm_g, m_ffn_norm_g, v_ffn_norm_g),
        (final_g, m_final_norm_g.reshape(1, D_MODEL), v_final_norm_g.reshape(1, D_MODEL))]
    sm, loss = _adamw_small(dev, ga, gb, gc, small_params)
    sm[3] = [a[None] for a in sm[3]]
    sm[10] = [a.reshape(D_MODEL) for a in sm[10]]

    per_param = [sm[0], sm[1], big_out["in"], sm[2], sm[3], sm[4], sm[5], sm[6], sm[7], sm[8], big_out["out"],
                 sm[9], big_out["gate"], big_out["up"], big_out["down"], sm[10]]
    loss = loss.reshape(())
    outs = [loss, grad_x[None]]
    for kind in range(4):
        outs += [p[kind] for p in per_param]
    return tuple(outs)
```

```python
import functools
import math

import jax
import jax.numpy as jnp
from jax import lax
from jax.experimental import pallas as pl
from jax.experimental.pallas import tpu as pltpu

F32, BF16 = jnp.float32, jnp.bfloat16
MESH = pl.DeviceIdType.MESH

D_MODEL = 1024
N_META = 16
BLOCK = 128
LEAD = BLOCK - N_META
HEAD_DIM = 64
N_HEADS = 8
GROUP = 4
ATTN_W = 512
KV_W = 128
CONV_W = 512
CONV_K = 31
HALO = 32
D_FF = 2816
FF_CHUNK = D_FF // 2
FF_SUB = [slice(s, s + 256) for s in range(0, D_FF, 256)]
N_DEV = 8
EPS = 1e-5
NEG = -1e30
TM = 640
TILE_BLOCKS = TM // BLOCK
AG_FORWARD_AT = 85
WGRAD_K_TILES = 5
CONV_ROWS = 32
VMEM_LIMIT = 56 * 1024 * 1024

ADAM_LR, ADAM_B1, ADAM_B2, ADAM_EPS, ADAM_WD, ADAM_STEP = 0.001, 0.9, 0.999, 1e-08, 0.01, 10

NT = (((1,), (1,)), ((), ()))
NN = (((1,), (0,)), ((), ()))
TN = (((0,), (0,)), ((), ()))


def _dot(a, b, dims):
    return lax.dot_general(a, b, dims, preferred_element_type=F32)


def _sigmoid(x):
    return 1.0 / (1.0 + jnp.exp(-x))


def _pcall(body, *, name, out_shape, grid=None, in_specs=None, out_specs=None, scratch_shapes=(),
           semantics=None, **kw):
    params = dict(vmem_limit_bytes=VMEM_LIMIT)
    if semantics is not None:
        params["dimension_semantics"] = semantics
    extra = {}
    if grid is not None:
        extra["grid"] = grid
    if in_specs is not None:
        extra["in_specs"] = in_specs
    if out_specs is not None:
        extra["out_specs"] = out_specs
    return pl.pallas_call(body, name=name, out_shape=out_shape, scratch_shapes=list(scratch_shapes),
                          compiler_params=pltpu.CompilerParams(**params), **extra, **kw)


def _rows(tm, cols):
    return pl.BlockSpec((tm, cols), lambda i, *_: (i, 0))


def _full(shape):
    nd = len(shape)
    return pl.BlockSpec(shape, lambda *_: (0,) * nd)


def _rms_stats(x):
    return lax.rsqrt(jnp.mean(x * x, axis=-1, keepdims=True) + EPS)


def _rms_bwd(dy, x, r, g):
    t = dy * g
    dx = r * (t - x * (r * r) * jnp.mean(t * x, axis=-1, keepdims=True))
    dg = jnp.sum(dy * x * r, axis=0, keepdims=True)
    return dx, dg


def _inproj_fwd(x, meta, g1, w_in_t):
    R = x.shape[0] + BLOCK
    nt = R // TM
    assert nt >= 2

    def body(x_hbm, meta_ref, g_ref, w_ref, h0_ref, q_ref, kv_ref, ca_ref, cg_ref, x_s, sems):
        i = pl.program_id(0)
        slot = i % 2

        def x_copy(step, slot, first):
            if first:
                return pltpu.make_async_copy(x_hbm.at[pl.ds(0, TM - BLOCK)],
                                             x_s.at[slot, pl.ds(BLOCK, TM - BLOCK)], sems.at[slot])
            return pltpu.make_async_copy(
                x_hbm.at[pl.ds(pl.multiple_of(step * TM - BLOCK, BLOCK), TM)], x_s.at[slot], sems.at[slot])

        @pl.when(i == 0)
        def _():
            x_copy(0, 0, True).start()
            x_s[0, 0:LEAD, :] = jnp.zeros((LEAD, D_MODEL), F32)
            x_s[0, LEAD:BLOCK, :] = meta_ref[...]

        pl.when(i + 1 < nt)(lambda: x_copy(i + 1, 1 - slot, False).start())
        pl.when(i == 0)(lambda: x_copy(0, 0, True).wait())
        pl.when(i > 0)(lambda: x_copy(i, slot, False).wait())
        h = x_s[slot]
        h0_ref[...] = h
        hn = (h * _rms_stats(h) * g_ref[...]).astype(BF16)
        q_ref[...] = _dot(hn, w_ref[0:512, :], NT).astype(BF16)
        kv_ref[...] = _dot(hn, w_ref[512:768, :], NT).astype(BF16)
        ca_ref[...] = _dot(hn, w_ref[768:1280, :], NT)
        cg_ref[...] = _dot(hn, w_ref[1280:1792, :], NT)

    return _pcall(
        body, name="inproj_fwd", grid=(nt,),
        in_specs=[pl.BlockSpec(memory_space=pl.ANY), _full((N_META, D_MODEL)), _full((1, D_MODEL)),
                  _full((1792, D_MODEL))],
        out_specs=[_rows(TM, D_MODEL), _rows(TM, 512), _rows(TM, 256), _rows(TM, 512), _rows(TM, 512)],
        out_shape=[jax.ShapeDtypeStruct((R, D_MODEL), F32),
                   jax.ShapeDtypeStruct((R, 512), BF16), jax.ShapeDtypeStruct((R, 256), BF16),
                   jax.ShapeDtypeStruct((R, 512), F32), jax.ShapeDtypeStruct((R, 512), F32)],
        scratch_shapes=[pltpu.VMEM((2, TM, D_MODEL), F32), pltpu.SemaphoreType.DMA((2,))],
        semantics=("arbitrary",),
    )(x, meta, g1, w_in_t)


GB = GROUP * BLOCK
ATTN_SCALE = 1.0 / math.sqrt(HEAD_DIM)


def _group_lanes(xt, g):
    return jnp.concatenate(
        [xt[HEAD_DIM * (GROUP * g + j):HEAD_DIM * (GROUP * g + j + 1), :] for j in range(GROUP)], axis=1)


def _head_lanes(ref, g):
    return jnp.concatenate([ref[GROUP * g + j:GROUP * g + j + 1, :] for j in range(GROUP)], axis=1)


def _head_rows(xs):
    return jnp.concatenate([x[:, BLOCK * j:BLOCK * (j + 1)] for x in xs for j in range(GROUP)], axis=0)


def _attn_tables(sink_ref, bias_s, sink_s):
    kk = lax.broadcasted_iota(jnp.int32, (BLOCK, BLOCK), 0)
    ii = lax.broadcasted_iota(jnp.int32, (BLOCK, BLOCK), 1)
    dist = jnp.where(kk <= ii, ii - kk, ii - kk + BLOCK).astype(F32)
    for h in range(N_HEADS):
        bias_s[:, BLOCK * h:BLOCK * (h + 1)] = dist * -(2.0 ** -(h + 1))
        sink_s[:, BLOCK * h:BLOCK * (h + 1)] = jnp.zeros((1, BLOCK), F32) + sink_ref[0, h]


def _attn_masks(b):
    kk = lax.broadcasted_iota(jnp.int32, (BLOCK, GB), 0)
    ii = lax.broadcasted_iota(jnp.int32, (BLOCK, GB), 1) & (BLOCK - 1)
    sel = kk <= ii
    pen = jnp.where(sel, jnp.where(b >= 1, 0.0, NEG), jnp.where(b >= 2, 0.0, NEG))
    mj = lax.broadcasted_iota(jnp.int32, (N_META, GB), 0)
    mi = lax.broadcasted_iota(jnp.int32, (N_META, GB), 1) & (BLOCK - 1)
    pen_m = jnp.where((mj + LEAD) <= (mi + b * BLOCK), 0.0, NEG)
    return sel, pen, pen_m


def _attn_scores(qt, kc, kp, km, sel, pen, pen_m, bias):
    s_b = jnp.where(sel, _dot(kc, qt, NN), _dot(kp, qt, NN)) + bias + pen
    s_m = _dot(km, qt, NN) + pen_m
    return s_b, s_m


def _attn_fwd(q, kv, sinks, shards):
    R = q.shape[0]
    nt = R // TM
    ns = len(shards)

    def body(sink_ref, q_ref, kv_ref, kvh_ref, kvm_ref, *refs):
        ag_ins, (o_ref, lse_ref), ag_outs = refs[:ns], refs[ns:ns + 2], refs[ns + 2:2 * ns + 2]
        ag_sems, (bias_s, sink_s) = refs[2 * ns + 2:2 * ns + 5], refs[2 * ns + 5:]
        i = pl.program_id(0)
        ag_finish = _carried_allgather(i, nt, shards, ag_ins + ag_outs + ag_sems)
        pl.when(i == 0)(functools.partial(_attn_tables, sink_ref, bias_s, sink_s))
        for s in range(TILE_BLOCKS):
            rows = slice(BLOCK * s, BLOCK * (s + 1))
            kvp = kvh_ref[...] if s == 0 else kv_ref[BLOCK * (s - 1):BLOCK * s, :]
            sel, pen, pen_m = _attn_masks(i * TILE_BLOCKS + s)
            q_t = (q_ref[rows, :] * ATTN_SCALE).T
            kvc_t, kvp_t = kv_ref[rows, :].T, kvp.T
            outs = []
            for g in range(N_HEADS // GROUP):
                ks = slice(HEAD_DIM * g, HEAD_DIM * (g + 1))
                vs = slice(KV_W + HEAD_DIM * g, KV_W + HEAD_DIM * (g + 1))
                lanes = slice(GB * g, GB * (g + 1))
                s_b, s_m = _attn_scores(_group_lanes(q_t, g), kv_ref[rows, ks], kvp[:, ks], kvm_ref[LEAD:BLOCK, ks],
                                        sel, pen, pen_m, bias_s[:, lanes])
                sink = sink_s[:, lanes]
                m = jnp.maximum(jnp.maximum(jnp.max(s_b, axis=0, keepdims=True),
                                            jnp.max(s_m, axis=0, keepdims=True)), sink)
                p_b = jnp.exp(s_b - m)
                p_m = jnp.exp(s_m - m)
                l = jnp.sum(p_b, axis=0, keepdims=True) + jnp.sum(p_m, axis=0, keepdims=True) + jnp.exp(sink - m)
                p_c = jnp.where(sel, p_b, 0.0).astype(BF16)
                p_p = jnp.where(sel, 0.0, p_b).astype(BF16)
                o_t = (_dot(kvc_t[vs, :], p_c, NN) + _dot(kvp_t[vs, :], p_p, NN)
                       + _dot(kvm_ref[LEAD:BLOCK, vs], p_m.astype(BF16), TN))
                outs.append(o_t / l)
                lse = m + jnp.log(l)
                for j in range(GROUP):
                    lse_ref[GROUP * g + j:GROUP * g + j + 1, rows] = lse[:, BLOCK * j:BLOCK * (j + 1)]
            o_ref[rows, :] = _head_rows(outs).T
        ag_finish()

    res = _pcall(
        body, name="attn_fwd", grid=(nt,),
        in_specs=[pl.BlockSpec(memory_space=pltpu.SMEM),
                  _rows(TM, 512), _rows(TM, 256),
                  pl.BlockSpec((BLOCK, 256), lambda i: (jnp.maximum(i * TILE_BLOCKS - 1, 0), 0)),
                  _full((BLOCK, 256))] + [ANY] * ns,
        out_specs=[_rows(TM, 512), pl.BlockSpec((N_HEADS, TM), lambda i: (0, i))] + [ANY] * ns,
        out_shape=[jax.ShapeDtypeStruct((R, 512), F32), jax.ShapeDtypeStruct((N_HEADS, R), F32)]
        + _gathered_shapes(shards),
        scratch_shapes=_allgather_sems(ns) + [pltpu.VMEM((BLOCK, N_HEADS * BLOCK), F32),
                                              pltpu.VMEM((1, N_HEADS * BLOCK), F32)],
        semantics=("arbitrary",),
    )(sinks, q, kv, kv, kv, *shards)
    return res[0], res[1], res[2:]


def _ln_silu(y, lg, lb):
    mu = jnp.mean(y, axis=-1, keepdims=True)
    xc = y - mu
    rstd = lax.rsqrt(jnp.mean(xc * xc, axis=-1, keepdims=True) + EPS)
    xhat = xc * rstd
    yn = xhat * lg + lb
    return yn, xhat, rstd


PHASE_ROWS = HALO + TM - 8


def _phase_scratch():
    return pltpu.VMEM((7, PHASE_ROWS, CONV_W), F32)


def _phase_copies(src_s, ph_s):
    for b in range(1, 8):
        ph_s[b - 1] = src_s[pl.ds(b, PHASE_ROWS), :]


def _shifted(src_s, ph_s, start, rows):
    a8, b = (start // 8) * 8, start % 8
    if b == 0:
        return src_s[pl.ds(a8, rows), :]
    return ph_s[b - 1, pl.ds(a8, rows), :]


def _conv_fwd(ca, cg, conv_w, conv_b, ln_g, ln_b, shards):
    R = ca.shape[0]
    nt = R // TM
    hpt = TM // HALO
    ns = len(shards)

    def body(ca_ref, cg_ref, cah_ref, cgh_ref, w_ref, b_ref, lg_ref, lb_ref, *refs):
        ag_ins, (oc_ref, y_ref), ag_outs = refs[:ns], refs[ns:ns + 2], refs[ns + 2:2 * ns + 2]
        ag_sems, (u_s, uph_s) = refs[2 * ns + 2:2 * ns + 5], refs[2 * ns + 5:]
        i = pl.program_id(0)
        ag_finish = _carried_allgather(i, nt, shards, ag_ins + ag_outs + ag_sems)
        u_s[HALO:HALO + TM, :] = ca_ref[...] * _sigmoid(cg_ref[...])
        u_s[0:HALO, :] = jnp.where(i > 0, cah_ref[...] * _sigmoid(cgh_ref[...]), 0.0)
        _phase_copies(u_s, uph_s)
        for rc in range(TM // CONV_ROWS):
            base = rc * CONV_ROWS + HALO - (CONV_K - 1)
            acc = jnp.zeros((CONV_ROWS, CONV_W), F32) + b_ref[...]
            for k in range(CONV_K):
                acc = acc + _shifted(u_s, uph_s, base + k, CONV_ROWS) * w_ref[k:k + 1, :]
            rows = slice(rc * CONV_ROWS, (rc + 1) * CONV_ROWS)
            y_ref[rows, :] = acc
            yn, _, _ = _ln_silu(acc, lg_ref[...], lb_ref[...])
            oc_ref[rows, :] = yn * _sigmoid(yn)
        ag_finish()

    prev_halo = pl.BlockSpec((HALO, CONV_W), lambda i: (jnp.maximum(i * hpt - 1, 0), 0))
    anywhere = pl.BlockSpec(memory_space=pl.ANY)
    res = _pcall(
        body, name="conv_fwd", grid=(nt,),
        in_specs=[_rows(TM, CONV_W), _rows(TM, CONV_W), prev_halo, prev_halo,
                  _full((CONV_K, CONV_W)), _full((1, CONV_W)), _full((1, CONV_W)), _full((1, CONV_W))]
        + [anywhere] * ns,
        out_specs=[_rows(TM, CONV_W), _rows(TM, CONV_W)] + [anywhere] * ns,
        out_shape=[jax.ShapeDtypeStruct((R, CONV_W), F32), jax.ShapeDtypeStruct((R, CONV_W), F32)]
        + _gathered_shapes(shards),
        scratch_shapes=_allgather_sems(ns) + [pltpu.VMEM((HALO + TM, CONV_W), F32), _phase_scratch()],
        semantics=("arbitrary",),
    )(ca, cg, ca, cg, conv_w, conv_b, ln_g, ln_b, *shards)
    return res[:2], res[2:]


def _outproj_fwd(h0, o_attn, o_conv, ga, gc, w_out):
    R = h0.shape[0]

    def body(h_ref, oa_ref, oc_ref, ga_ref, gc_ref, w_ref, h1_ref):
        oa, oc = oa_ref[...], oc_ref[...]
        ma = (oa * _rms_stats(oa) * ga_ref[...]).astype(BF16)
        mc = (oc * _rms_stats(oc) * gc_ref[...]).astype(BF16)
        h1_ref[...] = h_ref[...] + _dot(ma, w_ref[0:512, :], NN) + _dot(mc, w_ref[512:1024, :], NN)

    return _pcall(
        body, name="outproj_fwd", grid=(R // TM,),
        in_specs=[_rows(TM, D_MODEL), _rows(TM, 512), _rows(TM, 512), _full((1, 512)), _full((1, 512)),
                  _full((D_MODEL, D_MODEL))],
        out_specs=_rows(TM, D_MODEL),
        out_shape=jax.ShapeDtypeStruct((R, D_MODEL), F32),
        semantics=("parallel",),
    )(h0, o_attn, o_conv, ga, gc, w_out)


def _target_copy(tgt_hbm, tgt_s, sem, i, first):
    if first:
        return pltpu.make_async_copy(tgt_hbm.at[pl.ds(0, TM - BLOCK)], tgt_s.at[pl.ds(BLOCK, TM - BLOCK)], sem)
    return pltpu.make_async_copy(tgt_hbm.at[pl.ds(i * TM - BLOCK, TM)], tgt_s, sem)


def _resident(shape):
    nd = len(shape)
    return pl.BlockSpec(shape, lambda *_: (0,) * nd, pipeline_mode=pl.Buffered(1))


def _ffn_fwd(h1, g2, wg_t, wu_t, wd, gf, target):
    R = h1.shape[0]
    nt = R // TM

    def body(h1_ref, g2_ref, wg_ref, wu_ref, wd_ref, gf_ref, tgt_hbm,
             gate_ref, up_ref, act_s, dh2_ref, loss_ref, dgf_ref, tgt_s, sem):
        i = pl.program_id(0)

        @pl.when(i == 0)
        def _():
            loss_ref[...] = jnp.zeros_like(loss_ref)
            dgf_ref[...] = jnp.zeros_like(dgf_ref)
            tgt_s[0:BLOCK, :] = jnp.zeros((BLOCK, D_MODEL), F32)
            _target_copy(tgt_hbm, tgt_s, sem, i, True).start()

        pl.when(i > 0)(lambda: _target_copy(tgt_hbm, tgt_s, sem, i, False).start())
        h1 = h1_ref[...]
        hn = (h1 * _rms_stats(h1) * g2_ref[...]).astype(BF16)
        for cs in FF_SUB:
            gate = _dot(hn, wg_ref[cs, :], NT)
            up = _dot(hn, wu_ref[cs, :], NT)
            gate_ref[:, cs] = gate.astype(BF16)
            up_ref[:, cs] = up.astype(BF16)
            act_s[:, cs] = (gate * _sigmoid(gate) * up).astype(BF16)
        part = _dot(act_s[...], wd_ref[...], NN)
        pl.when(i == 0)(lambda: _target_copy(tgt_hbm, tgt_s, sem, i, True).wait())
        pl.when(i > 0)(lambda: _target_copy(tgt_hbm, tgt_s, sem, i, False).wait())
        h2 = h1 + part
        rf = _rms_stats(h2)
        gf = gf_ref[...]
        row = lax.broadcasted_iota(jnp.int32, (TM, 1), 0) + i * TM
        err = jnp.where(row >= BLOCK, h2 * rf * gf - tgt_s[...], 0.0)
        dy = err * (1.0 / D_MODEL)
        dh2, dgf = _rms_bwd(dy, h2, rf, gf)
        dh2_ref[...] = dh2
        loss_ref[...] += (0.5 / D_MODEL) * jnp.sum(err * err)
        dgf_ref[...] += dgf

    wspec = _resident((D_FF, D_MODEL))
    return _pcall(
        body, name="ffn_fwd", grid=(nt,),
        in_specs=[_rows(TM, D_MODEL), _full((1, D_MODEL)), wspec, wspec, wspec, _full((1, D_MODEL)),
                  pl.BlockSpec(memory_space=pl.ANY)],
        out_specs=[_rows(TM, D_FF), _rows(TM, D_FF), _rows(TM, D_FF), _rows(TM, D_MODEL), _full((8, 128)),
                   _full((1, D_MODEL))],
        out_shape=[jax.ShapeDtypeStruct((R, D_FF), BF16)] * 3
        + [jax.ShapeDtypeStruct((R, D_MODEL), F32),
           jax.ShapeDtypeStruct((8, 128), F32), jax.ShapeDtypeStruct((1, D_MODEL), F32)],
        scratch_shapes=[pltpu.VMEM((TM, D_MODEL), F32), pltpu.SemaphoreType.DMA],
        semantics=("arbitrary",),
    )(h1, g2, wg_t, wu_t, wd, gf, target)


def _ffn_bwd(dh2, h1, g2, gate, up, wg_t, wu_t, wd):
    R = h1.shape[0]
    nt = R // TM
    act_shape = jax.ShapeDtypeStruct((R, D_FF), BF16)

    def act_body(dh2_ref, gate_ref, up_ref, wd_ref, dgate_ref, dup_ref):
        dhb = dh2_ref[...].astype(BF16)
        for cs in FF_SUB:
            dact = _dot(dhb, wd_ref[cs, :], NT)
            gate = gate_ref[:, cs].astype(F32)
            up = up_ref[:, cs].astype(F32)
            sig = _sigmoid(gate)
            dgate_ref[:, cs] = (dact * up * (sig * (1.0 + gate * (1.0 - sig)))).astype(BF16)
            dup_ref[:, cs] = (dact * (gate * sig)).astype(BF16)

    dgate, dup = _pcall(
        act_body, name="ffn_bwd_act", grid=(nt,),
        in_specs=[_rows(TM, D_MODEL), _rows(TM, D_FF), _rows(TM, D_FF), _resident((D_FF, D_MODEL))],
        out_specs=[_rows(TM, D_FF), _rows(TM, D_FF)], out_shape=[act_shape, act_shape],
        semantics=("parallel",),
    )(dh2, gate, up, wd)

    def in_body(dh2_ref, h1_ref, g2_ref, dgate_ref, dup_ref, wg_ref, wu_ref, hn_ref, dh1_ref, dg2_ref):
        @pl.when(pl.program_id(0) == 0)
        def _():
            dg2_ref[...] = jnp.zeros_like(dg2_ref)

        dhn = _dot(dgate_ref[...], wg_ref[...], NN) + _dot(dup_ref[...], wu_ref[...], NN)
        h1 = h1_ref[...]
        r = _rms_stats(h1)
        g2 = g2_ref[...]
        hn_ref[...] = (h1 * r * g2).astype(BF16)
        dx, dg = _rms_bwd(dhn, h1, r, g2)
        dh1_ref[...] = dh2_ref[...] + dx
        dg2_ref[...] += dg

    hn2, dh1, dg2 = _pcall(
        in_body, name="ffn_bwd_in", grid=(nt,),
        in_specs=[_rows(TM, D_MODEL), _rows(TM, D_MODEL), _full((1, D_MODEL)), _rows(TM, D_FF), _rows(TM, D_FF),
                  _resident((D_FF, D_MODEL)), _resident((D_FF, D_MODEL))],
        out_specs=[_rows(TM, D_MODEL), _rows(TM, D_MODEL), _full((1, D_MODEL))],
        out_shape=[jax.ShapeDtypeStruct((R, D_MODEL), BF16), jax.ShapeDtypeStruct((R, D_MODEL), F32),
                   jax.ShapeDtypeStruct((1, D_MODEL), F32)],
        semantics=("arbitrary",),
    )(dh2, h1, g2, dgate, dup, wg_t, wu_t)
    return dgate, dup, hn2, dh1, dg2


def _wgrad(a, b, tm, name, partials=()):
    K, M = a.shape
    N = b.shape[1]
    tk = K // WGRAD_K_TILES if K % (WGRAD_K_TILES * BLOCK) == 0 else TM
    nm, nk, npart = M // tm, K // tk, len(partials)

    def body(a_ref, b_ref, *refs):
        p_ins, o_ref, p_outs, sems = refs[:npart], refs[npart], refs[npart + 1:2 * npart + 1], refs[2 * npart + 1:]
        step = pl.program_id(0) * nk + pl.program_id(1)
        exchange = functools.partial(_chip_copies, p_ins, p_outs, *sems)
        if npart:
            _hosted(step, exchange)

        @pl.when(pl.program_id(1) == 0)
        def _():
            o_ref[...] = jnp.zeros_like(o_ref)

        o_ref[...] += _dot(a_ref[...], b_ref[...].astype(BF16), TN)
        if npart:
            _hosted_wait(step, nm * nk, exchange)

    res = _pcall(
        body, name=name, grid=(nm, nk),
        in_specs=[pl.BlockSpec((tk, tm), lambda m, k: (k, m)), pl.BlockSpec((tk, N), lambda m, k: (k, 0))]
        + [ANY] * npart,
        out_specs=[pl.BlockSpec((tm, N), lambda m, k: (m, 0))] + [ANY] * npart,
        out_shape=[jax.ShapeDtypeStruct((M, N), F32)] + _chip_shapes(partials),
        scratch_shapes=_sem_pair(3 * npart) if npart else [],
        semantics=("arbitrary", "arbitrary"),
    )(a, b, *partials)
    return (res[0], res[1:]) if npart else res[0]


def _outproj_bwd(dh1, o_attn, o_conv, ga, gc, w_out, grads):
    R = dh1.shape[0]
    nt, ng = R // TM, len(grads)

    def body(dh1_ref, oa_ref, oc_ref, ga_ref, gc_ref, w_ref, *refs):
        g_ins, (doa_ref, doc_ref, mixed_ref, dga_ref, dgc_ref) = refs[:ng], refs[ng:ng + 5]
        g_outs, (send_sems, recv_sems) = refs[ng + 5:2 * ng + 5], refs[2 * ng + 5:]
        exchange = functools.partial(_sibling_copies, g_ins, g_outs, send_sems, recv_sems)
        _hosted(pl.program_id(0), exchange)

        @pl.when(pl.program_id(0) == 0)
        def _():
            dga_ref[...] = jnp.zeros_like(dga_ref)
            dgc_ref[...] = jnp.zeros_like(dgc_ref)

        dm = _dot(dh1_ref[...].astype(BF16), w_ref[...], NT)
        oa, oc = oa_ref[...], oc_ref[...]
        ra, rc = _rms_stats(oa), _rms_stats(oc)
        mixed_ref[:, 0:512] = (oa * ra * ga_ref[...]).astype(BF16)
        mixed_ref[:, 512:1024] = (oc * rc * gc_ref[...]).astype(BF16)
        doa, dga = _rms_bwd(dm[:, 0:512], oa, ra, ga_ref[...])
        doc, dgc = _rms_bwd(dm[:, 512:1024], oc, rc, gc_ref[...])
        doa_ref[...] = doa
        doc_ref[...] = doc
        dga_ref[...] += dga
        dgc_ref[...] += dgc
        _hosted_wait(pl.program_id(0), nt, exchange)

    res = _pcall(
        body, name="outproj_bwd", grid=(nt,),
        in_specs=[_rows(TM, D_MODEL), _rows(TM, 512), _rows(TM, 512), _full((1, 512)), _full((1, 512)),
                  _full((D_MODEL, D_MODEL))] + [ANY] * ng,
        out_specs=[_rows(TM, 512), _rows(TM, 512), _rows(TM, D_MODEL), _full((1, 512)), _full((1, 512))]
        + [ANY] * ng,
        out_shape=[jax.ShapeDtypeStruct((R, 512), F32), jax.ShapeDtypeStruct((R, 512), F32),
                   jax.ShapeDtypeStruct((R, D_MODEL), BF16),
                   jax.ShapeDtypeStruct((1, 512), F32), jax.ShapeDtypeStruct((1, 512), F32)]
        + _sibling_shapes(grads),
        scratch_shapes=_sem_pair(ng),
        semantics=("arbitrary",),
    )(dh1, o_attn, o_conv, ga, gc, w_out, *grads)
    return res[:5], res[5:]


def _conv_bwd(do_conv, y, ca, cg, conv_w, ln_g, ln_b, partials):
    R = ca.shape[0]
    nt = R // TM
    hpt = TM // HALO
    npart = len(partials)

    def body(do_ref, doh_ref, y_ref, yh_ref, ca_ref, cg_ref, cah_ref, cgh_ref, w_ref, lg_ref, lb_ref, *refs):
        p_ins, (dca_ref, dcg_ref, dw_ref, db_ref, dlg_ref, dlb_ref) = refs[:npart], refs[npart:npart + 6]
        p_outs, (send_sems, recv_sems, u_s, dy_s, uph_s, dyph_s) = refs[npart + 6:2 * npart + 6], refs[2 * npart + 6:]
        i = pl.program_id(0)
        exchange = functools.partial(_chip_copies, p_ins, p_outs, send_sems, recv_sems)
        _hosted(i, exchange)

        @pl.when(i == 0)
        def _():
            dw_ref[...] = jnp.zeros_like(dw_ref)
            db_ref[...] = jnp.zeros_like(db_ref)
            dlg_ref[...] = jnp.zeros_like(dlg_ref)
            dlb_ref[...] = jnp.zeros_like(dlb_ref)

        lg, lb = lg_ref[...], lb_ref[...]

        def ln_bwd(yv, dov):
            yn, xhat, rstd = _ln_silu(yv, lg, lb)
            sig = _sigmoid(yn)
            dyn = dov * (sig * (1.0 + yn * (1.0 - sig)))
            dxh = dyn * lg
            dyv = rstd * (dxh - jnp.mean(dxh, axis=-1, keepdims=True)
                          - xhat * jnp.mean(dxh * xhat, axis=-1, keepdims=True))
            return dyv, dyn, xhat

        dyv, dyn, xhat = ln_bwd(y_ref[...], do_ref[...])
        dy_s[0:TM, :] = dyv
        dlg_ref[...] += jnp.sum(dyn * xhat, axis=0, keepdims=True)
        dlb_ref[...] += jnp.sum(dyn, axis=0, keepdims=True)
        db_ref[...] += jnp.sum(dyv, axis=0, keepdims=True)
        dyh, _, _ = ln_bwd(yh_ref[...], doh_ref[...])
        dy_s[TM:TM + HALO, :] = jnp.where(i < nt - 1, dyh, 0.0)
        u_s[HALO:HALO + TM, :] = ca_ref[...] * _sigmoid(cg_ref[...])
        u_s[0:HALO, :] = jnp.where(i > 0, cah_ref[...] * _sigmoid(cgh_ref[...]), 0.0)
        _phase_copies(dy_s, dyph_s)
        _phase_copies(u_s, uph_s)

        for rc in range(TM // CONV_ROWS):
            acc = jnp.zeros((CONV_ROWS, CONV_W), F32)
            for k in range(CONV_K):
                acc = acc + _shifted(dy_s, dyph_s, rc * CONV_ROWS + CONV_K - 1 - k, CONV_ROWS) * w_ref[k:k + 1, :]
            rows = slice(rc * CONV_ROWS, (rc + 1) * CONV_ROWS)
            sg = _sigmoid(cg_ref[rows, :])
            dca_ref[rows, :] = (acc * sg).astype(BF16)
            dcg_ref[rows, :] = (acc * ca_ref[rows, :] * sg * (1.0 - sg)).astype(BF16)

        for k in range(CONV_K):
            prod = _shifted(u_s, uph_s, HALO - (CONV_K - 1) + k, TM) * dy_s[0:TM, :]
            dw_ref[k:k + 1, :] += jnp.sum(prod, axis=0, keepdims=True)
        _hosted_wait(i, nt, exchange)

    prev_halo = pl.BlockSpec((HALO, CONV_W), lambda i: (jnp.maximum(i * hpt - 1, 0), 0))
    next_halo = pl.BlockSpec((HALO, CONV_W), lambda i: (jnp.minimum((i + 1) * hpt, nt * hpt - 1), 0))
    vec = jax.ShapeDtypeStruct((1, CONV_W), F32)
    res = _pcall(
        body, name="conv_bwd", grid=(nt,),
        in_specs=[_rows(TM, CONV_W), next_halo, _rows(TM, CONV_W), next_halo,
                  _rows(TM, CONV_W), _rows(TM, CONV_W), prev_halo, prev_halo,
                  _full((CONV_K, CONV_W)), _full((1, CONV_W)), _full((1, CONV_W))] + [ANY] * npart,
        out_specs=[_rows(TM, CONV_W), _rows(TM, CONV_W), _full((32, CONV_W)),
                   _full((1, CONV_W)), _full((1, CONV_W)), _full((1, CONV_W))] + [ANY] * npart,
        out_shape=[jax.ShapeDtypeStruct((R, CONV_W), BF16), jax.ShapeDtypeStruct((R, CONV_W), BF16),
                   jax.ShapeDtypeStruct((32, CONV_W), F32), vec, vec, vec] + _chip_shapes(partials),
        scratch_shapes=_sem_pair(3 * npart)
        + [pltpu.VMEM((HALO + TM, CONV_W), F32), pltpu.VMEM((TM + HALO, CONV_W), F32),
           _phase_scratch(), _phase_scratch()],
        semantics=("arbitrary",),
    )(do_conv, do_conv, y, y, ca, cg, ca, cg, conv_w, ln_g, ln_b, *partials)
    return res[:6], res[6:]


def _attn_bwd(q, kv, sinks, o, lse, do, grads, partials):
    R = q.shape[0]
    nt = R // TM
    ng, npart = len(grads), len(partials)
    nx = ng + npart

    def body(sink_ref, q_ref, kv_ref, kvh_ref, kvm_ref, o_ref, lse_ref, do_ref, *refs):
        x_ins, (dq_ref, dkv_ref, dkvm_ref, dsink_ref) = refs[:nx], refs[nx:nx + 4]
        x_outs = refs[nx + 4:2 * nx + 4]
        g_send, g_recv, p_send, p_recv, carry_s, bias_s, sink_s = refs[2 * nx + 4:]
        i = pl.program_id(0)

        def exchange():
            return (_sibling_copies(x_ins[:ng], x_outs[:ng], g_send, g_recv)
                    + _chip_copies(x_ins[ng:], x_outs[ng:], p_send, p_recv))

        _hosted(i, exchange)

        @pl.when(i == 0)
        def _():
            dkvm_ref[...] = jnp.zeros_like(dkvm_ref)
            carry_s[...] = jnp.zeros_like(carry_s)
            for h in range(N_HEADS):
                dsink_ref[0, h] = 0.0
            _attn_tables(sink_ref, bias_s, sink_s)

        @pl.when(i < nt)
        def _():
            head_of = lax.broadcasted_iota(jnp.int32, (N_HEADS, ATTN_W), 1) // HEAD_DIM
            ind = (head_of == lax.broadcasted_iota(jnp.int32, (N_HEADS, ATTN_W), 0)).astype(BF16)
            dkm = [jnp.zeros((N_META, 2 * KV_W), F32)]
            dsink = [0.0] * N_HEADS
            prev_cur = carry_s[...]
            for s in range(TILE_BLOCKS):
                rows = slice(BLOCK * s, BLOCK * (s + 1))
                kvp = kvh_ref[...] if s == 0 else kv_ref[BLOCK * (s - 1):BLOCK * s, :]
                sel, pen, pen_m = _attn_masks(i * TILE_BLOCKS + s)
                q_t = (q_ref[rows, :] * ATTN_SCALE).T
                do_t = do_ref[rows, :].astype(BF16).T
                kvc_t, kvp_t = kv_ref[rows, :].T, kvp.T
                prod = do_ref[rows, :] * o_ref[rows, :]
                hi = prod.astype(BF16)
                lo = (prod - hi.astype(F32)).astype(BF16)
                delta8 = _dot(ind, hi, NT) + _dot(ind, lo, NT)
                lse8 = lse_ref[:, rows]
                dqs, cur, prev, meta = [], [None] * 4, [None] * 4, [None] * 4
                for g in range(N_HEADS // GROUP):
                    ks = slice(HEAD_DIM * g, HEAD_DIM * (g + 1))
                    vs = slice(KV_W + HEAD_DIM * g, KV_W + HEAD_DIM * (g + 1))
                    lanes = slice(GB * g, GB * (g + 1))
                    qg, dog = _group_lanes(q_t, g), _group_lanes(do_t, g)
                    kc, kp, km = kv_ref[rows, ks], kvp[:, ks], kvm_ref[LEAD:BLOCK, ks]
                    vc, vp, vm = kv_ref[rows, vs], kvp[:, vs], kvm_ref[LEAD:BLOCK, vs]
                    s_b, s_m = _attn_scores(qg, kc, kp, km, sel, pen, pen_m, bias_s[:, lanes])
                    lse, delta = _head_lanes(lse8, g), _head_lanes(delta8, g)
                    p_b = jnp.exp(s_b - lse)
                    p_m = jnp.exp(s_m - lse)
                    dp_b = jnp.where(sel, _dot(vc, dog, NN), _dot(vp, dog, NN))
                    ds_b = p_b * (dp_b - delta)
                    ds_m = (p_m * (_dot(vm, dog, NN) - delta)).astype(BF16)
                    dsk = jnp.exp(sink_s[:, lanes] - lse) * delta
                    for j in range(GROUP):
                        dsink[GROUP * g + j] = dsink[GROUP * g + j] - jnp.sum(dsk[:, BLOCK * j:BLOCK * (j + 1)])
                    ds_c = jnp.where(sel, ds_b, 0.0).astype(BF16)
                    ds_p = jnp.where(sel, 0.0, ds_b).astype(BF16)
                    p_c = jnp.where(sel, p_b, 0.0).astype(BF16)
                    p_p = jnp.where(sel, 0.0, p_b).astype(BF16)
                    dqs.append((_dot(kvc_t[ks, :], ds_c, NN) + _dot(kvp_t[ks, :], ds_p, NN)
                                + _dot(km, ds_m, TN)) * ATTN_SCALE)
                    cur[g], cur[2 + g] = _dot(ds_c, qg, NT), _dot(p_c, dog, NT)
                    prev[g], prev[2 + g] = _dot(ds_p, qg, NT), _dot(p_p, dog, NT)
                    meta[g], meta[2 + g] = _dot(ds_m, qg, NT), _dot(p_m.astype(BF16), dog, NT)
                dq_ref[rows, :] = _head_rows(dqs).astype(BF16).T
                dkv_ref[rows, :] = (prev_cur + jnp.concatenate(prev, axis=1)).astype(BF16)
                prev_cur = jnp.concatenate(cur, axis=1)
                dkm.append(jnp.concatenate(meta, axis=1))
            carry_s[...] = prev_cur
            dkvm_ref[...] += functools.reduce(lambda a, b: a + b, dkm)
            for h in range(N_HEADS):
                dsink_ref[0, h] += dsink[h]

        @pl.when(i == nt)
        def _():
            dkv_ref[...] = jnp.zeros_like(dkv_ref)
            dkv_ref[0:BLOCK, :] = carry_s[...].astype(BF16)

        _hosted_wait(i, nt + 1, exchange)

    tile = lambda cols: pl.BlockSpec((TM, cols), lambda i: (jnp.minimum(i, nt - 1), 0))
    res = _pcall(
        body, name="attn_bwd", grid=(nt + 1,),
        in_specs=[pl.BlockSpec(memory_space=pltpu.SMEM), tile(512), tile(256),
                  pl.BlockSpec((BLOCK, 256), lambda i: (jnp.clip(i * TILE_BLOCKS - 1, 0, R // BLOCK - 1), 0)),
                  _full((BLOCK, 256)), tile(512),
                  pl.BlockSpec((N_HEADS, TM), lambda i: (0, jnp.minimum(i, nt - 1))), tile(512)]
        + [ANY] * nx,
        out_specs=[tile(512), _rows(TM, 256), _full((N_META, 256)), pl.BlockSpec(memory_space=pltpu.SMEM)]
        + [ANY] * nx,
        out_shape=[jax.ShapeDtypeStruct((R, 512), BF16), jax.ShapeDtypeStruct((R + TM, 256), BF16),
                   jax.ShapeDtypeStruct((N_META, 256), F32), jax.ShapeDtypeStruct((1, N_HEADS), F32)]
        + _sibling_shapes(grads) + _chip_shapes(partials),
        scratch_shapes=_sem_pair(ng) + _sem_pair(3 * npart)
        + [pltpu.VMEM((BLOCK, 256), F32), pltpu.VMEM((BLOCK, N_HEADS * BLOCK), F32),
           pltpu.VMEM((1, N_HEADS * BLOCK), F32)],
        semantics=("arbitrary",),
    )(sinks, q, kv, kv, kv, o, lse, do, *grads, *partials)
    return res[:4], res[4:4 + ng], res[4 + ng:]


def _inproj_bwd(dh1, h0, g1, dq, dkv, dkvm, dca, dcg, w_in_t):
    R = h0.shape[0]
    nt = R // TM
    assert nt >= 2

    def body(dh1_ref, h0_ref, g_ref, dq_ref, dkv_ref, dkvm_ref, dca_ref, dcg_ref, w_ref,
             gx_hbm, dmeta_ref, dproj_ref, hn_ref, dg_ref, dx_s, gx_sems):
        i = pl.program_id(0)

        def gx_copy(step, slot, first):
            if first:
                return pltpu.make_async_copy(dx_s.at[slot, pl.ds(BLOCK, TM - BLOCK)],
                                             gx_hbm.at[pl.ds(0, TM - BLOCK)], gx_sems.at[slot])
            return pltpu.make_async_copy(
                dx_s.at[slot], gx_hbm.at[pl.ds(pl.multiple_of(step * TM - BLOCK, BLOCK), TM)], gx_sems.at[slot])

        @pl.when(i == 0)
        def _():
            dg_ref[...] = jnp.zeros_like(dg_ref)

        dproj_ref[:, 0:512] = dq_ref[...]
        dproj_ref[:, 512:768] = dkv_ref[...]
        dproj_ref[:, 768:1280] = dca_ref[...]
        dproj_ref[:, 1280:1792] = dcg_ref[...]

        @pl.when(i == 0)
        def _():
            dproj_ref[LEAD:BLOCK, 512:768] = dkvm_ref[...].astype(BF16)

        dhn = _dot(dproj_ref[...], w_ref[...], NN)
        h = h0_ref[...]
        r = _rms_stats(h)
        g = g_ref[...]
        hn_ref[...] = (h * r * g).astype(BF16)
        dx, dg = _rms_bwd(dhn, h, r, g)
        dg_ref[...] += dg
        slot = i % 2
        pl.when(i == 2)(lambda: gx_copy(0, 0, True).wait())
        pl.when(i > 2)(lambda: gx_copy(i - 2, slot, False).wait())
        dx_s[slot] = dh1_ref[...] + dx

        @pl.when(i == 0)
        def _():
            dmeta_ref[...] = dx_s[0, LEAD:BLOCK, :]
            gx_copy(0, 0, True).start()

        pl.when(i > 0)(lambda: gx_copy(i, slot, False).start())

        @pl.when(i == nt - 1)
        def _():
            gx_copy(nt - 2, (nt - 2) % 2, nt == 2).wait()
            gx_copy(nt - 1, (nt - 1) % 2, False).wait()

    return _pcall(
        body, name="inproj_bwd", grid=(nt,),
        in_specs=[_rows(TM, D_MODEL), _rows(TM, D_MODEL), _full((1, D_MODEL)), _rows(TM, 512), _rows(TM, 256),
                  _full((N_META, 256)), _rows(TM, 512), _rows(TM, 512), _full((1792, D_MODEL))],
        out_specs=[ANY, _full((N_META, D_MODEL)), _rows(TM, 1792), _rows(TM, D_MODEL), _full((1, D_MODEL))],
        out_shape=[jax.ShapeDtypeStruct((R - BLOCK, D_MODEL), F32), jax.ShapeDtypeStruct((N_META, D_MODEL), F32),
                   jax.ShapeDtypeStruct((R, 1792), BF16),
                   jax.ShapeDtypeStruct((R, D_MODEL), BF16), jax.ShapeDtypeStruct((1, D_MODEL), F32)],
        scratch_shapes=[pltpu.VMEM((2, TM, D_MODEL), F32), pltpu.SemaphoreType.DMA((2,))],
        semantics=("arbitrary",),
    )(dh1, h0, g1, dq, dkv, dkvm, dca, dcg, w_in_t)


ANY = pl.BlockSpec(memory_space=pl.ANY)


def _position():
    return lax.axis_index("x"), lax.axis_index("y"), lax.axis_index("c")


def _device_number(p):
    return 4 * p[0] + 2 * p[1] + p[2]


def _two_level_allgather(ins, outs, block, send_sems, recv_sems, local_sems, sem_base=0):
    n = len(ins)
    x, y, c = _position()
    me, sibling = (x, y, c), (x, y, 1 - c)
    chips = [(1 - x, y), (x, 1 - y), (1 - x, 1 - y)]

    def copy(w, k, origin, to, src=None):
        return pltpu.make_async_remote_copy(
            src_ref=block(w, origin) if src is None else src, dst_ref=block(w, origin),
            send_sem=send_sems.at[sem_base + 7 * w + k], recv_sem=recv_sems.at[sem_base + 7 * w + k],
            device_id=to, device_id_type=MESH)

    def mine(w):
        return pltpu.make_async_copy(ins[w], block(w, me), local_sems.at[w])

    def own(w):
        return [copy(w, 0, me, sibling, src=ins[w])] + [
            copy(w, 1 + j, me, (*chip, c), src=ins[w]) for j, chip in enumerate(chips)]

    def passed(w):
        return [copy(w, 4 + j, (*chip, c), sibling) for j, chip in enumerate(chips)]

    def start():
        for w in range(n):
            mine(w).start()
        for w in range(n):
            for cp in own(w):
                cp.start()

    def forward(w):
        fw = passed(w)
        for j, chip in enumerate(chips):
            copy(w, 1 + j, (*chip, c), me).wait_recv()
            fw[j].start()

    def finish():
        for w in range(n):
            copy(w, 0, sibling, me).wait_recv()
            for j, chip in enumerate(chips):
                copy(w, 4 + j, (*chip, 1 - c), me).wait_recv()
        for w in range(n):
            for cp in own(w) + passed(w):
                cp.wait_send()
            mine(w).wait()

    return start, forward, finish


def _carried_allgather(step, n_steps, shards, refs):
    ns = len(shards)
    ins, outs, (send_sems, recv_sems, local_sems) = refs[:ns], refs[ns:2 * ns], refs[2 * ns:]
    start, forward, finish = _two_level_allgather(
        ins, outs, _row_block(outs, [s.shape[0] for s in shards]), send_sems, recv_sems, local_sems)
    pl.when(step == 0)(start)
    total = sum(s.shape[0] for s in shards)
    sent = 0
    for w, s in enumerate(shards):
        sent += s.shape[0]
        pl.when(step == (AG_FORWARD_AT * sent * (n_steps - 1)) // (100 * total))(functools.partial(forward, w))
    return lambda: pl.when(step == n_steps - 1)(finish)


def _gathered_shapes(shards):
    return [jax.ShapeDtypeStruct((N_DEV * s.shape[0], s.shape[1]), s.dtype) for s in shards]


def _allgather_sems(ns):
    return _sem_pair(7 * ns) + [pltpu.SemaphoreType.DMA((ns,))]


def _blocking_allgather(ins, outs, block, send_sems, recv_sems, local_sems, sem_base=0):
    start, forward, finish = _two_level_allgather(ins, outs, block, send_sems, recv_sems, local_sems, sem_base)
    start()
    for w in range(len(ins)):
        forward(w)
    finish()


def _row_block(outs, rows):
    def block(w, p):
        return outs[w].at[pl.ds(pl.multiple_of(_device_number(p) * rows[w], 16), rows[w])]
    return block


def _sibling_copies(ins, outs, send_sems, recv_sems):
    x, y, c = _position()
    return [pltpu.make_async_remote_copy(
        src_ref=ins[w].at[:, 1 - c], dst_ref=outs[w], send_sem=send_sems.at[w], recv_sem=recv_sems.at[w],
        device_id=(x, y, 1 - c), device_id_type=MESH) for w in range(len(ins))]


def _chip_copies(ins, outs, send_sems, recv_sems):
    x, y, c = _position()
    chips = [(1 - x, y), (x, 1 - y), (1 - x, 1 - y)]
    return [pltpu.make_async_remote_copy(
        src_ref=ins[w].at[2 * chip[0] + chip[1]], dst_ref=outs[w].at[k],
        send_sem=send_sems.at[3 * w + k], recv_sem=recv_sems.at[3 * w + k],
        device_id=(*chip, c), device_id_type=MESH) for w in range(len(ins)) for k, chip in enumerate(chips)]


def _hosted(step, make_copies):
    @pl.when(step == 0)
    def _():
        for cp in make_copies():
            cp.start()


def _hosted_wait(step, n_steps, make_copies):
    @pl.when(step == n_steps - 1)
    def _():
        for cp in make_copies():
            cp.wait()


def _sem_pair(n):
    return [pltpu.SemaphoreType.DMA((n,)), pltpu.SemaphoreType.DMA((n,))]


def _allgather_params(shards, small):
    arrays = list(shards) + list(small)
    n, ns = len(arrays), len(shards)

    def body(*refs):
        ins, outs = refs[:n], refs[n:2 * n]
        send_sems, recv_sems, local_sems = refs[2 * n:]

        rows = _row_block(outs, [a.shape[0] for a in arrays])

        def block(w, p):
            return rows(w, p) if w < ns else outs[w].at[_device_number(p)]

        _blocking_allgather(ins, outs, block, send_sems, recv_sems, local_sems)

    out_shape = [jax.ShapeDtypeStruct((N_DEV * a.shape[0], a.shape[1]), a.dtype) for a in shards]
    out_shape += [jax.ShapeDtypeStruct((N_DEV,) + a.shape, a.dtype) for a in small]
    return _pcall(
        body, name="allgather_params", in_specs=[ANY] * n, out_specs=[ANY] * n, out_shape=out_shape,
        scratch_shapes=[pltpu.SemaphoreType.DMA((7 * n,)), pltpu.SemaphoreType.DMA((7 * n,)),
                        pltpu.SemaphoreType.DMA((n,))],
    )(*arrays)


def _reduce_siblings(grads):
    n = len(grads)

    def body(*refs):
        ins, outs = refs[:n], refs[n:2 * n]
        send_sems, recv_sems = refs[2 * n:]
        copies = _sibling_copies(ins, outs, send_sems, recv_sems)
        for cp in copies:
            cp.start()
        for cp in copies:
            cp.wait()

    return _pcall(
        body, name="reduce_siblings", in_specs=[ANY] * n, out_specs=[ANY] * n,
        out_shape=_sibling_shapes(grads), scratch_shapes=_sem_pair(n),
    )(*grads)


def _sibling_shapes(grads):
    return [jax.ShapeDtypeStruct((4,) + g.shape[2:], F32) for g in grads]


def _chip_shapes(partials):
    return [jax.ShapeDtypeStruct((3,) + p.shape[1:], F32) for p in partials]


def _add_sibling(grad, received, core, name):
    _, _, r, cols = grad.shape

    def body(core_ref, g_ref, r_ref, o_ref):
        o_ref[...] = g_ref[...] + r_ref[...]

    return pl.pallas_call(
        body, name=name,
        grid_spec=pltpu.PrefetchScalarGridSpec(
            num_scalar_prefetch=1, grid=(4,),
            in_specs=[pl.BlockSpec((None, None, r, cols), lambda s, core_ref: (s, core_ref[0], 0, 0)),
                      pl.BlockSpec((None, r, cols), lambda s, core_ref: (s, 0, 0))],
            out_specs=pl.BlockSpec((None, r, cols), lambda s, core_ref: (s, 0, 0))),
        out_shape=jax.ShapeDtypeStruct((4, r, cols), F32),
        compiler_params=pltpu.CompilerParams(vmem_limit_bytes=VMEM_LIMIT),
    )(core, grad, received)


def _reduce_chips(partials, small):
    n, ns = len(partials), len(small)

    def body(*refs):
        p_ins, s_ins = refs[:n], refs[n:n + ns]
        p_outs, s_outs = refs[n + ns:2 * n + ns], refs[2 * n + ns:2 * (n + ns)]
        send_sems, recv_sems, local_sems = refs[2 * (n + ns):]
        copies = _chip_copies(p_ins, p_outs, send_sems, recv_sems)
        for cp in copies:
            cp.start()
        _blocking_allgather(s_ins, s_outs, lambda w, p: s_outs[w].at[_device_number(p)],
                            send_sems, recv_sems, local_sems, sem_base=3 * n)
        for cp in copies:
            cp.wait()

    out_shape = _chip_shapes(partials)
    out_shape += [jax.ShapeDtypeStruct((N_DEV,) + a.shape, a.dtype) for a in small]
    nsem = 3 * n + 7 * ns
    return _pcall(
        body, name="reduce_chips", in_specs=[ANY] * (n + ns), out_specs=[ANY] * (n + ns), out_shape=out_shape,
        scratch_shapes=[pltpu.SemaphoreType.DMA((nsem,)), pltpu.SemaphoreType.DMA((nsem,)),
                        pltpu.SemaphoreType.DMA((ns,))],
    )(*partials, *small)


def _adam(g, w, m, v):
    m = ADAM_B1 * m + (1.0 - ADAM_B1) * g
    v = ADAM_B2 * v + (1.0 - ADAM_B2) * (g * g)
    m_hat = m / (1.0 - ADAM_B1 ** ADAM_STEP)
    v_hat = v / (1.0 - ADAM_B2 ** ADAM_STEP)
    delta = -ADAM_LR * (m_hat / (jnp.sqrt(v_hat) + ADAM_EPS) + ADAM_WD * w)
    return delta, m, v


def _adamw(partial, received, slot, w, m, v, name):
    _, r, cols = partial.shape

    def body(slot_ref, p_ref, r_ref, w_ref, m_ref, v_ref, g_ref, d_ref, nm_ref, nv_ref):
        g = p_ref[...] + r_ref[0] + r_ref[1] + r_ref[2]
        g_ref[...] = g
        d_ref[...], nm_ref[...], nv_ref[...] = _adam(g, w_ref[...], m_ref[...], v_ref[...])

    whole = pl.BlockSpec((r, cols), lambda i, slot_ref: (0, 0))
    out = jax.ShapeDtypeStruct((r, cols), F32)
    return pl.pallas_call(
        body, name=name,
        grid_spec=pltpu.PrefetchScalarGridSpec(
            num_scalar_prefetch=1, grid=(1,),
            in_specs=[pl.BlockSpec((None, r, cols), lambda i, slot_ref: (slot_ref[0], 0, 0)),
                      pl.BlockSpec((3, r, cols), lambda i, slot_ref: (0, 0, 0)), whole, whole, whole],
            out_specs=[whole, whole, whole, whole]),
        out_shape=[out, out, out, out],
        compiler_params=pltpu.CompilerParams(vmem_limit_bytes=VMEM_LIMIT),
    )(slot, partial, received, w, m, v)


def _adamw_small(dev, ga, gb, gc, params):
    names = ["meta", "attn_norm", "sinks", "conv_w", "conv_b", "ln_g", "ln_b", "attn_out", "conv_out",
             "ffn_norm", "final_norm"]
    flat = [a for p in params for a in p]
    n_in = len(flat)

    def body(dev_ref, ga_ref, gb_ref, gc_ref, *refs):
        ins, outs = refs[:n_in], refs[n_in:n_in + 4 * len(names)]
        loss_ref, sb, sc = refs[n_in + 4 * len(names):]
        a = ga_ref[0]
        sb[...] = gb_ref[0]
        sc[...] = gc_ref[0]
        for d in range(1, N_DEV):
            a = a + ga_ref[d]
            sb[...] += gb_ref[d]
            sc[...] += gc_ref[d]
        dev = dev_ref[0]
        grads = {
            "attn_norm": a[0:1, :], "ffn_norm": a[1:2, :], "final_norm": a[2:3, :],
            "conv_b": a[3:4, 0:512], "ln_g": a[3:4, 512:1024], "ln_b": a[4:5, 0:512],
            "attn_out": a[4:5, 512:1024], "conv_out": a[5:6, 0:512], "sinks": a[5:6, 512:512 + N_HEADS],
            "meta": sb[pl.ds(pl.multiple_of(dev * N_META, N_META), N_META), :],
            "conv_w": sc[pl.ds(pl.multiple_of(dev * 32, 32), 32), :][0:CONV_K, :],
        }
        for idx, nm in enumerate(names):
            w_ref, m_ref, v_ref = ins[3 * idx:3 * idx + 3]
            g = grads[nm]
            delta, m, v = _adam(g, w_ref[...], m_ref[...], v_ref[...])
            o = outs[4 * idx:4 * idx + 4]
            o[0][...], o[1][...], o[2][...], o[3][...] = g, delta, m, v
        loss_ref[...] = a[6:7, 0:1]

    vm = pl.BlockSpec(memory_space=pltpu.VMEM)
    out_shape = [jax.ShapeDtypeStruct(p[0].shape, F32) for p in params for _ in range(4)]
    out_shape.append(jax.ShapeDtypeStruct((1, 1), F32))
    res = pl.pallas_call(
        body, name="adamw_small",
        grid_spec=pltpu.PrefetchScalarGridSpec(
            num_scalar_prefetch=1, grid=(1,),
            in_specs=[pl.BlockSpec(ga.shape, lambda i, d: (0, 0, 0)), pl.BlockSpec(gb.shape, lambda i, d: (0, 0, 0)),
                      pl.BlockSpec(gc.shape, lambda i, d: (0, 0, 0))]
            + [pl.BlockSpec(a.shape, lambda i, d: (0, 0)) for a in flat],
            out_specs=[pl.BlockSpec(s.shape, lambda i, d: (0, 0)) for s in out_shape],
            scratch_shapes=[pltpu.VMEM(gb.shape[1:], F32), pltpu.VMEM(gc.shape[1:], F32)]),
        out_shape=out_shape,
        compiler_params=pltpu.CompilerParams(vmem_limit_bytes=VMEM_LIMIT),
    )(dev, ga, gb, gc, *flat)
    return [res[4 * i:4 * i + 4] for i in range(len(names))], res[-1]


def kernel(x, meta_tokens, attn_norm_g, w_in, attn_sinks, conv_w, conv_b, conv_ln_g, conv_ln_b, attn_out_g, conv_out_g, w_out, ffn_norm_g, w_gate, w_up, w_down, final_norm_g, loss_target, m_meta_tokens, m_attn_norm_g, m_w_in, m_attn_sinks, m_conv_w, m_conv_b, m_conv_ln_g, m_conv_ln_b, m_attn_out_g, m_conv_out_g, m_w_out, m_ffn_norm_g, m_w_gate, m_w_up, m_w_down, m_final_norm_g, v_meta_tokens, v_attn_norm_g, v_w_in, v_attn_sinks, v_conv_w, v_conv_b, v_conv_ln_g, v_conv_ln_b, v_attn_out_g, v_conv_out_g, v_w_out, v_ffn_norm_g, v_w_gate, v_w_up, v_w_down, v_final_norm_g):
    xi, yi, ci = _position()
    dev = jnp.reshape(_device_number((xi, yi, ci)), (1,)).astype(jnp.int32)
    core = jnp.reshape(ci, (1,)).astype(jnp.int32)
    slot = jnp.reshape(2 * xi + yi, (1,)).astype(jnp.int32)

    w_in_t, meta_st, convw_st = _allgather_params([w_in[0].T.astype(BF16)], [meta_tokens, conv_w[0]])
    meta_full = jnp.transpose(meta_st, (1, 0, 2)).reshape(N_META, D_MODEL)
    convw_full = jnp.transpose(convw_st, (1, 0, 2)).reshape(CONV_K, CONV_W)

    final_g = final_norm_g.reshape(1, D_MODEL)

    h0, q, kv, ca, cg = _inproj_fwd(x[0], meta_full, attn_norm_g, w_in_t)
    o_attn, lse, (w_out_b, wg_t) = _attn_fwd(
        q, kv, attn_sinks, [w_out[0].astype(BF16), w_gate[0].T.astype(BF16)])
    (o_conv, y_conv), (wu_t, wd_b) = _conv_fwd(
        ca, cg, convw_full, conv_b, conv_ln_g, conv_ln_b, [w_up[0].T.astype(BF16), w_down[0].astype(BF16)])
    h1 = _outproj_fwd(h0, o_attn, o_conv, attn_out_g, conv_out_g, w_out_b)
    gate, up, act, dh2, loss_sum, dg_final = _ffn_fwd(h1, ffn_norm_g, wg_t, wu_t, wd_b, final_g, loss_target[0])

    def blocks(g):
        return g.reshape(4, 2, g.shape[0] // N_DEV, D_MODEL)

    def add_siblings(grads, received, tags):
        return [_add_sibling(g, r, core, "add_sibling_" + t) for g, r, t in zip(grads, received, tags)]

    dgate, dup, hn2, dh1, dg_ffn = _ffn_bwd(dh2, h1, ffn_norm_g, gate, up, wg_t, wu_t, wd_b)
    ffn_grads = [blocks(_wgrad(dgate, hn2, FF_CHUNK, "wgrad_gate")), blocks(_wgrad(dup, hn2, FF_CHUNK, "wgrad_up")),
                 blocks(_wgrad(act, dh2, FF_CHUNK, "wgrad_down"))]
    (do_attn, do_conv, mixed, dg_ao, dg_co), ffn_sib = _outproj_bwd(
        dh1, o_attn, o_conv, attn_out_g, conv_out_g, w_out_b, ffn_grads)
    ffn_sums = add_siblings(ffn_grads, ffn_sib, ("gate", "up", "down"))
    out_grads = [blocks(_wgrad(mixed, dh1, D_MODEL, "wgrad_out"))]
    (dca, dcg, dconvw, dconvb, dln_g, dln_b), gate_up_chips = _conv_bwd(
        do_conv, y_conv, ca, cg, convw_full, conv_ln_g, conv_ln_b, ffn_sums[:2])
    (dq, dkv_shifted, dkvm, dsinks), out_sib, down_chips = _attn_bwd(
        q, kv, attn_sinks, o_attn, lse, do_attn, out_grads, ffn_sums[2:])
    dkv = dkv_shifted[BLOCK:BLOCK + dq.shape[0]]
    ffn_chips = list(gate_up_chips) + list(down_chips)
    out_sums = add_siblings(out_grads, out_sib, ("out",))
    grad_x, dmeta, dproj, hn1, dg_attn = _inproj_bwd(dh1, h0, attn_norm_g, dq, dkv, dkvm, dca, dcg, w_in_t)
    dwi_t, out_chips = _wgrad(dproj, hn1, 1792, "wgrad_in", out_sums)
    in_grads = [blocks(dwi_t)]
    in_sums = add_siblings(in_grads, _reduce_siblings(in_grads), ("in",))
    small_a = jnp.concatenate([
        dg_attn, dg_ffn, dg_final, jnp.concatenate([dconvb, dln_g], axis=1), jnp.concatenate([dln_b, dg_ao], axis=1),
        jnp.concatenate([dg_co, dsinks, jnp.zeros((1, 512 - N_HEADS), F32)], axis=1),
        jnp.concatenate([loss_sum[0:1, :], jnp.zeros((1, D_MODEL - 128), F32)], axis=1),
        jnp.zeros((1, D_MODEL), F32)], axis=0)
    small_b = jnp.transpose(dmeta.reshape(N_META, N_DEV, 128), (1, 0, 2)).reshape(N_DEV * N_META, 128)
    small_c = jnp.transpose(dconvw.reshape(32, N_DEV, 64), (1, 0, 2)).reshape(N_DEV * 32, 64)
    in_chips, ga, gb, gc = _reduce_chips(in_sums, [small_a, small_b, small_c])
    tags = ("in", "out", "gate", "up", "down")
    chip_sums = in_sums + out_sums + ffn_sums
    from_chips = [in_chips] + list(out_chips) + list(ffn_chips)

    big = [(True, w_in, m_w_in, v_w_in), (False, w_out, m_w_out, v_w_out), (True, w_gate, m_w_gate, v_w_gate),
           (True, w_up, m_w_up, v_w_up), (False, w_down, m_w_down, v_w_down)]
    big_out = {}
    for t, p, r, (transposed, w, m, v) in zip(tags, chip_sums, from_chips, big):
        rows = (lambda a: jnp.transpose(a[0])) if transposed else (lambda a: a[0])
        back = (lambda a: jnp.transpose(a)[None]) if transposed else (lambda a: a[None])
        big_out[t] = [back(a) for a in _adamw(p, r, slot, rows(w), rows(m), rows(v), "adamw_" + t)]

    small_params = [
        (meta_tokens, m_meta_tokens, v_meta_tokens), (attn_norm_g, m_attn_norm_g, v_attn_norm_g),
        (attn_sinks, m_attn_sinks, v_attn_sinks), (conv_w[0], m_conv_w[0], v_conv_w[0]),
        (conv_b, m_conv_b, v_conv_b), (conv_ln_g, m_conv_ln_g, v_conv_ln_g), (conv_ln_b, m_conv_ln_b, v_conv_ln_b),
        (attn_out_g, m_attn_out_g, v_attn_out_g), (conv_out_g, m_conv_out_g, v_conv_out_g),
        (ffn_norm_g, m_ffn_norm_g, v_ffn_norm_g),
        (final_g, m_final_norm_g.reshape(1, D_MODEL), v_final_norm_g.reshape(1, D_MODEL))]
    sm, loss = _adamw_small(dev, ga, gb, gc, small_params)
    sm[3] = [a[None] for a in sm[3]]
    sm[10] = [a.reshape(D_MODEL) for a in sm[10]]

    per_param = [sm[0], sm[1], big_out["in"], sm[2], sm[3], sm[4], sm[5], sm[6], sm[7], sm[8], big_out["out"],
                 sm[9], big_out["gate"], big_out["up"], big_out["down"], sm[10]]
    loss = loss.reshape(())
    outs = [loss, grad_x[None]]
    for kind in range(4):
        outs += [p[kind] for p in per_param]
    return tuple(outs)
```

```python
import functools
import math

import jax
import jax.numpy as jnp
from jax import lax
from jax.experimental import pallas as pl
from jax.experimental.pallas import tpu as pltpu

F32, BF16 = jnp.float32, jnp.bfloat16
MESH = pl.DeviceIdType.MESH

D_MODEL = 1024
N_META = 16
BLOCK = 128
LEAD = BLOCK - N_META
HEAD_DIM = 64
N_HEADS = 8
GROUP = 4
ATTN_W = 512
KV_W = 128
CONV_W = 512
CONV_K = 31
HALO = 32
D_FF = 2816
FF_CHUNK = D_FF // 2
FF_SUB = [slice(s, s + 256) for s in range(0, D_FF, 256)]
N_DEV = 8
EPS = 1e-5
NEG = -1e30
TM = 640
TILE_BLOCKS = TM // BLOCK
AG_FORWARD_AT = 85
WGRAD_K_TILES = 5
CONV_ROWS = 32
VMEM_LIMIT = 56 * 1024 * 1024

ADAM_LR, ADAM_B1, ADAM_B2, ADAM_EPS, ADAM_WD, ADAM_STEP = 0.001, 0.9, 0.999, 1e-08, 0.01, 10

NT = (((1,), (1,)), ((), ()))
NN = (((1,), (0,)), ((), ()))
TN = (((0,), (0,)), ((), ()))


def _dot(a, b, dims):
    return lax.dot_general(a, b, dims, preferred_element_type=F32)


def _sigmoid(x):
    return 1.0 / (1.0 + jnp.exp(-x))


def _pcall(body, *, name, out_shape, grid=None, in_specs=None, out_specs=None, scratch_shapes=(),
           semantics=None, **kw):
    params = dict(vmem_limit_bytes=VMEM_LIMIT)
    if semantics is not None:
        params["dimension_semantics"] = semantics
    extra = {}
    if grid is not None:
        extra["grid"] = grid
    if in_specs is not None:
        extra["in_specs"] = in_specs
    if out_specs is not None:
        extra["out_specs"] = out_specs
    return pl.pallas_call(body, name=name, out_shape=out_shape, scratch_shapes=list(scratch_shapes),
                          compiler_params=pltpu.CompilerParams(**params), **extra, **kw)


def _rows(tm, cols):
    return pl.BlockSpec((tm, cols), lambda i, *_: (i, 0))


def _full(shape):
    nd = len(shape)
    return pl.BlockSpec(shape, lambda *_: (0,) * nd)


def _rms_stats(x):
    return lax.rsqrt(jnp.mean(x * x, axis=-1, keepdims=True) + EPS)


def _rms_bwd(dy, x, r, g):
    t = dy * g
    dx = r * (t - x * (r * r) * jnp.mean(t * x, axis=-1, keepdims=True))
    dg = jnp.sum(dy * x * r, axis=0, keepdims=True)
    return dx, dg


def _inproj_fwd(x, meta, g1, w_in_t):
    R = x.shape[0] + BLOCK
    nt = R // TM
    assert nt >= 2

    def body(x_hbm, meta_ref, g_ref, w_ref, h0_ref, q_ref, kv_ref, ca_ref, cg_ref, x_s, sems):
        i = pl.program_id(0)
        slot = i % 2

        def x_copy(step, slot, first):
            if first:
                return pltpu.make_async_copy(x_hbm.at[pl.ds(0, TM - BLOCK)],
                                             x_s.at[slot, pl.ds(BLOCK, TM - BLOCK)], sems.at[slot])
            return pltpu.make_async_copy(
                x_hbm.at[pl.ds(pl.multiple_of(step * TM - BLOCK, BLOCK), TM)], x_s.at[slot], sems.at[slot])

        @pl.when(i == 0)
        def _():
            x_copy(0, 0, True).start()
            x_s[0, 0:LEAD, :] = jnp.zeros((LEAD, D_MODEL), F32)
            x_s[0, LEAD:BLOCK, :] = meta_ref[...]

        pl.when(i + 1 < nt)(lambda: x_copy(i + 1, 1 - slot, False).start())
        pl.when(i == 0)(lambda: x_copy(0, 0, True).wait())
        pl.when(i > 0)(lambda: x_copy(i, slot, False).wait())
        h = x_s[slot]
        h0_ref[...] = h
        hn = (h * _rms_stats(h) * g_ref[...]).astype(BF16)
        q_ref[...] = _dot(hn, w_ref[0:512, :], NT).astype(BF16)
        kv_ref[...] = _dot(hn, w_ref[512:768, :], NT).astype(BF16)
        ca_ref[...] = _dot(hn, w_ref[768:1280, :], NT)
        cg_ref[...] = _dot(hn, w_ref[1280:1792, :], NT)

    return _pcall(
        body, name="inproj_fwd", grid=(nt,),
        in_specs=[pl.BlockSpec(memory_space=pl.ANY), _full((N_META, D_MODEL)), _full((1, D_MODEL)),
                  _full((1792, D_MODEL))],
        out_specs=[_rows(TM, D_MODEL), _rows(TM, 512), _rows(TM, 256), _rows(TM, 512), _rows(TM, 512)],
        out_shape=[jax.ShapeDtypeStruct((R, D_MODEL), F32),
                   jax.ShapeDtypeStruct((R, 512), BF16), jax.ShapeDtypeStruct((R, 256), BF16),
                   jax.ShapeDtypeStruct((R, 512), F32), jax.ShapeDtypeStruct((R, 512), F32)],
        scratch_shapes=[pltpu.VMEM((2, TM, D_MODEL), F32), pltpu.SemaphoreType.DMA((2,))],
        semantics=("arbitrary",),
    )(x, meta, g1, w_in_t)


GB = GROUP * BLOCK
ATTN_SCALE = 1.0 / math.sqrt(HEAD_DIM)


def _group_lanes(xt, g):
    return jnp.concatenate(
        [xt[HEAD_DIM * (GROUP * g + j):HEAD_DIM * (GROUP * g + j + 1), :] for j in range(GROUP)], axis=1)


def _head_lanes(ref, g):
    return jnp.concatenate([ref[GROUP * g + j:GROUP * g + j + 1, :] for j in range(GROUP)], axis=1)


def _head_rows(xs):
    return jnp.concatenate([x[:, BLOCK * j:BLOCK * (j + 1)] for x in xs for j in range(GROUP)], axis=0)


def _attn_tables(sink_ref, bias_s, sink_s):
    kk = lax.broadcasted_iota(jnp.int32, (BLOCK, BLOCK), 0)
    ii = lax.broadcasted_iota(jnp.int32, (BLOCK, BLOCK), 1)
    dist = jnp.where(kk <= ii, ii - kk, ii - kk + BLOCK).astype(F32)
    for h in range(N_HEADS):
        bias_s[:, BLOCK * h:BLOCK * (h + 1)] = dist * -(2.0 ** -(h + 1))
        sink_s[:, BLOCK * h:BLOCK * (h + 1)] = jnp.zeros((1, BLOCK), F32) + sink_ref[0, h]


def _attn_masks(b):
    kk = lax.broadcasted_iota(jnp.int32, (BLOCK, GB), 0)
    ii = lax.broadcasted_iota(jnp.int32, (BLOCK, GB), 1) & (BLOCK - 1)
    sel = kk <= ii
    pen = jnp.where(sel, jnp.where(b >= 1, 0.0, NEG), jnp.where(b >= 2, 0.0, NEG))
    mj = lax.broadcasted_iota(jnp.int32, (N_META, GB), 0)
    mi = lax.broadcasted_iota(jnp.int32, (N_META, GB), 1) & (BLOCK - 1)
    pen_m = jnp.where((mj + LEAD) <= (mi + b * BLOCK), 0.0, NEG)
    return sel, pen, pen_m


def _attn_scores(qt, kc, kp, km, sel, pen, pen_m, bias):
    s_b = jnp.where(sel, _dot(kc, qt, NN), _dot(kp, qt, NN)) + bias + pen
    s_m = _dot(km, qt, NN) + pen_m
    return s_b, s_m


def _attn_fwd(q, kv, sinks, shards):
    R = q.shape[0]
    nt = R // TM
    ns = len(shards)

    def body(sink_ref, q_ref, kv_ref, kvh_ref, kvm_ref, *refs):
        ag_ins, (o_ref, lse_ref), ag_outs = refs[:ns], refs[ns:ns + 2], refs[ns + 2:2 * ns + 2]
        ag_sems, (bias_s, sink_s) = refs[2 * ns + 2:2 * ns + 5], refs[2 * ns + 5:]
        i = pl.program_id(0)
        ag_finish = _carried_allgather(i, nt, shards, ag_ins + ag_outs + ag_sems)
        pl.when(i == 0)(functools.partial(_attn_tables, sink_ref, bias_s, sink_s))
        for s in range(TILE_BLOCKS):
            rows = slice(BLOCK * s, BLOCK * (s + 1))
            kvp = kvh_ref[...] if s == 0 else kv_ref[BLOCK * (s - 1):BLOCK * s, :]
            sel, pen, pen_m = _attn_masks(i * TILE_BLOCKS + s)
            q_t = (q_ref[rows, :] * ATTN_SCALE).T
            kvc_t, kvp_t = kv_ref[rows, :].T, kvp.T
            outs = []
            for g in range(N_HEADS // GROUP):
                ks = slice(HEAD_DIM * g, HEAD_DIM * (g + 1))
                vs = slice(KV_W + HEAD_DIM * g, KV_W + HEAD_DIM * (g + 1))
                lanes = slice(GB * g, GB * (g + 1))
                s_b, s_m = _attn_scores(_group_lanes(q_t, g), kv_ref[rows, ks], kvp[:, ks], kvm_ref[LEAD:BLOCK, ks],
                                        sel, pen, pen_m, bias_s[:, lanes])
                sink = sink_s[:, lanes]
                m = jnp.maximum(jnp.maximum(jnp.max(s_b, axis=0, keepdims=True),
                                            jnp.max(s_m, axis=0, keepdims=True)), sink)
                p_b = jnp.exp(s_b - m)
                p_m = jnp.exp(s_m - m)
                l = jnp.sum(p_b, axis=0, keepdims=True) + jnp.sum(p_m, axis=0, keepdims=True) + jnp.exp(sink - m)
                p_c = jnp.where(sel, p_b, 0.0).astype(BF16)
                p_p = jnp.where(sel, 0.0, p_b).astype(BF16)
                o_t = (_dot(kvc_t[vs, :], p_c, NN) + _dot(kvp_t[vs, :], p_p, NN)
                       + _dot(kvm_ref[LEAD:BLOCK, vs], p_m.astype(BF16), TN))
                outs.append(o_t / l)
                lse = m + jnp.log(l)
                for j in range(GROUP):
                    lse_ref[GROUP * g + j:GROUP * g + j + 1, rows] = lse[:, BLOCK * j:BLOCK * (j + 1)]
            o_ref[rows, :] = _head_rows(outs).T
        ag_finish()

    res = _pcall(
        body, name="attn_fwd", grid=(nt,),
        in_specs=[pl.BlockSpec(memory_space=pltpu.SMEM),
                  _rows(TM, 512), _rows(TM, 256),
                  pl.BlockSpec((BLOCK, 256), lambda i: (jnp.maximum(i * TILE_BLOCKS - 1, 0), 0)),
                  _full((BLOCK, 256))] + [ANY] * ns,
        out_specs=[_rows(TM, 512), pl.BlockSpec((N_HEADS, TM), lambda i: (0, i))] + [ANY] * ns,
        out_shape=[jax.ShapeDtypeStruct((R, 512), F32), jax.ShapeDtypeStruct((N_HEADS, R), F32)]
        + _gathered_shapes(shards),
        scratch_shapes=_allgather_sems(ns) + [pltpu.VMEM((BLOCK, N_HEADS * BLOCK), F32),
                                              pltpu.VMEM((1, N_HEADS * BLOCK), F32)],
        semantics=("arbitrary",),
    )(sinks, q, kv, kv, kv, *shards)
    return res[0], res[1], res[2:]


def _ln_silu(y, lg, lb):
    mu = jnp.mean(y, axis=-1, keepdims=True)
    xc = y - mu
    rstd = lax.rsqrt(jnp.mean(xc * xc, axis=-1, keepdims=True) + EPS)
    xhat = xc * rstd
    yn = xhat * lg + lb
    return yn, xhat, rstd


PHASE_ROWS = HALO + TM - 8


def _phase_scratch():
    return pltpu.VMEM((7, PHASE_ROWS, CONV_W), F32)


def _phase_copies(src_s, ph_s):
    for b in range(1, 8):
        ph_s[b - 1] = src_s[pl.ds(b, PHASE_ROWS), :]


def _shifted(src_s, ph_s, start, rows):
    a8, b = (start // 8) * 8, start % 8
    if b == 0:
        return src_s[pl.ds(a8, rows), :]
    return ph_s[b - 1, pl.ds(a8, rows), :]


def _conv_fwd(ca, cg, conv_w, conv_b, ln_g, ln_b, shards):
    R = ca.shape[0]
    nt = R // TM
    hpt = TM // HALO
    ns = len(shards)

    def body(ca_ref, cg_ref, cah_ref, cgh_ref, w_ref, b_ref, lg_ref, lb_ref, *refs):
        ag_ins, (oc_ref, y_ref), ag_outs = refs[:ns], refs[ns:ns + 2], refs[ns + 2:2 * ns + 2]
        ag_sems, (u_s, uph_s) = refs[2 * ns + 2:2 * ns + 5], refs[2 * ns + 5:]
        i = pl.program_id(0)
        ag_finish = _carried_allgather(i, nt, shards, ag_ins + ag_outs + ag_sems)
        u_s[HALO:HALO + TM, :] = ca_ref[...] * _sigmoid(cg_ref[...])
        u_s[0:HALO, :] = jnp.where(i > 0, cah_ref[...] * _sigmoid(cgh_ref[...]), 0.0)
        _phase_copies(u_s, uph_s)
        for rc in range(TM // CONV_ROWS):
            base = rc * CONV_ROWS + HALO - (CONV_K - 1)
            acc = jnp.zeros((CONV_ROWS, CONV_W), F32) + b_ref[...]
            for k in range(CONV_K):
                acc = acc + _shifted(u_s, uph_s, base + k, CONV_ROWS) * w_ref[k:k + 1, :]
            rows = slice(rc * CONV_ROWS, (rc + 1) * CONV_ROWS)
            y_ref[rows, :] = acc
            yn, _, _ = _ln_silu(acc, lg_ref[...], lb_ref[...])
            oc_ref[rows, :] = yn * _sigmoid(yn)
        ag_finish()

    prev_halo = pl.BlockSpec((HALO, CONV_W), lambda i: (jnp.maximum(i * hpt - 1, 0), 0))
    anywhere = pl.BlockSpec(memory_space=pl.ANY)
    res = _pcall(
        body, name="conv_fwd", grid=(nt,),
        in_specs=[_rows(TM, CONV_W), _rows(TM, CONV_W), prev_halo, prev_halo,
                  _full((CONV_K, CONV_W)), _full((1, CONV_W)), _full((1, CONV_W)), _full((1, CONV_W))]
        + [anywhere] * ns,
        out_specs=[_rows(TM, CONV_W), _rows(TM, CONV_W)] + [anywhere] * ns,
        out_shape=[jax.ShapeDtypeStruct((R, CONV_W), F32), jax.ShapeDtypeStruct((R, CONV_W), F32)]
        + _gathered_shapes(shards),
        scratch_shapes=_allgather_sems(ns) + [pltpu.VMEM((HALO + TM, CONV_W), F32), _phase_scratch()],
        semantics=("arbitrary",),
    )(ca, cg, ca, cg, conv_w, conv_b, ln_g, ln_b, *shards)
    return res[:2], res[2:]


def _outproj_fwd(h0, o_attn, o_conv, ga, gc, w_out, shards):
    R = h0.shape[0]
    nt = R // TM
    ns = len(shards)

    def body(h_ref, oa_ref, oc_ref, ga_ref, gc_ref, w_ref, *refs):
        ag_ins, h1_ref, ag_outs, ag_sems = refs[:ns], refs[ns], refs[ns + 1:2 * ns + 1], refs[2 * ns + 1:]
        ag_finish = _carried_allgather(pl.program_id(0), nt, shards, ag_ins + ag_outs + ag_sems)
        oa, oc = oa_ref[...], oc_ref[...]
        ma = (oa * _rms_stats(oa) * ga_ref[...]).astype(BF16)
        mc = (oc * _rms_stats(oc) * gc_ref[...]).astype(BF16)
        h1_ref[...] = h_ref[...] + _dot(ma, w_ref[0:512, :], NN) + _dot(mc, w_ref[512:1024, :], NN)
        ag_finish()

    anywhere = pl.BlockSpec(memory_space=pl.ANY)
    res = _pcall(
        body, name="outproj_fwd", grid=(nt,),
        in_specs=[_rows(TM, D_MODEL), _rows(TM, 512), _rows(TM, 512), _full((1, 512)), _full((1, 512)),
                  _full((D_MODEL, D_MODEL))] + [anywhere] * ns,
        out_specs=[_rows(TM, D_MODEL)] + [anywhere] * ns,
        out_shape=[jax.ShapeDtypeStruct((R, D_MODEL), F32)] + _gathered_shapes(shards),
        scratch_shapes=_allgather_sems(ns),
        semantics=("arbitrary",),
    )(h0, o_attn, o_conv, ga, gc, w_out, *shards)
    return res[0], res[1:]


def _target_copy(tgt_hbm, tgt_s, sem, i, first):
    if first:
        return pltpu.make_async_copy(tgt_hbm.at[pl.ds(0, TM - BLOCK)], tgt_s.at[pl.ds(BLOCK, TM - BLOCK)], sem)
    return pltpu.make_async_copy(tgt_hbm.at[pl.ds(i * TM - BLOCK, TM)], tgt_s, sem)


def _resident(shape):
    nd = len(shape)
    return pl.BlockSpec(shape, lambda *_: (0,) * nd, pipeline_mode=pl.Buffered(1))


def _ffn_fwd(h1, g2, wg_t, wu_t, wd, gf, target):
    R = h1.shape[0]
    nt = R // TM

    def body(h1_ref, g2_ref, wg_ref, wu_ref, wd_ref, gf_ref, tgt_hbm,
             gate_ref, up_ref, act_s, dh2_ref, loss_ref, dgf_ref, tgt_s, sem):
        i = pl.program_id(0)

        @pl.when(i == 0)
        def _():
            loss_ref[...] = jnp.zeros_like(loss_ref)
            dgf_ref[...] = jnp.zeros_like(dgf_ref)
            tgt_s[0:BLOCK, :] = jnp.zeros((BLOCK, D_MODEL), F32)
            _target_copy(tgt_hbm, tgt_s, sem, i, True).start()

        pl.when(i > 0)(lambda: _target_copy(tgt_hbm, tgt_s, sem, i, False).start())
        h1 = h1_ref[...]
        hn = (h1 * _rms_stats(h1) * g2_ref[...]).astype(BF16)
        for cs in FF_SUB:
            gate = _dot(hn, wg_ref[cs, :], NT)
            up = _dot(hn, wu_ref[cs, :], NT)
            gate_ref[:, cs] = gate.astype(BF16)
            up_ref[:, cs] = up.astype(BF16)
            act_s[:, cs] = (gate * _sigmoid(gate) * up).astype(BF16)
        part = _dot(act_s[...], wd_ref[...], NN)
        pl.when(i == 0)(lambda: _target_copy(tgt_hbm, tgt_s, sem, i, True).wait())
        pl.when(i > 0)(lambda: _target_copy(tgt_hbm, tgt_s, sem, i, False).wait())
        h2 = h1 + part
        rf = _rms_stats(h2)
        gf = gf_ref[...]
        row = lax.broadcasted_iota(jnp.int32, (TM, 1), 0) + i * TM
        err = jnp.where(row >= BLOCK, h2 * rf * gf - tgt_s[...], 0.0)
        dy = err * (1.0 / D_MODEL)
        dh2, dgf = _rms_bwd(dy, h2, rf, gf)
        dh2_ref[...] = dh2
        loss_ref[...] += (0.5 / D_MODEL) * jnp.sum(err * err)
        dgf_ref[...] += dgf

    wspec = _resident((D_FF, D_MODEL))
    return _pcall(
        body, name="ffn_fwd", grid=(nt,),
        in_specs=[_rows(TM, D_MODEL), _full((1, D_MODEL)), wspec, wspec, wspec, _full((1, D_MODEL)),
                  pl.BlockSpec(memory_space=pl.ANY)],
        out_specs=[_rows(TM, D_FF), _rows(TM, D_FF), _rows(TM, D_FF), _rows(TM, D_MODEL), _full((8, 128)),
                   _full((1, D_MODEL))],
        out_shape=[jax.ShapeDtypeStruct((R, D_FF), BF16)] * 3
        + [jax.ShapeDtypeStruct((R, D_MODEL), F32),
           jax.ShapeDtypeStruct((8, 128), F32), jax.ShapeDtypeStruct((1, D_MODEL), F32)],
        scratch_shapes=[pltpu.VMEM((TM, D_MODEL), F32), pltpu.SemaphoreType.DMA],
        semantics=("arbitrary",),
    )(h1, g2, wg_t, wu_t, wd, gf, target)


def _ffn_bwd(dh2, h1, g2, gate, up, wg_t, wu_t, wd):
    R = h1.shape[0]
    nt = R // TM
    act_shape = jax.ShapeDtypeStruct((R, D_FF), BF16)

    def act_body(dh2_ref, gate_ref, up_ref, wd_ref, dgate_ref, dup_ref):
        dhb = dh2_ref[...].astype(BF16)
        for cs in FF_SUB:
            dact = _dot(dhb, wd_ref[cs, :], NT)
            gate = gate_ref[:, cs].astype(F32)
            up = up_ref[:, cs].astype(F32)
            sig = _sigmoid(gate)
            dgate_ref[:, cs] = (dact * up * (sig * (1.0 + gate * (1.0 - sig)))).astype(BF16)
            dup_ref[:, cs] = (dact * (gate * sig)).astype(BF16)

    dgate, dup = _pcall(
        act_body, name="ffn_bwd_act", grid=(nt,),
        in_specs=[_rows(TM, D_MODEL), _rows(TM, D_FF), _rows(TM, D_FF), _resident((D_FF, D_MODEL))],
        out_specs=[_rows(TM, D_FF), _rows(TM, D_FF)], out_shape=[act_shape, act_shape],
        semantics=("parallel",),
    )(dh2, gate, up, wd)

    def in_body(dh2_ref, h1_ref, g2_ref, dgate_ref, dup_ref, wg_ref, wu_ref, hn_ref, dh1_ref, dg2_ref):
        @pl.when(pl.program_id(0) == 0)
        def _():
            dg2_ref[...] = jnp.zeros_like(dg2_ref)

        dhn = _dot(dgate_ref[...], wg_ref[...], NN) + _dot(dup_ref[...], wu_ref[...], NN)
        h1 = h1_ref[...]
        r = _rms_stats(h1)
        g2 = g2_ref[...]
        hn_ref[...] = (h1 * r * g2).astype(BF16)
        dx, dg = _rms_bwd(dhn, h1, r, g2)
        dh1_ref[...] = dh2_ref[...] + dx
        dg2_ref[...] += dg

    hn2, dh1, dg2 = _pcall(
        in_body, name="ffn_bwd_in", grid=(nt,),
        in_specs=[_rows(TM, D_MODEL), _rows(TM, D_MODEL), _full((1, D_MODEL)), _rows(TM, D_FF), _rows(TM, D_FF),
                  _resident((D_FF, D_MODEL)), _resident((D_FF, D_MODEL))],
        out_specs=[_rows(TM, D_MODEL), _rows(TM, D_MODEL), _full((1, D_MODEL))],
        out_shape=[jax.ShapeDtypeStruct((R, D_MODEL), BF16), jax.ShapeDtypeStruct((R, D_MODEL), F32),
                   jax.ShapeDtypeStruct((1, D_MODEL), F32)],
        semantics=("arbitrary",),
    )(dh2, h1, g2, dgate, dup, wg_t, wu_t)
    return dgate, dup, hn2, dh1, dg2


def _wgrad(a, b, tm, name, partials=()):
    K, M = a.shape
    N = b.shape[1]
    tk = K // WGRAD_K_TILES if K % (WGRAD_K_TILES * BLOCK) == 0 else TM
    nm, nk, npart = M // tm, K // tk, len(partials)

    def body(a_ref, b_ref, *refs):
        p_ins, o_ref, p_outs, sems = refs[:npart], refs[npart], refs[npart + 1:2 * npart + 1], refs[2 * npart + 1:]
        step = pl.program_id(0) * nk + pl.program_id(1)
        exchange = functools.partial(_chip_copies, p_ins, p_outs, *sems)
        if npart:
            _hosted(step, exchange)

        @pl.when(pl.program_id(1) == 0)
        def _():
            o_ref[...] = jnp.zeros_like(o_ref)

        o_ref[...] += _dot(a_ref[...], b_ref[...].astype(BF16), TN)
        if npart:
            _hosted_wait(step, nm * nk, exchange)

    res = _pcall(
        body, name=name, grid=(nm, nk),
        in_specs=[pl.BlockSpec((tk, tm), lambda m, k: (k, m)), pl.BlockSpec((tk, N), lambda m, k: (k, 0))]
        + [ANY] * npart,
        out_specs=[pl.BlockSpec((tm, N), lambda m, k: (m, 0))] + [ANY] * npart,
        out_shape=[jax.ShapeDtypeStruct((M, N), F32)] + _chip_shapes(partials),
        scratch_shapes=_sem_pair(3 * npart) if npart else [],
        semantics=("arbitrary", "arbitrary"),
    )(a, b, *partials)
    return (res[0], res[1:]) if npart else res[0]


def _outproj_bwd(dh1, o_attn, o_conv, ga, gc, w_out, grads):
    R = dh1.shape[0]
    nt, ng = R // TM, len(grads)

    def body(dh1_ref, oa_ref, oc_ref, ga_ref, gc_ref, w_ref, *refs):
        g_ins, (doa_ref, doc_ref, mixed_ref, dga_ref, dgc_ref) = refs[:ng], refs[ng:ng + 5]
        g_outs, (send_sems, recv_sems) = refs[ng + 5:2 * ng + 5], refs[2 * ng + 5:]
        exchange = functools.partial(_sibling_copies, g_ins, g_outs, send_sems, recv_sems)
        _hosted(pl.program_id(0), exchange)

        @pl.when(pl.program_id(0) == 0)
        def _():
            dga_ref[...] = jnp.zeros_like(dga_ref)
            dgc_ref[...] = jnp.zeros_like(dgc_ref)

        dm = _dot(dh1_ref[...].astype(BF16), w_ref[...], NT)
        oa, oc = oa_ref[...], oc_ref[...]
        ra, rc = _rms_stats(oa), _rms_stats(oc)
        mixed_ref[:, 0:512] = (oa * ra * ga_ref[...]).astype(BF16)
        mixed_ref[:, 512:1024] = (oc * rc * gc_ref[...]).astype(BF16)
        doa, dga = _rms_bwd(dm[:, 0:512], oa, ra, ga_ref[...])
        doc, dgc = _rms_bwd(dm[:, 512:1024], oc, rc, gc_ref[...])
        doa_ref[...] = doa
        doc_ref[...] = doc
        dga_ref[...] += dga
        dgc_ref[...] += dgc
        _hosted_wait(pl.program_id(0), nt, exchange)

    res = _pcall(
        body, name="outproj_bwd", grid=(nt,),
        in_specs=[_rows(TM, D_MODEL), _rows(TM, 512), _rows(TM, 512), _full((1, 512)), _full((1, 512)),
                  _full((D_MODEL, D_MODEL))] + [ANY] * ng,
        out_specs=[_rows(TM, 512), _rows(TM, 512), _rows(TM, D_MODEL), _full((1, 512)), _full((1, 512))]
        + [ANY] * ng,
        out_shape=[jax.ShapeDtypeStruct((R, 512), F32), jax.ShapeDtypeStruct((R, 512), F32),
                   jax.ShapeDtypeStruct((R, D_MODEL), BF16),
                   jax.ShapeDtypeStruct((1, 512), F32), jax.ShapeDtypeStruct((1, 512), F32)]
        + _sibling_shapes(grads),
        scratch_shapes=_sem_pair(ng),
        semantics=("arbitrary",),
    )(dh1, o_attn, o_conv, ga, gc, w_out, *grads)
    return res[:5], res[5:]


def _conv_bwd(do_conv, y, ca, cg, conv_w, ln_g, ln_b, partials):
    R = ca.shape[0]
    nt = R // TM
    hpt = TM // HALO
    npart = len(partials)

    def body(do_ref, doh_ref, y_ref, yh_ref, ca_ref, cg_ref, cah_ref, cgh_ref, w_ref, lg_ref, lb_ref, *refs):
        p_ins, (dca_ref, dcg_ref, dw_ref, db_ref, dlg_ref, dlb_ref) = refs[:npart], refs[npart:npart + 6]
        p_outs, (send_sems, recv_sems, u_s, dy_s, uph_s, dyph_s) = refs[npart + 6:2 * npart + 6], refs[2 * npart + 6:]
        i = pl.program_id(0)
        exchange = functools.partial(_chip_copies, p_ins, p_outs, send_sems, recv_sems)
        _hosted(i, exchange)

        @pl.when(i == 0)
        def _():
            dw_ref[...] = jnp.zeros_like(dw_ref)
            db_ref[...] = jnp.zeros_like(db_ref)
            dlg_ref[...] = jnp.zeros_like(dlg_ref)
            dlb_ref[...] = jnp.zeros_like(dlb_ref)

        lg, lb = lg_ref[...], lb_ref[...]

        def ln_bwd(yv, dov):
            yn, xhat, rstd = _ln_silu(yv, lg, lb)
            sig = _sigmoid(yn)
            dyn = dov * (sig * (1.0 + yn * (1.0 - sig)))
            dxh = dyn * lg
            dyv = rstd * (dxh - jnp.mean(dxh, axis=-1, keepdims=True)
                          - xhat * jnp.mean(dxh * xhat, axis=-1, keepdims=True))
            return dyv, dyn, xhat

        dyv, dyn, xhat = ln_bwd(y_ref[...], do_ref[...])
        dy_s[0:TM, :] = dyv
        dlg_ref[...] += jnp.sum(dyn * xhat, axis=0, keepdims=True)
        dlb_ref[...] += jnp.sum(dyn, axis=0, keepdims=True)
        db_ref[...] += jnp.sum(dyv, axis=0, keepdims=True)
        dyh, _, _ = ln_bwd(yh_ref[...], doh_ref[...])
        dy_s[TM:TM + HALO, :] = jnp.where(i < nt - 1, dyh, 0.0)
        u_s[HALO:HALO + TM, :] = ca_ref[...] * _sigmoid(cg_ref[...])
        u_s[0:HALO, :] = jnp.where(i > 0, cah_ref[...] * _sigmoid(cgh_ref[...]), 0.0)
        _phase_copies(dy_s, dyph_s)
        _phase_copies(u_s, uph_s)

        for rc in range(TM // CONV_ROWS):
            acc = jnp.zeros((CONV_ROWS, CONV_W), F32)
            for k in range(CONV_K):
                acc = acc + _shifted(dy_s, dyph_s, rc * CONV_ROWS + CONV_K - 1 - k, CONV_ROWS) * w_ref[k:k + 1, :]
            rows = slice(rc * CONV_ROWS, (rc + 1) * CONV_ROWS)
            sg = _sigmoid(cg_ref[rows, :])
            dca_ref[rows, :] = (acc * sg).astype(BF16)
            dcg_ref[rows, :] = (acc * ca_ref[rows, :] * sg * (1.0 - sg)).astype(BF16)

        for k in range(CONV_K):
            prod = _shifted(u_s, uph_s, HALO - (CONV_K - 1) + k, TM) * dy_s[0:TM, :]
            dw_ref[k:k + 1, :] += jnp.sum(prod, axis=0, keepdims=True)
        _hosted_wait(i, nt, exchange)

    prev_halo = pl.BlockSpec((HALO, CONV_W), lambda i: (jnp.maximum(i * hpt - 1, 0), 0))
    next_halo = pl.BlockSpec((HALO, CONV_W), lambda i: (jnp.minimum((i + 1) * hpt, nt * hpt - 1), 0))
    vec = jax.ShapeDtypeStruct((1, CONV_W), F32)
    res = _pcall(
        body, name="conv_bwd", grid=(nt,),
        in_specs=[_rows(TM, CONV_W), next_halo, _rows(TM, CONV_W), next_halo,
                  _rows(TM, CONV_W), _rows(TM, CONV_W), prev_halo, prev_halo,
                  _full((CONV_K, CONV_W)), _full((1, CONV_W)), _full((1, CONV_W))] + [ANY] * npart,
        out_specs=[_rows(TM, CONV_W), _rows(TM, CONV_W), _full((32, CONV_W)),
                   _full((1, CONV_W)), _full((1, CONV_W)), _full((1, CONV_W))] + [ANY] * npart,
        out_shape=[jax.ShapeDtypeStruct((R, CONV_W), BF16), jax.ShapeDtypeStruct((R, CONV_W), BF16),
                   jax.ShapeDtypeStruct((32, CONV_W), F32), vec, vec, vec] + _chip_shapes(partials),
        scratch_shapes=_sem_pair(3 * npart)
        + [pltpu.VMEM((HALO + TM, CONV_W), F32), pltpu.VMEM((TM + HALO, CONV_W), F32),
           _phase_scratch(), _phase_scratch()],
        semantics=("arbitrary",),
    )(do_conv, do_conv, y, y, ca, cg, ca, cg, conv_w, ln_g, ln_b, *partials)
    return res[:6], res[6:]


def _attn_bwd(q, kv, sinks, o, lse, do, grads, partials):
    R = q.shape[0]
    nt = R // TM
    ng, npart = len(grads), len(partials)
    nx = ng + npart

    def body(sink_ref, q_ref, kv_ref, kvh_ref, kvm_ref, o_ref, lse_ref, do_ref, *refs):
        x_ins, (dq_ref, dkv_ref, dkvm_ref, dsink_ref) = refs[:nx], refs[nx:nx + 4]
        x_outs = refs[nx + 4:2 * nx + 4]
        g_send, g_recv, p_send, p_recv, carry_s, bias_s, sink_s = refs[2 * nx + 4:]
        i = pl.program_id(0)

        def exchange():
            return (_sibling_copies(x_ins[:ng], x_outs[:ng], g_send, g_recv)
                    + _chip_copies(x_ins[ng:], x_outs[ng:], p_send, p_recv))

        _hosted(i, exchange)

        @pl.when(i == 0)
        def _():
            dkvm_ref[...] = jnp.zeros_like(dkvm_ref)
            carry_s[...] = jnp.zeros_like(carry_s)
            for h in range(N_HEADS):
                dsink_ref[0, h] = 0.0
            _attn_tables(sink_ref, bias_s, sink_s)

        @pl.when(i < nt)
        def _():
            head_of = lax.broadcasted_iota(jnp.int32, (N_HEADS, ATTN_W), 1) // HEAD_DIM
            ind = (head_of == lax.broadcasted_iota(jnp.int32, (N_HEADS, ATTN_W), 0)).astype(BF16)
            dkm = [jnp.zeros((N_META, 2 * KV_W), F32)]
            dsink = [0.0] * N_HEADS
            prev_cur = carry_s[...]
            for s in range(TILE_BLOCKS):
                rows = slice(BLOCK * s, BLOCK * (s + 1))
                kvp = kvh_ref[...] if s == 0 else kv_ref[BLOCK * (s - 1):BLOCK * s, :]
                sel, pen, pen_m = _attn_masks(i * TILE_BLOCKS + s)
                q_t = (q_ref[rows, :] * ATTN_SCALE).T
                do_t = do_ref[rows, :].astype(BF16).T
                kvc_t, kvp_t = kv_ref[rows, :].T, kvp.T
                prod = do_ref[rows, :] * o_ref[rows, :]
                hi = prod.astype(BF16)
                lo = (prod - hi.astype(F32)).astype(BF16)
                delta8 = _dot(ind, hi, NT) + _dot(ind, lo, NT)
                lse8 = lse_ref[:, rows]
                dqs, cur, prev, meta = [], [None] * 4, [None] * 4, [None] * 4
                for g in range(N_HEADS // GROUP):
                    ks = slice(HEAD_DIM * g, HEAD_DIM * (g + 1))
                    vs = slice(KV_W + HEAD_DIM * g, KV_W + HEAD_DIM * (g + 1))
                    lanes = slice(GB * g, GB * (g + 1))
                    qg, dog = _group_lanes(q_t, g), _group_lanes(do_t, g)
                    kc, kp, km = kv_ref[rows, ks], kvp[:, ks], kvm_ref[LEAD:BLOCK, ks]
                    vc, vp, vm = kv_ref[rows, vs], kvp[:, vs], kvm_ref[LEAD:BLOCK, vs]
                    s_b, s_m = _attn_scores(qg, kc, kp, km, sel, pen, pen_m, bias_s[:, lanes])
                    lse, delta = _head_lanes(lse8, g), _head_lanes(delta8, g)
                    p_b = jnp.exp(s_b - lse)
                    p_m = jnp.exp(s_m - lse)
                    dp_b = jnp.where(sel, _dot(vc, dog, NN), _dot(vp, dog, NN))
                    ds_b = p_b * (dp_b - delta)
                    ds_m = (p_m * (_dot(vm, dog, NN) - delta)).astype(BF16)
                    dsk = jnp.exp(sink_s[:, lanes] - lse) * delta
                    for j in range(GROUP):
                        dsink[GROUP * g + j] = dsink[GROUP * g + j] - jnp.sum(dsk[:, BLOCK * j:BLOCK * (j + 1)])
                    ds_c = jnp.where(sel, ds_b, 0.0).astype(BF16)
                    ds_p = jnp.where(sel, 0.0, ds_b).astype(BF16)
                    p_c = jnp.where(sel, p_b, 0.0).astype(BF16)
                    p_p = jnp.where(sel, 0.0, p_b).astype(BF16)
                    dqs.append((_dot(kvc_t[ks, :], ds_c, NN) + _dot(kvp_t[ks, :], ds_p, NN)
                                + _dot(km, ds_m, TN)) * ATTN_SCALE)
                    cur[g], cur[2 + g] = _dot(ds_c, qg, NT), _dot(p_c, dog, NT)
                    prev[g], prev[2 + g] = _dot(ds_p, qg, NT), _dot(p_p, dog, NT)
                    meta[g], meta[2 + g] = _dot(ds_m, qg, NT), _dot(p_m.astype(BF16), dog, NT)
                dq_ref[rows, :] = _head_rows(dqs).astype(BF16).T
                dkv_ref[rows, :] = (prev_cur + jnp.concatenate(prev, axis=1)).astype(BF16)
                prev_cur = jnp.concatenate(cur, axis=1)
                dkm.append(jnp.concatenate(meta, axis=1))
            carry_s[...] = prev_cur
            dkvm_ref[...] += functools.reduce(lambda a, b: a + b, dkm)
            for h in range(N_HEADS):
                dsink_ref[0, h] += dsink[h]

        @pl.when(i == nt)
        def _():
            dkv_ref[...] = jnp.zeros_like(dkv_ref)
            dkv_ref[0:BLOCK, :] = carry_s[...].astype(BF16)

        _hosted_wait(i, nt + 1, exchange)

    tile = lambda cols: pl.BlockSpec((TM, cols), lambda i: (jnp.minimum(i, nt - 1), 0))
    res = _pcall(
        body, name="attn_bwd", grid=(nt + 1,),
        in_specs=[pl.BlockSpec(memory_space=pltpu.SMEM), tile(512), tile(256),
                  pl.BlockSpec((BLOCK, 256), lambda i: (jnp.clip(i * TILE_BLOCKS - 1, 0, R // BLOCK - 1), 0)),
                  _full((BLOCK, 256)), tile(512),
                  pl.BlockSpec((N_HEADS, TM), lambda i: (0, jnp.minimum(i, nt - 1))), tile(512)]
        + [ANY] * nx,
        out_specs=[tile(512), _rows(TM, 256), _full((N_META, 256)), pl.BlockSpec(memory_space=pltpu.SMEM)]
        + [ANY] * nx,
        out_shape=[jax.ShapeDtypeStruct((R, 512), BF16), jax.ShapeDtypeStruct((R + TM, 256), BF16),
                   jax.ShapeDtypeStruct((N_META, 256), F32), jax.ShapeDtypeStruct((1, N_HEADS), F32)]
        + _sibling_shapes(grads) + _chip_shapes(partials),
        scratch_shapes=_sem_pair(ng) + _sem_pair(3 * npart)
        + [pltpu.VMEM((BLOCK, 256), F32), pltpu.VMEM((BLOCK, N_HEADS * BLOCK), F32),
           pltpu.VMEM((1, N_HEADS * BLOCK), F32)],
        semantics=("arbitrary",),
    )(sinks, q, kv, kv, kv, o, lse, do, *grads, *partials)
    return res[:4], res[4:4 + ng], res[4 + ng:]


def _inproj_bwd(dh1, h0, g1, dq, dkv, dkvm, dca, dcg, w_in_t):
    R = h0.shape[0]
    nt = R // TM
    assert nt >= 2

    def body(dh1_ref, h0_ref, g_ref, dq_ref, dkv_ref, dkvm_ref, dca_ref, dcg_ref, w_ref,
             gx_hbm, dmeta_ref, dproj_ref, hn_ref, dg_ref, dx_s, gx_sems):
        i = pl.program_id(0)

        def gx_copy(step, slot, first):
            if first:
                return pltpu.make_async_copy(dx_s.at[slot, pl.ds(BLOCK, TM - BLOCK)],
                                             gx_hbm.at[pl.ds(0, TM - BLOCK)], gx_sems.at[slot])
            return pltpu.make_async_copy(
                dx_s.at[slot], gx_hbm.at[pl.ds(pl.multiple_of(step * TM - BLOCK, BLOCK), TM)], gx_sems.at[slot])

        @pl.when(i == 0)
        def _():
            dg_ref[...] = jnp.zeros_like(dg_ref)

        dproj_ref[:, 0:512] = dq_ref[...]
        dproj_ref[:, 512:768] = dkv_ref[...]
        dproj_ref[:, 768:1280] = dca_ref[...]
        dproj_ref[:, 1280:1792] = dcg_ref[...]

        @pl.when(i == 0)
        def _():
            dproj_ref[LEAD:BLOCK, 512:768] = dkvm_ref[...].astype(BF16)

        dhn = _dot(dproj_ref[...], w_ref[...], NN)
        h = h0_ref[...]
        r = _rms_stats(h)
        g = g_ref[...]
        hn_ref[...] = (h * r * g).astype(BF16)
        dx, dg = _rms_bwd(dhn, h, r, g)
        dg_ref[...] += dg
        slot = i % 2
        pl.when(i == 2)(lambda: gx_copy(0, 0, True).wait())
        pl.when(i > 2)(lambda: gx_copy(i - 2, slot, False).wait())
        dx_s[slot] = dh1_ref[...] + dx

        @pl.when(i == 0)
        def _():
            dmeta_ref[...] = dx_s[0, LEAD:BLOCK, :]
            gx_copy(0, 0, True).start()

        pl.when(i > 0)(lambda: gx_copy(i, slot, False).start())

        @pl.when(i == nt - 1)
        def _():
            gx_copy(nt - 2, (nt - 2) % 2, nt == 2).wait()
            gx_copy(nt - 1, (nt - 1) % 2, False).wait()

    return _pcall(
        body, name="inproj_bwd", grid=(nt,),
        in_specs=[_rows(TM, D_MODEL), _rows(TM, D_MODEL), _full((1, D_MODEL)), _rows(TM, 512), _rows(TM, 256),
                  _full((N_META, 256)), _rows(TM, 512), _rows(TM, 512), _full((1792, D_MODEL))],
        out_specs=[ANY, _full((N_META, D_MODEL)), _rows(TM, 1792), _rows(TM, D_MODEL), _full((1, D_MODEL))],
        out_shape=[jax.ShapeDtypeStruct((R - BLOCK, D_MODEL), F32), jax.ShapeDtypeStruct((N_META, D_MODEL), F32),
                   jax.ShapeDtypeStruct((R, 1792), BF16),
                   jax.ShapeDtypeStruct((R, D_MODEL), BF16), jax.ShapeDtypeStruct((1, D_MODEL), F32)],
        scratch_shapes=[pltpu.VMEM((2, TM, D_MODEL), F32), pltpu.SemaphoreType.DMA((2,))],
        semantics=("arbitrary",),
    )(dh1, h0, g1, dq, dkv, dkvm, dca, dcg, w_in_t)


ANY = pl.BlockSpec(memory_space=pl.ANY)


def _position():
    return lax.axis_index("x"), lax.axis_index("y"), lax.axis_index("c")


def _device_number(p):
    return 4 * p[0] + 2 * p[1] + p[2]


def _two_level_allgather(ins, outs, block, send_sems, recv_sems, local_sems, sem_base=0):
    n = len(ins)
    x, y, c = _position()
    me, sibling = (x, y, c), (x, y, 1 - c)
    chips = [(1 - x, y), (x, 1 - y), (1 - x, 1 - y)]

    def copy(w, k, origin, to, src=None):
        return pltpu.make_async_remote_copy(
            src_ref=block(w, origin) if src is None else src, dst_ref=block(w, origin),
            send_sem=send_sems.at[sem_base + 7 * w + k], recv_sem=recv_sems.at[sem_base + 7 * w + k],
            device_id=to, device_id_type=MESH)

    def mine(w):
        return pltpu.make_async_copy(ins[w], block(w, me), local_sems.at[w])

    def own(w):
        return [copy(w, 0, me, sibling, src=ins[w])] + [
            copy(w, 1 + j, me, (*chip, c), src=ins[w]) for j, chip in enumerate(chips)]

    def passed(w):
        return [copy(w, 4 + j, (*chip, c), sibling) for j, chip in enumerate(chips)]

    def start():
        for w in range(n):
            mine(w).start()
        for w in range(n):
            for cp in own(w):
                cp.start()

    def forward(w):
        fw = passed(w)
        for j, chip in enumerate(chips):
            copy(w, 1 + j, (*chip, c), me).wait_recv()
            fw[j].start()

    def finish():
        for w in range(n):
            copy(w, 0, sibling, me).wait_recv()
            for j, chip in enumerate(chips):
                copy(w, 4 + j, (*chip, 1 - c), me).wait_recv()
        for w in range(n):
            for cp in own(w) + passed(w):
                cp.wait_send()
            mine(w).wait()

    return start, forward, finish


def _carried_allgather(step, n_steps, shards, refs):
    ns = len(shards)
    ins, outs, (send_sems, recv_sems, local_sems) = refs[:ns], refs[ns:2 * ns], refs[2 * ns:]
    start, forward, finish = _two_level_allgather(
        ins, outs, _row_block(outs, [s.shape[0] for s in shards]), send_sems, recv_sems, local_sems)
    pl.when(step == 0)(start)
    total = sum(s.shape[0] for s in shards)
    sent = 0
    for w, s in enumerate(shards):
        sent += s.shape[0]
        pl.when(step == (AG_FORWARD_AT * sent * (n_steps - 1)) // (100 * total))(functools.partial(forward, w))
    return lambda: pl.when(step == n_steps - 1)(finish)


def _gathered_shapes(shards):
    return [jax.ShapeDtypeStruct((N_DEV * s.shape[0], s.shape[1]), s.dtype) for s in shards]


def _allgather_sems(ns):
    return _sem_pair(7 * ns) + [pltpu.SemaphoreType.DMA((ns,))]


def _blocking_allgather(ins, outs, block, send_sems, recv_sems, local_sems, sem_base=0):
    start, forward, finish = _two_level_allgather(ins, outs, block, send_sems, recv_sems, local_sems, sem_base)
    start()
    for w in range(len(ins)):
        forward(w)
    finish()


def _row_block(outs, rows):
    def block(w, p):
        return outs[w].at[pl.ds(pl.multiple_of(_device_number(p) * rows[w], 16), rows[w])]
    return block


def _sibling_copies(ins, outs, send_sems, recv_sems):
    x, y, c = _position()
    return [pltpu.make_async_remote_copy(
        src_ref=ins[w].at[:, 1 - c], dst_ref=outs[w], send_sem=send_sems.at[w], recv_sem=recv_sems.at[w],
        device_id=(x, y, 1 - c), device_id_type=MESH) for w in range(len(ins))]


def _chip_copies(ins, outs, send_sems, recv_sems):
    x, y, c = _position()
    chips = [(1 - x, y), (x, 1 - y), (1 - x, 1 - y)]
    return [pltpu.make_async_remote_copy(
        src_ref=ins[w].at[2 * chip[0] + chip[1]], dst_ref=outs[w].at[k],
        send_sem=send_sems.at[3 * w + k], recv_sem=recv_sems.at[3 * w + k],
        device_id=(*chip, c), device_id_type=MESH) for w in range(len(ins)) for k, chip in enumerate(chips)]


def _hosted(step, make_copies):
    @pl.when(step == 0)
    def _():
        for cp in make_copies():
            cp.start()


def _hosted_wait(step, n_steps, make_copies):
    @pl.when(step == n_steps - 1)
    def _():
        for cp in make_copies():
            cp.wait()


def _sem_pair(n):
    return [pltpu.SemaphoreType.DMA((n,)), pltpu.SemaphoreType.DMA((n,))]


def _allgather_params(shards, small):
    arrays = list(shards) + list(small)
    n, ns = len(arrays), len(shards)

    def body(*refs):
        ins, outs = refs[:n], refs[n:2 * n]
        send_sems, recv_sems, local_sems = refs[2 * n:]

        rows = _row_block(outs, [a.shape[0] for a in arrays])

        def block(w, p):
            return rows(w, p) if w < ns else outs[w].at[_device_number(p)]

        _blocking_allgather(ins, outs, block, send_sems, recv_sems, local_sems)

    out_shape = [jax.ShapeDtypeStruct((N_DEV * a.shape[0], a.shape[1]), a.dtype) for a in shards]
    out_shape += [jax.ShapeDtypeStruct((N_DEV,) + a.shape, a.dtype) for a in small]
    return _pcall(
        body, name="allgather_params", in_specs=[ANY] * n, out_specs=[ANY] * n, out_shape=out_shape,
        scratch_shapes=[pltpu.SemaphoreType.DMA((7 * n,)), pltpu.SemaphoreType.DMA((7 * n,)),
                        pltpu.SemaphoreType.DMA((n,))],
    )(*arrays)


def _sibling_shapes(grads):
    return [jax.ShapeDtypeStruct((4,) + g.shape[2:], F32) for g in grads]


def _chip_shapes(partials):
    return [jax.ShapeDtypeStruct((3,) + p.shape[1:], F32) for p in partials]


def _add_sibling(grad, received, core, name):
    _, _, r, cols = grad.shape

    def body(core_ref, g_ref, r_ref, o_ref):
        o_ref[...] = g_ref[...] + r_ref[...]

    return pl.pallas_call(
        body, name=name,
        grid_spec=pltpu.PrefetchScalarGridSpec(
            num_scalar_prefetch=1, grid=(4,),
            in_specs=[pl.BlockSpec((None, None, r, cols), lambda s, core_ref: (s, core_ref[0], 0, 0)),
                      pl.BlockSpec((None, r, cols), lambda s, core_ref: (s, 0, 0))],
            out_specs=pl.BlockSpec((None, r, cols), lambda s, core_ref: (s, 0, 0))),
        out_shape=jax.ShapeDtypeStruct((4, r, cols), F32),
        compiler_params=pltpu.CompilerParams(vmem_limit_bytes=VMEM_LIMIT),
    )(core, grad, received)


def _reduce_last(grad, small):
    _, _, r, cols = grad.shape
    ns = len(small)

    def body(g_ref, *refs):
        s_ins, (sums_out, chips_out), s_outs = refs[:ns], refs[ns:ns + 2], refs[ns + 2:2 * ns + 2]
        sib_send, sib_recv, chip_send, chip_recv, ag_send, ag_recv, local_sems, sums_s, recv_s = refs[2 * ns + 2:]
        x, y, c = _position()
        start, forward, finish = _two_level_allgather(
            s_ins, s_outs, lambda w, p: s_outs[w].at[_device_number(p)], ag_send, ag_recv, local_sems)
        mine = pltpu.make_async_copy(g_ref.at[:, c], sums_s, local_sems.at[ns])
        sibling = pltpu.make_async_remote_copy(
            src_ref=g_ref.at[:, 1 - c], dst_ref=recv_s, send_sem=sib_send, recv_sem=sib_recv,
            device_id=(x, y, 1 - c), device_id_type=MESH)
        start()
        mine.start()
        sibling.start()
        mine.wait()
        sibling.wait()
        sums_s[...] += recv_s[...]
        keep = pltpu.make_async_copy(sums_s, sums_out, local_sems.at[ns + 1])
        keep.start()
        copies = _chip_copies([sums_s], [chips_out], chip_send, chip_recv)
        for cp in copies:
            cp.start()
        for w in range(ns):
            forward(w)
        finish()
        for cp in copies:
            cp.wait()
        keep.wait()

    block = jax.ShapeDtypeStruct((4, r, cols), F32)
    dma = pltpu.SemaphoreType.DMA
    return _pcall(
        body, name="reduce_last", in_specs=[ANY] * (1 + ns), out_specs=[ANY] * (2 + ns),
        out_shape=[block, jax.ShapeDtypeStruct((3, r, cols), F32)]
        + [jax.ShapeDtypeStruct((N_DEV,) + a.shape, a.dtype) for a in small],
        scratch_shapes=[dma, dma, dma((3,)), dma((3,)), dma((7 * ns,)), dma((7 * ns,)), dma((ns + 2,)),
                        pltpu.VMEM((4, r, cols), F32), pltpu.VMEM((4, r, cols), F32)],
    )(grad, *small)


def _adam(g, w, m, v):
    m = ADAM_B1 * m + (1.0 - ADAM_B1) * g
    v = ADAM_B2 * v + (1.0 - ADAM_B2) * (g * g)
    m_hat = m / (1.0 - ADAM_B1 ** ADAM_STEP)
    v_hat = v / (1.0 - ADAM_B2 ** ADAM_STEP)
    delta = -ADAM_LR * (m_hat / (jnp.sqrt(v_hat) + ADAM_EPS) + ADAM_WD * w)
    return delta, m, v


def _adamw(partial, received, slot, w, m, v, name):
    _, r, cols = partial.shape

    def body(slot_ref, p_ref, r_ref, w_ref, m_ref, v_ref, g_ref, d_ref, nm_ref, nv_ref):
        g = p_ref[...] + r_ref[0] + r_ref[1] + r_ref[2]
        g_ref[...] = g
        d_ref[...], nm_ref[...], nv_ref[...] = _adam(g, w_ref[...], m_ref[...], v_ref[...])

    whole = pl.BlockSpec((r, cols), lambda i, slot_ref: (0, 0))
    out = jax.ShapeDtypeStruct((r, cols), F32)
    return pl.pallas_call(
        body, name=name,
        grid_spec=pltpu.PrefetchScalarGridSpec(
            num_scalar_prefetch=1, grid=(1,),
            in_specs=[pl.BlockSpec((None, r, cols), lambda i, slot_ref: (slot_ref[0], 0, 0)),
                      pl.BlockSpec((3, r, cols), lambda i, slot_ref: (0, 0, 0)), whole, whole, whole],
            out_specs=[whole, whole, whole, whole]),
        out_shape=[out, out, out, out],
        compiler_params=pltpu.CompilerParams(vmem_limit_bytes=VMEM_LIMIT),
    )(slot, partial, received, w, m, v)


def _adamw_small(dev, ga, gb, gc, params):
    names = ["meta", "attn_norm", "sinks", "conv_w", "conv_b", "ln_g", "ln_b", "attn_out", "conv_out",
             "ffn_norm", "final_norm"]
    flat = [a for p in params for a in p]
    n_in = len(flat)

    def body(dev_ref, ga_ref, gb_ref, gc_ref, *refs):
        ins, outs = refs[:n_in], refs[n_in:n_in + 4 * len(names)]
        loss_ref, sb, sc = refs[n_in + 4 * len(names):]
        a = ga_ref[0]
        sb[...] = gb_ref[0]
        sc[...] = gc_ref[0]
        for d in range(1, N_DEV):
            a = a + ga_ref[d]
            sb[...] += gb_ref[d]
            sc[...] += gc_ref[d]
        dev = dev_ref[0]
        grads = {
            "attn_norm": a[0:1, :], "ffn_norm": a[1:2, :], "final_norm": a[2:3, :],
            "conv_b": a[3:4, 0:512], "ln_g": a[3:4, 512:1024], "ln_b": a[4:5, 0:512],
            "attn_out": a[4:5, 512:1024], "conv_out": a[5:6, 0:512], "sinks": a[5:6, 512:512 + N_HEADS],
            "meta": sb[pl.ds(pl.multiple_of(dev * N_META, N_META), N_META), :],
            "conv_w": sc[pl.ds(pl.multiple_of(dev * 32, 32), 32), :][0:CONV_K, :],
        }
        for idx, nm in enumerate(names):
            w_ref, m_ref, v_ref = ins[3 * idx:3 * idx + 3]
            g = grads[nm]
            delta, m, v = _adam(g, w_ref[...], m_ref[...], v_ref[...])
            o = outs[4 * idx:4 * idx + 4]
            o[0][...], o[1][...], o[2][...], o[3][...] = g, delta, m, v
        loss_ref[...] = a[6:7, 0:1]

    vm = pl.BlockSpec(memory_space=pltpu.VMEM)
    out_shape = [jax.ShapeDtypeStruct(p[0].shape, F32) for p in params for _ in range(4)]
    out_shape.append(jax.ShapeDtypeStruct((1, 1), F32))
    res = pl.pallas_call(
        body, name="adamw_small",
        grid_spec=pltpu.PrefetchScalarGridSpec(
            num_scalar_prefetch=1, grid=(1,),
            in_specs=[pl.BlockSpec(ga.shape, lambda i, d: (0, 0, 0)), pl.BlockSpec(gb.shape, lambda i, d: (0, 0, 0)),
                      pl.BlockSpec(gc.shape, lambda i, d: (0, 0, 0))]
            + [pl.BlockSpec(a.shape, lambda i, d: (0, 0)) for a in flat],
            out_specs=[pl.BlockSpec(s.shape, lambda i, d: (0, 0)) for s in out_shape],
            scratch_shapes=[pltpu.VMEM(gb.shape[1:], F32), pltpu.VMEM(gc.shape[1:], F32)]),
        out_shape=out_shape,
        compiler_params=pltpu.CompilerParams(vmem_limit_bytes=VMEM_LIMIT),
    )(dev, ga, gb, gc, *flat)
    return [res[4 * i:4 * i + 4] for i in range(len(names))], res[-1]


def kernel(x, meta_tokens, attn_norm_g, w_in, attn_sinks, conv_w, conv_b, conv_ln_g, conv_ln_b, attn_out_g, conv_out_g, w_out, ffn_norm_g, w_gate, w_up, w_down, final_norm_g, loss_target, m_meta_tokens, m_attn_norm_g, m_w_in, m_attn_sinks, m_conv_w, m_conv_b, m_conv_ln_g, m_conv_ln_b, m_attn_out_g, m_conv_out_g, m_w_out, m_ffn_norm_g, m_w_gate, m_w_up, m_w_down, m_final_norm_g, v_meta_tokens, v_attn_norm_g, v_w_in, v_attn_sinks, v_conv_w, v_conv_b, v_conv_ln_g, v_conv_ln_b, v_attn_out_g, v_conv_out_g, v_w_out, v_ffn_norm_g, v_w_gate, v_w_up, v_w_down, v_final_norm_g):
    xi, yi, ci = _position()
    dev = jnp.reshape(_device_number((xi, yi, ci)), (1,)).astype(jnp.int32)
    core = jnp.reshape(ci, (1,)).astype(jnp.int32)
    slot = jnp.reshape(2 * xi + yi, (1,)).astype(jnp.int32)

    w_in_t, meta_st, convw_st = _allgather_params([w_in[0].T.astype(BF16)], [meta_tokens, conv_w[0]])
    meta_full = jnp.transpose(meta_st, (1, 0, 2)).reshape(N_META, D_MODEL)
    convw_full = jnp.transpose(convw_st, (1, 0, 2)).reshape(CONV_K, CONV_W)

    final_g = final_norm_g.reshape(1, D_MODEL)

    h0, q, kv, ca, cg = _inproj_fwd(x[0], meta_full, attn_norm_g, w_in_t)
    o_attn, lse, (wg_t,) = _attn_fwd(q, kv, attn_sinks, [w_gate[0].T.astype(BF16)])
    (o_conv, y_conv), (w_out_b, wd_b) = _conv_fwd(
        ca, cg, convw_full, conv_b, conv_ln_g, conv_ln_b, [w_out[0].astype(BF16), w_down[0].astype(BF16)])
    h1, (wu_t,) = _outproj_fwd(h0, o_attn, o_conv, attn_out_g, conv_out_g, w_out_b, [w_up[0].T.astype(BF16)])
    gate, up, act, dh2, loss_sum, dg_final = _ffn_fwd(h1, ffn_norm_g, wg_t, wu_t, wd_b, final_g, loss_target[0])

    def blocks(g):
        return g.reshape(4, 2, g.shape[0] // N_DEV, D_MODEL)

    def add_siblings(grads, received, tags):
        return [_add_sibling(g, r, core, "add_sibling_" + t) for g, r, t in zip(grads, received, tags)]

    dgate, dup, hn2, dh1, dg_ffn = _ffn_bwd(dh2, h1, ffn_norm_g, gate, up, wg_t, wu_t, wd_b)
    ffn_grads = [blocks(_wgrad(dgate, hn2, FF_CHUNK, "wgrad_gate")), blocks(_wgrad(dup, hn2, FF_CHUNK, "wgrad_up")),
                 blocks(_wgrad(act, dh2, FF_CHUNK, "wgrad_down"))]
    (do_attn, do_conv, mixed, dg_ao, dg_co), ffn_sib = _outproj_bwd(
        dh1, o_attn, o_conv, attn_out_g, conv_out_g, w_out_b, ffn_grads)
    ffn_sums = add_siblings(ffn_grads, ffn_sib, ("gate", "up", "down"))
    out_grads = [blocks(_wgrad(mixed, dh1, D_MODEL, "wgrad_out"))]
    (dca, dcg, dconvw, dconvb, dln_g, dln_b), gate_up_chips = _conv_bwd(
        do_conv, y_conv, ca, cg, convw_full, conv_ln_g, conv_ln_b, ffn_sums[:2])
    (dq, dkv_shifted, dkvm, dsinks), out_sib, down_chips = _attn_bwd(
        q, kv, attn_sinks, o_attn, lse, do_attn, out_grads, ffn_sums[2:])
    dkv = dkv_shifted[BLOCK:BLOCK + dq.shape[0]]
    ffn_chips = list(gate_up_chips) + list(down_chips)
    out_sums = add_siblings(out_grads, out_sib, ("out",))
    grad_x, dmeta, dproj, hn1, dg_attn = _inproj_bwd(dh1, h0, attn_norm_g, dq, dkv, dkvm, dca, dcg, w_in_t)
    dwi_t, out_chips = _wgrad(dproj, hn1, 1792, "wgrad_in", out_sums)
    small_a = jnp.concatenate([
        dg_attn, dg_ffn, dg_final, jnp.concatenate([dconvb, dln_g], axis=1), jnp.concatenate([dln_b, dg_ao], axis=1),
        jnp.concatenate([dg_co, dsinks, jnp.zeros((1, 512 - N_HEADS), F32)], axis=1),
        jnp.concatenate([loss_sum[0:1, :], jnp.zeros((1, D_MODEL - 128), F32)], axis=1),
        jnp.zeros((1, D_MODEL), F32)], axis=0)
    small_b = jnp.transpose(dmeta.reshape(N_META, N_DEV, 128), (1, 0, 2)).reshape(N_DEV * N_META, 128)
    small_c = jnp.transpose(dconvw.reshape(32, N_DEV, 64), (1, 0, 2)).reshape(N_DEV * 32, 64)
    in_sum, in_chips, ga, gb, gc = _reduce_last(blocks(dwi_t), [small_a, small_b, small_c])
    tags = ("in", "out", "gate", "up", "down")
    chip_sums = [in_sum] + out_sums + ffn_sums
    from_chips = [in_chips] + list(out_chips) + list(ffn_chips)

    big = [(True, w_in, m_w_in, v_w_in), (False, w_out, m_w_out, v_w_out), (True, w_gate, m_w_gate, v_w_gate),
           (True, w_up, m_w_up, v_w_up), (False, w_down, m_w_down, v_w_down)]
    big_out = {}
    for t, p, r, (transposed, w, m, v) in zip(tags, chip_sums, from_chips, big):
        rows = (lambda a: jnp.transpose(a[0])) if transposed else (lambda a: a[0])
        back = (lambda a: jnp.transpose(a)[None]) if transposed else (lambda a: a[None])
        big_out[t] = [back(a) for a in _adamw(p, r, slot, rows(w), rows(m), rows(v), "adamw_" + t)]

    small_params = [
        (meta_tokens, m_meta_tokens, v_meta_tokens), (attn_norm_g, m_attn_norm_g, v_attn_norm_g),
        (attn_sinks, m_attn_sinks, v_attn_sinks), (conv_w[0], m_conv_w[0], v_conv_w[0]),
        (conv_b, m_conv_b, v_conv_b), (conv_ln_g, m_conv_ln_g, v_conv_ln_g), (conv_ln_b, m_conv_ln_b, v_conv_ln_b),
        (attn_out_g, m_attn_out_g, v_attn_out_g), (conv_out_g, m_conv_out_g, v_conv_out_g),
        (ffn_norm_g, m_ffn_norm_g, v_ffn_norm_g),
        (final_g, m_final_norm_g.reshape(1, D_MODEL), v_final_norm_g.reshape(1, D_MODEL))]
    sm, loss = _adamw_small(dev, ga, gb, gc, small_params)
    sm[3] = [a[None] for a in sm[3]]
    sm[10] = [a.reshape(D_MODEL) for a in sm[10]]

    per_param = [sm[0], sm[1], big_out["in"], sm[2], sm[3], sm[4], sm[5], sm[6], sm[7], sm[8], big_out["out"],
                 sm[9], big_out["gate"], big_out["up"], big_out["down"], sm[10]]
    loss = loss.reshape(())
    outs = [loss, grad_x[None]]
    for kind in range(4):
        outs += [p[kind] for p in per_param]
    return tuple(outs)
```

```python
import functools
import math

import jax
import jax.numpy as jnp
from jax import lax
from jax.experimental import pallas as pl
from jax.experimental.pallas import tpu as pltpu

F32, BF16 = jnp.float32, jnp.bfloat16
MESH = pl.DeviceIdType.MESH

D_MODEL = 1024
N_META = 16
BLOCK = 128
LEAD = BLOCK - N_META
HEAD_DIM = 64
N_HEADS = 8
GROUP = 4
ATTN_W = 512
KV_W = 128
CONV_W = 512
CONV_K = 31
HALO = 32
D_FF = 2816
FF_CHUNK = D_FF // 2
FF_SUB = [slice(s, s + 256) for s in range(0, D_FF, 256)]
N_DEV = 8
EPS = 1e-5
NEG = -1e30
TM = 640
TILE_BLOCKS = TM // BLOCK
AG_FORWARD_AT = 85
WGRAD_K_TILES = 5
CONV_ROWS = 32
VMEM_LIMIT = 56 * 1024 * 1024

ADAM_LR, ADAM_B1, ADAM_B2, ADAM_EPS, ADAM_WD, ADAM_STEP = 0.001, 0.9, 0.999, 1e-08, 0.01, 10

NT = (((1,), (1,)), ((), ()))
NN = (((1,), (0,)), ((), ()))
TN = (((0,), (0,)), ((), ()))


def _dot(a, b, dims):
    return lax.dot_general(a, b, dims, preferred_element_type=F32)


def _sigmoid(x):
    return 1.0 / (1.0 + jnp.exp(-x))


def _pcall(body, *, name, out_shape, grid=None, in_specs=None, out_specs=None, scratch_shapes=(),
           semantics=None, **kw):
    params = dict(vmem_limit_bytes=VMEM_LIMIT)
    if semantics is not None:
        params["dimension_semantics"] = semantics
    extra = {}
    if grid is not None:
        extra["grid"] = grid
    if in_specs is not None:
        extra["in_specs"] = in_specs
    if out_specs is not None:
        extra["out_specs"] = out_specs
    return pl.pallas_call(body, name=name, out_shape=out_shape, scratch_shapes=list(scratch_shapes),
                          compiler_params=pltpu.CompilerParams(**params), **extra, **kw)


def _rows(tm, cols):
    return pl.BlockSpec((tm, cols), lambda i, *_: (i, 0))


def _full(shape):
    nd = len(shape)
    return pl.BlockSpec(shape, lambda *_: (0,) * nd)


def _rms_stats(x):
    return lax.rsqrt(jnp.mean(x * x, axis=-1, keepdims=True) + EPS)


def _rms_bwd(dy, x, r, g):
    t = dy * g
    dx = r * (t - x * (r * r) * jnp.mean(t * x, axis=-1, keepdims=True))
    dg = jnp.sum(dy * x * r, axis=0, keepdims=True)
    return dx, dg


def _inproj_fwd(x, meta, g1, w_in_t):
    R = x.shape[0] + BLOCK
    nt = R // TM
    assert nt >= 2

    def body(x_hbm, meta_ref, g_ref, w_ref, h0_ref, q_ref, kv_ref, ca_ref, cg_ref, x_s, sems):
        i = pl.program_id(0)
        slot = i % 2

        def x_copy(step, slot, first):
            if first:
                return pltpu.make_async_copy(x_hbm.at[pl.ds(0, TM - BLOCK)],
                                             x_s.at[slot, pl.ds(BLOCK, TM - BLOCK)], sems.at[slot])
            return pltpu.make_async_copy(
                x_hbm.at[pl.ds(pl.multiple_of(step * TM - BLOCK, BLOCK), TM)], x_s.at[slot], sems.at[slot])

        @pl.when(i == 0)
        def _():
            x_copy(0, 0, True).start()
            x_s[0, 0:LEAD, :] = jnp.zeros((LEAD, D_MODEL), F32)
            x_s[0, LEAD:BLOCK, :] = meta_ref[...]

        pl.when(i + 1 < nt)(lambda: x_copy(i + 1, 1 - slot, False).start())
        pl.when(i == 0)(lambda: x_copy(0, 0, True).wait())
        pl.when(i > 0)(lambda: x_copy(i, slot, False).wait())
        h = x_s[slot]
        h0_ref[...] = h
        hn = (h * _rms_stats(h) * g_ref[...]).astype(BF16)
        q_ref[...] = _dot(hn, w_ref[0:512, :], NT).astype(BF16)
        kv_ref[...] = _dot(hn, w_ref[512:768, :], NT).astype(BF16)
        ca_ref[...] = _dot(hn, w_ref[768:1280, :], NT)
        cg_ref[...] = _dot(hn, w_ref[1280:1792, :], NT)

    return _pcall(
        body, name="inproj_fwd", grid=(nt,),
        in_specs=[pl.BlockSpec(memory_space=pl.ANY), _full((N_META, D_MODEL)), _full((1, D_MODEL)),
                  _full((1792, D_MODEL))],
        out_specs=[_rows(TM, D_MODEL), _rows(TM, 512), _rows(TM, 256), _rows(TM, 512), _rows(TM, 512)],
        out_shape=[jax.ShapeDtypeStruct((R, D_MODEL), F32),
                   jax.ShapeDtypeStruct((R, 512), BF16), jax.ShapeDtypeStruct((R, 256), BF16),
                   jax.ShapeDtypeStruct((R, 512), F32), jax.ShapeDtypeStruct((R, 512), F32)],
        scratch_shapes=[pltpu.VMEM((2, TM, D_MODEL), F32), pltpu.SemaphoreType.DMA((2,))],
        semantics=("arbitrary",),
    )(x, meta, g1, w_in_t)


GB = GROUP * BLOCK
ATTN_SCALE = 1.0 / math.sqrt(HEAD_DIM)


def _group_lanes(xt, g):
    return jnp.concatenate(
        [xt[HEAD_DIM * (GROUP * g + j):HEAD_DIM * (GROUP * g + j + 1), :] for j in range(GROUP)], axis=1)


def _head_lanes(ref, g):
    return jnp.concatenate([ref[GROUP * g + j:GROUP * g + j + 1, :] for j in range(GROUP)], axis=1)


def _head_rows(xs):
    return jnp.concatenate([x[:, BLOCK * j:BLOCK * (j + 1)] for x in xs for j in range(GROUP)], axis=0)


def _attn_tables(sink_ref, bias_s, sink_s):
    kk = lax.broadcasted_iota(jnp.int32, (BLOCK, BLOCK), 0)
    ii = lax.broadcasted_iota(jnp.int32, (BLOCK, BLOCK), 1)
    dist = jnp.where(kk <= ii, ii - kk, ii - kk + BLOCK).astype(F32)
    for h in range(N_HEADS):
        bias_s[:, BLOCK * h:BLOCK * (h + 1)] = dist * -(2.0 ** -(h + 1))
        sink_s[:, BLOCK * h:BLOCK * (h + 1)] = jnp.zeros((1, BLOCK), F32) + sink_ref[0, h]


def _attn_masks(b):
    kk = lax.broadcasted_iota(jnp.int32, (BLOCK, GB), 0)
    ii = lax.broadcasted_iota(jnp.int32, (BLOCK, GB), 1) & (BLOCK - 1)
    sel = kk <= ii
    pen = jnp.where(sel, jnp.where(b >= 1, 0.0, NEG), jnp.where(b >= 2, 0.0, NEG))
    mj = lax.broadcasted_iota(jnp.int32, (N_META, GB), 0)
    mi = lax.broadcasted_iota(jnp.int32, (N_META, GB), 1) & (BLOCK - 1)
    pen_m = jnp.where((mj + LEAD) <= (mi + b * BLOCK), 0.0, NEG)
    return sel, pen, pen_m


def _attn_scores(qt, kc, kp, km, sel, pen, pen_m, bias):
    s_b = jnp.where(sel, _dot(kc, qt, NN), _dot(kp, qt, NN)) + bias + pen
    s_m = _dot(km, qt, NN) + pen_m
    return s_b, s_m


def _attn_fwd(q, kv, sinks, shards):
    R = q.shape[0]
    nt = R // TM
    ns = len(shards)

    def body(sink_ref, q_ref, kv_ref, kvh_ref, kvm_ref, *refs):
        ag_ins, (o_ref, lse_ref), ag_outs = refs[:ns], refs[ns:ns + 2], refs[ns + 2:2 * ns + 2]
        ag_sems, (bias_s, sink_s) = refs[2 * ns + 2:2 * ns + 5], refs[2 * ns + 5:]
        i = pl.program_id(0)
        ag_finish = _carried_allgather(i, nt, shards, ag_ins + ag_outs + ag_sems)
        pl.when(i == 0)(functools.partial(_attn_tables, sink_ref, bias_s, sink_s))
        for s in range(TILE_BLOCKS):
            rows = slice(BLOCK * s, BLOCK * (s + 1))
            kvp = kvh_ref[...] if s == 0 else kv_ref[BLOCK * (s - 1):BLOCK * s, :]
            sel, pen, pen_m = _attn_masks(i * TILE_BLOCKS + s)
            q_t = (q_ref[rows, :] * ATTN_SCALE).T
            kvc_t, kvp_t = kv_ref[rows, :].T, kvp.T
            outs = []
            for g in range(N_HEADS // GROUP):
                ks = slice(HEAD_DIM * g, HEAD_DIM * (g + 1))
                vs = slice(KV_W + HEAD_DIM * g, KV_W + HEAD_DIM * (g + 1))
                lanes = slice(GB * g, GB * (g + 1))
                s_b, s_m = _attn_scores(_group_lanes(q_t, g), kv_ref[rows, ks], kvp[:, ks], kvm_ref[LEAD:BLOCK, ks],
                                        sel, pen, pen_m, bias_s[:, lanes])
                sink = sink_s[:, lanes]
                m = jnp.maximum(jnp.maximum(jnp.max(s_b, axis=0, keepdims=True),
                                            jnp.max(s_m, axis=0, keepdims=True)), sink)
                p_b = jnp.exp(s_b - m)
                p_m = jnp.exp(s_m - m)
                l = jnp.sum(p_b, axis=0, keepdims=True) + jnp.sum(p_m, axis=0, keepdims=True) + jnp.exp(sink - m)
                p_c = jnp.where(sel, p_b, 0.0).astype(BF16)
                p_p = jnp.where(sel, 0.0, p_b).astype(BF16)
                o_t = (_dot(kvc_t[vs, :], p_c, NN) + _dot(kvp_t[vs, :], p_p, NN)
                       + _dot(kvm_ref[LEAD:BLOCK, vs], p_m.astype(BF16), TN))
                outs.append(o_t / l)
                lse = m + jnp.log(l)
                for j in range(GROUP):
                    lse_ref[GROUP * g + j:GROUP * g + j + 1, rows] = lse[:, BLOCK * j:BLOCK * (j + 1)]
            o_ref[rows, :] = _head_rows(outs).T
        ag_finish()

    res = _pcall(
        body, name="attn_fwd", grid=(nt,),
        in_specs=[pl.BlockSpec(memory_space=pltpu.SMEM),
                  _rows(TM, 512), _rows(TM, 256),
                  pl.BlockSpec((BLOCK, 256), lambda i: (jnp.maximum(i * TILE_BLOCKS - 1, 0), 0)),
                  _full((BLOCK, 256))] + [ANY] * ns,
        out_specs=[_rows(TM, 512), pl.BlockSpec((N_HEADS, TM), lambda i: (0, i))] + [ANY] * ns,
        out_shape=[jax.ShapeDtypeStruct((R, 512), F32), jax.ShapeDtypeStruct((N_HEADS, R), F32)]
        + _gathered_shapes(shards),
        scratch_shapes=_allgather_sems(ns) + [pltpu.VMEM((BLOCK, N_HEADS * BLOCK), F32),
                                              pltpu.VMEM((1, N_HEADS * BLOCK), F32)],
        semantics=("arbitrary",),
    )(sinks, q, kv, kv, kv, *shards)
    return res[0], res[1], res[2:]


def _ln_silu(y, lg, lb):
    mu = jnp.mean(y, axis=-1, keepdims=True)
    xc = y - mu
    rstd = lax.rsqrt(jnp.mean(xc * xc, axis=-1, keepdims=True) + EPS)
    xhat = xc * rstd
    yn = xhat * lg + lb
    return yn, xhat, rstd


PHASE_ROWS = HALO + TM - 8


def _phase_scratch():
    return pltpu.VMEM((7, PHASE_ROWS, CONV_W), F32)


def _phase_copies(src_s, ph_s):
    for b in range(1, 8):
        ph_s[b - 1] = src_s[pl.ds(b, PHASE_ROWS), :]


def _shifted(src_s, ph_s, start, rows):
    a8, b = (start // 8) * 8, start % 8
    if b == 0:
        return src_s[pl.ds(a8, rows), :]
    return ph_s[b - 1, pl.ds(a8, rows), :]


def _conv_fwd(ca, cg, conv_w, conv_b, ln_g, ln_b, shards):
    R = ca.shape[0]
    nt = R // TM
    hpt = TM // HALO
    ns = len(shards)

    def body(ca_ref, cg_ref, cah_ref, cgh_ref, w_ref, b_ref, lg_ref, lb_ref, *refs):
        ag_ins, (oc_ref, y_ref), ag_outs = refs[:ns], refs[ns:ns + 2], refs[ns + 2:2 * ns + 2]
        ag_sems, (u_s, uph_s) = refs[2 * ns + 2:2 * ns + 5], refs[2 * ns + 5:]
        i = pl.program_id(0)
        ag_finish = _carried_allgather(i, nt, shards, ag_ins + ag_outs + ag_sems)
        u_s[HALO:HALO + TM, :] = ca_ref[...] * _sigmoid(cg_ref[...])
        u_s[0:HALO, :] = jnp.where(i > 0, cah_ref[...] * _sigmoid(cgh_ref[...]), 0.0)
        _phase_copies(u_s, uph_s)
        for rc in range(TM // CONV_ROWS):
            base = rc * CONV_ROWS + HALO - (CONV_K - 1)
            acc = jnp.zeros((CONV_ROWS, CONV_W), F32) + b_ref[...]
            for k in range(CONV_K):
                acc = acc + _shifted(u_s, uph_s, base + k, CONV_ROWS) * w_ref[k:k + 1, :]
            rows = slice(rc * CONV_ROWS, (rc + 1) * CONV_ROWS)
            y_ref[rows, :] = acc
            yn, _, _ = _ln_silu(acc, lg_ref[...], lb_ref[...])
            oc_ref[rows, :] = yn * _sigmoid(yn)
        ag_finish()

    prev_halo = pl.BlockSpec((HALO, CONV_W), lambda i: (jnp.maximum(i * hpt - 1, 0), 0))
    anywhere = pl.BlockSpec(memory_space=pl.ANY)
    res = _pcall(
        body, name="conv_fwd", grid=(nt,),
        in_specs=[_rows(TM, CONV_W), _rows(TM, CONV_W), prev_halo, prev_halo,
                  _full((CONV_K, CONV_W)), _full((1, CONV_W)), _full((1, CONV_W)), _full((1, CONV_W))]
        + [anywhere] * ns,
        out_specs=[_rows(TM, CONV_W), _rows(TM, CONV_W)] + [anywhere] * ns,
        out_shape=[jax.ShapeDtypeStruct((R, CONV_W), F32), jax.ShapeDtypeStruct((R, CONV_W), F32)]
        + _gathered_shapes(shards),
        scratch_shapes=_allgather_sems(ns) + [pltpu.VMEM((HALO + TM, CONV_W), F32), _phase_scratch()],
        semantics=("arbitrary",),
    )(ca, cg, ca, cg, conv_w, conv_b, ln_g, ln_b, *shards)
    return res[:2], res[2:]


def _outproj_fwd(h0, o_attn, o_conv, ga, gc, w_out, shards):
    R = h0.shape[0]
    nt = R // TM
    ns = len(shards)

    def body(h_ref, oa_ref, oc_ref, ga_ref, gc_ref, w_ref, *refs):
        ag_ins, h1_ref, ag_outs, ag_sems = refs[:ns], refs[ns], refs[ns + 1:2 * ns + 1], refs[2 * ns + 1:]
        ag_finish = _carried_allgather(pl.program_id(0), nt, shards, ag_ins + ag_outs + ag_sems)
        oa, oc = oa_ref[...], oc_ref[...]
        ma = (oa * _rms_stats(oa) * ga_ref[...]).astype(BF16)
        mc = (oc * _rms_stats(oc) * gc_ref[...]).astype(BF16)
        h1_ref[...] = h_ref[...] + _dot(ma, w_ref[0:512, :], NN) + _dot(mc, w_ref[512:1024, :], NN)
        ag_finish()

    anywhere = pl.BlockSpec(memory_space=pl.ANY)
    res = _pcall(
        body, name="outproj_fwd", grid=(nt,),
        in_specs=[_rows(TM, D_MODEL), _rows(TM, 512), _rows(TM, 512), _full((1, 512)), _full((1, 512)),
                  _full((D_MODEL, D_MODEL))] + [anywhere] * ns,
        out_specs=[_rows(TM, D_MODEL)] + [anywhere] * ns,
        out_shape=[jax.ShapeDtypeStruct((R, D_MODEL), F32)] + _gathered_shapes(shards),
        scratch_shapes=_allgather_sems(ns),
        semantics=("arbitrary",),
    )(h0, o_attn, o_conv, ga, gc, w_out, *shards)
    return res[0], res[1:]


def _target_copy(tgt_hbm, tgt_s, sem, i, first):
    if first:
        return pltpu.make_async_copy(tgt_hbm.at[pl.ds(0, TM - BLOCK)], tgt_s.at[pl.ds(BLOCK, TM - BLOCK)], sem)
    return pltpu.make_async_copy(tgt_hbm.at[pl.ds(i * TM - BLOCK, TM)], tgt_s, sem)


def _resident(shape):
    nd = len(shape)
    return pl.BlockSpec(shape, lambda *_: (0,) * nd, pipeline_mode=pl.Buffered(1))


def _ffn_fwd(h1, g2, wg_t, wu_t, wd, gf, target):
    R = h1.shape[0]
    nt = R // TM

    def body(h1_ref, g2_ref, wg_ref, wu_ref, wd_ref, gf_ref, tgt_hbm,
             gate_ref, up_ref, act_s, dh2_ref, loss_ref, dgf_ref, tgt_s, sem):
        i = pl.program_id(0)

        @pl.when(i == 0)
        def _():
            loss_ref[...] = jnp.zeros_like(loss_ref)
            dgf_ref[...] = jnp.zeros_like(dgf_ref)
            tgt_s[0:BLOCK, :] = jnp.zeros((BLOCK, D_MODEL), F32)
            _target_copy(tgt_hbm, tgt_s, sem, i, True).start()

        pl.when(i > 0)(lambda: _target_copy(tgt_hbm, tgt_s, sem, i, False).start())
        h1 = h1_ref[...]
        hn = (h1 * _rms_stats(h1) * g2_ref[...]).astype(BF16)
        for cs in FF_SUB:
            gate = _dot(hn, wg_ref[cs, :], NT)
            up = _dot(hn, wu_ref[cs, :], NT)
            gate_ref[:, cs] = gate.astype(BF16)
            up_ref[:, cs] = up.astype(BF16)
            act_s[:, cs] = (gate * _sigmoid(gate) * up).astype(BF16)
        part = _dot(act_s[...], wd_ref[...], NN)
        pl.when(i == 0)(lambda: _target_copy(tgt_hbm, tgt_s, sem, i, True).wait())
        pl.when(i > 0)(lambda: _target_copy(tgt_hbm, tgt_s, sem, i, False).wait())
        h2 = h1 + part
        rf = _rms_stats(h2)
        gf = gf_ref[...]
        row = lax.broadcasted_iota(jnp.int32, (TM, 1), 0) + i * TM
        err = jnp.where(row >= BLOCK, h2 * rf * gf - tgt_s[...], 0.0)
        dy = err * (1.0 / D_MODEL)
        dh2, dgf = _rms_bwd(dy, h2, rf, gf)
        dh2_ref[...] = dh2
        loss_ref[...] += (0.5 / D_MODEL) * jnp.sum(err * err)
        dgf_ref[...] += dgf

    wspec = _resident((D_FF, D_MODEL))
    return _pcall(
        body, name="ffn_fwd", grid=(nt,),
        in_specs=[_rows(TM, D_MODEL), _full((1, D_MODEL)), wspec, wspec, wspec, _full((1, D_MODEL)),
                  pl.BlockSpec(memory_space=pl.ANY)],
        out_specs=[_rows(TM, D_FF), _rows(TM, D_FF), _rows(TM, D_FF), _rows(TM, D_MODEL), _full((8, 128)),
                   _full((1, D_MODEL))],
        out_shape=[jax.ShapeDtypeStruct((R, D_FF), BF16)] * 3
        + [jax.ShapeDtypeStruct((R, D_MODEL), F32),
           jax.ShapeDtypeStruct((8, 128), F32), jax.ShapeDtypeStruct((1, D_MODEL), F32)],
        scratch_shapes=[pltpu.VMEM((TM, D_MODEL), F32), pltpu.SemaphoreType.DMA],
        semantics=("arbitrary",),
    )(h1, g2, wg_t, wu_t, wd, gf, target)


def _ffn_bwd(dh2, h1, g2, gate, up, wg_t, wu_t, wd):
    R = h1.shape[0]
    nt = R // TM
    act_shape = jax.ShapeDtypeStruct((R, D_FF), BF16)

    def act_body(dh2_ref, gate_ref, up_ref, wd_ref, dgate_ref, dup_ref):
        dhb = dh2_ref[...].astype(BF16)
        for cs in FF_SUB:
            dact = _dot(dhb, wd_ref[cs, :], NT)
            gate = gate_ref[:, cs].astype(F32)
            up = up_ref[:, cs].astype(F32)
            sig = _sigmoid(gate)
            dgate_ref[:, cs] = (dact * up * (sig * (1.0 + gate * (1.0 - sig)))).astype(BF16)
            dup_ref[:, cs] = (dact * (gate * sig)).astype(BF16)

    dgate, dup = _pcall(
        act_body, name="ffn_bwd_act", grid=(nt,),
        in_specs=[_rows(TM, D_MODEL), _rows(TM, D_FF), _rows(TM, D_FF), _resident((D_FF, D_MODEL))],
        out_specs=[_rows(TM, D_FF), _rows(TM, D_FF)], out_shape=[act_shape, act_shape],
        semantics=("parallel",),
    )(dh2, gate, up, wd)

    def in_body(dh2_ref, h1_ref, g2_ref, dgate_ref, dup_ref, wg_ref, wu_ref, hn_ref, dh1_ref, dg2_ref):
        @pl.when(pl.program_id(0) == 0)
        def _():
            dg2_ref[...] = jnp.zeros_like(dg2_ref)

        dhn = _dot(dgate_ref[...], wg_ref[...], NN) + _dot(dup_ref[...], wu_ref[...], NN)
        h1 = h1_ref[...]
        r = _rms_stats(h1)
        g2 = g2_ref[...]
        hn_ref[...] = (h1 * r * g2).astype(BF16)
        dx, dg = _rms_bwd(dhn, h1, r, g2)
        dh1_ref[...] = dh2_ref[...] + dx
        dg2_ref[...] += dg

    hn2, dh1, dg2 = _pcall(
        in_body, name="ffn_bwd_in", grid=(nt,),
        in_specs=[_rows(TM, D_MODEL), _rows(TM, D_MODEL), _full((1, D_MODEL)), _rows(TM, D_FF), _rows(TM, D_FF),
                  _resident((D_FF, D_MODEL)), _resident((D_FF, D_MODEL))],
        out_specs=[_rows(TM, D_MODEL), _rows(TM, D_MODEL), _full((1, D_MODEL))],
        out_shape=[jax.ShapeDtypeStruct((R, D_MODEL), BF16), jax.ShapeDtypeStruct((R, D_MODEL), F32),
                   jax.ShapeDtypeStruct((1, D_MODEL), F32)],
        semantics=("arbitrary",),
    )(dh2, h1, g2, dgate, dup, wg_t, wu_t)
    return dgate, dup, hn2, dh1, dg2


def _wgrad(a, b, tm, name, reduce=None):
    K, M = a.shape
    N = b.shape[1]
    tk = K // WGRAD_K_TILES if K % (WGRAD_K_TILES * BLOCK) == 0 else TM
    nm, nk = M // tm, K // tk
    extra = [] if reduce is None else [reduce]

    def body(a_ref, b_ref, *refs):
        if reduce is None:
            (o_ref,) = refs
        else:
            g_ref, o_ref, sums_out, chips_out = refs[:4]
            step = pl.program_id(0) * nk + pl.program_id(1)
            finish = _carried_reduce(step, nm * nk, 1, g_ref, sums_out, chips_out, refs[4:])

        @pl.when(pl.program_id(1) == 0)
        def _():
            o_ref[...] = jnp.zeros_like(o_ref)

        o_ref[...] += _dot(a_ref[...], b_ref[...].astype(BF16), TN)
        if reduce is not None:
            finish()

    res = _pcall(
        body, name=name, grid=(nm, nk),
        in_specs=[pl.BlockSpec((tk, tm), lambda m, k: (k, m)), pl.BlockSpec((tk, N), lambda m, k: (k, 0))]
        + [ANY] * len(extra),
        out_specs=[pl.BlockSpec((tm, N), lambda m, k: (m, 0))] + [ANY] * (2 * len(extra)),
        out_shape=[jax.ShapeDtypeStruct((M, N), F32)] + (_reduce_outputs(reduce) if extra else []),
        scratch_shapes=_reduce_scratch(reduce) if extra else [],
        semantics=("arbitrary", "arbitrary"),
    )(a, b, *extra)
    return res[0] if reduce is None else (res[0], res[1], res[2])


def _outproj_bwd(dh1, o_attn, o_conv, ga, gc, w_out, grads):
    R = dh1.shape[0]
    nt, ng = R // TM, len(grads)

    def body(dh1_ref, oa_ref, oc_ref, ga_ref, gc_ref, w_ref, *refs):
        g_ins, (doa_ref, doc_ref, mixed_ref, dga_ref, dgc_ref) = refs[:ng], refs[ng:ng + 5]
        g_outs, (send_sems, recv_sems) = refs[ng + 5:2 * ng + 5], refs[2 * ng + 5:]
        exchange = functools.partial(_sibling_copies, g_ins, g_outs, send_sems, recv_sems)
        _hosted(pl.program_id(0), exchange)

        @pl.when(pl.program_id(0) == 0)
        def _():
            dga_ref[...] = jnp.zeros_like(dga_ref)
            dgc_ref[...] = jnp.zeros_like(dgc_ref)

        dm = _dot(dh1_ref[...].astype(BF16), w_ref[...], NT)
        oa, oc = oa_ref[...], oc_ref[...]
        ra, rc = _rms_stats(oa), _rms_stats(oc)
        mixed_ref[:, 0:512] = (oa * ra * ga_ref[...]).astype(BF16)
        mixed_ref[:, 512:1024] = (oc * rc * gc_ref[...]).astype(BF16)
        doa, dga = _rms_bwd(dm[:, 0:512], oa, ra, ga_ref[...])
        doc, dgc = _rms_bwd(dm[:, 512:1024], oc, rc, gc_ref[...])
        doa_ref[...] = doa
        doc_ref[...] = doc
        dga_ref[...] += dga
        dgc_ref[...] += dgc
        _hosted_wait(pl.program_id(0), nt, exchange)

    res = _pcall(
        body, name="outproj_bwd", grid=(nt,),
        in_specs=[_rows(TM, D_MODEL), _rows(TM, 512), _rows(TM, 512), _full((1, 512)), _full((1, 512)),
                  _full((D_MODEL, D_MODEL))] + [ANY] * ng,
        out_specs=[_rows(TM, 512), _rows(TM, 512), _rows(TM, D_MODEL), _full((1, 512)), _full((1, 512))]
        + [ANY] * ng,
        out_shape=[jax.ShapeDtypeStruct((R, 512), F32), jax.ShapeDtypeStruct((R, 512), F32),
                   jax.ShapeDtypeStruct((R, D_MODEL), BF16),
                   jax.ShapeDtypeStruct((1, 512), F32), jax.ShapeDtypeStruct((1, 512), F32)]
        + _sibling_shapes(grads),
        scratch_shapes=_sem_pair(ng),
        semantics=("arbitrary",),
    )(dh1, o_attn, o_conv, ga, gc, w_out, *grads)
    return res[:5], res[5:]


def _conv_bwd(do_conv, y, ca, cg, conv_w, ln_g, ln_b, partials):
    R = ca.shape[0]
    nt = R // TM
    hpt = TM // HALO
    npart = len(partials)

    def body(do_ref, doh_ref, y_ref, yh_ref, ca_ref, cg_ref, cah_ref, cgh_ref, w_ref, lg_ref, lb_ref, *refs):
        p_ins, (dca_ref, dcg_ref, dw_ref, db_ref, dlg_ref, dlb_ref) = refs[:npart], refs[npart:npart + 6]
        p_outs, (send_sems, recv_sems, u_s, dy_s, uph_s, dyph_s) = refs[npart + 6:2 * npart + 6], refs[2 * npart + 6:]
        i = pl.program_id(0)
        exchange = functools.partial(_chip_copies, p_ins, p_outs, send_sems, recv_sems)
        _hosted(i, exchange)

        @pl.when(i == 0)
        def _():
            dw_ref[...] = jnp.zeros_like(dw_ref)
            db_ref[...] = jnp.zeros_like(db_ref)
            dlg_ref[...] = jnp.zeros_like(dlg_ref)
            dlb_ref[...] = jnp.zeros_like(dlb_ref)

        lg, lb = lg_ref[...], lb_ref[...]

        def ln_bwd(yv, dov):
            yn, xhat, rstd = _ln_silu(yv, lg, lb)
            sig = _sigmoid(yn)
            dyn = dov * (sig * (1.0 + yn * (1.0 - sig)))
            dxh = dyn * lg
            dyv = rstd * (dxh - jnp.mean(dxh, axis=-1, keepdims=True)
                          - xhat * jnp.mean(dxh * xhat, axis=-1, keepdims=True))
            return dyv, dyn, xhat

        dyv, dyn, xhat = ln_bwd(y_ref[...], do_ref[...])
        dy_s[0:TM, :] = dyv
        dlg_ref[...] += jnp.sum(dyn * xhat, axis=0, keepdims=True)
        dlb_ref[...] += jnp.sum(dyn, axis=0, keepdims=True)
        db_ref[...] += jnp.sum(dyv, axis=0, keepdims=True)
        dyh, _, _ = ln_bwd(yh_ref[...], doh_ref[...])
        dy_s[TM:TM + HALO, :] = jnp.where(i < nt - 1, dyh, 0.0)
        u_s[HALO:HALO + TM, :] = ca_ref[...] * _sigmoid(cg_ref[...])
        u_s[0:HALO, :] = jnp.where(i > 0, cah_ref[...] * _sigmoid(cgh_ref[...]), 0.0)
        _phase_copies(dy_s, dyph_s)
        _phase_copies(u_s, uph_s)

        for rc in range(TM // CONV_ROWS):
            acc = jnp.zeros((CONV_ROWS, CONV_W), F32)
            for k in range(CONV_K):
                acc = acc + _shifted(dy_s, dyph_s, rc * CONV_ROWS + CONV_K - 1 - k, CONV_ROWS) * w_ref[k:k + 1, :]
            rows = slice(rc * CONV_ROWS, (rc + 1) * CONV_ROWS)
            sg = _sigmoid(cg_ref[rows, :])
            dca_ref[rows, :] = (acc * sg).astype(BF16)
            dcg_ref[rows, :] = (acc * ca_ref[rows, :] * sg * (1.0 - sg)).astype(BF16)

        for k in range(CONV_K):
            prod = _shifted(u_s, uph_s, HALO - (CONV_K - 1) + k, TM) * dy_s[0:TM, :]
            dw_ref[k:k + 1, :] += jnp.sum(prod, axis=0, keepdims=True)
        _hosted_wait(i, nt, exchange)

    prev_halo = pl.BlockSpec((HALO, CONV_W), lambda i: (jnp.maximum(i * hpt - 1, 0), 0))
    next_halo = pl.BlockSpec((HALO, CONV_W), lambda i: (jnp.minimum((i + 1) * hpt, nt * hpt - 1), 0))
    vec = jax.ShapeDtypeStruct((1, CONV_W), F32)
    res = _pcall(
        body, name="conv_bwd", grid=(nt,),
        in_specs=[_rows(TM, CONV_W), next_halo, _rows(TM, CONV_W), next_halo,
                  _rows(TM, CONV_W), _rows(TM, CONV_W), prev_halo, prev_halo,
                  _full((CONV_K, CONV_W)), _full((1, CONV_W)), _full((1, CONV_W))] + [ANY] * npart,
        out_specs=[_rows(TM, CONV_W), _rows(TM, CONV_W), _full((32, CONV_W)),
                   _full((1, CONV_W)), _full((1, CONV_W)), _full((1, CONV_W))] + [ANY] * npart,
        out_shape=[jax.ShapeDtypeStruct((R, CONV_W), BF16), jax.ShapeDtypeStruct((R, CONV_W), BF16),
                   jax.ShapeDtypeStruct((32, CONV_W), F32), vec, vec, vec] + _chip_shapes(partials),
        scratch_shapes=_sem_pair(3 * npart)
        + [pltpu.VMEM((HALO + TM, CONV_W), F32), pltpu.VMEM((TM + HALO, CONV_W), F32),
           _phase_scratch(), _phase_scratch()],
        semantics=("arbitrary",),
    )(do_conv, do_conv, y, y, ca, cg, ca, cg, conv_w, ln_g, ln_b, *partials)
    return res[:6], res[6:]


def _attn_bwd(q, kv, sinks, o, lse, do, reduce):
    R = q.shape[0]
    nt = R // TM

    def body(sink_ref, q_ref, kv_ref, kvh_ref, kvm_ref, o_ref, lse_ref, do_ref, g_ref,
             dq_ref, dkv_ref, dkvm_ref, dsink_ref, sums_out, chips_out, *scratch):
        carry_s, bias_s, sink_s = scratch[-3:]
        i = pl.program_id(0)
        reduce_finish = _carried_reduce(i, nt + 1, 2, g_ref, sums_out, chips_out, scratch[:-3])

        @pl.when(i == 0)
        def _():
            dkvm_ref[...] = jnp.zeros_like(dkvm_ref)
            carry_s[...] = jnp.zeros_like(carry_s)
            for h in range(N_HEADS):
                dsink_ref[0, h] = 0.0
            _attn_tables(sink_ref, bias_s, sink_s)

        @pl.when(i < nt)
        def _():
            head_of = lax.broadcasted_iota(jnp.int32, (N_HEADS, ATTN_W), 1) // HEAD_DIM
            ind = (head_of == lax.broadcasted_iota(jnp.int32, (N_HEADS, ATTN_W), 0)).astype(BF16)
            dkm = [jnp.zeros((N_META, 2 * KV_W), F32)]
            dsink = [0.0] * N_HEADS
            prev_cur = carry_s[...]
            for s in range(TILE_BLOCKS):
                rows = slice(BLOCK * s, BLOCK * (s + 1))
                kvp = kvh_ref[...] if s == 0 else kv_ref[BLOCK * (s - 1):BLOCK * s, :]
                sel, pen, pen_m = _attn_masks(i * TILE_BLOCKS + s)
                q_t = (q_ref[rows, :] * ATTN_SCALE).T
                do_t = do_ref[rows, :].astype(BF16).T
                kvc_t, kvp_t = kv_ref[rows, :].T, kvp.T
                prod = do_ref[rows, :] * o_ref[rows, :]
                hi = prod.astype(BF16)
                lo = (prod - hi.astype(F32)).astype(BF16)
                delta8 = _dot(ind, hi, NT) + _dot(ind, lo, NT)
                lse8 = lse_ref[:, rows]
                dqs, cur, prev, meta = [], [None] * 4, [None] * 4, [None] * 4
                for g in range(N_HEADS // GROUP):
                    ks = slice(HEAD_DIM * g, HEAD_DIM * (g + 1))
                    vs = slice(KV_W + HEAD_DIM * g, KV_W + HEAD_DIM * (g + 1))
                    lanes = slice(GB * g, GB * (g + 1))
                    qg, dog = _group_lanes(q_t, g), _group_lanes(do_t, g)
                    kc, kp, km = kv_ref[rows, ks], kvp[:, ks], kvm_ref[LEAD:BLOCK, ks]
                    vc, vp, vm = kv_ref[rows, vs], kvp[:, vs], kvm_ref[LEAD:BLOCK, vs]
                    s_b, s_m = _attn_scores(qg, kc, kp, km, sel, pen, pen_m, bias_s[:, lanes])
                    lse, delta = _head_lanes(lse8, g), _head_lanes(delta8, g)
                    p_b = jnp.exp(s_b - lse)
                    p_m = jnp.exp(s_m - lse)
                    dp_b = jnp.where(sel, _dot(vc, dog, NN), _dot(vp, dog, NN))
                    ds_b = p_b * (dp_b - delta)
                    ds_m = (p_m * (_dot(vm, dog, NN) - delta)).astype(BF16)
                    dsk = jnp.exp(sink_s[:, lanes] - lse) * delta
                    for j in range(GROUP):
                        dsink[GROUP * g + j] = dsink[GROUP * g + j] - jnp.sum(dsk[:, BLOCK * j:BLOCK * (j + 1)])
                    ds_c = jnp.where(sel, ds_b, 0.0).astype(BF16)
                    ds_p = jnp.where(sel, 0.0, ds_b).astype(BF16)
                    p_c = jnp.where(sel, p_b, 0.0).astype(BF16)
                    p_p = jnp.where(sel, 0.0, p_b).astype(BF16)
                    dqs.append((_dot(kvc_t[ks, :], ds_c, NN) + _dot(kvp_t[ks, :], ds_p, NN)
                                + _dot(km, ds_m, TN)) * ATTN_SCALE)
                    cur[g], cur[2 + g] = _dot(ds_c, qg, NT), _dot(p_c, dog, NT)
                    prev[g], prev[2 + g] = _dot(ds_p, qg, NT), _dot(p_p, dog, NT)
                    meta[g], meta[2 + g] = _dot(ds_m, qg, NT), _dot(p_m.astype(BF16), dog, NT)
                dq_ref[rows, :] = _head_rows(dqs).astype(BF16).T
                dkv_ref[rows, :] = (prev_cur + jnp.concatenate(prev, axis=1)).astype(BF16)
                prev_cur = jnp.concatenate(cur, axis=1)
                dkm.append(jnp.concatenate(meta, axis=1))
            carry_s[...] = prev_cur
            dkvm_ref[...] += functools.reduce(lambda a, b: a + b, dkm)
            for h in range(N_HEADS):
                dsink_ref[0, h] += dsink[h]

        @pl.when(i == nt)
        def _():
            dkv_ref[...] = jnp.zeros_like(dkv_ref)
            dkv_ref[0:BLOCK, :] = carry_s[...].astype(BF16)

        reduce_finish()

    tile = lambda cols: pl.BlockSpec((TM, cols), lambda i: (jnp.minimum(i, nt - 1), 0))
    res = _pcall(
        body, name="attn_bwd", grid=(nt + 1,),
        in_specs=[pl.BlockSpec(memory_space=pltpu.SMEM), tile(512), tile(256),
                  pl.BlockSpec((BLOCK, 256), lambda i: (jnp.clip(i * TILE_BLOCKS - 1, 0, R // BLOCK - 1), 0)),
                  _full((BLOCK, 256)), tile(512),
                  pl.BlockSpec((N_HEADS, TM), lambda i: (0, jnp.minimum(i, nt - 1))), tile(512), ANY],
        out_specs=[tile(512), _rows(TM, 256), _full((N_META, 256)), pl.BlockSpec(memory_space=pltpu.SMEM), ANY, ANY],
        out_shape=[jax.ShapeDtypeStruct((R, 512), BF16), jax.ShapeDtypeStruct((R + TM, 256), BF16),
                   jax.ShapeDtypeStruct((N_META, 256), F32), jax.ShapeDtypeStruct((1, N_HEADS), F32)]
        + _reduce_outputs(reduce),
        scratch_shapes=_reduce_scratch(reduce)
        + [pltpu.VMEM((BLOCK, 256), F32), pltpu.VMEM((BLOCK, N_HEADS * BLOCK), F32),
           pltpu.VMEM((1, N_HEADS * BLOCK), F32)],
        semantics=("arbitrary",),
    )(sinks, q, kv, kv, kv, o, lse, do, reduce)
    return res[:4], res[4], res[5]


def _inproj_bwd(dh1, h0, g1, dq, dkv, dkvm, dca, dcg, w_in_t):
    R = h0.shape[0]
    nt = R // TM
    assert nt >= 2

    def body(dh1_ref, h0_ref, g_ref, dq_ref, dkv_ref, dkvm_ref, dca_ref, dcg_ref, w_ref,
             gx_hbm, dmeta_ref, dproj_ref, hn_ref, dg_ref, dx_s, gx_sems):
        i = pl.program_id(0)

        def gx_copy(step, slot, first):
            if first:
                return pltpu.make_async_copy(dx_s.at[slot, pl.ds(BLOCK, TM - BLOCK)],
                                             gx_hbm.at[pl.ds(0, TM - BLOCK)], gx_sems.at[slot])
            return pltpu.make_async_copy(
                dx_s.at[slot], gx_hbm.at[pl.ds(pl.multiple_of(step * TM - BLOCK, BLOCK), TM)], gx_sems.at[slot])

        @pl.when(i == 0)
        def _():
            dg_ref[...] = jnp.zeros_like(dg_ref)

        dproj_ref[:, 0:512] = dq_ref[...]
        dproj_ref[:, 512:768] = dkv_ref[...]
        dproj_ref[:, 768:1280] = dca_ref[...]
        dproj_ref[:, 1280:1792] = dcg_ref[...]

        @pl.when(i == 0)
        def _():
            dproj_ref[LEAD:BLOCK, 512:768] = dkvm_ref[...].astype(BF16)

        dhn = _dot(dproj_ref[...], w_ref[...], NN)
        h = h0_ref[...]
        r = _rms_stats(h)
        g = g_ref[...]
        hn_ref[...] = (h * r * g).astype(BF16)
        dx, dg = _rms_bwd(dhn, h, r, g)
        dg_ref[...] += dg
        slot = i % 2
        pl.when(i == 2)(lambda: gx_copy(0, 0, True).wait())
        pl.when(i > 2)(lambda: gx_copy(i - 2, slot, False).wait())
        dx_s[slot] = dh1_ref[...] + dx

        @pl.when(i == 0)
        def _():
            dmeta_ref[...] = dx_s[0, LEAD:BLOCK, :]
            gx_copy(0, 0, True).start()

        pl.when(i > 0)(lambda: gx_copy(i, slot, False).start())

        @pl.when(i == nt - 1)
        def _():
            gx_copy(nt - 2, (nt - 2) % 2, nt == 2).wait()
            gx_copy(nt - 1, (nt - 1) % 2, False).wait()

    return _pcall(
        body, name="inproj_bwd", grid=(nt,),
        in_specs=[_rows(TM, D_MODEL), _rows(TM, D_MODEL), _full((1, D_MODEL)), _rows(TM, 512), _rows(TM, 256),
                  _full((N_META, 256)), _rows(TM, 512), _rows(TM, 512), _full((1792, D_MODEL))],
        out_specs=[ANY, _full((N_META, D_MODEL)), _rows(TM, 1792), _rows(TM, D_MODEL), _full((1, D_MODEL))],
        out_shape=[jax.ShapeDtypeStruct((R - BLOCK, D_MODEL), F32), jax.ShapeDtypeStruct((N_META, D_MODEL), F32),
                   jax.ShapeDtypeStruct((R, 1792), BF16),
                   jax.ShapeDtypeStruct((R, D_MODEL), BF16), jax.ShapeDtypeStruct((1, D_MODEL), F32)],
        scratch_shapes=[pltpu.VMEM((2, TM, D_MODEL), F32), pltpu.SemaphoreType.DMA((2,))],
        semantics=("arbitrary",),
    )(dh1, h0, g1, dq, dkv, dkvm, dca, dcg, w_in_t)


ANY = pl.BlockSpec(memory_space=pl.ANY)


def _position():
    return lax.axis_index("x"), lax.axis_index("y"), lax.axis_index("c")


def _device_number(p):
    return 4 * p[0] + 2 * p[1] + p[2]


def _two_level_allgather(ins, outs, block, send_sems, recv_sems, local_sems, sem_base=0):
    n = len(ins)
    x, y, c = _position()
    me, sibling = (x, y, c), (x, y, 1 - c)
    chips = [(1 - x, y), (x, 1 - y), (1 - x, 1 - y)]

    def copy(w, k, origin, to, src=None):
        return pltpu.make_async_remote_copy(
            src_ref=block(w, origin) if src is None else src, dst_ref=block(w, origin),
            send_sem=send_sems.at[sem_base + 7 * w + k], recv_sem=recv_sems.at[sem_base + 7 * w + k],
            device_id=to, device_id_type=MESH)

    def mine(w):
        return pltpu.make_async_copy(ins[w], block(w, me), local_sems.at[w])

    def own(w):
        return [copy(w, 0, me, sibling, src=ins[w])] + [
            copy(w, 1 + j, me, (*chip, c), src=ins[w]) for j, chip in enumerate(chips)]

    def passed(w):
        return [copy(w, 4 + j, (*chip, c), sibling) for j, chip in enumerate(chips)]

    def start():
        for w in range(n):
            mine(w).start()
        for w in range(n):
            for cp in own(w):
                cp.start()

    def forward(w):
        fw = passed(w)
        for j, chip in enumerate(chips):
            copy(w, 1 + j, (*chip, c), me).wait_recv()
            fw[j].start()

    def finish():
        for w in range(n):
            copy(w, 0, sibling, me).wait_recv()
            for j, chip in enumerate(chips):
                copy(w, 4 + j, (*chip, 1 - c), me).wait_recv()
        for w in range(n):
            for cp in own(w) + passed(w):
                cp.wait_send()
            mine(w).wait()

    return start, forward, finish


def _carried_allgather(step, n_steps, shards, refs):
    ns = len(shards)
    ins, outs, (send_sems, recv_sems, local_sems) = refs[:ns], refs[ns:2 * ns], refs[2 * ns:]
    start, forward, finish = _two_level_allgather(
        ins, outs, _row_block(outs, [s.shape[0] for s in shards]), send_sems, recv_sems, local_sems)
    pl.when(step == 0)(start)
    total = sum(s.shape[0] for s in shards)
    sent = 0
    for w, s in enumerate(shards):
        sent += s.shape[0]
        pl.when(step == (AG_FORWARD_AT * sent * (n_steps - 1)) // (100 * total))(functools.partial(forward, w))
    return lambda: pl.when(step == n_steps - 1)(finish)


def _gathered_shapes(shards):
    return [jax.ShapeDtypeStruct((N_DEV * s.shape[0], s.shape[1]), s.dtype) for s in shards]


def _allgather_sems(ns):
    return _sem_pair(7 * ns) + [pltpu.SemaphoreType.DMA((ns,))]


def _blocking_allgather(ins, outs, block, send_sems, recv_sems, local_sems, sem_base=0):
    start, forward, finish = _two_level_allgather(ins, outs, block, send_sems, recv_sems, local_sems, sem_base)
    start()
    for w in range(len(ins)):
        forward(w)
    finish()


def _row_block(outs, rows):
    def block(w, p):
        return outs[w].at[pl.ds(pl.multiple_of(_device_number(p) * rows[w], 16), rows[w])]
    return block


def _sibling_copies(ins, outs, send_sems, recv_sems):
    x, y, c = _position()
    return [pltpu.make_async_remote_copy(
        src_ref=ins[w].at[:, 1 - c], dst_ref=outs[w], send_sem=send_sems.at[w], recv_sem=recv_sems.at[w],
        device_id=(x, y, 1 - c), device_id_type=MESH) for w in range(len(ins))]


def _chip_copies(ins, outs, send_sems, recv_sems):
    x, y, c = _position()
    chips = [(1 - x, y), (x, 1 - y), (1 - x, 1 - y)]
    return [pltpu.make_async_remote_copy(
        src_ref=ins[w].at[2 * chip[0] + chip[1]], dst_ref=outs[w].at[k],
        send_sem=send_sems.at[3 * w + k], recv_sem=recv_sems.at[3 * w + k],
        device_id=(*chip, c), device_id_type=MESH) for w in range(len(ins)) for k, chip in enumerate(chips)]


def _hosted(step, make_copies):
    @pl.when(step == 0)
    def _():
        for cp in make_copies():
            cp.start()


def _hosted_wait(step, n_steps, make_copies):
    @pl.when(step == n_steps - 1)
    def _():
        for cp in make_copies():
            cp.wait()


def _carried_reduce(step, n_steps, add_step, g_ref, sums_out, chips_out, scratch):
    sib_send, sib_recv, chip_send, chip_recv, local_sems, sums_s, recv_s = scratch
    x, y, c = _position()
    mine = pltpu.make_async_copy(g_ref.at[:, c], sums_s, local_sems.at[0])
    sibling = pltpu.make_async_remote_copy(
        src_ref=g_ref.at[:, 1 - c], dst_ref=recv_s, send_sem=sib_send, recv_sem=sib_recv,
        device_id=(x, y, 1 - c), device_id_type=MESH)
    keep = pltpu.make_async_copy(sums_s, sums_out, local_sems.at[1])
    chips = functools.partial(_chip_copies, [sums_s], [chips_out], chip_send, chip_recv)

    @pl.when(step == 0)
    def _():
        mine.start()
        sibling.start()

    @pl.when(step == add_step)
    def _():
        mine.wait()
        sibling.wait()
        sums_s[...] += recv_s[...]
        keep.start()
        for cp in chips():
            cp.start()

    def finish():
        @pl.when(step == n_steps - 1)
        def _():
            for cp in chips():
                cp.wait()
            keep.wait()

    return finish


def _reduce_outputs(grad):
    _, _, r, cols = grad.shape
    return [jax.ShapeDtypeStruct((4, r, cols), F32), jax.ShapeDtypeStruct((3, r, cols), F32)]


def _reduce_scratch(grad):
    _, _, r, cols = grad.shape
    dma = pltpu.SemaphoreType.DMA
    return [dma, dma, dma((3,)), dma((3,)), dma((2,)), pltpu.VMEM((4, r, cols), F32), pltpu.VMEM((4, r, cols), F32)]


def _sem_pair(n):
    return [pltpu.SemaphoreType.DMA((n,)), pltpu.SemaphoreType.DMA((n,))]


def _allgather_params(shards, small):
    arrays = list(shards) + list(small)
    n, ns = len(arrays), len(shards)

    def body(*refs):
        ins, outs = refs[:n], refs[n:2 * n]
        send_sems, recv_sems, local_sems = refs[2 * n:]

        rows = _row_block(outs, [a.shape[0] for a in arrays])

        def block(w, p):
            return rows(w, p) if w < ns else outs[w].at[_device_number(p)]

        _blocking_allgather(ins, outs, block, send_sems, recv_sems, local_sems)

    out_shape = [jax.ShapeDtypeStruct((N_DEV * a.shape[0], a.shape[1]), a.dtype) for a in shards]
    out_shape += [jax.ShapeDtypeStruct((N_DEV,) + a.shape, a.dtype) for a in small]
    return _pcall(
        body, name="allgather_params", in_specs=[ANY] * n, out_specs=[ANY] * n, out_shape=out_shape,
        scratch_shapes=[pltpu.SemaphoreType.DMA((7 * n,)), pltpu.SemaphoreType.DMA((7 * n,)),
                        pltpu.SemaphoreType.DMA((n,))],
    )(*arrays)


def _sibling_shapes(grads):
    return [jax.ShapeDtypeStruct((4,) + g.shape[2:], F32) for g in grads]


def _chip_shapes(partials):
    return [jax.ShapeDtypeStruct((3,) + p.shape[1:], F32) for p in partials]


def _add_sibling(grad, received, core, name):
    _, _, r, cols = grad.shape

    def body(core_ref, g_ref, r_ref, o_ref):
        o_ref[...] = g_ref[...] + r_ref[...]

    return pl.pallas_call(
        body, name=name,
        grid_spec=pltpu.PrefetchScalarGridSpec(
            num_scalar_prefetch=1, grid=(4,),
            in_specs=[pl.BlockSpec((None, None, r, cols), lambda s, core_ref: (s, core_ref[0], 0, 0)),
                      pl.BlockSpec((None, r, cols), lambda s, core_ref: (s, 0, 0))],
            out_specs=pl.BlockSpec((None, r, cols), lambda s, core_ref: (s, 0, 0))),
        out_shape=jax.ShapeDtypeStruct((4, r, cols), F32),
        compiler_params=pltpu.CompilerParams(vmem_limit_bytes=VMEM_LIMIT),
    )(core, grad, received)


def _reduce_last(grad, small):
    _, _, r, cols = grad.shape
    ns = len(small)

    def body(g_ref, *refs):
        s_ins, (sums_out, chips_out), s_outs = refs[:ns], refs[ns:ns + 2], refs[ns + 2:2 * ns + 2]
        sib_send, sib_recv, chip_send, chip_recv, ag_send, ag_recv, local_sems, sums_s, recv_s = refs[2 * ns + 2:]
        x, y, c = _position()
        start, forward, finish = _two_level_allgather(
            s_ins, s_outs, lambda w, p: s_outs[w].at[_device_number(p)], ag_send, ag_recv, local_sems)
        mine = pltpu.make_async_copy(g_ref.at[:, c], sums_s, local_sems.at[ns])
        sibling = pltpu.make_async_remote_copy(
            src_ref=g_ref.at[:, 1 - c], dst_ref=recv_s, send_sem=sib_send, recv_sem=sib_recv,
            device_id=(x, y, 1 - c), device_id_type=MESH)
        start()
        mine.start()
        sibling.start()
        mine.wait()
        sibling.wait()
        sums_s[...] += recv_s[...]
        keep = pltpu.make_async_copy(sums_s, sums_out, local_sems.at[ns + 1])
        keep.start()
        copies = _chip_copies([sums_s], [chips_out], chip_send, chip_recv)
        for cp in copies:
            cp.start()
        for w in range(ns):
            forward(w)
        finish()
        for cp in copies:
            cp.wait()
        keep.wait()

    block = jax.ShapeDtypeStruct((4, r, cols), F32)
    dma = pltpu.SemaphoreType.DMA
    return _pcall(
        body, name="reduce_last", in_specs=[ANY] * (1 + ns), out_specs=[ANY] * (2 + ns),
        out_shape=[block, jax.ShapeDtypeStruct((3, r, cols), F32)]
        + [jax.ShapeDtypeStruct((N_DEV,) + a.shape, a.dtype) for a in small],
        scratch_shapes=[dma, dma, dma((3,)), dma((3,)), dma((7 * ns,)), dma((7 * ns,)), dma((ns + 2,)),
                        pltpu.VMEM((4, r, cols), F32), pltpu.VMEM((4, r, cols), F32)],
    )(grad, *small)


def _adam(g, w, m, v):
    m = ADAM_B1 * m + (1.0 - ADAM_B1) * g
    v = ADAM_B2 * v + (1.0 - ADAM_B2) * (g * g)
    m_hat = m / (1.0 - ADAM_B1 ** ADAM_STEP)
    v_hat = v / (1.0 - ADAM_B2 ** ADAM_STEP)
    delta = -ADAM_LR * (m_hat / (jnp.sqrt(v_hat) + ADAM_EPS) + ADAM_WD * w)
    return delta, m, v


def _adamw(partial, received, slot, w, m, v, name):
    _, r, cols = partial.shape

    def body(slot_ref, p_ref, r_ref, w_ref, m_ref, v_ref, g_ref, d_ref, nm_ref, nv_ref):
        g = p_ref[...] + r_ref[0] + r_ref[1] + r_ref[2]
        g_ref[...] = g
        d_ref[...], nm_ref[...], nv_ref[...] = _adam(g, w_ref[...], m_ref[...], v_ref[...])

    whole = pl.BlockSpec((r, cols), lambda i, slot_ref: (0, 0))
    out = jax.ShapeDtypeStruct((r, cols), F32)
    return pl.pallas_call(
        body, name=name,
        grid_spec=pltpu.PrefetchScalarGridSpec(
            num_scalar_prefetch=1, grid=(1,),
            in_specs=[pl.BlockSpec((None, r, cols), lambda i, slot_ref: (slot_ref[0], 0, 0)),
                      pl.BlockSpec((3, r, cols), lambda i, slot_ref: (0, 0, 0)), whole, whole, whole],
            out_specs=[whole, whole, whole, whole]),
        out_shape=[out, out, out, out],
        compiler_params=pltpu.CompilerParams(vmem_limit_bytes=VMEM_LIMIT),
    )(slot, partial, received, w, m, v)


def _adamw_small(dev, ga, gb, gc, params):
    names = ["meta", "attn_norm", "sinks", "conv_w", "conv_b", "ln_g", "ln_b", "attn_out", "conv_out",
             "ffn_norm", "final_norm"]
    flat = [a for p in params for a in p]
    n_in = len(flat)

    def body(dev_ref, ga_ref, gb_ref, gc_ref, *refs):
        ins, outs = refs[:n_in], refs[n_in:n_in + 4 * len(names)]
        loss_ref, sb, sc = refs[n_in + 4 * len(names):]
        a = ga_ref[0]
        sb[...] = gb_ref[0]
        sc[...] = gc_ref[0]
        for d in range(1, N_DEV):
            a = a + ga_ref[d]
            sb[...] += gb_ref[d]
            sc[...] += gc_ref[d]
        dev = dev_ref[0]
        grads = {
            "attn_norm": a[0:1, :], "ffn_norm": a[1:2, :], "final_norm": a[2:3, :],
            "conv_b": a[3:4, 0:512], "ln_g": a[3:4, 512:1024], "ln_b": a[4:5, 0:512],
            "attn_out": a[4:5, 512:1024], "conv_out": a[5:6, 0:512], "sinks": a[5:6, 512:512 + N_HEADS],
            "meta": sb[pl.ds(pl.multiple_of(dev * N_META, N_META), N_META), :],
            "conv_w": sc[pl.ds(pl.multiple_of(dev * 32, 32), 32), :][0:CONV_K, :],
        }
        for idx, nm in enumerate(names):
            w_ref, m_ref, v_ref = ins[3 * idx:3 * idx + 3]
            g = grads[nm]
            delta, m, v = _adam(g, w_ref[...], m_ref[...], v_ref[...])
            o = outs[4 * idx:4 * idx + 4]
            o[0][...], o[1][...], o[2][...], o[3][...] = g, delta, m, v
        loss_ref[...] = a[6:7, 0:1]

    vm = pl.BlockSpec(memory_space=pltpu.VMEM)
    out_shape = [jax.ShapeDtypeStruct(p[0].shape, F32) for p in params for _ in range(4)]
    out_shape.append(jax.ShapeDtypeStruct((1, 1), F32))
    res = pl.pallas_call(
        body, name="adamw_small",
        grid_spec=pltpu.PrefetchScalarGridSpec(
            num_scalar_prefetch=1, grid=(1,),
            in_specs=[pl.BlockSpec(ga.shape, lambda i, d: (0, 0, 0)), pl.BlockSpec(gb.shape, lambda i, d: (0, 0, 0)),
                      pl.BlockSpec(gc.shape, lambda i, d: (0, 0, 0))]
            + [pl.BlockSpec(a.shape, lambda i, d: (0, 0)) for a in flat],
            out_specs=[pl.BlockSpec(s.shape, lambda i, d: (0, 0)) for s in out_shape],
            scratch_shapes=[pltpu.VMEM(gb.shape[1:], F32), pltpu.VMEM(gc.shape[1:], F32)]),
        out_shape=out_shape,
        compiler_params=pltpu.CompilerParams(vmem_limit_bytes=VMEM_LIMIT),
    )(dev, ga, gb, gc, *flat)
    return [res[4 * i:4 * i + 4] for i in range(len(names))], res[-1]


def kernel(x, meta_tokens, attn_norm_g, w_in, attn_sinks, conv_w, conv_b, conv_ln_g, conv_ln_b, attn_out_g, conv_out_g, w_out, ffn_norm_g, w_gate, w_up, w_down, final_norm_g, loss_target, m_meta_tokens, m_attn_norm_g, m_w_in, m_attn_sinks, m_conv_w, m_conv_b, m_conv_ln_g, m_conv_ln_b, m_attn_out_g, m_conv_out_g, m_w_out, m_ffn_norm_g, m_w_gate, m_w_up, m_w_down, m_final_norm_g, v_meta_tokens, v_attn_norm_g, v_w_in, v_attn_sinks, v_conv_w, v_conv_b, v_conv_ln_g, v_conv_ln_b, v_attn_out_g, v_conv_out_g, v_w_out, v_ffn_norm_g, v_w_gate, v_w_up, v_w_down, v_final_norm_g):
    xi, yi, ci = _position()
    dev = jnp.reshape(_device_number((xi, yi, ci)), (1,)).astype(jnp.int32)
    core = jnp.reshape(ci, (1,)).astype(jnp.int32)
    slot = jnp.reshape(2 * xi + yi, (1,)).astype(jnp.int32)

    w_in_t, meta_st, convw_st = _allgather_params([w_in[0].T.astype(BF16)], [meta_tokens, conv_w[0]])
    meta_full = jnp.transpose(meta_st, (1, 0, 2)).reshape(N_META, D_MODEL)
    convw_full = jnp.transpose(convw_st, (1, 0, 2)).reshape(CONV_K, CONV_W)

    final_g = final_norm_g.reshape(1, D_MODEL)

    h0, q, kv, ca, cg = _inproj_fwd(x[0], meta_full, attn_norm_g, w_in_t)
    o_attn, lse, (wg_t,) = _attn_fwd(q, kv, attn_sinks, [w_gate[0].T.astype(BF16)])
    (o_conv, y_conv), (w_out_b, wd_b) = _conv_fwd(
        ca, cg, convw_full, conv_b, conv_ln_g, conv_ln_b, [w_out[0].astype(BF16), w_down[0].astype(BF16)])
    h1, (wu_t,) = _outproj_fwd(h0, o_attn, o_conv, attn_out_g, conv_out_g, w_out_b, [w_up[0].T.astype(BF16)])
    gate, up, act, dh2, loss_sum, dg_final = _ffn_fwd(h1, ffn_norm_g, wg_t, wu_t, wd_b, final_g, loss_target[0])

    def blocks(g):
        return g.reshape(4, 2, g.shape[0] // N_DEV, D_MODEL)

    dgate, dup, hn2, dh1, dg_ffn = _ffn_bwd(dh2, h1, ffn_norm_g, gate, up, wg_t, wu_t, wd_b)
    gate_up = [blocks(_wgrad(dgate, hn2, FF_CHUNK, "wgrad_gate")), blocks(_wgrad(dup, hn2, FF_CHUNK, "wgrad_up"))]
    down_grad = blocks(_wgrad(act, dh2, FF_CHUNK, "wgrad_down"))
    (do_attn, do_conv, mixed, dg_ao, dg_co), gate_up_sib = _outproj_bwd(
        dh1, o_attn, o_conv, attn_out_g, conv_out_g, w_out_b, gate_up)
    gate_up_sums = [_add_sibling(g, r, core, "add_sibling_" + t)
                    for g, r, t in zip(gate_up, gate_up_sib, ("gate", "up"))]
    out_grad = blocks(_wgrad(mixed, dh1, D_MODEL, "wgrad_out"))
    (dca, dcg, dconvw, dconvb, dln_g, dln_b), gate_up_chips = _conv_bwd(
        do_conv, y_conv, ca, cg, convw_full, conv_ln_g, conv_ln_b, gate_up_sums)
    (dq, dkv_shifted, dkvm, dsinks), down_sum, down_chips = _attn_bwd(
        q, kv, attn_sinks, o_attn, lse, do_attn, down_grad)
    dkv = dkv_shifted[BLOCK:BLOCK + dq.shape[0]]
    grad_x, dmeta, dproj, hn1, dg_attn = _inproj_bwd(dh1, h0, attn_norm_g, dq, dkv, dkvm, dca, dcg, w_in_t)
    dwi_t, out_sum, out_chips = _wgrad(dproj, hn1, 1792, "wgrad_in", out_grad)
    small_a = jnp.concatenate([
        dg_attn, dg_ffn, dg_final, jnp.concatenate([dconvb, dln_g], axis=1), jnp.concatenate([dln_b, dg_ao], axis=1),
        jnp.concatenate([dg_co, dsinks, jnp.zeros((1, 512 - N_HEADS), F32)], axis=1),
        jnp.concatenate([loss_sum[0:1, :], jnp.zeros((1, D_MODEL - 128), F32)], axis=1),
        jnp.zeros((1, D_MODEL), F32)], axis=0)
    small_b = jnp.transpose(dmeta.reshape(N_META, N_DEV, 128), (1, 0, 2)).reshape(N_DEV * N_META, 128)
    small_c = jnp.transpose(dconvw.reshape(32, N_DEV, 64), (1, 0, 2)).reshape(N_DEV * 32, 64)
    in_sum, in_chips, ga, gb, gc = _reduce_last(blocks(dwi_t), [small_a, small_b, small_c])
    tags = ("in", "out", "gate", "up", "down")
    chip_sums = [in_sum, out_sum] + gate_up_sums + [down_sum]
    from_chips = [in_chips, out_chips] + list(gate_up_chips) + [down_chips]

    big = [(True, w_in, m_w_in, v_w_in), (False, w_out, m_w_out, v_w_out), (True, w_gate, m_w_gate, v_w_gate),
           (True, w_up, m_w_up, v_w_up), (False, w_down, m_w_down, v_w_down)]
    big_out = {}
    for t, p, r, (transposed, w, m, v) in zip(tags, chip_sums, from_chips, big):
        rows = (lambda a: jnp.transpose(a[0])) if transposed else (lambda a: a[0])
        back = (lambda a: jnp.transpose(a)[None]) if transposed else (lambda a: a[None])
        big_out[t] = [back(a) for a in _adamw(p, r, slot, rows(w), rows(m), rows(v), "adamw_" + t)]

    small_params = [
        (meta_tokens, m_meta_tokens, v_meta_tokens), (attn_norm_g, m_attn_norm_g, v_attn_norm_g),
        (attn_sinks, m_attn_sinks, v_attn_sinks), (conv_w[0], m_conv_w[0], v_conv_w[0]),
        (conv_b, m_conv_b, v_conv_b), (conv_ln_g, m_conv_ln_g, v_conv_ln_g), (conv_ln_b, m_conv_ln_b, v_conv_ln_b),
        (attn_out_g, m_attn_out_g, v_attn_out_g), (conv_out_g, m_conv_out_g, v_conv_out_g),
        (ffn_norm_g, m_ffn_norm_g, v_ffn_norm_g),
        (final_g, m_final_norm_g.reshape(1, D_MODEL), v_final_norm_g.reshape(1, D_MODEL))]
    sm, loss = _adamw_small(dev, ga, gb, gc, small_params)
    sm[3] = [a[None] for a in sm[3]]
    sm[10] = [a.reshape(D_MODEL) for a in sm[10]]

    per_param = [sm[0], sm[1], big_out["in"], sm[2], sm[3], sm[4], sm[5], sm[6], sm[7], sm[8], big_out["out"],
                 sm[9], big_out["gate"], big_out["up"], big_out["down"], sm[10]]
    loss = loss.reshape(())
    outs = [loss, grad_x[None]]
    for kind in range(4):
        outs += [p[kind] for p in per_param]
    return tuple(outs)
```
